```python
import math
import jax
import jax.numpy as jnp
from jax import lax
import numpy as np

D_MODEL = 1024
BATCH = 16
SEQ = 256
DEPTH = 2
DEC_BATCH = 2
DEC_SEQ = 1024
PAST_LEN = 512

GRID_W = 64
HEAD_DIM = 64
N_MOD = 6
RMS_EPS = 1e-6
N_ATTN_LAYERS = (DEPTH + 1) // 2
N_REC_LAYERS = DEPTH // 2
A_HEADS = D_MODEL // (2 * HEAD_DIM)
A_KV_HEADS = A_HEADS // 4
WINDOW = 128
ROPE_BASE = 10000.0
QBLK = 128
B_HEADS = D_MODEL // (2 * HEAD_DIM)
NA_ROWS = 8
NA_COLS = 16
C_DIM = D_MODEL // 2
C_SHORT = 3
C_EMB = 33
C_FFN = 64
HYENA_MIN_DECAY = math.log(1e-2) / 1.5
HYENA_MAX_DECAY = math.log(1e-2) / 0.3
D_KDIM = 128
D_VDIM = 128
D_HEADS = (D_MODEL // 2) // D_VDIM
D_CHUNK = 32
N_EXPERTS = 64
TOP_K = 8
D_EXPERT = 256
D_SHARED = 256
ROUTE_SCALE = 2.5
MOE_BLK = 128

A_Q = A_HEADS * HEAD_DIM
A_KV = A_KV_HEADS * HEAD_DIM
B_W = B_HEADS * HEAD_DIM
ATTN_SPLITS = (A_Q, A_KV, A_KV, B_W, B_W, B_W)
ATTN_IN = A_Q + 2 * A_KV + 3 * B_W
D_KTOT = D_HEADS * D_KDIM
D_VTOT = D_HEADS * D_VDIM
REC_SPLITS = (3 * C_DIM, D_KTOT, D_KTOT, D_KTOT, D_VTOT, D_VTOT)
REC_IN = 3 * C_DIM + 3 * D_KTOT + 2 * D_VTOT

kernel_name = 'hybrid_diffusion_prefix_trunk'


def rmsnorm(x, g):
    xf = x.astype(jnp.float32)
    y = xf * lax.rsqrt(jnp.mean(xf * xf, axis=-1, keepdims=True) + RMS_EPS)
    return (y * g.astype(jnp.float32)).astype(x.dtype)


def modulate(h, shift, scale):
    return h * (1 + scale) + shift


def adaln(cvec, w, b):
    m = jax.nn.silu(cvec) @ w + b
    return jnp.split(m[:, None, :], N_MOD, axis=-1)


def split_cols(x, sizes):
    out, start = [], 0
    for s in sizes:
        out.append(x[..., start:start + s])
        start += s
    return out


def axial_rope(x):
    L, Dh = x.shape[1], x.shape[-1]
    half = Dh // 2
    t = jnp.arange(L)
    inv = ROPE_BASE ** (-jnp.arange(0, half, 2, dtype=jnp.float32) / half)

    def rot(xa, pos):
        ang = pos.astype(jnp.float32)[:, None] * inv[None, :]
        cos = jnp.cos(ang)[None, :, None, :].astype(x.dtype)
        sin = jnp.sin(ang)[None, :, None, :].astype(x.dtype)
        x1, x2 = xa[..., :half // 2], xa[..., half // 2:]
        return jnp.concatenate([x1 * cos - x2 * sin, x1 * sin + x2 * cos], axis=-1)

    return jnp.concatenate([rot(x[..., :half], t // GRID_W), rot(x[..., half:], t % GRID_W)], axis=-1)


def context_attention(q, k, v, sink):
    B, L, HK, G, Dh = q.shape
    nb = L // QBLK
    scale = Dh ** -0.5
    qb = jnp.moveaxis(q.reshape(B, nb, QBLK, HK, G, Dh), 1, 0)

    def block(qi):
        s = jnp.einsum('bqhgd,bkhd->bhgqk', qi, k).astype(jnp.float32) * scale
        if sink is not None:
            sk = jnp.broadcast_to(sink.astype(jnp.float32)[None, :, :, None, None], s.shape[:-1] + (1,))
            s = jnp.concatenate([s, sk], axis=-1)
        p = jax.nn.softmax(s, axis=-1)[..., :L]
        return jnp.einsum('bhgqk,bkhd->bqhgd', p.astype(v.dtype), v)

    o = lax.map(block, qb)
    return jnp.moveaxis(o, 0, 1).reshape(B, L, HK * G * Dh)


def window_attention(q_raw, q_rot, k_rot, v, k_ctx, v_ctx, sink):
    B, L, HQ, Dh = q_rot.shape
    HK = k_rot.shape[2]
    G = HQ // HK
    Lc = k_ctx.shape[1]
    nb = L // WINDOW
    scale = Dh ** -0.5

    def blocks(t):
        return t.reshape(B, nb, WINDOW, HK, G, Dh)

    def band(t):
        tp = jnp.pad(t, ((0, 0), (WINDOW, WINDOW), (0, 0), (0, 0))).reshape(B, nb + 2, WINDOW, HK, Dh)
        return jnp.concatenate([tp[:, :-2], tp[:, 1:-1], tp[:, 2:]], axis=2)

    kb, vb = band(k_rot), band(v)
    s_loc = jnp.einsum('bnqhgd,bnkhd->bnhgqk', blocks(q_rot), kb).astype(jnp.float32) * scale
    qi = jnp.arange(WINDOW)
    kj = jnp.arange(3 * WINDOW)
    rel = kj[None, :] - WINDOW - qi[:, None]
    kpos = (jnp.arange(nb)[:, None] - 1) * WINDOW + kj[None, :]
    valid = (jnp.abs(rel) <= WINDOW)[None] & ((kpos >= 0) & (kpos < L))[:, None, :]
    s_loc = jnp.where(valid[None, :, None, None], s_loc, -jnp.inf)
    s_ctx = jnp.einsum('bnqhgd,bchd->bnhgqc', blocks(q_raw), k_ctx).astype(jnp.float32) * scale
    s_sink = jnp.broadcast_to(sink.reshape(HK, G).astype(jnp.float32)[None, None, :, :, None, None], s_loc.shape[:-1] + (1,))
    p = jax.nn.softmax(jnp.concatenate([s_ctx, s_loc, s_sink], axis=-1), axis=-1)
    o = (jnp.einsum('bnhgqc,bchd->bnqhgd', p[..., :Lc].astype(v.dtype), v_ctx)
         + jnp.einsum('bnhgqk,bnkhd->bnqhgd', p[..., Lc:Lc + 3 * WINDOW].astype(v.dtype), vb))
    return o.reshape(B, L, HQ * Dh)


def neighborhood_attention(q, k, v, k_ctx, v_ctx, rpb):
    B, L, H, Dh = q.shape
    Lc = k_ctx.shape[1]
    rows = L // GRID_W
    wr = min(NA_ROWS, rows)
    scale = Dh ** -0.5
    r = jnp.arange(rows)
    rs = jnp.clip(r - NA_ROWS // 2, 0, rows - wr)
    band_rows = rs[:, None] + jnp.arange(wr)[None, :]

    def gather(t):
        return t.reshape(B, rows, GRID_W, H, Dh)[:, band_rows].reshape(B, rows, wr * GRID_W, H, Dh)

    kg, vg = gather(k), gather(v)
    qg = q.reshape(B, rows, GRID_W, H, Dh)
    s_loc = jnp.einsum('brqhd,brkhd->bhrqk', qg, kg).astype(jnp.float32) * scale
    cq = jnp.arange(GRID_W)
    cs = jnp.clip(cq - NA_COLS // 2, 0, GRID_W - NA_COLS)
    col_ok = (cq[None, :] >= cs[:, None]) & (cq[None, :] < cs[:, None] + NA_COLS)
    mask = jnp.broadcast_to(col_ok[:, None, :], (GRID_W, wr, GRID_W)).reshape(GRID_W, wr * GRID_W)
    idx_r = band_rows - r[:, None] + NA_ROWS - 1
    idx_c = jnp.clip(cq[None, :] - cq[:, None], -(NA_COLS - 1), NA_COLS - 1) + NA_COLS - 1
    bias = rpb[:, idx_r[:, None, :, None], idx_c[None, :, None, :]]
    bias = bias.reshape(H, rows, GRID_W, wr * GRID_W).astype(jnp.float32)
    s_loc = jnp.where(mask, s_loc + bias, -jnp.inf)
    s_ctx = jnp.einsum('brqhd,bchd->bhrqc', qg, k_ctx).astype(jnp.float32) * scale
    p = jax.nn.softmax(jnp.concatenate([s_ctx, s_loc], axis=-1), axis=-1)
    o = (jnp.einsum('bhrqc,bchd->brqhd', p[..., :Lc].astype(v.dtype), v_ctx)
         + jnp.einsum('bhrqk,brkhd->brqhd', p[..., Lc:].astype(v.dtype), vg))
    return o.reshape(B, L, H * Dh)


def attn_mixer_context(h, w_in, w_out, sink):
    B, L, _ = h.shape
    G = A_HEADS // A_KV_HEADS
    q_a, k_a, v_a, q_b, k_b, v_b = split_cols(h @ w_in, ATTN_SPLITS)
    k_a = k_a.reshape(B, L, A_KV_HEADS, HEAD_DIM)
    v_a = v_a.reshape(B, L, A_KV_HEADS, HEAD_DIM)
    k_b = k_b.reshape(B, L, B_HEADS, HEAD_DIM)
    v_b = v_b.reshape(B, L, B_HEADS, HEAD_DIM)
    o_a = context_attention(q_a.reshape(B, L, A_KV_HEADS, G, HEAD_DIM), k_a, v_a, sink.reshape(A_KV_HEADS, G))
    o_b = context_attention(q_b.reshape(B, L, B_HEADS, 1, HEAD_DIM), k_b, v_b, None)
    out = jnp.concatenate([o_a, o_b], axis=-1) @ w_out
    return out, k_a, v_a, k_b, v_b


def attn_mixer_latent(h, ck_a, cv_a, ck_b, cv_b, w_in, w_out, sink, rpb):
    B, L, _ = h.shape
    q_a, k_a, v_a, q_b, k_b, v_b = split_cols(h @ w_in, ATTN_SPLITS)
    q_a = q_a.reshape(B, L, A_HEADS, HEAD_DIM)
    k_a = k_a.reshape(B, L, A_KV_HEADS, HEAD_DIM)
    v_a = v_a.reshape(B, L, A_KV_HEADS, HEAD_DIM)
    o_a = window_attention(q_a, axial_rope(q_a), axial_rope(k_a), v_a, ck_a, cv_a, sink)
    hb = lambda t: t.reshape(B, L, B_HEADS, HEAD_DIM)
    o_b = neighborhood_attention(hb(q_b), hb(k_b), hb(v_b), ck_b, cv_b, rpb)
    return jnp.concatenate([o_a, o_b], axis=-1) @ w_out


def hyena_filter(L, w1, b1, w2, b2, w3, b3, freq, w4):
    f32 = jnp.float32
    t = jnp.linspace(0.0, 1.0, L, dtype=f32)[:, None]
    bands = (C_EMB - 1) // 2
    ang = (2.0 * math.pi / L) * jnp.arange(L, dtype=f32)[:, None] * jnp.linspace(1e-4, bands - 1, bands, dtype=f32)[None, :]
    z = jnp.concatenate([t, jnp.cos(ang), -jnp.sin(ang)], axis=-1)
    fr = freq.astype(f32)
    act = lambda y: jnp.sin(fr * y)
    hh = act(z @ w1.astype(f32) + b1.astype(f32))
    hh = act(hh @ w2.astype(f32) + b2.astype(f32))
    hh = act(hh @ w3.astype(f32) + b3.astype(f32))
    hh = hh @ w4.astype(f32)
    deltas = jnp.abs(jnp.linspace(HYENA_MIN_DECAY, HYENA_MAX_DECAY, C_DIM, dtype=f32))
    window = jnp.exp(-t * deltas[None, :])
    return hh.reshape(L, 2, C_DIM) * window[:, None, :]


def hyena(u, conv_w, conv_b, filt, d_skip):
    B, L, C3 = u.shape
    u = lax.conv_general_dilated(u, conv_w[:, None, :].astype(u.dtype), (1,), [((C_SHORT - 1) // 2, (C_SHORT - 1) // 2)],
                                 dimension_numbers=('NWC', 'WIO', 'NWC'), feature_group_count=C3) + conv_b.astype(u.dtype)
    x0, x1, v = u[..., :C_DIM], u[..., C_DIM:2 * C_DIM], u[..., 2 * C_DIM:]
    z = (x1 * v).astype(jnp.float32)
    h = hyena_filter(L, *filt)
    n = 2 * L
    zf = jnp.fft.rfft(z, n=n, axis=1)
    hf = jnp.fft.rfft(h[:, 0], n=n, axis=0) + jnp.conj(jnp.fft.rfft(h[:, 1], n=n, axis=0))
    y = jnp.fft.irfft(zf * hf[None], n=n, axis=1)[:, :L]
    y = y + z * d_skip.astype(jnp.float32)
    return (x0.astype(jnp.float32) * y).astype(u.dtype)


def gla_chunked(q, k, v, log_f, s0):
    B, L, H, DK = q.shape
    DV = v.shape[-1]
    nc = L // D_CHUNK

    def chunks(t):
        return t.reshape(B, nc, D_CHUNK, H, t.shape[-1]).transpose(1, 0, 3, 2, 4)

    qc, kc, vc, gc = chunks(q), chunks(k), chunks(v), chunks(log_f)
    b = jnp.cumsum(gc, axis=3)
    causal = jnp.tril(jnp.ones((D_CHUNK, D_CHUNK), dtype=bool))
    rel = jnp.where(causal[:, :, None], b[..., :, None, :] - b[..., None, :, :], -jnp.inf)
    scores = jnp.einsum('nbhtk,nbhsk,nbhtsk->nbhts', qc, kc, jnp.exp(rel))
    o_intra = jnp.einsum('nbhts,nbhsv->nbhtv', scores, vc)
    q_in = qc * jnp.exp(b)
    k_out = kc * jnp.exp(b[..., -1:, :] - b)
    decay = jnp.exp(b[..., -1, :])

    def step(S, inp):
        qi, ki, vi, di = inp
        o = jnp.einsum('bhtk,bhkv->bhtv', qi, S)
        S = di[..., None] * S + jnp.einsum('bhtk,bhtv->bhkv', ki, vi)
        return S, o

    s_fin, o_inter = lax.scan(step, s0, (q_in, k_out, vc, decay))
    o = (o_intra + o_inter).transpose(1, 0, 3, 2, 4).reshape(B, L, H, DV)
    return o, s_fin


def lower_bound(gamma, layer):
    p = jax.nn.softmax(gamma.astype(jnp.float32), axis=0)
    return (jnp.cumsum(p, axis=0) - p[0])[layer]


def hgrn2(q, f_fwd, f_bwd, i, g, s0_f, s0_b, lb_f, lb_b, norm_g):
    B, L, _ = q.shape
    f32 = jnp.float32
    heads = lambda t, d: t.reshape(B, L, D_HEADS, d)
    qh = heads(jax.nn.silu(q.astype(f32)), D_KDIM)
    vh = heads(i.astype(f32), D_VDIM)

    def gates(fx, lb):
        f = lb + (1.0 - lb) * jax.nn.sigmoid(fx.astype(f32))
        return heads(1.0 - f, D_KDIM), heads(jnp.log(f), D_KDIM)

    kf, lff = gates(f_fwd, lb_f)
    kb, lfb = gates(f_bwd, lb_b)
    rev = lambda t: jnp.flip(t, axis=1)
    o_f, s_f = gla_chunked(qh, kf, vh, lff, s0_f.astype(f32))
    o_b, s_b = gla_chunked(rev(qh), rev(kb), rev(vh), rev(lfb), s0_b.astype(f32))
    o = rmsnorm(o_f + rev(o_b), norm_g).reshape(B, L, D_VTOT)
    o = o * jax.nn.silu(g.astype(f32))
    return o.astype(q.dtype), s_f.astype(q.dtype), s_b.astype(q.dtype)


def rec_mixer(h, s0_f, s0_b, lb_f, lb_b, w_in, w_out, conv_w, conv_b, filt, d_skip, norm_g):
    u_c, q_d, f_f, f_b, i_d, g_d = split_cols(h @ w_in, REC_SPLITS)
    y_c = hyena(u_c, conv_w, conv_b, filt, d_skip)
    o_d, s_f, s_b = hgrn2(q_d, f_f, f_b, i_d, g_d, s0_f, s0_b, lb_f, lb_b, norm_g)
    return jnp.concatenate([y_c, o_d], axis=-1) @ w_out, s_f, s_b


def moe(x, w_router, router_bias, w_gate, w_up, w_down, ws_gate, ws_up, ws_down):
    shp = x.shape
    xt = x.reshape(-1, shp[-1])
    T, D = xt.shape
    scores = jax.nn.sigmoid((xt @ w_router).astype(jnp.float32))
    _, idx = lax.top_k(scores + router_bias.astype(jnp.float32), TOP_K)
    gate = jnp.take_along_axis(scores, idx, axis=-1)
    gate = gate / jnp.sum(gate, axis=-1, keepdims=True) * ROUTE_SCALE
    A = T * TOP_K
    e_flat = idx.reshape(-1)
    tok = jnp.arange(A) // TOP_K
    order = jnp.argsort(e_flat)
    e_sorted, tok_sorted, g_sorted = e_flat[order], tok[order], gate.reshape(-1)[order]
    counts = jnp.bincount(e_flat, length=N_EXPERTS)
    padded = ((counts + MOE_BLK - 1) // MOE_BLK) * MOE_BLK
    pad_end = jnp.cumsum(padded)
    pad_start = pad_end - padded
    grp_start = jnp.cumsum(counts) - counts
    dest = pad_start[e_sorted] + jnp.arange(A) - grp_start[e_sorted]
    n_rows = ((A + N_EXPERTS * (MOE_BLK - 1) + MOE_BLK - 1) // MOE_BLK) * MOE_BLK
    nb = n_rows // MOE_BLK
    blk_expert = jnp.minimum(jnp.searchsorted(pad_end, jnp.arange(nb) * MOE_BLK, side='right'), N_EXPERTS - 1)
    buf = jnp.zeros((n_rows, D), xt.dtype).at[dest].set(xt[tok_sorted])

    def expert_block(args):
        xb, e = args
        hb = jax.nn.silu(xb @ w_gate[e]) * (xb @ w_up[e])
        return hb @ w_down[e]

    yb = lax.map(expert_block, (buf.reshape(nb, MOE_BLK, D), blk_expert)).reshape(n_rows, D)
    routed = jnp.zeros((T, D), jnp.float32).at[tok_sorted].add(g_sorted[:, None] * yb[dest].astype(jnp.float32))
    shared = (jax.nn.silu(xt @ ws_gate) * (xt @ ws_up)) @ ws_down
    return (routed + shared.astype(jnp.float32)).astype(x.dtype).reshape(shp)


def setup_inputs(seed: int = 0) -> dict:
    key = jax.random.key(seed)
    keys = iter(jax.random.split(key, 48))
    f32 = jnp.float32

    def nrm(shape, std):
        return jax.random.normal(next(keys), shape, f32) * std

    def gain(shape):
        return 1.0 + nrm(shape, 0.05)

    D = D_MODEL
    return {
        'x_prompt': nrm((BATCH, SEQ, D), 1.0),
        'x_sample': nrm((DEC_BATCH, DEC_SEQ, D), 1.0),
        'cache_a_k': nrm((DEC_BATCH, N_ATTN_LAYERS, PAST_LEN, A_KV_HEADS, HEAD_DIM), 1.0),
        'cache_a_v': nrm((DEC_BATCH, N_ATTN_LAYERS, PAST_LEN, A_KV_HEADS, HEAD_DIM), 1.0),
        'cache_b_k': nrm((DEC_BATCH, N_ATTN_LAYERS, PAST_LEN, B_HEADS, HEAD_DIM), 1.0),
        'cache_b_v': nrm((DEC_BATCH, N_ATTN_LAYERS, PAST_LEN, B_HEADS, HEAD_DIM), 1.0),
        'state_d_fwd': nrm((DEC_BATCH, N_REC_LAYERS, D_HEADS, D_KDIM, D_VDIM), 0.5),
        'state_d_bwd': nrm((DEC_BATCH, N_REC_LAYERS, D_HEADS, D_KDIM, D_VDIM), 0.5),
        'c': nrm((DEC_BATCH, D), 1.0),
        'c_ctx': nrm((D,), 1.0),
        'w_ada': nrm((DEPTH, D, N_MOD * D), 0.5 * D ** -0.5),
        'b_ada': nrm((DEPTH, N_MOD * D), 0.02),
        'norm_mix': gain((DEPTH, D)),
        'norm_ffn': gain((DEPTH, D)),
        'w_in_attn': nrm((N_ATTN_LAYERS, D, ATTN_IN), D ** -0.5),
        'w_out_attn': nrm((N_ATTN_LAYERS, A_Q + B_W, D), (A_Q + B_W) ** -0.5),
        'sink_a': nrm((N_ATTN_LAYERS, A_HEADS), 1.0),
        'rpb_b': nrm((N_ATTN_LAYERS, B_HEADS, 2 * NA_ROWS - 1, 2 * NA_COLS - 1), 0.1),
        'w_in_rec': nrm((N_REC_LAYERS, D, REC_IN), D ** -0.5),
        'w_out_rec': nrm((N_REC_LAYERS, C_DIM + D_VTOT, D), (C_DIM + D_VTOT) ** -0.5),
        'conv_w': nrm((N_REC_LAYERS, C_SHORT, 3 * C_DIM), C_SHORT ** -0.5),
        'conv_b': nrm((N_REC_LAYERS, 3 * C_DIM), 0.02),
        'filt_w1': nrm((N_REC_LAYERS, C_EMB, C_FFN), C_EMB ** -0.5),
        'filt_b1': nrm((N_REC_LAYERS, C_FFN), 0.1),
        'filt_w2': nrm((N_REC_LAYERS, C_FFN, C_FFN), C_FFN ** -0.5),
        'filt_b2': nrm((N_REC_LAYERS, C_FFN), 0.1),
        'filt_w3': nrm((N_REC_LAYERS, C_FFN, C_FFN), C_FFN ** -0.5),
        'filt_b3': nrm((N_REC_LAYERS, C_FFN), 0.1),
        'filt_freq': gain((N_REC_LAYERS, C_FFN)),
        'filt_w4': nrm((N_REC_LAYERS, C_FFN, 2 * C_DIM), 0.1 * C_FFN ** -0.5),
        'd_skip': nrm((N_REC_LAYERS, C_DIM), 1.0),
        'lb_fwd': nrm((DEPTH, D_KTOT), 0.1),
        'lb_bwd': nrm((DEPTH, D_KTOT), 0.1),
        'norm_d': gain((N_REC_LAYERS, D_VDIM)),
        'w_router': nrm((DEPTH, D, N_EXPERTS), D ** -0.5),
        'router_bias': nrm((DEPTH, N_EXPERTS), 0.01),
        'w_gate': nrm((DEPTH, N_EXPERTS, D, D_EXPERT), D ** -0.5),
        'w_up': nrm((DEPTH, N_EXPERTS, D, D_EXPERT), D ** -0.5),
        'w_down': nrm((DEPTH, N_EXPERTS, D_EXPERT, D), D_EXPERT ** -0.5),
        'ws_gate': nrm((DEPTH, D, D_SHARED), D ** -0.5),
        'ws_up': nrm((DEPTH, D, D_SHARED), D ** -0.5),
        'ws_down': nrm((DEPTH, D_SHARED, D), D_SHARED ** -0.5),
        'final_norm': gain((D,)),
    }


def reference(x_prompt, x_sample, cache_a_k, cache_a_v, cache_b_k, cache_b_v, state_d_fwd, state_d_bwd, c, c_ctx,
              w_ada, b_ada, norm_mix, norm_ffn, w_in_attn, w_out_attn, sink_a, rpb_b, w_in_rec, w_out_rec,
              conv_w, conv_b, filt_w1, filt_b1, filt_w2, filt_b2, filt_w3, filt_b3, filt_freq, filt_w4, d_skip,
              lb_fwd, lb_bwd, norm_d, w_router, router_bias, w_gate, w_up, w_down, ws_gate, ws_up, ws_down,
              final_norm):
    def filt(j):
        return (filt_w1[j], filt_b1[j], filt_w2[j], filt_b2[j], filt_w3[j], filt_b3[j], filt_freq[j], filt_w4[j])

    def ffn(l, t):
        return moe(t, w_router[l], router_bias[l], w_gate[l], w_up[l], w_down[l], ws_gate[l], ws_up[l], ws_down[l])

    def rec(l, h, s0_f, s0_b):
        j = l // 2
        return rec_mixer(h, s0_f, s0_b, lower_bound(lb_fwd, l), lower_bound(lb_bwd, l), w_in_rec[j], w_out_rec[j],
                         conv_w[j], conv_b[j], filt(j), d_skip[j], norm_d[j])

    xp = x_prompt
    ak, av, bk, bv, sf, sb = [], [], [], [], [], []
    for l in range(DEPTH):
        j = l // 2
        sh1, sc1, g1, sh2, sc2, g2 = adaln(c_ctx[None, :], w_ada[l], b_ada[l])
        h = modulate(rmsnorm(xp, norm_mix[l]), sh1, sc1)
        if l % 2 == 0:
            out, k_a, v_a, k_b, v_b = attn_mixer_context(h, w_in_attn[j], w_out_attn[j], sink_a[j])
            ak.append(k_a)
            av.append(v_a)
            bk.append(k_b)
            bv.append(v_b)
        else:
            s0 = jnp.zeros((xp.shape[0], D_HEADS, D_KDIM, D_VDIM), jnp.float32)
            out, s_f, s_b = rec(l, h, s0, s0)
            sf.append(s_f)
            sb.append(s_b)
        xp = xp + g1 * out
        xp = xp + g2 * ffn(l, modulate(rmsnorm(xp, norm_ffn[l]), sh2, sc2))
    y_prompt = rmsnorm(xp, final_norm)

    xs = x_sample
    for l in range(DEPTH):
        j = l // 2
        sh1, sc1, g1, sh2, sc2, g2 = adaln(c, w_ada[l], b_ada[l])
        h = modulate(rmsnorm(xs, norm_mix[l]), sh1, sc1)
        if l % 2 == 0:
            out = attn_mixer_latent(h, cache_a_k[:, j], cache_a_v[:, j], cache_b_k[:, j], cache_b_v[:, j],
                                    w_in_attn[j], w_out_attn[j], sink_a[j], rpb_b[j])
        else:
            out, _, _ = rec(l, h, state_d_fwd[:, j], state_d_bwd[:, j])
        xs = xs + g1 * out
        xs = xs + g2 * ffn(l, modulate(rmsnorm(xs, norm_ffn[l]), sh2, sc2))
    y_sample = rmsnorm(xs, final_norm)

    new_a_k = jnp.stack(ak, axis=1)
    new_a_v = jnp.stack(av, axis=1)
    new_b_k = jnp.stack(bk, axis=1)
    new_b_v = jnp.stack(bv, axis=1)
    new_d_fwd = jnp.stack(sf, axis=1)
    new_d_bwd = jnp.stack(sb, axis=1)
    return (y_prompt, y_sample, new_a_k, new_a_v, new_b_k, new_b_v, new_d_fwd, new_d_bwd)
```

```python
import functools
import math

import numpy as np
import jax
import jax.numpy as jnp
from jax import lax
from jax.experimental import pallas as pl
from jax.experimental.pallas import tpu as pltpu

F32 = jnp.float32
BF16 = jnp.bfloat16
HI = lax.Precision.HIGHEST

D_MODEL = 1024
BATCH = 16
SEQ = 256
DEPTH = 2
DEC_BATCH = 2
DEC_SEQ = 1024
PAST_LEN = 512
GRID_W = 64
HEAD_DIM = 64
N_MOD = 6
RMS_EPS = 1e-6
A_HEADS = 8
A_KV_HEADS = 2
A_GROUP = A_HEADS // A_KV_HEADS
WINDOW = 128
ROPE_BASE = 10000.0
B_HEADS = 8
NA_ROWS = 8
NA_COLS = 16
C_DIM = 512
C_EMB = 33
C_FFN = 64
HYENA_MIN_DECAY = math.log(1e-2) / 1.5
HYENA_MAX_DECAY = math.log(1e-2) / 0.3
D_KDIM = 128
D_VDIM = 128
D_HEADS = 4
N_EXPERTS = 64
TOP_K = 8
D_EXPERT = 256
ROUTE_SCALE = 2.5
A_Q = A_HEADS * HEAD_DIM
A_KV = A_KV_HEADS * HEAD_DIM
B_W = B_HEADS * HEAD_DIM
ATTN_IN = A_Q + 2 * A_KV + 3 * B_W
REC_IN = 3 * C_DIM + 5 * D_HEADS * D_KDIM

T_CTX = BATCH * SEQ
T_LAT = DEC_BATCH * DEC_SEQ
T_ALL = T_CTX + T_LAT
N_CVEC = 1 + DEC_BATCH
CVEC_PAD = 8
TM = 256
MASK_NEG = -1e30
GLA_CHUNK = 64
DFT_CHUNK = 256
MOE_TM = 1024
VMEM_LIMIT = 56 * 1024 * 1024


def _cparams(*sem):
    return pltpu.CompilerParams(dimension_semantics=sem, vmem_limit_bytes=VMEM_LIMIT)


def _mod_row(i):
    return jnp.where(i < T_CTX // TM, 0, 1 + (i - T_CTX // TM) // (DEC_SEQ // TM))


def _dot(a, b):
    return jnp.dot(a.astype(BF16), b.astype(BF16), preferred_element_type=F32)


def _dot_nt(a, b):
    return lax.dot_general(a.astype(BF16), b.astype(BF16), (((1,), (1,)), ((), ())),
                           preferred_element_type=F32)


def _dot_tn(a, b):
    return lax.dot_general(a.astype(BF16), b.astype(BF16), (((0,), (0,)), ((), ())),
                           preferred_element_type=F32)


def _dot_hi(a, b):
    return jnp.dot(a, b, precision=HI, preferred_element_type=F32)


def _silu(x):
    return x * jax.nn.sigmoid(x)


def _rms(x, g):
    return x * lax.rsqrt(jnp.mean(x * x, axis=-1, keepdims=True) + RMS_EPS) * g


def _ada_body(ct_ref, w_ref, b_ref, o_ref):
    tn = o_ref.shape[-1]

    def step(k8, accs):
        r0 = pl.multiple_of(k8 * 8, 8)
        wk = w_ref[0, pl.ds(r0, 8), :]
        sk = _silu(ct_ref[pl.ds(r0, 8), :])
        return tuple(acc + wk * sk[:, j:j + 1] for j, acc in enumerate(accs))

    accs = lax.fori_loop(0, D_MODEL // 8, step, tuple(jnp.zeros((8, tn), F32) for _ in range(N_CVEC)))
    o_ref[0] = jnp.zeros((CVEC_PAD, tn), F32)
    for j in range(N_CVEC):
        o_ref[0, j:j + 1, :] = jnp.sum(accs[j], axis=0, keepdims=True) + b_ref[0]


def _ada(cvec_t, w_ada, b_ada):
    tn = 1536
    n_out = N_MOD * D_MODEL
    return pl.pallas_call(
        _ada_body,
        grid=(DEPTH, n_out // tn),
        in_specs=[pl.BlockSpec((D_MODEL, CVEC_PAD), lambda l, n: (0, 0)),
                  pl.BlockSpec((1, D_MODEL, tn), lambda l, n: (l, 0, n)),
                  pl.BlockSpec((1, 1, tn), lambda l, n: (l, 0, n))],
        out_specs=pl.BlockSpec((1, CVEC_PAD, tn), lambda l, n: (l, 0, n)),
        out_shape=jax.ShapeDtypeStruct((DEPTH, CVEC_PAD, n_out), F32),
        compiler_params=_cparams("parallel", "parallel"),
        name="ada",
    )(cvec_t, w_ada, b_ada.reshape(DEPTH, 1, n_out))


def _inproj_body(x_ref, mod_ref, g_ref, w_ref, o_ref):
    m = mod_ref[0]
    h = _rms(x_ref[...], g_ref[...]) * (1.0 + m[:, D_MODEL:2 * D_MODEL]) + m[:, 0:D_MODEL]
    o_ref[...] = _dot(h, w_ref[...])


def _inproj(x, mod_l, gain, w_bf16):
    n = w_bf16.shape[1]
    return pl.pallas_call(
        _inproj_body,
        grid=(T_ALL // TM,),
        in_specs=[pl.BlockSpec((TM, D_MODEL), lambda i: (i, 0)),
                  pl.BlockSpec((1, 1, N_MOD * D_MODEL), lambda i: (_mod_row(i), 0, 0)),
                  pl.BlockSpec((1, D_MODEL), lambda i: (0, 0)),
                  pl.BlockSpec((D_MODEL, n), lambda i: (0, 0))],
        out_specs=pl.BlockSpec((TM, n), lambda i: (i, 0)),
        out_shape=jax.ShapeDtypeStruct((T_ALL, n), F32),
        compiler_params=_cparams("parallel"),
        name="inproj",
    )(x, mod_l, gain.reshape(1, D_MODEL), w_bf16)


def _ctx_attn_body(qkv_ref, sink_ref, o_ref):
    scale = HEAD_DIM ** -0.5

    def head(q, k, v, sink):
        s = _dot_nt(q, k) * scale
        m = jnp.max(s, axis=-1, keepdims=True)
        if sink is not None:
            m = jnp.maximum(m, sink)
        p = jnp.exp(s - m)
        den = jnp.sum(p, axis=-1, keepdims=True)
        if sink is not None:
            den = den + jnp.exp(sink - m)
        return _dot(p, v) / den

    for h in range(A_HEADS):
        hk = h // A_GROUP
        q = qkv_ref[:, h * HEAD_DIM:(h + 1) * HEAD_DIM]
        k = qkv_ref[:, A_Q + hk * HEAD_DIM:A_Q + (hk + 1) * HEAD_DIM]
        v = qkv_ref[:, A_Q + A_KV + hk * HEAD_DIM:A_Q + A_KV + (hk + 1) * HEAD_DIM]
        o_ref[:, h * HEAD_DIM:(h + 1) * HEAD_DIM] = head(q, k, v, sink_ref[:, h:h + 1])
    base = A_Q + 2 * A_KV
    for h in range(B_HEADS):
        q = qkv_ref[:, base + h * HEAD_DIM:base + (h + 1) * HEAD_DIM]
        k = qkv_ref[:, base + B_W + h * HEAD_DIM:base + B_W + (h + 1) * HEAD_DIM]
        v = qkv_ref[:, base + 2 * B_W + h * HEAD_DIM:base + 2 * B_W + (h + 1) * HEAD_DIM]
        o_ref[:, A_Q + h * HEAD_DIM:A_Q + (h + 1) * HEAD_DIM] = head(q, k, v, None)


def _ctx_attn(qkv, sink):
    return pl.pallas_call(
        _ctx_attn_body,
        grid=(BATCH,),
        in_specs=[pl.BlockSpec((SEQ, ATTN_IN), lambda b: (b, 0)),
                  pl.BlockSpec((1, A_HEADS), lambda b: (0, 0))],
        out_specs=pl.BlockSpec((SEQ, A_Q + B_W), lambda b: (b, 0)),
        out_shape=jax.ShapeDtypeStruct((T_CTX, A_Q + B_W), F32),
        compiler_params=_cparams("parallel"),
        name="ctx_attn",
    )(qkv, sink.reshape(1, A_HEADS))


@functools.lru_cache(maxsize=None)
def _rope_tables(width):
    half = HEAD_DIM // 2
    t = np.arange(DEC_SEQ)
    inv = ROPE_BASE ** (-np.arange(0, half, 2, dtype=np.float64) / half)
    ang_r = (t // GRID_W)[:, None] * inv[None, :]
    ang_c = (t % GRID_W)[:, None] * inv[None, :]
    cos = np.concatenate([np.cos(ang_r)] * 2 + [np.cos(ang_c)] * 2, axis=-1)
    sin = np.concatenate([-np.sin(ang_r), np.sin(ang_r), -np.sin(ang_c), np.sin(ang_c)], axis=-1)
    reps = width // HEAD_DIM
    return (np.tile(cos, (1, reps)).astype(np.float32), np.tile(sin, (1, reps)).astype(np.float32))


def _rope_body(q_ref, k_ref, cq_ref, sq_ref, ck_ref, sk_ref, qo_ref, ko_ref):
    quarter = HEAD_DIM // 4

    def rot(x, cos, sin):
        w = x.shape[-1]
        lane = lax.broadcasted_iota(jnp.int32, x.shape, 1)
        fwd = pltpu.roll(x, w - quarter, axis=1)
        bwd = pltpu.roll(x, quarter, axis=1)
        partner = jnp.where((lane & (2 * quarter - 1)) < quarter, fwd, bwd)
        return x * cos + partner * sin

    qo_ref[...] = rot(q_ref[...], cq_ref[...], sq_ref[...])
    ko_ref[...] = rot(k_ref[...], ck_ref[...], sk_ref[...])


def _rope(qkv):
    cq, sq = _rope_tables(A_Q)
    ck, sk = _rope_tables(A_KV)
    tab = lambda w: pl.BlockSpec((DEC_SEQ, w), lambda b: (0, 0))
    row0 = T_CTX // DEC_SEQ
    return pl.pallas_call(
        _rope_body,
        grid=(DEC_BATCH,),
        in_specs=[pl.BlockSpec((DEC_SEQ, A_Q), lambda b: (row0 + b, 0)),
                  pl.BlockSpec((DEC_SEQ, A_KV), lambda b: (row0 + b, A_Q // A_KV)),
                  tab(A_Q), tab(A_Q), tab(A_KV), tab(A_KV)],
        out_specs=[pl.BlockSpec((DEC_SEQ, A_Q), lambda b: (b, 0)),
                   pl.BlockSpec((DEC_SEQ, A_KV), lambda b: (b, 0))],
        out_shape=[jax.ShapeDtypeStruct((T_LAT, A_Q), F32), jax.ShapeDtypeStruct((T_LAT, A_KV), F32)],
        compiler_params=_cparams("parallel"),
        name="rope",
    )(qkv, qkv, jnp.asarray(cq), jnp.asarray(sq), jnp.asarray(ck), jnp.asarray(sk))


WIN_QB = 256


def _win_attn_body(qraw_ref, qrot_ref, krot_ref, v_ref, kc_ref, vc_ref, sink_ref, o_ref):
    scale = HEAD_DIM ** -0.5
    sink = sink_ref[0]
    kc = kc_ref[0, 0]
    vc = vc_ref[0, 0]
    for qb in range(DEC_SEQ // WIN_QB):
        q0 = qb * WIN_QB
        lo = max(0, q0 - WINDOW)
        hi = min(DEC_SEQ, q0 + WIN_QB + WINDOW)
        s_loc = _dot_nt(qrot_ref[0, 0, q0:q0 + WIN_QB, :], krot_ref[0, 0, lo:hi, :]) * scale
        qpos = q0 + lax.broadcasted_iota(jnp.int32, s_loc.shape, 0)
        kpos = lo + lax.broadcasted_iota(jnp.int32, s_loc.shape, 1)
        s_loc = jnp.where(jnp.abs(kpos - qpos) <= WINDOW, s_loc, MASK_NEG)
        s_ctx = _dot_nt(qraw_ref[0, 0, q0:q0 + WIN_QB, :], kc) * scale
        m = jnp.maximum(jnp.maximum(jnp.max(s_loc, axis=-1, keepdims=True),
                                    jnp.max(s_ctx, axis=-1, keepdims=True)), sink)
        p_loc = jnp.exp(s_loc - m)
        p_ctx = jnp.exp(s_ctx - m)
        den = (jnp.sum(p_loc, axis=-1, keepdims=True) + jnp.sum(p_ctx, axis=-1, keepdims=True)
               + jnp.exp(sink - m))
        o_ref[0, 0, q0:q0 + WIN_QB, :] = (_dot(p_ctx, vc) + _dot(p_loc, v_ref[0, 0, lo:hi, :])) / den


def _win_attn(qraw, qrot, krot, v, kc, vc, sink):
    qs = pl.BlockSpec((1, 1, DEC_SEQ, HEAD_DIM), lambda b, h: (b, h, 0, 0))
    ks = pl.BlockSpec((1, 1, DEC_SEQ, HEAD_DIM), lambda b, h: (b, h // A_GROUP, 0, 0))
    cs = pl.BlockSpec((1, 1, PAST_LEN, HEAD_DIM), lambda b, h: (b, h // A_GROUP, 0, 0))
    return pl.pallas_call(
        _win_attn_body,
        grid=(DEC_BATCH, A_HEADS),
        in_specs=[qs, qs, ks, ks, cs, cs, pl.BlockSpec((1, 1, 1), lambda b, h: (h, 0, 0))],
        out_specs=qs,
        out_shape=jax.ShapeDtypeStruct((DEC_BATCH, A_HEADS, DEC_SEQ, HEAD_DIM), F32),
        compiler_params=_cparams("parallel", "parallel"),
        name="win_attn",
    )(qraw, qrot, krot, v, kc, vc, sink.reshape(A_HEADS, 1, 1))


GRID_ROWS = DEC_SEQ // GRID_W
NA_BAND = min(NA_ROWS, GRID_ROWS)


def _na_bias(rpb):
    r = np.arange(GRID_ROWS)
    rs = np.clip(r - NA_ROWS // 2, 0, GRID_ROWS - NA_BAND)
    band = rs[:, None] + np.arange(NA_BAND)[None, :]
    idx_r = band - r[:, None] + NA_ROWS - 1
    cq = np.arange(GRID_W)
    idx_c = np.clip(cq[None, :] - cq[:, None], -(NA_COLS - 1), NA_COLS - 1) + NA_COLS - 1
    cs = np.clip(cq - NA_COLS // 2, 0, GRID_W - NA_COLS)
    col_ok = (cq[None, :] >= cs[:, None]) & (cq[None, :] < cs[:, None] + NA_COLS)
    bias = rpb[:, idx_r[:, None, :, None], idx_c[None, :, None, :]]
    bias = jnp.where(col_ok[None, None, :, None, :], bias, MASK_NEG)
    return bias.reshape(B_HEADS, GRID_ROWS, GRID_W, NA_BAND * GRID_W).astype(F32)


def _na_attn_body(q_ref, k_ref, v_ref, kc_ref, vc_ref, bias_ref, o_ref):
    scale = HEAD_DIM ** -0.5
    kc = kc_ref[0, 0]
    vc = vc_ref[0, 0]
    for r in range(GRID_ROWS):
        rs = min(max(r - NA_ROWS // 2, 0), GRID_ROWS - NA_BAND)
        q = q_ref[0, 0, r * GRID_W:(r + 1) * GRID_W, :]
        kb = k_ref[0, 0, rs * GRID_W:(rs + NA_BAND) * GRID_W, :]
        vb = v_ref[0, 0, rs * GRID_W:(rs + NA_BAND) * GRID_W, :]
        s_loc = _dot_nt(q, kb) * scale + bias_ref[0, r]
        s_ctx = _dot_nt(q, kc) * scale
        m = jnp.maximum(jnp.max(s_loc, axis=-1, keepdims=True), jnp.max(s_ctx, axis=-1, keepdims=True))
        p_loc = jnp.exp(s_loc - m)
        p_ctx = jnp.exp(s_ctx - m)
        den = jnp.sum(p_loc, axis=-1, keepdims=True) + jnp.sum(p_ctx, axis=-1, keepdims=True)
        o_ref[0, 0, r * GRID_W:(r + 1) * GRID_W, :] = (_dot(p_ctx, vc) + _dot(p_loc, vb)) / den


def _na_attn(q, k, v, kc, vc, bias):
    qs = pl.BlockSpec((1, 1, DEC_SEQ, HEAD_DIM), lambda b, h: (b, h, 0, 0))
    cs = pl.BlockSpec((1, 1, PAST_LEN, HEAD_DIM), lambda b, h: (b, h, 0, 0))
    return pl.pallas_call(
        _na_attn_body,
        grid=(DEC_BATCH, B_HEADS),
        in_specs=[qs, qs, qs, cs, cs,
                  pl.BlockSpec((1, GRID_ROWS, GRID_W, NA_BAND * GRID_W), lambda b, h: (h, 0, 0, 0))],
        out_specs=qs,
        out_shape=jax.ShapeDtypeStruct((DEC_BATCH, B_HEADS, DEC_SEQ, HEAD_DIM), F32),
        compiler_params=_cparams("parallel", "parallel"),
        name="na_attn",
    )(q, k, v, kc, vc, bias)


@functools.lru_cache(maxsize=None)
def _dft_mats(L):
    n = 2 * L
    fc = min(L, DFT_CHUNK)
    f = np.arange(L)[:, None]
    t = np.arange(L)[None, :]
    ang = 2.0 * np.pi * ((f * t) % n) / n
    m1 = np.cos(ang)
    m2 = np.sin(ang)
    m2[0, :] = np.where(np.arange(L) % 2 == 0, 1.0, -1.0)
    wgt = np.full((L, 1), 2.0)
    wgt[0, 0] = 1.0
    nch = L // fc
    fwd = np.concatenate([m1.reshape(nch, fc, L), m2.reshape(nch, fc, L)], axis=1)
    inv = np.concatenate([(m1 * wgt / n).reshape(nch, fc, L), (m2 * wgt / n).reshape(nch, fc, L)], axis=1)
    inv = np.transpose(inv, (0, 2, 1))
    return fwd.astype(np.float32), inv.astype(np.float32)


@functools.lru_cache(maxsize=None)
def _filter_consts(L):
    t = np.linspace(0.0, 1.0, L)[:, None]
    bands = (C_EMB - 1) // 2
    ang = (2.0 * math.pi / L) * np.arange(L)[:, None] * np.linspace(1e-4, bands - 1, bands)[None, :]
    z = np.concatenate([t, np.cos(ang), -np.sin(ang)], axis=-1)
    zpad = np.zeros((L, 128))
    zpad[:, :C_EMB] = z
    deltas = np.abs(np.linspace(HYENA_MIN_DECAY, HYENA_MAX_DECAY, C_DIM))
    window = np.exp(-t * deltas[None, :])
    return zpad.astype(np.float32), window.astype(np.float32)


def _filter_body(z_ref, w1_ref, b1_ref, w2_ref, b2_ref, w3_ref, b3_ref, fr_ref, w4_ref, win_ref, fm_ref,
                 hr_ref, g_ref, hq_ref, hs_scr, hd_scr):
    c = pl.program_id(0)
    fc = hr_ref.shape[0]

    @pl.when(c == 0)
    def _():
        fr = fr_ref[...]
        hh = jnp.sin(fr * (_dot_hi(z_ref[...], w1_ref[...]) + b1_ref[...]))
        hh = jnp.sin(fr * (_dot_hi(hh, w2_ref[...]) + b2_ref[...]))
        hh = jnp.sin(fr * (_dot_hi(hh, w3_ref[...]) + b3_ref[...]))
        hh = _dot_hi(hh, w4_ref[...])
        hf = hh[:, :C_DIM] * win_ref[...]
        hb = hh[:, C_DIM:] * win_ref[...]
        hs_scr[...] = hf + hb
        hd_scr[...] = hf - hb

    fm = fm_ref[0]
    hr = _dot_hi(fm[:fc], hs_scr[...])
    first = (lax.broadcasted_iota(jnp.int32, (fc, C_DIM), 0) == 0) & (c == 0)
    hr_ref[...] = hr
    g_ref[...] = jnp.where(first, 0.0, _dot_hi(fm[fc:], hd_scr[...]))
    hs = hs_scr[...]
    sign = jnp.where((lax.broadcasted_iota(jnp.int32, hs.shape, 0) & 1) == 0, 1.0, -1.0)
    hq_ref[...] = jnp.where(first, jnp.sum(hs * sign, axis=0, keepdims=True), hr)


def _hyena_filter(L, filt):
    w1, b1, w2, b2, w3, b3, freq, w4 = filt
    zpad, window = _filter_consts(L)
    fwd, _ = _dft_mats(L)
    nch, fc2, _ = fwd.shape
    fc = fc2 // 2
    w1p = jnp.pad(w1, ((0, 128 - C_EMB), (0, 0)))
    full = lambda shape: pl.BlockSpec(shape, lambda c: tuple(0 for _ in shape))
    out_spec = pl.BlockSpec((fc, C_DIM), lambda c: (c, 0))
    out_sd = jax.ShapeDtypeStruct((L, C_DIM), F32)
    return pl.pallas_call(
        _filter_body,
        grid=(nch,),
        in_specs=[full((L, 128)), full((128, C_FFN)), full((1, C_FFN)), full((C_FFN, C_FFN)), full((1, C_FFN)),
                  full((C_FFN, C_FFN)), full((1, C_FFN)), full((1, C_FFN)), full((C_FFN, 2 * C_DIM)),
                  full((L, C_DIM)), pl.BlockSpec((1, fc2, L), lambda c: (c, 0, 0))],
        out_specs=[out_spec, out_spec, out_spec],
        out_shape=[out_sd, out_sd, out_sd],
        scratch_shapes=[pltpu.VMEM((L, C_DIM), F32), pltpu.VMEM((L, C_DIM), F32)],
        compiler_params=_cparams("arbitrary"),
        name="hyena_filter",
    )(jnp.asarray(zpad), w1p, b1.reshape(1, C_FFN), w2, b2.reshape(1, C_FFN), w3, b3.reshape(1, C_FFN),
      freq.reshape(1, C_FFN), w4, jnp.asarray(window), jnp.asarray(fwd))


def _hyena_body(u_ref, cw_ref, cb_ref, d_ref, fm_ref, fi_ref, hr_ref, g_ref, hq_ref, y_ref,
                x0_scr, z_scr, acc_scr):
    c = pl.program_id(1)
    L = y_ref.shape[0]
    fc = hr_ref.shape[0]

    @pl.when(c == 0)
    def _():
        row = lax.broadcasted_iota(jnp.int32, (L, C_DIM), 0)

        def short_conv(sec):
            cols = slice(sec * C_DIM, (sec + 1) * C_DIM)
            u = u_ref[:, cols]
            prev = jnp.where(row == 0, 0.0, pltpu.roll(u, 1, axis=0))
            nxt = jnp.where(row == L - 1, 0.0, pltpu.roll(u, L - 1, axis=0))
            return (prev * cw_ref[0:1, cols] + u * cw_ref[1:2, cols] + nxt * cw_ref[2:3, cols]
                    + cb_ref[:, cols])

        x0_scr[...] = short_conv(0)
        z_scr[...] = short_conv(1) * short_conv(2)
        acc_scr[...] = jnp.zeros((L, C_DIM), F32)

    ab = _dot_hi(fm_ref[0], z_scr[...])
    a, b = ab[:fc], ab[fc:]
    hr, g, hq = hr_ref[...], g_ref[...], hq_ref[...]
    pq = jnp.concatenate([a * hr - b * g, a * g + b * hq], axis=0)
    acc_scr[...] += _dot_hi(fi_ref[0], pq)

    @pl.when(c == pl.num_programs(1) - 1)
    def _():
        y_ref[...] = x0_scr[...] * (acc_scr[...] + z_scr[...] * d_ref[...])


def _hyena(u, row_blk0, n_seq, L, conv_w, conv_b, d_skip, spec):
    hr, g, hq = spec
    fwd, inv = _dft_mats(L)
    nch, fc2, _ = fwd.shape
    fc = fc2 // 2
    u_w = 3 * C_DIM
    return pl.pallas_call(
        _hyena_body,
        grid=(n_seq, nch),
        in_specs=[pl.BlockSpec((L, u_w), lambda b, c: (row_blk0 + b, 0)),
                  pl.BlockSpec((3, u_w), lambda b, c: (0, 0)),
                  pl.BlockSpec((1, u_w), lambda b, c: (0, 0)),
                  pl.BlockSpec((1, C_DIM), lambda b, c: (0, 0)),
                  pl.BlockSpec((1, fc2, L), lambda b, c: (c, 0, 0)),
                  pl.BlockSpec((1, L, fc2), lambda b, c: (c, 0, 0)),
                  pl.BlockSpec((fc, C_DIM), lambda b, c: (c, 0)),
                  pl.BlockSpec((fc, C_DIM), lambda b, c: (c, 0)),
                  pl.BlockSpec((fc, C_DIM), lambda b, c: (c, 0))],
        out_specs=pl.BlockSpec((L, C_DIM), lambda b, c: (b, 0)),
        out_shape=jax.ShapeDtypeStruct((n_seq * L, C_DIM), F32),
        scratch_shapes=[pltpu.VMEM((L, C_DIM), F32)] * 3,
        compiler_params=_cparams("parallel", "arbitrary"),
        name="hyena",
    )(u, conv_w, conv_b.reshape(1, u_w), d_skip.reshape(1, C_DIM), jnp.asarray(fwd), jnp.asarray(inv), hr, g, hq)


def _hgrn_body(q_ref, ff_ref, fb_ref, i_ref, g_ref, lbf_ref, lbb_ref, nd_ref, s0f_ref, s0b_ref,
               o_ref, sf_ref, sb_ref, o_scr, *, layer):
    L = o_ref.shape[0]
    C = GLA_CHUNK
    nc = L // C
    mid = C // 2
    q = _silu(q_ref[...])
    v = i_ref[...]

    def lower_bound(ref):
        gm = ref[...]
        e = jnp.exp(gm - jnp.max(gm, axis=0, keepdims=True))
        p = e / jnp.sum(e, axis=0, keepdims=True)
        return jnp.sum(p[0:layer + 1], axis=0, keepdims=True) - p[0:1]

    def gates(fx, lb):
        f = lb + (1.0 - lb) * jax.nn.sigmoid(fx)
        return 1.0 - f, jnp.log(f)

    kf, lgf = gates(ff_ref[...], lower_bound(lbf_ref))
    kb, lgb = gates(fb_ref[...], lower_bound(lbb_ref))
    ti = lax.broadcasted_iota(jnp.int32, (C, C), 0)
    si = lax.broadcasted_iota(jnp.int32, (C, C), 1)
    causal = si <= ti
    anti = si >= ti
    tril = causal.astype(F32)
    triu = anti.astype(F32)

    st = jnp.transpose(s0f_ref[0, 0])
    for n in range(nc):
        sl = slice(n * C, (n + 1) * C)
        b = _dot_hi(tril, lgf[sl])
        btot = b[C - 1:C]
        ref = b[mid:mid + 1]
        qc, kc, vc = q[sl], kf[sl], v[sl]
        sc = jnp.where(causal, _dot_nt(qc * jnp.exp(b - ref), kc * jnp.exp(ref - b)), 0.0)
        o_scr[sl, :] = _dot(sc, vc) + _dot_nt(qc * jnp.exp(b), st)
        st = st * jnp.exp(btot) + _dot_tn(vc, kc * jnp.exp(btot - b))
    sf_ref[0, 0] = jnp.transpose(st)

    st = jnp.transpose(s0b_ref[0, 0])
    for n in reversed(range(nc)):
        sl = slice(n * C, (n + 1) * C)
        b = _dot_hi(triu, lgb[sl])
        btot = b[0:1]
        ref = b[mid:mid + 1]
        qc, kc, vc = q[sl], kb[sl], v[sl]
        sc = jnp.where(anti, _dot_nt(qc * jnp.exp(b - ref), kc * jnp.exp(ref - b)), 0.0)
        o_scr[sl, :] += _dot(sc, vc) + _dot_nt(qc * jnp.exp(b), st)
        st = st * jnp.exp(btot) + _dot_tn(vc, kc * jnp.exp(btot - b))
    sb_ref[0, 0] = jnp.transpose(st)

    o_ref[...] = _rms(o_scr[...], nd_ref[...]) * _silu(g_ref[...])


def _hgrn(u, row_blk0, n_seq, L, lb_fwd, lb_bwd, norm_d, s0f, s0b, layer):
    col0 = 3 * C_DIM // D_KDIM
    col = lambda j: pl.BlockSpec((L, D_KDIM), lambda b, h: (row_blk0 + b, col0 + j * D_HEADS + h))
    lbs = pl.BlockSpec((DEPTH, D_KDIM), lambda b, h: (0, h))
    st = pl.BlockSpec((1, 1, D_KDIM, D_VDIM), lambda b, h: (b, h, 0, 0))
    st_sd = jax.ShapeDtypeStruct((n_seq, D_HEADS, D_KDIM, D_VDIM), F32)
    return pl.pallas_call(
        functools.partial(_hgrn_body, layer=layer),
        grid=(n_seq, D_HEADS),
        in_specs=[col(0), col(1), col(2), col(3), col(4), lbs, lbs,
                  pl.BlockSpec((1, D_VDIM), lambda b, h: (0, 0)), st, st],
        out_specs=[pl.BlockSpec((L, D_VDIM), lambda b, h: (b, h)), st, st],
        out_shape=[jax.ShapeDtypeStruct((n_seq * L, D_HEADS * D_VDIM), F32), st_sd, st_sd],
        scratch_shapes=[pltpu.VMEM((L, D_VDIM), F32)],
        compiler_params=_cparams("parallel", "parallel"),
        name="hgrn",
    )(u, u, u, u, u, lb_fwd, lb_bwd, norm_d.reshape(1, D_VDIM), s0f, s0b)


def _outproj_body(a_ref, b_ref, x_ref, mod_ref, gf_ref, w_ref, wr_ref, rb_ref, x1_ref, h2_ref, gate_ref):
    m = mod_ref[0]
    half = a_ref.shape[1]
    out = _dot(a_ref[...], w_ref[0:half, :]) + _dot(b_ref[...], w_ref[half:, :])
    x1 = x_ref[...] + m[:, 2 * D_MODEL:3 * D_MODEL] * out
    x1_ref[...] = x1
    h2 = _rms(x1, gf_ref[...]) * (1.0 + m[:, 4 * D_MODEL:5 * D_MODEL]) + m[:, 3 * D_MODEL:4 * D_MODEL]
    h2_ref[...] = h2.astype(BF16)
    scores = jax.nn.sigmoid(_dot_hi(h2, wr_ref[...]))
    work = scores + rb_ref[...]
    lane = lax.broadcasted_iota(jnp.int32, work.shape, 1).astype(F32)
    chosen = jnp.zeros(work.shape, jnp.bool_)
    for _ in range(TOP_K):
        best = jnp.max(work, axis=-1, keepdims=True)
        first = jnp.min(jnp.where(work == best, lane, float(N_EXPERTS)), axis=-1, keepdims=True)
        hit = lane == first
        chosen = chosen | hit
        work = jnp.where(hit, -jnp.inf, work)
    g = jnp.where(chosen, scores, 0.0)
    gate_ref[...] = g / jnp.sum(g, axis=-1, keepdims=True) * ROUTE_SCALE


def _outproj(a, b, x, mod_l, gain_ffn, w_out_bf16, w_router, router_bias):
    half = a.shape[1]
    return pl.pallas_call(
        _outproj_body,
        grid=(T_ALL // TM,),
        in_specs=[pl.BlockSpec((TM, half), lambda i: (i, 0)),
                  pl.BlockSpec((TM, half), lambda i: (i, 0)),
                  pl.BlockSpec((TM, D_MODEL), lambda i: (i, 0)),
                  pl.BlockSpec((1, 1, N_MOD * D_MODEL), lambda i: (_mod_row(i), 0, 0)),
                  pl.BlockSpec((1, D_MODEL), lambda i: (0, 0)),
                  pl.BlockSpec((2 * half, D_MODEL), lambda i: (0, 0)),
                  pl.BlockSpec((D_MODEL, N_EXPERTS), lambda i: (0, 0)),
                  pl.BlockSpec((1, N_EXPERTS), lambda i: (0, 0))],
        out_specs=[pl.BlockSpec((TM, D_MODEL), lambda i: (i, 0)),
                   pl.BlockSpec((TM, D_MODEL), lambda i: (i, 0)),
                   pl.BlockSpec((TM, N_EXPERTS), lambda i: (i, 0))],
        out_shape=[jax.ShapeDtypeStruct((T_ALL, D_MODEL), F32),
                   jax.ShapeDtypeStruct((T_ALL, D_MODEL), BF16),
                   jax.ShapeDtypeStruct((T_ALL, N_EXPERTS), F32)],
        compiler_params=_cparams("parallel"),
        name="outproj_router",
    )(a, b, x, mod_l, gain_ffn.reshape(1, D_MODEL), w_out_bf16, w_router, router_bias.reshape(1, N_EXPERTS))


def _moe_body(h_ref, gate_ref, x1_ref, mod_ref, wg_ref, wu_ref, wd_ref, sg_ref, su_ref, sd_ref, fn_ref,
              o_ref, acc_scr, *, final):
    e = pl.program_id(1)
    h = h_ref[...]

    def hidden(wg, wu):
        return _silu(_dot(h, wg)) * _dot(h, wu)

    @pl.when(e == 0)
    def _():
        acc_scr[...] = _dot(hidden(sg_ref[...], su_ref[...]), sd_ref[...])

    lane = lax.broadcasted_iota(jnp.int32, gate_ref.shape, 1)
    gcol = jnp.sum(jnp.where(lane == e, gate_ref[...], 0.0), axis=-1, keepdims=True)
    acc_scr[...] += _dot(hidden(wg_ref[0], wu_ref[0]) * gcol, wd_ref[0])

    @pl.when(e == N_EXPERTS - 1)
    def _():
        m = mod_ref[0]
        y = x1_ref[...] + m[:, 5 * D_MODEL:6 * D_MODEL] * acc_scr[...]
        o_ref[...] = _rms(y, fn_ref[...]) if final else y


def _moe(h2, gates, x1, mod_l, w_gate, w_up, w_down, ws_gate, ws_up, ws_down, final_norm, final):
    tm = MOE_TM
    tok = lambda shape: pl.BlockSpec(shape, lambda i, e: (i, 0))
    full = lambda shape: pl.BlockSpec(shape, lambda i, e: (0, 0))
    mod_spec = pl.BlockSpec((1, 1, N_MOD * D_MODEL), lambda i, e: (_mod_row(i * (tm // TM)), 0, 0))
    return pl.pallas_call(
        functools.partial(_moe_body, final=final),
        grid=(T_ALL // tm, N_EXPERTS),
        in_specs=[tok((tm, D_MODEL)), tok((tm, N_EXPERTS)), tok((tm, D_MODEL)), mod_spec,
                  pl.BlockSpec((1, D_MODEL, D_EXPERT), lambda i, e: (e, 0, 0)),
                  pl.BlockSpec((1, D_MODEL, D_EXPERT), lambda i, e: (e, 0, 0)),
                  pl.BlockSpec((1, D_EXPERT, D_MODEL), lambda i, e: (e, 0, 0)),
                  full((D_MODEL, D_EXPERT)), full((D_MODEL, D_EXPERT)), full((D_EXPERT, D_MODEL)),
                  full((1, D_MODEL))],
        out_specs=tok((tm, D_MODEL)),
        out_shape=jax.ShapeDtypeStruct((T_ALL, D_MODEL), F32),
        scratch_shapes=[pltpu.VMEM((tm, D_MODEL), F32)],
        compiler_params=_cparams("parallel", "arbitrary"),
        name="moe",
    )(h2, gates, x1, mod_l, w_gate, w_up, w_down, ws_gate, ws_up, ws_down, final_norm.reshape(1, D_MODEL))


def _heads_major(t, n_heads):
    return t.reshape(DEC_BATCH, DEC_SEQ, n_heads, HEAD_DIM).transpose(0, 2, 1, 3)


def _tokens_major(t):
    return t.transpose(0, 2, 1, 3).reshape(T_LAT, -1)


def kernel(x_prompt, x_sample, cache_a_k, cache_a_v, cache_b_k, cache_b_v, state_d_fwd, state_d_bwd, c, c_ctx, w_ada, b_ada, norm_mix, norm_ffn, w_in_attn, w_out_attn, sink_a, rpb_b, w_in_rec, w_out_rec, conv_w, conv_b, filt_w1, filt_b1, filt_w2, filt_b2, filt_w3, filt_b3, filt_freq, filt_w4, d_skip, lb_fwd, lb_bwd, norm_d, w_router, router_bias, w_gate, w_up, w_down, ws_gate, ws_up, ws_down, final_norm):
    x = jnp.concatenate([x_prompt.reshape(T_CTX, D_MODEL), x_sample.reshape(T_LAT, D_MODEL)], axis=0)
    cvec = jnp.concatenate([c_ctx[None, :], c, jnp.zeros((CVEC_PAD - N_CVEC, D_MODEL), F32)], axis=0)
    mod = _ada(cvec.T, w_ada, b_ada).reshape(DEPTH, CVEC_PAD, 1, N_MOD * D_MODEL)

    new_kv = None
    new_state = None
    for l in range(DEPTH):
        j = l // 2
        final = l == DEPTH - 1
        if l % 2 == 0:
            qkv = _inproj(x, mod[l], norm_mix[l], w_in_attn[j].astype(BF16))
            lat = qkv[T_CTX:]
            o_ctx = _ctx_attn(qkv, sink_a[j])
            kv_cols = lambda lo, heads: qkv[:T_CTX, lo:lo + heads * HEAD_DIM].reshape(BATCH, 1, SEQ, heads, HEAD_DIM)
            new_kv = (kv_cols(A_Q, A_KV_HEADS), kv_cols(A_Q + A_KV, A_KV_HEADS),
                      kv_cols(A_Q + 2 * A_KV + B_W, B_HEADS), kv_cols(A_Q + 2 * A_KV + 2 * B_W, B_HEADS))
            base = A_Q + 2 * A_KV
            q_rot, k_rot = _rope(qkv)
            hm = _heads_major
            cache_hm = lambda t: t[:, j].transpose(0, 2, 1, 3)
            o_a = _win_attn(hm(lat[:, :A_Q], A_HEADS), hm(q_rot, A_HEADS), hm(k_rot, A_KV_HEADS),
                            hm(lat[:, A_Q + A_KV:base], A_KV_HEADS), cache_hm(cache_a_k), cache_hm(cache_a_v),
                            sink_a[j])
            o_b = _na_attn(hm(lat[:, base:base + B_W], B_HEADS), hm(lat[:, base + B_W:base + 2 * B_W], B_HEADS),
                           hm(lat[:, base + 2 * B_W:], B_HEADS), cache_hm(cache_b_k), cache_hm(cache_b_v),
                           _na_bias(rpb_b[j]))
            mix_a = jnp.concatenate([o_ctx[:, :A_Q], _tokens_major(o_a)], axis=0)
            mix_b = jnp.concatenate([o_ctx[:, A_Q:], _tokens_major(o_b)], axis=0)
            w_out = w_out_attn[j]
        else:
            u = _inproj(x, mod[l], norm_mix[l], w_in_rec[j].astype(BF16))
            filt = (filt_w1[j], filt_b1[j], filt_w2[j], filt_b2[j], filt_w3[j], filt_b3[j], filt_freq[j],
                    filt_w4[j])
            y_ctx = _hyena(u, 0, BATCH, SEQ, conv_w[j], conv_b[j], d_skip[j], _hyena_filter(SEQ, filt))
            y_lat = _hyena(u, T_CTX // DEC_SEQ, DEC_BATCH, DEC_SEQ, conv_w[j], conv_b[j], d_skip[j],
                           _hyena_filter(DEC_SEQ, filt))
            zeros = jnp.zeros((BATCH, D_HEADS, D_KDIM, D_VDIM), F32)
            o_ctx, s_f, s_b = _hgrn(u, 0, BATCH, SEQ, lb_fwd, lb_bwd, norm_d[j], zeros, zeros, l)
            o_lat, _, _ = _hgrn(u, T_CTX // DEC_SEQ, DEC_BATCH, DEC_SEQ, lb_fwd, lb_bwd, norm_d[j],
                                state_d_fwd[:, j], state_d_bwd[:, j], l)
            new_state = (s_f[:, None], s_b[:, None])
            mix_a = jnp.concatenate([y_ctx, y_lat], axis=0)
            mix_b = jnp.concatenate([o_ctx, o_lat], axis=0)
            w_out = w_out_rec[j]
        x1, h2, gates = _outproj(mix_a, mix_b, x, mod[l], norm_ffn[l], w_out.astype(BF16), w_router[l],
                                 router_bias[l])
        x = _moe(h2, gates, x1, mod[l], w_gate[l], w_up[l], w_down[l], ws_gate[l], ws_up[l], ws_down[l],
                 final_norm, final)

    y_prompt = x[:T_CTX].reshape(BATCH, SEQ, D_MODEL)
    y_sample = x[T_CTX:].reshape(DEC_BATCH, DEC_SEQ, D_MODEL)
    return (y_prompt, y_sample) + new_kv + new_state
```

```python
import functools
import math

import numpy as np
import jax
import jax.numpy as jnp
from jax import lax
from jax.experimental import pallas as pl
from jax.experimental.pallas import tpu as pltpu

F32 = jnp.float32
BF16 = jnp.bfloat16
HI = lax.Precision.HIGHEST

D_MODEL = 1024
BATCH = 16
SEQ = 256
DEPTH = 2
DEC_BATCH = 2
DEC_SEQ = 1024
PAST_LEN = 512
GRID_W = 64
HEAD_DIM = 64
N_MOD = 6
RMS_EPS = 1e-6
A_HEADS = 8
A_KV_HEADS = 2
A_GROUP = A_HEADS // A_KV_HEADS
WINDOW = 128
ROPE_BASE = 10000.0
B_HEADS = 8
NA_ROWS = 8
NA_COLS = 16
C_DIM = 512
C_EMB = 33
C_FFN = 64
HYENA_MIN_DECAY = math.log(1e-2) / 1.5
HYENA_MAX_DECAY = math.log(1e-2) / 0.3
D_KDIM = 128
D_VDIM = 128
D_HEADS = 4
N_EXPERTS = 64
TOP_K = 8
D_EXPERT = 256
ROUTE_SCALE = 2.5
A_Q = A_HEADS * HEAD_DIM
A_KV = A_KV_HEADS * HEAD_DIM
B_W = B_HEADS * HEAD_DIM
ATTN_IN = A_Q + 2 * A_KV + 3 * B_W
REC_IN = 3 * C_DIM + 5 * D_HEADS * D_KDIM

T_CTX = BATCH * SEQ
T_LAT = DEC_BATCH * DEC_SEQ
T_ALL = T_CTX + T_LAT
N_CVEC = 1 + DEC_BATCH
CVEC_PAD = 8
TM = 256
MASK_NEG = -1e30
GLA_CHUNK = 64
DFT_CHUNK = 256
MOE_TM = 1024
VMEM_LIMIT = 56 * 1024 * 1024


def _cparams(*sem):
    return pltpu.CompilerParams(dimension_semantics=sem, vmem_limit_bytes=VMEM_LIMIT)


def _mod_row(i):
    return jnp.where(i < T_CTX // TM, 0, 1 + (i - T_CTX // TM) // (DEC_SEQ // TM))


def _dot(a, b):
    return jnp.dot(a.astype(BF16), b.astype(BF16), preferred_element_type=F32)


def _dot_nt(a, b):
    return lax.dot_general(a.astype(BF16), b.astype(BF16), (((1,), (1,)), ((), ())),
                           preferred_element_type=F32)


def _dot_tn(a, b):
    return lax.dot_general(a.astype(BF16), b.astype(BF16), (((0,), (0,)), ((), ())),
                           preferred_element_type=F32)


def _dot_hi(a, b):
    return jnp.dot(a, b, precision=HI, preferred_element_type=F32)


def _silu(x):
    return x * jax.nn.sigmoid(x)


def _rms(x, g):
    return x * lax.rsqrt(jnp.mean(x * x, axis=-1, keepdims=True) + RMS_EPS) * g


def _ada_body(ct_ref, w_ref, b_ref, o_ref):
    tn = o_ref.shape[-1]

    def step(k8, accs):
        r0 = pl.multiple_of(k8 * 8, 8)
        wk = w_ref[0, pl.ds(r0, 8), :]
        sk = _silu(ct_ref[pl.ds(r0, 8), :])
        return tuple(acc + wk * sk[:, j:j + 1] for j, acc in enumerate(accs))

    accs = lax.fori_loop(0, D_MODEL // 8, step, tuple(jnp.zeros((8, tn), F32) for _ in range(N_CVEC)))
    o_ref[0] = jnp.zeros((CVEC_PAD, tn), F32)
    for j in range(N_CVEC):
        o_ref[0, j:j + 1, :] = jnp.sum(accs[j], axis=0, keepdims=True) + b_ref[0]


def _ada(cvec_t, w_ada, b_ada):
    tn = 1536
    n_out = N_MOD * D_MODEL
    return pl.pallas_call(
        _ada_body,
        grid=(DEPTH, n_out // tn),
        in_specs=[pl.BlockSpec((D_MODEL, CVEC_PAD), lambda l, n: (0, 0)),
                  pl.BlockSpec((1, D_MODEL, tn), lambda l, n: (l, 0, n)),
                  pl.BlockSpec((1, 1, tn), lambda l, n: (l, 0, n))],
        out_specs=pl.BlockSpec((1, CVEC_PAD, tn), lambda l, n: (l, 0, n)),
        out_shape=jax.ShapeDtypeStruct((DEPTH, CVEC_PAD, n_out), F32),
        compiler_params=_cparams("parallel", "parallel"),
        name="ada",
    )(cvec_t, w_ada, b_ada.reshape(DEPTH, 1, n_out))


def _inproj_body(x_ref, mod_ref, g_ref, w_ref, o_ref):
    m = mod_ref[0]
    h = _rms(x_ref[...], g_ref[...]) * (1.0 + m[:, D_MODEL:2 * D_MODEL]) + m[:, 0:D_MODEL]
    o_ref[...] = _dot(h, w_ref[...])


def _inproj(x, mod_l, gain, w_bf16):
    n = w_bf16.shape[1]
    return pl.pallas_call(
        _inproj_body,
        grid=(T_ALL // TM,),
        in_specs=[pl.BlockSpec((TM, D_MODEL), lambda i: (i, 0)),
                  pl.BlockSpec((1, 1, N_MOD * D_MODEL), lambda i: (_mod_row(i), 0, 0)),
                  pl.BlockSpec((1, D_MODEL), lambda i: (0, 0)),
                  pl.BlockSpec((D_MODEL, n), lambda i: (0, 0))],
        out_specs=pl.BlockSpec((TM, n), lambda i: (i, 0)),
        out_shape=jax.ShapeDtypeStruct((T_ALL, n), F32),
        compiler_params=_cparams("parallel"),
        name="inproj",
    )(x, mod_l, gain.reshape(1, D_MODEL), w_bf16)


def _ctx_attn_body(qkv_ref, sink_ref, o_ref):
    scale = HEAD_DIM ** -0.5

    def head(q, k, v, sink):
        s = _dot_nt(q, k) * scale
        m = jnp.max(s, axis=-1, keepdims=True)
        if sink is not None:
            m = jnp.maximum(m, sink)
        p = jnp.exp(s - m)
        den = jnp.sum(p, axis=-1, keepdims=True)
        if sink is not None:
            den = den + jnp.exp(sink - m)
        return _dot(p, v) / den

    for h in range(A_HEADS):
        hk = h // A_GROUP
        q = qkv_ref[:, h * HEAD_DIM:(h + 1) * HEAD_DIM]
        k = qkv_ref[:, A_Q + hk * HEAD_DIM:A_Q + (hk + 1) * HEAD_DIM]
        v = qkv_ref[:, A_Q + A_KV + hk * HEAD_DIM:A_Q + A_KV + (hk + 1) * HEAD_DIM]
        o_ref[:, h * HEAD_DIM:(h + 1) * HEAD_DIM] = head(q, k, v, sink_ref[:, h:h + 1])
    base = A_Q + 2 * A_KV
    for h in range(B_HEADS):
        q = qkv_ref[:, base + h * HEAD_DIM:base + (h + 1) * HEAD_DIM]
        k = qkv_ref[:, base + B_W + h * HEAD_DIM:base + B_W + (h + 1) * HEAD_DIM]
        v = qkv_ref[:, base + 2 * B_W + h * HEAD_DIM:base + 2 * B_W + (h + 1) * HEAD_DIM]
        o_ref[:, A_Q + h * HEAD_DIM:A_Q + (h + 1) * HEAD_DIM] = head(q, k, v, None)


def _ctx_attn(qkv, sink):
    return pl.pallas_call(
        _ctx_attn_body,
        grid=(BATCH,),
        in_specs=[pl.BlockSpec((SEQ, ATTN_IN), lambda b: (b, 0)),
                  pl.BlockSpec((1, A_HEADS), lambda b: (0, 0))],
        out_specs=pl.BlockSpec((SEQ, A_Q + B_W), lambda b: (b, 0)),
        out_shape=jax.ShapeDtypeStruct((T_CTX, A_Q + B_W), F32),
        compiler_params=_cparams("parallel"),
        name="ctx_attn",
    )(qkv, sink.reshape(1, A_HEADS))


@functools.lru_cache(maxsize=None)
def _rope_tables(width):
    half = HEAD_DIM // 2
    t = np.arange(DEC_SEQ)
    inv = ROPE_BASE ** (-np.arange(0, half, 2, dtype=np.float64) / half)
    ang_r = (t // GRID_W)[:, None] * inv[None, :]
    ang_c = (t % GRID_W)[:, None] * inv[None, :]
    cos = np.concatenate([np.cos(ang_r)] * 2 + [np.cos(ang_c)] * 2, axis=-1)
    sin = np.concatenate([-np.sin(ang_r), np.sin(ang_r), -np.sin(ang_c), np.sin(ang_c)], axis=-1)
    reps = width // HEAD_DIM
    return (np.tile(cos, (1, reps)).astype(np.float32), np.tile(sin, (1, reps)).astype(np.float32))


def _rope_body(q_ref, k_ref, cq_ref, sq_ref, ck_ref, sk_ref, qo_ref, ko_ref):
    quarter = HEAD_DIM // 4

    def rot(x, cos, sin):
        w = x.shape[-1]
        lane = lax.broadcasted_iota(jnp.int32, x.shape, 1)
        fwd = pltpu.roll(x, w - quarter, axis=1)
        bwd = pltpu.roll(x, quarter, axis=1)
        partner = jnp.where((lane & (2 * quarter - 1)) < quarter, fwd, bwd)
        return x * cos + partner * sin

    qo_ref[...] = rot(q_ref[...], cq_ref[...], sq_ref[...])
    ko_ref[...] = rot(k_ref[...], ck_ref[...], sk_ref[...])


def _rope(qkv):
    cq, sq = _rope_tables(A_Q)
    ck, sk = _rope_tables(A_KV)
    tab = lambda w: pl.BlockSpec((DEC_SEQ, w), lambda b: (0, 0))
    row0 = T_CTX // DEC_SEQ
    return pl.pallas_call(
        _rope_body,
        grid=(DEC_BATCH,),
        in_specs=[pl.BlockSpec((DEC_SEQ, A_Q), lambda b: (row0 + b, 0)),
                  pl.BlockSpec((DEC_SEQ, A_KV), lambda b: (row0 + b, A_Q // A_KV)),
                  tab(A_Q), tab(A_Q), tab(A_KV), tab(A_KV)],
        out_specs=[pl.BlockSpec((DEC_SEQ, A_Q), lambda b: (b, 0)),
                   pl.BlockSpec((DEC_SEQ, A_KV), lambda b: (b, 0))],
        out_shape=[jax.ShapeDtypeStruct((T_LAT, A_Q), F32), jax.ShapeDtypeStruct((T_LAT, A_KV), F32)],
        compiler_params=_cparams("parallel"),
        name="rope",
    )(qkv, qkv, jnp.asarray(cq), jnp.asarray(sq), jnp.asarray(ck), jnp.asarray(sk))


WIN_QB = 256


def _win_attn_body(qraw_ref, qrot_ref, krot_ref, v_ref, kc_ref, vc_ref, sink_ref, o_ref):
    scale = HEAD_DIM ** -0.5
    sink = sink_ref[0]
    kc = kc_ref[0, 0]
    vc = vc_ref[0, 0]
    for qb in range(DEC_SEQ // WIN_QB):
        q0 = qb * WIN_QB
        lo = max(0, q0 - WINDOW)
        hi = min(DEC_SEQ, q0 + WIN_QB + WINDOW)
        s_loc = _dot_nt(qrot_ref[0, 0, q0:q0 + WIN_QB, :], krot_ref[0, 0, lo:hi, :]) * scale
        qpos = q0 + lax.broadcasted_iota(jnp.int32, s_loc.shape, 0)
        kpos = lo + lax.broadcasted_iota(jnp.int32, s_loc.shape, 1)
        s_loc = jnp.where(jnp.abs(kpos - qpos) <= WINDOW, s_loc, MASK_NEG)
        s_ctx = _dot_nt(qraw_ref[0, 0, q0:q0 + WIN_QB, :], kc) * scale
        m = jnp.maximum(jnp.maximum(jnp.max(s_loc, axis=-1, keepdims=True),
                                    jnp.max(s_ctx, axis=-1, keepdims=True)), sink)
        p_loc = jnp.exp(s_loc - m)
        p_ctx = jnp.exp(s_ctx - m)
        den = (jnp.sum(p_loc, axis=-1, keepdims=True) + jnp.sum(p_ctx, axis=-1, keepdims=True)
               + jnp.exp(sink - m))
        o_ref[0, 0, q0:q0 + WIN_QB, :] = (_dot(p_ctx, vc) + _dot(p_loc, v_ref[0, 0, lo:hi, :])) / den


def _win_attn(qraw, qrot, krot, v, kc, vc, sink):
    qs = pl.BlockSpec((1, 1, DEC_SEQ, HEAD_DIM), lambda b, h: (b, h, 0, 0))
    ks = pl.BlockSpec((1, 1, DEC_SEQ, HEAD_DIM), lambda b, h: (b, h // A_GROUP, 0, 0))
    cs = pl.BlockSpec((1, 1, PAST_LEN, HEAD_DIM), lambda b, h: (b, h // A_GROUP, 0, 0))
    return pl.pallas_call(
        _win_attn_body,
        grid=(DEC_BATCH, A_HEADS),
        in_specs=[qs, qs, ks, ks, cs, cs, pl.BlockSpec((1, 1, 1), lambda b, h: (h, 0, 0))],
        out_specs=qs,
        out_shape=jax.ShapeDtypeStruct((DEC_BATCH, A_HEADS, DEC_SEQ, HEAD_DIM), F32),
        compiler_params=_cparams("parallel", "parallel"),
        name="win_attn",
    )(qraw, qrot, krot, v, kc, vc, sink.reshape(A_HEADS, 1, 1))


GRID_ROWS = DEC_SEQ // GRID_W
NA_BAND = min(NA_ROWS, GRID_ROWS)


NA_REL_ROWS = 2 * NA_ROWS - 1
NA_REL_COLS = 2 * NA_COLS - 1
LANES = 128


def _na_rel_rows(rpb):
    pad = jnp.zeros((B_HEADS, NA_REL_ROWS, GRID_W - NA_REL_COLS), F32)
    one = jnp.concatenate([rpb, pad], axis=-1)
    nxt = jnp.concatenate([one[:, 1:], jnp.zeros((B_HEADS, 1, GRID_W), F32)], axis=1)
    both = jnp.concatenate([one, nxt], axis=-1)
    return jnp.concatenate([both, jnp.zeros((B_HEADS, 16 - NA_REL_ROWS, LANES), F32)], axis=1)


def _na_attn_body(q_ref, k_ref, v_ref, kc_ref, vc_ref, rel_ref, o_ref):
    scale = HEAD_DIM ** -0.5
    kc = kc_ref[0, 0]
    vc = vc_ref[0, 0]
    cq = lax.broadcasted_iota(jnp.int32, (GRID_W, LANES), 0)
    kcol = lax.broadcasted_iota(jnp.int32, (GRID_W, LANES), 1) & (GRID_W - 1)
    cs = jnp.clip(cq - NA_COLS // 2, 0, GRID_W - NA_COLS)
    col_ok = (kcol >= cs) & (kcol < cs + NA_COLS)
    tiles = {}

    def pair_tile(a):
        if a not in tiles:
            x = jnp.broadcast_to(rel_ref[0, a:a + 1, :], (GRID_W, LANES))
            t = pltpu.roll(x, LANES - (NA_COLS - 1), axis=1, stride=1, stride_axis=0)
            tiles[a] = jnp.where(col_ok, t, MASK_NEG)
        return tiles[a]

    for r in range(GRID_ROWS):
        rs = min(max(r - NA_ROWS // 2, 0), GRID_ROWS - NA_BAND)
        a0 = rs - r + NA_ROWS - 1
        bias = jnp.concatenate([pair_tile(a0 + 2 * i) for i in range(NA_BAND // 2)], axis=1)
        q = q_ref[0, 0, r * GRID_W:(r + 1) * GRID_W, :]
        kb = k_ref[0, 0, rs * GRID_W:(rs + NA_BAND) * GRID_W, :]
        vb = v_ref[0, 0, rs * GRID_W:(rs + NA_BAND) * GRID_W, :]
        s_loc = _dot_nt(q, kb) * scale + bias
        s_ctx = _dot_nt(q, kc) * scale
        m = jnp.maximum(jnp.max(s_loc, axis=-1, keepdims=True), jnp.max(s_ctx, axis=-1, keepdims=True))
        p_loc = jnp.exp(s_loc - m)
        p_ctx = jnp.exp(s_ctx - m)
        den = jnp.sum(p_loc, axis=-1, keepdims=True) + jnp.sum(p_ctx, axis=-1, keepdims=True)
        o_ref[0, 0, r * GRID_W:(r + 1) * GRID_W, :] = (_dot(p_ctx, vc) + _dot(p_loc, vb)) / den


def _na_attn(q, k, v, kc, vc, rel):
    qs = pl.BlockSpec((1, 1, DEC_SEQ, HEAD_DIM), lambda b, h: (b, h, 0, 0))
    cs = pl.BlockSpec((1, 1, PAST_LEN, HEAD_DIM), lambda b, h: (b, h, 0, 0))
    return pl.pallas_call(
        _na_attn_body,
        grid=(DEC_BATCH, B_HEADS),
        in_specs=[qs, qs, qs, cs, cs, pl.BlockSpec((1, 16, LANES), lambda b, h: (h, 0, 0))],
        out_specs=qs,
        out_shape=jax.ShapeDtypeStruct((DEC_BATCH, B_HEADS, DEC_SEQ, HEAD_DIM), F32),
        compiler_params=_cparams("parallel", "parallel"),
        name="na_attn",
    )(q, k, v, kc, vc, rel)


@functools.lru_cache(maxsize=None)
def _dft_mats(L):
    n = 2 * L
    fc = min(L, DFT_CHUNK)
    f = np.arange(L)[:, None]
    t = np.arange(L)[None, :]
    ang = 2.0 * np.pi * ((f * t) % n) / n
    m1 = np.cos(ang)
    m2 = np.sin(ang)
    m2[0, :] = np.where(np.arange(L) % 2 == 0, 1.0, -1.0)
    wgt = np.full((L, 1), 2.0)
    wgt[0, 0] = 1.0
    nch = L // fc
    fwd = np.concatenate([m1.reshape(nch, fc, L), m2.reshape(nch, fc, L)], axis=1)
    inv = np.concatenate([(m1 * wgt / n).reshape(nch, fc, L), (m2 * wgt / n).reshape(nch, fc, L)], axis=1)
    inv = np.transpose(inv, (0, 2, 1))
    return fwd.astype(np.float32), inv.astype(np.float32)


@functools.lru_cache(maxsize=None)
def _filter_consts(L):
    t = np.linspace(0.0, 1.0, L)[:, None]
    bands = (C_EMB - 1) // 2
    ang = (2.0 * math.pi / L) * np.arange(L)[:, None] * np.linspace(1e-4, bands - 1, bands)[None, :]
    z = np.concatenate([t, np.cos(ang), -np.sin(ang)], axis=-1)
    zpad = np.zeros((L, 128))
    zpad[:, :C_EMB] = z
    deltas = np.abs(np.linspace(HYENA_MIN_DECAY, HYENA_MAX_DECAY, C_DIM))
    window = np.exp(-t * deltas[None, :])
    return zpad.astype(np.float32), window.astype(np.float32)


def _filter_body(z_ref, w1_ref, b1_ref, w2_ref, b2_ref, w3_ref, b3_ref, fr_ref, w4_ref, win_ref, fm_ref,
                 hr_ref, g_ref, hq_ref, hs_scr, hd_scr):
    c = pl.program_id(0)
    fc = hr_ref.shape[0]

    @pl.when(c == 0)
    def _():
        fr = fr_ref[...]
        hh = jnp.sin(fr * (_dot_hi(z_ref[...], w1_ref[...]) + b1_ref[...]))
        hh = jnp.sin(fr * (_dot_hi(hh, w2_ref[...]) + b2_ref[...]))
        hh = jnp.sin(fr * (_dot_hi(hh, w3_ref[...]) + b3_ref[...]))
        hh = _dot_hi(hh, w4_ref[...])
        hf = hh[:, :C_DIM] * win_ref[...]
        hb = hh[:, C_DIM:] * win_ref[...]
        hs_scr[...] = hf + hb
        hd_scr[...] = hf - hb

    fm = fm_ref[0]
    hr = _dot_hi(fm[:fc], hs_scr[...])
    first = (lax.broadcasted_iota(jnp.int32, (fc, C_DIM), 0) == 0) & (c == 0)
    hr_ref[...] = hr
    g_ref[...] = jnp.where(first, 0.0, _dot_hi(fm[fc:], hd_scr[...]))
    hs = hs_scr[...]
    sign = jnp.where((lax.broadcasted_iota(jnp.int32, hs.shape, 0) & 1) == 0, 1.0, -1.0)
    hq_ref[...] = jnp.where(first, jnp.sum(hs * sign, axis=0, keepdims=True), hr)


def _hyena_filter(L, filt):
    w1, b1, w2, b2, w3, b3, freq, w4 = filt
    zpad, window = _filter_consts(L)
    fwd, _ = _dft_mats(L)
    nch, fc2, _ = fwd.shape
    fc = fc2 // 2
    w1p = jnp.pad(w1, ((0, 128 - C_EMB), (0, 0)))
    full = lambda shape: pl.BlockSpec(shape, lambda c: tuple(0 for _ in shape))
    out_spec = pl.BlockSpec((fc, C_DIM), lambda c: (c, 0))
    out_sd = jax.ShapeDtypeStruct((L, C_DIM), F32)
    return pl.pallas_call(
        _filter_body,
        grid=(nch,),
        in_specs=[full((L, 128)), full((128, C_FFN)), full((1, C_FFN)), full((C_FFN, C_FFN)), full((1, C_FFN)),
                  full((C_FFN, C_FFN)), full((1, C_FFN)), full((1, C_FFN)), full((C_FFN, 2 * C_DIM)),
                  full((L, C_DIM)), pl.BlockSpec((1, fc2, L), lambda c: (c, 0, 0))],
        out_specs=[out_spec, out_spec, out_spec],
        out_shape=[out_sd, out_sd, out_sd],
        scratch_shapes=[pltpu.VMEM((L, C_DIM), F32), pltpu.VMEM((L, C_DIM), F32)],
        compiler_params=_cparams("arbitrary"),
        name="hyena_filter",
    )(jnp.asarray(zpad), w1p, b1.reshape(1, C_FFN), w2, b2.reshape(1, C_FFN), w3, b3.reshape(1, C_FFN),
      freq.reshape(1, C_FFN), w4, jnp.asarray(window), jnp.asarray(fwd))


def _hyena_body(u_ref, cw_ref, cb_ref, d_ref, fm_ref, fi_ref, hr_ref, g_ref, hq_ref, y_ref,
                x0_scr, z_scr, acc_scr):
    c = pl.program_id(1)
    L = y_ref.shape[0]
    fc = hr_ref.shape[0]

    @pl.when(c == 0)
    def _():
        row = lax.broadcasted_iota(jnp.int32, (L, C_DIM), 0)

        def short_conv(sec):
            cols = slice(sec * C_DIM, (sec + 1) * C_DIM)
            u = u_ref[:, cols]
            prev = jnp.where(row == 0, 0.0, pltpu.roll(u, 1, axis=0))
            nxt = jnp.where(row == L - 1, 0.0, pltpu.roll(u, L - 1, axis=0))
            return (prev * cw_ref[0:1, cols] + u * cw_ref[1:2, cols] + nxt * cw_ref[2:3, cols]
                    + cb_ref[:, cols])

        x0_scr[...] = short_conv(0)
        z_scr[...] = short_conv(1) * short_conv(2)
        acc_scr[...] = jnp.zeros((L, C_DIM), F32)

    ab = _dot_hi(fm_ref[0], z_scr[...])
    a, b = ab[:fc], ab[fc:]
    hr, g, hq = hr_ref[...], g_ref[...], hq_ref[...]
    pq = jnp.concatenate([a * hr - b * g, a * g + b * hq], axis=0)
    acc_scr[...] += _dot_hi(fi_ref[0], pq)

    @pl.when(c == pl.num_programs(1) - 1)
    def _():
        y_ref[...] = x0_scr[...] * (acc_scr[...] + z_scr[...] * d_ref[...])


def _hyena(u, row_blk0, n_seq, L, conv_w, conv_b, d_skip, spec):
    hr, g, hq = spec
    fwd, inv = _dft_mats(L)
    nch, fc2, _ = fwd.shape
    fc = fc2 // 2
    u_w = 3 * C_DIM
    return pl.pallas_call(
        _hyena_body,
        grid=(n_seq, nch),
        in_specs=[pl.BlockSpec((L, u_w), lambda b, c: (row_blk0 + b, 0)),
                  pl.BlockSpec((3, u_w), lambda b, c: (0, 0)),
                  pl.BlockSpec((1, u_w), lambda b, c: (0, 0)),
                  pl.BlockSpec((1, C_DIM), lambda b, c: (0, 0)),
                  pl.BlockSpec((1, fc2, L), lambda b, c: (c, 0, 0)),
                  pl.BlockSpec((1, L, fc2), lambda b, c: (c, 0, 0)),
                  pl.BlockSpec((fc, C_DIM), lambda b, c: (c, 0)),
                  pl.BlockSpec((fc, C_DIM), lambda b, c: (c, 0)),
                  pl.BlockSpec((fc, C_DIM), lambda b, c: (c, 0))],
        out_specs=pl.BlockSpec((L, C_DIM), lambda b, c: (b, 0)),
        out_shape=jax.ShapeDtypeStruct((n_seq * L, C_DIM), F32),
        scratch_shapes=[pltpu.VMEM((L, C_DIM), F32)] * 3,
        compiler_params=_cparams("parallel", "arbitrary"),
        name="hyena",
    )(u, conv_w, conv_b.reshape(1, u_w), d_skip.reshape(1, C_DIM), jnp.asarray(fwd), jnp.asarray(inv), hr, g, hq)


def _hgrn_body(q_ref, ff_ref, fb_ref, i_ref, g_ref, lbf_ref, lbb_ref, nd_ref, s0f_ref, s0b_ref,
               o_ref, sf_ref, sb_ref, o_scr, *, layer):
    L = o_ref.shape[0]
    C = GLA_CHUNK
    nc = L // C
    mid = C // 2
    q = _silu(q_ref[...])
    v = i_ref[...]

    def lower_bound(ref):
        gm = ref[...]
        e = jnp.exp(gm - jnp.max(gm, axis=0, keepdims=True))
        p = e / jnp.sum(e, axis=0, keepdims=True)
        return jnp.sum(p[0:layer + 1], axis=0, keepdims=True) - p[0:1]

    def gates(fx, lb):
        f = lb + (1.0 - lb) * jax.nn.sigmoid(fx)
        return 1.0 - f, jnp.log(f)

    kf, lgf = gates(ff_ref[...], lower_bound(lbf_ref))
    kb, lgb = gates(fb_ref[...], lower_bound(lbb_ref))
    ti = lax.broadcasted_iota(jnp.int32, (C, C), 0)
    si = lax.broadcasted_iota(jnp.int32, (C, C), 1)
    causal = si <= ti
    anti = si >= ti
    tril = causal.astype(F32)
    triu = anti.astype(F32)

    st = jnp.transpose(s0f_ref[0, 0])
    for n in range(nc):
        sl = slice(n * C, (n + 1) * C)
        b = _dot_hi(tril, lgf[sl])
        btot = b[C - 1:C]
        ref = b[mid:mid + 1]
        qc, kc, vc = q[sl], kf[sl], v[sl]
        sc = jnp.where(causal, _dot_nt(qc * jnp.exp(b - ref), kc * jnp.exp(ref - b)), 0.0)
        o_scr[sl, :] = _dot(sc, vc) + _dot_nt(qc * jnp.exp(b), st)
        st = st * jnp.exp(btot) + _dot_tn(vc, kc * jnp.exp(btot - b))
    sf_ref[0, 0] = jnp.transpose(st)

    st = jnp.transpose(s0b_ref[0, 0])
    for n in reversed(range(nc)):
        sl = slice(n * C, (n + 1) * C)
        b = _dot_hi(triu, lgb[sl])
        btot = b[0:1]
        ref = b[mid:mid + 1]
        qc, kc, vc = q[sl], kb[sl], v[sl]
        sc = jnp.where(anti, _dot_nt(qc * jnp.exp(b - ref), kc * jnp.exp(ref - b)), 0.0)
        o_scr[sl, :] += _dot(sc, vc) + _dot_nt(qc * jnp.exp(b), st)
        st = st * jnp.exp(btot) + _dot_tn(vc, kc * jnp.exp(btot - b))
    sb_ref[0, 0] = jnp.transpose(st)

    o_ref[...] = _rms(o_scr[...], nd_ref[...]) * _silu(g_ref[...])


def _hgrn(u, row_blk0, n_seq, L, lb_fwd, lb_bwd, norm_d, s0f, s0b, layer):
    col0 = 3 * C_DIM // D_KDIM
    col = lambda j: pl.BlockSpec((L, D_KDIM), lambda b, h: (row_blk0 + b, col0 + j * D_HEADS + h))
    lbs = pl.BlockSpec((DEPTH, D_KDIM), lambda b, h: (0, h))
    st = pl.BlockSpec((1, 1, D_KDIM, D_VDIM), lambda b, h: (b, h, 0, 0))
    st_sd = jax.ShapeDtypeStruct((n_seq, D_HEADS, D_KDIM, D_VDIM), F32)
    return pl.pallas_call(
        functools.partial(_hgrn_body, layer=layer),
        grid=(n_seq, D_HEADS),
        in_specs=[col(0), col(1), col(2), col(3), col(4), lbs, lbs,
                  pl.BlockSpec((1, D_VDIM), lambda b, h: (0, 0)), st, st],
        out_specs=[pl.BlockSpec((L, D_VDIM), lambda b, h: (b, h)), st, st],
        out_shape=[jax.ShapeDtypeStruct((n_seq * L, D_HEADS * D_VDIM), F32), st_sd, st_sd],
        scratch_shapes=[pltpu.VMEM((L, D_VDIM), F32)],
        compiler_params=_cparams("parallel", "parallel"),
        name="hgrn",
    )(u, u, u, u, u, lb_fwd, lb_bwd, norm_d.reshape(1, D_VDIM), s0f, s0b)


def _outproj_body(a_ref, b_ref, x_ref, mod_ref, gf_ref, w_ref, wr_ref, rb_ref, x1_ref, h2_ref, gate_ref):
    m = mod_ref[0]
    half = a_ref.shape[1]
    out = _dot(a_ref[...], w_ref[0:half, :]) + _dot(b_ref[...], w_ref[half:, :])
    x1 = x_ref[...] + m[:, 2 * D_MODEL:3 * D_MODEL] * out
    x1_ref[...] = x1
    h2 = _rms(x1, gf_ref[...]) * (1.0 + m[:, 4 * D_MODEL:5 * D_MODEL]) + m[:, 3 * D_MODEL:4 * D_MODEL]
    h2_ref[...] = h2.astype(BF16)
    scores = jax.nn.sigmoid(_dot_hi(h2, wr_ref[...]))
    work = scores + rb_ref[...]
    lane = lax.broadcasted_iota(jnp.int32, work.shape, 1).astype(F32)
    chosen = jnp.zeros(work.shape, jnp.bool_)
    for _ in range(TOP_K):
        best = jnp.max(work, axis=-1, keepdims=True)
        first = jnp.min(jnp.where(work == best, lane, float(N_EXPERTS)), axis=-1, keepdims=True)
        hit = lane == first
        chosen = chosen | hit
        work = jnp.where(hit, -jnp.inf, work)
    g = jnp.where(chosen, scores, 0.0)
    gate_ref[...] = g / jnp.sum(g, axis=-1, keepdims=True) * ROUTE_SCALE


def _outproj(a, b, x, mod_l, gain_ffn, w_out_bf16, w_router, router_bias):
    half = a.shape[1]
    return pl.pallas_call(
        _outproj_body,
        grid=(T_ALL // TM,),
        in_specs=[pl.BlockSpec((TM, half), lambda i: (i, 0)),
                  pl.BlockSpec((TM, half), lambda i: (i, 0)),
                  pl.BlockSpec((TM, D_MODEL), lambda i: (i, 0)),
                  pl.BlockSpec((1, 1, N_MOD * D_MODEL), lambda i: (_mod_row(i), 0, 0)),
                  pl.BlockSpec((1, D_MODEL), lambda i: (0, 0)),
                  pl.BlockSpec((2 * half, D_MODEL), lambda i: (0, 0)),
                  pl.BlockSpec((D_MODEL, N_EXPERTS), lambda i: (0, 0)),
                  pl.BlockSpec((1, N_EXPERTS), lambda i: (0, 0))],
        out_specs=[pl.BlockSpec((TM, D_MODEL), lambda i: (i, 0)),
                   pl.BlockSpec((TM, D_MODEL), lambda i: (i, 0)),
                   pl.BlockSpec((TM, N_EXPERTS), lambda i: (i, 0))],
        out_shape=[jax.ShapeDtypeStruct((T_ALL, D_MODEL), F32),
                   jax.ShapeDtypeStruct((T_ALL, D_MODEL), BF16),
                   jax.ShapeDtypeStruct((T_ALL, N_EXPERTS), F32)],
        compiler_params=_cparams("parallel"),
        name="outproj_router",
    )(a, b, x, mod_l, gain_ffn.reshape(1, D_MODEL), w_out_bf16, w_router, router_bias.reshape(1, N_EXPERTS))


def _moe_body(h_ref, gate_ref, x1_ref, mod_ref, wg_ref, wu_ref, wd_ref, sg_ref, su_ref, sd_ref, fn_ref,
              o_ref, acc_scr, *, final):
    e = pl.program_id(1)
    h = h_ref[...]

    def hidden(wg, wu):
        return _silu(_dot(h, wg)) * _dot(h, wu)

    @pl.when(e == 0)
    def _():
        acc_scr[...] = _dot(hidden(sg_ref[...], su_ref[...]), sd_ref[...])

    lane = lax.broadcasted_iota(jnp.int32, gate_ref.shape, 1)
    gcol = jnp.sum(jnp.where(lane == e, gate_ref[...], 0.0), axis=-1, keepdims=True)
    acc_scr[...] += _dot(hidden(wg_ref[0], wu_ref[0]) * gcol, wd_ref[0])

    @pl.when(e == N_EXPERTS - 1)
    def _():
        m = mod_ref[0]
        y = x1_ref[...] + m[:, 5 * D_MODEL:6 * D_MODEL] * acc_scr[...]
        o_ref[...] = _rms(y, fn_ref[...]) if final else y


def _moe(h2, gates, x1, mod_l, w_gate, w_up, w_down, ws_gate, ws_up, ws_down, final_norm, final):
    tm = MOE_TM
    tok = lambda shape: pl.BlockSpec(shape, lambda i, e: (i, 0))
    full = lambda shape: pl.BlockSpec(shape, lambda i, e: (0, 0))
    mod_spec = pl.BlockSpec((1, 1, N_MOD * D_MODEL), lambda i, e: (_mod_row(i * (tm // TM)), 0, 0))
    return pl.pallas_call(
        functools.partial(_moe_body, final=final),
        grid=(T_ALL // tm, N_EXPERTS),
        in_specs=[tok((tm, D_MODEL)), tok((tm, N_EXPERTS)), tok((tm, D_MODEL)), mod_spec,
                  pl.BlockSpec((1, D_MODEL, D_EXPERT), lambda i, e: (e, 0, 0)),
                  pl.BlockSpec((1, D_MODEL, D_EXPERT), lambda i, e: (e, 0, 0)),
                  pl.BlockSpec((1, D_EXPERT, D_MODEL), lambda i, e: (e, 0, 0)),
                  full((D_MODEL, D_EXPERT)), full((D_MODEL, D_EXPERT)), full((D_EXPERT, D_MODEL)),
                  full((1, D_MODEL))],
        out_specs=tok((tm, D_MODEL)),
        out_shape=jax.ShapeDtypeStruct((T_ALL, D_MODEL), F32),
        scratch_shapes=[pltpu.VMEM((tm, D_MODEL), F32)],
        compiler_params=_cparams("parallel", "arbitrary"),
        name="moe",
    )(h2, gates, x1, mod_l, w_gate, w_up, w_down, ws_gate, ws_up, ws_down, final_norm.reshape(1, D_MODEL))


def _heads_major(t, n_heads):
    return t.reshape(DEC_BATCH, DEC_SEQ, n_heads, HEAD_DIM).transpose(0, 2, 1, 3)


def _tokens_major(t):
    return t.transpose(0, 2, 1, 3).reshape(T_LAT, -1)


def kernel(x_prompt, x_sample, cache_a_k, cache_a_v, cache_b_k, cache_b_v, state_d_fwd, state_d_bwd, c, c_ctx, w_ada, b_ada, norm_mix, norm_ffn, w_in_attn, w_out_attn, sink_a, rpb_b, w_in_rec, w_out_rec, conv_w, conv_b, filt_w1, filt_b1, filt_w2, filt_b2, filt_w3, filt_b3, filt_freq, filt_w4, d_skip, lb_fwd, lb_bwd, norm_d, w_router, router_bias, w_gate, w_up, w_down, ws_gate, ws_up, ws_down, final_norm):
    x = jnp.concatenate([x_prompt.reshape(T_CTX, D_MODEL), x_sample.reshape(T_LAT, D_MODEL)], axis=0)
    cvec = jnp.concatenate([c_ctx[None, :], c, jnp.zeros((CVEC_PAD - N_CVEC, D_MODEL), F32)], axis=0)
    mod = _ada(cvec.T, w_ada, b_ada).reshape(DEPTH, CVEC_PAD, 1, N_MOD * D_MODEL)

    new_kv = None
    new_state = None
    for l in range(DEPTH):
        j = l // 2
        final = l == DEPTH - 1
        if l % 2 == 0:
            qkv = _inproj(x, mod[l], norm_mix[l], w_in_attn[j].astype(BF16))
            lat = qkv[T_CTX:]
            o_ctx = _ctx_attn(qkv, sink_a[j])
            kv_cols = lambda lo, heads: qkv[:T_CTX, lo:lo + heads * HEAD_DIM].reshape(BATCH, 1, SEQ, heads, HEAD_DIM)
            new_kv = (kv_cols(A_Q, A_KV_HEADS), kv_cols(A_Q + A_KV, A_KV_HEADS),
                      kv_cols(A_Q + 2 * A_KV + B_W, B_HEADS), kv_cols(A_Q + 2 * A_KV + 2 * B_W, B_HEADS))
            base = A_Q + 2 * A_KV
            q_rot, k_rot = _rope(qkv)
            hm = _heads_major
            cache_hm = lambda t: t[:, j].transpose(0, 2, 1, 3)
            o_a = _win_attn(hm(lat[:, :A_Q], A_HEADS), hm(q_rot, A_HEADS), hm(k_rot, A_KV_HEADS),
                            hm(lat[:, A_Q + A_KV:base], A_KV_HEADS), cache_hm(cache_a_k), cache_hm(cache_a_v),
                            sink_a[j])
            o_b = _na_attn(hm(lat[:, base:base + B_W], B_HEADS), hm(lat[:, base + B_W:base + 2 * B_W], B_HEADS),
                           hm(lat[:, base + 2 * B_W:], B_HEADS), cache_hm(cache_b_k), cache_hm(cache_b_v),
                           _na_rel_rows(rpb_b[j]))
            mix_a = jnp.concatenate([o_ctx[:, :A_Q], _tokens_major(o_a)], axis=0)
            mix_b = jnp.concatenate([o_ctx[:, A_Q:], _tokens_major(o_b)], axis=0)
            w_out = w_out_attn[j]
        else:
            u = _inproj(x, mod[l], norm_mix[l], w_in_rec[j].astype(BF16))
            filt = (filt_w1[j], filt_b1[j], filt_w2[j], filt_b2[j], filt_w3[j], filt_b3[j], filt_freq[j],
                    filt_w4[j])
            y_ctx = _hyena(u, 0, BATCH, SEQ, conv_w[j], conv_b[j], d_skip[j], _hyena_filter(SEQ, filt))
            y_lat = _hyena(u, T_CTX // DEC_SEQ, DEC_BATCH, DEC_SEQ, conv_w[j], conv_b[j], d_skip[j],
                           _hyena_filter(DEC_SEQ, filt))
            zeros = jnp.zeros((BATCH, D_HEADS, D_KDIM, D_VDIM), F32)
            o_ctx, s_f, s_b = _hgrn(u, 0, BATCH, SEQ, lb_fwd, lb_bwd, norm_d[j], zeros, zeros, l)
            o_lat, _, _ = _hgrn(u, T_CTX // DEC_SEQ, DEC_BATCH, DEC_SEQ, lb_fwd, lb_bwd, norm_d[j],
                                state_d_fwd[:, j], state_d_bwd[:, j], l)
            new_state = (s_f[:, None], s_b[:, None])
            mix_a = jnp.concatenate([y_ctx, y_lat], axis=0)
            mix_b = jnp.concatenate([o_ctx, o_lat], axis=0)
            w_out = w_out_rec[j]
        x1, h2, gates = _outproj(mix_a, mix_b, x, mod[l], norm_ffn[l], w_out.astype(BF16), w_router[l],
                                 router_bias[l])
        x = _moe(h2, gates, x1, mod[l], w_gate[l], w_up[l], w_down[l], ws_gate[l], ws_up[l], ws_down[l],
                 final_norm, final)

    y_prompt = x[:T_CTX].reshape(BATCH, SEQ, D_MODEL)
    y_sample = x[T_CTX:].reshape(DEC_BATCH, DEC_SEQ, D_MODEL)
    return (y_prompt, y_sample) + new_kv + new_state
```

```python
import functools
import math

import numpy as np
import jax
import jax.numpy as jnp
from jax import lax
from jax.experimental import pallas as pl
from jax.experimental.pallas import tpu as pltpu
from jax.experimental.pallas import tpu_sc as plsc

F32 = jnp.float32
BF16 = jnp.bfloat16
HI = lax.Precision.HIGHEST

D_MODEL = 1024
BATCH = 16
SEQ = 256
DEPTH = 2
DEC_BATCH = 2
DEC_SEQ = 1024
PAST_LEN = 512
GRID_W = 64
HEAD_DIM = 64
N_MOD = 6
RMS_EPS = 1e-6
A_HEADS = 8
A_KV_HEADS = 2
A_GROUP = A_HEADS // A_KV_HEADS
WINDOW = 128
ROPE_BASE = 10000.0
B_HEADS = 8
NA_ROWS = 8
NA_COLS = 16
C_DIM = 512
C_EMB = 33
C_FFN = 64
HYENA_MIN_DECAY = math.log(1e-2) / 1.5
HYENA_MAX_DECAY = math.log(1e-2) / 0.3
D_KDIM = 128
D_VDIM = 128
D_HEADS = 4
N_EXPERTS = 64
TOP_K = 8
D_EXPERT = 256
ROUTE_SCALE = 2.5
A_Q = A_HEADS * HEAD_DIM
A_KV = A_KV_HEADS * HEAD_DIM
B_W = B_HEADS * HEAD_DIM
ATTN_IN = A_Q + 2 * A_KV + 3 * B_W
REC_IN = 3 * C_DIM + 5 * D_HEADS * D_KDIM

T_CTX = BATCH * SEQ
T_LAT = DEC_BATCH * DEC_SEQ
T_ALL = T_CTX + T_LAT
N_CVEC = 1 + DEC_BATCH
CVEC_PAD = 8
TM = 256
MASK_NEG = -1e30
GLA_CHUNK = 64
DFT_CHUNK = 256
MOE_BLK = 256
MOE_NBLK = -(-(T_ALL * TOP_K + N_EXPERTS * (MOE_BLK - 1)) // MOE_BLK)
MOE_ROWS = MOE_NBLK * MOE_BLK
SC_CORES = 2
SC_SUBCORES = 16
SC_WORKERS = SC_CORES * SC_SUBCORES
DISP_CHUNK = 128
COLLECT_CHUNK = 64
VMEM_LIMIT = 56 * 1024 * 1024


def _cparams(*sem):
    return pltpu.CompilerParams(dimension_semantics=sem, vmem_limit_bytes=VMEM_LIMIT)


def _mod_row(i):
    return jnp.where(i < T_CTX // TM, 0, 1 + (i - T_CTX // TM) // (DEC_SEQ // TM))


def _dot(a, b):
    return jnp.dot(a.astype(BF16), b.astype(BF16), preferred_element_type=F32)


def _dot_nt(a, b):
    return lax.dot_general(a.astype(BF16), b.astype(BF16), (((1,), (1,)), ((), ())),
                           preferred_element_type=F32)


def _dot_tn(a, b):
    return lax.dot_general(a.astype(BF16), b.astype(BF16), (((0,), (0,)), ((), ())),
                           preferred_element_type=F32)


def _dot_hi(a, b):
    return jnp.dot(a, b, precision=HI, preferred_element_type=F32)


def _silu(x):
    return x * jax.nn.sigmoid(x)


def _rms(x, g):
    return x * lax.rsqrt(jnp.mean(x * x, axis=-1, keepdims=True) + RMS_EPS) * g


def _ada_body(ct_ref, w_ref, b_ref, o_ref):
    tn = o_ref.shape[-1]

    def step(k8, accs):
        r0 = pl.multiple_of(k8 * 8, 8)
        wk = w_ref[0, pl.ds(r0, 8), :]
        sk = _silu(ct_ref[pl.ds(r0, 8), :])
        return tuple(acc + wk * sk[:, j:j + 1] for j, acc in enumerate(accs))

    accs = lax.fori_loop(0, D_MODEL // 8, step, tuple(jnp.zeros((8, tn), F32) for _ in range(N_CVEC)))
    o_ref[0] = jnp.zeros((CVEC_PAD, tn), F32)
    for j in range(N_CVEC):
        o_ref[0, j:j + 1, :] = jnp.sum(accs[j], axis=0, keepdims=True) + b_ref[0]


def _ada(cvec_t, w_ada, b_ada):
    tn = 1536
    n_out = N_MOD * D_MODEL
    return pl.pallas_call(
        _ada_body,
        grid=(DEPTH, n_out // tn),
        in_specs=[pl.BlockSpec((D_MODEL, CVEC_PAD), lambda l, n: (0, 0)),
                  pl.BlockSpec((1, D_MODEL, tn), lambda l, n: (l, 0, n)),
                  pl.BlockSpec((1, 1, tn), lambda l, n: (l, 0, n))],
        out_specs=pl.BlockSpec((1, CVEC_PAD, tn), lambda l, n: (l, 0, n)),
        out_shape=jax.ShapeDtypeStruct((DEPTH, CVEC_PAD, n_out), F32),
        compiler_params=_cparams("parallel", "parallel"),
        name="ada",
    )(cvec_t, w_ada, b_ada.reshape(DEPTH, 1, n_out))


def _inproj_body(x_ref, mod_ref, g_ref, w_ref, o_ref):
    m = mod_ref[0]
    h = _rms(x_ref[...], g_ref[...]) * (1.0 + m[:, D_MODEL:2 * D_MODEL]) + m[:, 0:D_MODEL]
    o_ref[...] = _dot(h, w_ref[...])


def _inproj(x, mod_l, gain, w_bf16):
    n = w_bf16.shape[1]
    return pl.pallas_call(
        _inproj_body,
        grid=(T_ALL // TM,),
        in_specs=[pl.BlockSpec((TM, D_MODEL), lambda i: (i, 0)),
                  pl.BlockSpec((1, 1, N_MOD * D_MODEL), lambda i: (_mod_row(i), 0, 0)),
                  pl.BlockSpec((1, D_MODEL), lambda i: (0, 0)),
                  pl.BlockSpec((D_MODEL, n), lambda i: (0, 0))],
        out_specs=pl.BlockSpec((TM, n), lambda i: (i, 0)),
        out_shape=jax.ShapeDtypeStruct((T_ALL, n), F32),
        compiler_params=_cparams("parallel"),
        name="inproj",
    )(x, mod_l, gain.reshape(1, D_MODEL), w_bf16)


def _ctx_attn_body(qkv_ref, sink_ref, o_ref):
    scale = HEAD_DIM ** -0.5

    def head(q, k, v, sink):
        s = _dot_nt(q, k) * scale
        m = jnp.max(s, axis=-1, keepdims=True)
        if sink is not None:
            m = jnp.maximum(m, sink)
        p = jnp.exp(s - m)
        den = jnp.sum(p, axis=-1, keepdims=True)
        if sink is not None:
            den = den + jnp.exp(sink - m)
        return _dot(p, v) / den

    for h in range(A_HEADS):
        hk = h // A_GROUP
        q = qkv_ref[:, h * HEAD_DIM:(h + 1) * HEAD_DIM]
        k = qkv_ref[:, A_Q + hk * HEAD_DIM:A_Q + (hk + 1) * HEAD_DIM]
        v = qkv_ref[:, A_Q + A_KV + hk * HEAD_DIM:A_Q + A_KV + (hk + 1) * HEAD_DIM]
        o_ref[:, h * HEAD_DIM:(h + 1) * HEAD_DIM] = head(q, k, v, sink_ref[:, h:h + 1])
    base = A_Q + 2 * A_KV
    for h in range(B_HEADS):
        q = qkv_ref[:, base + h * HEAD_DIM:base + (h + 1) * HEAD_DIM]
        k = qkv_ref[:, base + B_W + h * HEAD_DIM:base + B_W + (h + 1) * HEAD_DIM]
        v = qkv_ref[:, base + 2 * B_W + h * HEAD_DIM:base + 2 * B_W + (h + 1) * HEAD_DIM]
        o_ref[:, A_Q + h * HEAD_DIM:A_Q + (h + 1) * HEAD_DIM] = head(q, k, v, None)


def _ctx_attn(qkv, sink):
    return pl.pallas_call(
        _ctx_attn_body,
        grid=(BATCH,),
        in_specs=[pl.BlockSpec((SEQ, ATTN_IN), lambda b: (b, 0)),
                  pl.BlockSpec((1, A_HEADS), lambda b: (0, 0))],
        out_specs=pl.BlockSpec((SEQ, A_Q + B_W), lambda b: (b, 0)),
        out_shape=jax.ShapeDtypeStruct((T_CTX, A_Q + B_W), F32),
        compiler_params=_cparams("parallel"),
        name="ctx_attn",
    )(qkv, sink.reshape(1, A_HEADS))


@functools.lru_cache(maxsize=None)
def _rope_tables(width):
    half = HEAD_DIM // 2
    t = np.arange(DEC_SEQ)
    inv = ROPE_BASE ** (-np.arange(0, half, 2, dtype=np.float64) / half)
    ang_r = (t // GRID_W)[:, None] * inv[None, :]
    ang_c = (t % GRID_W)[:, None] * inv[None, :]
    cos = np.concatenate([np.cos(ang_r)] * 2 + [np.cos(ang_c)] * 2, axis=-1)
    sin = np.concatenate([-np.sin(ang_r), np.sin(ang_r), -np.sin(ang_c), np.sin(ang_c)], axis=-1)
    reps = width // HEAD_DIM
    return (np.tile(cos, (1, reps)).astype(np.float32), np.tile(sin, (1, reps)).astype(np.float32))


def _rope_body(q_ref, k_ref, cq_ref, sq_ref, ck_ref, sk_ref, qo_ref, ko_ref):
    quarter = HEAD_DIM // 4

    def rot(x, cos, sin):
        w = x.shape[-1]
        lane = lax.broadcasted_iota(jnp.int32, x.shape, 1)
        fwd = pltpu.roll(x, w - quarter, axis=1)
        bwd = pltpu.roll(x, quarter, axis=1)
        partner = jnp.where((lane & (2 * quarter - 1)) < quarter, fwd, bwd)
        return x * cos + partner * sin

    qo_ref[...] = rot(q_ref[...], cq_ref[...], sq_ref[...])
    ko_ref[...] = rot(k_ref[...], ck_ref[...], sk_ref[...])


def _rope(qkv):
    cq, sq = _rope_tables(A_Q)
    ck, sk = _rope_tables(A_KV)
    tab = lambda w: pl.BlockSpec((DEC_SEQ, w), lambda b: (0, 0))
    row0 = T_CTX // DEC_SEQ
    return pl.pallas_call(
        _rope_body,
        grid=(DEC_BATCH,),
        in_specs=[pl.BlockSpec((DEC_SEQ, A_Q), lambda b: (row0 + b, 0)),
                  pl.BlockSpec((DEC_SEQ, A_KV), lambda b: (row0 + b, A_Q // A_KV)),
                  tab(A_Q), tab(A_Q), tab(A_KV), tab(A_KV)],
        out_specs=[pl.BlockSpec((DEC_SEQ, A_Q), lambda b: (b, 0)),
                   pl.BlockSpec((DEC_SEQ, A_KV), lambda b: (b, 0))],
        out_shape=[jax.ShapeDtypeStruct((T_LAT, A_Q), F32), jax.ShapeDtypeStruct((T_LAT, A_KV), F32)],
        compiler_params=_cparams("parallel"),
        name="rope",
    )(qkv, qkv, jnp.asarray(cq), jnp.asarray(sq), jnp.asarray(ck), jnp.asarray(sk))


WIN_QB = 256


def _win_attn_body(qraw_ref, qrot_ref, krot_ref, v_ref, kc_ref, vc_ref, sink_ref, o_ref):
    scale = HEAD_DIM ** -0.5
    sink = sink_ref[0]
    kc = kc_ref[0, 0]
    vc = vc_ref[0, 0]
    for qb in range(DEC_SEQ // WIN_QB):
        q0 = qb * WIN_QB
        lo = max(0, q0 - WINDOW)
        hi = min(DEC_SEQ, q0 + WIN_QB + WINDOW)
        s_loc = _dot_nt(qrot_ref[0, 0, q0:q0 + WIN_QB, :], krot_ref[0, 0, lo:hi, :]) * scale
        qpos = q0 + lax.broadcasted_iota(jnp.int32, s_loc.shape, 0)
        kpos = lo + lax.broadcasted_iota(jnp.int32, s_loc.shape, 1)
        s_loc = jnp.where(jnp.abs(kpos - qpos) <= WINDOW, s_loc, MASK_NEG)
        s_ctx = _dot_nt(qraw_ref[0, 0, q0:q0 + WIN_QB, :], kc) * scale
        m = jnp.maximum(jnp.maximum(jnp.max(s_loc, axis=-1, keepdims=True),
                                    jnp.max(s_ctx, axis=-1, keepdims=True)), sink)
        p_loc = jnp.exp(s_loc - m)
        p_ctx = jnp.exp(s_ctx - m)
        den = (jnp.sum(p_loc, axis=-1, keepdims=True) + jnp.sum(p_ctx, axis=-1, keepdims=True)
               + jnp.exp(sink - m))
        o_ref[0, 0, q0:q0 + WIN_QB, :] = (_dot(p_ctx, vc) + _dot(p_loc, v_ref[0, 0, lo:hi, :])) / den


def _win_attn(qraw, qrot, krot, v, kc, vc, sink):
    qs = pl.BlockSpec((1, 1, DEC_SEQ, HEAD_DIM), lambda b, h: (b, h, 0, 0))
    ks = pl.BlockSpec((1, 1, DEC_SEQ, HEAD_DIM), lambda b, h: (b, h // A_GROUP, 0, 0))
    cs = pl.BlockSpec((1, 1, PAST_LEN, HEAD_DIM), lambda b, h: (b, h // A_GROUP, 0, 0))
    return pl.pallas_call(
        _win_attn_body,
        grid=(DEC_BATCH, A_HEADS),
        in_specs=[qs, qs, ks, ks, cs, cs, pl.BlockSpec((1, 1, 1), lambda b, h: (h, 0, 0))],
        out_specs=qs,
        out_shape=jax.ShapeDtypeStruct((DEC_BATCH, A_HEADS, DEC_SEQ, HEAD_DIM), F32),
        compiler_params=_cparams("parallel", "parallel"),
        name="win_attn",
    )(qraw, qrot, krot, v, kc, vc, sink.reshape(A_HEADS, 1, 1))


GRID_ROWS = DEC_SEQ // GRID_W
NA_BAND = min(NA_ROWS, GRID_ROWS)


NA_REL_ROWS = 2 * NA_ROWS - 1
NA_REL_COLS = 2 * NA_COLS - 1
LANES = 128


def _na_rel_rows(rpb):
    pad = jnp.zeros((B_HEADS, NA_REL_ROWS, GRID_W - NA_REL_COLS), F32)
    one = jnp.concatenate([rpb, pad], axis=-1)
    nxt = jnp.concatenate([one[:, 1:], jnp.zeros((B_HEADS, 1, GRID_W), F32)], axis=1)
    both = jnp.concatenate([one, nxt], axis=-1)
    return jnp.concatenate([both, jnp.zeros((B_HEADS, 16 - NA_REL_ROWS, LANES), F32)], axis=1)


def _na_attn_body(q_ref, k_ref, v_ref, kc_ref, vc_ref, rel_ref, o_ref):
    scale = HEAD_DIM ** -0.5
    kc = kc_ref[0, 0]
    vc = vc_ref[0, 0]
    cq = lax.broadcasted_iota(jnp.int32, (GRID_W, LANES), 0)
    kcol = lax.broadcasted_iota(jnp.int32, (GRID_W, LANES), 1) & (GRID_W - 1)
    cs = jnp.clip(cq - NA_COLS // 2, 0, GRID_W - NA_COLS)
    col_ok = (kcol >= cs) & (kcol < cs + NA_COLS)
    tiles = {}

    def pair_tile(a):
        if a not in tiles:
            x = jnp.broadcast_to(rel_ref[0, a:a + 1, :], (GRID_W, LANES))
            t = pltpu.roll(x, LANES - (NA_COLS - 1), axis=1, stride=1, stride_axis=0)
            tiles[a] = jnp.where(col_ok, t, MASK_NEG)
        return tiles[a]

    for r in range(GRID_ROWS):
        rs = min(max(r - NA_ROWS // 2, 0), GRID_ROWS - NA_BAND)
        a0 = rs - r + NA_ROWS - 1
        bias = jnp.concatenate([pair_tile(a0 + 2 * i) for i in range(NA_BAND // 2)], axis=1)
        q = q_ref[0, 0, r * GRID_W:(r + 1) * GRID_W, :]
        kb = k_ref[0, 0, rs * GRID_W:(rs + NA_BAND) * GRID_W, :]
        vb = v_ref[0, 0, rs * GRID_W:(rs + NA_BAND) * GRID_W, :]
        s_loc = _dot_nt(q, kb) * scale + bias
        s_ctx = _dot_nt(q, kc) * scale
        m = jnp.maximum(jnp.max(s_loc, axis=-1, keepdims=True), jnp.max(s_ctx, axis=-1, keepdims=True))
        p_loc = jnp.exp(s_loc - m)
        p_ctx = jnp.exp(s_ctx - m)
        den = jnp.sum(p_loc, axis=-1, keepdims=True) + jnp.sum(p_ctx, axis=-1, keepdims=True)
        o_ref[0, 0, r * GRID_W:(r + 1) * GRID_W, :] = (_dot(p_ctx, vc) + _dot(p_loc, vb)) / den


def _na_attn(q, k, v, kc, vc, rel):
    qs = pl.BlockSpec((1, 1, DEC_SEQ, HEAD_DIM), lambda b, h: (b, h, 0, 0))
    cs = pl.BlockSpec((1, 1, PAST_LEN, HEAD_DIM), lambda b, h: (b, h, 0, 0))
    return pl.pallas_call(
        _na_attn_body,
        grid=(DEC_BATCH, B_HEADS),
        in_specs=[qs, qs, qs, cs, cs, pl.BlockSpec((1, 16, LANES), lambda b, h: (h, 0, 0))],
        out_specs=qs,
        out_shape=jax.ShapeDtypeStruct((DEC_BATCH, B_HEADS, DEC_SEQ, HEAD_DIM), F32),
        compiler_params=_cparams("parallel", "parallel"),
        name="na_attn",
    )(q, k, v, kc, vc, rel)


@functools.lru_cache(maxsize=None)
def _dft_mats(L):
    n = 2 * L
    fc = min(L, DFT_CHUNK)
    f = np.arange(L)[:, None]
    t = np.arange(L)[None, :]
    ang = 2.0 * np.pi * ((f * t) % n) / n
    m1 = np.cos(ang)
    m2 = np.sin(ang)
    m2[0, :] = np.where(np.arange(L) % 2 == 0, 1.0, -1.0)
    wgt = np.full((L, 1), 2.0)
    wgt[0, 0] = 1.0
    nch = L // fc
    fwd = np.concatenate([m1.reshape(nch, fc, L), m2.reshape(nch, fc, L)], axis=1)
    inv = np.concatenate([(m1 * wgt / n).reshape(nch, fc, L), (m2 * wgt / n).reshape(nch, fc, L)], axis=1)
    inv = np.transpose(inv, (0, 2, 1))
    return fwd.astype(np.float32), inv.astype(np.float32)


@functools.lru_cache(maxsize=None)
def _filter_consts(L):
    t = np.linspace(0.0, 1.0, L)[:, None]
    bands = (C_EMB - 1) // 2
    ang = (2.0 * math.pi / L) * np.arange(L)[:, None] * np.linspace(1e-4, bands - 1, bands)[None, :]
    z = np.concatenate([t, np.cos(ang), -np.sin(ang)], axis=-1)
    zpad = np.zeros((L, 128))
    zpad[:, :C_EMB] = z
    deltas = np.abs(np.linspace(HYENA_MIN_DECAY, HYENA_MAX_DECAY, C_DIM))
    window = np.exp(-t * deltas[None, :])
    return zpad.astype(np.float32), window.astype(np.float32)


def _filter_body(z_ref, w1_ref, b1_ref, w2_ref, b2_ref, w3_ref, b3_ref, fr_ref, w4_ref, win_ref, fm_ref,
                 hr_ref, g_ref, hq_ref, hs_scr, hd_scr):
    c = pl.program_id(0)
    fc = hr_ref.shape[0]

    @pl.when(c == 0)
    def _():
        fr = fr_ref[...]
        hh = jnp.sin(fr * (_dot_hi(z_ref[...], w1_ref[...]) + b1_ref[...]))
        hh = jnp.sin(fr * (_dot_hi(hh, w2_ref[...]) + b2_ref[...]))
        hh = jnp.sin(fr * (_dot_hi(hh, w3_ref[...]) + b3_ref[...]))
        hh = _dot_hi(hh, w4_ref[...])
        hf = hh[:, :C_DIM] * win_ref[...]
        hb = hh[:, C_DIM:] * win_ref[...]
        hs_scr[...] = hf + hb
        hd_scr[...] = hf - hb

    fm = fm_ref[0]
    hr = _dot_hi(fm[:fc], hs_scr[...])
    first = (lax.broadcasted_iota(jnp.int32, (fc, C_DIM), 0) == 0) & (c == 0)
    hr_ref[...] = hr
    g_ref[...] = jnp.where(first, 0.0, _dot_hi(fm[fc:], hd_scr[...]))
    hs = hs_scr[...]
    sign = jnp.where((lax.broadcasted_iota(jnp.int32, hs.shape, 0) & 1) == 0, 1.0, -1.0)
    hq_ref[...] = jnp.where(first, jnp.sum(hs * sign, axis=0, keepdims=True), hr)


def _hyena_filter(L, filt):
    w1, b1, w2, b2, w3, b3, freq, w4 = filt
    zpad, window = _filter_consts(L)
    fwd, _ = _dft_mats(L)
    nch, fc2, _ = fwd.shape
    fc = fc2 // 2
    w1p = jnp.pad(w1, ((0, 128 - C_EMB), (0, 0)))
    full = lambda shape: pl.BlockSpec(shape, lambda c: tuple(0 for _ in shape))
    out_spec = pl.BlockSpec((fc, C_DIM), lambda c: (c, 0))
    out_sd = jax.ShapeDtypeStruct((L, C_DIM), F32)
    return pl.pallas_call(
        _filter_body,
        grid=(nch,),
        in_specs=[full((L, 128)), full((128, C_FFN)), full((1, C_FFN)), full((C_FFN, C_FFN)), full((1, C_FFN)),
                  full((C_FFN, C_FFN)), full((1, C_FFN)), full((1, C_FFN)), full((C_FFN, 2 * C_DIM)),
                  full((L, C_DIM)), pl.BlockSpec((1, fc2, L), lambda c: (c, 0, 0))],
        out_specs=[out_spec, out_spec, out_spec],
        out_shape=[out_sd, out_sd, out_sd],
        scratch_shapes=[pltpu.VMEM((L, C_DIM), F32), pltpu.VMEM((L, C_DIM), F32)],
        compiler_params=_cparams("arbitrary"),
        name="hyena_filter",
    )(jnp.asarray(zpad), w1p, b1.reshape(1, C_FFN), w2, b2.reshape(1, C_FFN), w3, b3.reshape(1, C_FFN),
      freq.reshape(1, C_FFN), w4, jnp.asarray(window), jnp.asarray(fwd))


def _hyena_body(u_ref, cw_ref, cb_ref, d_ref, fm_ref, fi_ref, hr_ref, g_ref, hq_ref, y_ref,
                x0_scr, z_scr, acc_scr):
    c = pl.program_id(1)
    L = y_ref.shape[0]
    fc = hr_ref.shape[0]

    @pl.when(c == 0)
    def _():
        row = lax.broadcasted_iota(jnp.int32, (L, C_DIM), 0)

        def short_conv(sec):
            cols = slice(sec * C_DIM, (sec + 1) * C_DIM)
            u = u_ref[:, cols]
            prev = jnp.where(row == 0, 0.0, pltpu.roll(u, 1, axis=0))
            nxt = jnp.where(row == L - 1, 0.0, pltpu.roll(u, L - 1, axis=0))
            return (prev * cw_ref[0:1, cols] + u * cw_ref[1:2, cols] + nxt * cw_ref[2:3, cols]
                    + cb_ref[:, cols])

        x0_scr[...] = short_conv(0)
        z_scr[...] = short_conv(1) * short_conv(2)
        acc_scr[...] = jnp.zeros((L, C_DIM), F32)

    ab = _dot_hi(fm_ref[0], z_scr[...])
    a, b = ab[:fc], ab[fc:]
    hr, g, hq = hr_ref[...], g_ref[...], hq_ref[...]
    pq = jnp.concatenate([a * hr - b * g, a * g + b * hq], axis=0)
    acc_scr[...] += _dot_hi(fi_ref[0], pq)

    @pl.when(c == pl.num_programs(1) - 1)
    def _():
        y_ref[...] = x0_scr[...] * (acc_scr[...] + z_scr[...] * d_ref[...])


def _hyena(u, row_blk0, n_seq, L, conv_w, conv_b, d_skip, spec):
    hr, g, hq = spec
    fwd, inv = _dft_mats(L)
    nch, fc2, _ = fwd.shape
    fc = fc2 // 2
    u_w = 3 * C_DIM
    return pl.pallas_call(
        _hyena_body,
        grid=(n_seq, nch),
        in_specs=[pl.BlockSpec((L, u_w), lambda b, c: (row_blk0 + b, 0)),
                  pl.BlockSpec((3, u_w), lambda b, c: (0, 0)),
                  pl.BlockSpec((1, u_w), lambda b, c: (0, 0)),
                  pl.BlockSpec((1, C_DIM), lambda b, c: (0, 0)),
                  pl.BlockSpec((1, fc2, L), lambda b, c: (c, 0, 0)),
                  pl.BlockSpec((1, L, fc2), lambda b, c: (c, 0, 0)),
                  pl.BlockSpec((fc, C_DIM), lambda b, c: (c, 0)),
                  pl.BlockSpec((fc, C_DIM), lambda b, c: (c, 0)),
                  pl.BlockSpec((fc, C_DIM), lambda b, c: (c, 0))],
        out_specs=pl.BlockSpec((L, C_DIM), lambda b, c: (b, 0)),
        out_shape=jax.ShapeDtypeStruct((n_seq * L, C_DIM), F32),
        scratch_shapes=[pltpu.VMEM((L, C_DIM), F32)] * 3,
        compiler_params=_cparams("parallel", "arbitrary"),
        name="hyena",
    )(u, conv_w, conv_b.reshape(1, u_w), d_skip.reshape(1, C_DIM), jnp.asarray(fwd), jnp.asarray(inv), hr, g, hq)


def _hgrn_body(q_ref, ff_ref, fb_ref, i_ref, g_ref, lbf_ref, lbb_ref, nd_ref, s0f_ref, s0b_ref,
               o_ref, sf_ref, sb_ref, o_scr, *, layer):
    L = o_ref.shape[0]
    C = GLA_CHUNK
    nc = L // C
    mid = C // 2
    q = _silu(q_ref[...])
    v = i_ref[...]

    def lower_bound(ref):
        gm = ref[...]
        e = jnp.exp(gm - jnp.max(gm, axis=0, keepdims=True))
        p = e / jnp.sum(e, axis=0, keepdims=True)
        return jnp.sum(p[0:layer + 1], axis=0, keepdims=True) - p[0:1]

    def gates(fx, lb):
        f = lb + (1.0 - lb) * jax.nn.sigmoid(fx)
        return 1.0 - f, jnp.log(f)

    kf, lgf = gates(ff_ref[...], lower_bound(lbf_ref))
    kb, lgb = gates(fb_ref[...], lower_bound(lbb_ref))
    ti = lax.broadcasted_iota(jnp.int32, (C, C), 0)
    si = lax.broadcasted_iota(jnp.int32, (C, C), 1)
    causal = si <= ti
    anti = si >= ti
    tril = causal.astype(F32)
    triu = anti.astype(F32)

    st = jnp.transpose(s0f_ref[0, 0])
    for n in range(nc):
        sl = slice(n * C, (n + 1) * C)
        b = _dot_hi(tril, lgf[sl])
        btot = b[C - 1:C]
        ref = b[mid:mid + 1]
        qc, kc, vc = q[sl], kf[sl], v[sl]
        sc = jnp.where(causal, _dot_nt(qc * jnp.exp(b - ref), kc * jnp.exp(ref - b)), 0.0)
        o_scr[sl, :] = _dot(sc, vc) + _dot_nt(qc * jnp.exp(b), st)
        st = st * jnp.exp(btot) + _dot_tn(vc, kc * jnp.exp(btot - b))
    sf_ref[0, 0] = jnp.transpose(st)

    st = jnp.transpose(s0b_ref[0, 0])
    for n in reversed(range(nc)):
        sl = slice(n * C, (n + 1) * C)
        b = _dot_hi(triu, lgb[sl])
        btot = b[0:1]
        ref = b[mid:mid + 1]
        qc, kc, vc = q[sl], kb[sl], v[sl]
        sc = jnp.where(anti, _dot_nt(qc * jnp.exp(b - ref), kc * jnp.exp(ref - b)), 0.0)
        o_scr[sl, :] += _dot(sc, vc) + _dot_nt(qc * jnp.exp(b), st)
        st = st * jnp.exp(btot) + _dot_tn(vc, kc * jnp.exp(btot - b))
    sb_ref[0, 0] = jnp.transpose(st)

    o_ref[...] = _rms(o_scr[...], nd_ref[...]) * _silu(g_ref[...])


def _hgrn(u, row_blk0, n_seq, L, lb_fwd, lb_bwd, norm_d, s0f, s0b, layer):
    col0 = 3 * C_DIM // D_KDIM
    col = lambda j: pl.BlockSpec((L, D_KDIM), lambda b, h: (row_blk0 + b, col0 + j * D_HEADS + h))
    lbs = pl.BlockSpec((DEPTH, D_KDIM), lambda b, h: (0, h))
    st = pl.BlockSpec((1, 1, D_KDIM, D_VDIM), lambda b, h: (b, h, 0, 0))
    st_sd = jax.ShapeDtypeStruct((n_seq, D_HEADS, D_KDIM, D_VDIM), F32)
    return pl.pallas_call(
        functools.partial(_hgrn_body, layer=layer),
        grid=(n_seq, D_HEADS),
        in_specs=[col(0), col(1), col(2), col(3), col(4), lbs, lbs,
                  pl.BlockSpec((1, D_VDIM), lambda b, h: (0, 0)), st, st],
        out_specs=[pl.BlockSpec((L, D_VDIM), lambda b, h: (b, h)), st, st],
        out_shape=[jax.ShapeDtypeStruct((n_seq * L, D_HEADS * D_VDIM), F32), st_sd, st_sd],
        scratch_shapes=[pltpu.VMEM((L, D_VDIM), F32)],
        compiler_params=_cparams("parallel", "parallel"),
        name="hgrn",
    )(u, u, u, u, u, lb_fwd, lb_bwd, norm_d.reshape(1, D_VDIM), s0f, s0b)


def _pack_bf16_pairs(h):
    n = h.shape[1] // 2
    hi = lax.bitcast_convert_type(h[:, :n].astype(BF16).astype(F32), jnp.int32)
    lo = lax.bitcast_convert_type(h[:, n:].astype(BF16).astype(F32), jnp.int32)
    return hi | lax.shift_right_logical(lo, 16)


def _unpack_bf16_pairs(p):
    hi = lax.bitcast_convert_type(p & jnp.int32(-65536), F32).astype(BF16)
    lo = lax.bitcast_convert_type(lax.shift_left(p, 16), F32).astype(BF16)
    return hi, lo


def _outproj_body(a_ref, b_ref, x_ref, mod_ref, gf_ref, w_ref, wr_ref, rb_ref,
                  x1_ref, h2_ref, chosen_ref, gk_ref, ik_ref):
    m = mod_ref[0]
    half = a_ref.shape[1]
    out = _dot(a_ref[...], w_ref[0:half, :]) + _dot(b_ref[...], w_ref[half:, :])
    x1 = x_ref[...] + m[:, 2 * D_MODEL:3 * D_MODEL] * out
    x1_ref[...] = x1
    h2 = _rms(x1, gf_ref[...]) * (1.0 + m[:, 4 * D_MODEL:5 * D_MODEL]) + m[:, 3 * D_MODEL:4 * D_MODEL]
    h2_ref[...] = _pack_bf16_pairs(h2)
    scores = jax.nn.sigmoid(_dot_hi(h2, wr_ref[...]))
    work = scores + rb_ref[...]
    lane = lax.broadcasted_iota(jnp.int32, work.shape, 1).astype(F32)
    slot = lax.broadcasted_iota(jnp.int32, (work.shape[0], LANES), 1)
    chosen = jnp.zeros(work.shape, F32)
    gk = jnp.zeros((work.shape[0], LANES), F32)
    ik = jnp.zeros((work.shape[0], LANES), F32)
    for k in range(TOP_K):
        best = jnp.max(work, axis=-1, keepdims=True)
        first = jnp.min(jnp.where(work == best, lane, float(N_EXPERTS)), axis=-1, keepdims=True)
        hit = lane == first
        chosen = jnp.where(hit, 1.0, chosen)
        gk = jnp.where(slot == k, jnp.sum(jnp.where(hit, scores, 0.0), axis=-1, keepdims=True), gk)
        ik = jnp.where(slot == k, first, ik)
        work = jnp.where(hit, -jnp.inf, work)
    chosen_ref[...] = chosen
    gk_ref[...] = gk / jnp.sum(gk, axis=-1, keepdims=True) * ROUTE_SCALE
    ik_ref[...] = ik


def _outproj(a, b, x, mod_l, gain_ffn, w_out_bf16, w_router, router_bias):
    half = a.shape[1]
    return pl.pallas_call(
        _outproj_body,
        grid=(T_ALL // TM,),
        in_specs=[pl.BlockSpec((TM, half), lambda i: (i, 0)),
                  pl.BlockSpec((TM, half), lambda i: (i, 0)),
                  pl.BlockSpec((TM, D_MODEL), lambda i: (i, 0)),
                  pl.BlockSpec((1, 1, N_MOD * D_MODEL), lambda i: (_mod_row(i), 0, 0)),
                  pl.BlockSpec((1, D_MODEL), lambda i: (0, 0)),
                  pl.BlockSpec((2 * half, D_MODEL), lambda i: (0, 0)),
                  pl.BlockSpec((D_MODEL, N_EXPERTS), lambda i: (0, 0)),
                  pl.BlockSpec((1, N_EXPERTS), lambda i: (0, 0))],
        out_specs=[pl.BlockSpec((TM, D_MODEL), lambda i: (i, 0)),
                   pl.BlockSpec((TM, D_MODEL // 2), lambda i: (i, 0)),
                   pl.BlockSpec((TM, N_EXPERTS), lambda i: (i, 0)),
                   pl.BlockSpec((TM, LANES), lambda i: (i, 0)),
                   pl.BlockSpec((TM, LANES), lambda i: (i, 0))],
        out_shape=[jax.ShapeDtypeStruct((T_ALL, D_MODEL), F32),
                   jax.ShapeDtypeStruct((T_ALL, D_MODEL // 2), jnp.int32),
                   jax.ShapeDtypeStruct((T_ALL, N_EXPERTS), F32),
                   jax.ShapeDtypeStruct((T_ALL, LANES), F32),
                   jax.ShapeDtypeStruct((T_ALL, LANES), F32)],
        compiler_params=_cparams("parallel"),
        name="outproj_router",
    )(a, b, x, mod_l, gain_ffn.reshape(1, D_MODEL), w_out_bf16, w_router, router_bias.reshape(1, N_EXPERTS))


def _route_body(chosen_ref, ik_ref, dest_ref, blk_ref, used_ref, pos_scr):
    n_tiles = T_ALL // TM
    r = lax.broadcasted_iota(jnp.int32, (TM, TM), 0)
    c = lax.broadcasted_iota(jnp.int32, (TM, TM), 1)
    before = (c < r).astype(BF16)

    def count_tile(i, carry):
        rows = pl.ds(pl.multiple_of(i * TM, TM), TM)
        m = chosen_ref[rows, :]
        pos_scr[rows, :] = jnp.dot(before, m.astype(BF16), preferred_element_type=F32) + carry
        return carry + jnp.sum(m, axis=0, keepdims=True)

    counts = lax.fori_loop(0, n_tiles, count_tile, jnp.zeros((1, N_EXPERTS), F32))
    padded = jnp.ceil(counts * (1.0 / MOE_BLK)) * MOE_BLK
    ei = lax.broadcasted_iota(jnp.int32, (N_EXPERTS, N_EXPERTS), 0)
    ej = lax.broadcasted_iota(jnp.int32, (N_EXPERTS, N_EXPERTS), 1)
    end = _dot_hi(jnp.broadcast_to(padded, (8, N_EXPERTS)), (ei <= ej).astype(F32))[0:1]
    start = end - padded

    lane = lax.broadcasted_iota(jnp.int32, (TM, N_EXPERTS), 1).astype(F32)
    slot = lax.broadcasted_iota(jnp.int32, (TM, LANES), 1)

    def dest_tile(i, carry):
        rows = pl.ds(pl.multiple_of(i * TM, TM), TM)
        row_of = pos_scr[rows, :] + start
        ik = ik_ref[rows, :]
        acc = jnp.zeros((TM, LANES), F32)
        for k in range(TOP_K):
            pick = jnp.sum(jnp.where(lane == ik[:, k:k + 1], row_of, 0.0), axis=-1, keepdims=True)
            acc = jnp.where(slot == k, pick, acc)
        dest_ref[rows, :] = acc.astype(jnp.int32)
        return carry

    lax.fori_loop(0, n_tiles, dest_tile, 0)
    blk_start = (lax.broadcasted_iota(jnp.int32, (MOE_NBLK, N_EXPERTS), 0) * MOE_BLK).astype(F32)
    owner = jnp.sum((end <= blk_start).astype(F32), axis=-1, keepdims=True)
    blk_ref[...] = jnp.broadcast_to(jnp.minimum(owner, N_EXPERTS - 1.0), (MOE_NBLK, LANES)).astype(jnp.int32)
    used = end[:, N_EXPERTS - 1:N_EXPERTS] * (1.0 / MOE_BLK)
    used_ref[...] = jnp.broadcast_to(used, (8, LANES)).astype(jnp.int32)


def _route(chosen, ik):
    full = lambda shape: pl.BlockSpec(shape, lambda i: (0, 0))
    return pl.pallas_call(
        _route_body,
        grid=(1,),
        in_specs=[full((T_ALL, N_EXPERTS)), full((T_ALL, LANES))],
        out_specs=[full((T_ALL, LANES)), full((MOE_NBLK, LANES)), full((8, LANES))],
        out_shape=[jax.ShapeDtypeStruct((T_ALL, LANES), jnp.int32),
                   jax.ShapeDtypeStruct((MOE_NBLK, LANES), jnp.int32),
                   jax.ShapeDtypeStruct((8, LANES), jnp.int32)],
        scratch_shapes=[pltpu.VMEM((T_ALL, N_EXPERTS), F32)],
        compiler_params=_cparams("arbitrary"),
        name="moe_route",
    )(chosen, ik)


def _sc_worker_id():
    return lax.axis_index("s") * SC_CORES + lax.axis_index("c")


def _sc_dispatch(h2p, dest_chunks):
    n_chunks = T_ALL // DISP_CHUNK
    width = h2p.shape[1]
    mesh = plsc.VectorSubcoreMesh(core_axis_name="c", subcore_axis_name="s")

    @functools.partial(
        pl.kernel, mesh=mesh,
        out_type=jax.ShapeDtypeStruct((MOE_ROWS, width), jnp.int32),
        scratch_types=[pltpu.VMEM((TOP_K, DISP_CHUNK), jnp.int32), pltpu.VMEM((DISP_CHUNK, width), jnp.int32)],
    )
    def run(x_hbm, dest_hbm, xs_hbm, idx_v, rows_v):
        wid = _sc_worker_id()
        for rep in range(-(-n_chunks // SC_WORKERS)):
            chunk = wid + rep * SC_WORKERS

            @pl.when(chunk < n_chunks)
            def _():
                pltpu.sync_copy(dest_hbm.at[chunk], idx_v)
                pltpu.sync_copy(x_hbm.at[pl.ds(chunk * DISP_CHUNK, DISP_CHUNK)], rows_v)
                for k in range(TOP_K):
                    pltpu.sync_copy(rows_v, xs_hbm.at[idx_v.at[k]])

    return run(h2p, dest_chunks)


def _sc_collect(y, dest_flat):
    per_worker = T_ALL // SC_WORKERS
    n_chunks = per_worker // COLLECT_CHUNK
    mesh = plsc.VectorSubcoreMesh(core_axis_name="c", subcore_axis_name="s")

    @functools.partial(
        pl.kernel, mesh=mesh,
        out_type=jax.ShapeDtypeStruct((TOP_K * T_ALL, D_MODEL), F32),
        scratch_types=[pltpu.VMEM((COLLECT_CHUNK,), jnp.int32), pltpu.VMEM((COLLECT_CHUNK, D_MODEL), F32),
                       pltpu.SemaphoreType.DMA],
    )
    def run(y_hbm, dest_hbm, yg_hbm, idx_v, rows_v, sem):
        wid = _sc_worker_id()

        @pl.loop(0, TOP_K * n_chunks)
        def _(step):
            k = step // n_chunks
            off = pl.multiple_of(k * T_ALL + wid * per_worker + (step % n_chunks) * COLLECT_CHUNK, COLLECT_CHUNK)
            pltpu.sync_copy(dest_hbm.at[pl.ds(off, COLLECT_CHUNK)], idx_v)
            pltpu.async_copy(y_hbm.at[idx_v], rows_v, sem).wait()
            pltpu.sync_copy(rows_v, yg_hbm.at[pl.ds(off, COLLECT_CHUNK)])

    return run(y, dest_flat)


def _expert_body(blk_ref, used_ref, xs_ref, wg_ref, wu_ref, wd_ref, y_ref):
    i = pl.program_id(0)

    @pl.when(i < used_ref[0])
    def _():
        hi, lo = _unpack_bf16_pairs(xs_ref[...])
        half = D_MODEL // 2

        def proj(w_ref):
            return _dot(hi, w_ref[0, 0:half, :]) + _dot(lo, w_ref[0, half:, :])

        y_ref[...] = _dot(_silu(proj(wg_ref)) * proj(wu_ref), wd_ref[0])

    @pl.when(i >= used_ref[0])
    def _():
        y_ref[...] = jnp.zeros(y_ref.shape, F32)


def _experts(blk_expert, n_used, xs, w_gate, w_up, w_down):
    w_in = pl.BlockSpec((1, D_MODEL, D_EXPERT), lambda i, blk, used: (blk[i], 0, 0))
    grid_spec = pltpu.PrefetchScalarGridSpec(
        num_scalar_prefetch=2,
        grid=(MOE_NBLK,),
        in_specs=[pl.BlockSpec((MOE_BLK, D_MODEL // 2), lambda i, blk, used: (jnp.minimum(i, used[0] - 1), 0)),
                  w_in, w_in,
                  pl.BlockSpec((1, D_EXPERT, D_MODEL), lambda i, blk, used: (blk[i], 0, 0))],
        out_specs=pl.BlockSpec((MOE_BLK, D_MODEL), lambda i, blk, used: (i, 0)),
    )
    return pl.pallas_call(
        _expert_body,
        grid_spec=grid_spec,
        out_shape=jax.ShapeDtypeStruct((MOE_ROWS, D_MODEL), F32),
        compiler_params=_cparams("arbitrary"),
        name="moe_experts",
    )(blk_expert, n_used, xs, w_gate, w_up, w_down)


def _combine_body(x1_ref, h2_ref, yg_ref, gk_ref, mod_ref, sg_ref, su_ref, sd_ref, fn_ref, o_ref, *, final):
    hi, lo = _unpack_bf16_pairs(h2_ref[...])
    half = D_MODEL // 2

    def proj(w_ref):
        return _dot(hi, w_ref[0:half, :]) + _dot(lo, w_ref[half:, :])

    acc = _dot(_silu(proj(sg_ref)) * proj(su_ref), sd_ref[...])
    gk = gk_ref[...]
    for k in range(TOP_K):
        acc = acc + gk[:, k:k + 1] * yg_ref[k]
    m = mod_ref[0]
    y = x1_ref[...] + m[:, 5 * D_MODEL:6 * D_MODEL] * acc
    o_ref[...] = _rms(y, fn_ref[...]) if final else y


def _combine(x1, h2p, yg, gk, mod_l, ws_gate, ws_up, ws_down, final_norm, final):
    tok = lambda shape: pl.BlockSpec(shape, lambda i: (i, 0))
    full = lambda shape: pl.BlockSpec(shape, lambda i: (0, 0))
    return pl.pallas_call(
        functools.partial(_combine_body, final=final),
        grid=(T_ALL // TM,),
        in_specs=[tok((TM, D_MODEL)), tok((TM, D_MODEL // 2)),
                  pl.BlockSpec((TOP_K, TM, D_MODEL), lambda i: (0, i, 0)),
                  tok((TM, LANES)),
                  pl.BlockSpec((1, 1, N_MOD * D_MODEL), lambda i: (_mod_row(i), 0, 0)),
                  full((D_MODEL, D_EXPERT)), full((D_MODEL, D_EXPERT)), full((D_EXPERT, D_MODEL)),
                  full((1, D_MODEL))],
        out_specs=tok((TM, D_MODEL)),
        out_shape=jax.ShapeDtypeStruct((T_ALL, D_MODEL), F32),
        compiler_params=_cparams("parallel"),
        name="moe_combine",
    )(x1, h2p, yg, gk, mod_l, ws_gate, ws_up, ws_down, final_norm.reshape(1, D_MODEL))


def _moe(x1, h2p, chosen, gk, ik, mod_l, w_gate, w_up, w_down, ws_gate, ws_up, ws_down, final_norm, final):
    dest, blk, used = _route(chosen, ik)
    dest = dest[:, :TOP_K]
    dest_chunks = dest.reshape(T_ALL // DISP_CHUNK, DISP_CHUNK, TOP_K).transpose(0, 2, 1)
    xs = _sc_dispatch(h2p, dest_chunks)
    y = _experts(blk[:, 0], used[0, :1], xs, w_gate, w_up, w_down)
    yg = _sc_collect(y, dest.T.reshape(-1)).reshape(TOP_K, T_ALL, D_MODEL)
    return _combine(x1, h2p, yg, gk, mod_l, ws_gate.astype(BF16), ws_up.astype(BF16), ws_down.astype(BF16),
                    final_norm, final)


def _heads_major(t, n_heads):
    return t.reshape(DEC_BATCH, DEC_SEQ, n_heads, HEAD_DIM).transpose(0, 2, 1, 3)


def _tokens_major(t):
    return t.transpose(0, 2, 1, 3).reshape(T_LAT, -1)


def kernel(x_prompt, x_sample, cache_a_k, cache_a_v, cache_b_k, cache_b_v, state_d_fwd, state_d_bwd, c, c_ctx, w_ada, b_ada, norm_mix, norm_ffn, w_in_attn, w_out_attn, sink_a, rpb_b, w_in_rec, w_out_rec, conv_w, conv_b, filt_w1, filt_b1, filt_w2, filt_b2, filt_w3, filt_b3, filt_freq, filt_w4, d_skip, lb_fwd, lb_bwd, norm_d, w_router, router_bias, w_gate, w_up, w_down, ws_gate, ws_up, ws_down, final_norm):
    x = jnp.concatenate([x_prompt.reshape(T_CTX, D_MODEL), x_sample.reshape(T_LAT, D_MODEL)], axis=0)
    cvec = jnp.concatenate([c_ctx[None, :], c, jnp.zeros((CVEC_PAD - N_CVEC, D_MODEL), F32)], axis=0)
    mod = _ada(cvec.T, w_ada, b_ada).reshape(DEPTH, CVEC_PAD, 1, N_MOD * D_MODEL)

    new_kv = None
    new_state = None
    for l in range(DEPTH):
        j = l // 2
        final = l == DEPTH - 1
        if l % 2 == 0:
            qkv = _inproj(x, mod[l], norm_mix[l], w_in_attn[j].astype(BF16))
            lat = qkv[T_CTX:]
            o_ctx = _ctx_attn(qkv, sink_a[j])
            kv_cols = lambda lo, heads: qkv[:T_CTX, lo:lo + heads * HEAD_DIM].reshape(BATCH, 1, SEQ, heads, HEAD_DIM)
            new_kv = (kv_cols(A_Q, A_KV_HEADS), kv_cols(A_Q + A_KV, A_KV_HEADS),
                      kv_cols(A_Q + 2 * A_KV + B_W, B_HEADS), kv_cols(A_Q + 2 * A_KV + 2 * B_W, B_HEADS))
            base = A_Q + 2 * A_KV
            q_rot, k_rot = _rope(qkv)
            hm = _heads_major
            cache_hm = lambda t: t[:, j].transpose(0, 2, 1, 3)
            o_a = _win_attn(hm(lat[:, :A_Q], A_HEADS), hm(q_rot, A_HEADS), hm(k_rot, A_KV_HEADS),
                            hm(lat[:, A_Q + A_KV:base], A_KV_HEADS), cache_hm(cache_a_k), cache_hm(cache_a_v),
                            sink_a[j])
            o_b = _na_attn(hm(lat[:, base:base + B_W], B_HEADS), hm(lat[:, base + B_W:base + 2 * B_W], B_HEADS),
                           hm(lat[:, base + 2 * B_W:], B_HEADS), cache_hm(cache_b_k), cache_hm(cache_b_v),
                           _na_rel_rows(rpb_b[j]))
            mix_a = jnp.concatenate([o_ctx[:, :A_Q], _tokens_major(o_a)], axis=0)
            mix_b = jnp.concatenate([o_ctx[:, A_Q:], _tokens_major(o_b)], axis=0)
            w_out = w_out_attn[j]
        else:
            u = _inproj(x, mod[l], norm_mix[l], w_in_rec[j].astype(BF16))
            filt = (filt_w1[j], filt_b1[j], filt_w2[j], filt_b2[j], filt_w3[j], filt_b3[j], filt_freq[j],
                    filt_w4[j])
            y_ctx = _hyena(u, 0, BATCH, SEQ, conv_w[j], conv_b[j], d_skip[j], _hyena_filter(SEQ, filt))
            y_lat = _hyena(u, T_CTX // DEC_SEQ, DEC_BATCH, DEC_SEQ, conv_w[j], conv_b[j], d_skip[j],
                           _hyena_filter(DEC_SEQ, filt))
            zeros = jnp.zeros((BATCH, D_HEADS, D_KDIM, D_VDIM), F32)
            o_ctx, s_f, s_b = _hgrn(u, 0, BATCH, SEQ, lb_fwd, lb_bwd, norm_d[j], zeros, zeros, l)
            o_lat, _, _ = _hgrn(u, T_CTX // DEC_SEQ, DEC_BATCH, DEC_SEQ, lb_fwd, lb_bwd, norm_d[j],
                                state_d_fwd[:, j], state_d_bwd[:, j], l)
            new_state = (s_f[:, None], s_b[:, None])
            mix_a = jnp.concatenate([y_ctx, y_lat], axis=0)
            mix_b = jnp.concatenate([o_ctx, o_lat], axis=0)
            w_out = w_out_rec[j]
        x1, h2p, chosen, gk, ik = _outproj(mix_a, mix_b, x, mod[l], norm_ffn[l], w_out.astype(BF16), w_router[l],
                                           router_bias[l])
        x = _moe(x1, h2p, chosen, gk, ik, mod[l], w_gate[l], w_up[l], w_down[l], ws_gate[l], ws_up[l],
                 ws_down[l], final_norm, final)

    y_prompt = x[:T_CTX].reshape(BATCH, SEQ, D_MODEL)
    y_sample = x[T_CTX:].reshape(DEC_BATCH, DEC_SEQ, D_MODEL)
    return (y_prompt, y_sample) + new_kv + new_state
```

```python
import functools
import math

import numpy as np
import jax
import jax.numpy as jnp
from jax import lax
from jax.experimental import pallas as pl
from jax.experimental.pallas import tpu as pltpu
from jax.experimental.pallas import tpu_sc as plsc

F32 = jnp.float32
BF16 = jnp.bfloat16
HI = lax.Precision.HIGHEST

D_MODEL = 1024
BATCH = 16
SEQ = 256
DEPTH = 2
DEC_BATCH = 2
DEC_SEQ = 1024
PAST_LEN = 512
GRID_W = 64
HEAD_DIM = 64
N_MOD = 6
RMS_EPS = 1e-6
A_HEADS = 8
A_KV_HEADS = 2
A_GROUP = A_HEADS // A_KV_HEADS
WINDOW = 128
ROPE_BASE = 10000.0
B_HEADS = 8
NA_ROWS = 8
NA_COLS = 16
C_DIM = 512
C_EMB = 33
C_FFN = 64
HYENA_MIN_DECAY = math.log(1e-2) / 1.5
HYENA_MAX_DECAY = math.log(1e-2) / 0.3
D_KDIM = 128
D_VDIM = 128
D_HEADS = 4
N_EXPERTS = 64
TOP_K = 8
D_EXPERT = 256
ROUTE_SCALE = 2.5
A_Q = A_HEADS * HEAD_DIM
A_KV = A_KV_HEADS * HEAD_DIM
B_W = B_HEADS * HEAD_DIM
ATTN_IN = A_Q + 2 * A_KV + 3 * B_W
REC_IN = 3 * C_DIM + 5 * D_HEADS * D_KDIM

T_CTX = BATCH * SEQ
T_LAT = DEC_BATCH * DEC_SEQ
T_ALL = T_CTX + T_LAT
N_CVEC = 1 + DEC_BATCH
CVEC_PAD = 8
TM = 256
MASK_NEG = -1e30
GLA_CHUNK = 64
DFT_CHUNK = 256
MOE_BLK = 256
MOE_NBLK = -(-(T_ALL * TOP_K + N_EXPERTS * (MOE_BLK - 1)) // MOE_BLK)
MOE_ROWS = MOE_NBLK * MOE_BLK
SC_CORES = 2
SC_SUBCORES = 16
SC_WORKERS = SC_CORES * SC_SUBCORES
DISP_CHUNK = 128
COLLECT_CHUNK = 64
VMEM_LIMIT = 56 * 1024 * 1024


def _cparams(*sem):
    return pltpu.CompilerParams(dimension_semantics=sem, vmem_limit_bytes=VMEM_LIMIT)


def _mod_row(i):
    return jnp.where(i < T_CTX // TM, 0, 1 + (i - T_CTX // TM) // (DEC_SEQ // TM))


def _dot(a, b):
    return jnp.dot(a.astype(BF16), b.astype(BF16), preferred_element_type=F32)


def _dot_nt(a, b):
    return lax.dot_general(a.astype(BF16), b.astype(BF16), (((1,), (1,)), ((), ())),
                           preferred_element_type=F32)


def _dot_tn(a, b):
    return lax.dot_general(a.astype(BF16), b.astype(BF16), (((0,), (0,)), ((), ())),
                           preferred_element_type=F32)


def _dot_hi(a, b):
    return jnp.dot(a, b, precision=HI, preferred_element_type=F32)


def _silu(x):
    return x * jax.nn.sigmoid(x)


def _rms(x, g):
    return x * lax.rsqrt(jnp.mean(x * x, axis=-1, keepdims=True) + RMS_EPS) * g


def _ada_body(ct_ref, w_ref, b_ref, o_ref):
    tn = o_ref.shape[-1]

    def step(k8, accs):
        r0 = pl.multiple_of(k8 * 8, 8)
        wk = w_ref[0, pl.ds(r0, 8), :]
        sk = _silu(ct_ref[pl.ds(r0, 8), :])
        return tuple(acc + wk * sk[:, j:j + 1] for j, acc in enumerate(accs))

    accs = lax.fori_loop(0, D_MODEL // 8, step, tuple(jnp.zeros((8, tn), F32) for _ in range(N_CVEC)))
    o_ref[0] = jnp.zeros((CVEC_PAD, tn), F32)
    for j in range(N_CVEC):
        o_ref[0, j:j + 1, :] = jnp.sum(accs[j], axis=0, keepdims=True) + b_ref[0]


def _ada(cvec_t, w_ada, b_ada):
    tn = 1536
    n_out = N_MOD * D_MODEL
    return pl.pallas_call(
        _ada_body,
        grid=(DEPTH, n_out // tn),
        in_specs=[pl.BlockSpec((D_MODEL, CVEC_PAD), lambda l, n: (0, 0)),
                  pl.BlockSpec((1, D_MODEL, tn), lambda l, n: (l, 0, n)),
                  pl.BlockSpec((1, 1, tn), lambda l, n: (l, 0, n))],
        out_specs=pl.BlockSpec((1, CVEC_PAD, tn), lambda l, n: (l, 0, n)),
        out_shape=jax.ShapeDtypeStruct((DEPTH, CVEC_PAD, n_out), F32),
        compiler_params=_cparams("parallel", "parallel"),
        name="ada",
    )(cvec_t, w_ada, b_ada.reshape(DEPTH, 1, n_out))


N_CTX_TILES = T_CTX // TM


def _token_specs(x, width):
    if not isinstance(x, tuple):
        return [pl.BlockSpec((TM, width), lambda i: (i, 0))], (x,)
    return ([pl.BlockSpec((TM, width), lambda i: (jnp.minimum(i, N_CTX_TILES - 1), 0)),
             pl.BlockSpec((TM, width), lambda i: (jnp.maximum(i - N_CTX_TILES, 0), 0))], x)


def _token_tile(refs):
    if len(refs) == 1:
        return refs[0][...]
    return jnp.where(pl.program_id(0) < N_CTX_TILES, refs[0][...], refs[1][...])


def _inproj_body(*refs, n_x):
    x_refs, (mod_ref, g_ref, w_ref, o_ref) = refs[:n_x], refs[n_x:]
    m = mod_ref[0]
    h = _rms(_token_tile(x_refs), g_ref[...]) * (1.0 + m[:, D_MODEL:2 * D_MODEL]) + m[:, 0:D_MODEL]
    o_ref[...] = _dot(h, w_ref[...])


def _inproj(x, mod_l, gain, w_bf16):
    n = w_bf16.shape[1]
    x_specs, x_args = _token_specs(x, D_MODEL)
    return pl.pallas_call(
        functools.partial(_inproj_body, n_x=len(x_args)),
        grid=(T_ALL // TM,),
        in_specs=x_specs + [pl.BlockSpec((1, 1, N_MOD * D_MODEL), lambda i: (_mod_row(i), 0, 0)),
                            pl.BlockSpec((1, D_MODEL), lambda i: (0, 0)),
                            pl.BlockSpec((D_MODEL, n), lambda i: (0, 0))],
        out_specs=pl.BlockSpec((TM, n), lambda i: (i, 0)),
        out_shape=jax.ShapeDtypeStruct((T_ALL, n), F32),
        compiler_params=_cparams("parallel"),
        name="inproj",
    )(*x_args, mod_l, gain.reshape(1, D_MODEL), w_bf16)


def _head_cols(h):
    return slice(h * HEAD_DIM, (h + 1) * HEAD_DIM)


def _group_rows(ref, rows, first_col, sink_ref, hk):
    n = rows.stop - rows.start
    q = jnp.concatenate([ref[rows, first_col + g * HEAD_DIM:first_col + (g + 1) * HEAD_DIM]
                         for g in range(A_GROUP)], axis=0)
    sink = jnp.concatenate([jnp.broadcast_to(sink_ref[:, hk * A_GROUP + g:hk * A_GROUP + g + 1], (n, 1))
                            for g in range(A_GROUP)], axis=0)
    return q, sink


def _ctx_attn_body(qkv_ref, sink_ref, oa_ref, ob_ref, ak_ref, av_ref, bk_ref, bv_ref):
    scale = HEAD_DIM ** -0.5
    rows = slice(0, SEQ)

    def attend(q, k, v, sink):
        s = _dot_nt(q, k) * scale
        m = jnp.max(s, axis=-1, keepdims=True)
        if sink is not None:
            m = jnp.maximum(m, sink)
        p = jnp.exp(s - m)
        den = jnp.sum(p, axis=-1, keepdims=True)
        if sink is not None:
            den = den + jnp.exp(sink - m)
        return _dot(p, v) / den

    for hk in range(A_KV_HEADS):
        k = qkv_ref[:, A_Q + hk * HEAD_DIM:A_Q + (hk + 1) * HEAD_DIM]
        v = qkv_ref[:, A_Q + A_KV + hk * HEAD_DIM:A_Q + A_KV + (hk + 1) * HEAD_DIM]
        ak_ref[0, 0, :, hk, :] = k
        av_ref[0, 0, :, hk, :] = v
        q, sink = _group_rows(qkv_ref, rows, hk * A_GROUP * HEAD_DIM, sink_ref, hk)
        o = attend(q, k, v, sink)
        for g in range(A_GROUP):
            oa_ref[:, _head_cols(hk * A_GROUP + g)] = o[g * SEQ:(g + 1) * SEQ]
    base = A_Q + 2 * A_KV
    for h in range(B_HEADS):
        q = qkv_ref[:, base + h * HEAD_DIM:base + (h + 1) * HEAD_DIM]
        k = qkv_ref[:, base + B_W + h * HEAD_DIM:base + B_W + (h + 1) * HEAD_DIM]
        v = qkv_ref[:, base + 2 * B_W + h * HEAD_DIM:base + 2 * B_W + (h + 1) * HEAD_DIM]
        bk_ref[0, 0, :, h, :] = k
        bv_ref[0, 0, :, h, :] = v
        ob_ref[:, _head_cols(h)] = attend(q, k, v, None)


def _ctx_attn(qkv, sink):
    kv_spec = lambda heads: pl.BlockSpec((1, 1, SEQ, heads, HEAD_DIM), lambda b: (b, 0, 0, 0, 0))
    kv_sd = lambda heads: jax.ShapeDtypeStruct((BATCH, 1, SEQ, heads, HEAD_DIM), F32)
    return pl.pallas_call(
        _ctx_attn_body,
        grid=(BATCH,),
        in_specs=[pl.BlockSpec((SEQ, ATTN_IN), lambda b: (b, 0)),
                  pl.BlockSpec((1, A_HEADS), lambda b: (0, 0))],
        out_specs=[pl.BlockSpec((SEQ, A_Q), lambda b: (b, 0)), pl.BlockSpec((SEQ, B_W), lambda b: (b, 0)),
                   kv_spec(A_KV_HEADS), kv_spec(A_KV_HEADS), kv_spec(B_HEADS), kv_spec(B_HEADS)],
        out_shape=[jax.ShapeDtypeStruct((T_CTX, A_Q), F32), jax.ShapeDtypeStruct((T_CTX, B_W), F32),
                   kv_sd(A_KV_HEADS), kv_sd(A_KV_HEADS), kv_sd(B_HEADS), kv_sd(B_HEADS)],
        compiler_params=_cparams("parallel"),
        name="ctx_attn",
    )(qkv, sink.reshape(1, A_HEADS))


@functools.lru_cache(maxsize=None)
def _rope_tables(width):
    half = HEAD_DIM // 2
    t = np.arange(DEC_SEQ)
    inv = ROPE_BASE ** (-np.arange(0, half, 2, dtype=np.float64) / half)
    ang_r = (t // GRID_W)[:, None] * inv[None, :]
    ang_c = (t % GRID_W)[:, None] * inv[None, :]
    cos = np.concatenate([np.cos(ang_r)] * 2 + [np.cos(ang_c)] * 2, axis=-1)
    sin = np.concatenate([-np.sin(ang_r), np.sin(ang_r), -np.sin(ang_c), np.sin(ang_c)], axis=-1)
    reps = width // HEAD_DIM
    return (np.tile(cos, (1, reps)).astype(np.float32), np.tile(sin, (1, reps)).astype(np.float32))


def _rope_body(q_ref, k_ref, cq_ref, sq_ref, ck_ref, sk_ref, qo_ref, ko_ref):
    quarter = HEAD_DIM // 4

    def rot(x, cos, sin):
        w = x.shape[-1]
        lane = lax.broadcasted_iota(jnp.int32, x.shape, 1)
        fwd = pltpu.roll(x, w - quarter, axis=1)
        bwd = pltpu.roll(x, quarter, axis=1)
        partner = jnp.where((lane & (2 * quarter - 1)) < quarter, fwd, bwd)
        return x * cos + partner * sin

    qo_ref[...] = rot(q_ref[...], cq_ref[...], sq_ref[...])
    ko_ref[...] = rot(k_ref[...], ck_ref[...], sk_ref[...])


def _rope(qkv):
    cq, sq = _rope_tables(A_Q)
    ck, sk = _rope_tables(A_KV)
    tab = lambda w: pl.BlockSpec((DEC_SEQ, w), lambda b: (0, 0))
    row0 = T_CTX // DEC_SEQ
    return pl.pallas_call(
        _rope_body,
        grid=(DEC_BATCH,),
        in_specs=[pl.BlockSpec((DEC_SEQ, A_Q), lambda b: (row0 + b, 0)),
                  pl.BlockSpec((DEC_SEQ, A_KV), lambda b: (row0 + b, A_Q // A_KV)),
                  tab(A_Q), tab(A_Q), tab(A_KV), tab(A_KV)],
        out_specs=[pl.BlockSpec((DEC_SEQ, A_Q), lambda b: (b, 0)),
                   pl.BlockSpec((DEC_SEQ, A_KV), lambda b: (b, 0))],
        out_shape=[jax.ShapeDtypeStruct((T_LAT, A_Q), F32), jax.ShapeDtypeStruct((T_LAT, A_KV), F32)],
        compiler_params=_cparams("parallel"),
        name="rope",
    )(qkv, qkv, jnp.asarray(cq), jnp.asarray(sq), jnp.asarray(ck), jnp.asarray(sk))


WIN_QB = 256


def _pick_head(x, h, n_heads):
    out = x[:, _head_cols(0)]
    for i in range(1, n_heads):
        out = jnp.where(h == i, x[:, _head_cols(i)], out)
    return out


def _win_attn_body(qraw_ref, qrot_ref, krot_ref, v_ref, kc_ref, vc_ref, sink_ref, o_ref):
    scale = HEAD_DIM ** -0.5
    hk = pl.program_id(1)
    k = _pick_head(krot_ref[...], hk, A_KV_HEADS)
    v = _pick_head(v_ref[...], hk, A_KV_HEADS)
    kc = _pick_head(kc_ref[0], hk, A_KV_HEADS)
    vc = _pick_head(vc_ref[0], hk, A_KV_HEADS)
    head_lane = lax.broadcasted_iota(jnp.int32, (1, A_HEADS), 1)
    sinks = [jnp.sum(jnp.where(head_lane == hk * A_GROUP + g, sink_ref[...], 0.0), axis=-1, keepdims=True)
             for g in range(A_GROUP)]
    sink = jnp.concatenate([jnp.broadcast_to(s, (WIN_QB, 1)) for s in sinks], axis=0)
    for qb in range(DEC_SEQ // WIN_QB):
        q0 = qb * WIN_QB
        rows = slice(q0, q0 + WIN_QB)
        lo = max(0, q0 - WINDOW)
        hi = min(DEC_SEQ, q0 + WIN_QB + WINDOW)
        q_rot = jnp.concatenate([qrot_ref[rows, _head_cols(g)] for g in range(A_GROUP)], axis=0)
        q_raw = jnp.concatenate([qraw_ref[rows, _head_cols(g)] for g in range(A_GROUP)], axis=0)
        s_loc = _dot_nt(q_rot, k[lo:hi]) * scale
        qpos = q0 + (lax.broadcasted_iota(jnp.int32, s_loc.shape, 0) & (WIN_QB - 1))
        kpos = lo + lax.broadcasted_iota(jnp.int32, s_loc.shape, 1)
        s_loc = jnp.where(jnp.abs(kpos - qpos) <= WINDOW, s_loc, MASK_NEG)
        s_ctx = _dot_nt(q_raw, kc) * scale
        m = jnp.maximum(jnp.maximum(jnp.max(s_loc, axis=-1, keepdims=True),
                                    jnp.max(s_ctx, axis=-1, keepdims=True)), sink)
        p_loc = jnp.exp(s_loc - m)
        p_ctx = jnp.exp(s_ctx - m)
        den = (jnp.sum(p_loc, axis=-1, keepdims=True) + jnp.sum(p_ctx, axis=-1, keepdims=True)
               + jnp.exp(sink - m))
        o = (_dot(p_ctx, vc) + _dot(p_loc, v[lo:hi])) / den
        for g in range(A_GROUP):
            o_ref[rows, _head_cols(g)] = o[g * WIN_QB:(g + 1) * WIN_QB]


def _win_attn(qkv, q_rot, k_rot, kc, vc, sink):
    row0 = T_CTX // DEC_SEQ
    gw = A_GROUP * HEAD_DIM
    return pl.pallas_call(
        _win_attn_body,
        grid=(DEC_BATCH, A_KV_HEADS),
        in_specs=[pl.BlockSpec((DEC_SEQ, gw), lambda b, h: (row0 + b, h)),
                  pl.BlockSpec((DEC_SEQ, gw), lambda b, h: (b, h)),
                  pl.BlockSpec((DEC_SEQ, A_KV), lambda b, h: (b, 0)),
                  pl.BlockSpec((DEC_SEQ, A_KV), lambda b, h: (row0 + b, (A_Q + A_KV) // A_KV)),
                  pl.BlockSpec((1, PAST_LEN, A_KV), lambda b, h: (b, 0, 0)),
                  pl.BlockSpec((1, PAST_LEN, A_KV), lambda b, h: (b, 0, 0)),
                  pl.BlockSpec((1, A_HEADS), lambda b, h: (0, 0))],
        out_specs=pl.BlockSpec((DEC_SEQ, gw), lambda b, h: (b, h)),
        out_shape=jax.ShapeDtypeStruct((T_LAT, A_Q), F32),
        compiler_params=_cparams("parallel", "parallel"),
        name="win_attn",
    )(qkv, q_rot, k_rot, qkv, kc, vc, sink.reshape(1, A_HEADS))


GRID_ROWS = DEC_SEQ // GRID_W
NA_BAND = min(NA_ROWS, GRID_ROWS)


NA_REL_ROWS = 2 * NA_ROWS - 1
NA_REL_COLS = 2 * NA_COLS - 1
LANES = 128


def _na_rel_rows(rpb):
    pad = jnp.zeros((B_HEADS, NA_REL_ROWS, GRID_W - NA_REL_COLS), F32)
    one = jnp.concatenate([rpb, pad], axis=-1)
    nxt = jnp.concatenate([one[:, 1:], jnp.zeros((B_HEADS, 1, GRID_W), F32)], axis=1)
    both = jnp.concatenate([one, nxt], axis=-1)
    return jnp.concatenate([both, jnp.zeros((B_HEADS, 16 - NA_REL_ROWS, LANES), F32)], axis=1)


NA_HEADS_PER_STEP = LANES // HEAD_DIM


def _na_row_groups():
    groups = []
    for r in range(GRID_ROWS):
        rs = min(max(r - NA_ROWS // 2, 0), GRID_ROWS - NA_BAND)
        if groups and groups[-1][2] == rs:
            groups[-1][1] += 1
        else:
            groups.append([r, 1, rs])
    return groups


def _na_attn_body(q_ref, k_ref, v_ref, kc_ref, vc_ref, rel_ref, o_ref):
    scale = HEAD_DIM ** -0.5
    cq = lax.broadcasted_iota(jnp.int32, (GRID_W, LANES), 0)
    kcol = lax.broadcasted_iota(jnp.int32, (GRID_W, LANES), 1) & (GRID_W - 1)
    cs = jnp.clip(cq - NA_COLS // 2, 0, GRID_W - NA_COLS)
    col_ok = (kcol >= cs) & (kcol < cs + NA_COLS)
    for hh in range(NA_HEADS_PER_STEP):
        cols = _head_cols(hh)
        kc = kc_ref[0, :, cols]
        vc = vc_ref[0, :, cols]
        tiles = {}

        def pair_tile(a):
            if a not in tiles:
                x = jnp.broadcast_to(rel_ref[hh, a:a + 1, :], (GRID_W, LANES))
                t = pltpu.roll(x, LANES - (NA_COLS - 1), axis=1, stride=1, stride_axis=0)
                tiles[a] = jnp.where(col_ok, t, MASK_NEG)
            return tiles[a]

        for r0, n_r, rs in _na_row_groups():
            bias = jnp.concatenate(
                [jnp.concatenate([pair_tile(rs - r + NA_ROWS - 1 + 2 * i) for i in range(NA_BAND // 2)], axis=1)
                 for r in range(r0, r0 + n_r)], axis=0)
            rows = slice(r0 * GRID_W, (r0 + n_r) * GRID_W)
            band = slice(rs * GRID_W, (rs + NA_BAND) * GRID_W)
            q = q_ref[rows, cols]
            s_loc = _dot_nt(q, k_ref[band, cols]) * scale + bias
            s_ctx = _dot_nt(q, kc) * scale
            m = jnp.maximum(jnp.max(s_loc, axis=-1, keepdims=True), jnp.max(s_ctx, axis=-1, keepdims=True))
            p_loc = jnp.exp(s_loc - m)
            p_ctx = jnp.exp(s_ctx - m)
            den = jnp.sum(p_loc, axis=-1, keepdims=True) + jnp.sum(p_ctx, axis=-1, keepdims=True)
            o_ref[rows, cols] = (_dot(p_ctx, vc) + _dot(p_loc, v_ref[band, cols])) / den


def _na_attn(qkv, kc, vc, rel):
    row0 = T_CTX // DEC_SEQ
    col0 = (A_Q + 2 * A_KV) // LANES
    n_blk = B_W // LANES
    col = lambda j: pl.BlockSpec((DEC_SEQ, LANES), lambda b, p: (row0 + b, col0 + j * n_blk + p))
    cache = pl.BlockSpec((1, PAST_LEN, LANES), lambda b, p: (b, 0, p))
    return pl.pallas_call(
        _na_attn_body,
        grid=(DEC_BATCH, n_blk),
        in_specs=[col(0), col(1), col(2), cache, cache,
                  pl.BlockSpec((NA_HEADS_PER_STEP, 16, LANES), lambda b, p: (p, 0, 0))],
        out_specs=pl.BlockSpec((DEC_SEQ, LANES), lambda b, p: (b, p)),
        out_shape=jax.ShapeDtypeStruct((T_LAT, B_W), F32),
        compiler_params=_cparams("parallel", "parallel"),
        name="na_attn",
    )(qkv, qkv, qkv, kc, vc, rel)


@functools.lru_cache(maxsize=None)
def _dft_mats(L):
    n = 2 * L
    fc = min(L, DFT_CHUNK)
    f = np.arange(L)[:, None]
    t = np.arange(L)[None, :]
    ang = 2.0 * np.pi * ((f * t) % n) / n
    m1 = np.cos(ang)
    m2 = np.sin(ang)
    m2[0, :] = np.where(np.arange(L) % 2 == 0, 1.0, -1.0)
    wgt = np.full((L, 1), 2.0)
    wgt[0, 0] = 1.0
    nch = L // fc
    fwd = np.concatenate([m1.reshape(nch, fc, L), m2.reshape(nch, fc, L)], axis=1)
    inv = np.concatenate([(m1 * wgt / n).reshape(nch, fc, L), (m2 * wgt / n).reshape(nch, fc, L)], axis=1)
    inv = np.transpose(inv, (0, 2, 1))
    return fwd.astype(np.float32), inv.astype(np.float32)


@functools.lru_cache(maxsize=None)
def _filter_consts(L):
    t = np.linspace(0.0, 1.0, L)[:, None]
    bands = (C_EMB - 1) // 2
    ang = (2.0 * math.pi / L) * np.arange(L)[:, None] * np.linspace(1e-4, bands - 1, bands)[None, :]
    z = np.concatenate([t, np.cos(ang), -np.sin(ang)], axis=-1)
    zpad = np.zeros((L, 128))
    zpad[:, :C_EMB] = z
    deltas = np.abs(np.linspace(HYENA_MIN_DECAY, HYENA_MAX_DECAY, C_DIM))
    window = np.exp(-t * deltas[None, :])
    return zpad.astype(np.float32), window.astype(np.float32)


def _filter_body(z_ref, w1_ref, b1_ref, w2_ref, b2_ref, w3_ref, b3_ref, fr_ref, w4_ref, win_ref, fm_ref,
                 hr_ref, g_ref, hq_ref, hs_scr, hd_scr):
    c = pl.program_id(0)
    fc = hr_ref.shape[0]

    @pl.when(c == 0)
    def _():
        fr = fr_ref[...]
        hh = jnp.sin(fr * (_dot_hi(z_ref[...], w1_ref[...]) + b1_ref[...]))
        hh = jnp.sin(fr * (_dot_hi(hh, w2_ref[...]) + b2_ref[...]))
        hh = jnp.sin(fr * (_dot_hi(hh, w3_ref[...]) + b3_ref[...]))
        hh = _dot_hi(hh, w4_ref[...])
        hf = hh[:, :C_DIM] * win_ref[...]
        hb = hh[:, C_DIM:] * win_ref[...]
        hs_scr[...] = hf + hb
        hd_scr[...] = hf - hb

    fm = fm_ref[0]
    hr = _dot_hi(fm[:fc], hs_scr[...])
    first = (lax.broadcasted_iota(jnp.int32, (fc, C_DIM), 0) == 0) & (c == 0)
    hr_ref[...] = hr
    g_ref[...] = jnp.where(first, 0.0, _dot_hi(fm[fc:], hd_scr[...]))
    hs = hs_scr[...]
    sign = jnp.where((lax.broadcasted_iota(jnp.int32, hs.shape, 0) & 1) == 0, 1.0, -1.0)
    hq_ref[...] = jnp.where(first, jnp.sum(hs * sign, axis=0, keepdims=True), hr)


def _hyena_filter(L, filt):
    w1, b1, w2, b2, w3, b3, freq, w4 = filt
    zpad, window = _filter_consts(L)
    fwd, _ = _dft_mats(L)
    nch, fc2, _ = fwd.shape
    fc = fc2 // 2
    w1p = jnp.pad(w1, ((0, 128 - C_EMB), (0, 0)))
    full = lambda shape: pl.BlockSpec(shape, lambda c: tuple(0 for _ in shape))
    out_spec = pl.BlockSpec((fc, C_DIM), lambda c: (c, 0))
    out_sd = jax.ShapeDtypeStruct((L, C_DIM), F32)
    return pl.pallas_call(
        _filter_body,
        grid=(nch,),
        in_specs=[full((L, 128)), full((128, C_FFN)), full((1, C_FFN)), full((C_FFN, C_FFN)), full((1, C_FFN)),
                  full((C_FFN, C_FFN)), full((1, C_FFN)), full((1, C_FFN)), full((C_FFN, 2 * C_DIM)),
                  full((L, C_DIM)), pl.BlockSpec((1, fc2, L), lambda c: (c, 0, 0))],
        out_specs=[out_spec, out_spec, out_spec],
        out_shape=[out_sd, out_sd, out_sd],
        scratch_shapes=[pltpu.VMEM((L, C_DIM), F32), pltpu.VMEM((L, C_DIM), F32)],
        compiler_params=_cparams("arbitrary"),
        name="hyena_filter",
    )(jnp.asarray(zpad), w1p, b1.reshape(1, C_FFN), w2, b2.reshape(1, C_FFN), w3, b3.reshape(1, C_FFN),
      freq.reshape(1, C_FFN), w4, jnp.asarray(window), jnp.asarray(fwd))


def _hyena_body(u_ref, cw_ref, cb_ref, d_ref, fm_ref, fi_ref, hr_ref, g_ref, hq_ref, y_ref,
                x0_scr, z_scr, acc_scr):
    c = pl.program_id(1)
    L = y_ref.shape[0]
    fc = hr_ref.shape[0]

    @pl.when(c == 0)
    def _():
        row = lax.broadcasted_iota(jnp.int32, (L, C_DIM), 0)

        def short_conv(sec):
            cols = slice(sec * C_DIM, (sec + 1) * C_DIM)
            u = u_ref[:, cols]
            prev = jnp.where(row == 0, 0.0, pltpu.roll(u, 1, axis=0))
            nxt = jnp.where(row == L - 1, 0.0, pltpu.roll(u, L - 1, axis=0))
            return (prev * cw_ref[0:1, cols] + u * cw_ref[1:2, cols] + nxt * cw_ref[2:3, cols]
                    + cb_ref[:, cols])

        x0_scr[...] = short_conv(0)
        z_scr[...] = short_conv(1) * short_conv(2)
        acc_scr[...] = jnp.zeros((L, C_DIM), F32)

    ab = _dot_hi(fm_ref[0], z_scr[...])
    a, b = ab[:fc], ab[fc:]
    hr, g, hq = hr_ref[...], g_ref[...], hq_ref[...]
    pq = jnp.concatenate([a * hr - b * g, a * g + b * hq], axis=0)
    acc_scr[...] += _dot_hi(fi_ref[0], pq)

    @pl.when(c == pl.num_programs(1) - 1)
    def _():
        y_ref[...] = x0_scr[...] * (acc_scr[...] + z_scr[...] * d_ref[...])


def _hyena(u, row_blk0, n_seq, L, conv_w, conv_b, d_skip, spec):
    hr, g, hq = spec
    fwd, inv = _dft_mats(L)
    nch, fc2, _ = fwd.shape
    fc = fc2 // 2
    u_w = 3 * C_DIM
    return pl.pallas_call(
        _hyena_body,
        grid=(n_seq, nch),
        in_specs=[pl.BlockSpec((L, u_w), lambda b, c: (row_blk0 + b, 0)),
                  pl.BlockSpec((3, u_w), lambda b, c: (0, 0)),
                  pl.BlockSpec((1, u_w), lambda b, c: (0, 0)),
                  pl.BlockSpec((1, C_DIM), lambda b, c: (0, 0)),
                  pl.BlockSpec((1, fc2, L), lambda b, c: (c, 0, 0)),
                  pl.BlockSpec((1, L, fc2), lambda b, c: (c, 0, 0)),
                  pl.BlockSpec((fc, C_DIM), lambda b, c: (c, 0)),
                  pl.BlockSpec((fc, C_DIM), lambda b, c: (c, 0)),
                  pl.BlockSpec((fc, C_DIM), lambda b, c: (c, 0))],
        out_specs=pl.BlockSpec((L, C_DIM), lambda b, c: (b, 0)),
        out_shape=jax.ShapeDtypeStruct((n_seq * L, C_DIM), F32),
        scratch_shapes=[pltpu.VMEM((L, C_DIM), F32)] * 3,
        compiler_params=_cparams("parallel", "arbitrary"),
        name="hyena",
    )(u, conv_w, conv_b.reshape(1, u_w), d_skip.reshape(1, C_DIM), jnp.asarray(fwd), jnp.asarray(inv), hr, g, hq)


def _hgrn_body(q_ref, ff_ref, fb_ref, i_ref, g_ref, lbf_ref, lbb_ref, nd_ref, s0f_ref, s0b_ref,
               o_ref, sf_ref, sb_ref, o_scr, *, layer):
    L = o_ref.shape[0]
    C = GLA_CHUNK
    nc = L // C
    mid = C // 2
    q = _silu(q_ref[...])
    v = i_ref[...]

    def lower_bound(ref):
        gm = ref[...]
        e = jnp.exp(gm - jnp.max(gm, axis=0, keepdims=True))
        p = e / jnp.sum(e, axis=0, keepdims=True)
        return jnp.sum(p[0:layer + 1], axis=0, keepdims=True) - p[0:1]

    def gates(fx, lb):
        f = lb + (1.0 - lb) * jax.nn.sigmoid(fx)
        return 1.0 - f, jnp.log(f)

    kf, lgf = gates(ff_ref[...], lower_bound(lbf_ref))
    kb, lgb = gates(fb_ref[...], lower_bound(lbb_ref))
    ti = lax.broadcasted_iota(jnp.int32, (C, C), 0)
    si = lax.broadcasted_iota(jnp.int32, (C, C), 1)
    causal = si <= ti
    anti = si >= ti
    tril = causal.astype(F32)
    triu = anti.astype(F32)

    st = jnp.transpose(s0f_ref[0, 0])
    for n in range(nc):
        sl = slice(n * C, (n + 1) * C)
        b = _dot_hi(tril, lgf[sl])
        btot = b[C - 1:C]
        ref = b[mid:mid + 1]
        qc, kc, vc = q[sl], kf[sl], v[sl]
        sc = jnp.where(causal, _dot_nt(qc * jnp.exp(b - ref), kc * jnp.exp(ref - b)), 0.0)
        o_scr[sl, :] = _dot(sc, vc) + _dot_nt(qc * jnp.exp(b), st)
        st = st * jnp.exp(btot) + _dot_tn(vc, kc * jnp.exp(btot - b))
    sf_ref[0, 0] = jnp.transpose(st)

    st = jnp.transpose(s0b_ref[0, 0])
    for n in reversed(range(nc)):
        sl = slice(n * C, (n + 1) * C)
        b = _dot_hi(triu, lgb[sl])
        btot = b[0:1]
        ref = b[mid:mid + 1]
        qc, kc, vc = q[sl], kb[sl], v[sl]
        sc = jnp.where(anti, _dot_nt(qc * jnp.exp(b - ref), kc * jnp.exp(ref - b)), 0.0)
        o_scr[sl, :] += _dot(sc, vc) + _dot_nt(qc * jnp.exp(b), st)
        st = st * jnp.exp(btot) + _dot_tn(vc, kc * jnp.exp(btot - b))
    sb_ref[0, 0] = jnp.transpose(st)

    o_ref[...] = _rms(o_scr[...], nd_ref[...]) * _silu(g_ref[...])


def _hgrn(u, row_blk0, n_seq, L, lb_fwd, lb_bwd, norm_d, s0f, s0b, layer):
    col0 = 3 * C_DIM // D_KDIM
    col = lambda j: pl.BlockSpec((L, D_KDIM), lambda b, h: (row_blk0 + b, col0 + j * D_HEADS + h))
    lbs = pl.BlockSpec((DEPTH, D_KDIM), lambda b, h: (0, h))
    st = pl.BlockSpec((1, 1, D_KDIM, D_VDIM), lambda b, h: (b, h, 0, 0))
    st_sd = jax.ShapeDtypeStruct((n_seq, D_HEADS, D_KDIM, D_VDIM), F32)
    return pl.pallas_call(
        functools.partial(_hgrn_body, layer=layer),
        grid=(n_seq, D_HEADS),
        in_specs=[col(0), col(1), col(2), col(3), col(4), lbs, lbs,
                  pl.BlockSpec((1, D_VDIM), lambda b, h: (0, 0)), st, st],
        out_specs=[pl.BlockSpec((L, D_VDIM), lambda b, h: (b, h)), st, st],
        out_shape=[jax.ShapeDtypeStruct((n_seq * L, D_HEADS * D_VDIM), F32), st_sd, st_sd],
        scratch_shapes=[pltpu.VMEM((L, D_VDIM), F32)],
        compiler_params=_cparams("parallel", "parallel"),
        name="hgrn",
    )(u, u, u, u, u, lb_fwd, lb_bwd, norm_d.reshape(1, D_VDIM), s0f, s0b)


def _pack_bf16_pairs(h):
    n = h.shape[1] // 2
    hi = lax.bitcast_convert_type(h[:, :n].astype(BF16).astype(F32), jnp.int32)
    lo = lax.bitcast_convert_type(h[:, n:].astype(BF16).astype(F32), jnp.int32)
    return hi | lax.shift_right_logical(lo, 16)


def _unpack_bf16_pairs(p):
    hi = lax.bitcast_convert_type(p & jnp.int32(-65536), F32).astype(BF16)
    lo = lax.bitcast_convert_type(lax.shift_left(p, 16), F32).astype(BF16)
    return hi, lo


def _outproj_body(*refs, n_x):
    a_refs, b_refs, x_refs = refs[0:2], refs[2:4], refs[4:4 + n_x]
    mod_ref, gf_ref, w_ref, wr_ref, rb_ref, x1_ref, h2_ref, chosen_ref, gk_ref, ik_ref = refs[4 + n_x:]
    m = mod_ref[0]
    half = a_refs[0].shape[1]
    out = _dot(_token_tile(a_refs), w_ref[0:half, :]) + _dot(_token_tile(b_refs), w_ref[half:, :])
    x1 = _token_tile(x_refs) + m[:, 2 * D_MODEL:3 * D_MODEL] * out
    x1_ref[...] = x1
    h2 = _rms(x1, gf_ref[...]) * (1.0 + m[:, 4 * D_MODEL:5 * D_MODEL]) + m[:, 3 * D_MODEL:4 * D_MODEL]
    h2_ref[...] = _pack_bf16_pairs(h2)
    scores = jax.nn.sigmoid(_dot_hi(h2, wr_ref[...]))
    work = scores + rb_ref[...]
    lane = lax.broadcasted_iota(jnp.int32, work.shape, 1).astype(F32)
    slot = lax.broadcasted_iota(jnp.int32, (work.shape[0], LANES), 1)
    chosen = jnp.zeros(work.shape, F32)
    gk = jnp.zeros((work.shape[0], LANES), F32)
    ik = jnp.zeros((work.shape[0], LANES), F32)
    for k in range(TOP_K):
        best = jnp.max(work, axis=-1, keepdims=True)
        first = jnp.min(jnp.where(work == best, lane, float(N_EXPERTS)), axis=-1, keepdims=True)
        hit = lane == first
        chosen = jnp.where(hit, 1.0, chosen)
        gk = jnp.where(slot == k, jnp.sum(jnp.where(hit, scores, 0.0), axis=-1, keepdims=True), gk)
        ik = jnp.where(slot == k, first, ik)
        work = jnp.where(hit, -jnp.inf, work)
    chosen_ref[...] = chosen
    gk_ref[...] = gk / jnp.sum(gk, axis=-1, keepdims=True) * ROUTE_SCALE
    ik_ref[...] = ik


def _outproj(a, b, x, mod_l, gain_ffn, w_out_bf16, w_router, router_bias):
    half = a[0].shape[1]
    a_specs, a_args = _token_specs(a, half)
    b_specs, b_args = _token_specs(b, half)
    x_specs, x_args = _token_specs(x, D_MODEL)
    return pl.pallas_call(
        functools.partial(_outproj_body, n_x=len(x_args)),
        grid=(T_ALL // TM,),
        in_specs=a_specs + b_specs + x_specs + [
                  pl.BlockSpec((1, 1, N_MOD * D_MODEL), lambda i: (_mod_row(i), 0, 0)),
                  pl.BlockSpec((1, D_MODEL), lambda i: (0, 0)),
                  pl.BlockSpec((2 * half, D_MODEL), lambda i: (0, 0)),
                  pl.BlockSpec((D_MODEL, N_EXPERTS), lambda i: (0, 0)),
                  pl.BlockSpec((1, N_EXPERTS), lambda i: (0, 0))],
        out_specs=[pl.BlockSpec((TM, D_MODEL), lambda i: (i, 0)),
                   pl.BlockSpec((TM, D_MODEL // 2), lambda i: (i, 0)),
                   pl.BlockSpec((TM, N_EXPERTS), lambda i: (i, 0)),
                   pl.BlockSpec((TM, LANES), lambda i: (i, 0)),
                   pl.BlockSpec((TM, LANES), lambda i: (i, 0))],
        out_shape=[jax.ShapeDtypeStruct((T_ALL, D_MODEL), F32),
                   jax.ShapeDtypeStruct((T_ALL, D_MODEL // 2), jnp.int32),
                   jax.ShapeDtypeStruct((T_ALL, N_EXPERTS), F32),
                   jax.ShapeDtypeStruct((T_ALL, LANES), F32),
                   jax.ShapeDtypeStruct((T_ALL, LANES), F32)],
        compiler_params=_cparams("parallel"),
        name="outproj_router",
    )(*a_args, *b_args, *x_args, mod_l, gain_ffn.reshape(1, D_MODEL), w_out_bf16, w_router,
      router_bias.reshape(1, N_EXPERTS))


def _route_body(chosen_ref, ik_ref, dest_ref, blk_ref, used_ref, pos_scr):
    n_tiles = T_ALL // TM
    r = lax.broadcasted_iota(jnp.int32, (TM, TM), 0)
    c = lax.broadcasted_iota(jnp.int32, (TM, TM), 1)
    before = (c < r).astype(BF16)

    def count_tile(i, carry):
        rows = pl.ds(pl.multiple_of(i * TM, TM), TM)
        m = chosen_ref[rows, :]
        pos_scr[rows, :] = jnp.dot(before, m.astype(BF16), preferred_element_type=F32) + carry
        return carry + jnp.sum(m, axis=0, keepdims=True)

    counts = lax.fori_loop(0, n_tiles, count_tile, jnp.zeros((1, N_EXPERTS), F32))
    padded = jnp.ceil(counts * (1.0 / MOE_BLK)) * MOE_BLK
    ei = lax.broadcasted_iota(jnp.int32, (N_EXPERTS, N_EXPERTS), 0)
    ej = lax.broadcasted_iota(jnp.int32, (N_EXPERTS, N_EXPERTS), 1)
    end = _dot_hi(jnp.broadcast_to(padded, (8, N_EXPERTS)), (ei <= ej).astype(F32))[0:1]
    start = end - padded

    lane = lax.broadcasted_iota(jnp.int32, (TM, N_EXPERTS), 1).astype(F32)
    slot = lax.broadcasted_iota(jnp.int32, (TM, LANES), 1)

    def dest_tile(i, carry):
        rows = pl.ds(pl.multiple_of(i * TM, TM), TM)
        row_of = pos_scr[rows, :] + start
        ik = ik_ref[rows, :]
        acc = jnp.zeros((TM, LANES), F32)
        for k in range(TOP_K):
            pick = jnp.sum(jnp.where(lane == ik[:, k:k + 1], row_of, 0.0), axis=-1, keepdims=True)
            acc = jnp.where(slot == k, pick, acc)
        dest_ref[rows, :] = acc.astype(jnp.int32)
        return carry

    lax.fori_loop(0, n_tiles, dest_tile, 0)
    blk_start = (lax.broadcasted_iota(jnp.int32, (MOE_NBLK, N_EXPERTS), 0) * MOE_BLK).astype(F32)
    owner = jnp.sum((end <= blk_start).astype(F32), axis=-1, keepdims=True)
    blk_ref[...] = jnp.broadcast_to(jnp.minimum(owner, N_EXPERTS - 1.0), (MOE_NBLK, LANES)).astype(jnp.int32)
    used = end[:, N_EXPERTS - 1:N_EXPERTS] * (1.0 / MOE_BLK)
    used_ref[...] = jnp.broadcast_to(used, (8, LANES)).astype(jnp.int32)


def _route(chosen, ik):
    full = lambda shape: pl.BlockSpec(shape, lambda i: (0, 0))
    return pl.pallas_call(
        _route_body,
        grid=(1,),
        in_specs=[full((T_ALL, N_EXPERTS)), full((T_ALL, LANES))],
        out_specs=[full((T_ALL, LANES)), full((MOE_NBLK, LANES)), full((8, LANES))],
        out_shape=[jax.ShapeDtypeStruct((T_ALL, LANES), jnp.int32),
                   jax.ShapeDtypeStruct((MOE_NBLK, LANES), jnp.int32),
                   jax.ShapeDtypeStruct((8, LANES), jnp.int32)],
        scratch_shapes=[pltpu.VMEM((T_ALL, N_EXPERTS), F32)],
        compiler_params=_cparams("arbitrary"),
        name="moe_route",
    )(chosen, ik)


def _sc_worker_id():
    return lax.axis_index("s") * SC_CORES + lax.axis_index("c")


def _sc_dispatch(h2p, dest_chunks):
    n_chunks = T_ALL // DISP_CHUNK
    width = h2p.shape[1]
    mesh = plsc.VectorSubcoreMesh(core_axis_name="c", subcore_axis_name="s")

    @functools.partial(
        pl.kernel, mesh=mesh,
        out_type=jax.ShapeDtypeStruct((MOE_ROWS, width), jnp.int32),
        scratch_types=[pltpu.VMEM((TOP_K, DISP_CHUNK), jnp.int32), pltpu.VMEM((DISP_CHUNK, width), jnp.int32)],
    )
    def run(x_hbm, dest_hbm, xs_hbm, idx_v, rows_v):
        wid = _sc_worker_id()
        for rep in range(-(-n_chunks // SC_WORKERS)):
            chunk = wid + rep * SC_WORKERS

            @pl.when(chunk < n_chunks)
            def _():
                pltpu.sync_copy(dest_hbm.at[chunk], idx_v)
                pltpu.sync_copy(x_hbm.at[pl.ds(chunk * DISP_CHUNK, DISP_CHUNK)], rows_v)
                for k in range(TOP_K):
                    pltpu.sync_copy(rows_v, xs_hbm.at[idx_v.at[k]])

    return run(h2p, dest_chunks)


def _sc_collect(y, dest_flat):
    per_worker = T_ALL // SC_WORKERS
    n_chunks = per_worker // COLLECT_CHUNK
    mesh = plsc.VectorSubcoreMesh(core_axis_name="c", subcore_axis_name="s")

    @functools.partial(
        pl.kernel, mesh=mesh,
        out_type=jax.ShapeDtypeStruct((TOP_K * T_ALL, D_MODEL), F32),
        scratch_types=[pltpu.VMEM((COLLECT_CHUNK,), jnp.int32), pltpu.VMEM((COLLECT_CHUNK, D_MODEL), F32),
                       pltpu.SemaphoreType.DMA],
    )
    def run(y_hbm, dest_hbm, yg_hbm, idx_v, rows_v, sem):
        wid = _sc_worker_id()

        @pl.loop(0, TOP_K * n_chunks)
        def _(step):
            k = step // n_chunks
            off = pl.multiple_of(k * T_ALL + wid * per_worker + (step % n_chunks) * COLLECT_CHUNK, COLLECT_CHUNK)
            pltpu.sync_copy(dest_hbm.at[pl.ds(off, COLLECT_CHUNK)], idx_v)
            pltpu.async_copy(y_hbm.at[idx_v], rows_v, sem).wait()
            pltpu.sync_copy(rows_v, yg_hbm.at[pl.ds(off, COLLECT_CHUNK)])

    return run(y, dest_flat)


def _expert_body(blk_ref, used_ref, xs_ref, wg_ref, wu_ref, wd_ref, y_ref):
    i = pl.program_id(0)

    @pl.when(i < used_ref[0])
    def _():
        hi, lo = _unpack_bf16_pairs(xs_ref[...])
        half = D_MODEL // 2

        def proj(w_ref):
            return _dot(hi, w_ref[0, 0:half, :]) + _dot(lo, w_ref[0, half:, :])

        y_ref[...] = _dot(_silu(proj(wg_ref)) * proj(wu_ref), wd_ref[0])

    @pl.when(i >= used_ref[0])
    def _():
        y_ref[...] = jnp.zeros(y_ref.shape, F32)


def _experts(blk_expert, n_used, xs, w_gate, w_up, w_down):
    w_in = pl.BlockSpec((1, D_MODEL, D_EXPERT), lambda i, blk, used: (blk[i], 0, 0))
    grid_spec = pltpu.PrefetchScalarGridSpec(
        num_scalar_prefetch=2,
        grid=(MOE_NBLK,),
        in_specs=[pl.BlockSpec((MOE_BLK, D_MODEL // 2), lambda i, blk, used: (jnp.minimum(i, used[0] - 1), 0)),
                  w_in, w_in,
                  pl.BlockSpec((1, D_EXPERT, D_MODEL), lambda i, blk, used: (blk[i], 0, 0))],
        out_specs=pl.BlockSpec((MOE_BLK, D_MODEL), lambda i, blk, used: (i, 0)),
    )
    return pl.pallas_call(
        _expert_body,
        grid_spec=grid_spec,
        out_shape=jax.ShapeDtypeStruct((MOE_ROWS, D_MODEL), F32),
        compiler_params=_cparams("arbitrary"),
        name="moe_experts",
    )(blk_expert, n_used, xs, w_gate, w_up, w_down)


def _combine_body(x1_ref, h2_ref, yg_ref, gk_ref, mod_ref, sg_ref, su_ref, sd_ref, fn_ref, o_ref, *, final):
    hi, lo = _unpack_bf16_pairs(h2_ref[...])
    half = D_MODEL // 2

    def proj(w_ref):
        return _dot(hi, w_ref[0:half, :]) + _dot(lo, w_ref[half:, :])

    acc = _dot(_silu(proj(sg_ref)) * proj(su_ref), sd_ref[...])
    gk = gk_ref[...]
    for k in range(TOP_K):
        acc = acc + gk[:, k:k + 1] * yg_ref[k]
    m = mod_ref[0]
    y = x1_ref[...] + m[:, 5 * D_MODEL:6 * D_MODEL] * acc
    o_ref[...] = _rms(y, fn_ref[...]) if final else y


def _combine(x1, h2p, yg, gk, mod_l, ws_gate, ws_up, ws_down, final_norm, final):
    tok = lambda shape: pl.BlockSpec(shape, lambda i: (i, 0))
    full = lambda shape: pl.BlockSpec(shape, lambda i: (0, 0))
    return pl.pallas_call(
        functools.partial(_combine_body, final=final),
        grid=(T_ALL // TM,),
        in_specs=[tok((TM, D_MODEL)), tok((TM, D_MODEL // 2)),
                  pl.BlockSpec((TOP_K, TM, D_MODEL), lambda i: (0, i, 0)),
                  tok((TM, LANES)),
                  pl.BlockSpec((1, 1, N_MOD * D_MODEL), lambda i: (_mod_row(i), 0, 0)),
                  full((D_MODEL, D_EXPERT)), full((D_MODEL, D_EXPERT)), full((D_EXPERT, D_MODEL)),
                  full((1, D_MODEL))],
        out_specs=tok((TM, D_MODEL)),
        out_shape=jax.ShapeDtypeStruct((T_ALL, D_MODEL), F32),
        compiler_params=_cparams("parallel"),
        name="moe_combine",
    )(x1, h2p, yg, gk, mod_l, ws_gate, ws_up, ws_down, final_norm.reshape(1, D_MODEL))


def _moe(x1, h2p, chosen, gk, ik, mod_l, w_gate, w_up, w_down, ws_gate, ws_up, ws_down, final_norm, final):
    dest, blk, used = _route(chosen, ik)
    dest = dest[:, :TOP_K]
    dest_chunks = dest.reshape(T_ALL // DISP_CHUNK, DISP_CHUNK, TOP_K).transpose(0, 2, 1)
    xs = _sc_dispatch(h2p, dest_chunks)
    y = _experts(blk[:, 0], used[0, :1], xs, w_gate, w_up, w_down)
    yg = _sc_collect(y, dest.T.reshape(-1)).reshape(TOP_K, T_ALL, D_MODEL)
    return _combine(x1, h2p, yg, gk, mod_l, ws_gate.astype(BF16), ws_up.astype(BF16), ws_down.astype(BF16),
                    final_norm, final)


def kernel(x_prompt, x_sample, cache_a_k, cache_a_v, cache_b_k, cache_b_v, state_d_fwd, state_d_bwd, c, c_ctx, w_ada, b_ada, norm_mix, norm_ffn, w_in_attn, w_out_attn, sink_a, rpb_b, w_in_rec, w_out_rec, conv_w, conv_b, filt_w1, filt_b1, filt_w2, filt_b2, filt_w3, filt_b3, filt_freq, filt_w4, d_skip, lb_fwd, lb_bwd, norm_d, w_router, router_bias, w_gate, w_up, w_down, ws_gate, ws_up, ws_down, final_norm):
    x = (x_prompt.reshape(T_CTX, D_MODEL), x_sample.reshape(T_LAT, D_MODEL))
    cvec = jnp.concatenate([c_ctx[None, :], c, jnp.zeros((CVEC_PAD - N_CVEC, D_MODEL), F32)], axis=0)
    mod = _ada(cvec.T, w_ada, b_ada).reshape(DEPTH, CVEC_PAD, 1, N_MOD * D_MODEL)

    new_kv = None
    new_state = None
    for l in range(DEPTH):
        j = l // 2
        final = l == DEPTH - 1
        if l % 2 == 0:
            qkv = _inproj(x, mod[l], norm_mix[l], w_in_attn[j].astype(BF16))
            oa_ctx, ob_ctx, *new_kv = _ctx_attn(qkv, sink_a[j])
            new_kv = tuple(new_kv)
            q_rot, k_rot = _rope(qkv)
            cache = lambda t: t[:, j].reshape(DEC_BATCH, PAST_LEN, -1)
            oa_lat = _win_attn(qkv, q_rot, k_rot, cache(cache_a_k), cache(cache_a_v), sink_a[j])
            ob_lat = _na_attn(qkv, cache(cache_b_k), cache(cache_b_v), _na_rel_rows(rpb_b[j]))
            mix_a = (oa_ctx, oa_lat)
            mix_b = (ob_ctx, ob_lat)
            w_out = w_out_attn[j]
        else:
            u = _inproj(x, mod[l], norm_mix[l], w_in_rec[j].astype(BF16))
            filt = (filt_w1[j], filt_b1[j], filt_w2[j], filt_b2[j], filt_w3[j], filt_b3[j], filt_freq[j],
                    filt_w4[j])
            y_ctx = _hyena(u, 0, BATCH, SEQ, conv_w[j], conv_b[j], d_skip[j], _hyena_filter(SEQ, filt))
            y_lat = _hyena(u, T_CTX // DEC_SEQ, DEC_BATCH, DEC_SEQ, conv_w[j], conv_b[j], d_skip[j],
                           _hyena_filter(DEC_SEQ, filt))
            zeros = jnp.zeros((BATCH, D_HEADS, D_KDIM, D_VDIM), F32)
            o_ctx, s_f, s_b = _hgrn(u, 0, BATCH, SEQ, lb_fwd, lb_bwd, norm_d[j], zeros, zeros, l)
            o_lat, _, _ = _hgrn(u, T_CTX // DEC_SEQ, DEC_BATCH, DEC_SEQ, lb_fwd, lb_bwd, norm_d[j],
                                state_d_fwd[:, j], state_d_bwd[:, j], l)
            new_state = (s_f[:, None], s_b[:, None])
            mix_a = (y_ctx, y_lat)
            mix_b = (o_ctx, o_lat)
            w_out = w_out_rec[j]
        x1, h2p, chosen, gk, ik = _outproj(mix_a, mix_b, x, mod[l], norm_ffn[l], w_out.astype(BF16), w_router[l],
                                           router_bias[l])
        x = _moe(x1, h2p, chosen, gk, ik, mod[l], w_gate[l], w_up[l], w_down[l], ws_gate[l], ws_up[l],
                 ws_down[l], final_norm, final)

    y_prompt = x[:T_CTX].reshape(BATCH, SEQ, D_MODEL)
    y_sample = x[T_CTX:].reshape(DEC_BATCH, DEC_SEQ, D_MODEL)
    return (y_prompt, y_sample) + new_kv + new_state
```

```python
import functools
import math

import numpy as np
import jax
import jax.numpy as jnp
from jax import lax
from jax.experimental import pallas as pl
from jax.experimental.pallas import tpu as pltpu
from jax.experimental.pallas import tpu_sc as plsc

F32 = jnp.float32
BF16 = jnp.bfloat16
HI = lax.Precision.HIGHEST

D_MODEL = 1024
BATCH = 16
SEQ = 256
DEPTH = 2
DEC_BATCH = 2
DEC_SEQ = 1024
PAST_LEN = 512
GRID_W = 64
HEAD_DIM = 64
N_MOD = 6
RMS_EPS = 1e-6
A_HEADS = 8
A_KV_HEADS = 2
A_GROUP = A_HEADS // A_KV_HEADS
WINDOW = 128
ROPE_BASE = 10000.0
B_HEADS = 8
NA_ROWS = 8
NA_COLS = 16
C_DIM = 512
C_EMB = 33
C_FFN = 64
HYENA_MIN_DECAY = math.log(1e-2) / 1.5
HYENA_MAX_DECAY = math.log(1e-2) / 0.3
D_KDIM = 128
D_VDIM = 128
D_HEADS = 4
N_EXPERTS = 64
TOP_K = 8
D_EXPERT = 256
ROUTE_SCALE = 2.5
A_Q = A_HEADS * HEAD_DIM
A_KV = A_KV_HEADS * HEAD_DIM
B_W = B_HEADS * HEAD_DIM
ATTN_IN = A_Q + 2 * A_KV + 3 * B_W
REC_IN = 3 * C_DIM + 5 * D_HEADS * D_KDIM

T_CTX = BATCH * SEQ
T_LAT = DEC_BATCH * DEC_SEQ
T_ALL = T_CTX + T_LAT
N_CVEC = 1 + DEC_BATCH
CVEC_PAD = 8
TM = 256
MASK_NEG = -1e30
GLA_CHUNK = 64
DFT_CHUNK = 256
MOE_BLK = 256
MOE_NBLK = -(-(T_ALL * TOP_K + N_EXPERTS * (MOE_BLK - 1)) // MOE_BLK)
MOE_ROWS = MOE_NBLK * MOE_BLK
SC_CORES = 2
SC_SUBCORES = 16
SC_WORKERS = SC_CORES * SC_SUBCORES
DISP_CHUNK = 128
COLLECT_CHUNK = 64
VMEM_LIMIT = 56 * 1024 * 1024


def _cparams(*sem):
    return pltpu.CompilerParams(dimension_semantics=sem, vmem_limit_bytes=VMEM_LIMIT)


def _mod_row(i):
    return jnp.where(i < T_CTX // TM, 0, 1 + (i - T_CTX // TM) // (DEC_SEQ // TM))


def _dot(a, b):
    return jnp.dot(a.astype(BF16), b.astype(BF16), preferred_element_type=F32)


def _dot_nt(a, b):
    return lax.dot_general(a.astype(BF16), b.astype(BF16), (((1,), (1,)), ((), ())),
                           preferred_element_type=F32)


def _dot_tn(a, b):
    return lax.dot_general(a.astype(BF16), b.astype(BF16), (((0,), (0,)), ((), ())),
                           preferred_element_type=F32)


def _dot_hi(a, b):
    return jnp.dot(a, b, precision=HI, preferred_element_type=F32)


def _silu(x):
    return x * jax.nn.sigmoid(x)


def _rms(x, g):
    return x * lax.rsqrt(jnp.mean(x * x, axis=-1, keepdims=True) + RMS_EPS) * g


ADA_TN = 1536
ADA_UNROLL = 4


def _ada_body(cb_ref, w_ref, b_ref, o_ref):
    tn = o_ref.shape[-1]
    n_slab = tn // LANES

    def step(k8, accs):
        r0 = pl.multiple_of(k8 * 8, 8)
        sk = [_silu(cb_ref[j, pl.ds(r0, 8), :]) for j in range(N_CVEC)]
        out = []
        for s in range(n_slab):
            wk = w_ref[0, pl.ds(r0, 8), s * LANES:(s + 1) * LANES]
            out.extend(accs[s * N_CVEC + j] + wk * sk[j] for j in range(N_CVEC))
        return tuple(out)

    accs = lax.fori_loop(0, D_MODEL // 8, step,
                         tuple(jnp.zeros((8, LANES), F32) for _ in range(n_slab * N_CVEC)), unroll=ADA_UNROLL)
    o_ref[0] = jnp.zeros((CVEC_PAD, tn), F32)
    for s in range(n_slab):
        for j in range(N_CVEC):
            o_ref[0, j:j + 1, s * LANES:(s + 1) * LANES] = (
                jnp.sum(accs[s * N_CVEC + j], axis=0, keepdims=True) + b_ref[0, :, s * LANES:(s + 1) * LANES])


def _ada(cvec, w_ada, b_ada):
    n_out = N_MOD * D_MODEL
    c_lanes = jnp.broadcast_to(cvec[:, :, None], (N_CVEC, D_MODEL, LANES))
    return pl.pallas_call(
        _ada_body,
        grid=(DEPTH, n_out // ADA_TN),
        in_specs=[pl.BlockSpec((N_CVEC, D_MODEL, LANES), lambda l, n: (0, 0, 0)),
                  pl.BlockSpec((1, D_MODEL, ADA_TN), lambda l, n: (l, 0, n)),
                  pl.BlockSpec((1, 1, ADA_TN), lambda l, n: (l, 0, n))],
        out_specs=pl.BlockSpec((1, CVEC_PAD, ADA_TN), lambda l, n: (l, 0, n)),
        out_shape=jax.ShapeDtypeStruct((DEPTH, CVEC_PAD, n_out), F32),
        compiler_params=_cparams("parallel", "parallel"),
        name="ada",
    )(c_lanes, w_ada, b_ada.reshape(DEPTH, 1, n_out))


N_CTX_TILES = T_CTX // TM


def _token_specs(x, width):
    if not isinstance(x, tuple):
        return [pl.BlockSpec((TM, width), lambda i: (i, 0))], (x,)
    return ([pl.BlockSpec((TM, width), lambda i: (jnp.minimum(i, N_CTX_TILES - 1), 0)),
             pl.BlockSpec((TM, width), lambda i: (jnp.maximum(i - N_CTX_TILES, 0), 0))], x)


def _token_tile(refs):
    if len(refs) == 1:
        return refs[0][...]
    return jnp.where(pl.program_id(0) < N_CTX_TILES, refs[0][...], refs[1][...])


def _inproj_body(*refs, n_x):
    x_refs, (mod_ref, g_ref, w_ref, o_ref) = refs[:n_x], refs[n_x:]
    m = mod_ref[0]
    h = _rms(_token_tile(x_refs), g_ref[...]) * (1.0 + m[:, D_MODEL:2 * D_MODEL]) + m[:, 0:D_MODEL]
    o_ref[...] = _dot(h, w_ref[...])


def _inproj(x, mod_l, gain, w_bf16):
    n = w_bf16.shape[1]
    x_specs, x_args = _token_specs(x, D_MODEL)
    return pl.pallas_call(
        functools.partial(_inproj_body, n_x=len(x_args)),
        grid=(T_ALL // TM,),
        in_specs=x_specs + [pl.BlockSpec((1, 1, N_MOD * D_MODEL), lambda i: (_mod_row(i), 0, 0)),
                            pl.BlockSpec((1, D_MODEL), lambda i: (0, 0)),
                            pl.BlockSpec((D_MODEL, n), lambda i: (0, 0))],
        out_specs=pl.BlockSpec((TM, n), lambda i: (i, 0)),
        out_shape=jax.ShapeDtypeStruct((T_ALL, n), F32),
        compiler_params=_cparams("parallel"),
        name="inproj",
    )(*x_args, mod_l, gain.reshape(1, D_MODEL), w_bf16)


def _head_cols(h):
    return slice(h * HEAD_DIM, (h + 1) * HEAD_DIM)


def _group_rows(ref, rows, first_col, sink_ref, hk):
    n = rows.stop - rows.start
    q = jnp.concatenate([ref[rows, first_col + g * HEAD_DIM:first_col + (g + 1) * HEAD_DIM]
                         for g in range(A_GROUP)], axis=0)
    sink = jnp.concatenate([jnp.broadcast_to(sink_ref[:, hk * A_GROUP + g:hk * A_GROUP + g + 1], (n, 1))
                            for g in range(A_GROUP)], axis=0)
    return q, sink


def _ctx_attn_body(qkv_ref, sink_ref, oa_ref, ob_ref, ak_ref, av_ref, bk_ref, bv_ref):
    scale = HEAD_DIM ** -0.5
    rows = slice(0, SEQ)

    def attend(q, k, v, sink):
        s = _dot_nt(q, k) * scale
        m = jnp.max(s, axis=-1, keepdims=True)
        if sink is not None:
            m = jnp.maximum(m, sink)
        p = jnp.exp(s - m)
        den = jnp.sum(p, axis=-1, keepdims=True)
        if sink is not None:
            den = den + jnp.exp(sink - m)
        return _dot(p, v) / den

    for hk in range(A_KV_HEADS):
        k = qkv_ref[:, A_Q + hk * HEAD_DIM:A_Q + (hk + 1) * HEAD_DIM]
        v = qkv_ref[:, A_Q + A_KV + hk * HEAD_DIM:A_Q + A_KV + (hk + 1) * HEAD_DIM]
        ak_ref[0, 0, :, hk, :] = k
        av_ref[0, 0, :, hk, :] = v
        q, sink = _group_rows(qkv_ref, rows, hk * A_GROUP * HEAD_DIM, sink_ref, hk)
        o = attend(q, k, v, sink)
        for g in range(A_GROUP):
            oa_ref[:, _head_cols(hk * A_GROUP + g)] = o[g * SEQ:(g + 1) * SEQ]
    base = A_Q + 2 * A_KV
    for h in range(B_HEADS):
        q = qkv_ref[:, base + h * HEAD_DIM:base + (h + 1) * HEAD_DIM]
        k = qkv_ref[:, base + B_W + h * HEAD_DIM:base + B_W + (h + 1) * HEAD_DIM]
        v = qkv_ref[:, base + 2 * B_W + h * HEAD_DIM:base + 2 * B_W + (h + 1) * HEAD_DIM]
        bk_ref[0, 0, :, h, :] = k
        bv_ref[0, 0, :, h, :] = v
        ob_ref[:, _head_cols(h)] = attend(q, k, v, None)


def _ctx_attn(qkv, sink):
    kv_spec = lambda heads: pl.BlockSpec((1, 1, SEQ, heads, HEAD_DIM), lambda b: (b, 0, 0, 0, 0))
    kv_sd = lambda heads: jax.ShapeDtypeStruct((BATCH, 1, SEQ, heads, HEAD_DIM), F32)
    return pl.pallas_call(
        _ctx_attn_body,
        grid=(BATCH,),
        in_specs=[pl.BlockSpec((SEQ, ATTN_IN), lambda b: (b, 0)),
                  pl.BlockSpec((1, A_HEADS), lambda b: (0, 0))],
        out_specs=[pl.BlockSpec((SEQ, A_Q), lambda b: (b, 0)), pl.BlockSpec((SEQ, B_W), lambda b: (b, 0)),
                   kv_spec(A_KV_HEADS), kv_spec(A_KV_HEADS), kv_spec(B_HEADS), kv_spec(B_HEADS)],
        out_shape=[jax.ShapeDtypeStruct((T_CTX, A_Q), F32), jax.ShapeDtypeStruct((T_CTX, B_W), F32),
                   kv_sd(A_KV_HEADS), kv_sd(A_KV_HEADS), kv_sd(B_HEADS), kv_sd(B_HEADS)],
        compiler_params=_cparams("parallel"),
        name="ctx_attn",
    )(qkv, sink.reshape(1, A_HEADS))


@functools.lru_cache(maxsize=None)
def _rope_tables(width):
    half = HEAD_DIM // 2
    t = np.arange(DEC_SEQ)
    inv = ROPE_BASE ** (-np.arange(0, half, 2, dtype=np.float64) / half)
    ang_r = (t // GRID_W)[:, None] * inv[None, :]
    ang_c = (t % GRID_W)[:, None] * inv[None, :]
    cos = np.concatenate([np.cos(ang_r)] * 2 + [np.cos(ang_c)] * 2, axis=-1)
    sin = np.concatenate([-np.sin(ang_r), np.sin(ang_r), -np.sin(ang_c), np.sin(ang_c)], axis=-1)
    reps = width // HEAD_DIM
    return (np.tile(cos, (1, reps)).astype(np.float32), np.tile(sin, (1, reps)).astype(np.float32))


def _rope_body(q_ref, k_ref, cq_ref, sq_ref, ck_ref, sk_ref, qo_ref, ko_ref):
    quarter = HEAD_DIM // 4

    def rot(x, cos, sin):
        w = x.shape[-1]
        lane = lax.broadcasted_iota(jnp.int32, x.shape, 1)
        fwd = pltpu.roll(x, w - quarter, axis=1)
        bwd = pltpu.roll(x, quarter, axis=1)
        partner = jnp.where((lane & (2 * quarter - 1)) < quarter, fwd, bwd)
        return x * cos + partner * sin

    qo_ref[...] = rot(q_ref[...], cq_ref[...], sq_ref[...])
    ko_ref[...] = rot(k_ref[...], ck_ref[...], sk_ref[...])


def _rope(qkv):
    cq, sq = _rope_tables(A_Q)
    ck, sk = _rope_tables(A_KV)
    tab = lambda w: pl.BlockSpec((DEC_SEQ, w), lambda b: (0, 0))
    row0 = T_CTX // DEC_SEQ
    return pl.pallas_call(
        _rope_body,
        grid=(DEC_BATCH,),
        in_specs=[pl.BlockSpec((DEC_SEQ, A_Q), lambda b: (row0 + b, 0)),
                  pl.BlockSpec((DEC_SEQ, A_KV), lambda b: (row0 + b, A_Q // A_KV)),
                  tab(A_Q), tab(A_Q), tab(A_KV), tab(A_KV)],
        out_specs=[pl.BlockSpec((DEC_SEQ, A_Q), lambda b: (b, 0)),
                   pl.BlockSpec((DEC_SEQ, A_KV), lambda b: (b, 0))],
        out_shape=[jax.ShapeDtypeStruct((T_LAT, A_Q), F32), jax.ShapeDtypeStruct((T_LAT, A_KV), F32)],
        compiler_params=_cparams("parallel"),
        name="rope",
    )(qkv, qkv, jnp.asarray(cq), jnp.asarray(sq), jnp.asarray(ck), jnp.asarray(sk))


WIN_QB = 256


def _pick_head(x, h, n_heads):
    out = x[:, _head_cols(0)]
    for i in range(1, n_heads):
        out = jnp.where(h == i, x[:, _head_cols(i)], out)
    return out


def _win_attn_body(qraw_ref, qrot_ref, krot_ref, v_ref, kc_ref, vc_ref, sink_ref, o_ref):
    scale = HEAD_DIM ** -0.5
    hk = pl.program_id(1)
    k = _pick_head(krot_ref[...], hk, A_KV_HEADS)
    v = _pick_head(v_ref[...], hk, A_KV_HEADS)
    kc = _pick_head(kc_ref[0], hk, A_KV_HEADS)
    vc = _pick_head(vc_ref[0], hk, A_KV_HEADS)
    head_lane = lax.broadcasted_iota(jnp.int32, (1, A_HEADS), 1)
    sinks = [jnp.sum(jnp.where(head_lane == hk * A_GROUP + g, sink_ref[...], 0.0), axis=-1, keepdims=True)
             for g in range(A_GROUP)]
    sink = jnp.concatenate([jnp.broadcast_to(s, (WIN_QB, 1)) for s in sinks], axis=0)
    for qb in range(DEC_SEQ // WIN_QB):
        q0 = qb * WIN_QB
        rows = slice(q0, q0 + WIN_QB)
        lo = max(0, q0 - WINDOW)
        hi = min(DEC_SEQ, q0 + WIN_QB + WINDOW)
        q_rot = jnp.concatenate([qrot_ref[rows, _head_cols(g)] for g in range(A_GROUP)], axis=0)
        q_raw = jnp.concatenate([qraw_ref[rows, _head_cols(g)] for g in range(A_GROUP)], axis=0)
        s_loc = _dot_nt(q_rot, k[lo:hi]) * scale
        qpos = q0 + (lax.broadcasted_iota(jnp.int32, s_loc.shape, 0) & (WIN_QB - 1))
        kpos = lo + lax.broadcasted_iota(jnp.int32, s_loc.shape, 1)
        s_loc = jnp.where(jnp.abs(kpos - qpos) <= WINDOW, s_loc, MASK_NEG)
        s_ctx = _dot_nt(q_raw, kc) * scale
        m = jnp.maximum(jnp.maximum(jnp.max(s_loc, axis=-1, keepdims=True),
                                    jnp.max(s_ctx, axis=-1, keepdims=True)), sink)
        p_loc = jnp.exp(s_loc - m)
        p_ctx = jnp.exp(s_ctx - m)
        den = (jnp.sum(p_loc, axis=-1, keepdims=True) + jnp.sum(p_ctx, axis=-1, keepdims=True)
               + jnp.exp(sink - m))
        o = (_dot(p_ctx, vc) + _dot(p_loc, v[lo:hi])) / den
        for g in range(A_GROUP):
            o_ref[rows, _head_cols(g)] = o[g * WIN_QB:(g + 1) * WIN_QB]


def _win_attn(qkv, q_rot, k_rot, kc, vc, sink):
    row0 = T_CTX // DEC_SEQ
    gw = A_GROUP * HEAD_DIM
    return pl.pallas_call(
        _win_attn_body,
        grid=(DEC_BATCH, A_KV_HEADS),
        in_specs=[pl.BlockSpec((DEC_SEQ, gw), lambda b, h: (row0 + b, h)),
                  pl.BlockSpec((DEC_SEQ, gw), lambda b, h: (b, h)),
                  pl.BlockSpec((DEC_SEQ, A_KV), lambda b, h: (b, 0)),
                  pl.BlockSpec((DEC_SEQ, A_KV), lambda b, h: (row0 + b, (A_Q + A_KV) // A_KV)),
                  pl.BlockSpec((1, PAST_LEN, A_KV), lambda b, h: (b, 0, 0)),
                  pl.BlockSpec((1, PAST_LEN, A_KV), lambda b, h: (b, 0, 0)),
                  pl.BlockSpec((1, A_HEADS), lambda b, h: (0, 0))],
        out_specs=pl.BlockSpec((DEC_SEQ, gw), lambda b, h: (b, h)),
        out_shape=jax.ShapeDtypeStruct((T_LAT, A_Q), F32),
        compiler_params=_cparams("parallel", "parallel"),
        name="win_attn",
    )(qkv, q_rot, k_rot, qkv, kc, vc, sink.reshape(1, A_HEADS))


GRID_ROWS = DEC_SEQ // GRID_W
NA_BAND = min(NA_ROWS, GRID_ROWS)


NA_REL_ROWS = 2 * NA_ROWS - 1
NA_REL_COLS = 2 * NA_COLS - 1
LANES = 128


def _na_rel_rows(rpb):
    pad = jnp.zeros((B_HEADS, NA_REL_ROWS, GRID_W - NA_REL_COLS), F32)
    one = jnp.concatenate([rpb, pad], axis=-1)
    nxt = jnp.concatenate([one[:, 1:], jnp.zeros((B_HEADS, 1, GRID_W), F32)], axis=1)
    both = jnp.concatenate([one, nxt], axis=-1)
    return jnp.concatenate([both, jnp.zeros((B_HEADS, 16 - NA_REL_ROWS, LANES), F32)], axis=1)


NA_HEADS_PER_STEP = LANES // HEAD_DIM


def _na_row_groups():
    groups = []
    for r in range(GRID_ROWS):
        rs = min(max(r - NA_ROWS // 2, 0), GRID_ROWS - NA_BAND)
        if groups and groups[-1][2] == rs:
            groups[-1][1] += 1
        else:
            groups.append([r, 1, rs])
    return groups


def _na_attn_body(q_ref, k_ref, v_ref, kc_ref, vc_ref, rel_ref, o_ref):
    scale = HEAD_DIM ** -0.5
    cq = lax.broadcasted_iota(jnp.int32, (GRID_W, LANES), 0)
    kcol = lax.broadcasted_iota(jnp.int32, (GRID_W, LANES), 1) & (GRID_W - 1)
    cs = jnp.clip(cq - NA_COLS // 2, 0, GRID_W - NA_COLS)
    col_ok = (kcol >= cs) & (kcol < cs + NA_COLS)
    for hh in range(NA_HEADS_PER_STEP):
        cols = _head_cols(hh)
        kc = kc_ref[0, :, cols]
        vc = vc_ref[0, :, cols]
        tiles = {}

        def pair_tile(a):
            if a not in tiles:
                x = jnp.broadcast_to(rel_ref[hh, a:a + 1, :], (GRID_W, LANES))
                t = pltpu.roll(x, LANES - (NA_COLS - 1), axis=1, stride=1, stride_axis=0)
                tiles[a] = jnp.where(col_ok, t, MASK_NEG)
            return tiles[a]

        for r0, n_r, rs in _na_row_groups():
            bias = jnp.concatenate(
                [jnp.concatenate([pair_tile(rs - r + NA_ROWS - 1 + 2 * i) for i in range(NA_BAND // 2)], axis=1)
                 for r in range(r0, r0 + n_r)], axis=0)
            rows = slice(r0 * GRID_W, (r0 + n_r) * GRID_W)
            band = slice(rs * GRID_W, (rs + NA_BAND) * GRID_W)
            q = q_ref[rows, cols]
            s_loc = _dot_nt(q, k_ref[band, cols]) * scale + bias
            s_ctx = _dot_nt(q, kc) * scale
            m = jnp.maximum(jnp.max(s_loc, axis=-1, keepdims=True), jnp.max(s_ctx, axis=-1, keepdims=True))
            p_loc = jnp.exp(s_loc - m)
            p_ctx = jnp.exp(s_ctx - m)
            den = jnp.sum(p_loc, axis=-1, keepdims=True) + jnp.sum(p_ctx, axis=-1, keepdims=True)
            o_ref[rows, cols] = (_dot(p_ctx, vc) + _dot(p_loc, v_ref[band, cols])) / den


def _na_attn(qkv, kc, vc, rel):
    row0 = T_CTX // DEC_SEQ
    col0 = (A_Q + 2 * A_KV) // LANES
    n_blk = B_W // LANES
    col = lambda j: pl.BlockSpec((DEC_SEQ, LANES), lambda b, p: (row0 + b, col0 + j * n_blk + p))
    cache = pl.BlockSpec((1, PAST_LEN, LANES), lambda b, p: (b, 0, p))
    return pl.pallas_call(
        _na_attn_body,
        grid=(DEC_BATCH, n_blk),
        in_specs=[col(0), col(1), col(2), cache, cache,
                  pl.BlockSpec((NA_HEADS_PER_STEP, 16, LANES), lambda b, p: (p, 0, 0))],
        out_specs=pl.BlockSpec((DEC_SEQ, LANES), lambda b, p: (b, p)),
        out_shape=jax.ShapeDtypeStruct((T_LAT, B_W), F32),
        compiler_params=_cparams("parallel", "parallel"),
        name="na_attn",
    )(qkv, qkv, qkv, kc, vc, rel)


@functools.lru_cache(maxsize=None)
def _dft_mats(L):
    n = 2 * L
    fc = min(L, DFT_CHUNK)
    f = np.arange(L)[:, None]
    t = np.arange(L)[None, :]
    ang = 2.0 * np.pi * ((f * t) % n) / n
    m1 = np.cos(ang)
    m2 = np.sin(ang)
    m2[0, :] = np.where(np.arange(L) % 2 == 0, 1.0, -1.0)
    wgt = np.full((L, 1), 2.0)
    wgt[0, 0] = 1.0
    nch = L // fc
    fwd = np.concatenate([m1.reshape(nch, fc, L), m2.reshape(nch, fc, L)], axis=1)
    inv = np.concatenate([(m1 * wgt / n).reshape(nch, fc, L), (m2 * wgt / n).reshape(nch, fc, L)], axis=1)
    inv = np.transpose(inv, (0, 2, 1))
    return fwd.astype(np.float32), inv.astype(np.float32)


@functools.lru_cache(maxsize=None)
def _filter_consts(L):
    t = np.linspace(0.0, 1.0, L)[:, None]
    bands = (C_EMB - 1) // 2
    ang = (2.0 * math.pi / L) * np.arange(L)[:, None] * np.linspace(1e-4, bands - 1, bands)[None, :]
    z = np.concatenate([t, np.cos(ang), -np.sin(ang)], axis=-1)
    zpad = np.zeros((L, 128))
    zpad[:, :C_EMB] = z
    deltas = np.abs(np.linspace(HYENA_MIN_DECAY, HYENA_MAX_DECAY, C_DIM))
    window = np.exp(-t * deltas[None, :])
    return zpad.astype(np.float32), window.astype(np.float32)


def _filter_body(z_ref, w1_ref, b1_ref, w2_ref, b2_ref, w3_ref, b3_ref, fr_ref, w4_ref, win_ref, fm_ref,
                 hr_ref, g_ref, hq_ref, hs_scr, hd_scr):
    c = pl.program_id(0)
    fc = hr_ref.shape[0]

    @pl.when(c == 0)
    def _():
        fr = fr_ref[...]
        hh = jnp.sin(fr * (_dot_hi(z_ref[...], w1_ref[...]) + b1_ref[...]))
        hh = jnp.sin(fr * (_dot_hi(hh, w2_ref[...]) + b2_ref[...]))
        hh = jnp.sin(fr * (_dot_hi(hh, w3_ref[...]) + b3_ref[...]))
        hh = _dot_hi(hh, w4_ref[...])
        hf = hh[:, :C_DIM] * win_ref[...]
        hb = hh[:, C_DIM:] * win_ref[...]
        hs_scr[...] = hf + hb
        hd_scr[...] = hf - hb

    fm = fm_ref[0]
    hr = _dot_hi(fm[:fc], hs_scr[...])
    first = (lax.broadcasted_iota(jnp.int32, (fc, C_DIM), 0) == 0) & (c == 0)
    hr_ref[...] = hr
    g_ref[...] = jnp.where(first, 0.0, _dot_hi(fm[fc:], hd_scr[...]))
    hs = hs_scr[...]
    sign = jnp.where((lax.broadcasted_iota(jnp.int32, hs.shape, 0) & 1) == 0, 1.0, -1.0)
    hq_ref[...] = jnp.where(first, jnp.sum(hs * sign, axis=0, keepdims=True), hr)


def _hyena_filter(L, filt):
    w1, b1, w2, b2, w3, b3, freq, w4 = filt
    zpad, window = _filter_consts(L)
    fwd, _ = _dft_mats(L)
    nch, fc2, _ = fwd.shape
    fc = fc2 // 2
    w1p = jnp.pad(w1, ((0, 128 - C_EMB), (0, 0)))
    full = lambda shape: pl.BlockSpec(shape, lambda c: tuple(0 for _ in shape))
    out_spec = pl.BlockSpec((fc, C_DIM), lambda c: (c, 0))
    out_sd = jax.ShapeDtypeStruct((L, C_DIM), F32)
    return pl.pallas_call(
        _filter_body,
        grid=(nch,),
        in_specs=[full((L, 128)), full((128, C_FFN)), full((1, C_FFN)), full((C_FFN, C_FFN)), full((1, C_FFN)),
                  full((C_FFN, C_FFN)), full((1, C_FFN)), full((1, C_FFN)), full((C_FFN, 2 * C_DIM)),
                  full((L, C_DIM)), pl.BlockSpec((1, fc2, L), lambda c: (c, 0, 0))],
        out_specs=[out_spec, out_spec, out_spec],
        out_shape=[out_sd, out_sd, out_sd],
        scratch_shapes=[pltpu.VMEM((L, C_DIM), F32), pltpu.VMEM((L, C_DIM), F32)],
        compiler_params=_cparams("arbitrary"),
        name="hyena_filter",
    )(jnp.asarray(zpad), w1p, b1.reshape(1, C_FFN), w2, b2.reshape(1, C_FFN), w3, b3.reshape(1, C_FFN),
      freq.reshape(1, C_FFN), w4, jnp.asarray(window), jnp.asarray(fwd))


def _hyena_body(u_ref, cw_ref, cb_ref, d_ref, fm_ref, fi_ref, hr_ref, g_ref, hq_ref, y_ref,
                x0_scr, z_scr, acc_scr):
    c = pl.program_id(1)
    L = y_ref.shape[0]
    fc = hr_ref.shape[0]

    @pl.when(c == 0)
    def _():
        row = lax.broadcasted_iota(jnp.int32, (L, C_DIM), 0)

        def short_conv(sec):
            cols = slice(sec * C_DIM, (sec + 1) * C_DIM)
            u = u_ref[:, cols]
            prev = jnp.where(row == 0, 0.0, pltpu.roll(u, 1, axis=0))
            nxt = jnp.where(row == L - 1, 0.0, pltpu.roll(u, L - 1, axis=0))
            return (prev * cw_ref[0:1, cols] + u * cw_ref[1:2, cols] + nxt * cw_ref[2:3, cols]
                    + cb_ref[:, cols])

        x0_scr[...] = short_conv(0)
        z_scr[...] = short_conv(1) * short_conv(2)
        acc_scr[...] = jnp.zeros((L, C_DIM), F32)

    ab = _dot_hi(fm_ref[0], z_scr[...])
    a, b = ab[:fc], ab[fc:]
    hr, g, hq = hr_ref[...], g_ref[...], hq_ref[...]
    pq = jnp.concatenate([a * hr - b * g, a * g + b * hq], axis=0)
    acc_scr[...] += _dot_hi(fi_ref[0], pq)

    @pl.when(c == pl.num_programs(1) - 1)
    def _():
        y_ref[...] = x0_scr[...] * (acc_scr[...] + z_scr[...] * d_ref[...])


def _hyena(u, row_blk0, n_seq, L, conv_w, conv_b, d_skip, spec):
    hr, g, hq = spec
    fwd, inv = _dft_mats(L)
    nch, fc2, _ = fwd.shape
    fc = fc2 // 2
    u_w = 3 * C_DIM
    return pl.pallas_call(
        _hyena_body,
        grid=(n_seq, nch),
        in_specs=[pl.BlockSpec((L, u_w), lambda b, c: (row_blk0 + b, 0)),
                  pl.BlockSpec((3, u_w), lambda b, c: (0, 0)),
                  pl.BlockSpec((1, u_w), lambda b, c: (0, 0)),
                  pl.BlockSpec((1, C_DIM), lambda b, c: (0, 0)),
                  pl.BlockSpec((1, fc2, L), lambda b, c: (c, 0, 0)),
                  pl.BlockSpec((1, L, fc2), lambda b, c: (c, 0, 0)),
                  pl.BlockSpec((fc, C_DIM), lambda b, c: (c, 0)),
                  pl.BlockSpec((fc, C_DIM), lambda b, c: (c, 0)),
                  pl.BlockSpec((fc, C_DIM), lambda b, c: (c, 0))],
        out_specs=pl.BlockSpec((L, C_DIM), lambda b, c: (b, 0)),
        out_shape=jax.ShapeDtypeStruct((n_seq * L, C_DIM), F32),
        scratch_shapes=[pltpu.VMEM((L, C_DIM), F32)] * 3,
        compiler_params=_cparams("parallel", "arbitrary"),
        name="hyena",
    )(u, conv_w, conv_b.reshape(1, u_w), d_skip.reshape(1, C_DIM), jnp.asarray(fwd), jnp.asarray(inv), hr, g, hq)


def _hgrn_body(q_ref, ff_ref, fb_ref, i_ref, g_ref, lbf_ref, lbb_ref, nd_ref, s0f_ref, s0b_ref,
               o_ref, sf_ref, sb_ref, o_scr, *, layer):
    L = o_ref.shape[0]
    C = GLA_CHUNK
    nc = L // C
    mid = C // 2
    q = _silu(q_ref[...])
    v = i_ref[...]

    def lower_bound(ref):
        gm = ref[...]
        e = jnp.exp(gm - jnp.max(gm, axis=0, keepdims=True))
        p = e / jnp.sum(e, axis=0, keepdims=True)
        return jnp.sum(p[0:layer + 1], axis=0, keepdims=True) - p[0:1]

    def gates(fx, lb):
        f = lb + (1.0 - lb) * jax.nn.sigmoid(fx)
        return 1.0 - f, jnp.log(f)

    kf, lgf = gates(ff_ref[...], lower_bound(lbf_ref))
    kb, lgb = gates(fb_ref[...], lower_bound(lbb_ref))
    ti = lax.broadcasted_iota(jnp.int32, (C, C), 0)
    si = lax.broadcasted_iota(jnp.int32, (C, C), 1)
    causal = si <= ti
    anti = si >= ti
    tril = causal.astype(F32)
    triu = anti.astype(F32)

    st = jnp.transpose(s0f_ref[0, 0])
    for n in range(nc):
        sl = slice(n * C, (n + 1) * C)
        b = _dot_hi(tril, lgf[sl])
        btot = b[C - 1:C]
        ref = b[mid:mid + 1]
        qc, kc, vc = q[sl], kf[sl], v[sl]
        sc = jnp.where(causal, _dot_nt(qc * jnp.exp(b - ref), kc * jnp.exp(ref - b)), 0.0)
        o_scr[sl, :] = _dot(sc, vc) + _dot_nt(qc * jnp.exp(b), st)
        st = st * jnp.exp(btot) + _dot_tn(vc, kc * jnp.exp(btot - b))
    sf_ref[0, 0] = jnp.transpose(st)

    st = jnp.transpose(s0b_ref[0, 0])
    for n in reversed(range(nc)):
        sl = slice(n * C, (n + 1) * C)
        b = _dot_hi(triu, lgb[sl])
        btot = b[0:1]
        ref = b[mid:mid + 1]
        qc, kc, vc = q[sl], kb[sl], v[sl]
        sc = jnp.where(anti, _dot_nt(qc * jnp.exp(b - ref), kc * jnp.exp(ref - b)), 0.0)
        o_scr[sl, :] += _dot(sc, vc) + _dot_nt(qc * jnp.exp(b), st)
        st = st * jnp.exp(btot) + _dot_tn(vc, kc * jnp.exp(btot - b))
    sb_ref[0, 0] = jnp.transpose(st)

    o_ref[...] = _rms(o_scr[...], nd_ref[...]) * _silu(g_ref[...])


def _hgrn(u, row_blk0, n_seq, L, lb_fwd, lb_bwd, norm_d, s0f, s0b, layer):
    col0 = 3 * C_DIM // D_KDIM
    col = lambda j: pl.BlockSpec((L, D_KDIM), lambda b, h: (row_blk0 + b, col0 + j * D_HEADS + h))
    lbs = pl.BlockSpec((DEPTH, D_KDIM), lambda b, h: (0, h))
    st = pl.BlockSpec((1, 1, D_KDIM, D_VDIM), lambda b, h: (b, h, 0, 0))
    st_sd = jax.ShapeDtypeStruct((n_seq, D_HEADS, D_KDIM, D_VDIM), F32)
    return pl.pallas_call(
        functools.partial(_hgrn_body, layer=layer),
        grid=(n_seq, D_HEADS),
        in_specs=[col(0), col(1), col(2), col(3), col(4), lbs, lbs,
                  pl.BlockSpec((1, D_VDIM), lambda b, h: (0, 0)), st, st],
        out_specs=[pl.BlockSpec((L, D_VDIM), lambda b, h: (b, h)), st, st],
        out_shape=[jax.ShapeDtypeStruct((n_seq * L, D_HEADS * D_VDIM), F32), st_sd, st_sd],
        scratch_shapes=[pltpu.VMEM((L, D_VDIM), F32)],
        compiler_params=_cparams("parallel", "parallel"),
        name="hgrn",
    )(u, u, u, u, u, lb_fwd, lb_bwd, norm_d.reshape(1, D_VDIM), s0f, s0b)


def _pack_bf16_pairs(h):
    n = h.shape[1] // 2
    hi = lax.bitcast_convert_type(h[:, :n].astype(BF16).astype(F32), jnp.int32)
    lo = lax.bitcast_convert_type(h[:, n:].astype(BF16).astype(F32), jnp.int32)
    return hi | lax.shift_right_logical(lo, 16)


def _unpack_bf16_pairs(p):
    hi = lax.bitcast_convert_type(p & jnp.int32(-65536), F32).astype(BF16)
    lo = lax.bitcast_convert_type(lax.shift_left(p, 16), F32).astype(BF16)
    return hi, lo


def _outproj_body(*refs, n_x):
    a_refs, b_refs, x_refs = refs[0:2], refs[2:4], refs[4:4 + n_x]
    mod_ref, gf_ref, w_ref, wr_ref, rb_ref, x1_ref, h2_ref, chosen_ref, gk_ref, ik_ref = refs[4 + n_x:]
    m = mod_ref[0]
    half = a_refs[0].shape[1]
    out = _dot(_token_tile(a_refs), w_ref[0:half, :]) + _dot(_token_tile(b_refs), w_ref[half:, :])
    x1 = _token_tile(x_refs) + m[:, 2 * D_MODEL:3 * D_MODEL] * out
    x1_ref[...] = x1
    h2 = _rms(x1, gf_ref[...]) * (1.0 + m[:, 4 * D_MODEL:5 * D_MODEL]) + m[:, 3 * D_MODEL:4 * D_MODEL]
    h2_ref[...] = _pack_bf16_pairs(h2)
    scores = jax.nn.sigmoid(_dot_hi(h2, wr_ref[...]))
    work = scores + rb_ref[...]
    lane = lax.broadcasted_iota(jnp.int32, work.shape, 1).astype(F32)
    slot = lax.broadcasted_iota(jnp.int32, (work.shape[0], LANES), 1)
    chosen = jnp.zeros(work.shape, F32)
    gk = jnp.zeros((work.shape[0], LANES), F32)
    ik = jnp.zeros((work.shape[0], LANES), F32)
    for k in range(TOP_K):
        best = jnp.max(work, axis=-1, keepdims=True)
        first = jnp.min(jnp.where(work == best, lane, float(N_EXPERTS)), axis=-1, keepdims=True)
        hit = lane == first
        chosen = jnp.where(hit, 1.0, chosen)
        gk = jnp.where(slot == k, jnp.sum(jnp.where(hit, scores, 0.0), axis=-1, keepdims=True), gk)
        ik = jnp.where(slot == k, first, ik)
        work = jnp.where(hit, -jnp.inf, work)
    chosen_ref[...] = chosen
    gk_ref[...] = gk / jnp.sum(gk, axis=-1, keepdims=True) * ROUTE_SCALE
    ik_ref[...] = ik


def _outproj(a, b, x, mod_l, gain_ffn, w_out_bf16, w_router, router_bias):
    half = a[0].shape[1]
    a_specs, a_args = _token_specs(a, half)
    b_specs, b_args = _token_specs(b, half)
    x_specs, x_args = _token_specs(x, D_MODEL)
    return pl.pallas_call(
        functools.partial(_outproj_body, n_x=len(x_args)),
        grid=(T_ALL // TM,),
        in_specs=a_specs + b_specs + x_specs + [
                  pl.BlockSpec((1, 1, N_MOD * D_MODEL), lambda i: (_mod_row(i), 0, 0)),
                  pl.BlockSpec((1, D_MODEL), lambda i: (0, 0)),
                  pl.BlockSpec((2 * half, D_MODEL), lambda i: (0, 0)),
                  pl.BlockSpec((D_MODEL, N_EXPERTS), lambda i: (0, 0)),
                  pl.BlockSpec((1, N_EXPERTS), lambda i: (0, 0))],
        out_specs=[pl.BlockSpec((TM, D_MODEL), lambda i: (i, 0)),
                   pl.BlockSpec((TM, D_MODEL // 2), lambda i: (i, 0)),
                   pl.BlockSpec((TM, N_EXPERTS), lambda i: (i, 0)),
                   pl.BlockSpec((TM, LANES), lambda i: (i, 0)),
                   pl.BlockSpec((TM, LANES), lambda i: (i, 0))],
        out_shape=[jax.ShapeDtypeStruct((T_ALL, D_MODEL), F32),
                   jax.ShapeDtypeStruct((T_ALL, D_MODEL // 2), jnp.int32),
                   jax.ShapeDtypeStruct((T_ALL, N_EXPERTS), F32),
                   jax.ShapeDtypeStruct((T_ALL, LANES), F32),
                   jax.ShapeDtypeStruct((T_ALL, LANES), F32)],
        compiler_params=_cparams("parallel"),
        name="outproj_router",
    )(*a_args, *b_args, *x_args, mod_l, gain_ffn.reshape(1, D_MODEL), w_out_bf16, w_router,
      router_bias.reshape(1, N_EXPERTS))


def _route_body(chosen_ref, ik_ref, dest_ref, first_ref, count_ref, pos_scr):
    n_tiles = T_ALL // TM
    r = lax.broadcasted_iota(jnp.int32, (TM, TM), 0)
    c = lax.broadcasted_iota(jnp.int32, (TM, TM), 1)
    before = (c < r).astype(BF16)

    def count_tile(i, carry):
        rows = pl.ds(pl.multiple_of(i * TM, TM), TM)
        m = chosen_ref[rows, :]
        pos_scr[rows, :] = jnp.dot(before, m.astype(BF16), preferred_element_type=F32) + carry
        return carry + jnp.sum(m, axis=0, keepdims=True)

    counts = lax.fori_loop(0, n_tiles, count_tile, jnp.zeros((1, N_EXPERTS), F32))
    padded = jnp.ceil(counts * (1.0 / MOE_BLK)) * MOE_BLK
    ei = lax.broadcasted_iota(jnp.int32, (N_EXPERTS, N_EXPERTS), 0)
    ej = lax.broadcasted_iota(jnp.int32, (N_EXPERTS, N_EXPERTS), 1)
    end = _dot_hi(jnp.broadcast_to(padded, (8, N_EXPERTS)), (ei <= ej).astype(F32))[0:1]
    start = end - padded

    lane = lax.broadcasted_iota(jnp.int32, (TM, N_EXPERTS), 1).astype(F32)
    slot = lax.broadcasted_iota(jnp.int32, (TM, LANES), 1)

    def dest_tile(i, carry):
        rows = pl.ds(pl.multiple_of(i * TM, TM), TM)
        row_of = pos_scr[rows, :] + start
        ik = ik_ref[rows, :]
        acc = jnp.zeros((TM, LANES), F32)
        for k in range(TOP_K):
            pick = jnp.sum(jnp.where(lane == ik[:, k:k + 1], row_of, 0.0), axis=-1, keepdims=True)
            acc = jnp.where(slot == k, pick, acc)
        dest_ref[rows, :] = acc.astype(jnp.int32)
        return carry

    lax.fori_loop(0, n_tiles, dest_tile, 0)
    first_ref[...] = jnp.broadcast_to(start * (1.0 / MOE_BLK), (8, N_EXPERTS)).astype(jnp.int32)
    count_ref[...] = jnp.broadcast_to(padded * (1.0 / MOE_BLK), (8, N_EXPERTS)).astype(jnp.int32)


def _route(chosen, ik):
    full = lambda shape: pl.BlockSpec(shape, lambda i: (0, 0))
    return pl.pallas_call(
        _route_body,
        grid=(1,),
        in_specs=[full((T_ALL, N_EXPERTS)), full((T_ALL, LANES))],
        out_specs=[full((T_ALL, LANES)), full((8, N_EXPERTS)), full((8, N_EXPERTS))],
        out_shape=[jax.ShapeDtypeStruct((T_ALL, LANES), jnp.int32),
                   jax.ShapeDtypeStruct((8, N_EXPERTS), jnp.int32),
                   jax.ShapeDtypeStruct((8, N_EXPERTS), jnp.int32)],
        scratch_shapes=[pltpu.VMEM((T_ALL, N_EXPERTS), F32)],
        compiler_params=_cparams("arbitrary"),
        name="moe_route",
    )(chosen, ik)


def _sc_worker_id():
    return lax.axis_index("s") * SC_CORES + lax.axis_index("c")


def _sc_dispatch(h2p, dest_chunks):
    n_chunks = T_ALL // DISP_CHUNK
    width = h2p.shape[1]
    mesh = plsc.VectorSubcoreMesh(core_axis_name="c", subcore_axis_name="s")

    @functools.partial(
        pl.kernel, mesh=mesh,
        out_type=jax.ShapeDtypeStruct((MOE_ROWS, width), jnp.int32),
        scratch_types=[pltpu.VMEM((TOP_K, DISP_CHUNK), jnp.int32), pltpu.VMEM((DISP_CHUNK, width), jnp.int32)],
    )
    def run(x_hbm, dest_hbm, xs_hbm, idx_v, rows_v):
        wid = _sc_worker_id()
        for rep in range(-(-n_chunks // SC_WORKERS)):
            chunk = wid + rep * SC_WORKERS

            @pl.when(chunk < n_chunks)
            def _():
                pltpu.sync_copy(dest_hbm.at[chunk], idx_v)
                pltpu.sync_copy(x_hbm.at[pl.ds(chunk * DISP_CHUNK, DISP_CHUNK)], rows_v)
                for k in range(TOP_K):
                    pltpu.sync_copy(rows_v, xs_hbm.at[idx_v.at[k]])

    return run(h2p, dest_chunks)


def _sc_collect(y, dest_flat):
    per_worker = T_ALL // SC_WORKERS
    n_chunks = per_worker // COLLECT_CHUNK
    mesh = plsc.VectorSubcoreMesh(core_axis_name="c", subcore_axis_name="s")

    @functools.partial(
        pl.kernel, mesh=mesh,
        out_type=jax.ShapeDtypeStruct((TOP_K * T_ALL, D_MODEL), F32),
        scratch_types=[pltpu.VMEM((COLLECT_CHUNK,), jnp.int32), pltpu.VMEM((COLLECT_CHUNK, D_MODEL), F32),
                       pltpu.SemaphoreType.DMA],
    )
    def run(y_hbm, dest_hbm, yg_hbm, idx_v, rows_v, sem):
        wid = _sc_worker_id()

        @pl.loop(0, TOP_K * n_chunks)
        def _(step):
            k = step // n_chunks
            off = pl.multiple_of(k * T_ALL + wid * per_worker + (step % n_chunks) * COLLECT_CHUNK, COLLECT_CHUNK)
            pltpu.sync_copy(dest_hbm.at[pl.ds(off, COLLECT_CHUNK)], idx_v)
            pltpu.async_copy(y_hbm.at[idx_v], rows_v, sem).wait()
            pltpu.sync_copy(rows_v, yg_hbm.at[pl.ds(off, COLLECT_CHUNK)])

    return run(y, dest_flat)


def _expert_body(first_ref, count_ref, xs_hbm, wg_ref, wu_ref, wd_ref, y_hbm,
                 wg_bf, wu_bf, wd_bf, x_buf, y_buf, in_sem, out_sem):
    e = pl.program_id(0)
    first = first_ref[e]
    count = count_ref[e]
    half = D_MODEL // 2
    wg_bf[...] = wg_ref[0].astype(BF16)
    wu_bf[...] = wu_ref[0].astype(BF16)
    wd_bf[...] = wd_ref[0].astype(BF16)

    def rows_of(b):
        return pl.ds(pl.multiple_of((first + b) * MOE_BLK, MOE_BLK), MOE_BLK)

    def in_copy(b, slot):
        return pltpu.make_async_copy(xs_hbm.at[rows_of(b)], x_buf.at[slot], in_sem.at[slot])

    def out_copy(b, slot):
        return pltpu.make_async_copy(y_buf.at[slot], y_hbm.at[rows_of(b)], out_sem.at[slot])

    @pl.when(count > 0)
    def _():
        in_copy(0, 0).start()

    def block(b, carry):
        slot = b & 1
        in_copy(b, slot).wait()

        @pl.when(b + 1 < count)
        def _():
            in_copy(b + 1, 1 - slot).start()

        @pl.when(b >= 2)
        def _():
            out_copy(b - 2, slot).wait()

        hi, lo = _unpack_bf16_pairs(x_buf[slot])

        def proj(w_bf):
            return (jnp.dot(hi, w_bf[0:half, :], preferred_element_type=F32)
                    + jnp.dot(lo, w_bf[half:, :], preferred_element_type=F32))

        hid = _silu(proj(wg_bf)) * proj(wu_bf)
        y_buf[slot] = jnp.dot(hid.astype(BF16), wd_bf[...], preferred_element_type=F32)
        out_copy(b, slot).start()
        return carry

    lax.fori_loop(0, count, block, 0)

    @pl.when(count >= 2)
    def _():
        out_copy(count - 2, count & 1).wait()

    @pl.when(count >= 1)
    def _():
        out_copy(count - 1, (count - 1) & 1).wait()


def _experts(first_blk, n_blk, xs, w_gate, w_up, w_down):
    w_in = pl.BlockSpec((1, D_MODEL, D_EXPERT), lambda e, first, count: (e, 0, 0))
    grid_spec = pltpu.PrefetchScalarGridSpec(
        num_scalar_prefetch=2,
        grid=(N_EXPERTS,),
        in_specs=[pl.BlockSpec(memory_space=pl.ANY), w_in, w_in,
                  pl.BlockSpec((1, D_EXPERT, D_MODEL), lambda e, first, count: (e, 0, 0))],
        out_specs=pl.BlockSpec(memory_space=pl.ANY),
        scratch_shapes=[pltpu.VMEM((D_MODEL, D_EXPERT), BF16), pltpu.VMEM((D_MODEL, D_EXPERT), BF16),
                        pltpu.VMEM((D_EXPERT, D_MODEL), BF16),
                        pltpu.VMEM((2, MOE_BLK, D_MODEL // 2), jnp.int32), pltpu.VMEM((2, MOE_BLK, D_MODEL), F32),
                        pltpu.SemaphoreType.DMA((2,)), pltpu.SemaphoreType.DMA((2,))],
    )
    return pl.pallas_call(
        _expert_body,
        grid_spec=grid_spec,
        out_shape=jax.ShapeDtypeStruct((MOE_ROWS, D_MODEL), F32),
        compiler_params=_cparams("arbitrary"),
        name="moe_experts",
    )(first_blk, n_blk, xs, w_gate, w_up, w_down)


def _combine_body(x1_ref, h2_ref, yg_ref, gk_ref, mod_ref, sg_ref, su_ref, sd_ref, fn_ref, *o_refs, final):
    hi, lo = _unpack_bf16_pairs(h2_ref[...])
    half = D_MODEL // 2

    def proj(w_ref):
        return _dot(hi, w_ref[0:half, :]) + _dot(lo, w_ref[half:, :])

    acc = _dot(_silu(proj(sg_ref)) * proj(su_ref), sd_ref[...])
    gk = gk_ref[...]
    for k in range(TOP_K):
        acc = acc + gk[:, k:k + 1] * yg_ref[k]
    m = mod_ref[0]
    y = x1_ref[...] + m[:, 5 * D_MODEL:6 * D_MODEL] * acc
    if not final:
        o_refs[0][...] = y
        return
    y = _rms(y, fn_ref[...])
    is_ctx = pl.program_id(0) < N_CTX_TILES

    @pl.when(is_ctx)
    def _():
        o_refs[0][...] = y

    @pl.when(jnp.logical_not(is_ctx))
    def _():
        o_refs[1][...] = y


def _combine(x1, h2p, yg, gk, mod_l, ws_gate, ws_up, ws_down, final_norm, final):
    tok = lambda shape: pl.BlockSpec(shape, lambda i: (i, 0))
    full = lambda shape: pl.BlockSpec(shape, lambda i: (0, 0))
    if final:
        out_specs, _ = _token_specs((None, None), D_MODEL)
        out_shape = [jax.ShapeDtypeStruct((T_CTX, D_MODEL), F32), jax.ShapeDtypeStruct((T_LAT, D_MODEL), F32)]
    else:
        out_specs = tok((TM, D_MODEL))
        out_shape = jax.ShapeDtypeStruct((T_ALL, D_MODEL), F32)
    return pl.pallas_call(
        functools.partial(_combine_body, final=final),
        grid=(T_ALL // TM,),
        in_specs=[tok((TM, D_MODEL)), tok((TM, D_MODEL // 2)),
                  pl.BlockSpec((TOP_K, TM, D_MODEL), lambda i: (0, i, 0)),
                  tok((TM, LANES)),
                  pl.BlockSpec((1, 1, N_MOD * D_MODEL), lambda i: (_mod_row(i), 0, 0)),
                  full((D_MODEL, D_EXPERT)), full((D_MODEL, D_EXPERT)), full((D_EXPERT, D_MODEL)),
                  full((1, D_MODEL))],
        out_specs=out_specs,
        out_shape=out_shape,
        compiler_params=_cparams("arbitrary"),
        name="moe_combine",
    )(x1, h2p, yg, gk, mod_l, ws_gate, ws_up, ws_down, final_norm.reshape(1, D_MODEL))


def _moe(x1, h2p, chosen, gk, ik, mod_l, w_gate, w_up, w_down, ws_gate, ws_up, ws_down, final_norm, final):
    dest, first_blk, n_blk = _route(chosen, ik)
    dest = dest[:, :TOP_K]
    dest_chunks = dest.reshape(T_ALL // DISP_CHUNK, DISP_CHUNK, TOP_K).transpose(0, 2, 1)
    xs = _sc_dispatch(h2p, dest_chunks)
    y = _experts(first_blk[0], n_blk[0], xs, w_gate, w_up, w_down)
    yg = _sc_collect(y, dest.T.reshape(-1)).reshape(TOP_K, T_ALL, D_MODEL)
    return _combine(x1, h2p, yg, gk, mod_l, ws_gate.astype(BF16), ws_up.astype(BF16), ws_down.astype(BF16),
                    final_norm, final)


def kernel(x_prompt, x_sample, cache_a_k, cache_a_v, cache_b_k, cache_b_v, state_d_fwd, state_d_bwd, c, c_ctx, w_ada, b_ada, norm_mix, norm_ffn, w_in_attn, w_out_attn, sink_a, rpb_b, w_in_rec, w_out_rec, conv_w, conv_b, filt_w1, filt_b1, filt_w2, filt_b2, filt_w3, filt_b3, filt_freq, filt_w4, d_skip, lb_fwd, lb_bwd, norm_d, w_router, router_bias, w_gate, w_up, w_down, ws_gate, ws_up, ws_down, final_norm):
    x = (x_prompt.reshape(T_CTX, D_MODEL), x_sample.reshape(T_LAT, D_MODEL))
    cvec = jnp.concatenate([c_ctx[None, :], c], axis=0)
    mod = _ada(cvec, w_ada, b_ada).reshape(DEPTH, CVEC_PAD, 1, N_MOD * D_MODEL)

    new_kv = None
    new_state = None
    for l in range(DEPTH):
        j = l // 2
        final = l == DEPTH - 1
        if l % 2 == 0:
            qkv = _inproj(x, mod[l], norm_mix[l], w_in_attn[j].astype(BF16))
            oa_ctx, ob_ctx, *new_kv = _ctx_attn(qkv, sink_a[j])
            new_kv = tuple(new_kv)
            q_rot, k_rot = _rope(qkv)
            cache = lambda t: t[:, j].reshape(DEC_BATCH, PAST_LEN, -1)
            oa_lat = _win_attn(qkv, q_rot, k_rot, cache(cache_a_k), cache(cache_a_v), sink_a[j])
            ob_lat = _na_attn(qkv, cache(cache_b_k), cache(cache_b_v), _na_rel_rows(rpb_b[j]))
            mix_a = (oa_ctx, oa_lat)
            mix_b = (ob_ctx, ob_lat)
            w_out = w_out_attn[j]
        else:
            u = _inproj(x, mod[l], norm_mix[l], w_in_rec[j].astype(BF16))
            filt = (filt_w1[j], filt_b1[j], filt_w2[j], filt_b2[j], filt_w3[j], filt_b3[j], filt_freq[j],
                    filt_w4[j])
            y_ctx = _hyena(u, 0, BATCH, SEQ, conv_w[j], conv_b[j], d_skip[j], _hyena_filter(SEQ, filt))
            y_lat = _hyena(u, T_CTX // DEC_SEQ, DEC_BATCH, DEC_SEQ, conv_w[j], conv_b[j], d_skip[j],
                           _hyena_filter(DEC_SEQ, filt))
            zeros = jnp.zeros((BATCH, D_HEADS, D_KDIM, D_VDIM), F32)
            o_ctx, s_f, s_b = _hgrn(u, 0, BATCH, SEQ, lb_fwd, lb_bwd, norm_d[j], zeros, zeros, l)
            o_lat, _, _ = _hgrn(u, T_CTX // DEC_SEQ, DEC_BATCH, DEC_SEQ, lb_fwd, lb_bwd, norm_d[j],
                                state_d_fwd[:, j], state_d_bwd[:, j], l)
            new_state = (s_f[:, None], s_b[:, None])
            mix_a = (y_ctx, y_lat)
            mix_b = (o_ctx, o_lat)
            w_out = w_out_rec[j]
        x1, h2p, chosen, gk, ik = _outproj(mix_a, mix_b, x, mod[l], norm_ffn[l], w_out.astype(BF16), w_router[l],
                                           router_bias[l])
        x = _moe(x1, h2p, chosen, gk, ik, mod[l], w_gate[l], w_up[l], w_down[l], ws_gate[l], ws_up[l],
                 ws_down[l], final_norm, final)

    y_prompt = x[0].reshape(BATCH, SEQ, D_MODEL)
    y_sample = x[1].reshape(DEC_BATCH, DEC_SEQ, D_MODEL)
    return (y_prompt, y_sample) + new_kv + new_state
```

```python
import functools
import math

import numpy as np
import jax
import jax.numpy as jnp
from jax import lax
from jax.experimental import pallas as pl
from jax.experimental.pallas import tpu as pltpu
from jax.experimental.pallas import tpu_sc as plsc

F32 = jnp.float32
BF16 = jnp.bfloat16
HI = lax.Precision.HIGHEST

D_MODEL = 1024
BATCH = 16
SEQ = 256
DEPTH = 2
DEC_BATCH = 2
DEC_SEQ = 1024
PAST_LEN = 512
GRID_W = 64
HEAD_DIM = 64
N_MOD = 6
RMS_EPS = 1e-6
A_HEADS = 8
A_KV_HEADS = 2
A_GROUP = A_HEADS // A_KV_HEADS
WINDOW = 128
ROPE_BASE = 10000.0
B_HEADS = 8
NA_ROWS = 8
NA_COLS = 16
C_DIM = 512
C_EMB = 33
C_FFN = 64
HYENA_MIN_DECAY = math.log(1e-2) / 1.5
HYENA_MAX_DECAY = math.log(1e-2) / 0.3
D_KDIM = 128
D_VDIM = 128
D_HEADS = 4
N_EXPERTS = 64
TOP_K = 8
D_EXPERT = 256
ROUTE_SCALE = 2.5
A_Q = A_HEADS * HEAD_DIM
A_KV = A_KV_HEADS * HEAD_DIM
B_W = B_HEADS * HEAD_DIM
ATTN_IN = A_Q + 2 * A_KV + 3 * B_W
REC_IN = 3 * C_DIM + 5 * D_HEADS * D_KDIM

T_CTX = BATCH * SEQ
T_LAT = DEC_BATCH * DEC_SEQ
T_ALL = T_CTX + T_LAT
N_CVEC = 1 + DEC_BATCH
CVEC_PAD = 8
TM = 256
MASK_NEG = -1e30
GLA_CHUNK = 64
DFT_CHUNK = 256
MOE_BLK = 256
MOE_NBLK = -(-(T_ALL * TOP_K + N_EXPERTS * (MOE_BLK - 1)) // MOE_BLK)
MOE_ROWS = MOE_NBLK * MOE_BLK
SC_CORES = 2
SC_SUBCORES = 16
SC_WORKERS = SC_CORES * SC_SUBCORES
DISP_CHUNK = 128
COLLECT_CHUNK = 32
VMEM_LIMIT = 56 * 1024 * 1024


def _cparams(*sem):
    return pltpu.CompilerParams(dimension_semantics=sem, vmem_limit_bytes=VMEM_LIMIT)


def _mod_row(i):
    return jnp.where(i < T_CTX // TM, 0, 1 + (i - T_CTX // TM) // (DEC_SEQ // TM))


def _dot(a, b):
    return jnp.dot(a.astype(BF16), b.astype(BF16), preferred_element_type=F32)


def _dot_nt(a, b):
    return lax.dot_general(a.astype(BF16), b.astype(BF16), (((1,), (1,)), ((), ())),
                           preferred_element_type=F32)


def _dot_tn(a, b):
    return lax.dot_general(a.astype(BF16), b.astype(BF16), (((0,), (0,)), ((), ())),
                           preferred_element_type=F32)


def _dot_hi(a, b):
    return jnp.dot(a, b, precision=HI, preferred_element_type=F32)


def _silu(x):
    return x * jax.nn.sigmoid(x)


def _rms(x, g):
    return x * lax.rsqrt(jnp.mean(x * x, axis=-1, keepdims=True) + RMS_EPS) * g


ADA_TN = 1536
ADA_UNROLL = 4


def _ada_body(cb_ref, w_ref, b_ref, o_ref):
    tn = o_ref.shape[-1]
    n_slab = tn // LANES

    def step(k8, accs):
        r0 = pl.multiple_of(k8 * 8, 8)
        sk = [_silu(cb_ref[j, pl.ds(r0, 8), :]) for j in range(N_CVEC)]
        out = []
        for s in range(n_slab):
            wk = w_ref[0, pl.ds(r0, 8), s * LANES:(s + 1) * LANES]
            out.extend(accs[s * N_CVEC + j] + wk * sk[j] for j in range(N_CVEC))
        return tuple(out)

    accs = lax.fori_loop(0, D_MODEL // 8, step,
                         tuple(jnp.zeros((8, LANES), F32) for _ in range(n_slab * N_CVEC)), unroll=ADA_UNROLL)
    o_ref[0] = jnp.zeros((CVEC_PAD, tn), F32)
    for s in range(n_slab):
        for j in range(N_CVEC):
            o_ref[0, j:j + 1, s * LANES:(s + 1) * LANES] = (
                jnp.sum(accs[s * N_CVEC + j], axis=0, keepdims=True) + b_ref[0, :, s * LANES:(s + 1) * LANES])


def _ada(cvec, w_ada, b_ada):
    n_out = N_MOD * D_MODEL
    c_lanes = jnp.broadcast_to(cvec[:, :, None], (N_CVEC, D_MODEL, LANES))
    return pl.pallas_call(
        _ada_body,
        grid=(DEPTH, n_out // ADA_TN),
        in_specs=[pl.BlockSpec((N_CVEC, D_MODEL, LANES), lambda l, n: (0, 0, 0)),
                  pl.BlockSpec((1, D_MODEL, ADA_TN), lambda l, n: (l, 0, n)),
                  pl.BlockSpec((1, 1, ADA_TN), lambda l, n: (l, 0, n))],
        out_specs=pl.BlockSpec((1, CVEC_PAD, ADA_TN), lambda l, n: (l, 0, n)),
        out_shape=jax.ShapeDtypeStruct((DEPTH, CVEC_PAD, n_out), F32),
        compiler_params=_cparams("parallel", "parallel"),
        name="ada",
    )(c_lanes, w_ada, b_ada.reshape(DEPTH, 1, n_out))


N_CTX_TILES = T_CTX // TM


def _token_specs(x, width):
    if not isinstance(x, tuple):
        return [pl.BlockSpec((TM, width), lambda i: (i, 0))], (x,)
    return ([pl.BlockSpec((TM, width), lambda i: (jnp.minimum(i, N_CTX_TILES - 1), 0)),
             pl.BlockSpec((TM, width), lambda i: (jnp.maximum(i - N_CTX_TILES, 0), 0))], x)


def _token_tile(refs):
    if len(refs) == 1:
        return refs[0][...]
    return jnp.where(pl.program_id(0) < N_CTX_TILES, refs[0][...], refs[1][...])


def _inproj_body(*refs, n_x):
    x_refs, (mod_ref, g_ref, w_ref, o_ref) = refs[:n_x], refs[n_x:]
    m = mod_ref[0]
    h = _rms(_token_tile(x_refs), g_ref[...]) * (1.0 + m[:, D_MODEL:2 * D_MODEL]) + m[:, 0:D_MODEL]
    o_ref[...] = _dot(h, w_ref[...])


def _inproj(x, mod_l, gain, w_bf16):
    n = w_bf16.shape[1]
    x_specs, x_args = _token_specs(x, D_MODEL)
    return pl.pallas_call(
        functools.partial(_inproj_body, n_x=len(x_args)),
        grid=(T_ALL // TM,),
        in_specs=x_specs + [pl.BlockSpec((1, 1, N_MOD * D_MODEL), lambda i: (_mod_row(i), 0, 0)),
                            pl.BlockSpec((1, D_MODEL), lambda i: (0, 0)),
                            pl.BlockSpec((D_MODEL, n), lambda i: (0, 0))],
        out_specs=pl.BlockSpec((TM, n), lambda i: (i, 0)),
        out_shape=jax.ShapeDtypeStruct((T_ALL, n), F32),
        compiler_params=_cparams("parallel"),
        name="inproj",
    )(*x_args, mod_l, gain.reshape(1, D_MODEL), w_bf16)


def _head_cols(h):
    return slice(h * HEAD_DIM, (h + 1) * HEAD_DIM)


def _group_rows(ref, rows, first_col, sink_ref, hk):
    n = rows.stop - rows.start
    q = jnp.concatenate([ref[rows, first_col + g * HEAD_DIM:first_col + (g + 1) * HEAD_DIM]
                         for g in range(A_GROUP)], axis=0)
    sink = jnp.concatenate([jnp.broadcast_to(sink_ref[:, hk * A_GROUP + g:hk * A_GROUP + g + 1], (n, 1))
                            for g in range(A_GROUP)], axis=0)
    return q, sink


def _ctx_attn_body(qkv_ref, sink_ref, oa_ref, ob_ref, ak_ref, av_ref, bk_ref, bv_ref):
    scale = HEAD_DIM ** -0.5
    rows = slice(0, SEQ)

    def attend(q, k, v, sink):
        s = _dot_nt(q, k) * scale
        m = jnp.max(s, axis=-1, keepdims=True)
        if sink is not None:
            m = jnp.maximum(m, sink)
        p = jnp.exp(s - m)
        den = jnp.sum(p, axis=-1, keepdims=True)
        if sink is not None:
            den = den + jnp.exp(sink - m)
        return _dot(p, v) / den

    for hk in range(A_KV_HEADS):
        k = qkv_ref[:, A_Q + hk * HEAD_DIM:A_Q + (hk + 1) * HEAD_DIM]
        v = qkv_ref[:, A_Q + A_KV + hk * HEAD_DIM:A_Q + A_KV + (hk + 1) * HEAD_DIM]
        ak_ref[0, 0, :, hk, :] = k
        av_ref[0, 0, :, hk, :] = v
        q, sink = _group_rows(qkv_ref, rows, hk * A_GROUP * HEAD_DIM, sink_ref, hk)
        o = attend(q, k, v, sink)
        for g in range(A_GROUP):
            oa_ref[:, _head_cols(hk * A_GROUP + g)] = o[g * SEQ:(g + 1) * SEQ]
    base = A_Q + 2 * A_KV
    for h in range(B_HEADS):
        q = qkv_ref[:, base + h * HEAD_DIM:base + (h + 1) * HEAD_DIM]
        k = qkv_ref[:, base + B_W + h * HEAD_DIM:base + B_W + (h + 1) * HEAD_DIM]
        v = qkv_ref[:, base + 2 * B_W + h * HEAD_DIM:base + 2 * B_W + (h + 1) * HEAD_DIM]
        bk_ref[0, 0, :, h, :] = k
        bv_ref[0, 0, :, h, :] = v
        ob_ref[:, _head_cols(h)] = attend(q, k, v, None)


def _ctx_attn(qkv, sink):
    kv_spec = lambda heads: pl.BlockSpec((1, 1, SEQ, heads, HEAD_DIM), lambda b: (b, 0, 0, 0, 0))
    kv_sd = lambda heads: jax.ShapeDtypeStruct((BATCH, 1, SEQ, heads, HEAD_DIM), F32)
    return pl.pallas_call(
        _ctx_attn_body,
        grid=(BATCH,),
        in_specs=[pl.BlockSpec((SEQ, ATTN_IN), lambda b: (b, 0)),
                  pl.BlockSpec((1, A_HEADS), lambda b: (0, 0))],
        out_specs=[pl.BlockSpec((SEQ, A_Q), lambda b: (b, 0)), pl.BlockSpec((SEQ, B_W), lambda b: (b, 0)),
                   kv_spec(A_KV_HEADS), kv_spec(A_KV_HEADS), kv_spec(B_HEADS), kv_spec(B_HEADS)],
        out_shape=[jax.ShapeDtypeStruct((T_CTX, A_Q), F32), jax.ShapeDtypeStruct((T_CTX, B_W), F32),
                   kv_sd(A_KV_HEADS), kv_sd(A_KV_HEADS), kv_sd(B_HEADS), kv_sd(B_HEADS)],
        compiler_params=_cparams("parallel"),
        name="ctx_attn",
    )(qkv, sink.reshape(1, A_HEADS))


@functools.lru_cache(maxsize=None)
def _rope_tables(width):
    half = HEAD_DIM // 2
    t = np.arange(DEC_SEQ)
    inv = ROPE_BASE ** (-np.arange(0, half, 2, dtype=np.float64) / half)
    ang_r = (t // GRID_W)[:, None] * inv[None, :]
    ang_c = (t % GRID_W)[:, None] * inv[None, :]
    cos = np.concatenate([np.cos(ang_r)] * 2 + [np.cos(ang_c)] * 2, axis=-1)
    sin = np.concatenate([-np.sin(ang_r), np.sin(ang_r), -np.sin(ang_c), np.sin(ang_c)], axis=-1)
    reps = width // HEAD_DIM
    return (np.tile(cos, (1, reps)).astype(np.float32), np.tile(sin, (1, reps)).astype(np.float32))


def _rope_body(q_ref, k_ref, cq_ref, sq_ref, ck_ref, sk_ref, qo_ref, ko_ref):
    quarter = HEAD_DIM // 4

    def rot(x, cos, sin):
        w = x.shape[-1]
        lane = lax.broadcasted_iota(jnp.int32, x.shape, 1)
        fwd = pltpu.roll(x, w - quarter, axis=1)
        bwd = pltpu.roll(x, quarter, axis=1)
        partner = jnp.where((lane & (2 * quarter - 1)) < quarter, fwd, bwd)
        return x * cos + partner * sin

    qo_ref[...] = rot(q_ref[...], cq_ref[...], sq_ref[...])
    ko_ref[...] = rot(k_ref[...], ck_ref[...], sk_ref[...])


def _rope(qkv):
    cq, sq = _rope_tables(A_Q)
    ck, sk = _rope_tables(A_KV)
    tab = lambda w: pl.BlockSpec((DEC_SEQ, w), lambda b: (0, 0))
    row0 = T_CTX // DEC_SEQ
    return pl.pallas_call(
        _rope_body,
        grid=(DEC_BATCH,),
        in_specs=[pl.BlockSpec((DEC_SEQ, A_Q), lambda b: (row0 + b, 0)),
                  pl.BlockSpec((DEC_SEQ, A_KV), lambda b: (row0 + b, A_Q // A_KV)),
                  tab(A_Q), tab(A_Q), tab(A_KV), tab(A_KV)],
        out_specs=[pl.BlockSpec((DEC_SEQ, A_Q), lambda b: (b, 0)),
                   pl.BlockSpec((DEC_SEQ, A_KV), lambda b: (b, 0))],
        out_shape=[jax.ShapeDtypeStruct((T_LAT, A_Q), F32), jax.ShapeDtypeStruct((T_LAT, A_KV), F32)],
        compiler_params=_cparams("parallel"),
        name="rope",
    )(qkv, qkv, jnp.asarray(cq), jnp.asarray(sq), jnp.asarray(ck), jnp.asarray(sk))


WIN_QB = 256


def _pick_head(x, h, n_heads):
    out = x[:, _head_cols(0)]
    for i in range(1, n_heads):
        out = jnp.where(h == i, x[:, _head_cols(i)], out)
    return out


def _win_attn_body(qraw_ref, qrot_ref, krot_ref, v_ref, kc_ref, vc_ref, sink_ref, o_ref):
    scale = HEAD_DIM ** -0.5
    hk = pl.program_id(1)
    k = _pick_head(krot_ref[...], hk, A_KV_HEADS)
    v = _pick_head(v_ref[...], hk, A_KV_HEADS)
    kc = _pick_head(kc_ref[0], hk, A_KV_HEADS)
    vc = _pick_head(vc_ref[0], hk, A_KV_HEADS)
    head_lane = lax.broadcasted_iota(jnp.int32, (1, A_HEADS), 1)
    sinks = [jnp.sum(jnp.where(head_lane == hk * A_GROUP + g, sink_ref[...], 0.0), axis=-1, keepdims=True)
             for g in range(A_GROUP)]
    sink = jnp.concatenate([jnp.broadcast_to(s, (WIN_QB, 1)) for s in sinks], axis=0)
    for qb in range(DEC_SEQ // WIN_QB):
        q0 = qb * WIN_QB
        rows = slice(q0, q0 + WIN_QB)
        lo = max(0, q0 - WINDOW)
        hi = min(DEC_SEQ, q0 + WIN_QB + WINDOW)
        q_rot = jnp.concatenate([qrot_ref[rows, _head_cols(g)] for g in range(A_GROUP)], axis=0)
        q_raw = jnp.concatenate([qraw_ref[rows, _head_cols(g)] for g in range(A_GROUP)], axis=0)
        s_loc = _dot_nt(q_rot, k[lo:hi]) * scale
        qpos = q0 + (lax.broadcasted_iota(jnp.int32, s_loc.shape, 0) & (WIN_QB - 1))
        kpos = lo + lax.broadcasted_iota(jnp.int32, s_loc.shape, 1)
        s_loc = jnp.where(jnp.abs(kpos - qpos) <= WINDOW, s_loc, MASK_NEG)
        s_ctx = _dot_nt(q_raw, kc) * scale
        m = jnp.maximum(jnp.maximum(jnp.max(s_loc, axis=-1, keepdims=True),
                                    jnp.max(s_ctx, axis=-1, keepdims=True)), sink)
        p_loc = jnp.exp(s_loc - m)
        p_ctx = jnp.exp(s_ctx - m)
        den = (jnp.sum(p_loc, axis=-1, keepdims=True) + jnp.sum(p_ctx, axis=-1, keepdims=True)
               + jnp.exp(sink - m))
        o = (_dot(p_ctx, vc) + _dot(p_loc, v[lo:hi])) / den
        for g in range(A_GROUP):
            o_ref[rows, _head_cols(g)] = o[g * WIN_QB:(g + 1) * WIN_QB]


def _win_attn(qkv, q_rot, k_rot, kc, vc, sink):
    row0 = T_CTX // DEC_SEQ
    gw = A_GROUP * HEAD_DIM
    return pl.pallas_call(
        _win_attn_body,
        grid=(DEC_BATCH, A_KV_HEADS),
        in_specs=[pl.BlockSpec((DEC_SEQ, gw), lambda b, h: (row0 + b, h)),
                  pl.BlockSpec((DEC_SEQ, gw), lambda b, h: (b, h)),
                  pl.BlockSpec((DEC_SEQ, A_KV), lambda b, h: (b, 0)),
                  pl.BlockSpec((DEC_SEQ, A_KV), lambda b, h: (row0 + b, (A_Q + A_KV) // A_KV)),
                  pl.BlockSpec((1, PAST_LEN, A_KV), lambda b, h: (b, 0, 0)),
                  pl.BlockSpec((1, PAST_LEN, A_KV), lambda b, h: (b, 0, 0)),
                  pl.BlockSpec((1, A_HEADS), lambda b, h: (0, 0))],
        out_specs=pl.BlockSpec((DEC_SEQ, gw), lambda b, h: (b, h)),
        out_shape=jax.ShapeDtypeStruct((T_LAT, A_Q), F32),
        compiler_params=_cparams("parallel", "parallel"),
        name="win_attn",
    )(qkv, q_rot, k_rot, qkv, kc, vc, sink.reshape(1, A_HEADS))


GRID_ROWS = DEC_SEQ // GRID_W
NA_BAND = min(NA_ROWS, GRID_ROWS)


NA_REL_ROWS = 2 * NA_ROWS - 1
NA_REL_COLS = 2 * NA_COLS - 1
LANES = 128


def _na_rel_rows(rpb):
    pad = jnp.zeros((B_HEADS, NA_REL_ROWS, GRID_W - NA_REL_COLS), F32)
    one = jnp.concatenate([rpb, pad], axis=-1)
    nxt = jnp.concatenate([one[:, 1:], jnp.zeros((B_HEADS, 1, GRID_W), F32)], axis=1)
    both = jnp.concatenate([one, nxt], axis=-1)
    return jnp.concatenate([both, jnp.zeros((B_HEADS, 16 - NA_REL_ROWS, LANES), F32)], axis=1)


NA_HEADS_PER_STEP = LANES // HEAD_DIM


def _na_row_groups():
    groups = []
    for r in range(GRID_ROWS):
        rs = min(max(r - NA_ROWS // 2, 0), GRID_ROWS - NA_BAND)
        if groups and groups[-1][2] == rs:
            groups[-1][1] += 1
        else:
            groups.append([r, 1, rs])
    return groups


def _na_attn_body(q_ref, k_ref, v_ref, kc_ref, vc_ref, rel_ref, o_ref):
    scale = HEAD_DIM ** -0.5
    cq = lax.broadcasted_iota(jnp.int32, (GRID_W, LANES), 0)
    kcol = lax.broadcasted_iota(jnp.int32, (GRID_W, LANES), 1) & (GRID_W - 1)
    cs = jnp.clip(cq - NA_COLS // 2, 0, GRID_W - NA_COLS)
    col_ok = (kcol >= cs) & (kcol < cs + NA_COLS)
    for hh in range(NA_HEADS_PER_STEP):
        cols = _head_cols(hh)
        kc = kc_ref[0, :, cols]
        vc = vc_ref[0, :, cols]
        tiles = {}

        def pair_tile(a):
            if a not in tiles:
                x = jnp.broadcast_to(rel_ref[hh, a:a + 1, :], (GRID_W, LANES))
                t = pltpu.roll(x, LANES - (NA_COLS - 1), axis=1, stride=1, stride_axis=0)
                tiles[a] = jnp.where(col_ok, t, MASK_NEG)
            return tiles[a]

        for r0, n_r, rs in _na_row_groups():
            bias = jnp.concatenate(
                [jnp.concatenate([pair_tile(rs - r + NA_ROWS - 1 + 2 * i) for i in range(NA_BAND // 2)], axis=1)
                 for r in range(r0, r0 + n_r)], axis=0)
            rows = slice(r0 * GRID_W, (r0 + n_r) * GRID_W)
            band = slice(rs * GRID_W, (rs + NA_BAND) * GRID_W)
            q = q_ref[rows, cols]
            s_loc = _dot_nt(q, k_ref[band, cols]) * scale + bias
            s_ctx = _dot_nt(q, kc) * scale
            m = jnp.maximum(jnp.max(s_loc, axis=-1, keepdims=True), jnp.max(s_ctx, axis=-1, keepdims=True))
            p_loc = jnp.exp(s_loc - m)
            p_ctx = jnp.exp(s_ctx - m)
            den = jnp.sum(p_loc, axis=-1, keepdims=True) + jnp.sum(p_ctx, axis=-1, keepdims=True)
            o_ref[rows, cols] = (_dot(p_ctx, vc) + _dot(p_loc, v_ref[band, cols])) / den


def _na_attn(qkv, kc, vc, rel):
    row0 = T_CTX // DEC_SEQ
    col0 = (A_Q + 2 * A_KV) // LANES
    n_blk = B_W // LANES
    col = lambda j: pl.BlockSpec((DEC_SEQ, LANES), lambda b, p: (row0 + b, col0 + j * n_blk + p))
    cache = pl.BlockSpec((1, PAST_LEN, LANES), lambda b, p: (b, 0, p))
    return pl.pallas_call(
        _na_attn_body,
        grid=(DEC_BATCH, n_blk),
        in_specs=[col(0), col(1), col(2), cache, cache,
                  pl.BlockSpec((NA_HEADS_PER_STEP, 16, LANES), lambda b, p: (p, 0, 0))],
        out_specs=pl.BlockSpec((DEC_SEQ, LANES), lambda b, p: (b, p)),
        out_shape=jax.ShapeDtypeStruct((T_LAT, B_W), F32),
        compiler_params=_cparams("parallel", "parallel"),
        name="na_attn",
    )(qkv, qkv, qkv, kc, vc, rel)


@functools.lru_cache(maxsize=None)
def _dft_mats(L):
    n = 2 * L
    fc = min(L, DFT_CHUNK)
    f = np.arange(L)[:, None]
    t = np.arange(L)[None, :]
    ang = 2.0 * np.pi * ((f * t) % n) / n
    m1 = np.cos(ang)
    m2 = np.sin(ang)
    m2[0, :] = np.where(np.arange(L) % 2 == 0, 1.0, -1.0)
    wgt = np.full((L, 1), 2.0)
    wgt[0, 0] = 1.0
    nch = L // fc
    fwd = np.concatenate([m1.reshape(nch, fc, L), m2.reshape(nch, fc, L)], axis=1)
    inv = np.concatenate([(m1 * wgt / n).reshape(nch, fc, L), (m2 * wgt / n).reshape(nch, fc, L)], axis=1)
    inv = np.transpose(inv, (0, 2, 1))
    return fwd.astype(np.float32), inv.astype(np.float32)


@functools.lru_cache(maxsize=None)
def _filter_consts(L):
    t = np.linspace(0.0, 1.0, L)[:, None]
    bands = (C_EMB - 1) // 2
    ang = (2.0 * math.pi / L) * np.arange(L)[:, None] * np.linspace(1e-4, bands - 1, bands)[None, :]
    z = np.concatenate([t, np.cos(ang), -np.sin(ang)], axis=-1)
    zpad = np.zeros((L, 128))
    zpad[:, :C_EMB] = z
    deltas = np.abs(np.linspace(HYENA_MIN_DECAY, HYENA_MAX_DECAY, C_DIM))
    window = np.exp(-t * deltas[None, :])
    return zpad.astype(np.float32), window.astype(np.float32)


def _filter_body(z_ref, w1_ref, b1_ref, w2_ref, b2_ref, w3_ref, b3_ref, fr_ref, w4_ref, win_ref, fm_ref,
                 hr_ref, g_ref, hq_ref, hs_scr, hd_scr):
    c = pl.program_id(0)
    fc = hr_ref.shape[0]

    @pl.when(c == 0)
    def _():
        fr = fr_ref[...]
        hh = jnp.sin(fr * (_dot_hi(z_ref[...], w1_ref[...]) + b1_ref[...]))
        hh = jnp.sin(fr * (_dot_hi(hh, w2_ref[...]) + b2_ref[...]))
        hh = jnp.sin(fr * (_dot_hi(hh, w3_ref[...]) + b3_ref[...]))
        hh = _dot_hi(hh, w4_ref[...])
        hf = hh[:, :C_DIM] * win_ref[...]
        hb = hh[:, C_DIM:] * win_ref[...]
        hs_scr[...] = hf + hb
        hd_scr[...] = hf - hb

    fm = fm_ref[0]
    hr = _dot_hi(fm[:fc], hs_scr[...])
    first = (lax.broadcasted_iota(jnp.int32, (fc, C_DIM), 0) == 0) & (c == 0)
    hr_ref[...] = hr
    g_ref[...] = jnp.where(first, 0.0, _dot_hi(fm[fc:], hd_scr[...]))
    hs = hs_scr[...]
    sign = jnp.where((lax.broadcasted_iota(jnp.int32, hs.shape, 0) & 1) == 0, 1.0, -1.0)
    hq_ref[...] = jnp.where(first, jnp.sum(hs * sign, axis=0, keepdims=True), hr)


def _hyena_filter(L, filt):
    w1, b1, w2, b2, w3, b3, freq, w4 = filt
    zpad, window = _filter_consts(L)
    fwd, _ = _dft_mats(L)
    nch, fc2, _ = fwd.shape
    fc = fc2 // 2
    w1p = jnp.pad(w1, ((0, 128 - C_EMB), (0, 0)))
    full = lambda shape: pl.BlockSpec(shape, lambda c: tuple(0 for _ in shape))
    out_spec = pl.BlockSpec((fc, C_DIM), lambda c: (c, 0))
    out_sd = jax.ShapeDtypeStruct((L, C_DIM), F32)
    return pl.pallas_call(
        _filter_body,
        grid=(nch,),
        in_specs=[full((L, 128)), full((128, C_FFN)), full((1, C_FFN)), full((C_FFN, C_FFN)), full((1, C_FFN)),
                  full((C_FFN, C_FFN)), full((1, C_FFN)), full((1, C_FFN)), full((C_FFN, 2 * C_DIM)),
                  full((L, C_DIM)), pl.BlockSpec((1, fc2, L), lambda c: (c, 0, 0))],
        out_specs=[out_spec, out_spec, out_spec],
        out_shape=[out_sd, out_sd, out_sd],
        scratch_shapes=[pltpu.VMEM((L, C_DIM), F32), pltpu.VMEM((L, C_DIM), F32)],
        compiler_params=_cparams("arbitrary"),
        name="hyena_filter",
    )(jnp.asarray(zpad), w1p, b1.reshape(1, C_FFN), w2, b2.reshape(1, C_FFN), w3, b3.reshape(1, C_FFN),
      freq.reshape(1, C_FFN), w4, jnp.asarray(window), jnp.asarray(fwd))


def _hyena_body(u_ref, cw_ref, cb_ref, d_ref, fm_ref, fi_ref, hr_ref, g_ref, hq_ref, y_ref,
                x0_scr, z_scr, acc_scr):
    c = pl.program_id(1)
    L = y_ref.shape[0]
    fc = hr_ref.shape[0]

    @pl.when(c == 0)
    def _():
        row = lax.broadcasted_iota(jnp.int32, (L, C_DIM), 0)

        def short_conv(sec):
            cols = slice(sec * C_DIM, (sec + 1) * C_DIM)
            u = u_ref[:, cols]
            prev = jnp.where(row == 0, 0.0, pltpu.roll(u, 1, axis=0))
            nxt = jnp.where(row == L - 1, 0.0, pltpu.roll(u, L - 1, axis=0))
            return (prev * cw_ref[0:1, cols] + u * cw_ref[1:2, cols] + nxt * cw_ref[2:3, cols]
                    + cb_ref[:, cols])

        x0_scr[...] = short_conv(0)
        z_scr[...] = short_conv(1) * short_conv(2)
        acc_scr[...] = jnp.zeros((L, C_DIM), F32)

    ab = _dot_hi(fm_ref[0], z_scr[...])
    a, b = ab[:fc], ab[fc:]
    hr, g, hq = hr_ref[...], g_ref[...], hq_ref[...]
    pq = jnp.concatenate([a * hr - b * g, a * g + b * hq], axis=0)
    acc_scr[...] += _dot_hi(fi_ref[0], pq)

    @pl.when(c == pl.num_programs(1) - 1)
    def _():
        y_ref[...] = x0_scr[...] * (acc_scr[...] + z_scr[...] * d_ref[...])


def _hyena(u, row_blk0, n_seq, L, conv_w, conv_b, d_skip, spec):
    hr, g, hq = spec
    fwd, inv = _dft_mats(L)
    nch, fc2, _ = fwd.shape
    fc = fc2 // 2
    u_w = 3 * C_DIM
    return pl.pallas_call(
        _hyena_body,
        grid=(n_seq, nch),
        in_specs=[pl.BlockSpec((L, u_w), lambda b, c: (row_blk0 + b, 0)),
                  pl.BlockSpec((3, u_w), lambda b, c: (0, 0)),
                  pl.BlockSpec((1, u_w), lambda b, c: (0, 0)),
                  pl.BlockSpec((1, C_DIM), lambda b, c: (0, 0)),
                  pl.BlockSpec((1, fc2, L), lambda b, c: (c, 0, 0)),
                  pl.BlockSpec((1, L, fc2), lambda b, c: (c, 0, 0)),
                  pl.BlockSpec((fc, C_DIM), lambda b, c: (c, 0)),
                  pl.BlockSpec((fc, C_DIM), lambda b, c: (c, 0)),
                  pl.BlockSpec((fc, C_DIM), lambda b, c: (c, 0))],
        out_specs=pl.BlockSpec((L, C_DIM), lambda b, c: (b, 0)),
        out_shape=jax.ShapeDtypeStruct((n_seq * L, C_DIM), F32),
        scratch_shapes=[pltpu.VMEM((L, C_DIM), F32)] * 3,
        compiler_params=_cparams("parallel", "arbitrary"),
        name="hyena",
    )(u, conv_w, conv_b.reshape(1, u_w), d_skip.reshape(1, C_DIM), jnp.asarray(fwd), jnp.asarray(inv), hr, g, hq)


def _hgrn_body(q_ref, ff_ref, fb_ref, i_ref, g_ref, lbf_ref, lbb_ref, nd_ref, s0f_ref, s0b_ref,
               o_ref, sf_ref, sb_ref, o_scr, *, layer):
    L = o_ref.shape[0]
    C = GLA_CHUNK
    nc = L // C
    mid = C // 2
    q = _silu(q_ref[...])
    v = i_ref[...]

    def lower_bound(ref):
        gm = ref[...]
        e = jnp.exp(gm - jnp.max(gm, axis=0, keepdims=True))
        p = e / jnp.sum(e, axis=0, keepdims=True)
        return jnp.sum(p[0:layer + 1], axis=0, keepdims=True) - p[0:1]

    def gates(fx, lb):
        f = lb + (1.0 - lb) * jax.nn.sigmoid(fx)
        return 1.0 - f, jnp.log(f)

    kf, lgf = gates(ff_ref[...], lower_bound(lbf_ref))
    kb, lgb = gates(fb_ref[...], lower_bound(lbb_ref))
    ti = lax.broadcasted_iota(jnp.int32, (C, C), 0)
    si = lax.broadcasted_iota(jnp.int32, (C, C), 1)
    causal = si <= ti
    anti = si >= ti
    tril = causal.astype(F32)
    triu = anti.astype(F32)

    st = jnp.transpose(s0f_ref[0, 0])
    for n in range(nc):
        sl = slice(n * C, (n + 1) * C)
        b = _dot_hi(tril, lgf[sl])
        btot = b[C - 1:C]
        ref = b[mid:mid + 1]
        qc, kc, vc = q[sl], kf[sl], v[sl]
        sc = jnp.where(causal, _dot_nt(qc * jnp.exp(b - ref), kc * jnp.exp(ref - b)), 0.0)
        o_scr[sl, :] = _dot(sc, vc) + _dot_nt(qc * jnp.exp(b), st)
        st = st * jnp.exp(btot) + _dot_tn(vc, kc * jnp.exp(btot - b))
    sf_ref[0, 0] = jnp.transpose(st)

    st = jnp.transpose(s0b_ref[0, 0])
    for n in reversed(range(nc)):
        sl = slice(n * C, (n + 1) * C)
        b = _dot_hi(triu, lgb[sl])
        btot = b[0:1]
        ref = b[mid:mid + 1]
        qc, kc, vc = q[sl], kb[sl], v[sl]
        sc = jnp.where(anti, _dot_nt(qc * jnp.exp(b - ref), kc * jnp.exp(ref - b)), 0.0)
        o_scr[sl, :] += _dot(sc, vc) + _dot_nt(qc * jnp.exp(b), st)
        st = st * jnp.exp(btot) + _dot_tn(vc, kc * jnp.exp(btot - b))
    sb_ref[0, 0] = jnp.transpose(st)

    o_ref[...] = _rms(o_scr[...], nd_ref[...]) * _silu(g_ref[...])


def _hgrn(u, row_blk0, n_seq, L, lb_fwd, lb_bwd, norm_d, s0f, s0b, layer):
    col0 = 3 * C_DIM // D_KDIM
    col = lambda j: pl.BlockSpec((L, D_KDIM), lambda b, h: (row_blk0 + b, col0 + j * D_HEADS + h))
    lbs = pl.BlockSpec((DEPTH, D_KDIM), lambda b, h: (0, h))
    st = pl.BlockSpec((1, 1, D_KDIM, D_VDIM), lambda b, h: (b, h, 0, 0))
    st_sd = jax.ShapeDtypeStruct((n_seq, D_HEADS, D_KDIM, D_VDIM), F32)
    return pl.pallas_call(
        functools.partial(_hgrn_body, layer=layer),
        grid=(n_seq, D_HEADS),
        in_specs=[col(0), col(1), col(2), col(3), col(4), lbs, lbs,
                  pl.BlockSpec((1, D_VDIM), lambda b, h: (0, 0)), st, st],
        out_specs=[pl.BlockSpec((L, D_VDIM), lambda b, h: (b, h)), st, st],
        out_shape=[jax.ShapeDtypeStruct((n_seq * L, D_HEADS * D_VDIM), F32), st_sd, st_sd],
        scratch_shapes=[pltpu.VMEM((L, D_VDIM), F32)],
        compiler_params=_cparams("parallel", "parallel"),
        name="hgrn",
    )(u, u, u, u, u, lb_fwd, lb_bwd, norm_d.reshape(1, D_VDIM), s0f, s0b)


def _pack_bf16_pairs(h):
    n = h.shape[1] // 2
    hi = lax.bitcast_convert_type(h[:, :n].astype(BF16).astype(F32), jnp.int32)
    lo = lax.bitcast_convert_type(h[:, n:].astype(BF16).astype(F32), jnp.int32)
    return hi | lax.shift_right_logical(lo, 16)


def _unpack_bf16_pairs(p):
    hi = lax.bitcast_convert_type(p & jnp.int32(-65536), F32).astype(BF16)
    lo = lax.bitcast_convert_type(lax.shift_left(p, 16), F32).astype(BF16)
    return hi, lo


def _outproj_body(*refs, n_x):
    a_refs, b_refs, x_refs = refs[0:2], refs[2:4], refs[4:4 + n_x]
    mod_ref, gf_ref, w_ref, wr_ref, rb_ref, x1_ref, h2_ref, chosen_ref, gk_ref, ik_ref = refs[4 + n_x:]
    m = mod_ref[0]
    half = a_refs[0].shape[1]
    out = _dot(_token_tile(a_refs), w_ref[0:half, :]) + _dot(_token_tile(b_refs), w_ref[half:, :])
    x1 = _token_tile(x_refs) + m[:, 2 * D_MODEL:3 * D_MODEL] * out
    x1_ref[...] = x1
    h2 = _rms(x1, gf_ref[...]) * (1.0 + m[:, 4 * D_MODEL:5 * D_MODEL]) + m[:, 3 * D_MODEL:4 * D_MODEL]
    h2_ref[...] = _pack_bf16_pairs(h2)
    scores = jax.nn.sigmoid(_dot_hi(h2, wr_ref[...]))
    work = scores + rb_ref[...]
    lane = lax.broadcasted_iota(jnp.int32, work.shape, 1).astype(F32)
    slot = lax.broadcasted_iota(jnp.int32, (work.shape[0], LANES), 1)
    chosen = jnp.zeros(work.shape, F32)
    gk = jnp.zeros((work.shape[0], LANES), F32)
    ik = jnp.zeros((work.shape[0], LANES), F32)
    for k in range(TOP_K):
        best = jnp.max(work, axis=-1, keepdims=True)
        first = jnp.min(jnp.where(work == best, lane, float(N_EXPERTS)), axis=-1, keepdims=True)
        hit = lane == first
        chosen = jnp.where(hit, 1.0, chosen)
        gk = jnp.where(slot == k, jnp.sum(jnp.where(hit, scores, 0.0), axis=-1, keepdims=True), gk)
        ik = jnp.where(slot == k, first, ik)
        work = jnp.where(hit, -jnp.inf, work)
    chosen_ref[...] = chosen
    gk_ref[...] = gk / jnp.sum(gk, axis=-1, keepdims=True) * ROUTE_SCALE
    ik_ref[...] = ik


def _outproj(a, b, x, mod_l, gain_ffn, w_out_bf16, w_router, router_bias):
    half = a[0].shape[1]
    a_specs, a_args = _token_specs(a, half)
    b_specs, b_args = _token_specs(b, half)
    x_specs, x_args = _token_specs(x, D_MODEL)
    return pl.pallas_call(
        functools.partial(_outproj_body, n_x=len(x_args)),
        grid=(T_ALL // TM,),
        in_specs=a_specs + b_specs + x_specs + [
                  pl.BlockSpec((1, 1, N_MOD * D_MODEL), lambda i: (_mod_row(i), 0, 0)),
                  pl.BlockSpec((1, D_MODEL), lambda i: (0, 0)),
                  pl.BlockSpec((2 * half, D_MODEL), lambda i: (0, 0)),
                  pl.BlockSpec((D_MODEL, N_EXPERTS), lambda i: (0, 0)),
                  pl.BlockSpec((1, N_EXPERTS), lambda i: (0, 0))],
        out_specs=[pl.BlockSpec((TM, D_MODEL), lambda i: (i, 0)),
                   pl.BlockSpec((TM, D_MODEL // 2), lambda i: (i, 0)),
                   pl.BlockSpec((TM, N_EXPERTS), lambda i: (i, 0)),
                   pl.BlockSpec((TM, LANES), lambda i: (i, 0)),
                   pl.BlockSpec((TM, LANES), lambda i: (i, 0))],
        out_shape=[jax.ShapeDtypeStruct((T_ALL, D_MODEL), F32),
                   jax.ShapeDtypeStruct((T_ALL, D_MODEL // 2), jnp.int32),
                   jax.ShapeDtypeStruct((T_ALL, N_EXPERTS), F32),
                   jax.ShapeDtypeStruct((T_ALL, LANES), F32),
                   jax.ShapeDtypeStruct((T_ALL, LANES), F32)],
        compiler_params=_cparams("parallel"),
        name="outproj_router",
    )(*a_args, *b_args, *x_args, mod_l, gain_ffn.reshape(1, D_MODEL), w_out_bf16, w_router,
      router_bias.reshape(1, N_EXPERTS))


def _route_body(chosen_ref, ik_ref, dest_ref, first_ref, count_ref, pos_scr):
    n_tiles = T_ALL // TM
    r = lax.broadcasted_iota(jnp.int32, (TM, TM), 0)
    c = lax.broadcasted_iota(jnp.int32, (TM, TM), 1)
    before = (c < r).astype(BF16)

    def count_tile(i, carry):
        rows = pl.ds(pl.multiple_of(i * TM, TM), TM)
        m = chosen_ref[rows, :]
        pos_scr[rows, :] = jnp.dot(before, m.astype(BF16), preferred_element_type=F32) + carry
        return carry + jnp.sum(m, axis=0, keepdims=True)

    counts = lax.fori_loop(0, n_tiles, count_tile, jnp.zeros((1, N_EXPERTS), F32))
    padded = jnp.ceil(counts * (1.0 / MOE_BLK)) * MOE_BLK
    ei = lax.broadcasted_iota(jnp.int32, (N_EXPERTS, N_EXPERTS), 0)
    ej = lax.broadcasted_iota(jnp.int32, (N_EXPERTS, N_EXPERTS), 1)
    end = _dot_hi(jnp.broadcast_to(padded, (8, N_EXPERTS)), (ei <= ej).astype(F32))[0:1]
    start = end - padded

    lane = lax.broadcasted_iota(jnp.int32, (TM, N_EXPERTS), 1).astype(F32)
    slot = lax.broadcasted_iota(jnp.int32, (TM, LANES), 1)

    def dest_tile(i, carry):
        rows = pl.ds(pl.multiple_of(i * TM, TM), TM)
        row_of = pos_scr[rows, :] + start
        ik = ik_ref[rows, :]
        acc = jnp.zeros((TM, LANES), F32)
        for k in range(TOP_K):
            pick = jnp.sum(jnp.where(lane == ik[:, k:k + 1], row_of, 0.0), axis=-1, keepdims=True)
            acc = jnp.where(slot == k, pick, acc)
        dest_ref[rows, :] = acc.astype(jnp.int32)
        return carry

    lax.fori_loop(0, n_tiles, dest_tile, 0)
    first_ref[...] = jnp.broadcast_to(start * (1.0 / MOE_BLK), (8, N_EXPERTS)).astype(jnp.int32)
    count_ref[...] = jnp.broadcast_to(padded * (1.0 / MOE_BLK), (8, N_EXPERTS)).astype(jnp.int32)


def _route(chosen, ik):
    full = lambda shape: pl.BlockSpec(shape, lambda i: (0, 0))
    return pl.pallas_call(
        _route_body,
        grid=(1,),
        in_specs=[full((T_ALL, N_EXPERTS)), full((T_ALL, LANES))],
        out_specs=[full((T_ALL, LANES)), full((8, N_EXPERTS)), full((8, N_EXPERTS))],
        out_shape=[jax.ShapeDtypeStruct((T_ALL, LANES), jnp.int32),
                   jax.ShapeDtypeStruct((8, N_EXPERTS), jnp.int32),
                   jax.ShapeDtypeStruct((8, N_EXPERTS), jnp.int32)],
        scratch_shapes=[pltpu.VMEM((T_ALL, N_EXPERTS), F32)],
        compiler_params=_cparams("arbitrary"),
        name="moe_route",
    )(chosen, ik)


def _sc_worker_id():
    return lax.axis_index("s") * SC_CORES + lax.axis_index("c")


def _sc_dispatch(h2p, dest_chunks):
    n_chunks = T_ALL // DISP_CHUNK
    width = h2p.shape[1]
    mesh = plsc.VectorSubcoreMesh(core_axis_name="c", subcore_axis_name="s")

    @functools.partial(
        pl.kernel, mesh=mesh,
        out_type=jax.ShapeDtypeStruct((MOE_ROWS, width), jnp.int32),
        scratch_types=[pltpu.VMEM((TOP_K, DISP_CHUNK), jnp.int32), pltpu.VMEM((DISP_CHUNK, width), jnp.int32)],
    )
    def run(x_hbm, dest_hbm, xs_hbm, idx_v, rows_v):
        wid = _sc_worker_id()
        for rep in range(-(-n_chunks // SC_WORKERS)):
            chunk = wid + rep * SC_WORKERS

            @pl.when(chunk < n_chunks)
            def _():
                pltpu.sync_copy(dest_hbm.at[chunk], idx_v)
                pltpu.sync_copy(x_hbm.at[pl.ds(chunk * DISP_CHUNK, DISP_CHUNK)], rows_v)
                for k in range(TOP_K):
                    pltpu.sync_copy(rows_v, xs_hbm.at[idx_v.at[k]])

    return run(h2p, dest_chunks)


def _sc_collect(y, dest_flat):
    per_worker = T_ALL // SC_WORKERS
    n_chunks = per_worker // COLLECT_CHUNK
    n_steps = TOP_K * n_chunks
    mesh = plsc.VectorSubcoreMesh(core_axis_name="c", subcore_axis_name="s")

    @functools.partial(
        pl.kernel, mesh=mesh,
        out_type=jax.ShapeDtypeStruct((TOP_K * T_ALL, D_MODEL), F32),
        scratch_types=[pltpu.VMEM((TOP_K * per_worker,), jnp.int32),
                       pltpu.VMEM((COLLECT_CHUNK, D_MODEL), F32), pltpu.VMEM((COLLECT_CHUNK, D_MODEL), F32),
                       pltpu.SemaphoreType.DMA, pltpu.SemaphoreType.DMA],
    )
    def run(y_hbm, dest_hbm, yg_hbm, idx_v, rows0, rows1, sem0, sem1):
        wid = _sc_worker_id()
        bufs = ((rows0, sem0), (rows1, sem1))
        for k in range(TOP_K):
            pltpu.sync_copy(dest_hbm.at[pl.ds(k * T_ALL + wid * per_worker, per_worker)],
                            idx_v.at[pl.ds(k * per_worker, per_worker)])

        def gather(step, buf):
            rows, sem = buf
            idx = idx_v.at[pl.ds(pl.multiple_of(step * COLLECT_CHUNK, 8), COLLECT_CHUNK)]
            return pltpu.make_async_copy(y_hbm.at[idx], rows, sem)

        def out_rows(step):
            off = (step // n_chunks) * T_ALL + wid * per_worker + (step % n_chunks) * COLLECT_CHUNK
            return yg_hbm.at[pl.ds(pl.multiple_of(off, 8), COLLECT_CHUNK)]

        gather(0, bufs[0]).start()

        @pl.loop(0, n_steps, step=2)
        def _(base):
            for j in range(2):
                step = base + j

                @pl.when(step + 1 < n_steps)
                def _():
                    gather(step + 1, bufs[1 - j]).start()

                gather(step, bufs[j]).wait()
                pltpu.sync_copy(bufs[j][0], out_rows(step))

    return run(y, dest_flat)


def _expert_body(first_ref, count_ref, xs_hbm, wg_ref, wu_ref, wd_ref, y_hbm,
                 wg_bf, wu_bf, wd_bf, x_buf, y_buf, in_sem, out_sem):
    e = pl.program_id(0)
    first = first_ref[e]
    count = count_ref[e]
    n_used = first_ref[N_EXPERTS - 1] + count_ref[N_EXPERTS - 1]
    half = D_MODEL // 2
    wg_bf[...] = wg_ref[0, 0].astype(BF16)
    wu_bf[...] = wu_ref[0, 0].astype(BF16)
    wd_bf[...] = wd_ref[0, 0].astype(BF16)

    def part_rows(g, part, n_parts):
        size = MOE_BLK // n_parts
        return pl.ds(pl.multiple_of(g * MOE_BLK + part * size, size), size), pl.ds(part * size, size)

    def in_copies(g):
        slot = g & 1
        out = []
        for part in range(EXPERT_IN_PARTS):
            src, dst = part_rows(g, part, EXPERT_IN_PARTS)
            out.append(pltpu.make_async_copy(xs_hbm.at[src], x_buf.at[slot, dst], in_sem.at[slot]))
        return out

    def out_copies(g):
        slot = g & 1
        out = []
        for part in range(EXPERT_OUT_PARTS):
            dst, src = part_rows(g, part, EXPERT_OUT_PARTS)
            out.append(pltpu.make_async_copy(y_buf.at[slot, src], y_hbm.at[dst], out_sem.at[slot]))
        return out

    @pl.when((first == 0) & (count > 0))
    def _():
        for cp in in_copies(0):
            cp.start()

    def block(b, carry):
        g = first + b
        slot = g & 1
        for cp in in_copies(g):
            cp.wait()

        @pl.when(g + 1 < n_used)
        def _():
            for cp in in_copies(g + 1):
                cp.start()

        @pl.when(g >= 2)
        def _():
            for cp in out_copies(g - 2):
                cp.wait()

        hi, lo = _unpack_bf16_pairs(x_buf[slot])

        def proj(w_bf):
            return (jnp.dot(hi, w_bf[0:half, :], preferred_element_type=F32)
                    + jnp.dot(lo, w_bf[half:, :], preferred_element_type=F32))

        hid = _silu(proj(wg_bf)) * proj(wu_bf)
        y_buf[slot] = jnp.dot(hid.astype(BF16), wd_bf[...], preferred_element_type=F32)
        for cp in out_copies(g):
            cp.start()
        return carry

    lax.fori_loop(0, count, block, 0)

    @pl.when(e == N_EXPERTS - 1)
    def _():
        @pl.when(n_used >= 2)
        def _():
            for cp in out_copies(n_used - 2):
                cp.wait()

        @pl.when(n_used >= 1)
        def _():
            for cp in out_copies(n_used - 1):
                cp.wait()


EXPERT_IN_PARTS = 2
EXPERT_OUT_PARTS = 4


def _experts(first_blk, n_blk, xs, layer, w_gate, w_up, w_down):
    w_in = pl.BlockSpec((1, 1, D_MODEL, D_EXPERT), lambda e, first, count: (layer, e, 0, 0))
    grid_spec = pltpu.PrefetchScalarGridSpec(
        num_scalar_prefetch=2,
        grid=(N_EXPERTS,),
        in_specs=[pl.BlockSpec(memory_space=pl.ANY), w_in, w_in,
                  pl.BlockSpec((1, 1, D_EXPERT, D_MODEL), lambda e, first, count: (layer, e, 0, 0))],
        out_specs=pl.BlockSpec(memory_space=pl.ANY),
        scratch_shapes=[pltpu.VMEM((D_MODEL, D_EXPERT), BF16), pltpu.VMEM((D_MODEL, D_EXPERT), BF16),
                        pltpu.VMEM((D_EXPERT, D_MODEL), BF16),
                        pltpu.VMEM((2, MOE_BLK, D_MODEL // 2), jnp.int32), pltpu.VMEM((2, MOE_BLK, D_MODEL), F32),
                        pltpu.SemaphoreType.DMA((2,)), pltpu.SemaphoreType.DMA((2,))],
    )
    return pl.pallas_call(
        _expert_body,
        grid_spec=grid_spec,
        out_shape=jax.ShapeDtypeStruct((MOE_ROWS, D_MODEL), F32),
        compiler_params=_cparams("arbitrary"),
        name="moe_experts",
    )(first_blk, n_blk, xs, w_gate, w_up, w_down)


def _combine_body(x1_ref, h2_ref, yg_ref, gk_ref, mod_ref, sg_ref, su_ref, sd_ref, fn_ref, *o_refs, final):
    hi, lo = _unpack_bf16_pairs(h2_ref[...])
    half = D_MODEL // 2

    def proj(w_ref):
        return _dot(hi, w_ref[0:half, :]) + _dot(lo, w_ref[half:, :])

    acc = _dot(_silu(proj(sg_ref)) * proj(su_ref), sd_ref[...])
    gk = gk_ref[...]
    for k in range(TOP_K):
        acc = acc + gk[:, k:k + 1] * yg_ref[k]
    m = mod_ref[0]
    y = x1_ref[...] + m[:, 5 * D_MODEL:6 * D_MODEL] * acc
    if not final:
        o_refs[0][...] = y
        return
    y = _rms(y, fn_ref[...])
    is_ctx = pl.program_id(0) < N_CTX_TILES

    @pl.when(is_ctx)
    def _():
        o_refs[0][...] = y

    @pl.when(jnp.logical_not(is_ctx))
    def _():
        o_refs[1][...] = y


def _combine(x1, h2p, yg, gk, mod_l, ws_gate, ws_up, ws_down, final_norm, final):
    tok = lambda shape: pl.BlockSpec(shape, lambda i: (i, 0))
    full = lambda shape: pl.BlockSpec(shape, lambda i: (0, 0))
    if final:
        out_specs, _ = _token_specs((None, None), D_MODEL)
        out_shape = [jax.ShapeDtypeStruct((T_CTX, D_MODEL), F32), jax.ShapeDtypeStruct((T_LAT, D_MODEL), F32)]
    else:
        out_specs = tok((TM, D_MODEL))
        out_shape = jax.ShapeDtypeStruct((T_ALL, D_MODEL), F32)
    return pl.pallas_call(
        functools.partial(_combine_body, final=final),
        grid=(T_ALL // TM,),
        in_specs=[tok((TM, D_MODEL)), tok((TM, D_MODEL // 2)),
                  pl.BlockSpec((TOP_K, TM, D_MODEL), lambda i: (0, i, 0)),
                  tok((TM, LANES)),
                  pl.BlockSpec((1, 1, N_MOD * D_MODEL), lambda i: (_mod_row(i), 0, 0)),
                  full((D_MODEL, D_EXPERT)), full((D_MODEL, D_EXPERT)), full((D_EXPERT, D_MODEL)),
                  full((1, D_MODEL))],
        out_specs=out_specs,
        out_shape=out_shape,
        compiler_params=_cparams("arbitrary"),
        name="moe_combine",
    )(x1, h2p, yg, gk, mod_l, ws_gate, ws_up, ws_down, final_norm.reshape(1, D_MODEL))


def _moe(x1, h2p, chosen, gk, ik, mod_l, layer, w_gate, w_up, w_down, ws_gate, ws_up, ws_down, final_norm, final):
    dest, first_blk, n_blk = _route(chosen, ik)
    dest = dest[:, :TOP_K]
    dest_chunks = dest.reshape(T_ALL // DISP_CHUNK, DISP_CHUNK, TOP_K).transpose(0, 2, 1)
    xs = _sc_dispatch(h2p, dest_chunks)
    y = _experts(first_blk[0], n_blk[0], xs, layer, w_gate, w_up, w_down)
    yg = _sc_collect(y, dest.T.reshape(-1)).reshape(TOP_K, T_ALL, D_MODEL)
    return _combine(x1, h2p, yg, gk, mod_l, ws_gate.astype(BF16), ws_up.astype(BF16), ws_down.astype(BF16),
                    final_norm, final)


def kernel(x_prompt, x_sample, cache_a_k, cache_a_v, cache_b_k, cache_b_v, state_d_fwd, state_d_bwd, c, c_ctx, w_ada, b_ada, norm_mix, norm_ffn, w_in_attn, w_out_attn, sink_a, rpb_b, w_in_rec, w_out_rec, conv_w, conv_b, filt_w1, filt_b1, filt_w2, filt_b2, filt_w3, filt_b3, filt_freq, filt_w4, d_skip, lb_fwd, lb_bwd, norm_d, w_router, router_bias, w_gate, w_up, w_down, ws_gate, ws_up, ws_down, final_norm):
    x = (x_prompt.reshape(T_CTX, D_MODEL), x_sample.reshape(T_LAT, D_MODEL))
    cvec = jnp.concatenate([c_ctx[None, :], c], axis=0)
    mod = _ada(cvec, w_ada, b_ada).reshape(DEPTH, CVEC_PAD, 1, N_MOD * D_MODEL)

    new_kv = None
    new_state = None
    for l in range(DEPTH):
        j = l // 2
        final = l == DEPTH - 1
        if l % 2 == 0:
            qkv = _inproj(x, mod[l], norm_mix[l], w_in_attn[j].astype(BF16))
            oa_ctx, ob_ctx, *new_kv = _ctx_attn(qkv, sink_a[j])
            new_kv = tuple(new_kv)
            q_rot, k_rot = _rope(qkv)
            cache = lambda t: t[:, j].reshape(DEC_BATCH, PAST_LEN, -1)
            oa_lat = _win_attn(qkv, q_rot, k_rot, cache(cache_a_k), cache(cache_a_v), sink_a[j])
            ob_lat = _na_attn(qkv, cache(cache_b_k), cache(cache_b_v), _na_rel_rows(rpb_b[j]))
            mix_a = (oa_ctx, oa_lat)
            mix_b = (ob_ctx, ob_lat)
            w_out = w_out_attn[j]
        else:
            u = _inproj(x, mod[l], norm_mix[l], w_in_rec[j].astype(BF16))
            filt = (filt_w1[j], filt_b1[j], filt_w2[j], filt_b2[j], filt_w3[j], filt_b3[j], filt_freq[j],
                    filt_w4[j])
            y_ctx = _hyena(u, 0, BATCH, SEQ, conv_w[j], conv_b[j], d_skip[j], _hyena_filter(SEQ, filt))
            y_lat = _hyena(u, T_CTX // DEC_SEQ, DEC_BATCH, DEC_SEQ, conv_w[j], conv_b[j], d_skip[j],
                           _hyena_filter(DEC_SEQ, filt))
            zeros = jnp.zeros((BATCH, D_HEADS, D_KDIM, D_VDIM), F32)
            o_ctx, s_f, s_b = _hgrn(u, 0, BATCH, SEQ, lb_fwd, lb_bwd, norm_d[j], zeros, zeros, l)
            o_lat, _, _ = _hgrn(u, T_CTX // DEC_SEQ, DEC_BATCH, DEC_SEQ, lb_fwd, lb_bwd, norm_d[j],
                                state_d_fwd[:, j], state_d_bwd[:, j], l)
            new_state = (s_f[:, None], s_b[:, None])
            mix_a = (y_ctx, y_lat)
            mix_b = (o_ctx, o_lat)
            w_out = w_out_rec[j]
        x1, h2p, chosen, gk, ik = _outproj(mix_a, mix_b, x, mod[l], norm_ffn[l], w_out.astype(BF16), w_router[l],
                                           router_bias[l])
        x = _moe(x1, h2p, chosen, gk, ik, mod[l], l, w_gate, w_up, w_down, ws_gate[l], ws_up[l],
                 ws_down[l], final_norm, final)

    y_prompt = x[0].reshape(BATCH, SEQ, D_MODEL)
    y_sample = x[1].reshape(DEC_BATCH, DEC_SEQ, D_MODEL)
    return (y_prompt, y_sample) + new_kv + new_state
```

```python
import functools
import math

import numpy as np
import jax
import jax.numpy as jnp
from jax import lax
from jax.experimental import pallas as pl
from jax.experimental.pallas import tpu as pltpu
from jax.experimental.pallas import tpu_sc as plsc

F32 = jnp.float32
BF16 = jnp.bfloat16
HI = lax.Precision.HIGHEST

D_MODEL = 1024
BATCH = 16
SEQ = 256
DEPTH = 2
DEC_BATCH = 2
DEC_SEQ = 1024
PAST_LEN = 512
GRID_W = 64
HEAD_DIM = 64
N_MOD = 6
RMS_EPS = 1e-6
A_HEADS = 8
A_KV_HEADS = 2
A_GROUP = A_HEADS // A_KV_HEADS
WINDOW = 128
ROPE_BASE = 10000.0
B_HEADS = 8
NA_ROWS = 8
NA_COLS = 16
C_DIM = 512
C_EMB = 33
C_FFN = 64
HYENA_MIN_DECAY = math.log(1e-2) / 1.5
HYENA_MAX_DECAY = math.log(1e-2) / 0.3
D_KDIM = 128
D_VDIM = 128
D_HEADS = 4
N_EXPERTS = 64
TOP_K = 8
D_EXPERT = 256
ROUTE_SCALE = 2.5
A_Q = A_HEADS * HEAD_DIM
A_KV = A_KV_HEADS * HEAD_DIM
B_W = B_HEADS * HEAD_DIM
ATTN_IN = A_Q + 2 * A_KV + 3 * B_W
REC_IN = 3 * C_DIM + 5 * D_HEADS * D_KDIM

T_CTX = BATCH * SEQ
T_LAT = DEC_BATCH * DEC_SEQ
T_ALL = T_CTX + T_LAT
N_CVEC = 1 + DEC_BATCH
CVEC_PAD = 8
TM = 256
MASK_NEG = -1e30
GLA_CHUNK = 64
DFT_CHUNK = 256
MOE_BLK = 256
MOE_NBLK = -(-(T_ALL * TOP_K + N_EXPERTS * (MOE_BLK - 1)) // MOE_BLK)
MOE_ROWS = MOE_NBLK * MOE_BLK
SC_CORES = 2
SC_SUBCORES = 16
SC_WORKERS = SC_CORES * SC_SUBCORES
DISP_CHUNK = 128
COLLECT_CHUNK = 64
VMEM_LIMIT = 56 * 1024 * 1024


def _cparams(*sem):
    return pltpu.CompilerParams(dimension_semantics=sem, vmem_limit_bytes=VMEM_LIMIT)


def _mod_row(i):
    return jnp.where(i < T_CTX // TM, 0, 1 + (i - T_CTX // TM) // (DEC_SEQ // TM))


def _dot(a, b):
    return jnp.dot(a.astype(BF16), b.astype(BF16), preferred_element_type=F32)


def _dot_nt(a, b):
    return lax.dot_general(a.astype(BF16), b.astype(BF16), (((1,), (1,)), ((), ())),
                           preferred_element_type=F32)


def _dot_tn(a, b):
    return lax.dot_general(a.astype(BF16), b.astype(BF16), (((0,), (0,)), ((), ())),
                           preferred_element_type=F32)


def _dot_hi(a, b):
    return jnp.dot(a, b, precision=HI, preferred_element_type=F32)


def _silu(x):
    return x * jax.nn.sigmoid(x)


def _rms(x, g):
    return x * lax.rsqrt(jnp.mean(x * x, axis=-1, keepdims=True) + RMS_EPS) * g


ADA_TN = 1536
ADA_UNROLL = 4


def _ada_body(cb_ref, w_ref, b_ref, o_ref):
    tn = o_ref.shape[-1]
    n_slab = tn // LANES

    def step(k8, accs):
        r0 = pl.multiple_of(k8 * 8, 8)
        sk = [_silu(cb_ref[j, pl.ds(r0, 8), :]) for j in range(N_CVEC)]
        out = []
        for s in range(n_slab):
            wk = w_ref[0, pl.ds(r0, 8), s * LANES:(s + 1) * LANES]
            out.extend(accs[s * N_CVEC + j] + wk * sk[j] for j in range(N_CVEC))
        return tuple(out)

    accs = lax.fori_loop(0, D_MODEL // 8, step,
                         tuple(jnp.zeros((8, LANES), F32) for _ in range(n_slab * N_CVEC)), unroll=ADA_UNROLL)
    o_ref[0] = jnp.zeros((CVEC_PAD, tn), F32)
    for s in range(n_slab):
        for j in range(N_CVEC):
            o_ref[0, j:j + 1, s * LANES:(s + 1) * LANES] = (
                jnp.sum(accs[s * N_CVEC + j], axis=0, keepdims=True) + b_ref[0, :, s * LANES:(s + 1) * LANES])


def _ada(cvec, w_ada, b_ada):
    n_out = N_MOD * D_MODEL
    c_lanes = jnp.broadcast_to(cvec[:, :, None], (N_CVEC, D_MODEL, LANES))
    return pl.pallas_call(
        _ada_body,
        grid=(DEPTH, n_out // ADA_TN),
        in_specs=[pl.BlockSpec((N_CVEC, D_MODEL, LANES), lambda l, n: (0, 0, 0)),
                  pl.BlockSpec((1, D_MODEL, ADA_TN), lambda l, n: (l, 0, n)),
                  pl.BlockSpec((1, 1, ADA_TN), lambda l, n: (l, 0, n))],
        out_specs=pl.BlockSpec((1, CVEC_PAD, ADA_TN), lambda l, n: (l, 0, n)),
        out_shape=jax.ShapeDtypeStruct((DEPTH, CVEC_PAD, n_out), F32),
        compiler_params=_cparams("parallel", "parallel"),
        name="ada",
    )(c_lanes, w_ada, b_ada.reshape(DEPTH, 1, n_out))


N_CTX_TILES = T_CTX // TM


def _token_specs(x, width):
    if not isinstance(x, tuple):
        return [pl.BlockSpec((TM, width), lambda i: (i, 0))], (x,)
    return ([pl.BlockSpec((TM, width), lambda i: (jnp.minimum(i, N_CTX_TILES - 1), 0)),
             pl.BlockSpec((TM, width), lambda i: (jnp.maximum(i - N_CTX_TILES, 0), 0))], x)


def _token_tile(refs):
    if len(refs) == 1:
        return refs[0][...]
    return jnp.where(pl.program_id(0) < N_CTX_TILES, refs[0][...], refs[1][...])


def _inproj_body(*refs, n_x):
    x_refs, (mod_ref, g_ref, w_ref, o_ref) = refs[:n_x], refs[n_x:]
    m = mod_ref[0]
    h = _rms(_token_tile(x_refs), g_ref[...]) * (1.0 + m[:, D_MODEL:2 * D_MODEL]) + m[:, 0:D_MODEL]
    o_ref[...] = _dot(h, w_ref[...])


def _inproj(x, mod_l, gain, w_bf16):
    n = w_bf16.shape[1]
    x_specs, x_args = _token_specs(x, D_MODEL)
    return pl.pallas_call(
        functools.partial(_inproj_body, n_x=len(x_args)),
        grid=(T_ALL // TM,),
        in_specs=x_specs + [pl.BlockSpec((1, 1, N_MOD * D_MODEL), lambda i: (_mod_row(i), 0, 0)),
                            pl.BlockSpec((1, D_MODEL), lambda i: (0, 0)),
                            pl.BlockSpec((D_MODEL, n), lambda i: (0, 0))],
        out_specs=pl.BlockSpec((TM, n), lambda i: (i, 0)),
        out_shape=jax.ShapeDtypeStruct((T_ALL, n), F32),
        compiler_params=_cparams("parallel"),
        name="inproj",
    )(*x_args, mod_l, gain.reshape(1, D_MODEL), w_bf16)


def _head_cols(h):
    return slice(h * HEAD_DIM, (h + 1) * HEAD_DIM)


def _group_rows(ref, rows, first_col, sink_ref, hk):
    n = rows.stop - rows.start
    q = jnp.concatenate([ref[rows, first_col + g * HEAD_DIM:first_col + (g + 1) * HEAD_DIM]
                         for g in range(A_GROUP)], axis=0)
    sink = jnp.concatenate([jnp.broadcast_to(sink_ref[:, hk * A_GROUP + g:hk * A_GROUP + g + 1], (n, 1))
                            for g in range(A_GROUP)], axis=0)
    return q, sink


def _ctx_attn_body(qkv_ref, sink_ref, oa_ref, ob_ref, ak_ref, av_ref, bk_ref, bv_ref):
    scale = HEAD_DIM ** -0.5
    rows = slice(0, SEQ)

    def attend(q, k, v, sink):
        s = _dot_nt(q, k) * scale
        m = jnp.max(s, axis=-1, keepdims=True)
        if sink is not None:
            m = jnp.maximum(m, sink)
        p = jnp.exp(s - m)
        den = jnp.sum(p, axis=-1, keepdims=True)
        if sink is not None:
            den = den + jnp.exp(sink - m)
        return _dot(p, v) / den

    for hk in range(A_KV_HEADS):
        k = qkv_ref[:, A_Q + hk * HEAD_DIM:A_Q + (hk + 1) * HEAD_DIM]
        v = qkv_ref[:, A_Q + A_KV + hk * HEAD_DIM:A_Q + A_KV + (hk + 1) * HEAD_DIM]
        ak_ref[0, 0, :, hk, :] = k
        av_ref[0, 0, :, hk, :] = v
        q, sink = _group_rows(qkv_ref, rows, hk * A_GROUP * HEAD_DIM, sink_ref, hk)
        o = attend(q, k, v, sink)
        for g in range(A_GROUP):
            oa_ref[:, _head_cols(hk * A_GROUP + g)] = o[g * SEQ:(g + 1) * SEQ]
    base = A_Q + 2 * A_KV
    for h in range(B_HEADS):
        q = qkv_ref[:, base + h * HEAD_DIM:base + (h + 1) * HEAD_DIM]
        k = qkv_ref[:, base + B_W + h * HEAD_DIM:base + B_W + (h + 1) * HEAD_DIM]
        v = qkv_ref[:, base + 2 * B_W + h * HEAD_DIM:base + 2 * B_W + (h + 1) * HEAD_DIM]
        bk_ref[0, 0, :, h, :] = k
        bv_ref[0, 0, :, h, :] = v
        ob_ref[:, _head_cols(h)] = attend(q, k, v, None)


def _ctx_attn(qkv, sink):
    kv_spec = lambda heads: pl.BlockSpec((1, 1, SEQ, heads, HEAD_DIM), lambda b: (b, 0, 0, 0, 0))
    kv_sd = lambda heads: jax.ShapeDtypeStruct((BATCH, 1, SEQ, heads, HEAD_DIM), F32)
    return pl.pallas_call(
        _ctx_attn_body,
        grid=(BATCH,),
        in_specs=[pl.BlockSpec((SEQ, ATTN_IN), lambda b: (b, 0)),
                  pl.BlockSpec((1, A_HEADS), lambda b: (0, 0))],
        out_specs=[pl.BlockSpec((SEQ, A_Q), lambda b: (b, 0)), pl.BlockSpec((SEQ, B_W), lambda b: (b, 0)),
                   kv_spec(A_KV_HEADS), kv_spec(A_KV_HEADS), kv_spec(B_HEADS), kv_spec(B_HEADS)],
        out_shape=[jax.ShapeDtypeStruct((T_CTX, A_Q), F32), jax.ShapeDtypeStruct((T_CTX, B_W), F32),
                   kv_sd(A_KV_HEADS), kv_sd(A_KV_HEADS), kv_sd(B_HEADS), kv_sd(B_HEADS)],
        compiler_params=_cparams("parallel"),
        name="ctx_attn",
    )(qkv, sink.reshape(1, A_HEADS))


@functools.lru_cache(maxsize=None)
def _rope_tables(width):
    half = HEAD_DIM // 2
    t = np.arange(DEC_SEQ)
    inv = ROPE_BASE ** (-np.arange(0, half, 2, dtype=np.float64) / half)
    ang_r = (t // GRID_W)[:, None] * inv[None, :]
    ang_c = (t % GRID_W)[:, None] * inv[None, :]
    cos = np.concatenate([np.cos(ang_r)] * 2 + [np.cos(ang_c)] * 2, axis=-1)
    sin = np.concatenate([-np.sin(ang_r), np.sin(ang_r), -np.sin(ang_c), np.sin(ang_c)], axis=-1)
    reps = width // HEAD_DIM
    return (np.tile(cos, (1, reps)).astype(np.float32), np.tile(sin, (1, reps)).astype(np.float32))


def _rope_body(q_ref, k_ref, cq_ref, sq_ref, ck_ref, sk_ref, qo_ref, ko_ref):
    quarter = HEAD_DIM // 4

    def rot(x, cos, sin):
        w = x.shape[-1]
        lane = lax.broadcasted_iota(jnp.int32, x.shape, 1)
        fwd = pltpu.roll(x, w - quarter, axis=1)
        bwd = pltpu.roll(x, quarter, axis=1)
        partner = jnp.where((lane & (2 * quarter - 1)) < quarter, fwd, bwd)
        return x * cos + partner * sin

    qo_ref[...] = rot(q_ref[...], cq_ref[...], sq_ref[...])
    ko_ref[...] = rot(k_ref[...], ck_ref[...], sk_ref[...])


def _rope(qkv):
    cq, sq = _rope_tables(A_Q)
    ck, sk = _rope_tables(A_KV)
    tab = lambda w: pl.BlockSpec((DEC_SEQ, w), lambda b: (0, 0))
    row0 = T_CTX // DEC_SEQ
    return pl.pallas_call(
        _rope_body,
        grid=(DEC_BATCH,),
        in_specs=[pl.BlockSpec((DEC_SEQ, A_Q), lambda b: (row0 + b, 0)),
                  pl.BlockSpec((DEC_SEQ, A_KV), lambda b: (row0 + b, A_Q // A_KV)),
                  tab(A_Q), tab(A_Q), tab(A_KV), tab(A_KV)],
        out_specs=[pl.BlockSpec((DEC_SEQ, A_Q), lambda b: (b, 0)),
                   pl.BlockSpec((DEC_SEQ, A_KV), lambda b: (b, 0))],
        out_shape=[jax.ShapeDtypeStruct((T_LAT, A_Q), F32), jax.ShapeDtypeStruct((T_LAT, A_KV), F32)],
        compiler_params=_cparams("parallel"),
        name="rope",
    )(qkv, qkv, jnp.asarray(cq), jnp.asarray(sq), jnp.asarray(ck), jnp.asarray(sk))


WIN_QB = 256


def _pick_head(x, h, n_heads):
    out = x[:, _head_cols(0)]
    for i in range(1, n_heads):
        out = jnp.where(h == i, x[:, _head_cols(i)], out)
    return out


def _win_attn_body(qraw_ref, qrot_ref, krot_ref, v_ref, kc_ref, vc_ref, sink_ref, o_ref):
    scale = HEAD_DIM ** -0.5
    hk = pl.program_id(1)
    k = _pick_head(krot_ref[...], hk, A_KV_HEADS)
    v = _pick_head(v_ref[...], hk, A_KV_HEADS)
    kc = _pick_head(kc_ref[0], hk, A_KV_HEADS)
    vc = _pick_head(vc_ref[0], hk, A_KV_HEADS)
    head_lane = lax.broadcasted_iota(jnp.int32, (1, A_HEADS), 1)
    sinks = [jnp.sum(jnp.where(head_lane == hk * A_GROUP + g, sink_ref[...], 0.0), axis=-1, keepdims=True)
             for g in range(A_GROUP)]
    sink = jnp.concatenate([jnp.broadcast_to(s, (WIN_QB, 1)) for s in sinks], axis=0)
    for qb in range(DEC_SEQ // WIN_QB):
        q0 = qb * WIN_QB
        rows = slice(q0, q0 + WIN_QB)
        lo = max(0, q0 - WINDOW)
        hi = min(DEC_SEQ, q0 + WIN_QB + WINDOW)
        q_rot = jnp.concatenate([qrot_ref[rows, _head_cols(g)] for g in range(A_GROUP)], axis=0)
        q_raw = jnp.concatenate([qraw_ref[rows, _head_cols(g)] for g in range(A_GROUP)], axis=0)
        s_loc = _dot_nt(q_rot, k[lo:hi]) * scale
        qpos = q0 + (lax.broadcasted_iota(jnp.int32, s_loc.shape, 0) & (WIN_QB - 1))
        kpos = lo + lax.broadcasted_iota(jnp.int32, s_loc.shape, 1)
        s_loc = jnp.where(jnp.abs(kpos - qpos) <= WINDOW, s_loc, MASK_NEG)
        s_ctx = _dot_nt(q_raw, kc) * scale
        m = jnp.maximum(jnp.maximum(jnp.max(s_loc, axis=-1, keepdims=True),
                                    jnp.max(s_ctx, axis=-1, keepdims=True)), sink)
        p_loc = jnp.exp(s_loc - m)
        p_ctx = jnp.exp(s_ctx - m)
        den = (jnp.sum(p_loc, axis=-1, keepdims=True) + jnp.sum(p_ctx, axis=-1, keepdims=True)
               + jnp.exp(sink - m))
        o = (_dot(p_ctx, vc) + _dot(p_loc, v[lo:hi])) / den
        for g in range(A_GROUP):
            o_ref[rows, _head_cols(g)] = o[g * WIN_QB:(g + 1) * WIN_QB]


def _win_attn(qkv, q_rot, k_rot, kc, vc, sink):
    row0 = T_CTX // DEC_SEQ
    gw = A_GROUP * HEAD_DIM
    return pl.pallas_call(
        _win_attn_body,
        grid=(DEC_BATCH, A_KV_HEADS),
        in_specs=[pl.BlockSpec((DEC_SEQ, gw), lambda b, h: (row0 + b, h)),
                  pl.BlockSpec((DEC_SEQ, gw), lambda b, h: (b, h)),
                  pl.BlockSpec((DEC_SEQ, A_KV), lambda b, h: (b, 0)),
                  pl.BlockSpec((DEC_SEQ, A_KV), lambda b, h: (row0 + b, (A_Q + A_KV) // A_KV)),
                  pl.BlockSpec((1, PAST_LEN, A_KV), lambda b, h: (b, 0, 0)),
                  pl.BlockSpec((1, PAST_LEN, A_KV), lambda b, h: (b, 0, 0)),
                  pl.BlockSpec((1, A_HEADS), lambda b, h: (0, 0))],
        out_specs=pl.BlockSpec((DEC_SEQ, gw), lambda b, h: (b, h)),
        out_shape=jax.ShapeDtypeStruct((T_LAT, A_Q), F32),
        compiler_params=_cparams("parallel", "parallel"),
        name="win_attn",
    )(qkv, q_rot, k_rot, qkv, kc, vc, sink.reshape(1, A_HEADS))


GRID_ROWS = DEC_SEQ // GRID_W
NA_BAND = min(NA_ROWS, GRID_ROWS)


NA_REL_ROWS = 2 * NA_ROWS - 1
NA_REL_COLS = 2 * NA_COLS - 1
LANES = 128


def _na_rel_rows(rpb):
    pad = jnp.zeros((B_HEADS, NA_REL_ROWS, GRID_W - NA_REL_COLS), F32)
    one = jnp.concatenate([rpb, pad], axis=-1)
    nxt = jnp.concatenate([one[:, 1:], jnp.zeros((B_HEADS, 1, GRID_W), F32)], axis=1)
    both = jnp.concatenate([one, nxt], axis=-1)
    return jnp.concatenate([both, jnp.zeros((B_HEADS, 16 - NA_REL_ROWS, LANES), F32)], axis=1)


NA_HEADS_PER_STEP = LANES // HEAD_DIM


def _na_row_groups():
    groups = []
    for r in range(GRID_ROWS):
        rs = min(max(r - NA_ROWS // 2, 0), GRID_ROWS - NA_BAND)
        if groups and groups[-1][2] == rs:
            groups[-1][1] += 1
        else:
            groups.append([r, 1, rs])
    return groups


def _na_attn_body(q_ref, k_ref, v_ref, kc_ref, vc_ref, rel_ref, o_ref):
    scale = HEAD_DIM ** -0.5
    cq = lax.broadcasted_iota(jnp.int32, (GRID_W, LANES), 0)
    kcol = lax.broadcasted_iota(jnp.int32, (GRID_W, LANES), 1) & (GRID_W - 1)
    cs = jnp.clip(cq - NA_COLS // 2, 0, GRID_W - NA_COLS)
    col_ok = (kcol >= cs) & (kcol < cs + NA_COLS)
    for hh in range(NA_HEADS_PER_STEP):
        cols = _head_cols(hh)
        kc = kc_ref[0, :, cols]
        vc = vc_ref[0, :, cols]
        tiles = {}

        def pair_tile(a):
            if a not in tiles:
                x = jnp.broadcast_to(rel_ref[hh, a:a + 1, :], (GRID_W, LANES))
                t = pltpu.roll(x, LANES - (NA_COLS - 1), axis=1, stride=1, stride_axis=0)
                tiles[a] = jnp.where(col_ok, t, MASK_NEG)
            return tiles[a]

        for r0, n_r, rs in _na_row_groups():
            bias = jnp.concatenate(
                [jnp.concatenate([pair_tile(rs - r + NA_ROWS - 1 + 2 * i) for i in range(NA_BAND // 2)], axis=1)
                 for r in range(r0, r0 + n_r)], axis=0)
            rows = slice(r0 * GRID_W, (r0 + n_r) * GRID_W)
            band = slice(rs * GRID_W, (rs + NA_BAND) * GRID_W)
            q = q_ref[rows, cols]
            s_loc = _dot_nt(q, k_ref[band, cols]) * scale + bias
            s_ctx = _dot_nt(q, kc) * scale
            m = jnp.maximum(jnp.max(s_loc, axis=-1, keepdims=True), jnp.max(s_ctx, axis=-1, keepdims=True))
            p_loc = jnp.exp(s_loc - m)
            p_ctx = jnp.exp(s_ctx - m)
            den = jnp.sum(p_loc, axis=-1, keepdims=True) + jnp.sum(p_ctx, axis=-1, keepdims=True)
            o_ref[rows, cols] = (_dot(p_ctx, vc) + _dot(p_loc, v_ref[band, cols])) / den


def _na_attn(qkv, kc, vc, rel):
    row0 = T_CTX // DEC_SEQ
    col0 = (A_Q + 2 * A_KV) // LANES
    n_blk = B_W // LANES
    col = lambda j: pl.BlockSpec((DEC_SEQ, LANES), lambda b, p: (row0 + b, col0 + j * n_blk + p))
    cache = pl.BlockSpec((1, PAST_LEN, LANES), lambda b, p: (b, 0, p))
    return pl.pallas_call(
        _na_attn_body,
        grid=(DEC_BATCH, n_blk),
        in_specs=[col(0), col(1), col(2), cache, cache,
                  pl.BlockSpec((NA_HEADS_PER_STEP, 16, LANES), lambda b, p: (p, 0, 0))],
        out_specs=pl.BlockSpec((DEC_SEQ, LANES), lambda b, p: (b, p)),
        out_shape=jax.ShapeDtypeStruct((T_LAT, B_W), F32),
        compiler_params=_cparams("parallel", "parallel"),
        name="na_attn",
    )(qkv, qkv, qkv, kc, vc, rel)


@functools.lru_cache(maxsize=None)
def _dft_mats(L):
    n = 2 * L
    fc = min(L, DFT_CHUNK)
    f = np.arange(L)[:, None]
    t = np.arange(L)[None, :]
    ang = 2.0 * np.pi * ((f * t) % n) / n
    m1 = np.cos(ang)
    m2 = np.sin(ang)
    m2[0, :] = np.where(np.arange(L) % 2 == 0, 1.0, -1.0)
    wgt = np.full((L, 1), 2.0)
    wgt[0, 0] = 1.0
    nch = L // fc
    fwd = np.concatenate([m1.reshape(nch, fc, L), m2.reshape(nch, fc, L)], axis=1)
    inv = np.concatenate([(m1 * wgt / n).reshape(nch, fc, L), (m2 * wgt / n).reshape(nch, fc, L)], axis=1)
    inv = np.transpose(inv, (0, 2, 1))
    return fwd.astype(np.float32), inv.astype(np.float32)


@functools.lru_cache(maxsize=None)
def _filter_consts(L):
    t = np.linspace(0.0, 1.0, L)[:, None]
    bands = (C_EMB - 1) // 2
    ang = (2.0 * math.pi / L) * np.arange(L)[:, None] * np.linspace(1e-4, bands - 1, bands)[None, :]
    z = np.concatenate([t, np.cos(ang), -np.sin(ang)], axis=-1)
    zpad = np.zeros((L, 128))
    zpad[:, :C_EMB] = z
    deltas = np.abs(np.linspace(HYENA_MIN_DECAY, HYENA_MAX_DECAY, C_DIM))
    window = np.exp(-t * deltas[None, :])
    return zpad.astype(np.float32), window.astype(np.float32)


def _filter_body(z_ref, w1_ref, b1_ref, w2_ref, b2_ref, w3_ref, b3_ref, fr_ref, w4_ref, win_ref, fm_ref,
                 hr_ref, g_ref, hq_ref, hs_scr, hd_scr):
    c = pl.program_id(0)
    fc = hr_ref.shape[0]

    @pl.when(c == 0)
    def _():
        fr = fr_ref[...]
        hh = jnp.sin(fr * (_dot_hi(z_ref[...], w1_ref[...]) + b1_ref[...]))
        hh = jnp.sin(fr * (_dot_hi(hh, w2_ref[...]) + b2_ref[...]))
        hh = jnp.sin(fr * (_dot_hi(hh, w3_ref[...]) + b3_ref[...]))
        hh = _dot_hi(hh, w4_ref[...])
        hf = hh[:, :C_DIM] * win_ref[...]
        hb = hh[:, C_DIM:] * win_ref[...]
        hs_scr[...] = hf + hb
        hd_scr[...] = hf - hb

    fm = fm_ref[0]
    hr = _dot_hi(fm[:fc], hs_scr[...])
    first = (lax.broadcasted_iota(jnp.int32, (fc, C_DIM), 0) == 0) & (c == 0)
    hr_ref[...] = hr
    g_ref[...] = jnp.where(first, 0.0, _dot_hi(fm[fc:], hd_scr[...]))
    hs = hs_scr[...]
    sign = jnp.where((lax.broadcasted_iota(jnp.int32, hs.shape, 0) & 1) == 0, 1.0, -1.0)
    hq_ref[...] = jnp.where(first, jnp.sum(hs * sign, axis=0, keepdims=True), hr)


def _hyena_filter(L, filt):
    w1, b1, w2, b2, w3, b3, freq, w4 = filt
    zpad, window = _filter_consts(L)
    fwd, _ = _dft_mats(L)
    nch, fc2, _ = fwd.shape
    fc = fc2 // 2
    w1p = jnp.pad(w1, ((0, 128 - C_EMB), (0, 0)))
    full = lambda shape: pl.BlockSpec(shape, lambda c: tuple(0 for _ in shape))
    out_spec = pl.BlockSpec((fc, C_DIM), lambda c: (c, 0))
    out_sd = jax.ShapeDtypeStruct((L, C_DIM), F32)
    return pl.pallas_call(
        _filter_body,
        grid=(nch,),
        in_specs=[full((L, 128)), full((128, C_FFN)), full((1, C_FFN)), full((C_FFN, C_FFN)), full((1, C_FFN)),
                  full((C_FFN, C_FFN)), full((1, C_FFN)), full((1, C_FFN)), full((C_FFN, 2 * C_DIM)),
                  full((L, C_DIM)), pl.BlockSpec((1, fc2, L), lambda c: (c, 0, 0))],
        out_specs=[out_spec, out_spec, out_spec],
        out_shape=[out_sd, out_sd, out_sd],
        scratch_shapes=[pltpu.VMEM((L, C_DIM), F32), pltpu.VMEM((L, C_DIM), F32)],
        compiler_params=_cparams("arbitrary"),
        name="hyena_filter",
    )(jnp.asarray(zpad), w1p, b1.reshape(1, C_FFN), w2, b2.reshape(1, C_FFN), w3, b3.reshape(1, C_FFN),
      freq.reshape(1, C_FFN), w4, jnp.asarray(window), jnp.asarray(fwd))


def _hyena_body(u_ref, cw_ref, cb_ref, d_ref, fm_ref, fi_ref, hr_ref, g_ref, hq_ref, y_ref,
                x0_scr, z_scr, acc_scr):
    c = pl.program_id(1)
    L = y_ref.shape[0]
    fc = hr_ref.shape[0]

    @pl.when(c == 0)
    def _():
        row = lax.broadcasted_iota(jnp.int32, (L, C_DIM), 0)

        def short_conv(sec):
            cols = slice(sec * C_DIM, (sec + 1) * C_DIM)
            u = u_ref[:, cols]
            prev = jnp.where(row == 0, 0.0, pltpu.roll(u, 1, axis=0))
            nxt = jnp.where(row == L - 1, 0.0, pltpu.roll(u, L - 1, axis=0))
            return (prev * cw_ref[0:1, cols] + u * cw_ref[1:2, cols] + nxt * cw_ref[2:3, cols]
                    + cb_ref[:, cols])

        x0_scr[...] = short_conv(0)
        z_scr[...] = short_conv(1) * short_conv(2)
        acc_scr[...] = jnp.zeros((L, C_DIM), F32)

    ab = _dot_hi(fm_ref[0], z_scr[...])
    a, b = ab[:fc], ab[fc:]
    hr, g, hq = hr_ref[...], g_ref[...], hq_ref[...]
    pq = jnp.concatenate([a * hr - b * g, a * g + b * hq], axis=0)
    acc_scr[...] += _dot_hi(fi_ref[0], pq)

    @pl.when(c == pl.num_programs(1) - 1)
    def _():
        y_ref[...] = x0_scr[...] * (acc_scr[...] + z_scr[...] * d_ref[...])


def _hyena(u, row_blk0, n_seq, L, conv_w, conv_b, d_skip, spec):
    hr, g, hq = spec
    fwd, inv = _dft_mats(L)
    nch, fc2, _ = fwd.shape
    fc = fc2 // 2
    u_w = 3 * C_DIM
    return pl.pallas_call(
        _hyena_body,
        grid=(n_seq, nch),
        in_specs=[pl.BlockSpec((L, u_w), lambda b, c: (row_blk0 + b, 0)),
                  pl.BlockSpec((3, u_w), lambda b, c: (0, 0)),
                  pl.BlockSpec((1, u_w), lambda b, c: (0, 0)),
                  pl.BlockSpec((1, C_DIM), lambda b, c: (0, 0)),
                  pl.BlockSpec((1, fc2, L), lambda b, c: (c, 0, 0)),
                  pl.BlockSpec((1, L, fc2), lambda b, c: (c, 0, 0)),
                  pl.BlockSpec((fc, C_DIM), lambda b, c: (c, 0)),
                  pl.BlockSpec((fc, C_DIM), lambda b, c: (c, 0)),
                  pl.BlockSpec((fc, C_DIM), lambda b, c: (c, 0))],
        out_specs=pl.BlockSpec((L, C_DIM), lambda b, c: (b, 0)),
        out_shape=jax.ShapeDtypeStruct((n_seq * L, C_DIM), F32),
        scratch_shapes=[pltpu.VMEM((L, C_DIM), F32)] * 3,
        compiler_params=_cparams("parallel", "arbitrary"),
        name="hyena",
    )(u, conv_w, conv_b.reshape(1, u_w), d_skip.reshape(1, C_DIM), jnp.asarray(fwd), jnp.asarray(inv), hr, g, hq)


def _hgrn_body(q_ref, ff_ref, fb_ref, i_ref, g_ref, lbf_ref, lbb_ref, nd_ref, s0f_ref, s0b_ref,
               o_ref, sf_ref, sb_ref, o_scr, *, layer):
    L = o_ref.shape[0]
    C = GLA_CHUNK
    nc = L // C
    mid = C // 2
    q = _silu(q_ref[...])
    v = i_ref[...]

    def lower_bound(ref):
        gm = ref[...]
        e = jnp.exp(gm - jnp.max(gm, axis=0, keepdims=True))
        p = e / jnp.sum(e, axis=0, keepdims=True)
        return jnp.sum(p[0:layer + 1], axis=0, keepdims=True) - p[0:1]

    def gates(fx, lb):
        f = lb + (1.0 - lb) * jax.nn.sigmoid(fx)
        return 1.0 - f, jnp.log(f)

    kf, lgf = gates(ff_ref[...], lower_bound(lbf_ref))
    kb, lgb = gates(fb_ref[...], lower_bound(lbb_ref))
    ti = lax.broadcasted_iota(jnp.int32, (C, C), 0)
    si = lax.broadcasted_iota(jnp.int32, (C, C), 1)
    causal = si <= ti
    anti = si >= ti
    tril = causal.astype(F32)
    triu = anti.astype(F32)

    st = jnp.transpose(s0f_ref[0, 0])
    for n in range(nc):
        sl = slice(n * C, (n + 1) * C)
        b = _dot_hi(tril, lgf[sl])
        btot = b[C - 1:C]
        ref = b[mid:mid + 1]
        qc, kc, vc = q[sl], kf[sl], v[sl]
        sc = jnp.where(causal, _dot_nt(qc * jnp.exp(b - ref), kc * jnp.exp(ref - b)), 0.0)
        o_scr[sl, :] = _dot(sc, vc) + _dot_nt(qc * jnp.exp(b), st)
        st = st * jnp.exp(btot) + _dot_tn(vc, kc * jnp.exp(btot - b))
    sf_ref[0, 0] = jnp.transpose(st)

    st = jnp.transpose(s0b_ref[0, 0])
    for n in reversed(range(nc)):
        sl = slice(n * C, (n + 1) * C)
        b = _dot_hi(triu, lgb[sl])
        btot = b[0:1]
        ref = b[mid:mid + 1]
        qc, kc, vc = q[sl], kb[sl], v[sl]
        sc = jnp.where(anti, _dot_nt(qc * jnp.exp(b - ref), kc * jnp.exp(ref - b)), 0.0)
        o_scr[sl, :] += _dot(sc, vc) + _dot_nt(qc * jnp.exp(b), st)
        st = st * jnp.exp(btot) + _dot_tn(vc, kc * jnp.exp(btot - b))
    sb_ref[0, 0] = jnp.transpose(st)

    o_ref[...] = _rms(o_scr[...], nd_ref[...]) * _silu(g_ref[...])


def _hgrn(u, row_blk0, n_seq, L, lb_fwd, lb_bwd, norm_d, s0f, s0b, layer):
    col0 = 3 * C_DIM // D_KDIM
    col = lambda j: pl.BlockSpec((L, D_KDIM), lambda b, h: (row_blk0 + b, col0 + j * D_HEADS + h))
    lbs = pl.BlockSpec((DEPTH, D_KDIM), lambda b, h: (0, h))
    st = pl.BlockSpec((1, 1, D_KDIM, D_VDIM), lambda b, h: (b, h, 0, 0))
    st_sd = jax.ShapeDtypeStruct((n_seq, D_HEADS, D_KDIM, D_VDIM), F32)
    return pl.pallas_call(
        functools.partial(_hgrn_body, layer=layer),
        grid=(n_seq, D_HEADS),
        in_specs=[col(0), col(1), col(2), col(3), col(4), lbs, lbs,
                  pl.BlockSpec((1, D_VDIM), lambda b, h: (0, 0)), st, st],
        out_specs=[pl.BlockSpec((L, D_VDIM), lambda b, h: (b, h)), st, st],
        out_shape=[jax.ShapeDtypeStruct((n_seq * L, D_HEADS * D_VDIM), F32), st_sd, st_sd],
        scratch_shapes=[pltpu.VMEM((L, D_VDIM), F32)],
        compiler_params=_cparams("parallel", "parallel"),
        name="hgrn",
    )(u, u, u, u, u, lb_fwd, lb_bwd, norm_d.reshape(1, D_VDIM), s0f, s0b)


def _pack_bf16_pairs(h):
    n = h.shape[1] // 2
    hi = lax.bitcast_convert_type(h[:, :n].astype(BF16).astype(F32), jnp.int32)
    lo = lax.bitcast_convert_type(h[:, n:].astype(BF16).astype(F32), jnp.int32)
    return hi | lax.shift_right_logical(lo, 16)


def _unpack_bf16_pairs(p):
    hi = lax.bitcast_convert_type(p & jnp.int32(-65536), F32).astype(BF16)
    lo = lax.bitcast_convert_type(lax.shift_left(p, 16), F32).astype(BF16)
    return hi, lo


def _outproj_body(*refs, n_x):
    a_refs, b_refs, x_refs = refs[0:2], refs[2:4], refs[4:4 + n_x]
    mod_ref, gf_ref, w_ref, wr_ref, rb_ref, x1_ref, h2_ref, chosen_ref, gk_ref, ik_ref = refs[4 + n_x:]
    m = mod_ref[0]
    half = a_refs[0].shape[1]
    out = _dot(_token_tile(a_refs), w_ref[0:half, :]) + _dot(_token_tile(b_refs), w_ref[half:, :])
    x1 = _token_tile(x_refs) + m[:, 2 * D_MODEL:3 * D_MODEL] * out
    x1_ref[...] = x1
    h2 = _rms(x1, gf_ref[...]) * (1.0 + m[:, 4 * D_MODEL:5 * D_MODEL]) + m[:, 3 * D_MODEL:4 * D_MODEL]
    h2_ref[...] = _pack_bf16_pairs(h2)
    scores = jax.nn.sigmoid(_dot_hi(h2, wr_ref[...]))
    work = scores + rb_ref[...]
    lane = lax.broadcasted_iota(jnp.int32, work.shape, 1).astype(F32)
    slot = lax.broadcasted_iota(jnp.int32, (work.shape[0], LANES), 1)
    chosen = jnp.zeros(work.shape, F32)
    gk = jnp.zeros((work.shape[0], LANES), F32)
    ik = jnp.zeros((work.shape[0], LANES), F32)
    for k in range(TOP_K):
        best = jnp.max(work, axis=-1, keepdims=True)
        first = jnp.min(jnp.where(work == best, lane, float(N_EXPERTS)), axis=-1, keepdims=True)
        hit = lane == first
        chosen = jnp.where(hit, 1.0, chosen)
        gk = jnp.where(slot == k, jnp.sum(jnp.where(hit, scores, 0.0), axis=-1, keepdims=True), gk)
        ik = jnp.where(slot == k, first, ik)
        work = jnp.where(hit, -jnp.inf, work)
    chosen_ref[...] = chosen
    gk_ref[...] = gk / jnp.sum(gk, axis=-1, keepdims=True) * ROUTE_SCALE
    ik_ref[...] = ik


def _outproj(a, b, x, mod_l, gain_ffn, w_out_bf16, w_router, router_bias):
    half = a[0].shape[1]
    a_specs, a_args = _token_specs(a, half)
    b_specs, b_args = _token_specs(b, half)
    x_specs, x_args = _token_specs(x, D_MODEL)
    return pl.pallas_call(
        functools.partial(_outproj_body, n_x=len(x_args)),
        grid=(T_ALL // TM,),
        in_specs=a_specs + b_specs + x_specs + [
                  pl.BlockSpec((1, 1, N_MOD * D_MODEL), lambda i: (_mod_row(i), 0, 0)),
                  pl.BlockSpec((1, D_MODEL), lambda i: (0, 0)),
                  pl.BlockSpec((2 * half, D_MODEL), lambda i: (0, 0)),
                  pl.BlockSpec((D_MODEL, N_EXPERTS), lambda i: (0, 0)),
                  pl.BlockSpec((1, N_EXPERTS), lambda i: (0, 0))],
        out_specs=[pl.BlockSpec((TM, D_MODEL), lambda i: (i, 0)),
                   pl.BlockSpec((TM, D_MODEL // 2), lambda i: (i, 0)),
                   pl.BlockSpec((TM, N_EXPERTS), lambda i: (i, 0)),
                   pl.BlockSpec((TM, LANES), lambda i: (i, 0)),
                   pl.BlockSpec((TM, LANES), lambda i: (i, 0))],
        out_shape=[jax.ShapeDtypeStruct((T_ALL, D_MODEL), F32),
                   jax.ShapeDtypeStruct((T_ALL, D_MODEL // 2), jnp.int32),
                   jax.ShapeDtypeStruct((T_ALL, N_EXPERTS), F32),
                   jax.ShapeDtypeStruct((T_ALL, LANES), F32),
                   jax.ShapeDtypeStruct((T_ALL, LANES), F32)],
        compiler_params=_cparams("parallel"),
        name="outproj_router",
    )(*a_args, *b_args, *x_args, mod_l, gain_ffn.reshape(1, D_MODEL), w_out_bf16, w_router,
      router_bias.reshape(1, N_EXPERTS))


def _route_body(chosen_ref, ik_ref, dest_ref, first_ref, count_ref, pos_scr):
    n_tiles = T_ALL // TM
    r = lax.broadcasted_iota(jnp.int32, (TM, TM), 0)
    c = lax.broadcasted_iota(jnp.int32, (TM, TM), 1)
    before = (c < r).astype(BF16)

    def count_tile(i, carry):
        rows = pl.ds(pl.multiple_of(i * TM, TM), TM)
        m = chosen_ref[rows, :]
        pos_scr[rows, :] = jnp.dot(before, m.astype(BF16), preferred_element_type=F32) + carry
        return carry + jnp.sum(m, axis=0, keepdims=True)

    counts = lax.fori_loop(0, n_tiles, count_tile, jnp.zeros((1, N_EXPERTS), F32))
    padded = jnp.ceil(counts * (1.0 / MOE_BLK)) * MOE_BLK
    ei = lax.broadcasted_iota(jnp.int32, (N_EXPERTS, N_EXPERTS), 0)
    ej = lax.broadcasted_iota(jnp.int32, (N_EXPERTS, N_EXPERTS), 1)
    end = _dot_hi(jnp.broadcast_to(padded, (8, N_EXPERTS)), (ei <= ej).astype(F32))[0:1]
    start = end - padded

    lane = lax.broadcasted_iota(jnp.int32, (TM, N_EXPERTS), 1).astype(F32)
    slot = lax.broadcasted_iota(jnp.int32, (TM, LANES), 1)

    def dest_tile(i, carry):
        rows = pl.ds(pl.multiple_of(i * TM, TM), TM)
        row_of = pos_scr[rows, :] + start
        ik = ik_ref[rows, :]
        acc = jnp.zeros((TM, LANES), F32)
        for k in range(TOP_K):
            pick = jnp.sum(jnp.where(lane == ik[:, k:k + 1], row_of, 0.0), axis=-1, keepdims=True)
            acc = jnp.where(slot == k, pick, acc)
        dest_ref[rows, :] = acc.astype(jnp.int32)
        return carry

    lax.fori_loop(0, n_tiles, dest_tile, 0)
    first_ref[...] = jnp.broadcast_to(start * (1.0 / MOE_BLK), (8, N_EXPERTS)).astype(jnp.int32)
    count_ref[...] = jnp.broadcast_to(padded * (1.0 / MOE_BLK), (8, N_EXPERTS)).astype(jnp.int32)


def _route(chosen, ik):
    full = lambda shape: pl.BlockSpec(shape, lambda i: (0, 0))
    return pl.pallas_call(
        _route_body,
        grid=(1,),
        in_specs=[full((T_ALL, N_EXPERTS)), full((T_ALL, LANES))],
        out_specs=[full((T_ALL, LANES)), full((8, N_EXPERTS)), full((8, N_EXPERTS))],
        out_shape=[jax.ShapeDtypeStruct((T_ALL, LANES), jnp.int32),
                   jax.ShapeDtypeStruct((8, N_EXPERTS), jnp.int32),
                   jax.ShapeDtypeStruct((8, N_EXPERTS), jnp.int32)],
        scratch_shapes=[pltpu.VMEM((T_ALL, N_EXPERTS), F32)],
        compiler_params=_cparams("arbitrary"),
        name="moe_route",
    )(chosen, ik)


def _sc_worker_id():
    return lax.axis_index("s") * SC_CORES + lax.axis_index("c")


def _sc_dispatch(h2p, dest_chunks):
    n_chunks = T_ALL // DISP_CHUNK
    width = h2p.shape[1]
    mesh = plsc.VectorSubcoreMesh(core_axis_name="c", subcore_axis_name="s")

    @functools.partial(
        pl.kernel, mesh=mesh,
        out_type=jax.ShapeDtypeStruct((MOE_ROWS, width), jnp.int32),
        scratch_types=[pltpu.VMEM((TOP_K, DISP_CHUNK), jnp.int32), pltpu.VMEM((DISP_CHUNK, width), jnp.int32)],
    )
    def run(x_hbm, dest_hbm, xs_hbm, idx_v, rows_v):
        wid = _sc_worker_id()
        for rep in range(-(-n_chunks // SC_WORKERS)):
            chunk = wid + rep * SC_WORKERS

            @pl.when(chunk < n_chunks)
            def _():
                pltpu.sync_copy(dest_hbm.at[chunk], idx_v)
                pltpu.sync_copy(x_hbm.at[pl.ds(chunk * DISP_CHUNK, DISP_CHUNK)], rows_v)
                for k in range(TOP_K):
                    pltpu.sync_copy(rows_v, xs_hbm.at[idx_v.at[k]])

    return run(h2p, dest_chunks)


def _sc_collect(y, dest_flat):
    per_worker = T_ALL // SC_WORKERS
    n_chunks = per_worker // COLLECT_CHUNK
    n_steps = TOP_K * n_chunks
    width = y.shape[1]
    mesh = plsc.VectorSubcoreMesh(core_axis_name="c", subcore_axis_name="s")

    @functools.partial(
        pl.kernel, mesh=mesh,
        out_type=jax.ShapeDtypeStruct((TOP_K * T_ALL, width), y.dtype),
        scratch_types=[pltpu.VMEM((TOP_K * per_worker,), jnp.int32),
                       pltpu.VMEM((COLLECT_CHUNK, width), y.dtype), pltpu.VMEM((COLLECT_CHUNK, width), y.dtype),
                       pltpu.SemaphoreType.DMA, pltpu.SemaphoreType.DMA],
    )
    def run(y_hbm, dest_hbm, yg_hbm, idx_v, rows0, rows1, sem0, sem1):
        wid = _sc_worker_id()
        bufs = ((rows0, sem0), (rows1, sem1))
        for k in range(TOP_K):
            pltpu.sync_copy(dest_hbm.at[pl.ds(k * T_ALL + wid * per_worker, per_worker)],
                            idx_v.at[pl.ds(k * per_worker, per_worker)])

        def gather(step, buf):
            rows, sem = buf
            idx = idx_v.at[pl.ds(pl.multiple_of(step * COLLECT_CHUNK, 8), COLLECT_CHUNK)]
            return pltpu.make_async_copy(y_hbm.at[idx], rows, sem)

        def out_rows(step):
            off = (step // n_chunks) * T_ALL + wid * per_worker + (step % n_chunks) * COLLECT_CHUNK
            return yg_hbm.at[pl.ds(pl.multiple_of(off, 8), COLLECT_CHUNK)]

        gather(0, bufs[0]).start()

        @pl.loop(0, n_steps, step=2)
        def _(base):
            for j in range(2):
                step = base + j

                @pl.when(step + 1 < n_steps)
                def _():
                    gather(step + 1, bufs[1 - j]).start()

                gather(step, bufs[j]).wait()
                pltpu.sync_copy(bufs[j][0], out_rows(step))

    return run(y, dest_flat)


def _expert_body(first_ref, count_ref, xs_hbm, wg_ref, wu_ref, wd_ref, y_hbm,
                 wg_bf, wu_bf, wd_bf, x_buf, y_buf, in_sem, out_sem):
    e = pl.program_id(0)
    first = first_ref[e]
    count = count_ref[e]
    n_used = first_ref[N_EXPERTS - 1] + count_ref[N_EXPERTS - 1]
    half = D_MODEL // 2
    wg_bf[...] = wg_ref[0, 0].astype(BF16)
    wu_bf[...] = wu_ref[0, 0].astype(BF16)
    wd_bf[...] = wd_ref[0, 0].astype(BF16)

    def part_rows(g, part, n_parts):
        size = MOE_BLK // n_parts
        return pl.ds(pl.multiple_of(g * MOE_BLK + part * size, size), size), pl.ds(part * size, size)

    def in_copies(g):
        slot = g & (EXPERT_SLOTS - 1)
        out = []
        for part in range(EXPERT_IN_PARTS):
            src, dst = part_rows(g, part, EXPERT_IN_PARTS)
            out.append(pltpu.make_async_copy(xs_hbm.at[src], x_buf.at[slot, dst], in_sem.at[slot]))
        return out

    def out_copies(g):
        slot = g & (EXPERT_SLOTS - 1)
        out = []
        for part in range(EXPERT_OUT_PARTS):
            dst, src = part_rows(g, part, EXPERT_OUT_PARTS)
            out.append(pltpu.make_async_copy(y_buf.at[slot, src], y_hbm.at[dst], out_sem.at[slot]))
        return out

    @pl.when((first == 0) & (count > 0))
    def _():
        for ahead in range(EXPERT_SLOTS - 1):
            @pl.when(ahead < n_used)
            def _():
                for cp in in_copies(ahead):
                    cp.start()

    def block(b, carry):
        g = first + b
        slot = g & (EXPERT_SLOTS - 1)
        for cp in in_copies(g):
            cp.wait()

        @pl.when(g + EXPERT_SLOTS - 1 < n_used)
        def _():
            for cp in in_copies(g + EXPERT_SLOTS - 1):
                cp.start()

        @pl.when(g >= EXPERT_SLOTS)
        def _():
            for cp in out_copies(g - EXPERT_SLOTS):
                cp.wait()

        hi, lo = _unpack_bf16_pairs(x_buf[slot])

        def proj(w_bf):
            return (jnp.dot(hi, w_bf[0:half, :], preferred_element_type=F32)
                    + jnp.dot(lo, w_bf[half:, :], preferred_element_type=F32))

        hid = _silu(proj(wg_bf)) * proj(wu_bf)
        y_buf[slot] = _pack_bf16_pairs(jnp.dot(hid.astype(BF16), wd_bf[...], preferred_element_type=F32))
        for cp in out_copies(g):
            cp.start()
        return carry

    lax.fori_loop(0, count, block, 0)

    @pl.when(e == N_EXPERTS - 1)
    def _():
        for back in range(EXPERT_SLOTS, 0, -1):
            @pl.when(n_used >= back)
            def _():
                for cp in out_copies(n_used - back):
                    cp.wait()


EXPERT_SLOTS = 4
EXPERT_IN_PARTS = 2
EXPERT_OUT_PARTS = 4


def _experts(first_blk, n_blk, xs, layer, w_gate, w_up, w_down):
    w_in = pl.BlockSpec((1, 1, D_MODEL, D_EXPERT), lambda e, first, count: (layer, e, 0, 0))
    grid_spec = pltpu.PrefetchScalarGridSpec(
        num_scalar_prefetch=2,
        grid=(N_EXPERTS,),
        in_specs=[pl.BlockSpec(memory_space=pl.ANY), w_in, w_in,
                  pl.BlockSpec((1, 1, D_EXPERT, D_MODEL), lambda e, first, count: (layer, e, 0, 0))],
        out_specs=pl.BlockSpec(memory_space=pl.ANY),
        scratch_shapes=[pltpu.VMEM((D_MODEL, D_EXPERT), BF16), pltpu.VMEM((D_MODEL, D_EXPERT), BF16),
                        pltpu.VMEM((D_EXPERT, D_MODEL), BF16),
                        pltpu.VMEM((EXPERT_SLOTS, MOE_BLK, D_MODEL // 2), jnp.int32),
                        pltpu.VMEM((EXPERT_SLOTS, MOE_BLK, D_MODEL // 2), jnp.int32),
                        pltpu.SemaphoreType.DMA((EXPERT_SLOTS,)), pltpu.SemaphoreType.DMA((EXPERT_SLOTS,))],
    )
    return pl.pallas_call(
        _expert_body,
        grid_spec=grid_spec,
        out_shape=jax.ShapeDtypeStruct((MOE_ROWS, D_MODEL // 2), jnp.int32),
        compiler_params=_cparams("arbitrary"),
        name="moe_experts",
    )(first_blk, n_blk, xs, w_gate, w_up, w_down)


def _combine_body(x1_ref, h2_ref, yg_ref, gk_ref, mod_ref, sg_ref, su_ref, sd_ref, fn_ref, *o_refs, final):
    hi, lo = _unpack_bf16_pairs(h2_ref[...])
    half = D_MODEL // 2

    def proj(w_ref):
        return _dot(hi, w_ref[0:half, :]) + _dot(lo, w_ref[half:, :])

    shared = _dot(_silu(proj(sg_ref)) * proj(su_ref), sd_ref[...])
    acc_hi, acc_lo = shared[:, :half], shared[:, half:]
    gk = gk_ref[...]
    for k in range(TOP_K):
        y_hi, y_lo = _unpack_bf16_pairs(yg_ref[k])
        acc_hi = acc_hi + gk[:, k:k + 1] * y_hi.astype(F32)
        acc_lo = acc_lo + gk[:, k:k + 1] * y_lo.astype(F32)
    acc = jnp.concatenate([acc_hi, acc_lo], axis=1)
    m = mod_ref[0]
    y = x1_ref[...] + m[:, 5 * D_MODEL:6 * D_MODEL] * acc
    if not final:
        o_refs[0][...] = y
        return
    y = _rms(y, fn_ref[...])
    is_ctx = pl.program_id(0) < N_CTX_TILES

    @pl.when(is_ctx)
    def _():
        o_refs[0][...] = y

    @pl.when(jnp.logical_not(is_ctx))
    def _():
        o_refs[1][...] = y


def _combine(x1, h2p, yg, gk, mod_l, ws_gate, ws_up, ws_down, final_norm, final):
    tok = lambda shape: pl.BlockSpec(shape, lambda i: (i, 0))
    full = lambda shape: pl.BlockSpec(shape, lambda i: (0, 0))
    if final:
        out_specs, _ = _token_specs((None, None), D_MODEL)
        out_shape = [jax.ShapeDtypeStruct((T_CTX, D_MODEL), F32), jax.ShapeDtypeStruct((T_LAT, D_MODEL), F32)]
    else:
        out_specs = tok((TM, D_MODEL))
        out_shape = jax.ShapeDtypeStruct((T_ALL, D_MODEL), F32)
    return pl.pallas_call(
        functools.partial(_combine_body, final=final),
        grid=(T_ALL // TM,),
        in_specs=[tok((TM, D_MODEL)), tok((TM, D_MODEL // 2)),
                  pl.BlockSpec((TOP_K, TM, D_MODEL // 2), lambda i: (0, i, 0)),
                  tok((TM, LANES)),
                  pl.BlockSpec((1, 1, N_MOD * D_MODEL), lambda i: (_mod_row(i), 0, 0)),
                  full((D_MODEL, D_EXPERT)), full((D_MODEL, D_EXPERT)), full((D_EXPERT, D_MODEL)),
                  full((1, D_MODEL))],
        out_specs=out_specs,
        out_shape=out_shape,
        compiler_params=_cparams("arbitrary"),
        name="moe_combine",
    )(x1, h2p, yg, gk, mod_l, ws_gate, ws_up, ws_down, final_norm.reshape(1, D_MODEL))


def _moe(x1, h2p, chosen, gk, ik, mod_l, layer, w_gate, w_up, w_down, ws_gate, ws_up, ws_down, final_norm, final):
    dest, first_blk, n_blk = _route(chosen, ik)
    dest = dest[:, :TOP_K]
    dest_chunks = dest.reshape(T_ALL // DISP_CHUNK, DISP_CHUNK, TOP_K).transpose(0, 2, 1)
    xs = _sc_dispatch(h2p, dest_chunks)
    y = _experts(first_blk[0], n_blk[0], xs, layer, w_gate, w_up, w_down)
    yg = _sc_collect(y, dest.T.reshape(-1)).reshape(TOP_K, T_ALL, D_MODEL // 2)
    return _combine(x1, h2p, yg, gk, mod_l, ws_gate.astype(BF16), ws_up.astype(BF16), ws_down.astype(BF16),
                    final_norm, final)


def kernel(x_prompt, x_sample, cache_a_k, cache_a_v, cache_b_k, cache_b_v, state_d_fwd, state_d_bwd, c, c_ctx, w_ada, b_ada, norm_mix, norm_ffn, w_in_attn, w_out_attn, sink_a, rpb_b, w_in_rec, w_out_rec, conv_w, conv_b, filt_w1, filt_b1, filt_w2, filt_b2, filt_w3, filt_b3, filt_freq, filt_w4, d_skip, lb_fwd, lb_bwd, norm_d, w_router, router_bias, w_gate, w_up, w_down, ws_gate, ws_up, ws_down, final_norm):
    x = (x_prompt.reshape(T_CTX, D_MODEL), x_sample.reshape(T_LAT, D_MODEL))
    cvec = jnp.concatenate([c_ctx[None, :], c], axis=0)
    mod = _ada(cvec, w_ada, b_ada).reshape(DEPTH, CVEC_PAD, 1, N_MOD * D_MODEL)

    new_kv = None
    new_state = None
    for l in range(DEPTH):
        j = l // 2
        final = l == DEPTH - 1
        if l % 2 == 0:
            qkv = _inproj(x, mod[l], norm_mix[l], w_in_attn[j].astype(BF16))
            oa_ctx, ob_ctx, *new_kv = _ctx_attn(qkv, sink_a[j])
            new_kv = tuple(new_kv)
            q_rot, k_rot = _rope(qkv)
            cache = lambda t: t[:, j].reshape(DEC_BATCH, PAST_LEN, -1)
            oa_lat = _win_attn(qkv, q_rot, k_rot, cache(cache_a_k), cache(cache_a_v), sink_a[j])
            ob_lat = _na_attn(qkv, cache(cache_b_k), cache(cache_b_v), _na_rel_rows(rpb_b[j]))
            mix_a = (oa_ctx, oa_lat)
            mix_b = (ob_ctx, ob_lat)
            w_out = w_out_attn[j]
        else:
            u = _inproj(x, mod[l], norm_mix[l], w_in_rec[j].astype(BF16))
            filt = (filt_w1[j], filt_b1[j], filt_w2[j], filt_b2[j], filt_w3[j], filt_b3[j], filt_freq[j],
                    filt_w4[j])
            y_ctx = _hyena(u, 0, BATCH, SEQ, conv_w[j], conv_b[j], d_skip[j], _hyena_filter(SEQ, filt))
            y_lat = _hyena(u, T_CTX // DEC_SEQ, DEC_BATCH, DEC_SEQ, conv_w[j], conv_b[j], d_skip[j],
                           _hyena_filter(DEC_SEQ, filt))
            zeros = jnp.zeros((BATCH, D_HEADS, D_KDIM, D_VDIM), F32)
            o_ctx, s_f, s_b = _hgrn(u, 0, BATCH, SEQ, lb_fwd, lb_bwd, norm_d[j], zeros, zeros, l)
            o_lat, _, _ = _hgrn(u, T_CTX // DEC_SEQ, DEC_BATCH, DEC_SEQ, lb_fwd, lb_bwd, norm_d[j],
                                state_d_fwd[:, j], state_d_bwd[:, j], l)
            new_state = (s_f[:, None], s_b[:, None])
            mix_a = (y_ctx, y_lat)
            mix_b = (o_ctx, o_lat)
            w_out = w_out_rec[j]
        x1, h2p, chosen, gk, ik = _outproj(mix_a, mix_b, x, mod[l], norm_ffn[l], w_out.astype(BF16), w_router[l],
                                           router_bias[l])
        x = _moe(x1, h2p, chosen, gk, ik, mod[l], l, w_gate, w_up, w_down, ws_gate[l], ws_up[l],
                 ws_down[l], final_norm, final)

    y_prompt = x[0].reshape(BATCH, SEQ, D_MODEL)
    y_sample = x[1].reshape(DEC_BATCH, DEC_SEQ, D_MODEL)
    return (y_prompt, y_sample) + new_kv + new_state
```

```python
import functools
import math

import numpy as np
import jax
import jax.numpy as jnp
from jax import lax
from jax.experimental import pallas as pl
from jax.experimental.pallas import tpu as pltpu
from jax.experimental.pallas import tpu_sc as plsc

F32 = jnp.float32
BF16 = jnp.bfloat16
HI = lax.Precision.HIGHEST

D_MODEL = 1024
BATCH = 16
SEQ = 256
DEPTH = 2
DEC_BATCH = 2
DEC_SEQ = 1024
PAST_LEN = 512
GRID_W = 64
HEAD_DIM = 64
N_MOD = 6
RMS_EPS = 1e-6
A_HEADS = 8
A_KV_HEADS = 2
A_GROUP = A_HEADS // A_KV_HEADS
WINDOW = 128
ROPE_BASE = 10000.0
B_HEADS = 8
NA_ROWS = 8
NA_COLS = 16
C_DIM = 512
C_EMB = 33
C_FFN = 64
HYENA_MIN_DECAY = math.log(1e-2) / 1.5
HYENA_MAX_DECAY = math.log(1e-2) / 0.3
D_KDIM = 128
D_VDIM = 128
D_HEADS = 4
N_EXPERTS = 64
TOP_K = 8
D_EXPERT = 256
ROUTE_SCALE = 2.5
A_Q = A_HEADS * HEAD_DIM
A_KV = A_KV_HEADS * HEAD_DIM
B_W = B_HEADS * HEAD_DIM
ATTN_IN = A_Q + 2 * A_KV + 3 * B_W
REC_IN = 3 * C_DIM + 5 * D_HEADS * D_KDIM

T_CTX = BATCH * SEQ
T_LAT = DEC_BATCH * DEC_SEQ
T_ALL = T_CTX + T_LAT
N_CVEC = 1 + DEC_BATCH
CVEC_PAD = 8
TM = 256
MASK_NEG = -1e30
GLA_CHUNK = 64
GLA_SPAN = 256
DFT_CHUNK = 256
MOE_BLK = 256
MOE_NBLK = -(-(T_ALL * TOP_K + N_EXPERTS * (MOE_BLK - 1)) // MOE_BLK)
MOE_ROWS = MOE_NBLK * MOE_BLK
SC_CORES = 2
SC_SUBCORES = 16
SC_WORKERS = SC_CORES * SC_SUBCORES
DISP_CHUNK = 128
COLLECT_CHUNK = 64
VMEM_LIMIT = 56 * 1024 * 1024


def _cparams(*sem):
    return pltpu.CompilerParams(dimension_semantics=sem, vmem_limit_bytes=VMEM_LIMIT)


def _mod_row(i):
    return jnp.where(i < T_CTX // TM, 0, 1 + (i - T_CTX // TM) // (DEC_SEQ // TM))


def _dot(a, b):
    return jnp.dot(a.astype(BF16), b.astype(BF16), preferred_element_type=F32)


def _dot_nt(a, b):
    return lax.dot_general(a.astype(BF16), b.astype(BF16), (((1,), (1,)), ((), ())),
                           preferred_element_type=F32)


def _dot_tn(a, b):
    return lax.dot_general(a.astype(BF16), b.astype(BF16), (((0,), (0,)), ((), ())),
                           preferred_element_type=F32)


def _dot_hi(a, b):
    return jnp.dot(a, b, precision=HI, preferred_element_type=F32)


def _silu(x):
    return x * jax.nn.sigmoid(x)


def _rms(x, g):
    return x * lax.rsqrt(jnp.mean(x * x, axis=-1, keepdims=True) + RMS_EPS) * g


ADA_TN = 1536
ADA_UNROLL = 4


def _ada_body(cb_ref, w_ref, b_ref, o_ref):
    tn = o_ref.shape[-1]
    n_slab = tn // LANES

    def step(k8, accs):
        r0 = pl.multiple_of(k8 * 8, 8)
        sk = [_silu(cb_ref[j, pl.ds(r0, 8), :]) for j in range(N_CVEC)]
        out = []
        for s in range(n_slab):
            wk = w_ref[0, pl.ds(r0, 8), s * LANES:(s + 1) * LANES]
            out.extend(accs[s * N_CVEC + j] + wk * sk[j] for j in range(N_CVEC))
        return tuple(out)

    accs = lax.fori_loop(0, D_MODEL // 8, step,
                         tuple(jnp.zeros((8, LANES), F32) for _ in range(n_slab * N_CVEC)), unroll=ADA_UNROLL)
    o_ref[0] = jnp.zeros((CVEC_PAD, tn), F32)
    for s in range(n_slab):
        for j in range(N_CVEC):
            o_ref[0, j:j + 1, s * LANES:(s + 1) * LANES] = (
                jnp.sum(accs[s * N_CVEC + j], axis=0, keepdims=True) + b_ref[0, :, s * LANES:(s + 1) * LANES])


def _ada(cvec, w_ada, b_ada):
    n_out = N_MOD * D_MODEL
    c_lanes = jnp.broadcast_to(cvec[:, :, None], (N_CVEC, D_MODEL, LANES))
    return pl.pallas_call(
        _ada_body,
        grid=(DEPTH, n_out // ADA_TN),
        in_specs=[pl.BlockSpec((N_CVEC, D_MODEL, LANES), lambda l, n: (0, 0, 0)),
                  pl.BlockSpec((1, D_MODEL, ADA_TN), lambda l, n: (l, 0, n)),
                  pl.BlockSpec((1, 1, ADA_TN), lambda l, n: (l, 0, n))],
        out_specs=pl.BlockSpec((1, CVEC_PAD, ADA_TN), lambda l, n: (l, 0, n)),
        out_shape=jax.ShapeDtypeStruct((DEPTH, CVEC_PAD, n_out), F32),
        compiler_params=_cparams("parallel", "parallel"),
        name="ada",
    )(c_lanes, w_ada, b_ada.reshape(DEPTH, 1, n_out))


N_CTX_TILES = T_CTX // TM


def _token_specs(x, width):
    if not isinstance(x, tuple):
        return [pl.BlockSpec((TM, width), lambda i: (i, 0))], (x,)
    return ([pl.BlockSpec((TM, width), lambda i: (jnp.minimum(i, N_CTX_TILES - 1), 0)),
             pl.BlockSpec((TM, width), lambda i: (jnp.maximum(i - N_CTX_TILES, 0), 0))], x)


def _token_tile(refs):
    if len(refs) == 1:
        return refs[0][...]
    return jnp.where(pl.program_id(0) < N_CTX_TILES, refs[0][...], refs[1][...])


def _inproj_body(*refs, n_x):
    x_refs, (mod_ref, g_ref, w_ref, o_ref) = refs[:n_x], refs[n_x:]
    m = mod_ref[0]
    h = _rms(_token_tile(x_refs), g_ref[...]) * (1.0 + m[:, D_MODEL:2 * D_MODEL]) + m[:, 0:D_MODEL]
    o_ref[...] = _dot(h, w_ref[...])


def _inproj(x, mod_l, gain, w_bf16):
    n = w_bf16.shape[1]
    x_specs, x_args = _token_specs(x, D_MODEL)
    return pl.pallas_call(
        functools.partial(_inproj_body, n_x=len(x_args)),
        grid=(T_ALL // TM,),
        in_specs=x_specs + [pl.BlockSpec((1, 1, N_MOD * D_MODEL), lambda i: (_mod_row(i), 0, 0)),
                            pl.BlockSpec((1, D_MODEL), lambda i: (0, 0)),
                            pl.BlockSpec((D_MODEL, n), lambda i: (0, 0))],
        out_specs=pl.BlockSpec((TM, n), lambda i: (i, 0)),
        out_shape=jax.ShapeDtypeStruct((T_ALL, n), F32),
        compiler_params=_cparams("parallel"),
        name="inproj",
    )(*x_args, mod_l, gain.reshape(1, D_MODEL), w_bf16)


def _head_cols(h):
    return slice(h * HEAD_DIM, (h + 1) * HEAD_DIM)


def _group_rows(ref, rows, first_col, sink_ref, hk):
    n = rows.stop - rows.start
    q = jnp.concatenate([ref[rows, first_col + g * HEAD_DIM:first_col + (g + 1) * HEAD_DIM]
                         for g in range(A_GROUP)], axis=0)
    sink = jnp.concatenate([jnp.broadcast_to(sink_ref[:, hk * A_GROUP + g:hk * A_GROUP + g + 1], (n, 1))
                            for g in range(A_GROUP)], axis=0)
    return q, sink


def _ctx_attn_body(qkv_ref, sink_ref, oa_ref, ob_ref, ak_ref, av_ref, bk_ref, bv_ref):
    scale = HEAD_DIM ** -0.5
    rows = slice(0, SEQ)

    def attend(q, k, v, sink):
        s = _dot_nt(q, k) * scale
        m = jnp.max(s, axis=-1, keepdims=True)
        if sink is not None:
            m = jnp.maximum(m, sink)
        p = jnp.exp(s - m)
        den = jnp.sum(p, axis=-1, keepdims=True)
        if sink is not None:
            den = den + jnp.exp(sink - m)
        return _dot(p, v) / den

    for hk in range(A_KV_HEADS):
        k = qkv_ref[:, A_Q + hk * HEAD_DIM:A_Q + (hk + 1) * HEAD_DIM]
        v = qkv_ref[:, A_Q + A_KV + hk * HEAD_DIM:A_Q + A_KV + (hk + 1) * HEAD_DIM]
        ak_ref[0, 0, :, hk, :] = k
        av_ref[0, 0, :, hk, :] = v
        q, sink = _group_rows(qkv_ref, rows, hk * A_GROUP * HEAD_DIM, sink_ref, hk)
        o = attend(q, k, v, sink)
        for g in range(A_GROUP):
            oa_ref[:, _head_cols(hk * A_GROUP + g)] = o[g * SEQ:(g + 1) * SEQ]
    base = A_Q + 2 * A_KV
    for h in range(B_HEADS):
        q = qkv_ref[:, base + h * HEAD_DIM:base + (h + 1) * HEAD_DIM]
        k = qkv_ref[:, base + B_W + h * HEAD_DIM:base + B_W + (h + 1) * HEAD_DIM]
        v = qkv_ref[:, base + 2 * B_W + h * HEAD_DIM:base + 2 * B_W + (h + 1) * HEAD_DIM]
        bk_ref[0, 0, :, h, :] = k
        bv_ref[0, 0, :, h, :] = v
        ob_ref[:, _head_cols(h)] = attend(q, k, v, None)


def _ctx_attn(qkv, sink):
    kv_spec = lambda heads: pl.BlockSpec((1, 1, SEQ, heads, HEAD_DIM), lambda b: (b, 0, 0, 0, 0))
    kv_sd = lambda heads: jax.ShapeDtypeStruct((BATCH, 1, SEQ, heads, HEAD_DIM), F32)
    return pl.pallas_call(
        _ctx_attn_body,
        grid=(BATCH,),
        in_specs=[pl.BlockSpec((SEQ, ATTN_IN), lambda b: (b, 0)),
                  pl.BlockSpec((1, A_HEADS), lambda b: (0, 0))],
        out_specs=[pl.BlockSpec((SEQ, A_Q), lambda b: (b, 0)), pl.BlockSpec((SEQ, B_W), lambda b: (b, 0)),
                   kv_spec(A_KV_HEADS), kv_spec(A_KV_HEADS), kv_spec(B_HEADS), kv_spec(B_HEADS)],
        out_shape=[jax.ShapeDtypeStruct((T_CTX, A_Q), F32), jax.ShapeDtypeStruct((T_CTX, B_W), F32),
                   kv_sd(A_KV_HEADS), kv_sd(A_KV_HEADS), kv_sd(B_HEADS), kv_sd(B_HEADS)],
        compiler_params=_cparams("parallel"),
        name="ctx_attn",
    )(qkv, sink.reshape(1, A_HEADS))


@functools.lru_cache(maxsize=None)
def _rope_tables(width):
    half = HEAD_DIM // 2
    t = np.arange(DEC_SEQ)
    inv = ROPE_BASE ** (-np.arange(0, half, 2, dtype=np.float64) / half)
    ang_r = (t // GRID_W)[:, None] * inv[None, :]
    ang_c = (t % GRID_W)[:, None] * inv[None, :]
    cos = np.concatenate([np.cos(ang_r)] * 2 + [np.cos(ang_c)] * 2, axis=-1)
    sin = np.concatenate([-np.sin(ang_r), np.sin(ang_r), -np.sin(ang_c), np.sin(ang_c)], axis=-1)
    reps = width // HEAD_DIM
    return (np.tile(cos, (1, reps)).astype(np.float32), np.tile(sin, (1, reps)).astype(np.float32))


def _rope_body(q_ref, k_ref, cq_ref, sq_ref, ck_ref, sk_ref, qo_ref, ko_ref):
    quarter = HEAD_DIM // 4

    def rot(x, cos, sin):
        w = x.shape[-1]
        lane = lax.broadcasted_iota(jnp.int32, x.shape, 1)
        fwd = pltpu.roll(x, w - quarter, axis=1)
        bwd = pltpu.roll(x, quarter, axis=1)
        partner = jnp.where((lane & (2 * quarter - 1)) < quarter, fwd, bwd)
        return x * cos + partner * sin

    qo_ref[...] = rot(q_ref[...], cq_ref[...], sq_ref[...])
    ko_ref[...] = rot(k_ref[...], ck_ref[...], sk_ref[...])


def _rope(qkv):
    cq, sq = _rope_tables(A_Q)
    ck, sk = _rope_tables(A_KV)
    tab = lambda w: pl.BlockSpec((DEC_SEQ, w), lambda b: (0, 0))
    row0 = T_CTX // DEC_SEQ
    return pl.pallas_call(
        _rope_body,
        grid=(DEC_BATCH,),
        in_specs=[pl.BlockSpec((DEC_SEQ, A_Q), lambda b: (row0 + b, 0)),
                  pl.BlockSpec((DEC_SEQ, A_KV), lambda b: (row0 + b, A_Q // A_KV)),
                  tab(A_Q), tab(A_Q), tab(A_KV), tab(A_KV)],
        out_specs=[pl.BlockSpec((DEC_SEQ, A_Q), lambda b: (b, 0)),
                   pl.BlockSpec((DEC_SEQ, A_KV), lambda b: (b, 0))],
        out_shape=[jax.ShapeDtypeStruct((T_LAT, A_Q), F32), jax.ShapeDtypeStruct((T_LAT, A_KV), F32)],
        compiler_params=_cparams("parallel"),
        name="rope",
    )(qkv, qkv, jnp.asarray(cq), jnp.asarray(sq), jnp.asarray(ck), jnp.asarray(sk))


WIN_QB = 256


def _pick_head(x, h, n_heads):
    out = x[:, _head_cols(0)]
    for i in range(1, n_heads):
        out = jnp.where(h == i, x[:, _head_cols(i)], out)
    return out


def _win_attn_body(qraw_ref, qrot_ref, krot_ref, v_ref, kc_ref, vc_ref, sink_ref, o_ref):
    scale = HEAD_DIM ** -0.5
    hk = pl.program_id(1)
    k = _pick_head(krot_ref[...], hk, A_KV_HEADS)
    v = _pick_head(v_ref[...], hk, A_KV_HEADS)
    kc = _pick_head(kc_ref[0], hk, A_KV_HEADS)
    vc = _pick_head(vc_ref[0], hk, A_KV_HEADS)
    head_lane = lax.broadcasted_iota(jnp.int32, (1, A_HEADS), 1)
    sinks = [jnp.sum(jnp.where(head_lane == hk * A_GROUP + g, sink_ref[...], 0.0), axis=-1, keepdims=True)
             for g in range(A_GROUP)]
    sink = jnp.concatenate([jnp.broadcast_to(s, (WIN_QB, 1)) for s in sinks], axis=0)
    for qb in range(DEC_SEQ // WIN_QB):
        q0 = qb * WIN_QB
        rows = slice(q0, q0 + WIN_QB)
        lo = max(0, q0 - WINDOW)
        hi = min(DEC_SEQ, q0 + WIN_QB + WINDOW)
        q_rot = jnp.concatenate([qrot_ref[rows, _head_cols(g)] for g in range(A_GROUP)], axis=0)
        q_raw = jnp.concatenate([qraw_ref[rows, _head_cols(g)] for g in range(A_GROUP)], axis=0)
        s_loc = _dot_nt(q_rot, k[lo:hi]) * scale
        qpos = q0 + (lax.broadcasted_iota(jnp.int32, s_loc.shape, 0) & (WIN_QB - 1))
        kpos = lo + lax.broadcasted_iota(jnp.int32, s_loc.shape, 1)
        s_loc = jnp.where(jnp.abs(kpos - qpos) <= WINDOW, s_loc, MASK_NEG)
        s_ctx = _dot_nt(q_raw, kc) * scale
        m = jnp.maximum(jnp.maximum(jnp.max(s_loc, axis=-1, keepdims=True),
                                    jnp.max(s_ctx, axis=-1, keepdims=True)), sink)
        p_loc = jnp.exp(s_loc - m)
        p_ctx = jnp.exp(s_ctx - m)
        den = (jnp.sum(p_loc, axis=-1, keepdims=True) + jnp.sum(p_ctx, axis=-1, keepdims=True)
               + jnp.exp(sink - m))
        o = (_dot(p_ctx, vc) + _dot(p_loc, v[lo:hi])) / den
        for g in range(A_GROUP):
            o_ref[rows, _head_cols(g)] = o[g * WIN_QB:(g + 1) * WIN_QB]


def _win_attn(qkv, q_rot, k_rot, kc, vc, sink):
    row0 = T_CTX // DEC_SEQ
    gw = A_GROUP * HEAD_DIM
    return pl.pallas_call(
        _win_attn_body,
        grid=(DEC_BATCH, A_KV_HEADS),
        in_specs=[pl.BlockSpec((DEC_SEQ, gw), lambda b, h: (row0 + b, h)),
                  pl.BlockSpec((DEC_SEQ, gw), lambda b, h: (b, h)),
                  pl.BlockSpec((DEC_SEQ, A_KV), lambda b, h: (b, 0)),
                  pl.BlockSpec((DEC_SEQ, A_KV), lambda b, h: (row0 + b, (A_Q + A_KV) // A_KV)),
                  pl.BlockSpec((1, PAST_LEN, A_KV), lambda b, h: (b, 0, 0)),
                  pl.BlockSpec((1, PAST_LEN, A_KV), lambda b, h: (b, 0, 0)),
                  pl.BlockSpec((1, A_HEADS), lambda b, h: (0, 0))],
        out_specs=pl.BlockSpec((DEC_SEQ, gw), lambda b, h: (b, h)),
        out_shape=jax.ShapeDtypeStruct((T_LAT, A_Q), F32),
        compiler_params=_cparams("parallel", "parallel"),
        name="win_attn",
    )(qkv, q_rot, k_rot, qkv, kc, vc, sink.reshape(1, A_HEADS))


GRID_ROWS = DEC_SEQ // GRID_W
NA_BAND = min(NA_ROWS, GRID_ROWS)


NA_REL_ROWS = 2 * NA_ROWS - 1
NA_REL_COLS = 2 * NA_COLS - 1
LANES = 128


def _na_rel_rows(rpb):
    pad = jnp.zeros((B_HEADS, NA_REL_ROWS, GRID_W - NA_REL_COLS), F32)
    one = jnp.concatenate([rpb, pad], axis=-1)
    nxt = jnp.concatenate([one[:, 1:], jnp.zeros((B_HEADS, 1, GRID_W), F32)], axis=1)
    both = jnp.concatenate([one, nxt], axis=-1)
    return jnp.concatenate([both, jnp.zeros((B_HEADS, 16 - NA_REL_ROWS, LANES), F32)], axis=1)


NA_HEADS_PER_STEP = LANES // HEAD_DIM


def _na_row_groups():
    groups = []
    for r in range(GRID_ROWS):
        rs = min(max(r - NA_ROWS // 2, 0), GRID_ROWS - NA_BAND)
        if groups and groups[-1][2] == rs:
            groups[-1][1] += 1
        else:
            groups.append([r, 1, rs])
    return groups


def _na_attn_body(q_ref, k_ref, v_ref, kc_ref, vc_ref, rel_ref, o_ref):
    scale = HEAD_DIM ** -0.5
    cq = lax.broadcasted_iota(jnp.int32, (GRID_W, LANES), 0)
    kcol = lax.broadcasted_iota(jnp.int32, (GRID_W, LANES), 1) & (GRID_W - 1)
    cs = jnp.clip(cq - NA_COLS // 2, 0, GRID_W - NA_COLS)
    col_ok = (kcol >= cs) & (kcol < cs + NA_COLS)
    for hh in range(NA_HEADS_PER_STEP):
        cols = _head_cols(hh)
        kc = kc_ref[0, :, cols]
        vc = vc_ref[0, :, cols]
        tiles = {}

        def pair_tile(a):
            if a not in tiles:
                x = jnp.broadcast_to(rel_ref[hh, a:a + 1, :], (GRID_W, LANES))
                t = pltpu.roll(x, LANES - (NA_COLS - 1), axis=1, stride=1, stride_axis=0)
                tiles[a] = jnp.where(col_ok, t, MASK_NEG)
            return tiles[a]

        for r0, n_r, rs in _na_row_groups():
            bias = jnp.concatenate(
                [jnp.concatenate([pair_tile(rs - r + NA_ROWS - 1 + 2 * i) for i in range(NA_BAND // 2)], axis=1)
                 for r in range(r0, r0 + n_r)], axis=0)
            rows = slice(r0 * GRID_W, (r0 + n_r) * GRID_W)
            band = slice(rs * GRID_W, (rs + NA_BAND) * GRID_W)
            q = q_ref[rows, cols]
            s_loc = _dot_nt(q, k_ref[band, cols]) * scale + bias
            s_ctx = _dot_nt(q, kc) * scale
            m = jnp.maximum(jnp.max(s_loc, axis=-1, keepdims=True), jnp.max(s_ctx, axis=-1, keepdims=True))
            p_loc = jnp.exp(s_loc - m)
            p_ctx = jnp.exp(s_ctx - m)
            den = jnp.sum(p_loc, axis=-1, keepdims=True) + jnp.sum(p_ctx, axis=-1, keepdims=True)
            o_ref[rows, cols] = (_dot(p_ctx, vc) + _dot(p_loc, v_ref[band, cols])) / den


def _na_attn(qkv, kc, vc, rel):
    row0 = T_CTX // DEC_SEQ
    col0 = (A_Q + 2 * A_KV) // LANES
    n_blk = B_W // LANES
    col = lambda j: pl.BlockSpec((DEC_SEQ, LANES), lambda b, p: (row0 + b, col0 + j * n_blk + p))
    cache = pl.BlockSpec((1, PAST_LEN, LANES), lambda b, p: (b, 0, p))
    return pl.pallas_call(
        _na_attn_body,
        grid=(DEC_BATCH, n_blk),
        in_specs=[col(0), col(1), col(2), cache, cache,
                  pl.BlockSpec((NA_HEADS_PER_STEP, 16, LANES), lambda b, p: (p, 0, 0))],
        out_specs=pl.BlockSpec((DEC_SEQ, LANES), lambda b, p: (b, p)),
        out_shape=jax.ShapeDtypeStruct((T_LAT, B_W), F32),
        compiler_params=_cparams("parallel", "parallel"),
        name="na_attn",
    )(qkv, qkv, qkv, kc, vc, rel)


@functools.lru_cache(maxsize=None)
def _dft_mats(L):
    n = 2 * L
    fc = min(L, DFT_CHUNK)
    f = np.arange(L)[:, None]
    t = np.arange(L)[None, :]
    ang = 2.0 * np.pi * ((f * t) % n) / n
    m1 = np.cos(ang)
    m2 = np.sin(ang)
    m2[0, :] = np.where(np.arange(L) % 2 == 0, 1.0, -1.0)
    wgt = np.full((L, 1), 2.0)
    wgt[0, 0] = 1.0
    nch = L // fc
    fwd = np.concatenate([m1.reshape(nch, fc, L), m2.reshape(nch, fc, L)], axis=1)
    inv = np.concatenate([(m1 * wgt / n).reshape(nch, fc, L), (m2 * wgt / n).reshape(nch, fc, L)], axis=1)
    inv = np.transpose(inv, (0, 2, 1))
    return fwd.astype(np.float32), inv.astype(np.float32)


@functools.lru_cache(maxsize=None)
def _filter_consts(L):
    t = np.linspace(0.0, 1.0, L)[:, None]
    bands = (C_EMB - 1) // 2
    ang = (2.0 * math.pi / L) * np.arange(L)[:, None] * np.linspace(1e-4, bands - 1, bands)[None, :]
    z = np.concatenate([t, np.cos(ang), -np.sin(ang)], axis=-1)
    zpad = np.zeros((L, 128))
    zpad[:, :C_EMB] = z
    deltas = np.abs(np.linspace(HYENA_MIN_DECAY, HYENA_MAX_DECAY, C_DIM))
    window = np.exp(-t * deltas[None, :])
    return zpad.astype(np.float32), window.astype(np.float32)


def _filter_body(z_ref, w1_ref, b1_ref, w2_ref, b2_ref, w3_ref, b3_ref, fr_ref, w4_ref, win_ref, fm_ref,
                 hr_ref, g_ref, hq_ref, hs_scr, hd_scr):
    c = pl.program_id(0)
    fc = hr_ref.shape[0]

    @pl.when(c == 0)
    def _():
        fr = fr_ref[...]
        hh = jnp.sin(fr * (_dot_hi(z_ref[...], w1_ref[...]) + b1_ref[...]))
        hh = jnp.sin(fr * (_dot_hi(hh, w2_ref[...]) + b2_ref[...]))
        hh = jnp.sin(fr * (_dot_hi(hh, w3_ref[...]) + b3_ref[...]))
        hh = _dot_hi(hh, w4_ref[...])
        hf = hh[:, :C_DIM] * win_ref[...]
        hb = hh[:, C_DIM:] * win_ref[...]
        hs_scr[...] = hf + hb
        hd_scr[...] = hf - hb

    fm = fm_ref[0]
    hr = _dot_hi(fm[:fc], hs_scr[...])
    first = (lax.broadcasted_iota(jnp.int32, (fc, C_DIM), 0) == 0) & (c == 0)
    hr_ref[...] = hr
    g_ref[...] = jnp.where(first, 0.0, _dot_hi(fm[fc:], hd_scr[...]))
    hs = hs_scr[...]
    sign = jnp.where((lax.broadcasted_iota(jnp.int32, hs.shape, 0) & 1) == 0, 1.0, -1.0)
    hq_ref[...] = jnp.where(first, jnp.sum(hs * sign, axis=0, keepdims=True), hr)


def _hyena_filter(L, filt):
    w1, b1, w2, b2, w3, b3, freq, w4 = filt
    zpad, window = _filter_consts(L)
    fwd, _ = _dft_mats(L)
    nch, fc2, _ = fwd.shape
    fc = fc2 // 2
    w1p = jnp.pad(w1, ((0, 128 - C_EMB), (0, 0)))
    full = lambda shape: pl.BlockSpec(shape, lambda c: tuple(0 for _ in shape))
    out_spec = pl.BlockSpec((fc, C_DIM), lambda c: (c, 0))
    out_sd = jax.ShapeDtypeStruct((L, C_DIM), F32)
    return pl.pallas_call(
        _filter_body,
        grid=(nch,),
        in_specs=[full((L, 128)), full((128, C_FFN)), full((1, C_FFN)), full((C_FFN, C_FFN)), full((1, C_FFN)),
                  full((C_FFN, C_FFN)), full((1, C_FFN)), full((1, C_FFN)), full((C_FFN, 2 * C_DIM)),
                  full((L, C_DIM)), pl.BlockSpec((1, fc2, L), lambda c: (c, 0, 0))],
        out_specs=[out_spec, out_spec, out_spec],
        out_shape=[out_sd, out_sd, out_sd],
        scratch_shapes=[pltpu.VMEM((L, C_DIM), F32), pltpu.VMEM((L, C_DIM), F32)],
        compiler_params=_cparams("arbitrary"),
        name="hyena_filter",
    )(jnp.asarray(zpad), w1p, b1.reshape(1, C_FFN), w2, b2.reshape(1, C_FFN), w3, b3.reshape(1, C_FFN),
      freq.reshape(1, C_FFN), w4, jnp.asarray(window), jnp.asarray(fwd))


def _hyena_body(u_ref, cw_ref, cb_ref, d_ref, fm_ref, fi_ref, hr_ref, g_ref, hq_ref, y_ref,
                x0_scr, z_scr, acc_scr):
    c = pl.program_id(1)
    L = y_ref.shape[0]
    fc = hr_ref.shape[0]

    @pl.when(c == 0)
    def _():
        row = lax.broadcasted_iota(jnp.int32, (L, C_DIM), 0)

        def short_conv(sec):
            cols = slice(sec * C_DIM, (sec + 1) * C_DIM)
            u = u_ref[:, cols]
            prev = jnp.where(row == 0, 0.0, pltpu.roll(u, 1, axis=0))
            nxt = jnp.where(row == L - 1, 0.0, pltpu.roll(u, L - 1, axis=0))
            return (prev * cw_ref[0:1, cols] + u * cw_ref[1:2, cols] + nxt * cw_ref[2:3, cols]
                    + cb_ref[:, cols])

        x0_scr[...] = short_conv(0)
        z_scr[...] = short_conv(1) * short_conv(2)
        acc_scr[...] = jnp.zeros((L, C_DIM), F32)

    ab = _dot_hi(fm_ref[0], z_scr[...])
    a, b = ab[:fc], ab[fc:]
    hr, g, hq = hr_ref[...], g_ref[...], hq_ref[...]
    pq = jnp.concatenate([a * hr - b * g, a * g + b * hq], axis=0)
    acc_scr[...] += _dot_hi(fi_ref[0], pq)

    @pl.when(c == pl.num_programs(1) - 1)
    def _():
        y_ref[...] = x0_scr[...] * (acc_scr[...] + z_scr[...] * d_ref[...])


def _hyena(u, row_blk0, n_seq, L, conv_w, conv_b, d_skip, spec):
    hr, g, hq = spec
    fwd, inv = _dft_mats(L)
    nch, fc2, _ = fwd.shape
    fc = fc2 // 2
    u_w = 3 * C_DIM
    return pl.pallas_call(
        _hyena_body,
        grid=(n_seq, nch),
        in_specs=[pl.BlockSpec((L, u_w), lambda b, c: (row_blk0 + b, 0)),
                  pl.BlockSpec((3, u_w), lambda b, c: (0, 0)),
                  pl.BlockSpec((1, u_w), lambda b, c: (0, 0)),
                  pl.BlockSpec((1, C_DIM), lambda b, c: (0, 0)),
                  pl.BlockSpec((1, fc2, L), lambda b, c: (c, 0, 0)),
                  pl.BlockSpec((1, L, fc2), lambda b, c: (c, 0, 0)),
                  pl.BlockSpec((fc, C_DIM), lambda b, c: (c, 0)),
                  pl.BlockSpec((fc, C_DIM), lambda b, c: (c, 0)),
                  pl.BlockSpec((fc, C_DIM), lambda b, c: (c, 0))],
        out_specs=pl.BlockSpec((L, C_DIM), lambda b, c: (b, 0)),
        out_shape=jax.ShapeDtypeStruct((n_seq * L, C_DIM), F32),
        scratch_shapes=[pltpu.VMEM((L, C_DIM), F32)] * 3,
        compiler_params=_cparams("parallel", "arbitrary"),
        name="hyena",
    )(u, conv_w, conv_b.reshape(1, u_w), d_skip.reshape(1, C_DIM), jnp.asarray(fwd), jnp.asarray(inv), hr, g, hq)


def _hgrn_body(q_ref, ff_ref, fb_ref, i_ref, g_ref, lbf_ref, lbb_ref, nd_ref, s0f_ref, s0b_ref,
               o_ref, sf_ref, sb_ref, *, layer):
    L = o_ref.shape[0]
    C = GLA_CHUNK
    S = min(L, GLA_SPAN)
    nc = S // C
    n_span = L // S
    mid = C // 2
    q = _silu(q_ref[...])
    v = i_ref[...]

    def lower_bound(ref):
        gm = ref[...]
        e = jnp.exp(gm - jnp.max(gm, axis=0, keepdims=True))
        p = e / jnp.sum(e, axis=0, keepdims=True)
        return jnp.sum(p[0:layer + 1], axis=0, keepdims=True) - p[0:1]

    def gates(fx, lb):
        f = lb + (1.0 - lb) * jax.nn.sigmoid(fx)
        return 1.0 - f, jnp.log(f)

    kf, lgf = gates(ff_ref[...], lower_bound(lbf_ref))
    kb, lgb = gates(fb_ref[...], lower_bound(lbb_ref))

    chunk_shift = C.bit_length() - 1
    block_shift = D_KDIM.bit_length() - 1
    ti = lax.broadcasted_iota(jnp.int32, (S, S), 0)
    si = lax.broadcasted_iota(jnp.int32, (S, S), 1)
    same_chunk = (ti >> chunk_shift) == (si >> chunk_shift)
    causal = same_chunk & (si <= ti)
    anti = same_chunk & (si >= ti)
    row_chunk = lax.broadcasted_iota(jnp.int32, (S, nc * D_KDIM), 0) >> chunk_shift
    col_chunk = lax.broadcasted_iota(jnp.int32, (S, nc * D_KDIM), 1) >> block_shift
    own_block = row_chunk == col_chunk

    def spread(x):
        return jnp.where(own_block, jnp.concatenate([x] * nc, axis=1), 0.0)

    def chunk_cumsum(mask, lg):
        tri = mask.astype(BF16)
        hi = lg.astype(BF16)
        r1 = lg - hi.astype(F32)
        mid_t = r1.astype(BF16)
        lo = (r1 - mid_t.astype(F32)).astype(BF16)
        dot = lambda t: jnp.dot(tri, t, preferred_element_type=F32)
        return dot(hi) + dot(mid_t) + dot(lo)

    def per_chunk_rows(b, pos):
        return jnp.concatenate([jnp.broadcast_to(b[n * C + pos:n * C + pos + 1], (C, D_KDIM)) for n in range(nc)],
                               axis=0)

    def direction(k, lg, st, forward):
        outs = [None] * n_span
        mask = causal if forward else anti
        last = C - 1 if forward else 0
        for u in (range(n_span) if forward else reversed(range(n_span))):
            rows = slice(u * S, (u + 1) * S)
            qs, ks, vs = q[rows], k[rows], v[rows]
            b = chunk_cumsum(mask, lg[rows])
            btot = per_chunk_rows(b, last)
            ref = per_chunk_rows(b, mid)
            sc = jnp.where(mask, _dot_nt(qs * jnp.exp(b - ref), ks * jnp.exp(ref - b)), 0.0)
            kv_t = _dot_tn(spread(vs), ks * jnp.exp(btot - b))
            states = [None] * nc
            for n in (range(nc) if forward else reversed(range(nc))):
                states[n] = st
                decay = jnp.exp(b[n * C + last:n * C + last + 1])
                st = st * decay + kv_t[n * D_VDIM:(n + 1) * D_VDIM]
            inter = _dot_nt(spread(qs * jnp.exp(b)), jnp.concatenate(states, axis=1))
            outs[u] = _dot(sc, vs) + inter
        return outs, st

    o_f, st_f = direction(kf, lgf, jnp.transpose(s0f_ref[0, 0]), True)
    o_b, st_b = direction(kb, lgb, jnp.transpose(s0b_ref[0, 0]), False)
    sf_ref[0, 0] = jnp.transpose(st_f)
    sb_ref[0, 0] = jnp.transpose(st_b)
    o = jnp.concatenate([f + b for f, b in zip(o_f, o_b)], axis=0) if n_span > 1 else o_f[0] + o_b[0]
    o_ref[...] = _rms(o, nd_ref[...]) * _silu(g_ref[...])


def _hgrn(u, row_blk0, n_seq, L, lb_fwd, lb_bwd, norm_d, s0f, s0b, layer):
    col0 = 3 * C_DIM // D_KDIM
    col = lambda j: pl.BlockSpec((L, D_KDIM), lambda b, h: (row_blk0 + b, col0 + j * D_HEADS + h))
    lbs = pl.BlockSpec((DEPTH, D_KDIM), lambda b, h: (0, h))
    st = pl.BlockSpec((1, 1, D_KDIM, D_VDIM), lambda b, h: (b, h, 0, 0))
    st_sd = jax.ShapeDtypeStruct((n_seq, D_HEADS, D_KDIM, D_VDIM), F32)
    return pl.pallas_call(
        functools.partial(_hgrn_body, layer=layer),
        grid=(n_seq, D_HEADS),
        in_specs=[col(0), col(1), col(2), col(3), col(4), lbs, lbs,
                  pl.BlockSpec((1, D_VDIM), lambda b, h: (0, 0)), st, st],
        out_specs=[pl.BlockSpec((L, D_VDIM), lambda b, h: (b, h)), st, st],
        out_shape=[jax.ShapeDtypeStruct((n_seq * L, D_HEADS * D_VDIM), F32), st_sd, st_sd],
        compiler_params=_cparams("parallel", "parallel"),
        name="hgrn",
    )(u, u, u, u, u, lb_fwd, lb_bwd, norm_d.reshape(1, D_VDIM), s0f, s0b)


def _pack_bf16_pairs(h):
    n = h.shape[1] // 2
    hi = lax.bitcast_convert_type(h[:, :n].astype(BF16).astype(F32), jnp.int32)
    lo = lax.bitcast_convert_type(h[:, n:].astype(BF16).astype(F32), jnp.int32)
    return hi | lax.shift_right_logical(lo, 16)


def _unpack_bf16_pairs(p):
    hi = lax.bitcast_convert_type(p & jnp.int32(-65536), F32).astype(BF16)
    lo = lax.bitcast_convert_type(lax.shift_left(p, 16), F32).astype(BF16)
    return hi, lo


def _outproj_body(*refs, n_x):
    a_refs, b_refs, x_refs = refs[0:2], refs[2:4], refs[4:4 + n_x]
    mod_ref, gf_ref, w_ref, wrh_ref, wrl_ref, rb_ref, x1_ref, h2_ref, chosen_ref, gk_ref, ik_ref = refs[4 + n_x:]
    m = mod_ref[0]
    half = a_refs[0].shape[1]
    out = _dot(_token_tile(a_refs), w_ref[0:half, :]) + _dot(_token_tile(b_refs), w_ref[half:, :])
    x1 = _token_tile(x_refs) + m[:, 2 * D_MODEL:3 * D_MODEL] * out
    x1_ref[...] = x1
    h2 = _rms(x1, gf_ref[...]) * (1.0 + m[:, 4 * D_MODEL:5 * D_MODEL]) + m[:, 3 * D_MODEL:4 * D_MODEL]
    h2_ref[...] = _pack_bf16_pairs(h2)
    h_hi = h2.astype(BF16)
    h_lo = (h2 - h_hi.astype(F32)).astype(BF16)
    logits = _dot_nt(wrh_ref[...], h_hi) + _dot_nt(wrh_ref[...], h_lo) + _dot_nt(wrl_ref[...], h_hi)
    scores = jax.nn.sigmoid(logits)
    work = scores + rb_ref[...]
    expert = lax.broadcasted_iota(jnp.int32, work.shape, 0).astype(F32)
    slot = lax.broadcasted_iota(jnp.int32, (TOP_K, work.shape[1]), 0)
    chosen = jnp.zeros(work.shape, F32)
    gk = jnp.zeros((TOP_K, work.shape[1]), F32)
    ik = jnp.zeros((TOP_K, work.shape[1]), F32)
    for k in range(TOP_K):
        best = jnp.max(work, axis=0, keepdims=True)
        first = jnp.min(jnp.where(work == best, expert, float(N_EXPERTS)), axis=0, keepdims=True)
        hit = expert == first
        chosen = jnp.where(hit, 1.0, chosen)
        gk = jnp.where(slot == k, jnp.sum(jnp.where(hit, scores, 0.0), axis=0, keepdims=True), gk)
        ik = jnp.where(slot == k, first, ik)
        work = jnp.where(hit, -jnp.inf, work)
    chosen_ref[...] = chosen
    gk_ref[...] = gk / jnp.sum(gk, axis=0, keepdims=True) * ROUTE_SCALE
    ik_ref[...] = ik


def _outproj(a, b, x, mod_l, gain_ffn, w_out_bf16, w_router, router_bias):
    half = a[0].shape[1]
    a_specs, a_args = _token_specs(a, half)
    b_specs, b_args = _token_specs(b, half)
    x_specs, x_args = _token_specs(x, D_MODEL)
    wr_t = w_router.T
    wr_hi = wr_t.astype(BF16)
    wr_lo = (wr_t - wr_hi.astype(F32)).astype(BF16)
    return pl.pallas_call(
        functools.partial(_outproj_body, n_x=len(x_args)),
        grid=(T_ALL // TM,),
        in_specs=a_specs + b_specs + x_specs + [
                  pl.BlockSpec((1, 1, N_MOD * D_MODEL), lambda i: (_mod_row(i), 0, 0)),
                  pl.BlockSpec((1, D_MODEL), lambda i: (0, 0)),
                  pl.BlockSpec((2 * half, D_MODEL), lambda i: (0, 0)),
                  pl.BlockSpec((N_EXPERTS, D_MODEL), lambda i: (0, 0)),
                  pl.BlockSpec((N_EXPERTS, D_MODEL), lambda i: (0, 0)),
                  pl.BlockSpec((N_EXPERTS, 1), lambda i: (0, 0))],
        out_specs=[pl.BlockSpec((TM, D_MODEL), lambda i: (i, 0)),
                   pl.BlockSpec((TM, D_MODEL // 2), lambda i: (i, 0)),
                   pl.BlockSpec((N_EXPERTS, TM), lambda i: (0, i)),
                   pl.BlockSpec((TOP_K, TM), lambda i: (0, i)),
                   pl.BlockSpec((TOP_K, TM), lambda i: (0, i))],
        out_shape=[jax.ShapeDtypeStruct((T_ALL, D_MODEL), F32),
                   jax.ShapeDtypeStruct((T_ALL, D_MODEL // 2), jnp.int32),
                   jax.ShapeDtypeStruct((N_EXPERTS, T_ALL), F32),
                   jax.ShapeDtypeStruct((TOP_K, T_ALL), F32),
                   jax.ShapeDtypeStruct((TOP_K, T_ALL), F32)],
        compiler_params=_cparams("parallel"),
        name="outproj_router",
    )(*a_args, *b_args, *x_args, mod_l, gain_ffn.reshape(1, D_MODEL), w_out_bf16, wr_hi, wr_lo,
      router_bias.reshape(N_EXPERTS, 1))


def _route_body(chosen_ref, ik_ref, dest_ref, first_ref, count_ref, pos_scr):
    n_tiles = T_ALL // TM
    r = lax.broadcasted_iota(jnp.int32, (TM, TM), 0)
    c = lax.broadcasted_iota(jnp.int32, (TM, TM), 1)
    before = (r < c).astype(BF16)

    counts = jnp.zeros((N_EXPERTS, 1), F32)
    for i in range(n_tiles):
        cols = slice(i * TM, (i + 1) * TM)
        m = chosen_ref[:, cols]
        pos_scr[:, cols] = jnp.dot(m.astype(BF16), before, preferred_element_type=F32) + counts
        counts = counts + jnp.sum(m, axis=1, keepdims=True)
    padded = jnp.ceil(counts * (1.0 / MOE_BLK)) * MOE_BLK
    ei = lax.broadcasted_iota(jnp.int32, (N_EXPERTS, N_EXPERTS), 0)
    ej = lax.broadcasted_iota(jnp.int32, (N_EXPERTS, N_EXPERTS), 1)
    end = _dot_hi((ej <= ei).astype(F32), jnp.broadcast_to(padded, (N_EXPERTS, LANES)))[:, 0:1]
    start = end - padded

    expert = lax.broadcasted_iota(jnp.int32, (N_EXPERTS, TM), 0).astype(F32)
    slot = lax.broadcasted_iota(jnp.int32, (TOP_K, TM), 0)
    for i in range(n_tiles):
        cols = slice(i * TM, (i + 1) * TM)
        row_of = pos_scr[:, cols] + start
        ik = ik_ref[:, cols]
        acc = jnp.zeros((TOP_K, TM), F32)
        for k in range(TOP_K):
            pick = jnp.sum(jnp.where(expert == ik[k:k + 1, :], row_of, 0.0), axis=0, keepdims=True)
            acc = jnp.where(slot == k, pick, acc)
        dest_ref[:, cols] = acc.astype(jnp.int32)
    first_ref[...] = jnp.broadcast_to(start * (1.0 / MOE_BLK), (N_EXPERTS, LANES)).astype(jnp.int32)
    count_ref[...] = jnp.broadcast_to(padded * (1.0 / MOE_BLK), (N_EXPERTS, LANES)).astype(jnp.int32)


def _route(chosen, ik):
    full = lambda shape: pl.BlockSpec(shape, lambda i: (0, 0))
    return pl.pallas_call(
        _route_body,
        grid=(1,),
        in_specs=[full((N_EXPERTS, T_ALL)), full((TOP_K, T_ALL))],
        out_specs=[full((TOP_K, T_ALL)), full((N_EXPERTS, LANES)), full((N_EXPERTS, LANES))],
        out_shape=[jax.ShapeDtypeStruct((TOP_K, T_ALL), jnp.int32),
                   jax.ShapeDtypeStruct((N_EXPERTS, LANES), jnp.int32),
                   jax.ShapeDtypeStruct((N_EXPERTS, LANES), jnp.int32)],
        scratch_shapes=[pltpu.VMEM((N_EXPERTS, T_ALL), F32)],
        compiler_params=_cparams("arbitrary"),
        name="moe_route",
    )(chosen, ik)


def _sc_worker_id():
    return lax.axis_index("s") * SC_CORES + lax.axis_index("c")


def _sc_dispatch(h2p, dest):
    n_chunks = T_ALL // DISP_CHUNK
    width = h2p.shape[1]
    mesh = plsc.VectorSubcoreMesh(core_axis_name="c", subcore_axis_name="s")

    @functools.partial(
        pl.kernel, mesh=mesh,
        out_type=jax.ShapeDtypeStruct((MOE_ROWS, width), jnp.int32),
        scratch_types=[pltpu.VMEM((TOP_K, DISP_CHUNK), jnp.int32), pltpu.VMEM((DISP_CHUNK, width), jnp.int32)],
    )
    def run(x_hbm, dest_hbm, xs_hbm, idx_v, rows_v):
        wid = _sc_worker_id()
        for rep in range(-(-n_chunks // SC_WORKERS)):
            chunk = wid + rep * SC_WORKERS

            @pl.when(chunk < n_chunks)
            def _():
                tokens = pl.ds(pl.multiple_of(chunk * DISP_CHUNK, DISP_CHUNK), DISP_CHUNK)
                pltpu.sync_copy(dest_hbm.at[:, tokens], idx_v)
                pltpu.sync_copy(x_hbm.at[tokens], rows_v)
                for k in range(TOP_K):
                    pltpu.sync_copy(rows_v, xs_hbm.at[idx_v.at[k]])

    return run(h2p, dest)


def _sc_collect(y, dest_flat):
    per_worker = T_ALL // SC_WORKERS
    n_chunks = per_worker // COLLECT_CHUNK
    n_steps = TOP_K * n_chunks
    width = y.shape[1]
    mesh = plsc.VectorSubcoreMesh(core_axis_name="c", subcore_axis_name="s")

    @functools.partial(
        pl.kernel, mesh=mesh,
        out_type=jax.ShapeDtypeStruct((TOP_K * T_ALL, width), y.dtype),
        scratch_types=[pltpu.VMEM((TOP_K * per_worker,), jnp.int32),
                       pltpu.VMEM((COLLECT_CHUNK, width), y.dtype), pltpu.VMEM((COLLECT_CHUNK, width), y.dtype),
                       pltpu.SemaphoreType.DMA, pltpu.SemaphoreType.DMA],
    )
    def run(y_hbm, dest_hbm, yg_hbm, idx_v, rows0, rows1, sem0, sem1):
        wid = _sc_worker_id()
        bufs = ((rows0, sem0), (rows1, sem1))
        for k in range(TOP_K):
            pltpu.sync_copy(dest_hbm.at[pl.ds(k * T_ALL + wid * per_worker, per_worker)],
                            idx_v.at[pl.ds(k * per_worker, per_worker)])

        def gather(step, buf):
            rows, sem = buf
            idx = idx_v.at[pl.ds(pl.multiple_of(step * COLLECT_CHUNK, 8), COLLECT_CHUNK)]
            return pltpu.make_async_copy(y_hbm.at[idx], rows, sem)

        def out_rows(step):
            off = (step // n_chunks) * T_ALL + wid * per_worker + (step % n_chunks) * COLLECT_CHUNK
            return yg_hbm.at[pl.ds(pl.multiple_of(off, 8), COLLECT_CHUNK)]

        gather(0, bufs[0]).start()

        @pl.loop(0, n_steps, step=2)
        def _(base):
            for j in range(2):
                step = base + j

                @pl.when(step + 1 < n_steps)
                def _():
                    gather(step + 1, bufs[1 - j]).start()

                gather(step, bufs[j]).wait()
                pltpu.sync_copy(bufs[j][0], out_rows(step))

    return run(y, dest_flat)


def _expert_body(first_ref, count_ref, xs_hbm, wg_ref, wu_ref, wd_ref, y_hbm,
                 wg_bf, wu_bf, wd_bf, x_buf, y_buf, in_sem, out_sem):
    e = pl.program_id(0)
    first = first_ref[e]
    count = count_ref[e]
    n_used = first_ref[N_EXPERTS - 1] + count_ref[N_EXPERTS - 1]
    half = D_MODEL // 2
    wg_bf[...] = wg_ref[0, 0].astype(BF16)
    wu_bf[...] = wu_ref[0, 0].astype(BF16)
    wd_bf[...] = wd_ref[0, 0].astype(BF16)

    def part_rows(g, part, n_parts):
        size = MOE_BLK // n_parts
        return pl.ds(pl.multiple_of(g * MOE_BLK + part * size, size), size), pl.ds(part * size, size)

    def in_copies(g):
        slot = g & (EXPERT_SLOTS - 1)
        out = []
        for part in range(EXPERT_IN_PARTS):
            src, dst = part_rows(g, part, EXPERT_IN_PARTS)
            out.append(pltpu.make_async_copy(xs_hbm.at[src], x_buf.at[slot, dst], in_sem.at[slot]))
        return out

    def out_copies(g):
        slot = g & (EXPERT_SLOTS - 1)
        out = []
        for part in range(EXPERT_OUT_PARTS):
            dst, src = part_rows(g, part, EXPERT_OUT_PARTS)
            out.append(pltpu.make_async_copy(y_buf.at[slot, src], y_hbm.at[dst], out_sem.at[slot]))
        return out

    @pl.when((first == 0) & (count > 0))
    def _():
        for ahead in range(EXPERT_SLOTS - 1):
            @pl.when(ahead < n_used)
            def _():
                for cp in in_copies(ahead):
                    cp.start()

    def block(b, carry):
        g = first + b
        slot = g & (EXPERT_SLOTS - 1)
        for cp in in_copies(g):
            cp.wait()

        @pl.when(g + EXPERT_SLOTS - 1 < n_used)
        def _():
            for cp in in_copies(g + EXPERT_SLOTS - 1):
                cp.start()

        @pl.when(g >= EXPERT_SLOTS)
        def _():
            for cp in out_copies(g - EXPERT_SLOTS):
                cp.wait()

        hi, lo = _unpack_bf16_pairs(x_buf[slot])

        def proj(w_bf):
            return (jnp.dot(hi, w_bf[0:half, :], preferred_element_type=F32)
                    + jnp.dot(lo, w_bf[half:, :], preferred_element_type=F32))

        hid = _silu(proj(wg_bf)) * proj(wu_bf)
        y_buf[slot] = _pack_bf16_pairs(jnp.dot(hid.astype(BF16), wd_bf[...], preferred_element_type=F32))
        for cp in out_copies(g):
            cp.start()
        return carry

    lax.fori_loop(0, count, block, 0)

    @pl.when(e == N_EXPERTS - 1)
    def _():
        for back in range(EXPERT_SLOTS, 0, -1):
            @pl.when(n_used >= back)
            def _():
                for cp in out_copies(n_used - back):
                    cp.wait()


EXPERT_SLOTS = 4
EXPERT_IN_PARTS = 2
EXPERT_OUT_PARTS = 4


def _experts(first_blk, n_blk, xs, layer, w_gate, w_up, w_down):
    w_in = pl.BlockSpec((1, 1, D_MODEL, D_EXPERT), lambda e, first, count: (layer, e, 0, 0))
    grid_spec = pltpu.PrefetchScalarGridSpec(
        num_scalar_prefetch=2,
        grid=(N_EXPERTS,),
        in_specs=[pl.BlockSpec(memory_space=pl.ANY), w_in, w_in,
                  pl.BlockSpec((1, 1, D_EXPERT, D_MODEL), lambda e, first, count: (layer, e, 0, 0))],
        out_specs=pl.BlockSpec(memory_space=pl.ANY),
        scratch_shapes=[pltpu.VMEM((D_MODEL, D_EXPERT), BF16), pltpu.VMEM((D_MODEL, D_EXPERT), BF16),
                        pltpu.VMEM((D_EXPERT, D_MODEL), BF16),
                        pltpu.VMEM((EXPERT_SLOTS, MOE_BLK, D_MODEL // 2), jnp.int32),
                        pltpu.VMEM((EXPERT_SLOTS, MOE_BLK, D_MODEL // 2), jnp.int32),
                        pltpu.SemaphoreType.DMA((EXPERT_SLOTS,)), pltpu.SemaphoreType.DMA((EXPERT_SLOTS,))],
    )
    return pl.pallas_call(
        _expert_body,
        grid_spec=grid_spec,
        out_shape=jax.ShapeDtypeStruct((MOE_ROWS, D_MODEL // 2), jnp.int32),
        compiler_params=_cparams("arbitrary"),
        name="moe_experts",
    )(first_blk, n_blk, xs, w_gate, w_up, w_down)


def _combine_body(x1_ref, h2_ref, yg_ref, gk_ref, mod_ref, sg_ref, su_ref, sd_ref, fn_ref, *o_refs, final):
    hi, lo = _unpack_bf16_pairs(h2_ref[...])
    half = D_MODEL // 2

    def proj(w_ref):
        return _dot(hi, w_ref[0:half, :]) + _dot(lo, w_ref[half:, :])

    shared = _dot(_silu(proj(sg_ref)) * proj(su_ref), sd_ref[...])
    acc_hi, acc_lo = shared[:, :half], shared[:, half:]
    gk = gk_ref[...]
    for k in range(TOP_K):
        y_hi, y_lo = _unpack_bf16_pairs(yg_ref[k])
        acc_hi = acc_hi + gk[:, k:k + 1] * y_hi.astype(F32)
        acc_lo = acc_lo + gk[:, k:k + 1] * y_lo.astype(F32)
    acc = jnp.concatenate([acc_hi, acc_lo], axis=1)
    m = mod_ref[0]
    y = x1_ref[...] + m[:, 5 * D_MODEL:6 * D_MODEL] * acc
    if not final:
        o_refs[0][...] = y
        return
    y = _rms(y, fn_ref[...])
    is_ctx = pl.program_id(0) < N_CTX_TILES

    @pl.when(is_ctx)
    def _():
        o_refs[0][...] = y

    @pl.when(jnp.logical_not(is_ctx))
    def _():
        o_refs[1][...] = y


def _combine(x1, h2p, yg, gk, mod_l, ws_gate, ws_up, ws_down, final_norm, final):
    tok = lambda shape: pl.BlockSpec(shape, lambda i: (i, 0))
    full = lambda shape: pl.BlockSpec(shape, lambda i: (0, 0))
    if final:
        out_specs, _ = _token_specs((None, None), D_MODEL)
        out_shape = [jax.ShapeDtypeStruct((T_CTX, D_MODEL), F32), jax.ShapeDtypeStruct((T_LAT, D_MODEL), F32)]
    else:
        out_specs = tok((TM, D_MODEL))
        out_shape = jax.ShapeDtypeStruct((T_ALL, D_MODEL), F32)
    return pl.pallas_call(
        functools.partial(_combine_body, final=final),
        grid=(T_ALL // TM,),
        in_specs=[tok((TM, D_MODEL)), tok((TM, D_MODEL // 2)),
                  pl.BlockSpec((TOP_K, TM, D_MODEL // 2), lambda i: (0, i, 0)),
                  tok((TM, TOP_K)),
                  pl.BlockSpec((1, 1, N_MOD * D_MODEL), lambda i: (_mod_row(i), 0, 0)),
                  full((D_MODEL, D_EXPERT)), full((D_MODEL, D_EXPERT)), full((D_EXPERT, D_MODEL)),
                  full((1, D_MODEL))],
        out_specs=out_specs,
        out_shape=out_shape,
        compiler_params=_cparams("arbitrary"),
        name="moe_combine",
    )(x1, h2p, yg, gk, mod_l, ws_gate, ws_up, ws_down, final_norm.reshape(1, D_MODEL))


def _moe(x1, h2p, chosen, gk, ik, mod_l, layer, w_gate, w_up, w_down, ws_gate, ws_up, ws_down, final_norm, final):
    dest, first_blk, n_blk = _route(chosen, ik)
    xs = _sc_dispatch(h2p, dest)
    y = _experts(first_blk[:, 0], n_blk[:, 0], xs, layer, w_gate, w_up, w_down)
    yg = _sc_collect(y, dest.reshape(-1)).reshape(TOP_K, T_ALL, D_MODEL // 2)
    return _combine(x1, h2p, yg, gk.T, mod_l, ws_gate.astype(BF16), ws_up.astype(BF16), ws_down.astype(BF16),
                    final_norm, final)


def kernel(x_prompt, x_sample, cache_a_k, cache_a_v, cache_b_k, cache_b_v, state_d_fwd, state_d_bwd, c, c_ctx, w_ada, b_ada, norm_mix, norm_ffn, w_in_attn, w_out_attn, sink_a, rpb_b, w_in_rec, w_out_rec, conv_w, conv_b, filt_w1, filt_b1, filt_w2, filt_b2, filt_w3, filt_b3, filt_freq, filt_w4, d_skip, lb_fwd, lb_bwd, norm_d, w_router, router_bias, w_gate, w_up, w_down, ws_gate, ws_up, ws_down, final_norm):
    x = (x_prompt.reshape(T_CTX, D_MODEL), x_sample.reshape(T_LAT, D_MODEL))
    cvec = jnp.concatenate([c_ctx[None, :], c], axis=0)
    mod = _ada(cvec, w_ada, b_ada).reshape(DEPTH, CVEC_PAD, 1, N_MOD * D_MODEL)

    new_kv = None
    new_state = None
    for l in range(DEPTH):
        j = l // 2
        final = l == DEPTH - 1
        if l % 2 == 0:
            qkv = _inproj(x, mod[l], norm_mix[l], w_in_attn[j].astype(BF16))
            oa_ctx, ob_ctx, *new_kv = _ctx_attn(qkv, sink_a[j])
            new_kv = tuple(new_kv)
            q_rot, k_rot = _rope(qkv)
            cache = lambda t: t[:, j].reshape(DEC_BATCH, PAST_LEN, -1)
            oa_lat = _win_attn(qkv, q_rot, k_rot, cache(cache_a_k), cache(cache_a_v), sink_a[j])
            ob_lat = _na_attn(qkv, cache(cache_b_k), cache(cache_b_v), _na_rel_rows(rpb_b[j]))
            mix_a = (oa_ctx, oa_lat)
            mix_b = (ob_ctx, ob_lat)
            w_out = w_out_attn[j]
        else:
            u = _inproj(x, mod[l], norm_mix[l], w_in_rec[j].astype(BF16))
            filt = (filt_w1[j], filt_b1[j], filt_w2[j], filt_b2[j], filt_w3[j], filt_b3[j], filt_freq[j],
                    filt_w4[j])
            y_ctx = _hyena(u, 0, BATCH, SEQ, conv_w[j], conv_b[j], d_skip[j], _hyena_filter(SEQ, filt))
            y_lat = _hyena(u, T_CTX // DEC_SEQ, DEC_BATCH, DEC_SEQ, conv_w[j], conv_b[j], d_skip[j],
                           _hyena_filter(DEC_SEQ, filt))
            zeros = jnp.zeros((BATCH, D_HEADS, D_KDIM, D_VDIM), F32)
            o_ctx, s_f, s_b = _hgrn(u, 0, BATCH, SEQ, lb_fwd, lb_bwd, norm_d[j], zeros, zeros, l)
            o_lat, _, _ = _hgrn(u, T_CTX // DEC_SEQ, DEC_BATCH, DEC_SEQ, lb_fwd, lb_bwd, norm_d[j],
                                state_d_fwd[:, j], state_d_bwd[:, j], l)
            new_state = (s_f[:, None], s_b[:, None])
            mix_a = (y_ctx, y_lat)
            mix_b = (o_ctx, o_lat)
            w_out = w_out_rec[j]
        x1, h2p, chosen, gk, ik = _outproj(mix_a, mix_b, x, mod[l], norm_ffn[l], w_out.astype(BF16), w_router[l],
                                           router_bias[l])
        x = _moe(x1, h2p, chosen, gk, ik, mod[l], l, w_gate, w_up, w_down, ws_gate[l], ws_up[l],
                 ws_down[l], final_norm, final)

    y_prompt = x[0].reshape(BATCH, SEQ, D_MODEL)
    y_sample = x[1].reshape(DEC_BATCH, DEC_SEQ, D_MODEL)
    return (y_prompt, y_sample) + new_kv + new_state
```

```python
import functools
import math

import numpy as np
import jax
import jax.numpy as jnp
from jax import lax
from jax.experimental import pallas as pl
from jax.experimental.pallas import tpu as pltpu
from jax.experimental.pallas import tpu_sc as plsc

F32 = jnp.float32
BF16 = jnp.bfloat16
HI = lax.Precision.HIGHEST

D_MODEL = 1024
BATCH = 16
SEQ = 256
DEPTH = 2
DEC_BATCH = 2
DEC_SEQ = 1024
PAST_LEN = 512
GRID_W = 64
HEAD_DIM = 64
N_MOD = 6
RMS_EPS = 1e-6
A_HEADS = 8
A_KV_HEADS = 2
A_GROUP = A_HEADS // A_KV_HEADS
WINDOW = 128
ROPE_BASE = 10000.0
B_HEADS = 8
NA_ROWS = 8
NA_COLS = 16
C_DIM = 512
C_EMB = 33
C_FFN = 64
HYENA_MIN_DECAY = math.log(1e-2) / 1.5
HYENA_MAX_DECAY = math.log(1e-2) / 0.3
D_KDIM = 128
D_VDIM = 128
D_HEADS = 4
N_EXPERTS = 64
TOP_K = 8
D_EXPERT = 256
ROUTE_SCALE = 2.5
A_Q = A_HEADS * HEAD_DIM
A_KV = A_KV_HEADS * HEAD_DIM
B_W = B_HEADS * HEAD_DIM
ATTN_IN = A_Q + 2 * A_KV + 3 * B_W
REC_IN = 3 * C_DIM + 5 * D_HEADS * D_KDIM

T_CTX = BATCH * SEQ
T_LAT = DEC_BATCH * DEC_SEQ
T_ALL = T_CTX + T_LAT
N_CVEC = 1 + DEC_BATCH
CVEC_PAD = 8
TM = 256
MASK_NEG = -1e30
GLA_CHUNK = 64
GLA_SPAN = 256
DFT_CHUNK = 256
MOE_BLK = 256
MOE_NBLK = -(-(T_ALL * TOP_K + N_EXPERTS * (MOE_BLK - 1)) // MOE_BLK)
MOE_ROWS = MOE_NBLK * MOE_BLK
SC_CORES = 2
SC_SUBCORES = 16
SC_WORKERS = SC_CORES * SC_SUBCORES
DISP_CHUNK = 128
COLLECT_CHUNK = 64
VMEM_LIMIT = 56 * 1024 * 1024


def _cparams(*sem):
    return pltpu.CompilerParams(dimension_semantics=sem, vmem_limit_bytes=VMEM_LIMIT)


def _mod_row(i):
    return jnp.where(i < T_CTX // TM, 0, 1 + (i - T_CTX // TM) // (DEC_SEQ // TM))


def _dot(a, b):
    return jnp.dot(a.astype(BF16), b.astype(BF16), preferred_element_type=F32)


def _dot_nt(a, b):
    return lax.dot_general(a.astype(BF16), b.astype(BF16), (((1,), (1,)), ((), ())),
                           preferred_element_type=F32)


def _dot_tn(a, b):
    return lax.dot_general(a.astype(BF16), b.astype(BF16), (((0,), (0,)), ((), ())),
                           preferred_element_type=F32)


def _dot_hi(a, b):
    return jnp.dot(a, b, precision=HI, preferred_element_type=F32)


def _split_bf16(x):
    hi = x.astype(BF16)
    return hi, (x - hi.astype(F32)).astype(BF16)


def _dot_split(a, b):
    a_hi, a_lo = _split_bf16(a)
    b_hi, b_lo = _split_bf16(b)
    dot = lambda x, y: jnp.dot(x, y, preferred_element_type=F32)
    return dot(a_hi, b_hi) + dot(a_hi, b_lo) + dot(a_lo, b_hi)


def _silu(x):
    return x * jax.nn.sigmoid(x)


def _rms(x, g):
    return x * lax.rsqrt(jnp.mean(x * x, axis=-1, keepdims=True) + RMS_EPS) * g


ADA_TN = 1536
ADA_UNROLL = 4


def _ada_body(cb_ref, w_ref, b_ref, o_ref):
    tn = o_ref.shape[-1]
    n_slab = tn // LANES

    def step(k8, accs):
        r0 = pl.multiple_of(k8 * 8, 8)
        sk = [_silu(cb_ref[j, pl.ds(r0, 8), :]) for j in range(N_CVEC)]
        out = []
        for s in range(n_slab):
            wk = w_ref[0, pl.ds(r0, 8), s * LANES:(s + 1) * LANES]
            out.extend(accs[s * N_CVEC + j] + wk * sk[j] for j in range(N_CVEC))
        return tuple(out)

    accs = lax.fori_loop(0, D_MODEL // 8, step,
                         tuple(jnp.zeros((8, LANES), F32) for _ in range(n_slab * N_CVEC)), unroll=ADA_UNROLL)
    o_ref[0] = jnp.zeros((CVEC_PAD, tn), F32)
    for s in range(n_slab):
        for j in range(N_CVEC):
            o_ref[0, j:j + 1, s * LANES:(s + 1) * LANES] = (
                jnp.sum(accs[s * N_CVEC + j], axis=0, keepdims=True) + b_ref[0, :, s * LANES:(s + 1) * LANES])


def _ada(cvec, w_ada, b_ada):
    n_out = N_MOD * D_MODEL
    c_lanes = jnp.broadcast_to(cvec[:, :, None], (N_CVEC, D_MODEL, LANES))
    return pl.pallas_call(
        _ada_body,
        grid=(DEPTH, n_out // ADA_TN),
        in_specs=[pl.BlockSpec((N_CVEC, D_MODEL, LANES), lambda l, n: (0, 0, 0)),
                  pl.BlockSpec((1, D_MODEL, ADA_TN), lambda l, n: (l, 0, n)),
                  pl.BlockSpec((1, 1, ADA_TN), lambda l, n: (l, 0, n))],
        out_specs=pl.BlockSpec((1, CVEC_PAD, ADA_TN), lambda l, n: (l, 0, n)),
        out_shape=jax.ShapeDtypeStruct((DEPTH, CVEC_PAD, n_out), F32),
        compiler_params=_cparams("parallel", "parallel"),
        name="ada",
    )(c_lanes, w_ada, b_ada.reshape(DEPTH, 1, n_out))


N_CTX_TILES = T_CTX // TM


def _token_specs(x, width):
    if not isinstance(x, tuple):
        return [pl.BlockSpec((TM, width), lambda i: (i, 0))], (x,)
    return ([pl.BlockSpec((TM, width), lambda i: (jnp.minimum(i, N_CTX_TILES - 1), 0)),
             pl.BlockSpec((TM, width), lambda i: (jnp.maximum(i - N_CTX_TILES, 0), 0))], x)


def _token_tile(refs):
    if len(refs) == 1:
        return refs[0][...]
    return jnp.where(pl.program_id(0) < N_CTX_TILES, refs[0][...], refs[1][...])


def _inproj_body(*refs, n_x):
    x_refs, (mod_ref, g_ref, w_ref, o_ref) = refs[:n_x], refs[n_x:]
    m = mod_ref[0]
    h = _rms(_token_tile(x_refs), g_ref[...]) * (1.0 + m[:, D_MODEL:2 * D_MODEL]) + m[:, 0:D_MODEL]
    o_ref[...] = _dot(h, w_ref[...])


def _inproj(x, mod_l, gain, w_bf16):
    n = w_bf16.shape[1]
    x_specs, x_args = _token_specs(x, D_MODEL)
    return pl.pallas_call(
        functools.partial(_inproj_body, n_x=len(x_args)),
        grid=(T_ALL // TM,),
        in_specs=x_specs + [pl.BlockSpec((1, 1, N_MOD * D_MODEL), lambda i: (_mod_row(i), 0, 0)),
                            pl.BlockSpec((1, D_MODEL), lambda i: (0, 0)),
                            pl.BlockSpec((D_MODEL, n), lambda i: (0, 0))],
        out_specs=pl.BlockSpec((TM, n), lambda i: (i, 0)),
        out_shape=jax.ShapeDtypeStruct((T_ALL, n), F32),
        compiler_params=_cparams("parallel"),
        name="inproj",
    )(*x_args, mod_l, gain.reshape(1, D_MODEL), w_bf16)


def _head_cols(h):
    return slice(h * HEAD_DIM, (h + 1) * HEAD_DIM)


def _group_rows(ref, rows, first_col, sink_ref, hk):
    n = rows.stop - rows.start
    q = jnp.concatenate([ref[rows, first_col + g * HEAD_DIM:first_col + (g + 1) * HEAD_DIM]
                         for g in range(A_GROUP)], axis=0)
    sink = jnp.concatenate([jnp.broadcast_to(sink_ref[:, hk * A_GROUP + g:hk * A_GROUP + g + 1], (n, 1))
                            for g in range(A_GROUP)], axis=0)
    return q, sink


def _ctx_attn_body(qkv_ref, sink_ref, oa_ref, ob_ref, ak_ref, av_ref, bk_ref, bv_ref):
    scale = HEAD_DIM ** -0.5
    lane = lax.broadcasted_iota(jnp.int32, (SEQ, LANES), 1)
    in_half = [lane < HEAD_DIM, lane >= HEAD_DIM]

    def attend(q, k, v, sink):
        s = _dot_nt(q, k) * scale
        m = jnp.max(s, axis=-1, keepdims=True)
        if sink is not None:
            m = jnp.maximum(m, sink)
        p = jnp.exp(s - m)
        den = jnp.sum(p, axis=-1, keepdims=True)
        if sink is not None:
            den = den + jnp.exp(sink - m)
        return _dot(p, v) / den

    def tile(first_col, t):
        return qkv_ref[:, first_col + t * LANES:first_col + (t + 1) * LANES]

    base = A_Q + 2 * A_KV
    for hk in range(A_KV_HEADS):
        ak_ref[0, 0, :, hk, :] = qkv_ref[:, A_Q + hk * HEAD_DIM:A_Q + (hk + 1) * HEAD_DIM]
        av_ref[0, 0, :, hk, :] = qkv_ref[:, A_Q + A_KV + hk * HEAD_DIM:A_Q + A_KV + (hk + 1) * HEAD_DIM]
    for h in range(B_HEADS):
        bk_ref[0, 0, :, h, :] = qkv_ref[:, base + B_W + h * HEAD_DIM:base + B_W + (h + 1) * HEAD_DIM]
        bv_ref[0, 0, :, h, :] = qkv_ref[:, base + 2 * B_W + h * HEAD_DIM:base + 2 * B_W + (h + 1) * HEAD_DIM]

    k_t, v_t = tile(A_Q, 0), tile(A_Q + A_KV, 0)
    k_sw, v_sw = pltpu.roll(k_t, HEAD_DIM, axis=1), pltpu.roll(v_t, HEAD_DIM, axis=1)
    tiles_per_kv = A_GROUP // HEADS_PER_TILE
    for hk in range(A_KV_HEADS):
        q_tiles = [tile(0, hk * tiles_per_kv + j) for j in range(tiles_per_kv)]
        halves = []
        for p in range(HEADS_PER_TILE):
            q = jnp.concatenate([jnp.where(in_half[p], qt, 0.0) for qt in q_tiles], axis=0)
            heads = [(hk * tiles_per_kv + j) * HEADS_PER_TILE + p for j in range(tiles_per_kv)]
            sink = jnp.concatenate([jnp.broadcast_to(sink_ref[:, h:h + 1], (SEQ, 1)) for h in heads], axis=0)
            halves.append(attend(q, k_t if p == hk else k_sw, v_t if p == hk else v_sw, sink))
        first_half = lax.broadcasted_iota(jnp.int32, halves[0].shape, 1) < HEAD_DIM
        o = jnp.where(first_half, halves[0], halves[1])
        for j in range(tiles_per_kv):
            t = hk * tiles_per_kv + j
            oa_ref[:, t * LANES:(t + 1) * LANES] = o[j * SEQ:(j + 1) * SEQ]

    for t in range(B_HEADS // HEADS_PER_TILE):
        q_t, k_b, v_b = tile(base, t), tile(base + B_W, t), tile(base + 2 * B_W, t)
        halves = [attend(jnp.where(in_half[p], q_t, 0.0), k_b, v_b, None) for p in range(HEADS_PER_TILE)]
        ob_ref[:, t * LANES:(t + 1) * LANES] = jnp.where(in_half[0], halves[0], halves[1])


def _ctx_attn(qkv, sink):
    kv_spec = lambda heads: pl.BlockSpec((1, 1, SEQ, heads, HEAD_DIM), lambda b: (b, 0, 0, 0, 0))
    kv_sd = lambda heads: jax.ShapeDtypeStruct((BATCH, 1, SEQ, heads, HEAD_DIM), F32)
    return pl.pallas_call(
        _ctx_attn_body,
        grid=(BATCH,),
        in_specs=[pl.BlockSpec((SEQ, ATTN_IN), lambda b: (b, 0)),
                  pl.BlockSpec((1, A_HEADS), lambda b: (0, 0))],
        out_specs=[pl.BlockSpec((SEQ, A_Q), lambda b: (b, 0)), pl.BlockSpec((SEQ, B_W), lambda b: (b, 0)),
                   kv_spec(A_KV_HEADS), kv_spec(A_KV_HEADS), kv_spec(B_HEADS), kv_spec(B_HEADS)],
        out_shape=[jax.ShapeDtypeStruct((T_CTX, A_Q), F32), jax.ShapeDtypeStruct((T_CTX, B_W), F32),
                   kv_sd(A_KV_HEADS), kv_sd(A_KV_HEADS), kv_sd(B_HEADS), kv_sd(B_HEADS)],
        compiler_params=_cparams("parallel"),
        name="ctx_attn",
    )(qkv, sink.reshape(1, A_HEADS))


@functools.lru_cache(maxsize=None)
def _rope_tables(width):
    half = HEAD_DIM // 2
    t = np.arange(DEC_SEQ)
    inv = ROPE_BASE ** (-np.arange(0, half, 2, dtype=np.float64) / half)
    ang_r = (t // GRID_W)[:, None] * inv[None, :]
    ang_c = (t % GRID_W)[:, None] * inv[None, :]
    cos = np.concatenate([np.cos(ang_r)] * 2 + [np.cos(ang_c)] * 2, axis=-1)
    sin = np.concatenate([-np.sin(ang_r), np.sin(ang_r), -np.sin(ang_c), np.sin(ang_c)], axis=-1)
    reps = width // HEAD_DIM
    return (np.tile(cos, (1, reps)).astype(np.float32), np.tile(sin, (1, reps)).astype(np.float32))


def _rope_body(q_ref, k_ref, cq_ref, sq_ref, ck_ref, sk_ref, qo_ref, ko_ref):
    quarter = HEAD_DIM // 4

    def rot(x, cos, sin):
        w = x.shape[-1]
        lane = lax.broadcasted_iota(jnp.int32, x.shape, 1)
        fwd = pltpu.roll(x, w - quarter, axis=1)
        bwd = pltpu.roll(x, quarter, axis=1)
        partner = jnp.where((lane & (2 * quarter - 1)) < quarter, fwd, bwd)
        return x * cos + partner * sin

    qo_ref[...] = rot(q_ref[...], cq_ref[...], sq_ref[...])
    ko_ref[...] = rot(k_ref[...], ck_ref[...], sk_ref[...])


def _rope(qkv):
    cq, sq = _rope_tables(A_Q)
    ck, sk = _rope_tables(A_KV)
    tab = lambda w: pl.BlockSpec((DEC_SEQ, w), lambda b: (0, 0))
    row0 = T_CTX // DEC_SEQ
    return pl.pallas_call(
        _rope_body,
        grid=(DEC_BATCH,),
        in_specs=[pl.BlockSpec((DEC_SEQ, A_Q), lambda b: (row0 + b, 0)),
                  pl.BlockSpec((DEC_SEQ, A_KV), lambda b: (row0 + b, A_Q // A_KV)),
                  tab(A_Q), tab(A_Q), tab(A_KV), tab(A_KV)],
        out_specs=[pl.BlockSpec((DEC_SEQ, A_Q), lambda b: (b, 0)),
                   pl.BlockSpec((DEC_SEQ, A_KV), lambda b: (b, 0))],
        out_shape=[jax.ShapeDtypeStruct((T_LAT, A_Q), F32), jax.ShapeDtypeStruct((T_LAT, A_KV), F32)],
        compiler_params=_cparams("parallel"),
        name="rope",
    )(qkv, qkv, jnp.asarray(cq), jnp.asarray(sq), jnp.asarray(ck), jnp.asarray(sk))


WIN_QB = 256


def _pick_head(x, h, n_heads):
    out = x[:, _head_cols(0)]
    for i in range(1, n_heads):
        out = jnp.where(h == i, x[:, _head_cols(i)], out)
    return out


def _win_attn_body(qraw_ref, qrot_ref, krot_ref, v_ref, kc_ref, vc_ref, sink_ref, o_ref):
    scale = HEAD_DIM ** -0.5
    hk = pl.program_id(1)
    k = _pick_head(krot_ref[...], hk, A_KV_HEADS)
    v = _pick_head(v_ref[...], hk, A_KV_HEADS)
    kc = _pick_head(kc_ref[0], hk, A_KV_HEADS)
    vc = _pick_head(vc_ref[0], hk, A_KV_HEADS)
    head_lane = lax.broadcasted_iota(jnp.int32, (1, A_HEADS), 1)
    sinks = [jnp.sum(jnp.where(head_lane == hk * A_GROUP + g, sink_ref[...], 0.0), axis=-1, keepdims=True)
             for g in range(A_GROUP)]
    sink = jnp.concatenate([jnp.broadcast_to(s, (WIN_QB, 1)) for s in sinks], axis=0)
    for qb in range(DEC_SEQ // WIN_QB):
        q0 = qb * WIN_QB
        rows = slice(q0, q0 + WIN_QB)
        lo = max(0, q0 - WINDOW)
        hi = min(DEC_SEQ, q0 + WIN_QB + WINDOW)
        q_rot = jnp.concatenate([qrot_ref[rows, _head_cols(g)] for g in range(A_GROUP)], axis=0)
        q_raw = jnp.concatenate([qraw_ref[rows, _head_cols(g)] for g in range(A_GROUP)], axis=0)
        s_loc = _dot_nt(q_rot, k[lo:hi]) * scale
        qpos = q0 + (lax.broadcasted_iota(jnp.int32, s_loc.shape, 0) & (WIN_QB - 1))
        kpos = lo + lax.broadcasted_iota(jnp.int32, s_loc.shape, 1)
        s_loc = jnp.where(jnp.abs(kpos - qpos) <= WINDOW, s_loc, MASK_NEG)
        s_ctx = _dot_nt(q_raw, kc) * scale
        m = jnp.maximum(jnp.maximum(jnp.max(s_loc, axis=-1, keepdims=True),
                                    jnp.max(s_ctx, axis=-1, keepdims=True)), sink)
        p_loc = jnp.exp(s_loc - m)
        p_ctx = jnp.exp(s_ctx - m)
        den = (jnp.sum(p_loc, axis=-1, keepdims=True) + jnp.sum(p_ctx, axis=-1, keepdims=True)
               + jnp.exp(sink - m))
        o = (_dot(p_ctx, vc) + _dot(p_loc, v[lo:hi])) / den
        for g in range(A_GROUP):
            o_ref[rows, _head_cols(g)] = o[g * WIN_QB:(g + 1) * WIN_QB]


def _win_attn(qkv, q_rot, k_rot, kc, vc, sink):
    row0 = T_CTX // DEC_SEQ
    gw = A_GROUP * HEAD_DIM
    return pl.pallas_call(
        _win_attn_body,
        grid=(DEC_BATCH, A_KV_HEADS),
        in_specs=[pl.BlockSpec((DEC_SEQ, gw), lambda b, h: (row0 + b, h)),
                  pl.BlockSpec((DEC_SEQ, gw), lambda b, h: (b, h)),
                  pl.BlockSpec((DEC_SEQ, A_KV), lambda b, h: (b, 0)),
                  pl.BlockSpec((DEC_SEQ, A_KV), lambda b, h: (row0 + b, (A_Q + A_KV) // A_KV)),
                  pl.BlockSpec((1, PAST_LEN, A_KV), lambda b, h: (b, 0, 0)),
                  pl.BlockSpec((1, PAST_LEN, A_KV), lambda b, h: (b, 0, 0)),
                  pl.BlockSpec((1, A_HEADS), lambda b, h: (0, 0))],
        out_specs=pl.BlockSpec((DEC_SEQ, gw), lambda b, h: (b, h)),
        out_shape=jax.ShapeDtypeStruct((T_LAT, A_Q), F32),
        compiler_params=_cparams("parallel", "parallel"),
        name="win_attn",
    )(qkv, q_rot, k_rot, qkv, kc, vc, sink.reshape(1, A_HEADS))


GRID_ROWS = DEC_SEQ // GRID_W
NA_BAND = min(NA_ROWS, GRID_ROWS)


NA_REL_ROWS = 2 * NA_ROWS - 1
NA_REL_COLS = 2 * NA_COLS - 1
LANES = 128
HEADS_PER_TILE = LANES // HEAD_DIM


def _na_rel_rows(rpb):
    pad = jnp.zeros((B_HEADS, NA_REL_ROWS, GRID_W - NA_REL_COLS), F32)
    one = jnp.concatenate([rpb, pad], axis=-1)
    nxt = jnp.concatenate([one[:, 1:], jnp.zeros((B_HEADS, 1, GRID_W), F32)], axis=1)
    both = jnp.concatenate([one, nxt], axis=-1)
    return jnp.concatenate([both, jnp.zeros((B_HEADS, 16 - NA_REL_ROWS, LANES), F32)], axis=1)


NA_HEADS_PER_STEP = LANES // HEAD_DIM


def _na_row_groups():
    groups = []
    for r in range(GRID_ROWS):
        rs = min(max(r - NA_ROWS // 2, 0), GRID_ROWS - NA_BAND)
        if groups and groups[-1][2] == rs:
            groups[-1][1] += 1
        else:
            groups.append([r, 1, rs])
    return groups


def _na_attn_body(q_ref, k_ref, v_ref, kc_ref, vc_ref, rel_ref, o_ref):
    scale = HEAD_DIM ** -0.5
    cq = lax.broadcasted_iota(jnp.int32, (GRID_W, LANES), 0)
    kcol = lax.broadcasted_iota(jnp.int32, (GRID_W, LANES), 1) & (GRID_W - 1)
    cs = jnp.clip(cq - NA_COLS // 2, 0, GRID_W - NA_COLS)
    col_ok = (kcol >= cs) & (kcol < cs + NA_COLS)
    for hh in range(NA_HEADS_PER_STEP):
        cols = _head_cols(hh)
        kc = kc_ref[0, :, cols]
        vc = vc_ref[0, :, cols]
        tiles = {}

        def pair_tile(a):
            if a not in tiles:
                x = jnp.broadcast_to(rel_ref[hh, a:a + 1, :], (GRID_W, LANES))
                t = pltpu.roll(x, LANES - (NA_COLS - 1), axis=1, stride=1, stride_axis=0)
                tiles[a] = jnp.where(col_ok, t, MASK_NEG)
            return tiles[a]

        for r0, n_r, rs in _na_row_groups():
            bias = jnp.concatenate(
                [jnp.concatenate([pair_tile(rs - r + NA_ROWS - 1 + 2 * i) for i in range(NA_BAND // 2)], axis=1)
                 for r in range(r0, r0 + n_r)], axis=0)
            rows = slice(r0 * GRID_W, (r0 + n_r) * GRID_W)
            band = slice(rs * GRID_W, (rs + NA_BAND) * GRID_W)
            q = q_ref[rows, cols]
            s_loc = _dot_nt(q, k_ref[band, cols]) * scale + bias
            s_ctx = _dot_nt(q, kc) * scale
            m = jnp.maximum(jnp.max(s_loc, axis=-1, keepdims=True), jnp.max(s_ctx, axis=-1, keepdims=True))
            p_loc = jnp.exp(s_loc - m)
            p_ctx = jnp.exp(s_ctx - m)
            den = jnp.sum(p_loc, axis=-1, keepdims=True) + jnp.sum(p_ctx, axis=-1, keepdims=True)
            o_ref[rows, cols] = (_dot(p_ctx, vc) + _dot(p_loc, v_ref[band, cols])) / den


def _na_attn(qkv, kc, vc, rel):
    row0 = T_CTX // DEC_SEQ
    col0 = (A_Q + 2 * A_KV) // LANES
    n_blk = B_W // LANES
    col = lambda j: pl.BlockSpec((DEC_SEQ, LANES), lambda b, p: (row0 + b, col0 + j * n_blk + p))
    cache = pl.BlockSpec((1, PAST_LEN, LANES), lambda b, p: (b, 0, p))
    return pl.pallas_call(
        _na_attn_body,
        grid=(DEC_BATCH, n_blk),
        in_specs=[col(0), col(1), col(2), cache, cache,
                  pl.BlockSpec((NA_HEADS_PER_STEP, 16, LANES), lambda b, p: (p, 0, 0))],
        out_specs=pl.BlockSpec((DEC_SEQ, LANES), lambda b, p: (b, p)),
        out_shape=jax.ShapeDtypeStruct((T_LAT, B_W), F32),
        compiler_params=_cparams("parallel", "parallel"),
        name="na_attn",
    )(qkv, qkv, qkv, kc, vc, rel)


@functools.lru_cache(maxsize=None)
def _dft_mats(L):
    n = 2 * L
    fc = min(L, DFT_CHUNK)
    f = np.arange(L)[:, None]
    t = np.arange(L)[None, :]
    ang = 2.0 * np.pi * ((f * t) % n) / n
    m1 = np.cos(ang)
    m2 = np.sin(ang)
    m2[0, :] = np.where(np.arange(L) % 2 == 0, 1.0, -1.0)
    wgt = np.full((L, 1), 2.0)
    wgt[0, 0] = 1.0
    nch = L // fc
    fwd = np.concatenate([m1.reshape(nch, fc, L), m2.reshape(nch, fc, L)], axis=1)
    inv = np.concatenate([(m1 * wgt / n).reshape(nch, fc, L), (m2 * wgt / n).reshape(nch, fc, L)], axis=1)
    inv = np.transpose(inv, (0, 2, 1))
    return fwd.astype(np.float32), inv.astype(np.float32)


@functools.lru_cache(maxsize=None)
def _filter_consts(L):
    t = np.linspace(0.0, 1.0, L)[:, None]
    bands = (C_EMB - 1) // 2
    ang = (2.0 * math.pi / L) * np.arange(L)[:, None] * np.linspace(1e-4, bands - 1, bands)[None, :]
    z = np.concatenate([t, np.cos(ang), -np.sin(ang)], axis=-1)
    zpad = np.zeros((L, 128))
    zpad[:, :C_EMB] = z
    deltas = np.abs(np.linspace(HYENA_MIN_DECAY, HYENA_MAX_DECAY, C_DIM))
    window = np.exp(-t * deltas[None, :])
    return zpad.astype(np.float32), window.astype(np.float32)


def _filter_body(z_ref, w1_ref, b1_ref, w2_ref, b2_ref, w3_ref, b3_ref, fr_ref, w4_ref, win_ref, fm_ref,
                 hr_ref, g_ref, hq_ref, hs_scr, hd_scr):
    c = pl.program_id(0)
    fc = hr_ref.shape[0]

    @pl.when(c == 0)
    def _():
        fr = fr_ref[...]
        hh = jnp.sin(fr * (_dot_hi(z_ref[...], w1_ref[...]) + b1_ref[...]))
        hh = jnp.sin(fr * (_dot_hi(hh, w2_ref[...]) + b2_ref[...]))
        hh = jnp.sin(fr * (_dot_hi(hh, w3_ref[...]) + b3_ref[...]))
        hh = _dot_hi(hh, w4_ref[...])
        hf = hh[:, :C_DIM] * win_ref[...]
        hb = hh[:, C_DIM:] * win_ref[...]
        hs_scr[...] = hf + hb
        hd_scr[...] = hf - hb

    fm = fm_ref[0]
    hr = _dot_split(fm[:fc], hs_scr[...])
    first = (lax.broadcasted_iota(jnp.int32, (fc, C_DIM), 0) == 0) & (c == 0)
    hr_ref[...] = hr
    g_ref[...] = jnp.where(first, 0.0, _dot_split(fm[fc:], hd_scr[...]))
    hs = hs_scr[...]
    sign = jnp.where((lax.broadcasted_iota(jnp.int32, hs.shape, 0) & 1) == 0, 1.0, -1.0)
    hq_ref[...] = jnp.where(first, jnp.sum(hs * sign, axis=0, keepdims=True), hr)


def _hyena_filter(L, filt):
    w1, b1, w2, b2, w3, b3, freq, w4 = filt
    zpad, window = _filter_consts(L)
    fwd, _ = _dft_mats(L)
    nch, fc2, _ = fwd.shape
    fc = fc2 // 2
    w1p = jnp.pad(w1, ((0, 128 - C_EMB), (0, 0)))
    full = lambda shape: pl.BlockSpec(shape, lambda c: tuple(0 for _ in shape))
    out_spec = pl.BlockSpec((fc, C_DIM), lambda c: (c, 0))
    out_sd = jax.ShapeDtypeStruct((L, C_DIM), F32)
    return pl.pallas_call(
        _filter_body,
        grid=(nch,),
        in_specs=[full((L, 128)), full((128, C_FFN)), full((1, C_FFN)), full((C_FFN, C_FFN)), full((1, C_FFN)),
                  full((C_FFN, C_FFN)), full((1, C_FFN)), full((1, C_FFN)), full((C_FFN, 2 * C_DIM)),
                  full((L, C_DIM)), pl.BlockSpec((1, fc2, L), lambda c: (c, 0, 0))],
        out_specs=[out_spec, out_spec, out_spec],
        out_shape=[out_sd, out_sd, out_sd],
        scratch_shapes=[pltpu.VMEM((L, C_DIM), F32), pltpu.VMEM((L, C_DIM), F32)],
        compiler_params=_cparams("arbitrary"),
        name="hyena_filter",
    )(jnp.asarray(zpad), w1p, b1.reshape(1, C_FFN), w2, b2.reshape(1, C_FFN), w3, b3.reshape(1, C_FFN),
      freq.reshape(1, C_FFN), w4, jnp.asarray(window), jnp.asarray(fwd))


def _hyena_body(u_ref, cw_ref, cb_ref, d_ref, fm_ref, fi_ref, hr_ref, g_ref, hq_ref, y_ref,
                x0_scr, z_scr, acc_scr):
    c = pl.program_id(1)
    L = y_ref.shape[0]
    fc = hr_ref.shape[0]

    @pl.when(c == 0)
    def _():
        row = lax.broadcasted_iota(jnp.int32, (L, C_DIM), 0)

        def short_conv(sec):
            cols = slice(sec * C_DIM, (sec + 1) * C_DIM)
            u = u_ref[:, cols]
            prev = jnp.where(row == 0, 0.0, pltpu.roll(u, 1, axis=0))
            nxt = jnp.where(row == L - 1, 0.0, pltpu.roll(u, L - 1, axis=0))
            return (prev * cw_ref[0:1, cols] + u * cw_ref[1:2, cols] + nxt * cw_ref[2:3, cols]
                    + cb_ref[:, cols])

        x0_scr[...] = short_conv(0)
        z_scr[...] = short_conv(1) * short_conv(2)
        acc_scr[...] = jnp.zeros((L, C_DIM), F32)

    ab = _dot_split(fm_ref[0], z_scr[...])
    a, b = ab[:fc], ab[fc:]
    hr, g, hq = hr_ref[...], g_ref[...], hq_ref[...]
    pq = jnp.concatenate([a * hr - b * g, a * g + b * hq], axis=0)
    acc_scr[...] += _dot_split(fi_ref[0], pq)

    @pl.when(c == pl.num_programs(1) - 1)
    def _():
        y_ref[...] = x0_scr[...] * (acc_scr[...] + z_scr[...] * d_ref[...])


def _hyena(u, row_blk0, n_seq, L, conv_w, conv_b, d_skip, spec):
    hr, g, hq = spec
    fwd, inv = _dft_mats(L)
    nch, fc2, _ = fwd.shape
    fc = fc2 // 2
    u_w = 3 * C_DIM
    return pl.pallas_call(
        _hyena_body,
        grid=(n_seq, nch),
        in_specs=[pl.BlockSpec((L, u_w), lambda b, c: (row_blk0 + b, 0)),
                  pl.BlockSpec((3, u_w), lambda b, c: (0, 0)),
                  pl.BlockSpec((1, u_w), lambda b, c: (0, 0)),
                  pl.BlockSpec((1, C_DIM), lambda b, c: (0, 0)),
                  pl.BlockSpec((1, fc2, L), lambda b, c: (c, 0, 0)),
                  pl.BlockSpec((1, L, fc2), lambda b, c: (c, 0, 0)),
                  pl.BlockSpec((fc, C_DIM), lambda b, c: (c, 0)),
                  pl.BlockSpec((fc, C_DIM), lambda b, c: (c, 0)),
                  pl.BlockSpec((fc, C_DIM), lambda b, c: (c, 0))],
        out_specs=pl.BlockSpec((L, C_DIM), lambda b, c: (b, 0)),
        out_shape=jax.ShapeDtypeStruct((n_seq * L, C_DIM), F32),
        scratch_shapes=[pltpu.VMEM((L, C_DIM), F32)] * 3,
        compiler_params=_cparams("parallel", "arbitrary"),
        name="hyena",
    )(u, conv_w, conv_b.reshape(1, u_w), d_skip.reshape(1, C_DIM), jnp.asarray(fwd), jnp.asarray(inv), hr, g, hq)


def _hgrn_body(q_ref, ff_ref, fb_ref, i_ref, g_ref, lbf_ref, lbb_ref, nd_ref, s0f_ref, s0b_ref,
               o_ref, sf_ref, sb_ref, *, layer):
    L = o_ref.shape[0]
    C = GLA_CHUNK
    S = min(L, GLA_SPAN)
    nc = S // C
    n_span = L // S
    mid = C // 2
    q = _silu(q_ref[...])
    v = i_ref[...]

    def lower_bound(ref):
        gm = ref[...]
        e = jnp.exp(gm - jnp.max(gm, axis=0, keepdims=True))
        p = e / jnp.sum(e, axis=0, keepdims=True)
        return jnp.sum(p[0:layer + 1], axis=0, keepdims=True) - p[0:1]

    def gates(fx, lb):
        f = lb + (1.0 - lb) * jax.nn.sigmoid(fx)
        return 1.0 - f, jnp.log(f)

    kf, lgf = gates(ff_ref[...], lower_bound(lbf_ref))
    kb, lgb = gates(fb_ref[...], lower_bound(lbb_ref))

    chunk_shift = C.bit_length() - 1
    block_shift = D_KDIM.bit_length() - 1
    ti = lax.broadcasted_iota(jnp.int32, (S, S), 0)
    si = lax.broadcasted_iota(jnp.int32, (S, S), 1)
    same_chunk = (ti >> chunk_shift) == (si >> chunk_shift)
    causal = same_chunk & (si <= ti)
    anti = same_chunk & (si >= ti)
    row_chunk = lax.broadcasted_iota(jnp.int32, (S, nc * D_KDIM), 0) >> chunk_shift
    col_chunk = lax.broadcasted_iota(jnp.int32, (S, nc * D_KDIM), 1) >> block_shift
    own_block = row_chunk == col_chunk

    def spread(x):
        return jnp.where(own_block, jnp.concatenate([x] * nc, axis=1), 0.0)

    def chunk_cumsum(mask, lg):
        tri = mask.astype(BF16)
        hi = lg.astype(BF16)
        r1 = lg - hi.astype(F32)
        mid_t = r1.astype(BF16)
        lo = (r1 - mid_t.astype(F32)).astype(BF16)
        dot = lambda t: jnp.dot(tri, t, preferred_element_type=F32)
        return dot(hi) + dot(mid_t) + dot(lo)

    def per_chunk_rows(b, pos):
        return jnp.concatenate([jnp.broadcast_to(b[n * C + pos:n * C + pos + 1], (C, D_KDIM)) for n in range(nc)],
                               axis=0)

    def direction(k, lg, st, forward):
        outs = [None] * n_span
        mask = causal if forward else anti
        last = C - 1 if forward else 0
        for u in (range(n_span) if forward else reversed(range(n_span))):
            rows = slice(u * S, (u + 1) * S)
            qs, ks, vs = q[rows], k[rows], v[rows]
            b = chunk_cumsum(mask, lg[rows])
            btot = per_chunk_rows(b, last)
            ref = per_chunk_rows(b, mid)
            sc = jnp.where(mask, _dot_nt(qs * jnp.exp(b - ref), ks * jnp.exp(ref - b)), 0.0)
            kv_t = _dot_tn(spread(vs), ks * jnp.exp(btot - b))
            states = [None] * nc
            for n in (range(nc) if forward else reversed(range(nc))):
                states[n] = st
                decay = jnp.exp(b[n * C + last:n * C + last + 1])
                st = st * decay + kv_t[n * D_VDIM:(n + 1) * D_VDIM]
            inter = _dot_nt(spread(qs * jnp.exp(b)), jnp.concatenate(states, axis=1))
            outs[u] = _dot(sc, vs) + inter
        return outs, st

    o_f, st_f = direction(kf, lgf, jnp.transpose(s0f_ref[0, 0]), True)
    o_b, st_b = direction(kb, lgb, jnp.transpose(s0b_ref[0, 0]), False)
    sf_ref[0, 0] = jnp.transpose(st_f)
    sb_ref[0, 0] = jnp.transpose(st_b)
    o = jnp.concatenate([f + b for f, b in zip(o_f, o_b)], axis=0) if n_span > 1 else o_f[0] + o_b[0]
    o_ref[...] = _rms(o, nd_ref[...]) * _silu(g_ref[...])


def _hgrn(u, row_blk0, n_seq, L, lb_fwd, lb_bwd, norm_d, s0f, s0b, layer):
    col0 = 3 * C_DIM // D_KDIM
    col = lambda j: pl.BlockSpec((L, D_KDIM), lambda b, h: (row_blk0 + b, col0 + j * D_HEADS + h))
    lbs = pl.BlockSpec((DEPTH, D_KDIM), lambda b, h: (0, h))
    st = pl.BlockSpec((1, 1, D_KDIM, D_VDIM), lambda b, h: (b, h, 0, 0))
    st_sd = jax.ShapeDtypeStruct((n_seq, D_HEADS, D_KDIM, D_VDIM), F32)
    return pl.pallas_call(
        functools.partial(_hgrn_body, layer=layer),
        grid=(n_seq, D_HEADS),
        in_specs=[col(0), col(1), col(2), col(3), col(4), lbs, lbs,
                  pl.BlockSpec((1, D_VDIM), lambda b, h: (0, 0)), st, st],
        out_specs=[pl.BlockSpec((L, D_VDIM), lambda b, h: (b, h)), st, st],
        out_shape=[jax.ShapeDtypeStruct((n_seq * L, D_HEADS * D_VDIM), F32), st_sd, st_sd],
        compiler_params=_cparams("parallel", "parallel"),
        name="hgrn",
    )(u, u, u, u, u, lb_fwd, lb_bwd, norm_d.reshape(1, D_VDIM), s0f, s0b)


def _pack_bf16_pairs(h):
    n = h.shape[1] // 2
    hi = lax.bitcast_convert_type(h[:, :n].astype(BF16).astype(F32), jnp.int32)
    lo = lax.bitcast_convert_type(h[:, n:].astype(BF16).astype(F32), jnp.int32)
    return hi | lax.shift_right_logical(lo, 16)


def _unpack_bf16_pairs(p):
    hi = lax.bitcast_convert_type(p & jnp.int32(-65536), F32).astype(BF16)
    lo = lax.bitcast_convert_type(lax.shift_left(p, 16), F32).astype(BF16)
    return hi, lo


def _outproj_body(*refs, n_x):
    a_refs, b_refs, x_refs = refs[0:2], refs[2:4], refs[4:4 + n_x]
    mod_ref, gf_ref, w_ref, wrh_ref, wrl_ref, rb_ref, x1_ref, h2_ref, chosen_ref, gk_ref, ik_ref = refs[4 + n_x:]
    m = mod_ref[0]
    half = a_refs[0].shape[1]
    out = _dot(_token_tile(a_refs), w_ref[0:half, :]) + _dot(_token_tile(b_refs), w_ref[half:, :])
    x1 = _token_tile(x_refs) + m[:, 2 * D_MODEL:3 * D_MODEL] * out
    x1_ref[...] = x1
    h2 = _rms(x1, gf_ref[...]) * (1.0 + m[:, 4 * D_MODEL:5 * D_MODEL]) + m[:, 3 * D_MODEL:4 * D_MODEL]
    h2_ref[...] = _pack_bf16_pairs(h2)
    h_hi = h2.astype(BF16)
    h_lo = (h2 - h_hi.astype(F32)).astype(BF16)
    logits = _dot_nt(wrh_ref[...], h_hi) + _dot_nt(wrh_ref[...], h_lo) + _dot_nt(wrl_ref[...], h_hi)
    scores = jax.nn.sigmoid(logits)
    work = scores + rb_ref[...]
    expert = lax.broadcasted_iota(jnp.int32, work.shape, 0).astype(F32)
    slot = lax.broadcasted_iota(jnp.int32, (TOP_K, work.shape[1]), 0)
    chosen = jnp.zeros(work.shape, F32)
    gk = jnp.zeros((TOP_K, work.shape[1]), F32)
    ik = jnp.zeros((TOP_K, work.shape[1]), F32)
    for k in range(TOP_K):
        best = jnp.max(work, axis=0, keepdims=True)
        first = jnp.min(jnp.where(work == best, expert, float(N_EXPERTS)), axis=0, keepdims=True)
        hit = expert == first
        chosen = jnp.where(hit, 1.0, chosen)
        gk = jnp.where(slot == k, jnp.sum(jnp.where(hit, scores, 0.0), axis=0, keepdims=True), gk)
        ik = jnp.where(slot == k, first, ik)
        work = jnp.where(hit, -jnp.inf, work)
    chosen_ref[...] = chosen
    gk_ref[...] = gk / jnp.sum(gk, axis=0, keepdims=True) * ROUTE_SCALE
    ik_ref[...] = ik


def _outproj(a, b, x, mod_l, gain_ffn, w_out_bf16, w_router, router_bias):
    half = a[0].shape[1]
    a_specs, a_args = _token_specs(a, half)
    b_specs, b_args = _token_specs(b, half)
    x_specs, x_args = _token_specs(x, D_MODEL)
    wr_t = w_router.T
    wr_hi = wr_t.astype(BF16)
    wr_lo = (wr_t - wr_hi.astype(F32)).astype(BF16)
    return pl.pallas_call(
        functools.partial(_outproj_body, n_x=len(x_args)),
        grid=(T_ALL // TM,),
        in_specs=a_specs + b_specs + x_specs + [
                  pl.BlockSpec((1, 1, N_MOD * D_MODEL), lambda i: (_mod_row(i), 0, 0)),
                  pl.BlockSpec((1, D_MODEL), lambda i: (0, 0)),
                  pl.BlockSpec((2 * half, D_MODEL), lambda i: (0, 0)),
                  pl.BlockSpec((N_EXPERTS, D_MODEL), lambda i: (0, 0)),
                  pl.BlockSpec((N_EXPERTS, D_MODEL), lambda i: (0, 0)),
                  pl.BlockSpec((N_EXPERTS, 1), lambda i: (0, 0))],
        out_specs=[pl.BlockSpec((TM, D_MODEL), lambda i: (i, 0)),
                   pl.BlockSpec((TM, D_MODEL // 2), lambda i: (i, 0)),
                   pl.BlockSpec((N_EXPERTS, TM), lambda i: (0, i)),
                   pl.BlockSpec((TOP_K, TM), lambda i: (0, i)),
                   pl.BlockSpec((TOP_K, TM), lambda i: (0, i))],
        out_shape=[jax.ShapeDtypeStruct((T_ALL, D_MODEL), F32),
                   jax.ShapeDtypeStruct((T_ALL, D_MODEL // 2), jnp.int32),
                   jax.ShapeDtypeStruct((N_EXPERTS, T_ALL), F32),
                   jax.ShapeDtypeStruct((TOP_K, T_ALL), F32),
                   jax.ShapeDtypeStruct((TOP_K, T_ALL), F32)],
        compiler_params=_cparams("parallel"),
        name="outproj_router",
    )(*a_args, *b_args, *x_args, mod_l, gain_ffn.reshape(1, D_MODEL), w_out_bf16, wr_hi, wr_lo,
      router_bias.reshape(N_EXPERTS, 1))


def _route_body(chosen_ref, ik_ref, dest_ref, first_ref, count_ref, pos_scr):
    n_tiles = T_ALL // TM
    r = lax.broadcasted_iota(jnp.int32, (TM, TM), 0)
    c = lax.broadcasted_iota(jnp.int32, (TM, TM), 1)
    before = (r < c).astype(BF16)

    counts = jnp.zeros((N_EXPERTS, 1), F32)
    for i in range(n_tiles):
        cols = slice(i * TM, (i + 1) * TM)
        m = chosen_ref[:, cols]
        pos_scr[:, cols] = jnp.dot(m.astype(BF16), before, preferred_element_type=F32) + counts
        counts = counts + jnp.sum(m, axis=1, keepdims=True)
    padded = jnp.ceil(counts * (1.0 / MOE_BLK)) * MOE_BLK
    ei = lax.broadcasted_iota(jnp.int32, (N_EXPERTS, N_EXPERTS), 0)
    ej = lax.broadcasted_iota(jnp.int32, (N_EXPERTS, N_EXPERTS), 1)
    end = _dot_hi((ej <= ei).astype(F32), jnp.broadcast_to(padded, (N_EXPERTS, LANES)))[:, 0:1]
    start = end - padded

    expert = lax.broadcasted_iota(jnp.int32, (N_EXPERTS, TM), 0).astype(F32)
    slot = lax.broadcasted_iota(jnp.int32, (TOP_K, TM), 0)
    for i in range(n_tiles):
        cols = slice(i * TM, (i + 1) * TM)
        row_of = pos_scr[:, cols] + start
        ik = ik_ref[:, cols]
        acc = jnp.zeros((TOP_K, TM), F32)
        for k in range(TOP_K):
            pick = jnp.sum(jnp.where(expert == ik[k:k + 1, :], row_of, 0.0), axis=0, keepdims=True)
            acc = jnp.where(slot == k, pick, acc)
        dest_ref[:, cols] = acc.astype(jnp.int32)
    first_ref[...] = jnp.broadcast_to(start * (1.0 / MOE_BLK), (N_EXPERTS, LANES)).astype(jnp.int32)
    count_ref[...] = jnp.broadcast_to(padded * (1.0 / MOE_BLK), (N_EXPERTS, LANES)).astype(jnp.int32)


def _route(chosen, ik):
    full = lambda shape: pl.BlockSpec(shape, lambda i: (0, 0))
    return pl.pallas_call(
        _route_body,
        grid=(1,),
        in_specs=[full((N_EXPERTS, T_ALL)), full((TOP_K, T_ALL))],
        out_specs=[full((TOP_K, T_ALL)), full((N_EXPERTS, LANES)), full((N_EXPERTS, LANES))],
        out_shape=[jax.ShapeDtypeStruct((TOP_K, T_ALL), jnp.int32),
                   jax.ShapeDtypeStruct((N_EXPERTS, LANES), jnp.int32),
                   jax.ShapeDtypeStruct((N_EXPERTS, LANES), jnp.int32)],
        scratch_shapes=[pltpu.VMEM((N_EXPERTS, T_ALL), F32)],
        compiler_params=_cparams("arbitrary"),
        name="moe_route",
    )(chosen, ik)


def _sc_worker_id():
    return lax.axis_index("s") * SC_CORES + lax.axis_index("c")


def _sc_dispatch(h2p, dest):
    n_chunks = T_ALL // DISP_CHUNK
    width = h2p.shape[1]
    mesh = plsc.VectorSubcoreMesh(core_axis_name="c", subcore_axis_name="s")

    @functools.partial(
        pl.kernel, mesh=mesh,
        out_type=jax.ShapeDtypeStruct((MOE_ROWS, width), jnp.int32),
        scratch_types=[pltpu.VMEM((TOP_K, DISP_CHUNK), jnp.int32), pltpu.VMEM((DISP_CHUNK, width), jnp.int32)],
    )
    def run(x_hbm, dest_hbm, xs_hbm, idx_v, rows_v):
        wid = _sc_worker_id()
        for rep in range(-(-n_chunks // SC_WORKERS)):
            chunk = wid + rep * SC_WORKERS

            @pl.when(chunk < n_chunks)
            def _():
                tokens = pl.ds(pl.multiple_of(chunk * DISP_CHUNK, DISP_CHUNK), DISP_CHUNK)
                pltpu.sync_copy(dest_hbm.at[:, tokens], idx_v)
                pltpu.sync_copy(x_hbm.at[tokens], rows_v)
                for k in range(TOP_K):
                    pltpu.sync_copy(rows_v, xs_hbm.at[idx_v.at[k]])

    return run(h2p, dest)


def _sc_collect(y, dest_flat):
    per_worker = T_ALL // SC_WORKERS
    n_chunks = per_worker // COLLECT_CHUNK
    n_steps = TOP_K * n_chunks
    width = y.shape[1]
    mesh = plsc.VectorSubcoreMesh(core_axis_name="c", subcore_axis_name="s")

    @functools.partial(
        pl.kernel, mesh=mesh,
        out_type=jax.ShapeDtypeStruct((TOP_K * T_ALL, width), y.dtype),
        scratch_types=[pltpu.VMEM((TOP_K * per_worker,), jnp.int32),
                       pltpu.VMEM((COLLECT_CHUNK, width), y.dtype), pltpu.VMEM((COLLECT_CHUNK, width), y.dtype),
                       pltpu.SemaphoreType.DMA, pltpu.SemaphoreType.DMA],
    )
    def run(y_hbm, dest_hbm, yg_hbm, idx_v, rows0, rows1, sem0, sem1):
        wid = _sc_worker_id()
        bufs = ((rows0, sem0), (rows1, sem1))
        for k in range(TOP_K):
            pltpu.sync_copy(dest_hbm.at[pl.ds(k * T_ALL + wid * per_worker, per_worker)],
                            idx_v.at[pl.ds(k * per_worker, per_worker)])

        def gather(step, buf):
            rows, sem = buf
            idx = idx_v.at[pl.ds(pl.multiple_of(step * COLLECT_CHUNK, 8), COLLECT_CHUNK)]
            return pltpu.make_async_copy(y_hbm.at[idx], rows, sem)

        def out_rows(step):
            off = (step // n_chunks) * T_ALL + wid * per_worker + (step % n_chunks) * COLLECT_CHUNK
            return yg_hbm.at[pl.ds(pl.multiple_of(off, 8), COLLECT_CHUNK)]

        gather(0, bufs[0]).start()

        @pl.loop(0, n_steps, step=2)
        def _(base):
            for j in range(2):
                step = base + j

                @pl.when(step + 1 < n_steps)
                def _():
                    gather(step + 1, bufs[1 - j]).start()

                gather(step, bufs[j]).wait()
                pltpu.sync_copy(bufs[j][0], out_rows(step))

    return run(y, dest_flat)


def _expert_body(first_ref, count_ref, xs_hbm, wg_ref, wu_ref, wd_ref, y_hbm,
                 wg_bf, wu_bf, wd_bf, x_buf, y_buf, in_sem, out_sem):
    e = pl.program_id(0)
    first = first_ref[e]
    count = count_ref[e]
    n_used = first_ref[N_EXPERTS - 1] + count_ref[N_EXPERTS - 1]
    half = D_MODEL // 2
    wg_bf[...] = wg_ref[0, 0].astype(BF16)
    wu_bf[...] = wu_ref[0, 0].astype(BF16)
    wd_bf[...] = wd_ref[0, 0].astype(BF16)

    def part_rows(g, part, n_parts):
        size = MOE_BLK // n_parts
        return pl.ds(pl.multiple_of(g * MOE_BLK + part * size, size), size), pl.ds(part * size, size)

    def in_copies(g):
        slot = g & (EXPERT_SLOTS - 1)
        out = []
        for part in range(EXPERT_IN_PARTS):
            src, dst = part_rows(g, part, EXPERT_IN_PARTS)
            out.append(pltpu.make_async_copy(xs_hbm.at[src], x_buf.at[slot, dst], in_sem.at[slot]))
        return out

    def out_copies(g):
        slot = g & (EXPERT_SLOTS - 1)
        out = []
        for part in range(EXPERT_OUT_PARTS):
            dst, src = part_rows(g, part, EXPERT_OUT_PARTS)
            out.append(pltpu.make_async_copy(y_buf.at[slot, src], y_hbm.at[dst], out_sem.at[slot]))
        return out

    @pl.when((first == 0) & (count > 0))
    def _():
        for ahead in range(EXPERT_SLOTS - 1):
            @pl.when(ahead < n_used)
            def _():
                for cp in in_copies(ahead):
                    cp.start()

    def block(b, carry):
        g = first + b
        slot = g & (EXPERT_SLOTS - 1)
        for cp in in_copies(g):
            cp.wait()

        @pl.when(g + EXPERT_SLOTS - 1 < n_used)
        def _():
            for cp in in_copies(g + EXPERT_SLOTS - 1):
                cp.start()

        @pl.when(g >= EXPERT_SLOTS)
        def _():
            for cp in out_copies(g - EXPERT_SLOTS):
                cp.wait()

        hi, lo = _unpack_bf16_pairs(x_buf[slot])

        def proj(w_bf):
            return (jnp.dot(hi, w_bf[0:half, :], preferred_element_type=F32)
                    + jnp.dot(lo, w_bf[half:, :], preferred_element_type=F32))

        hid = _silu(proj(wg_bf)) * proj(wu_bf)
        y_buf[slot] = _pack_bf16_pairs(jnp.dot(hid.astype(BF16), wd_bf[...], preferred_element_type=F32))
        for cp in out_copies(g):
            cp.start()
        return carry

    lax.fori_loop(0, count, block, 0)

    @pl.when(e == N_EXPERTS - 1)
    def _():
        for back in range(EXPERT_SLOTS, 0, -1):
            @pl.when(n_used >= back)
            def _():
                for cp in out_copies(n_used - back):
                    cp.wait()


EXPERT_SLOTS = 4
EXPERT_IN_PARTS = 2
EXPERT_OUT_PARTS = 4


def _experts(first_blk, n_blk, xs, layer, w_gate, w_up, w_down):
    w_in = pl.BlockSpec((1, 1, D_MODEL, D_EXPERT), lambda e, first, count: (layer, e, 0, 0))
    grid_spec = pltpu.PrefetchScalarGridSpec(
        num_scalar_prefetch=2,
        grid=(N_EXPERTS,),
        in_specs=[pl.BlockSpec(memory_space=pl.ANY), w_in, w_in,
                  pl.BlockSpec((1, 1, D_EXPERT, D_MODEL), lambda e, first, count: (layer, e, 0, 0))],
        out_specs=pl.BlockSpec(memory_space=pl.ANY),
        scratch_shapes=[pltpu.VMEM((D_MODEL, D_EXPERT), BF16), pltpu.VMEM((D_MODEL, D_EXPERT), BF16),
                        pltpu.VMEM((D_EXPERT, D_MODEL), BF16),
                        pltpu.VMEM((EXPERT_SLOTS, MOE_BLK, D_MODEL // 2), jnp.int32),
                        pltpu.VMEM((EXPERT_SLOTS, MOE_BLK, D_MODEL // 2), jnp.int32),
                        pltpu.SemaphoreType.DMA((EXPERT_SLOTS,)), pltpu.SemaphoreType.DMA((EXPERT_SLOTS,))],
    )
    return pl.pallas_call(
        _expert_body,
        grid_spec=grid_spec,
        out_shape=jax.ShapeDtypeStruct((MOE_ROWS, D_MODEL // 2), jnp.int32),
        compiler_params=_cparams("arbitrary"),
        name="moe_experts",
    )(first_blk, n_blk, xs, w_gate, w_up, w_down)


def _combine_body(x1_ref, h2_ref, yg_ref, gk_ref, mod_ref, sg_ref, su_ref, sd_ref, fn_ref, *o_refs, final):
    hi, lo = _unpack_bf16_pairs(h2_ref[...])
    half = D_MODEL // 2

    def proj(w_ref):
        return _dot(hi, w_ref[0:half, :]) + _dot(lo, w_ref[half:, :])

    shared = _dot(_silu(proj(sg_ref)) * proj(su_ref), sd_ref[...])
    acc_hi, acc_lo = shared[:, :half], shared[:, half:]
    gk = gk_ref[...]
    for k in range(TOP_K):
        y_hi, y_lo = _unpack_bf16_pairs(yg_ref[k])
        acc_hi = acc_hi + gk[:, k:k + 1] * y_hi.astype(F32)
        acc_lo = acc_lo + gk[:, k:k + 1] * y_lo.astype(F32)
    acc = jnp.concatenate([acc_hi, acc_lo], axis=1)
    m = mod_ref[0]
    y = x1_ref[...] + m[:, 5 * D_MODEL:6 * D_MODEL] * acc
    if not final:
        o_refs[0][...] = y
        return
    y = _rms(y, fn_ref[...])
    is_ctx = pl.program_id(0) < N_CTX_TILES

    @pl.when(is_ctx)
    def _():
        o_refs[0][...] = y

    @pl.when(jnp.logical_not(is_ctx))
    def _():
        o_refs[1][...] = y


def _combine(x1, h2p, yg, gk, mod_l, ws_gate, ws_up, ws_down, final_norm, final):
    tok = lambda shape: pl.BlockSpec(shape, lambda i: (i, 0))
    full = lambda shape: pl.BlockSpec(shape, lambda i: (0, 0))
    if final:
        out_specs, _ = _token_specs((None, None), D_MODEL)
        out_shape = [jax.ShapeDtypeStruct((T_CTX, D_MODEL), F32), jax.ShapeDtypeStruct((T_LAT, D_MODEL), F32)]
    else:
        out_specs = tok((TM, D_MODEL))
        out_shape = jax.ShapeDtypeStruct((T_ALL, D_MODEL), F32)
    return pl.pallas_call(
        functools.partial(_combine_body, final=final),
        grid=(T_ALL // TM,),
        in_specs=[tok((TM, D_MODEL)), tok((TM, D_MODEL // 2)),
                  pl.BlockSpec((TOP_K, TM, D_MODEL // 2), lambda i: (0, i, 0)),
                  tok((TM, TOP_K)),
                  pl.BlockSpec((1, 1, N_MOD * D_MODEL), lambda i: (_mod_row(i), 0, 0)),
                  full((D_MODEL, D_EXPERT)), full((D_MODEL, D_EXPERT)), full((D_EXPERT, D_MODEL)),
                  full((1, D_MODEL))],
        out_specs=out_specs,
        out_shape=out_shape,
        compiler_params=_cparams("arbitrary"),
        name="moe_combine",
    )(x1, h2p, yg, gk, mod_l, ws_gate, ws_up, ws_down, final_norm.reshape(1, D_MODEL))


def _moe(x1, h2p, chosen, gk, ik, mod_l, layer, w_gate, w_up, w_down, ws_gate, ws_up, ws_down, final_norm, final):
    dest, first_blk, n_blk = _route(chosen, ik)
    xs = _sc_dispatch(h2p, dest)
    y = _experts(first_blk[:, 0], n_blk[:, 0], xs, layer, w_gate, w_up, w_down)
    yg = _sc_collect(y, dest.reshape(-1)).reshape(TOP_K, T_ALL, D_MODEL // 2)
    return _combine(x1, h2p, yg, gk.T, mod_l, ws_gate.astype(BF16), ws_up.astype(BF16), ws_down.astype(BF16),
                    final_norm, final)


def kernel(x_prompt, x_sample, cache_a_k, cache_a_v, cache_b_k, cache_b_v, state_d_fwd, state_d_bwd, c, c_ctx, w_ada, b_ada, norm_mix, norm_ffn, w_in_attn, w_out_attn, sink_a, rpb_b, w_in_rec, w_out_rec, conv_w, conv_b, filt_w1, filt_b1, filt_w2, filt_b2, filt_w3, filt_b3, filt_freq, filt_w4, d_skip, lb_fwd, lb_bwd, norm_d, w_router, router_bias, w_gate, w_up, w_down, ws_gate, ws_up, ws_down, final_norm):
    x = (x_prompt.reshape(T_CTX, D_MODEL), x_sample.reshape(T_LAT, D_MODEL))
    cvec = jnp.concatenate([c_ctx[None, :], c], axis=0)
    mod = _ada(cvec, w_ada, b_ada).reshape(DEPTH, CVEC_PAD, 1, N_MOD * D_MODEL)

    new_kv = None
    new_state = None
    for l in range(DEPTH):
        j = l // 2
        final = l == DEPTH - 1
        if l % 2 == 0:
            qkv = _inproj(x, mod[l], norm_mix[l], w_in_attn[j].astype(BF16))
            oa_ctx, ob_ctx, *new_kv = _ctx_attn(qkv, sink_a[j])
            new_kv = tuple(new_kv)
            q_rot, k_rot = _rope(qkv)
            cache = lambda t: t[:, j].reshape(DEC_BATCH, PAST_LEN, -1)
            oa_lat = _win_attn(qkv, q_rot, k_rot, cache(cache_a_k), cache(cache_a_v), sink_a[j])
            ob_lat = _na_attn(qkv, cache(cache_b_k), cache(cache_b_v), _na_rel_rows(rpb_b[j]))
            mix_a = (oa_ctx, oa_lat)
            mix_b = (ob_ctx, ob_lat)
            w_out = w_out_attn[j]
        else:
            u = _inproj(x, mod[l], norm_mix[l], w_in_rec[j].astype(BF16))
            filt = (filt_w1[j], filt_b1[j], filt_w2[j], filt_b2[j], filt_w3[j], filt_b3[j], filt_freq[j],
                    filt_w4[j])
            y_ctx = _hyena(u, 0, BATCH, SEQ, conv_w[j], conv_b[j], d_skip[j], _hyena_filter(SEQ, filt))
            y_lat = _hyena(u, T_CTX // DEC_SEQ, DEC_BATCH, DEC_SEQ, conv_w[j], conv_b[j], d_skip[j],
                           _hyena_filter(DEC_SEQ, filt))
            zeros = jnp.zeros((BATCH, D_HEADS, D_KDIM, D_VDIM), F32)
            o_ctx, s_f, s_b = _hgrn(u, 0, BATCH, SEQ, lb_fwd, lb_bwd, norm_d[j], zeros, zeros, l)
            o_lat, _, _ = _hgrn(u, T_CTX // DEC_SEQ, DEC_BATCH, DEC_SEQ, lb_fwd, lb_bwd, norm_d[j],
                                state_d_fwd[:, j], state_d_bwd[:, j], l)
            new_state = (s_f[:, None], s_b[:, None])
            mix_a = (y_ctx, y_lat)
            mix_b = (o_ctx, o_lat)
            w_out = w_out_rec[j]
        x1, h2p, chosen, gk, ik = _outproj(mix_a, mix_b, x, mod[l], norm_ffn[l], w_out.astype(BF16), w_router[l],
                                           router_bias[l])
        x = _moe(x1, h2p, chosen, gk, ik, mod[l], l, w_gate, w_up, w_down, ws_gate[l], ws_up[l],
                 ws_down[l], final_norm, final)

    y_prompt = x[0].reshape(BATCH, SEQ, D_MODEL)
    y_sample = x[1].reshape(DEC_BATCH, DEC_SEQ, D_MODEL)
    return (y_prompt, y_sample) + new_kv + new_state
```

```python
import functools
import math

import numpy as np
import jax
import jax.numpy as jnp
from jax import lax
from jax.experimental import pallas as pl
from jax.experimental.pallas import tpu as pltpu
from jax.experimental.pallas import tpu_sc as plsc

F32 = jnp.float32
BF16 = jnp.bfloat16
HI = lax.Precision.HIGHEST

D_MODEL = 1024
BATCH = 16
SEQ = 256
DEPTH = 2
DEC_BATCH = 2
DEC_SEQ = 1024
PAST_LEN = 512
GRID_W = 64
HEAD_DIM = 64
N_MOD = 6
RMS_EPS = 1e-6
A_HEADS = 8
A_KV_HEADS = 2
A_GROUP = A_HEADS // A_KV_HEADS
WINDOW = 128
ROPE_BASE = 10000.0
B_HEADS = 8
NA_ROWS = 8
NA_COLS = 16
C_DIM = 512
C_EMB = 33
C_FFN = 64
HYENA_MIN_DECAY = math.log(1e-2) / 1.5
HYENA_MAX_DECAY = math.log(1e-2) / 0.3
D_KDIM = 128
D_VDIM = 128
D_HEADS = 4
N_EXPERTS = 64
TOP_K = 8
D_EXPERT = 256
ROUTE_SCALE = 2.5
A_Q = A_HEADS * HEAD_DIM
A_KV = A_KV_HEADS * HEAD_DIM
B_W = B_HEADS * HEAD_DIM
ATTN_IN = A_Q + 2 * A_KV + 3 * B_W
REC_IN = 3 * C_DIM + 5 * D_HEADS * D_KDIM

T_CTX = BATCH * SEQ
T_LAT = DEC_BATCH * DEC_SEQ
T_ALL = T_CTX + T_LAT
N_CVEC = 1 + DEC_BATCH
CVEC_PAD = 8
TM = 256
MASK_NEG = -1e30
GLA_CHUNK = 64
GLA_SPAN = 256
DFT_CHUNK = 256
MOE_BLK = 512
MOE_NBLK = -(-(T_ALL * TOP_K + N_EXPERTS * (MOE_BLK - 1)) // MOE_BLK)
MOE_ROWS = MOE_NBLK * MOE_BLK
SC_CORES = 2
SC_SUBCORES = 16
SC_WORKERS = SC_CORES * SC_SUBCORES
DISP_CHUNK = 128
DISP_SPLIT = 2
COLLECT_CHUNK = 64
VMEM_LIMIT = 56 * 1024 * 1024


def _cparams(*sem):
    return pltpu.CompilerParams(dimension_semantics=sem, vmem_limit_bytes=VMEM_LIMIT)


def _mod_row(i):
    return jnp.where(i < T_CTX // TM, 0, 1 + (i - T_CTX // TM) // (DEC_SEQ // TM))


def _dot(a, b):
    return jnp.dot(a.astype(BF16), b.astype(BF16), preferred_element_type=F32)


def _dot_nt(a, b):
    return lax.dot_general(a.astype(BF16), b.astype(BF16), (((1,), (1,)), ((), ())),
                           preferred_element_type=F32)


def _dot_tn(a, b):
    return lax.dot_general(a.astype(BF16), b.astype(BF16), (((0,), (0,)), ((), ())),
                           preferred_element_type=F32)


def _dot_hi(a, b):
    return jnp.dot(a, b, precision=HI, preferred_element_type=F32)


def _split_bf16(x):
    hi = x.astype(BF16)
    return hi, (x - hi.astype(F32)).astype(BF16)


def _dot_split(a, b):
    a_hi, a_lo = _split_bf16(a)
    b_hi, b_lo = _split_bf16(b)
    dot = lambda x, y: jnp.dot(x, y, preferred_element_type=F32)
    return dot(a_hi, b_hi) + dot(a_hi, b_lo) + dot(a_lo, b_hi)


def _silu(x):
    return x * jax.nn.sigmoid(x)


def _rms(x, g):
    return x * lax.rsqrt(jnp.mean(x * x, axis=-1, keepdims=True) + RMS_EPS) * g


ADA_TN = 1536
ADA_UNROLL = 4


def _ada_body(cb_ref, w_ref, b_ref, o_ref):
    tn = o_ref.shape[-1]
    n_slab = tn // LANES

    def step(k8, accs):
        r0 = pl.multiple_of(k8 * 8, 8)
        sk = [_silu(cb_ref[j, pl.ds(r0, 8), :]) for j in range(N_CVEC)]
        out = []
        for s in range(n_slab):
            wk = w_ref[0, pl.ds(r0, 8), s * LANES:(s + 1) * LANES]
            out.extend(accs[s * N_CVEC + j] + wk * sk[j] for j in range(N_CVEC))
        return tuple(out)

    accs = lax.fori_loop(0, D_MODEL // 8, step,
                         tuple(jnp.zeros((8, LANES), F32) for _ in range(n_slab * N_CVEC)), unroll=ADA_UNROLL)
    o_ref[0] = jnp.zeros((CVEC_PAD, tn), F32)
    for s in range(n_slab):
        for j in range(N_CVEC):
            o_ref[0, j:j + 1, s * LANES:(s + 1) * LANES] = (
                jnp.sum(accs[s * N_CVEC + j], axis=0, keepdims=True) + b_ref[0, :, s * LANES:(s + 1) * LANES])


def _ada(cvec, w_ada, b_ada):
    n_out = N_MOD * D_MODEL
    c_lanes = jnp.broadcast_to(cvec[:, :, None], (N_CVEC, D_MODEL, LANES))
    return pl.pallas_call(
        _ada_body,
        grid=(DEPTH, n_out // ADA_TN),
        in_specs=[pl.BlockSpec((N_CVEC, D_MODEL, LANES), lambda l, n: (0, 0, 0)),
                  pl.BlockSpec((1, D_MODEL, ADA_TN), lambda l, n: (l, 0, n)),
                  pl.BlockSpec((1, 1, ADA_TN), lambda l, n: (l, 0, n))],
        out_specs=pl.BlockSpec((1, CVEC_PAD, ADA_TN), lambda l, n: (l, 0, n)),
        out_shape=jax.ShapeDtypeStruct((DEPTH, CVEC_PAD, n_out), F32),
        compiler_params=_cparams("parallel", "parallel"),
        name="ada",
    )(c_lanes, w_ada, b_ada.reshape(DEPTH, 1, n_out))


N_CTX_TILES = T_CTX // TM


def _token_specs(x, width):
    if not isinstance(x, tuple):
        return [pl.BlockSpec((TM, width), lambda i: (i, 0))], (x,)
    return ([pl.BlockSpec((TM, width), lambda i: (jnp.minimum(i, N_CTX_TILES - 1), 0)),
             pl.BlockSpec((TM, width), lambda i: (jnp.maximum(i - N_CTX_TILES, 0), 0))], x)


def _token_tile(refs):
    if len(refs) == 1:
        return refs[0][...]
    return jnp.where(pl.program_id(0) < N_CTX_TILES, refs[0][...], refs[1][...])


def _inproj_body(*refs, n_x):
    x_refs, (mod_ref, g_ref, w_ref, o_ref) = refs[:n_x], refs[n_x:]
    m = mod_ref[0]
    h = _rms(_token_tile(x_refs), g_ref[...]) * (1.0 + m[:, D_MODEL:2 * D_MODEL]) + m[:, 0:D_MODEL]
    o_ref[...] = _dot(h, w_ref[...])


def _inproj(x, mod_l, gain, w_bf16):
    n = w_bf16.shape[1]
    x_specs, x_args = _token_specs(x, D_MODEL)
    return pl.pallas_call(
        functools.partial(_inproj_body, n_x=len(x_args)),
        grid=(T_ALL // TM,),
        in_specs=x_specs + [pl.BlockSpec((1, 1, N_MOD * D_MODEL), lambda i: (_mod_row(i), 0, 0)),
                            pl.BlockSpec((1, D_MODEL), lambda i: (0, 0)),
                            pl.BlockSpec((D_MODEL, n), lambda i: (0, 0))],
        out_specs=pl.BlockSpec((TM, n), lambda i: (i, 0)),
        out_shape=jax.ShapeDtypeStruct((T_ALL, n), F32),
        compiler_params=_cparams("parallel"),
        name="inproj",
    )(*x_args, mod_l, gain.reshape(1, D_MODEL), w_bf16)


def _head_cols(h):
    return slice(h * HEAD_DIM, (h + 1) * HEAD_DIM)


def _group_rows(ref, rows, first_col, sink_ref, hk):
    n = rows.stop - rows.start
    q = jnp.concatenate([ref[rows, first_col + g * HEAD_DIM:first_col + (g + 1) * HEAD_DIM]
                         for g in range(A_GROUP)], axis=0)
    sink = jnp.concatenate([jnp.broadcast_to(sink_ref[:, hk * A_GROUP + g:hk * A_GROUP + g + 1], (n, 1))
                            for g in range(A_GROUP)], axis=0)
    return q, sink


def _ctx_attn_body(qkv_ref, sink_ref, oa_ref, ob_ref, ak_ref, av_ref, bk_ref, bv_ref):
    scale = HEAD_DIM ** -0.5
    lane = lax.broadcasted_iota(jnp.int32, (SEQ, LANES), 1)
    in_half = [lane < HEAD_DIM, lane >= HEAD_DIM]

    def attend(q, k, v, sink):
        s = _dot_nt(q, k) * scale
        m = jnp.max(s, axis=-1, keepdims=True)
        if sink is not None:
            m = jnp.maximum(m, sink)
        p = jnp.exp(s - m)
        den = jnp.sum(p, axis=-1, keepdims=True)
        if sink is not None:
            den = den + jnp.exp(sink - m)
        return _dot(p, v) / den

    def tile(first_col, t):
        return qkv_ref[:, first_col + t * LANES:first_col + (t + 1) * LANES]

    base = A_Q + 2 * A_KV
    for hk in range(A_KV_HEADS):
        ak_ref[0, 0, :, hk, :] = qkv_ref[:, A_Q + hk * HEAD_DIM:A_Q + (hk + 1) * HEAD_DIM]
        av_ref[0, 0, :, hk, :] = qkv_ref[:, A_Q + A_KV + hk * HEAD_DIM:A_Q + A_KV + (hk + 1) * HEAD_DIM]
    for h in range(B_HEADS):
        bk_ref[0, 0, :, h, :] = qkv_ref[:, base + B_W + h * HEAD_DIM:base + B_W + (h + 1) * HEAD_DIM]
        bv_ref[0, 0, :, h, :] = qkv_ref[:, base + 2 * B_W + h * HEAD_DIM:base + 2 * B_W + (h + 1) * HEAD_DIM]

    k_t, v_t = tile(A_Q, 0), tile(A_Q + A_KV, 0)
    k_sw, v_sw = pltpu.roll(k_t, HEAD_DIM, axis=1), pltpu.roll(v_t, HEAD_DIM, axis=1)
    tiles_per_kv = A_GROUP // HEADS_PER_TILE
    for hk in range(A_KV_HEADS):
        q_tiles = [tile(0, hk * tiles_per_kv + j) for j in range(tiles_per_kv)]
        halves = []
        for p in range(HEADS_PER_TILE):
            q = jnp.concatenate([jnp.where(in_half[p], qt, 0.0) for qt in q_tiles], axis=0)
            heads = [(hk * tiles_per_kv + j) * HEADS_PER_TILE + p for j in range(tiles_per_kv)]
            sink = jnp.concatenate([jnp.broadcast_to(sink_ref[:, h:h + 1], (SEQ, 1)) for h in heads], axis=0)
            halves.append(attend(q, k_t if p == hk else k_sw, v_t if p == hk else v_sw, sink))
        first_half = lax.broadcasted_iota(jnp.int32, halves[0].shape, 1) < HEAD_DIM
        o = jnp.where(first_half, halves[0], halves[1])
        for j in range(tiles_per_kv):
            t = hk * tiles_per_kv + j
            oa_ref[:, t * LANES:(t + 1) * LANES] = o[j * SEQ:(j + 1) * SEQ]

    for t in range(B_HEADS // HEADS_PER_TILE):
        q_t, k_b, v_b = tile(base, t), tile(base + B_W, t), tile(base + 2 * B_W, t)
        halves = [attend(jnp.where(in_half[p], q_t, 0.0), k_b, v_b, None) for p in range(HEADS_PER_TILE)]
        ob_ref[:, t * LANES:(t + 1) * LANES] = jnp.where(in_half[0], halves[0], halves[1])


def _ctx_attn(qkv, sink):
    kv_spec = lambda heads: pl.BlockSpec((1, 1, SEQ, heads, HEAD_DIM), lambda b: (b, 0, 0, 0, 0))
    kv_sd = lambda heads: jax.ShapeDtypeStruct((BATCH, 1, SEQ, heads, HEAD_DIM), F32)
    return pl.pallas_call(
        _ctx_attn_body,
        grid=(BATCH,),
        in_specs=[pl.BlockSpec((SEQ, ATTN_IN), lambda b: (b, 0)),
                  pl.BlockSpec((1, A_HEADS), lambda b: (0, 0))],
        out_specs=[pl.BlockSpec((SEQ, A_Q), lambda b: (b, 0)), pl.BlockSpec((SEQ, B_W), lambda b: (b, 0)),
                   kv_spec(A_KV_HEADS), kv_spec(A_KV_HEADS), kv_spec(B_HEADS), kv_spec(B_HEADS)],
        out_shape=[jax.ShapeDtypeStruct((T_CTX, A_Q), F32), jax.ShapeDtypeStruct((T_CTX, B_W), F32),
                   kv_sd(A_KV_HEADS), kv_sd(A_KV_HEADS), kv_sd(B_HEADS), kv_sd(B_HEADS)],
        compiler_params=_cparams("parallel"),
        name="ctx_attn",
    )(qkv, sink.reshape(1, A_HEADS))


@functools.lru_cache(maxsize=None)
def _rope_tables(width):
    half = HEAD_DIM // 2
    t = np.arange(DEC_SEQ)
    inv = ROPE_BASE ** (-np.arange(0, half, 2, dtype=np.float64) / half)
    ang_r = (t // GRID_W)[:, None] * inv[None, :]
    ang_c = (t % GRID_W)[:, None] * inv[None, :]
    cos = np.concatenate([np.cos(ang_r)] * 2 + [np.cos(ang_c)] * 2, axis=-1)
    sin = np.concatenate([-np.sin(ang_r), np.sin(ang_r), -np.sin(ang_c), np.sin(ang_c)], axis=-1)
    reps = width // HEAD_DIM
    return (np.tile(cos, (1, reps)).astype(np.float32), np.tile(sin, (1, reps)).astype(np.float32))


def _rope_body(q_ref, k_ref, cq_ref, sq_ref, ck_ref, sk_ref, qo_ref, ko_ref):
    quarter = HEAD_DIM // 4

    def rot(x, cos, sin):
        w = x.shape[-1]
        lane = lax.broadcasted_iota(jnp.int32, x.shape, 1)
        fwd = pltpu.roll(x, w - quarter, axis=1)
        bwd = pltpu.roll(x, quarter, axis=1)
        partner = jnp.where((lane & (2 * quarter - 1)) < quarter, fwd, bwd)
        return x * cos + partner * sin

    qo_ref[...] = rot(q_ref[...], cq_ref[...], sq_ref[...])
    ko_ref[...] = rot(k_ref[...], ck_ref[...], sk_ref[...])


def _rope(qkv):
    cq, sq = _rope_tables(A_Q)
    ck, sk = _rope_tables(A_KV)
    tab = lambda w: pl.BlockSpec((DEC_SEQ, w), lambda b: (0, 0))
    row0 = T_CTX // DEC_SEQ
    return pl.pallas_call(
        _rope_body,
        grid=(DEC_BATCH,),
        in_specs=[pl.BlockSpec((DEC_SEQ, A_Q), lambda b: (row0 + b, 0)),
                  pl.BlockSpec((DEC_SEQ, A_KV), lambda b: (row0 + b, A_Q // A_KV)),
                  tab(A_Q), tab(A_Q), tab(A_KV), tab(A_KV)],
        out_specs=[pl.BlockSpec((DEC_SEQ, A_Q), lambda b: (b, 0)),
                   pl.BlockSpec((DEC_SEQ, A_KV), lambda b: (b, 0))],
        out_shape=[jax.ShapeDtypeStruct((T_LAT, A_Q), F32), jax.ShapeDtypeStruct((T_LAT, A_KV), F32)],
        compiler_params=_cparams("parallel"),
        name="rope",
    )(qkv, qkv, jnp.asarray(cq), jnp.asarray(sq), jnp.asarray(ck), jnp.asarray(sk))


WIN_QB = 256


def _pick_head(x, h, n_heads):
    out = x[:, _head_cols(0)]
    for i in range(1, n_heads):
        out = jnp.where(h == i, x[:, _head_cols(i)], out)
    return out


def _win_attn_body(qraw_ref, qrot_ref, krot_ref, v_ref, kc_ref, vc_ref, sink_ref, o_ref):
    scale = HEAD_DIM ** -0.5
    hk = pl.program_id(1)
    tiles = A_GROUP // HEADS_PER_TILE

    def kv_in_half(x):
        swapped = pltpu.roll(x, HEAD_DIM, axis=1)
        return [jnp.where(hk == p, x, swapped) for p in range(HEADS_PER_TILE)]

    k, v, kc, vc = kv_in_half(krot_ref[...]), kv_in_half(v_ref[...]), kv_in_half(kc_ref[0]), kv_in_half(vc_ref[0])
    head_lane = lax.broadcasted_iota(jnp.int32, (1, A_HEADS), 1)

    def sink_rows(p):
        heads = [hk * A_GROUP + j * HEADS_PER_TILE + p for j in range(tiles)]
        vals = [jnp.sum(jnp.where(head_lane == h, sink_ref[...], 0.0), axis=-1, keepdims=True) for h in heads]
        return jnp.concatenate([jnp.broadcast_to(s, (WIN_QB, 1)) for s in vals], axis=0)

    sinks = [sink_rows(p) for p in range(HEADS_PER_TILE)]
    lane = lax.broadcasted_iota(jnp.int32, (tiles * WIN_QB, LANES), 1)
    in_half = [lane < HEAD_DIM, lane >= HEAD_DIM]
    for qb in range(DEC_SEQ // WIN_QB):
        q0 = qb * WIN_QB
        rows = slice(q0, q0 + WIN_QB)
        lo = max(0, q0 - WINDOW)
        hi = min(DEC_SEQ, q0 + WIN_QB + WINDOW)
        q_rot = jnp.concatenate([qrot_ref[rows, j * LANES:(j + 1) * LANES] for j in range(tiles)], axis=0)
        q_raw = jnp.concatenate([qraw_ref[rows, j * LANES:(j + 1) * LANES] for j in range(tiles)], axis=0)
        halves = []
        for p in range(HEADS_PER_TILE):
            s_loc = _dot_nt(jnp.where(in_half[p], q_rot, 0.0), k[p][lo:hi]) * scale
            qpos = q0 + (lax.broadcasted_iota(jnp.int32, s_loc.shape, 0) & (WIN_QB - 1))
            kpos = lo + lax.broadcasted_iota(jnp.int32, s_loc.shape, 1)
            s_loc = jnp.where(jnp.abs(kpos - qpos) <= WINDOW, s_loc, MASK_NEG)
            s_ctx = _dot_nt(jnp.where(in_half[p], q_raw, 0.0), kc[p]) * scale
            m = jnp.maximum(jnp.maximum(jnp.max(s_loc, axis=-1, keepdims=True),
                                        jnp.max(s_ctx, axis=-1, keepdims=True)), sinks[p])
            p_loc = jnp.exp(s_loc - m)
            p_ctx = jnp.exp(s_ctx - m)
            den = (jnp.sum(p_loc, axis=-1, keepdims=True) + jnp.sum(p_ctx, axis=-1, keepdims=True)
                   + jnp.exp(sinks[p] - m))
            halves.append((_dot(p_ctx, vc[p]) + _dot(p_loc, v[p][lo:hi])) / den)
        o = jnp.where(in_half[0], halves[0], halves[1])
        for j in range(tiles):
            o_ref[rows, j * LANES:(j + 1) * LANES] = o[j * WIN_QB:(j + 1) * WIN_QB]


def _win_attn(qkv, q_rot, k_rot, kc, vc, sink):
    row0 = T_CTX // DEC_SEQ
    gw = A_GROUP * HEAD_DIM
    return pl.pallas_call(
        _win_attn_body,
        grid=(DEC_BATCH, A_KV_HEADS),
        in_specs=[pl.BlockSpec((DEC_SEQ, gw), lambda b, h: (row0 + b, h)),
                  pl.BlockSpec((DEC_SEQ, gw), lambda b, h: (b, h)),
                  pl.BlockSpec((DEC_SEQ, A_KV), lambda b, h: (b, 0)),
                  pl.BlockSpec((DEC_SEQ, A_KV), lambda b, h: (row0 + b, (A_Q + A_KV) // A_KV)),
                  pl.BlockSpec((1, PAST_LEN, A_KV), lambda b, h: (b, 0, 0)),
                  pl.BlockSpec((1, PAST_LEN, A_KV), lambda b, h: (b, 0, 0)),
                  pl.BlockSpec((1, A_HEADS), lambda b, h: (0, 0))],
        out_specs=pl.BlockSpec((DEC_SEQ, gw), lambda b, h: (b, h)),
        out_shape=jax.ShapeDtypeStruct((T_LAT, A_Q), F32),
        compiler_params=_cparams("parallel", "parallel"),
        name="win_attn",
    )(qkv, q_rot, k_rot, qkv, kc, vc, sink.reshape(1, A_HEADS))


GRID_ROWS = DEC_SEQ // GRID_W
NA_BAND = min(NA_ROWS, GRID_ROWS)


NA_REL_ROWS = 2 * NA_ROWS - 1
NA_REL_COLS = 2 * NA_COLS - 1
LANES = 128
HEADS_PER_TILE = LANES // HEAD_DIM


def _na_rel_rows(rpb):
    pad = jnp.zeros((B_HEADS, NA_REL_ROWS, GRID_W - NA_REL_COLS), F32)
    one = jnp.concatenate([rpb, pad], axis=-1)
    nxt = jnp.concatenate([one[:, 1:], jnp.zeros((B_HEADS, 1, GRID_W), F32)], axis=1)
    both = jnp.concatenate([one, nxt], axis=-1)
    return jnp.concatenate([both, jnp.zeros((B_HEADS, 16 - NA_REL_ROWS, LANES), F32)], axis=1)


NA_HEADS_PER_STEP = LANES // HEAD_DIM


def _na_row_groups():
    groups = []
    for r in range(GRID_ROWS):
        rs = min(max(r - NA_ROWS // 2, 0), GRID_ROWS - NA_BAND)
        if groups and groups[-1][2] == rs:
            groups[-1][1] += 1
        else:
            groups.append([r, 1, rs])
    return groups


def _na_attn_body(q_ref, k_ref, v_ref, kc_ref, vc_ref, rel_ref, o_ref):
    scale = HEAD_DIM ** -0.5
    cq = lax.broadcasted_iota(jnp.int32, (GRID_W, LANES), 0)
    kcol = lax.broadcasted_iota(jnp.int32, (GRID_W, LANES), 1) & (GRID_W - 1)
    cs = jnp.clip(cq - NA_COLS // 2, 0, GRID_W - NA_COLS)
    col_ok = (kcol >= cs) & (kcol < cs + NA_COLS)
    kc = kc_ref[0]
    vc = vc_ref[0]
    tiles = {}

    def pair_tile(hh, a):
        if (hh, a) not in tiles:
            x = jnp.broadcast_to(rel_ref[hh, a:a + 1, :], (GRID_W, LANES))
            t = pltpu.roll(x, LANES - (NA_COLS - 1), axis=1, stride=1, stride_axis=0)
            tiles[hh, a] = jnp.where(col_ok, t, MASK_NEG)
        return tiles[hh, a]

    for r0, n_r, rs in _na_row_groups():
        rows = slice(r0 * GRID_W, (r0 + n_r) * GRID_W)
        band = slice(rs * GRID_W, (rs + NA_BAND) * GRID_W)
        q_t, k_t, v_t = q_ref[rows, :], k_ref[band, :], v_ref[band, :]
        head_of_lane = lax.broadcasted_iota(jnp.int32, q_t.shape, 1) >> (HEAD_DIM.bit_length() - 1)
        o = jnp.zeros(q_t.shape, F32)
        for hh in range(NA_HEADS_PER_STEP):
            bias = jnp.concatenate(
                [jnp.concatenate([pair_tile(hh, rs - r + NA_ROWS - 1 + 2 * i) for i in range(NA_BAND // 2)], axis=1)
                 for r in range(r0, r0 + n_r)], axis=0)
            q = jnp.where(head_of_lane == hh, q_t, 0.0)
            s_loc = _dot_nt(q, k_t) * scale + bias
            s_ctx = _dot_nt(q, kc) * scale
            m = jnp.maximum(jnp.max(s_loc, axis=-1, keepdims=True), jnp.max(s_ctx, axis=-1, keepdims=True))
            p_loc = jnp.exp(s_loc - m)
            p_ctx = jnp.exp(s_ctx - m)
            den = jnp.sum(p_loc, axis=-1, keepdims=True) + jnp.sum(p_ctx, axis=-1, keepdims=True)
            o = jnp.where(head_of_lane == hh, (_dot(p_ctx, vc) + _dot(p_loc, v_t)) / den, o)
        o_ref[rows, :] = o


def _na_attn(qkv, kc, vc, rel):
    row0 = T_CTX // DEC_SEQ
    col0 = (A_Q + 2 * A_KV) // LANES
    n_blk = B_W // LANES
    col = lambda j: pl.BlockSpec((DEC_SEQ, LANES), lambda b, p: (row0 + b, col0 + j * n_blk + p))
    cache = pl.BlockSpec((1, PAST_LEN, LANES), lambda b, p: (b, 0, p))
    return pl.pallas_call(
        _na_attn_body,
        grid=(DEC_BATCH, n_blk),
        in_specs=[col(0), col(1), col(2), cache, cache,
                  pl.BlockSpec((NA_HEADS_PER_STEP, 16, LANES), lambda b, p: (p, 0, 0))],
        out_specs=pl.BlockSpec((DEC_SEQ, LANES), lambda b, p: (b, p)),
        out_shape=jax.ShapeDtypeStruct((T_LAT, B_W), F32),
        compiler_params=_cparams("parallel", "parallel"),
        name="na_attn",
    )(qkv, qkv, qkv, kc, vc, rel)


@functools.lru_cache(maxsize=None)
def _dft_mats(L):
    n = 2 * L
    fc = min(L, DFT_CHUNK)
    f = np.arange(L)[:, None]
    t = np.arange(L)[None, :]
    ang = 2.0 * np.pi * ((f * t) % n) / n
    m1 = np.cos(ang)
    m2 = np.sin(ang)
    m2[0, :] = np.where(np.arange(L) % 2 == 0, 1.0, -1.0)
    wgt = np.full((L, 1), 2.0)
    wgt[0, 0] = 1.0
    nch = L // fc
    fwd = np.concatenate([m1.reshape(nch, fc, L), m2.reshape(nch, fc, L)], axis=1)
    inv = np.concatenate([(m1 * wgt / n).reshape(nch, fc, L), (m2 * wgt / n).reshape(nch, fc, L)], axis=1)
    inv = np.transpose(inv, (0, 2, 1))
    return fwd.astype(np.float32), inv.astype(np.float32)


@functools.lru_cache(maxsize=None)
def _filter_consts(L):
    t = np.linspace(0.0, 1.0, L)[:, None]
    bands = (C_EMB - 1) // 2
    ang = (2.0 * math.pi / L) * np.arange(L)[:, None] * np.linspace(1e-4, bands - 1, bands)[None, :]
    z = np.concatenate([t, np.cos(ang), -np.sin(ang)], axis=-1)
    zpad = np.zeros((L, 128))
    zpad[:, :C_EMB] = z
    deltas = np.abs(np.linspace(HYENA_MIN_DECAY, HYENA_MAX_DECAY, C_DIM))
    window = np.exp(-t * deltas[None, :])
    return zpad.astype(np.float32), window.astype(np.float32)


def _filter_body(z_ref, w1_ref, b1_ref, w2_ref, b2_ref, w3_ref, b3_ref, fr_ref, w4_ref, win_ref, fm_ref,
                 hr_ref, g_ref, hq_ref, hs_scr, hd_scr):
    c = pl.program_id(0)
    fc = hr_ref.shape[0]

    @pl.when(c == 0)
    def _():
        fr = fr_ref[...]
        hh = jnp.sin(fr * (_dot_hi(z_ref[...], w1_ref[...]) + b1_ref[...]))
        hh = jnp.sin(fr * (_dot_hi(hh, w2_ref[...]) + b2_ref[...]))
        hh = jnp.sin(fr * (_dot_hi(hh, w3_ref[...]) + b3_ref[...]))
        hh = _dot_hi(hh, w4_ref[...])
        hf = hh[:, :C_DIM] * win_ref[...]
        hb = hh[:, C_DIM:] * win_ref[...]
        hs_scr[...] = hf + hb
        hd_scr[...] = hf - hb

    fm = fm_ref[0]
    hr = _dot_split(fm[:fc], hs_scr[...])
    first = (lax.broadcasted_iota(jnp.int32, (fc, C_DIM), 0) == 0) & (c == 0)
    hr_ref[...] = hr
    g_ref[...] = jnp.where(first, 0.0, _dot_split(fm[fc:], hd_scr[...]))
    hs = hs_scr[...]
    sign = jnp.where((lax.broadcasted_iota(jnp.int32, hs.shape, 0) & 1) == 0, 1.0, -1.0)
    hq_ref[...] = jnp.where(first, jnp.sum(hs * sign, axis=0, keepdims=True), hr)


def _hyena_filter(L, filt):
    w1, b1, w2, b2, w3, b3, freq, w4 = filt
    zpad, window = _filter_consts(L)
    fwd, _ = _dft_mats(L)
    nch, fc2, _ = fwd.shape
    fc = fc2 // 2
    w1p = jnp.pad(w1, ((0, 128 - C_EMB), (0, 0)))
    full = lambda shape: pl.BlockSpec(shape, lambda c: tuple(0 for _ in shape))
    out_spec = pl.BlockSpec((fc, C_DIM), lambda c: (c, 0))
    out_sd = jax.ShapeDtypeStruct((L, C_DIM), F32)
    return pl.pallas_call(
        _filter_body,
        grid=(nch,),
        in_specs=[full((L, 128)), full((128, C_FFN)), full((1, C_FFN)), full((C_FFN, C_FFN)), full((1, C_FFN)),
                  full((C_FFN, C_FFN)), full((1, C_FFN)), full((1, C_FFN)), full((C_FFN, 2 * C_DIM)),
                  full((L, C_DIM)), pl.BlockSpec((1, fc2, L), lambda c: (c, 0, 0))],
        out_specs=[out_spec, out_spec, out_spec],
        out_shape=[out_sd, out_sd, out_sd],
        scratch_shapes=[pltpu.VMEM((L, C_DIM), F32), pltpu.VMEM((L, C_DIM), F32)],
        compiler_params=_cparams("arbitrary"),
        name="hyena_filter",
    )(jnp.asarray(zpad), w1p, b1.reshape(1, C_FFN), w2, b2.reshape(1, C_FFN), w3, b3.reshape(1, C_FFN),
      freq.reshape(1, C_FFN), w4, jnp.asarray(window), jnp.asarray(fwd))


def _hyena_body(u_ref, cw_ref, cb_ref, d_ref, fm_ref, fi_ref, hr_ref, g_ref, hq_ref, y_ref,
                x0_scr, z_scr, acc_scr):
    c = pl.program_id(1)
    L = y_ref.shape[0]
    fc = hr_ref.shape[0]

    @pl.when(c == 0)
    def _():
        row = lax.broadcasted_iota(jnp.int32, (L, C_DIM), 0)

        def short_conv(sec):
            cols = slice(sec * C_DIM, (sec + 1) * C_DIM)
            u = u_ref[:, cols]
            prev = jnp.where(row == 0, 0.0, pltpu.roll(u, 1, axis=0))
            nxt = jnp.where(row == L - 1, 0.0, pltpu.roll(u, L - 1, axis=0))
            return (prev * cw_ref[0:1, cols] + u * cw_ref[1:2, cols] + nxt * cw_ref[2:3, cols]
                    + cb_ref[:, cols])

        x0_scr[...] = short_conv(0)
        z_scr[...] = short_conv(1) * short_conv(2)
        acc_scr[...] = jnp.zeros((L, C_DIM), F32)

    ab = _dot_split(fm_ref[0], z_scr[...])
    a, b = ab[:fc], ab[fc:]
    hr, g, hq = hr_ref[...], g_ref[...], hq_ref[...]
    pq = jnp.concatenate([a * hr - b * g, a * g + b * hq], axis=0)
    acc_scr[...] += _dot_split(fi_ref[0], pq)

    @pl.when(c == pl.num_programs(1) - 1)
    def _():
        y_ref[...] = x0_scr[...] * (acc_scr[...] + z_scr[...] * d_ref[...])


def _hyena(u, row_blk0, n_seq, L, conv_w, conv_b, d_skip, spec):
    hr, g, hq = spec
    fwd, inv = _dft_mats(L)
    nch, fc2, _ = fwd.shape
    fc = fc2 // 2
    u_w = 3 * C_DIM
    return pl.pallas_call(
        _hyena_body,
        grid=(n_seq, nch),
        in_specs=[pl.BlockSpec((L, u_w), lambda b, c: (row_blk0 + b, 0)),
                  pl.BlockSpec((3, u_w), lambda b, c: (0, 0)),
                  pl.BlockSpec((1, u_w), lambda b, c: (0, 0)),
                  pl.BlockSpec((1, C_DIM), lambda b, c: (0, 0)),
                  pl.BlockSpec((1, fc2, L), lambda b, c: (c, 0, 0)),
                  pl.BlockSpec((1, L, fc2), lambda b, c: (c, 0, 0)),
                  pl.BlockSpec((fc, C_DIM), lambda b, c: (c, 0)),
                  pl.BlockSpec((fc, C_DIM), lambda b, c: (c, 0)),
                  pl.BlockSpec((fc, C_DIM), lambda b, c: (c, 0))],
        out_specs=pl.BlockSpec((L, C_DIM), lambda b, c: (b, 0)),
        out_shape=jax.ShapeDtypeStruct((n_seq * L, C_DIM), F32),
        scratch_shapes=[pltpu.VMEM((L, C_DIM), F32)] * 3,
        compiler_params=_cparams("parallel", "arbitrary"),
        name="hyena",
    )(u, conv_w, conv_b.reshape(1, u_w), d_skip.reshape(1, C_DIM), jnp.asarray(fwd), jnp.asarray(inv), hr, g, hq)


def _hgrn_body(q_ref, ff_ref, fb_ref, i_ref, g_ref, lbf_ref, lbb_ref, nd_ref, s0f_ref, s0b_ref,
               o_ref, sf_ref, sb_ref, *, layer):
    L = o_ref.shape[0]
    C = GLA_CHUNK
    S = min(L, GLA_SPAN)
    nc = S // C
    n_span = L // S
    mid = C // 2
    q = _silu(q_ref[...])
    v = i_ref[...]

    def lower_bound(ref):
        gm = ref[...]
        e = jnp.exp(gm - jnp.max(gm, axis=0, keepdims=True))
        p = e / jnp.sum(e, axis=0, keepdims=True)
        return jnp.sum(p[0:layer + 1], axis=0, keepdims=True) - p[0:1]

    def gates(fx, lb):
        f = lb + (1.0 - lb) * jax.nn.sigmoid(fx)
        return 1.0 - f, jnp.log(f)

    kf, lgf = gates(ff_ref[...], lower_bound(lbf_ref))
    kb, lgb = gates(fb_ref[...], lower_bound(lbb_ref))

    chunk_shift = C.bit_length() - 1
    block_shift = D_KDIM.bit_length() - 1
    ti = lax.broadcasted_iota(jnp.int32, (S, S), 0)
    si = lax.broadcasted_iota(jnp.int32, (S, S), 1)
    same_chunk = (ti >> chunk_shift) == (si >> chunk_shift)
    causal = same_chunk & (si <= ti)
    anti = same_chunk & (si >= ti)
    row_chunk = lax.broadcasted_iota(jnp.int32, (S, nc * D_KDIM), 0) >> chunk_shift
    col_chunk = lax.broadcasted_iota(jnp.int32, (S, nc * D_KDIM), 1) >> block_shift
    own_block = row_chunk == col_chunk

    def spread(x):
        return jnp.where(own_block, jnp.concatenate([x] * nc, axis=1), 0.0)

    def chunk_cumsum(mask, lg):
        tri = mask.astype(BF16)
        hi = lg.astype(BF16)
        r1 = lg - hi.astype(F32)
        mid_t = r1.astype(BF16)
        lo = (r1 - mid_t.astype(F32)).astype(BF16)
        dot = lambda t: jnp.dot(tri, t, preferred_element_type=F32)
        return dot(hi) + dot(mid_t) + dot(lo)

    def per_chunk_rows(b, pos):
        return jnp.concatenate([jnp.broadcast_to(b[n * C + pos:n * C + pos + 1], (C, D_KDIM)) for n in range(nc)],
                               axis=0)

    def direction(k, lg, st, forward):
        outs = [None] * n_span
        mask = causal if forward else anti
        last = C - 1 if forward else 0
        for u in (range(n_span) if forward else reversed(range(n_span))):
            rows = slice(u * S, (u + 1) * S)
            qs, ks, vs = q[rows], k[rows], v[rows]
            b = chunk_cumsum(mask, lg[rows])
            btot = per_chunk_rows(b, last)
            ref = per_chunk_rows(b, mid)
            sc = jnp.where(mask, _dot_nt(qs * jnp.exp(b - ref), ks * jnp.exp(ref - b)), 0.0)
            kv_t = _dot_tn(spread(vs), ks * jnp.exp(btot - b))
            states = [None] * nc
            for n in (range(nc) if forward else reversed(range(nc))):
                states[n] = st
                decay = jnp.exp(b[n * C + last:n * C + last + 1])
                st = st * decay + kv_t[n * D_VDIM:(n + 1) * D_VDIM]
            inter = _dot_nt(spread(qs * jnp.exp(b)), jnp.concatenate(states, axis=1))
            outs[u] = _dot(sc, vs) + inter
        return outs, st

    o_f, st_f = direction(kf, lgf, jnp.transpose(s0f_ref[0, 0]), True)
    o_b, st_b = direction(kb, lgb, jnp.transpose(s0b_ref[0, 0]), False)
    sf_ref[0, 0] = jnp.transpose(st_f)
    sb_ref[0, 0] = jnp.transpose(st_b)
    o = jnp.concatenate([f + b for f, b in zip(o_f, o_b)], axis=0) if n_span > 1 else o_f[0] + o_b[0]
    o_ref[...] = _rms(o, nd_ref[...]) * _silu(g_ref[...])


def _hgrn(u, row_blk0, n_seq, L, lb_fwd, lb_bwd, norm_d, s0f, s0b, layer):
    col0 = 3 * C_DIM // D_KDIM
    col = lambda j: pl.BlockSpec((L, D_KDIM), lambda b, h: (row_blk0 + b, col0 + j * D_HEADS + h))
    lbs = pl.BlockSpec((DEPTH, D_KDIM), lambda b, h: (0, h))
    st = pl.BlockSpec((1, 1, D_KDIM, D_VDIM), lambda b, h: (b, h, 0, 0))
    st_sd = jax.ShapeDtypeStruct((n_seq, D_HEADS, D_KDIM, D_VDIM), F32)
    return pl.pallas_call(
        functools.partial(_hgrn_body, layer=layer),
        grid=(n_seq, D_HEADS),
        in_specs=[col(0), col(1), col(2), col(3), col(4), lbs, lbs,
                  pl.BlockSpec((1, D_VDIM), lambda b, h: (0, 0)), st, st],
        out_specs=[pl.BlockSpec((L, D_VDIM), lambda b, h: (b, h)), st, st],
        out_shape=[jax.ShapeDtypeStruct((n_seq * L, D_HEADS * D_VDIM), F32), st_sd, st_sd],
        compiler_params=_cparams("parallel", "parallel"),
        name="hgrn",
    )(u, u, u, u, u, lb_fwd, lb_bwd, norm_d.reshape(1, D_VDIM), s0f, s0b)


def _pack_bf16_pairs(h):
    n = h.shape[1] // 2
    hi = lax.bitcast_convert_type(h[:, :n].astype(BF16).astype(F32), jnp.int32)
    lo = lax.bitcast_convert_type(h[:, n:].astype(BF16).astype(F32), jnp.int32)
    return hi | lax.shift_right_logical(lo, 16)


def _unpack_bf16_pairs(p):
    hi = lax.bitcast_convert_type(p & jnp.int32(-65536), F32).astype(BF16)
    lo = lax.bitcast_convert_type(lax.shift_left(p, 16), F32).astype(BF16)
    return hi, lo


def _outproj_body(*refs, n_x):
    a_refs, b_refs, x_refs = refs[0:2], refs[2:4], refs[4:4 + n_x]
    mod_ref, gf_ref, w_ref, wrh_ref, wrl_ref, rb_ref, x1_ref, h2_ref, chosen_ref, gk_ref, ik_ref = refs[4 + n_x:]
    m = mod_ref[0]
    half = a_refs[0].shape[1]
    out = _dot(_token_tile(a_refs), w_ref[0:half, :]) + _dot(_token_tile(b_refs), w_ref[half:, :])
    x1 = _token_tile(x_refs) + m[:, 2 * D_MODEL:3 * D_MODEL] * out
    x1_ref[...] = x1
    h2 = _rms(x1, gf_ref[...]) * (1.0 + m[:, 4 * D_MODEL:5 * D_MODEL]) + m[:, 3 * D_MODEL:4 * D_MODEL]
    h2_ref[...] = _pack_bf16_pairs(h2)
    h_hi = h2.astype(BF16)
    h_lo = (h2 - h_hi.astype(F32)).astype(BF16)
    logits = _dot_nt(wrh_ref[...], h_hi) + _dot_nt(wrh_ref[...], h_lo) + _dot_nt(wrl_ref[...], h_hi)
    scores = jax.nn.sigmoid(logits)
    work = scores + rb_ref[...]
    expert = lax.broadcasted_iota(jnp.int32, work.shape, 0).astype(F32)
    slot = lax.broadcasted_iota(jnp.int32, (TOP_K, work.shape[1]), 0)
    chosen = jnp.zeros(work.shape, F32)
    gk = jnp.zeros((TOP_K, work.shape[1]), F32)
    ik = jnp.zeros((TOP_K, work.shape[1]), F32)
    for k in range(TOP_K):
        best = jnp.max(work, axis=0, keepdims=True)
        first = jnp.min(jnp.where(work == best, expert, float(N_EXPERTS)), axis=0, keepdims=True)
        hit = expert == first
        chosen = jnp.where(hit, 1.0, chosen)
        gk = jnp.where(slot == k, jnp.sum(jnp.where(hit, scores, 0.0), axis=0, keepdims=True), gk)
        ik = jnp.where(slot == k, first, ik)
        work = jnp.where(hit, -jnp.inf, work)
    chosen_ref[...] = chosen
    gk_ref[...] = gk / jnp.sum(gk, axis=0, keepdims=True) * ROUTE_SCALE
    ik_ref[...] = ik


def _outproj(a, b, x, mod_l, gain_ffn, w_out_bf16, w_router, router_bias):
    half = a[0].shape[1]
    a_specs, a_args = _token_specs(a, half)
    b_specs, b_args = _token_specs(b, half)
    x_specs, x_args = _token_specs(x, D_MODEL)
    wr_t = w_router.T
    wr_hi = wr_t.astype(BF16)
    wr_lo = (wr_t - wr_hi.astype(F32)).astype(BF16)
    return pl.pallas_call(
        functools.partial(_outproj_body, n_x=len(x_args)),
        grid=(T_ALL // TM,),
        in_specs=a_specs + b_specs + x_specs + [
                  pl.BlockSpec((1, 1, N_MOD * D_MODEL), lambda i: (_mod_row(i), 0, 0)),
                  pl.BlockSpec((1, D_MODEL), lambda i: (0, 0)),
                  pl.BlockSpec((2 * half, D_MODEL), lambda i: (0, 0)),
                  pl.BlockSpec((N_EXPERTS, D_MODEL), lambda i: (0, 0)),
                  pl.BlockSpec((N_EXPERTS, D_MODEL), lambda i: (0, 0)),
                  pl.BlockSpec((N_EXPERTS, 1), lambda i: (0, 0))],
        out_specs=[pl.BlockSpec((TM, D_MODEL), lambda i: (i, 0)),
                   pl.BlockSpec((TM, D_MODEL // 2), lambda i: (i, 0)),
                   pl.BlockSpec((N_EXPERTS, TM), lambda i: (0, i)),
                   pl.BlockSpec((TOP_K, TM), lambda i: (0, i)),
                   pl.BlockSpec((TOP_K, TM), lambda i: (0, i))],
        out_shape=[jax.ShapeDtypeStruct((T_ALL, D_MODEL), F32),
                   jax.ShapeDtypeStruct((T_ALL, D_MODEL // 2), jnp.int32),
                   jax.ShapeDtypeStruct((N_EXPERTS, T_ALL), F32),
                   jax.ShapeDtypeStruct((TOP_K, T_ALL), F32),
                   jax.ShapeDtypeStruct((TOP_K, T_ALL), F32)],
        compiler_params=_cparams("parallel"),
        name="outproj_router",
    )(*a_args, *b_args, *x_args, mod_l, gain_ffn.reshape(1, D_MODEL), w_out_bf16, wr_hi, wr_lo,
      router_bias.reshape(N_EXPERTS, 1))


def _route_body(chosen_ref, ik_ref, dest_ref, first_ref, count_ref, pos_scr):
    n_tiles = T_ALL // TM
    r = lax.broadcasted_iota(jnp.int32, (TM, TM), 0)
    c = lax.broadcasted_iota(jnp.int32, (TM, TM), 1)
    before = (r < c).astype(BF16)

    counts = jnp.zeros((N_EXPERTS, 1), F32)
    for i in range(n_tiles):
        cols = slice(i * TM, (i + 1) * TM)
        m = chosen_ref[:, cols]
        pos_scr[:, cols] = jnp.dot(m.astype(BF16), before, preferred_element_type=F32) + counts
        counts = counts + jnp.sum(m, axis=1, keepdims=True)
    padded = jnp.ceil(counts * (1.0 / MOE_BLK)) * MOE_BLK
    ei = lax.broadcasted_iota(jnp.int32, (N_EXPERTS, N_EXPERTS), 0)
    ej = lax.broadcasted_iota(jnp.int32, (N_EXPERTS, N_EXPERTS), 1)
    end = _dot_hi((ej <= ei).astype(F32), jnp.broadcast_to(padded, (N_EXPERTS, LANES)))[:, 0:1]
    start = end - padded

    expert = lax.broadcasted_iota(jnp.int32, (N_EXPERTS, TM), 0).astype(F32)
    slot = lax.broadcasted_iota(jnp.int32, (TOP_K, TM), 0)
    for i in range(n_tiles):
        cols = slice(i * TM, (i + 1) * TM)
        row_of = pos_scr[:, cols] + start
        ik = ik_ref[:, cols]
        acc = jnp.zeros((TOP_K, TM), F32)
        for k in range(TOP_K):
            pick = jnp.sum(jnp.where(expert == ik[k:k + 1, :], row_of, 0.0), axis=0, keepdims=True)
            acc = jnp.where(slot == k, pick, acc)
        dest_ref[:, cols] = acc.astype(jnp.int32)
    first_ref[...] = jnp.broadcast_to(start * (1.0 / MOE_BLK), (N_EXPERTS, LANES)).astype(jnp.int32)
    count_ref[...] = jnp.broadcast_to(padded * (1.0 / MOE_BLK), (N_EXPERTS, LANES)).astype(jnp.int32)


def _route(chosen, ik):
    full = lambda shape: pl.BlockSpec(shape, lambda i: (0, 0))
    return pl.pallas_call(
        _route_body,
        grid=(1,),
        in_specs=[full((N_EXPERTS, T_ALL)), full((TOP_K, T_ALL))],
        out_specs=[full((TOP_K, T_ALL)), full((N_EXPERTS, LANES)), full((N_EXPERTS, LANES))],
        out_shape=[jax.ShapeDtypeStruct((TOP_K, T_ALL), jnp.int32),
                   jax.ShapeDtypeStruct((N_EXPERTS, LANES), jnp.int32),
                   jax.ShapeDtypeStruct((N_EXPERTS, LANES), jnp.int32)],
        scratch_shapes=[pltpu.VMEM((N_EXPERTS, T_ALL), F32)],
        compiler_params=_cparams("arbitrary"),
        name="moe_route",
    )(chosen, ik)


def _sc_worker_id():
    return lax.axis_index("s") * SC_CORES + lax.axis_index("c")


def _sc_dispatch(h2p, dest):
    n_chunks = T_ALL // DISP_CHUNK
    k_per = TOP_K // DISP_SPLIT
    items_per_worker = n_chunks * DISP_SPLIT // SC_WORKERS
    chunk_stride = SC_WORKERS // DISP_SPLIT
    width = h2p.shape[1]
    mesh = plsc.VectorSubcoreMesh(core_axis_name="c", subcore_axis_name="s")

    @functools.partial(
        pl.kernel, mesh=mesh,
        out_type=jax.ShapeDtypeStruct((MOE_ROWS, width), jnp.int32),
        scratch_types=[pltpu.VMEM((k_per, DISP_CHUNK), jnp.int32), pltpu.VMEM((DISP_CHUNK, width), jnp.int32)],
    )
    def run(x_hbm, dest_hbm, xs_hbm, idx_v, rows_v):
        wid = _sc_worker_id()
        group = wid % DISP_SPLIT
        for i in range(items_per_worker):
            chunk = i * chunk_stride + wid // DISP_SPLIT
            tokens = pl.ds(pl.multiple_of(chunk * DISP_CHUNK, DISP_CHUNK), DISP_CHUNK)
            pltpu.sync_copy(dest_hbm.at[group, :, tokens], idx_v)
            pltpu.sync_copy(x_hbm.at[tokens], rows_v)
            for k in range(k_per):
                pltpu.sync_copy(rows_v, xs_hbm.at[idx_v.at[k]])

    return run(h2p, dest.reshape(DISP_SPLIT, k_per, T_ALL))


def _sc_collect(y, dest_flat):
    per_worker = T_ALL // SC_WORKERS
    n_chunks = per_worker // COLLECT_CHUNK
    n_steps = TOP_K * n_chunks
    width = y.shape[1]
    mesh = plsc.VectorSubcoreMesh(core_axis_name="c", subcore_axis_name="s")

    @functools.partial(
        pl.kernel, mesh=mesh,
        out_type=jax.ShapeDtypeStruct((TOP_K * T_ALL, width), y.dtype),
        scratch_types=[pltpu.VMEM((TOP_K * per_worker,), jnp.int32),
                       pltpu.VMEM((COLLECT_CHUNK, width), y.dtype), pltpu.VMEM((COLLECT_CHUNK, width), y.dtype),
                       pltpu.SemaphoreType.DMA, pltpu.SemaphoreType.DMA],
    )
    def run(y_hbm, dest_hbm, yg_hbm, idx_v, rows0, rows1, sem0, sem1):
        wid = _sc_worker_id()
        bufs = ((rows0, sem0), (rows1, sem1))
        for k in range(TOP_K):
            pltpu.sync_copy(dest_hbm.at[pl.ds(k * T_ALL + wid * per_worker, per_worker)],
                            idx_v.at[pl.ds(k * per_worker, per_worker)])

        def gather(step, buf):
            rows, sem = buf
            idx = idx_v.at[pl.ds(pl.multiple_of(step * COLLECT_CHUNK, 8), COLLECT_CHUNK)]
            return pltpu.make_async_copy(y_hbm.at[idx], rows, sem)

        def out_rows(step):
            off = (step // n_chunks) * T_ALL + wid * per_worker + (step % n_chunks) * COLLECT_CHUNK
            return yg_hbm.at[pl.ds(pl.multiple_of(off, 8), COLLECT_CHUNK)]

        gather(0, bufs[0]).start()

        @pl.loop(0, n_steps, step=2)
        def _(base):
            for j in range(2):
                step = base + j

                @pl.when(step + 1 < n_steps)
                def _():
                    gather(step + 1, bufs[1 - j]).start()

                gather(step, bufs[j]).wait()
                pltpu.sync_copy(bufs[j][0], out_rows(step))

    return run(y, dest_flat)


def _expert_body(first_ref, count_ref, xs_hbm, wg_ref, wu_ref, wd_ref, y_hbm,
                 wg_bf, wu_bf, wd_bf, x_buf, y_buf, in_sem, out_sem):
    e = pl.program_id(0)
    first = first_ref[e]
    count = count_ref[e]
    n_used = first_ref[N_EXPERTS - 1] + count_ref[N_EXPERTS - 1]
    half = D_MODEL // 2
    wg_bf[...] = wg_ref[0, 0].astype(BF16)
    wu_bf[...] = wu_ref[0, 0].astype(BF16)
    wd_bf[...] = wd_ref[0, 0].astype(BF16)

    def part_rows(g, part, n_parts):
        size = MOE_BLK // n_parts
        return pl.ds(pl.multiple_of(g * MOE_BLK + part * size, size), size), pl.ds(part * size, size)

    def in_copies(g):
        slot = g & (EXPERT_SLOTS - 1)
        out = []
        for part in range(EXPERT_IN_PARTS):
            src, dst = part_rows(g, part, EXPERT_IN_PARTS)
            out.append(pltpu.make_async_copy(xs_hbm.at[src], x_buf.at[slot, dst], in_sem.at[slot]))
        return out

    def out_copies(g):
        slot = g & (EXPERT_SLOTS - 1)
        out = []
        for part in range(EXPERT_OUT_PARTS):
            dst, src = part_rows(g, part, EXPERT_OUT_PARTS)
            out.append(pltpu.make_async_copy(y_buf.at[slot, src], y_hbm.at[dst], out_sem.at[slot]))
        return out

    @pl.when((first == 0) & (count > 0))
    def _():
        for ahead in range(EXPERT_SLOTS - 1):
            @pl.when(ahead < n_used)
            def _():
                for cp in in_copies(ahead):
                    cp.start()

    def block(b, carry):
        g = first + b
        slot = g & (EXPERT_SLOTS - 1)
        for cp in in_copies(g):
            cp.wait()

        @pl.when(g + EXPERT_SLOTS - 1 < n_used)
        def _():
            for cp in in_copies(g + EXPERT_SLOTS - 1):
                cp.start()

        @pl.when(g >= EXPERT_SLOTS)
        def _():
            for cp in out_copies(g - EXPERT_SLOTS):
                cp.wait()

        hi, lo = _unpack_bf16_pairs(x_buf[slot])

        def proj(w_bf):
            return (jnp.dot(hi, w_bf[0:half, :], preferred_element_type=F32)
                    + jnp.dot(lo, w_bf[half:, :], preferred_element_type=F32))

        hid = _silu(proj(wg_bf)) * proj(wu_bf)
        y_buf[slot] = _pack_bf16_pairs(jnp.dot(hid.astype(BF16), wd_bf[...], preferred_element_type=F32))
        for cp in out_copies(g):
            cp.start()
        return carry

    lax.fori_loop(0, count, block, 0)

    @pl.when(e == N_EXPERTS - 1)
    def _():
        for back in range(EXPERT_SLOTS, 0, -1):
            @pl.when(n_used >= back)
            def _():
                for cp in out_copies(n_used - back):
                    cp.wait()


EXPERT_SLOTS = 4
EXPERT_IN_PARTS = 2
EXPERT_OUT_PARTS = 4


def _experts(first_blk, n_blk, xs, layer, w_gate, w_up, w_down):
    w_in = pl.BlockSpec((1, 1, D_MODEL, D_EXPERT), lambda e, first, count: (layer, e, 0, 0))
    grid_spec = pltpu.PrefetchScalarGridSpec(
        num_scalar_prefetch=2,
        grid=(N_EXPERTS,),
        in_specs=[pl.BlockSpec(memory_space=pl.ANY), w_in, w_in,
                  pl.BlockSpec((1, 1, D_EXPERT, D_MODEL), lambda e, first, count: (layer, e, 0, 0))],
        out_specs=pl.BlockSpec(memory_space=pl.ANY),
        scratch_shapes=[pltpu.VMEM((D_MODEL, D_EXPERT), BF16), pltpu.VMEM((D_MODEL, D_EXPERT), BF16),
                        pltpu.VMEM((D_EXPERT, D_MODEL), BF16),
                        pltpu.VMEM((EXPERT_SLOTS, MOE_BLK, D_MODEL // 2), jnp.int32),
                        pltpu.VMEM((EXPERT_SLOTS, MOE_BLK, D_MODEL // 2), jnp.int32),
                        pltpu.SemaphoreType.DMA((EXPERT_SLOTS,)), pltpu.SemaphoreType.DMA((EXPERT_SLOTS,))],
    )
    return pl.pallas_call(
        _expert_body,
        grid_spec=grid_spec,
        out_shape=jax.ShapeDtypeStruct((MOE_ROWS, D_MODEL // 2), jnp.int32),
        compiler_params=_cparams("arbitrary"),
        name="moe_experts",
    )(first_blk, n_blk, xs, w_gate, w_up, w_down)


def _combine_body(x1_ref, h2_ref, yg_ref, gk_ref, mod_ref, sg_ref, su_ref, sd_ref, fn_ref, *o_refs, final):
    hi, lo = _unpack_bf16_pairs(h2_ref[...])
    half = D_MODEL // 2

    def proj(w_ref):
        return _dot(hi, w_ref[0:half, :]) + _dot(lo, w_ref[half:, :])

    shared = _dot(_silu(proj(sg_ref)) * proj(su_ref), sd_ref[...])
    acc_hi, acc_lo = shared[:, :half], shared[:, half:]
    gk = gk_ref[...]
    for k in range(TOP_K):
        y_hi, y_lo = _unpack_bf16_pairs(yg_ref[k])
        acc_hi = acc_hi + gk[:, k:k + 1] * y_hi.astype(F32)
        acc_lo = acc_lo + gk[:, k:k + 1] * y_lo.astype(F32)
    acc = jnp.concatenate([acc_hi, acc_lo], axis=1)
    m = mod_ref[0]
    y = x1_ref[...] + m[:, 5 * D_MODEL:6 * D_MODEL] * acc
    if not final:
        o_refs[0][...] = y
        return
    y = _rms(y, fn_ref[...])
    is_ctx = pl.program_id(0) < N_CTX_TILES

    @pl.when(is_ctx)
    def _():
        o_refs[0][...] = y

    @pl.when(jnp.logical_not(is_ctx))
    def _():
        o_refs[1][...] = y


def _combine(x1, h2p, yg, gk, mod_l, ws_gate, ws_up, ws_down, final_norm, final):
    tok = lambda shape: pl.BlockSpec(shape, lambda i: (i, 0))
    full = lambda shape: pl.BlockSpec(shape, lambda i: (0, 0))
    if final:
        out_specs, _ = _token_specs((None, None), D_MODEL)
        out_shape = [jax.ShapeDtypeStruct((T_CTX, D_MODEL), F32), jax.ShapeDtypeStruct((T_LAT, D_MODEL), F32)]
    else:
        out_specs = tok((TM, D_MODEL))
        out_shape = jax.ShapeDtypeStruct((T_ALL, D_MODEL), F32)
    return pl.pallas_call(
        functools.partial(_combine_body, final=final),
        grid=(T_ALL // TM,),
        in_specs=[tok((TM, D_MODEL)), tok((TM, D_MODEL // 2)),
                  pl.BlockSpec((TOP_K, TM, D_MODEL // 2), lambda i: (0, i, 0)),
                  tok((TM, TOP_K)),
                  pl.BlockSpec((1, 1, N_MOD * D_MODEL), lambda i: (_mod_row(i), 0, 0)),
                  full((D_MODEL, D_EXPERT)), full((D_MODEL, D_EXPERT)), full((D_EXPERT, D_MODEL)),
                  full((1, D_MODEL))],
        out_specs=out_specs,
        out_shape=out_shape,
        compiler_params=_cparams("arbitrary"),
        name="moe_combine",
    )(x1, h2p, yg, gk, mod_l, ws_gate, ws_up, ws_down, final_norm.reshape(1, D_MODEL))


def _moe(x1, h2p, chosen, gk, ik, mod_l, layer, w_gate, w_up, w_down, ws_gate, ws_up, ws_down, final_norm, final):
    dest, first_blk, n_blk = _route(chosen, ik)
    xs = _sc_dispatch(h2p, dest)
    y = _experts(first_blk[:, 0], n_blk[:, 0], xs, layer, w_gate, w_up, w_down)
    yg = _sc_collect(y, dest.reshape(-1)).reshape(TOP_K, T_ALL, D_MODEL // 2)
    return _combine(x1, h2p, yg, gk.T, mod_l, ws_gate.astype(BF16), ws_up.astype(BF16), ws_down.astype(BF16),
                    final_norm, final)


def kernel(x_prompt, x_sample, cache_a_k, cache_a_v, cache_b_k, cache_b_v, state_d_fwd, state_d_bwd, c, c_ctx, w_ada, b_ada, norm_mix, norm_ffn, w_in_attn, w_out_attn, sink_a, rpb_b, w_in_rec, w_out_rec, conv_w, conv_b, filt_w1, filt_b1, filt_w2, filt_b2, filt_w3, filt_b3, filt_freq, filt_w4, d_skip, lb_fwd, lb_bwd, norm_d, w_router, router_bias, w_gate, w_up, w_down, ws_gate, ws_up, ws_down, final_norm):
    x = (x_prompt.reshape(T_CTX, D_MODEL), x_sample.reshape(T_LAT, D_MODEL))
    cvec = jnp.concatenate([c_ctx[None, :], c], axis=0)
    mod = _ada(cvec, w_ada, b_ada).reshape(DEPTH, CVEC_PAD, 1, N_MOD * D_MODEL)

    new_kv = None
    new_state = None
    for l in range(DEPTH):
        j = l // 2
        final = l == DEPTH - 1
        if l % 2 == 0:
            qkv = _inproj(x, mod[l], norm_mix[l], w_in_attn[j].astype(BF16))
            oa_ctx, ob_ctx, *new_kv = _ctx_attn(qkv, sink_a[j])
            new_kv = tuple(new_kv)
            q_rot, k_rot = _rope(qkv)
            cache = lambda t: t[:, j].reshape(DEC_BATCH, PAST_LEN, -1)
            oa_lat = _win_attn(qkv, q_rot, k_rot, cache(cache_a_k), cache(cache_a_v), sink_a[j])
            ob_lat = _na_attn(qkv, cache(cache_b_k), cache(cache_b_v), _na_rel_rows(rpb_b[j]))
            mix_a = (oa_ctx, oa_lat)
            mix_b = (ob_ctx, ob_lat)
            w_out = w_out_attn[j]
        else:
            u = _inproj(x, mod[l], norm_mix[l], w_in_rec[j].astype(BF16))
            filt = (filt_w1[j], filt_b1[j], filt_w2[j], filt_b2[j], filt_w3[j], filt_b3[j], filt_freq[j],
                    filt_w4[j])
            y_ctx = _hyena(u, 0, BATCH, SEQ, conv_w[j], conv_b[j], d_skip[j], _hyena_filter(SEQ, filt))
            y_lat = _hyena(u, T_CTX // DEC_SEQ, DEC_BATCH, DEC_SEQ, conv_w[j], conv_b[j], d_skip[j],
                           _hyena_filter(DEC_SEQ, filt))
            zeros = jnp.zeros((BATCH, D_HEADS, D_KDIM, D_VDIM), F32)
            o_ctx, s_f, s_b = _hgrn(u, 0, BATCH, SEQ, lb_fwd, lb_bwd, norm_d[j], zeros, zeros, l)
            o_lat, _, _ = _hgrn(u, T_CTX // DEC_SEQ, DEC_BATCH, DEC_SEQ, lb_fwd, lb_bwd, norm_d[j],
                                state_d_fwd[:, j], state_d_bwd[:, j], l)
            new_state = (s_f[:, None], s_b[:, None])
            mix_a = (y_ctx, y_lat)
            mix_b = (o_ctx, o_lat)
            w_out = w_out_rec[j]
        x1, h2p, chosen, gk, ik = _outproj(mix_a, mix_b, x, mod[l], norm_ffn[l], w_out.astype(BF16), w_router[l],
                                           router_bias[l])
        x = _moe(x1, h2p, chosen, gk, ik, mod[l], l, w_gate, w_up, w_down, ws_gate[l], ws_up[l],
                 ws_down[l], final_norm, final)

    y_prompt = x[0].reshape(BATCH, SEQ, D_MODEL)
    y_sample = x[1].reshape(DEC_BATCH, DEC_SEQ, D_MODEL)
    return (y_prompt, y_sample) + new_kv + new_state
```

```python
import functools
import math

import numpy as np
import jax
import jax.numpy as jnp
from jax import lax
from jax.experimental import pallas as pl
from jax.experimental.pallas import tpu as pltpu
from jax.experimental.pallas import tpu_sc as plsc

F32 = jnp.float32
BF16 = jnp.bfloat16
HI = lax.Precision.HIGHEST

D_MODEL = 1024
BATCH = 16
SEQ = 256
DEPTH = 2
DEC_BATCH = 2
DEC_SEQ = 1024
PAST_LEN = 512
GRID_W = 64
HEAD_DIM = 64
N_MOD = 6
RMS_EPS = 1e-6
A_HEADS = 8
A_KV_HEADS = 2
A_GROUP = A_HEADS // A_KV_HEADS
WINDOW = 128
ROPE_BASE = 10000.0
B_HEADS = 8
NA_ROWS = 8
NA_COLS = 16
C_DIM = 512
C_EMB = 33
C_FFN = 64
HYENA_MIN_DECAY = math.log(1e-2) / 1.5
HYENA_MAX_DECAY = math.log(1e-2) / 0.3
D_KDIM = 128
D_VDIM = 128
D_HEADS = 4
N_EXPERTS = 64
TOP_K = 8
D_EXPERT = 256
ROUTE_SCALE = 2.5
A_Q = A_HEADS * HEAD_DIM
A_KV = A_KV_HEADS * HEAD_DIM
B_W = B_HEADS * HEAD_DIM
ATTN_IN = A_Q + 2 * A_KV + 3 * B_W
REC_IN = 3 * C_DIM + 5 * D_HEADS * D_KDIM

T_CTX = BATCH * SEQ
T_LAT = DEC_BATCH * DEC_SEQ
T_ALL = T_CTX + T_LAT
N_CVEC = 1 + DEC_BATCH
CVEC_PAD = 8
TM = 512
MASK_NEG = -1e30
GLA_CHUNK = 64
GLA_SPAN = 256
HGRN_HEADS_PER_STEP = 4
DFT_CHUNK = 256
MOE_BLK = 512
MOE_NBLK = -(-(T_ALL * TOP_K + N_EXPERTS * (MOE_BLK - 1)) // MOE_BLK)
MOE_ROWS = MOE_NBLK * MOE_BLK
SC_CORES = 2
SC_SUBCORES = 16
SC_WORKERS = SC_CORES * SC_SUBCORES
DISP_CHUNK = 128
DISP_SPLIT = 2
COLLECT_CHUNK = 64
VMEM_LIMIT = 56 * 1024 * 1024


def _cparams(*sem):
    return pltpu.CompilerParams(dimension_semantics=sem, vmem_limit_bytes=VMEM_LIMIT)


def _mod_row(i):
    return jnp.where(i < T_CTX // TM, 0, 1 + (i - T_CTX // TM) // (DEC_SEQ // TM))


def _dot(a, b):
    return jnp.dot(a.astype(BF16), b.astype(BF16), preferred_element_type=F32)


def _dot_nt(a, b):
    return lax.dot_general(a.astype(BF16), b.astype(BF16), (((1,), (1,)), ((), ())),
                           preferred_element_type=F32)


def _dot_tn(a, b):
    return lax.dot_general(a.astype(BF16), b.astype(BF16), (((0,), (0,)), ((), ())),
                           preferred_element_type=F32)


def _dot_hi(a, b):
    return jnp.dot(a, b, precision=HI, preferred_element_type=F32)


def _split_bf16(x):
    hi = x.astype(BF16)
    return hi, (x - hi.astype(F32)).astype(BF16)


def _dot_split(a, b):
    a_hi, a_lo = _split_bf16(a)
    b_hi, b_lo = _split_bf16(b)
    dot = lambda x, y: jnp.dot(x, y, preferred_element_type=F32)
    return dot(a_hi, b_hi) + dot(a_hi, b_lo) + dot(a_lo, b_hi)


def _silu(x):
    return x * jax.nn.sigmoid(x)


def _rms(x, g):
    return x * lax.rsqrt(jnp.mean(x * x, axis=-1, keepdims=True) + RMS_EPS) * g


ADA_TN = 1536
ADA_UNROLL = 4


def _ada_body(cb_ref, w_ref, b_ref, o_ref):
    tn = o_ref.shape[-1]
    n_slab = tn // LANES

    def step(k8, accs):
        r0 = pl.multiple_of(k8 * 8, 8)
        sk = [_silu(cb_ref[j, pl.ds(r0, 8), :]) for j in range(N_CVEC)]
        out = []
        for s in range(n_slab):
            wk = w_ref[0, pl.ds(r0, 8), s * LANES:(s + 1) * LANES]
            out.extend(accs[s * N_CVEC + j] + wk * sk[j] for j in range(N_CVEC))
        return tuple(out)

    accs = lax.fori_loop(0, D_MODEL // 8, step,
                         tuple(jnp.zeros((8, LANES), F32) for _ in range(n_slab * N_CVEC)), unroll=ADA_UNROLL)
    o_ref[0] = jnp.zeros((CVEC_PAD, tn), F32)
    for s in range(n_slab):
        for j in range(N_CVEC):
            o_ref[0, j:j + 1, s * LANES:(s + 1) * LANES] = (
                jnp.sum(accs[s * N_CVEC + j], axis=0, keepdims=True) + b_ref[0, :, s * LANES:(s + 1) * LANES])


def _ada(cvec, w_ada, b_ada):
    n_out = N_MOD * D_MODEL
    c_lanes = jnp.broadcast_to(cvec[:, :, None], (N_CVEC, D_MODEL, LANES))
    return pl.pallas_call(
        _ada_body,
        grid=(DEPTH, n_out // ADA_TN),
        in_specs=[pl.BlockSpec((N_CVEC, D_MODEL, LANES), lambda l, n: (0, 0, 0)),
                  pl.BlockSpec((1, D_MODEL, ADA_TN), lambda l, n: (l, 0, n)),
                  pl.BlockSpec((1, 1, ADA_TN), lambda l, n: (l, 0, n))],
        out_specs=pl.BlockSpec((1, CVEC_PAD, ADA_TN), lambda l, n: (l, 0, n)),
        out_shape=jax.ShapeDtypeStruct((DEPTH, CVEC_PAD, n_out), F32),
        compiler_params=_cparams("parallel", "parallel"),
        name="ada",
    )(c_lanes, w_ada, b_ada.reshape(DEPTH, 1, n_out))


N_CTX_TILES = T_CTX // TM


def _token_specs(x, width):
    if not isinstance(x, tuple):
        return [pl.BlockSpec((TM, width), lambda i: (i, 0))], (x,)
    return ([pl.BlockSpec((TM, width), lambda i: (jnp.minimum(i, N_CTX_TILES - 1), 0)),
             pl.BlockSpec((TM, width), lambda i: (jnp.maximum(i - N_CTX_TILES, 0), 0))], x)


def _token_tile(refs):
    if len(refs) == 1:
        return refs[0][...]
    return jnp.where(pl.program_id(0) < N_CTX_TILES, refs[0][...], refs[1][...])


def _inproj_body(*refs, n_x):
    x_refs, (mod_ref, g_ref, w_ref, o_ref) = refs[:n_x], refs[n_x:]
    m = mod_ref[0]
    h = _rms(_token_tile(x_refs), g_ref[...]) * (1.0 + m[:, D_MODEL:2 * D_MODEL]) + m[:, 0:D_MODEL]
    o_ref[...] = _dot(h, w_ref[...])


def _inproj(x, mod_l, gain, w_bf16):
    n = w_bf16.shape[1]
    x_specs, x_args = _token_specs(x, D_MODEL)
    return pl.pallas_call(
        functools.partial(_inproj_body, n_x=len(x_args)),
        grid=(T_ALL // TM,),
        in_specs=x_specs + [pl.BlockSpec((1, 1, N_MOD * D_MODEL), lambda i: (_mod_row(i), 0, 0)),
                            pl.BlockSpec((1, D_MODEL), lambda i: (0, 0)),
                            pl.BlockSpec((D_MODEL, n), lambda i: (0, 0))],
        out_specs=pl.BlockSpec((TM, n), lambda i: (i, 0)),
        out_shape=jax.ShapeDtypeStruct((T_ALL, n), F32),
        compiler_params=_cparams("parallel"),
        name="inproj",
    )(*x_args, mod_l, gain.reshape(1, D_MODEL), w_bf16)


def _head_cols(h):
    return slice(h * HEAD_DIM, (h + 1) * HEAD_DIM)


def _group_rows(ref, rows, first_col, sink_ref, hk):
    n = rows.stop - rows.start
    q = jnp.concatenate([ref[rows, first_col + g * HEAD_DIM:first_col + (g + 1) * HEAD_DIM]
                         for g in range(A_GROUP)], axis=0)
    sink = jnp.concatenate([jnp.broadcast_to(sink_ref[:, hk * A_GROUP + g:hk * A_GROUP + g + 1], (n, 1))
                            for g in range(A_GROUP)], axis=0)
    return q, sink


def _ctx_attn_body(qkv_ref, sink_ref, oa_ref, ob_ref, ak_ref, av_ref, bk_ref, bv_ref):
    scale = HEAD_DIM ** -0.5
    lane = lax.broadcasted_iota(jnp.int32, (SEQ, LANES), 1)
    in_half = [lane < HEAD_DIM, lane >= HEAD_DIM]

    def attend(q, k, v, sink):
        s = _dot_nt(q, k) * scale
        m = jnp.max(s, axis=-1, keepdims=True)
        if sink is not None:
            m = jnp.maximum(m, sink)
        p = jnp.exp(s - m)
        den = jnp.sum(p, axis=-1, keepdims=True)
        if sink is not None:
            den = den + jnp.exp(sink - m)
        return _dot(p, v) / den

    def tile(first_col, t):
        return qkv_ref[:, first_col + t * LANES:first_col + (t + 1) * LANES]

    base = A_Q + 2 * A_KV
    for hk in range(A_KV_HEADS):
        dst = pl.ds(hk, SEQ, stride=A_KV_HEADS)
        ak_ref[0, dst, :] = qkv_ref[:, A_Q + hk * HEAD_DIM:A_Q + (hk + 1) * HEAD_DIM]
        av_ref[0, dst, :] = qkv_ref[:, A_Q + A_KV + hk * HEAD_DIM:A_Q + A_KV + (hk + 1) * HEAD_DIM]
    for h in range(B_HEADS):
        dst = pl.ds(h, SEQ, stride=B_HEADS)
        bk_ref[0, dst, :] = qkv_ref[:, base + B_W + h * HEAD_DIM:base + B_W + (h + 1) * HEAD_DIM]
        bv_ref[0, dst, :] = qkv_ref[:, base + 2 * B_W + h * HEAD_DIM:base + 2 * B_W + (h + 1) * HEAD_DIM]

    k_t, v_t = tile(A_Q, 0), tile(A_Q + A_KV, 0)
    k_sw, v_sw = pltpu.roll(k_t, HEAD_DIM, axis=1), pltpu.roll(v_t, HEAD_DIM, axis=1)
    tiles_per_kv = A_GROUP // HEADS_PER_TILE
    for hk in range(A_KV_HEADS):
        q_tiles = [tile(0, hk * tiles_per_kv + j) for j in range(tiles_per_kv)]
        halves = []
        for p in range(HEADS_PER_TILE):
            q = jnp.concatenate([jnp.where(in_half[p], qt, 0.0) for qt in q_tiles], axis=0)
            heads = [(hk * tiles_per_kv + j) * HEADS_PER_TILE + p for j in range(tiles_per_kv)]
            sink = jnp.concatenate([jnp.broadcast_to(sink_ref[:, h:h + 1], (SEQ, 1)) for h in heads], axis=0)
            halves.append(attend(q, k_t if p == hk else k_sw, v_t if p == hk else v_sw, sink))
        first_half = lax.broadcasted_iota(jnp.int32, halves[0].shape, 1) < HEAD_DIM
        o = jnp.where(first_half, halves[0], halves[1])
        for j in range(tiles_per_kv):
            t = hk * tiles_per_kv + j
            oa_ref[:, t * LANES:(t + 1) * LANES] = o[j * SEQ:(j + 1) * SEQ]

    for t in range(B_HEADS // HEADS_PER_TILE):
        q_t, k_b, v_b = tile(base, t), tile(base + B_W, t), tile(base + 2 * B_W, t)
        halves = [attend(jnp.where(in_half[p], q_t, 0.0), k_b, v_b, None) for p in range(HEADS_PER_TILE)]
        ob_ref[:, t * LANES:(t + 1) * LANES] = jnp.where(in_half[0], halves[0], halves[1])


def _ctx_attn(qkv, sink):
    kv_spec = lambda heads: pl.BlockSpec((1, SEQ * heads, HEAD_DIM), lambda b: (b, 0, 0))
    kv_sd = lambda heads: jax.ShapeDtypeStruct((BATCH, SEQ * heads, HEAD_DIM), F32)
    outs = pl.pallas_call(
        _ctx_attn_body,
        grid=(BATCH,),
        in_specs=[pl.BlockSpec((SEQ, ATTN_IN), lambda b: (b, 0)),
                  pl.BlockSpec((1, A_HEADS), lambda b: (0, 0))],
        out_specs=[pl.BlockSpec((SEQ, A_Q), lambda b: (b, 0)), pl.BlockSpec((SEQ, B_W), lambda b: (b, 0)),
                   kv_spec(A_KV_HEADS), kv_spec(A_KV_HEADS), kv_spec(B_HEADS), kv_spec(B_HEADS)],
        out_shape=[jax.ShapeDtypeStruct((T_CTX, A_Q), F32), jax.ShapeDtypeStruct((T_CTX, B_W), F32),
                   kv_sd(A_KV_HEADS), kv_sd(A_KV_HEADS), kv_sd(B_HEADS), kv_sd(B_HEADS)],
        compiler_params=_cparams("parallel"),
        name="ctx_attn",
    )(qkv, sink.reshape(1, A_HEADS))
    caches = [t.reshape(BATCH, 1, SEQ, -1, HEAD_DIM) for t in outs[2:]]
    return outs[0], outs[1], *caches


@functools.lru_cache(maxsize=None)
def _rope_tables(width):
    half = HEAD_DIM // 2
    t = np.arange(DEC_SEQ)
    inv = ROPE_BASE ** (-np.arange(0, half, 2, dtype=np.float64) / half)
    ang_r = (t // GRID_W)[:, None] * inv[None, :]
    ang_c = (t % GRID_W)[:, None] * inv[None, :]
    cos = np.concatenate([np.cos(ang_r)] * 2 + [np.cos(ang_c)] * 2, axis=-1)
    sin = np.concatenate([-np.sin(ang_r), np.sin(ang_r), -np.sin(ang_c), np.sin(ang_c)], axis=-1)
    reps = width // HEAD_DIM
    return (np.tile(cos, (1, reps)).astype(np.float32), np.tile(sin, (1, reps)).astype(np.float32))


def _rope_body(q_ref, k_ref, cq_ref, sq_ref, ck_ref, sk_ref, qo_ref, ko_ref):
    quarter = HEAD_DIM // 4

    def rot(x, cos, sin):
        w = x.shape[-1]
        lane = lax.broadcasted_iota(jnp.int32, x.shape, 1)
        fwd = pltpu.roll(x, w - quarter, axis=1)
        bwd = pltpu.roll(x, quarter, axis=1)
        partner = jnp.where((lane & (2 * quarter - 1)) < quarter, fwd, bwd)
        return x * cos + partner * sin

    qo_ref[...] = rot(q_ref[...], cq_ref[...], sq_ref[...])
    ko_ref[...] = rot(k_ref[...], ck_ref[...], sk_ref[...])


def _rope(qkv):
    cq, sq = _rope_tables(A_Q)
    ck, sk = _rope_tables(A_KV)
    tab = lambda w: pl.BlockSpec((DEC_SEQ, w), lambda b: (0, 0))
    row0 = T_CTX // DEC_SEQ
    return pl.pallas_call(
        _rope_body,
        grid=(DEC_BATCH,),
        in_specs=[pl.BlockSpec((DEC_SEQ, A_Q), lambda b: (row0 + b, 0)),
                  pl.BlockSpec((DEC_SEQ, A_KV), lambda b: (row0 + b, A_Q // A_KV)),
                  tab(A_Q), tab(A_Q), tab(A_KV), tab(A_KV)],
        out_specs=[pl.BlockSpec((DEC_SEQ, A_Q), lambda b: (b, 0)),
                   pl.BlockSpec((DEC_SEQ, A_KV), lambda b: (b, 0))],
        out_shape=[jax.ShapeDtypeStruct((T_LAT, A_Q), F32), jax.ShapeDtypeStruct((T_LAT, A_KV), F32)],
        compiler_params=_cparams("parallel"),
        name="rope",
    )(qkv, qkv, jnp.asarray(cq), jnp.asarray(sq), jnp.asarray(ck), jnp.asarray(sk))


WIN_QB = 256


def _pick_head(x, h, n_heads):
    out = x[:, _head_cols(0)]
    for i in range(1, n_heads):
        out = jnp.where(h == i, x[:, _head_cols(i)], out)
    return out


def _win_attn_body(qraw_ref, qrot_ref, krot_ref, v_ref, kc_ref, vc_ref, sink_ref, o_ref):
    scale = HEAD_DIM ** -0.5
    hk = pl.program_id(1)
    tiles = A_GROUP // HEADS_PER_TILE

    def kv_in_half(x):
        swapped = pltpu.roll(x, HEAD_DIM, axis=1)
        return [jnp.where(hk == p, x, swapped) for p in range(HEADS_PER_TILE)]

    k, v, kc, vc = kv_in_half(krot_ref[...]), kv_in_half(v_ref[...]), kv_in_half(kc_ref[0]), kv_in_half(vc_ref[0])
    head_lane = lax.broadcasted_iota(jnp.int32, (1, A_HEADS), 1)

    def sink_rows(p):
        heads = [hk * A_GROUP + j * HEADS_PER_TILE + p for j in range(tiles)]
        vals = [jnp.sum(jnp.where(head_lane == h, sink_ref[...], 0.0), axis=-1, keepdims=True) for h in heads]
        return jnp.concatenate([jnp.broadcast_to(s, (WIN_QB, 1)) for s in vals], axis=0)

    sinks = [sink_rows(p) for p in range(HEADS_PER_TILE)]
    lane = lax.broadcasted_iota(jnp.int32, (tiles * WIN_QB, LANES), 1)
    in_half = [lane < HEAD_DIM, lane >= HEAD_DIM]
    for qb in range(DEC_SEQ // WIN_QB):
        q0 = qb * WIN_QB
        rows = slice(q0, q0 + WIN_QB)
        lo = max(0, q0 - WINDOW)
        hi = min(DEC_SEQ, q0 + WIN_QB + WINDOW)
        q_rot = jnp.concatenate([qrot_ref[rows, j * LANES:(j + 1) * LANES] for j in range(tiles)], axis=0)
        q_raw = jnp.concatenate([qraw_ref[rows, j * LANES:(j + 1) * LANES] for j in range(tiles)], axis=0)
        halves = []
        for p in range(HEADS_PER_TILE):
            s_loc = _dot_nt(jnp.where(in_half[p], q_rot, 0.0), k[p][lo:hi]) * scale
            qpos = q0 + (lax.broadcasted_iota(jnp.int32, s_loc.shape, 0) & (WIN_QB - 1))
            kpos = lo + lax.broadcasted_iota(jnp.int32, s_loc.shape, 1)
            s_loc = jnp.where(jnp.abs(kpos - qpos) <= WINDOW, s_loc, MASK_NEG)
            s_ctx = _dot_nt(jnp.where(in_half[p], q_raw, 0.0), kc[p]) * scale
            m = jnp.maximum(jnp.maximum(jnp.max(s_loc, axis=-1, keepdims=True),
                                        jnp.max(s_ctx, axis=-1, keepdims=True)), sinks[p])
            p_loc = jnp.exp(s_loc - m)
            p_ctx = jnp.exp(s_ctx - m)
            den = (jnp.sum(p_loc, axis=-1, keepdims=True) + jnp.sum(p_ctx, axis=-1, keepdims=True)
                   + jnp.exp(sinks[p] - m))
            halves.append((_dot(p_ctx, vc[p]) + _dot(p_loc, v[p][lo:hi])) / den)
        o = jnp.where(in_half[0], halves[0], halves[1])
        for j in range(tiles):
            o_ref[rows, j * LANES:(j + 1) * LANES] = o[j * WIN_QB:(j + 1) * WIN_QB]


def _win_attn(qkv, q_rot, k_rot, kc, vc, sink):
    row0 = T_CTX // DEC_SEQ
    gw = A_GROUP * HEAD_DIM
    return pl.pallas_call(
        _win_attn_body,
        grid=(DEC_BATCH, A_KV_HEADS),
        in_specs=[pl.BlockSpec((DEC_SEQ, gw), lambda b, h: (row0 + b, h)),
                  pl.BlockSpec((DEC_SEQ, gw), lambda b, h: (b, h)),
                  pl.BlockSpec((DEC_SEQ, A_KV), lambda b, h: (b, 0)),
                  pl.BlockSpec((DEC_SEQ, A_KV), lambda b, h: (row0 + b, (A_Q + A_KV) // A_KV)),
                  pl.BlockSpec((1, PAST_LEN, A_KV), lambda b, h: (b, 0, 0)),
                  pl.BlockSpec((1, PAST_LEN, A_KV), lambda b, h: (b, 0, 0)),
                  pl.BlockSpec((1, A_HEADS), lambda b, h: (0, 0))],
        out_specs=pl.BlockSpec((DEC_SEQ, gw), lambda b, h: (b, h)),
        out_shape=jax.ShapeDtypeStruct((T_LAT, A_Q), F32),
        compiler_params=_cparams("parallel", "parallel"),
        name="win_attn",
    )(qkv, q_rot, k_rot, qkv, kc, vc, sink.reshape(1, A_HEADS))


GRID_ROWS = DEC_SEQ // GRID_W
NA_BAND = min(NA_ROWS, GRID_ROWS)


NA_REL_ROWS = 2 * NA_ROWS - 1
NA_REL_COLS = 2 * NA_COLS - 1
LANES = 128
HEADS_PER_TILE = LANES // HEAD_DIM


def _na_rel_rows(rpb):
    pad = jnp.zeros((B_HEADS, NA_REL_ROWS, GRID_W - NA_REL_COLS), F32)
    one = jnp.concatenate([rpb, pad], axis=-1)
    nxt = jnp.concatenate([one[:, 1:], jnp.zeros((B_HEADS, 1, GRID_W), F32)], axis=1)
    both = jnp.concatenate([one, nxt], axis=-1)
    return jnp.concatenate([both, jnp.zeros((B_HEADS, 16 - NA_REL_ROWS, LANES), F32)], axis=1)


NA_HEADS_PER_STEP = LANES // HEAD_DIM


def _na_row_groups():
    groups = []
    for r in range(GRID_ROWS):
        rs = min(max(r - NA_ROWS // 2, 0), GRID_ROWS - NA_BAND)
        if groups and groups[-1][2] == rs:
            groups[-1][1] += 1
        else:
            groups.append([r, 1, rs])
    return groups


def _na_attn_body(q_ref, k_ref, v_ref, kc_ref, vc_ref, rel_ref, o_ref):
    scale = HEAD_DIM ** -0.5
    cq = lax.broadcasted_iota(jnp.int32, (GRID_W, LANES), 0)
    kcol = lax.broadcasted_iota(jnp.int32, (GRID_W, LANES), 1) & (GRID_W - 1)
    cs = jnp.clip(cq - NA_COLS // 2, 0, GRID_W - NA_COLS)
    col_ok = (kcol >= cs) & (kcol < cs + NA_COLS)
    kc = kc_ref[0]
    vc = vc_ref[0]
    tiles = {}

    def pair_tile(hh, a):
        if (hh, a) not in tiles:
            x = jnp.broadcast_to(rel_ref[hh, a:a + 1, :], (GRID_W, LANES))
            t = pltpu.roll(x, LANES - (NA_COLS - 1), axis=1, stride=1, stride_axis=0)
            tiles[hh, a] = jnp.where(col_ok, t, MASK_NEG)
        return tiles[hh, a]

    for r0, n_r, rs in _na_row_groups():
        rows = slice(r0 * GRID_W, (r0 + n_r) * GRID_W)
        band = slice(rs * GRID_W, (rs + NA_BAND) * GRID_W)
        q_t, k_t, v_t = q_ref[rows, :], k_ref[band, :], v_ref[band, :]
        head_of_lane = lax.broadcasted_iota(jnp.int32, q_t.shape, 1) >> (HEAD_DIM.bit_length() - 1)
        o = jnp.zeros(q_t.shape, F32)
        for hh in range(NA_HEADS_PER_STEP):
            bias = jnp.concatenate(
                [jnp.concatenate([pair_tile(hh, rs - r + NA_ROWS - 1 + 2 * i) for i in range(NA_BAND // 2)], axis=1)
                 for r in range(r0, r0 + n_r)], axis=0)
            q = jnp.where(head_of_lane == hh, q_t, 0.0)
            s_loc = _dot_nt(q, k_t) * scale + bias
            s_ctx = _dot_nt(q, kc) * scale
            m = jnp.maximum(jnp.max(s_loc, axis=-1, keepdims=True), jnp.max(s_ctx, axis=-1, keepdims=True))
            p_loc = jnp.exp(s_loc - m)
            p_ctx = jnp.exp(s_ctx - m)
            den = jnp.sum(p_loc, axis=-1, keepdims=True) + jnp.sum(p_ctx, axis=-1, keepdims=True)
            o = jnp.where(head_of_lane == hh, (_dot(p_ctx, vc) + _dot(p_loc, v_t)) / den, o)
        o_ref[rows, :] = o


def _na_attn(qkv, kc, vc, rel):
    row0 = T_CTX // DEC_SEQ
    col0 = (A_Q + 2 * A_KV) // LANES
    n_blk = B_W // LANES
    col = lambda j: pl.BlockSpec((DEC_SEQ, LANES), lambda b, p: (row0 + b, col0 + j * n_blk + p))
    cache = pl.BlockSpec((1, PAST_LEN, LANES), lambda b, p: (b, 0, p))
    return pl.pallas_call(
        _na_attn_body,
        grid=(DEC_BATCH, n_blk),
        in_specs=[col(0), col(1), col(2), cache, cache,
                  pl.BlockSpec((NA_HEADS_PER_STEP, 16, LANES), lambda b, p: (p, 0, 0))],
        out_specs=pl.BlockSpec((DEC_SEQ, LANES), lambda b, p: (b, p)),
        out_shape=jax.ShapeDtypeStruct((T_LAT, B_W), F32),
        compiler_params=_cparams("parallel", "parallel"),
        name="na_attn",
    )(qkv, qkv, qkv, kc, vc, rel)


@functools.lru_cache(maxsize=None)
def _dft_mats(L):
    n = 2 * L
    fc = min(L, DFT_CHUNK)
    f = np.arange(L)[:, None]
    t = np.arange(L)[None, :]
    ang = 2.0 * np.pi * ((f * t) % n) / n
    m1 = np.cos(ang)
    m2 = np.sin(ang)
    m2[0, :] = np.where(np.arange(L) % 2 == 0, 1.0, -1.0)
    wgt = np.full((L, 1), 2.0)
    wgt[0, 0] = 1.0
    nch = L // fc
    fwd = np.concatenate([m1.reshape(nch, fc, L), m2.reshape(nch, fc, L)], axis=1)
    inv = np.concatenate([(m1 * wgt / n).reshape(nch, fc, L), (m2 * wgt / n).reshape(nch, fc, L)], axis=1)
    inv = np.transpose(inv, (0, 2, 1))
    return fwd.astype(np.float32), inv.astype(np.float32)


@functools.lru_cache(maxsize=None)
def _filter_consts(L):
    t = np.linspace(0.0, 1.0, L)[:, None]
    bands = (C_EMB - 1) // 2
    ang = (2.0 * math.pi / L) * np.arange(L)[:, None] * np.linspace(1e-4, bands - 1, bands)[None, :]
    z = np.concatenate([t, np.cos(ang), -np.sin(ang)], axis=-1)
    zpad = np.zeros((L, 128))
    zpad[:, :C_EMB] = z
    deltas = np.abs(np.linspace(HYENA_MIN_DECAY, HYENA_MAX_DECAY, C_DIM))
    window = np.exp(-t * deltas[None, :])
    return zpad.astype(np.float32), window.astype(np.float32)


def _filter_body(z_ref, w1_ref, b1_ref, w2_ref, b2_ref, w3_ref, b3_ref, fr_ref, w4_ref, win_ref, fm_ref,
                 hr_ref, g_ref, hq_ref, hs_scr, hd_scr):
    c = pl.program_id(0)
    fc = hr_ref.shape[0]

    @pl.when(c == 0)
    def _():
        fr = fr_ref[...]
        hh = jnp.sin(fr * (_dot_hi(z_ref[...], w1_ref[...]) + b1_ref[...]))
        hh = jnp.sin(fr * (_dot_hi(hh, w2_ref[...]) + b2_ref[...]))
        hh = jnp.sin(fr * (_dot_hi(hh, w3_ref[...]) + b3_ref[...]))
        hh = _dot_hi(hh, w4_ref[...])
        hf = hh[:, :C_DIM] * win_ref[...]
        hb = hh[:, C_DIM:] * win_ref[...]
        hs_scr[...] = hf + hb
        hd_scr[...] = hf - hb

    fm = fm_ref[0]
    hr = _dot_split(fm[:fc], hs_scr[...])
    first = (lax.broadcasted_iota(jnp.int32, (fc, C_DIM), 0) == 0) & (c == 0)
    hr_ref[...] = hr
    g_ref[...] = jnp.where(first, 0.0, _dot_split(fm[fc:], hd_scr[...]))
    hs = hs_scr[...]
    sign = jnp.where((lax.broadcasted_iota(jnp.int32, hs.shape, 0) & 1) == 0, 1.0, -1.0)
    hq_ref[...] = jnp.where(first, jnp.sum(hs * sign, axis=0, keepdims=True), hr)


def _hyena_filter(L, filt):
    w1, b1, w2, b2, w3, b3, freq, w4 = filt
    zpad, window = _filter_consts(L)
    fwd, _ = _dft_mats(L)
    nch, fc2, _ = fwd.shape
    fc = fc2 // 2
    w1p = jnp.pad(w1, ((0, 128 - C_EMB), (0, 0)))
    full = lambda shape: pl.BlockSpec(shape, lambda c: tuple(0 for _ in shape))
    out_spec = pl.BlockSpec((fc, C_DIM), lambda c: (c, 0))
    out_sd = jax.ShapeDtypeStruct((L, C_DIM), F32)
    return pl.pallas_call(
        _filter_body,
        grid=(nch,),
        in_specs=[full((L, 128)), full((128, C_FFN)), full((1, C_FFN)), full((C_FFN, C_FFN)), full((1, C_FFN)),
                  full((C_FFN, C_FFN)), full((1, C_FFN)), full((1, C_FFN)), full((C_FFN, 2 * C_DIM)),
                  full((L, C_DIM)), pl.BlockSpec((1, fc2, L), lambda c: (c, 0, 0))],
        out_specs=[out_spec, out_spec, out_spec],
        out_shape=[out_sd, out_sd, out_sd],
        scratch_shapes=[pltpu.VMEM((L, C_DIM), F32), pltpu.VMEM((L, C_DIM), F32)],
        compiler_params=_cparams("arbitrary"),
        name="hyena_filter",
    )(jnp.asarray(zpad), w1p, b1.reshape(1, C_FFN), w2, b2.reshape(1, C_FFN), w3, b3.reshape(1, C_FFN),
      freq.reshape(1, C_FFN), w4, jnp.asarray(window), jnp.asarray(fwd))


def _hyena_body(u_ref, cw_ref, cb_ref, d_ref, fm_ref, fi_ref, hr_ref, g_ref, hq_ref, y_ref,
                x0_scr, z_scr, acc_scr):
    c = pl.program_id(1)
    L = y_ref.shape[0]
    fc = hr_ref.shape[0]

    @pl.when(c == 0)
    def _():
        row = lax.broadcasted_iota(jnp.int32, (L, C_DIM), 0)

        def short_conv(sec):
            cols = slice(sec * C_DIM, (sec + 1) * C_DIM)
            u = u_ref[:, cols]
            prev = jnp.where(row == 0, 0.0, pltpu.roll(u, 1, axis=0))
            nxt = jnp.where(row == L - 1, 0.0, pltpu.roll(u, L - 1, axis=0))
            return (prev * cw_ref[0:1, cols] + u * cw_ref[1:2, cols] + nxt * cw_ref[2:3, cols]
                    + cb_ref[:, cols])

        x0_scr[...] = short_conv(0)
        z_scr[...] = short_conv(1) * short_conv(2)
        acc_scr[...] = jnp.zeros((L, C_DIM), F32)

    ab = _dot_split(fm_ref[0], z_scr[...])
    a, b = ab[:fc], ab[fc:]
    hr, g, hq = hr_ref[...], g_ref[...], hq_ref[...]
    pq = jnp.concatenate([a * hr - b * g, a * g + b * hq], axis=0)
    acc_scr[...] += _dot_split(fi_ref[0], pq)

    @pl.when(c == pl.num_programs(1) - 1)
    def _():
        y_ref[...] = x0_scr[...] * (acc_scr[...] + z_scr[...] * d_ref[...])


def _hyena(u, row_blk0, n_seq, L, conv_w, conv_b, d_skip, spec):
    hr, g, hq = spec
    fwd, inv = _dft_mats(L)
    nch, fc2, _ = fwd.shape
    fc = fc2 // 2
    u_w = 3 * C_DIM
    return pl.pallas_call(
        _hyena_body,
        grid=(n_seq, nch),
        in_specs=[pl.BlockSpec((L, u_w), lambda b, c: (row_blk0 + b, 0)),
                  pl.BlockSpec((3, u_w), lambda b, c: (0, 0)),
                  pl.BlockSpec((1, u_w), lambda b, c: (0, 0)),
                  pl.BlockSpec((1, C_DIM), lambda b, c: (0, 0)),
                  pl.BlockSpec((1, fc2, L), lambda b, c: (c, 0, 0)),
                  pl.BlockSpec((1, L, fc2), lambda b, c: (c, 0, 0)),
                  pl.BlockSpec((fc, C_DIM), lambda b, c: (c, 0)),
                  pl.BlockSpec((fc, C_DIM), lambda b, c: (c, 0)),
                  pl.BlockSpec((fc, C_DIM), lambda b, c: (c, 0))],
        out_specs=pl.BlockSpec((L, C_DIM), lambda b, c: (b, 0)),
        out_shape=jax.ShapeDtypeStruct((n_seq * L, C_DIM), F32),
        scratch_shapes=[pltpu.VMEM((L, C_DIM), F32)] * 3,
        compiler_params=_cparams("parallel", "arbitrary"),
        name="hyena",
    )(u, conv_w, conv_b.reshape(1, u_w), d_skip.reshape(1, C_DIM), jnp.asarray(fwd), jnp.asarray(inv), hr, g, hq)


def _hgrn_body(q_ref, ff_ref, fb_ref, i_ref, g_ref, lbf_ref, lbb_ref, nd_ref, s0f_ref, s0b_ref,
               o_ref, sf_ref, sb_ref, *, layer):
    L = o_ref.shape[0]
    C = GLA_CHUNK
    S = min(L, GLA_SPAN)
    nc = S // C
    n_span = L // S
    mid = C // 2
    def lower_bound(gm):
        e = jnp.exp(gm - jnp.max(gm, axis=0, keepdims=True))
        p = e / jnp.sum(e, axis=0, keepdims=True)
        return jnp.sum(p[0:layer + 1], axis=0, keepdims=True) - p[0:1]

    def gates(fx, lb):
        f = lb + (1.0 - lb) * jax.nn.sigmoid(fx)
        return 1.0 - f, jnp.log(f)


    chunk_shift = C.bit_length() - 1
    block_shift = D_KDIM.bit_length() - 1
    ti = lax.broadcasted_iota(jnp.int32, (S, S), 0)
    si = lax.broadcasted_iota(jnp.int32, (S, S), 1)
    same_chunk = (ti >> chunk_shift) == (si >> chunk_shift)
    causal = same_chunk & (si <= ti)
    anti = same_chunk & (si >= ti)
    row_chunk = lax.broadcasted_iota(jnp.int32, (S, nc * D_KDIM), 0) >> chunk_shift
    col_chunk = lax.broadcasted_iota(jnp.int32, (S, nc * D_KDIM), 1) >> block_shift
    own_block = row_chunk == col_chunk

    def spread(x):
        return jnp.where(own_block, jnp.concatenate([x] * nc, axis=1), 0.0)

    def chunk_cumsum(mask, lg):
        tri = mask.astype(BF16)
        hi = lg.astype(BF16)
        r1 = lg - hi.astype(F32)
        mid_t = r1.astype(BF16)
        lo = (r1 - mid_t.astype(F32)).astype(BF16)
        dot = lambda t: jnp.dot(tri, t, preferred_element_type=F32)
        return dot(hi) + dot(mid_t) + dot(lo)

    def per_chunk_rows(b, pos):
        return jnp.concatenate([jnp.broadcast_to(b[n * C + pos:n * C + pos + 1], (C, D_KDIM)) for n in range(nc)],
                               axis=0)

    def direction(q, k, v, lg, st, forward):
        outs = [None] * n_span
        mask = causal if forward else anti
        last = C - 1 if forward else 0
        for u in (range(n_span) if forward else reversed(range(n_span))):
            rows = slice(u * S, (u + 1) * S)
            qs, ks, vs = q[rows], k[rows], v[rows]
            b = chunk_cumsum(mask, lg[rows])
            btot = per_chunk_rows(b, last)
            ref = per_chunk_rows(b, mid)
            sc = jnp.where(mask, _dot_nt(qs * jnp.exp(b - ref), ks * jnp.exp(ref - b)), 0.0)
            kv_t = _dot_tn(spread(vs), ks * jnp.exp(btot - b))
            states = [None] * nc
            for n in (range(nc) if forward else reversed(range(nc))):
                states[n] = st
                decay = jnp.exp(b[n * C + last:n * C + last + 1])
                st = st * decay + kv_t[n * D_VDIM:(n + 1) * D_VDIM]
            inter = _dot_nt(spread(qs * jnp.exp(b)), jnp.concatenate(states, axis=1))
            outs[u] = _dot(sc, vs) + inter
        return outs, st

    for hh in range(o_ref.shape[1] // D_VDIM):
        cols = slice(hh * D_KDIM, (hh + 1) * D_KDIM)
        q = _silu(q_ref[:, cols])
        v = i_ref[:, cols]
        kf, lgf = gates(ff_ref[:, cols], lower_bound(lbf_ref[:, cols]))
        kb, lgb = gates(fb_ref[:, cols], lower_bound(lbb_ref[:, cols]))
        o_f, st_f = direction(q, kf, v, lgf, jnp.transpose(s0f_ref[0, hh]), True)
        o_b, st_b = direction(q, kb, v, lgb, jnp.transpose(s0b_ref[0, hh]), False)
        sf_ref[0, hh] = jnp.transpose(st_f)
        sb_ref[0, hh] = jnp.transpose(st_b)
        o = jnp.concatenate([f + b for f, b in zip(o_f, o_b)], axis=0) if n_span > 1 else o_f[0] + o_b[0]
        o_ref[:, cols] = _rms(o, nd_ref[...]) * _silu(g_ref[:, cols])


def _hgrn(u, row_blk0, n_seq, L, lb_fwd, lb_bwd, norm_d, s0f, s0b, layer):
    hps = HGRN_HEADS_PER_STEP
    width = hps * D_KDIM
    col0 = 3 * C_DIM // width
    groups = D_HEADS // hps
    col = lambda j: pl.BlockSpec((L, width), lambda b, h: (row_blk0 + b, col0 + j * groups + h))
    lbs = pl.BlockSpec((DEPTH, width), lambda b, h: (0, h))
    st = pl.BlockSpec((1, hps, D_KDIM, D_VDIM), lambda b, h: (b, h, 0, 0))
    st_sd = jax.ShapeDtypeStruct((n_seq, D_HEADS, D_KDIM, D_VDIM), F32)
    return pl.pallas_call(
        functools.partial(_hgrn_body, layer=layer),
        grid=(n_seq, groups),
        in_specs=[col(0), col(1), col(2), col(3), col(4), lbs, lbs,
                  pl.BlockSpec((1, D_VDIM), lambda b, h: (0, 0)), st, st],
        out_specs=[pl.BlockSpec((L, width), lambda b, h: (b, h)), st, st],
        out_shape=[jax.ShapeDtypeStruct((n_seq * L, D_HEADS * D_VDIM), F32), st_sd, st_sd],
        compiler_params=_cparams("parallel", "parallel"),
        name="hgrn",
    )(u, u, u, u, u, lb_fwd, lb_bwd, norm_d.reshape(1, D_VDIM), s0f, s0b)


def _pack_bf16_pairs(h):
    n = h.shape[1] // 2
    hi = lax.bitcast_convert_type(h[:, :n].astype(BF16).astype(F32), jnp.int32)
    lo = lax.bitcast_convert_type(h[:, n:].astype(BF16).astype(F32), jnp.int32)
    return hi | lax.shift_right_logical(lo, 16)


def _unpack_bf16_pairs(p):
    hi = lax.bitcast_convert_type(p & jnp.int32(-65536), F32).astype(BF16)
    lo = lax.bitcast_convert_type(lax.shift_left(p, 16), F32).astype(BF16)
    return hi, lo


def _outproj_body(*refs, n_x):
    a_refs, b_refs, x_refs = refs[0:2], refs[2:4], refs[4:4 + n_x]
    mod_ref, gf_ref, w_ref, wrh_ref, wrl_ref, rb_ref, x1_ref, h2_ref, chosen_ref, gk_ref, ik_ref = refs[4 + n_x:]
    m = mod_ref[0]
    half = a_refs[0].shape[1]
    out = _dot(_token_tile(a_refs), w_ref[0:half, :]) + _dot(_token_tile(b_refs), w_ref[half:, :])
    x1 = _token_tile(x_refs) + m[:, 2 * D_MODEL:3 * D_MODEL] * out
    x1_ref[...] = x1
    h2 = _rms(x1, gf_ref[...]) * (1.0 + m[:, 4 * D_MODEL:5 * D_MODEL]) + m[:, 3 * D_MODEL:4 * D_MODEL]
    h2_ref[...] = _pack_bf16_pairs(h2)
    h_hi = h2.astype(BF16)
    h_lo = (h2 - h_hi.astype(F32)).astype(BF16)
    logits = _dot_nt(wrh_ref[...], h_hi) + _dot_nt(wrh_ref[...], h_lo) + _dot_nt(wrl_ref[...], h_hi)
    scores = jax.nn.sigmoid(logits)
    work = scores + rb_ref[...]
    expert = lax.broadcasted_iota(jnp.int32, work.shape, 0).astype(F32)
    slot = lax.broadcasted_iota(jnp.int32, (TOP_K, work.shape[1]), 0)
    chosen = jnp.zeros(work.shape, F32)
    gk = jnp.zeros((TOP_K, work.shape[1]), F32)
    ik = jnp.zeros((TOP_K, work.shape[1]), F32)
    for k in range(TOP_K):
        best = jnp.max(work, axis=0, keepdims=True)
        first = jnp.min(jnp.where(work == best, expert, float(N_EXPERTS)), axis=0, keepdims=True)
        hit = expert == first
        chosen = jnp.where(hit, 1.0, chosen)
        gk = jnp.where(slot == k, jnp.sum(jnp.where(hit, scores, 0.0), axis=0, keepdims=True), gk)
        ik = jnp.where(slot == k, first, ik)
        work = jnp.where(hit, -jnp.inf, work)
    chosen_ref[...] = chosen
    gk_ref[...] = gk / jnp.sum(gk, axis=0, keepdims=True) * ROUTE_SCALE
    ik_ref[...] = ik


def _outproj(a, b, x, mod_l, gain_ffn, w_out_bf16, w_router, router_bias):
    half = a[0].shape[1]
    a_specs, a_args = _token_specs(a, half)
    b_specs, b_args = _token_specs(b, half)
    x_specs, x_args = _token_specs(x, D_MODEL)
    wr_t = w_router.T
    wr_hi = wr_t.astype(BF16)
    wr_lo = (wr_t - wr_hi.astype(F32)).astype(BF16)
    return pl.pallas_call(
        functools.partial(_outproj_body, n_x=len(x_args)),
        grid=(T_ALL // TM,),
        in_specs=a_specs + b_specs + x_specs + [
                  pl.BlockSpec((1, 1, N_MOD * D_MODEL), lambda i: (_mod_row(i), 0, 0)),
                  pl.BlockSpec((1, D_MODEL), lambda i: (0, 0)),
                  pl.BlockSpec((2 * half, D_MODEL), lambda i: (0, 0)),
                  pl.BlockSpec((N_EXPERTS, D_MODEL), lambda i: (0, 0)),
                  pl.BlockSpec((N_EXPERTS, D_MODEL), lambda i: (0, 0)),
                  pl.BlockSpec((N_EXPERTS, 1), lambda i: (0, 0))],
        out_specs=[pl.BlockSpec((TM, D_MODEL), lambda i: (i, 0)),
                   pl.BlockSpec((TM, D_MODEL // 2), lambda i: (i, 0)),
                   pl.BlockSpec((N_EXPERTS, TM), lambda i: (0, i)),
                   pl.BlockSpec((TOP_K, TM), lambda i: (0, i)),
                   pl.BlockSpec((TOP_K, TM), lambda i: (0, i))],
        out_shape=[jax.ShapeDtypeStruct((T_ALL, D_MODEL), F32),
                   jax.ShapeDtypeStruct((T_ALL, D_MODEL // 2), jnp.int32),
                   jax.ShapeDtypeStruct((N_EXPERTS, T_ALL), F32),
                   jax.ShapeDtypeStruct((TOP_K, T_ALL), F32),
                   jax.ShapeDtypeStruct((TOP_K, T_ALL), F32)],
        compiler_params=_cparams("parallel"),
        name="outproj_router",
    )(*a_args, *b_args, *x_args, mod_l, gain_ffn.reshape(1, D_MODEL), w_out_bf16, wr_hi, wr_lo,
      router_bias.reshape(N_EXPERTS, 1))


def _route_body(chosen_ref, ik_ref, dest_ref, first_ref, count_ref, pos_scr):
    n_tiles = T_ALL // TM
    r = lax.broadcasted_iota(jnp.int32, (TM, TM), 0)
    c = lax.broadcasted_iota(jnp.int32, (TM, TM), 1)
    before = (r < c).astype(BF16)

    counts = jnp.zeros((N_EXPERTS, 1), F32)
    for i in range(n_tiles):
        cols = slice(i * TM, (i + 1) * TM)
        m = chosen_ref[:, cols]
        pos_scr[:, cols] = jnp.dot(m.astype(BF16), before, preferred_element_type=F32) + counts
        counts = counts + jnp.sum(m, axis=1, keepdims=True)
    padded = jnp.ceil(counts * (1.0 / MOE_BLK)) * MOE_BLK
    ei = lax.broadcasted_iota(jnp.int32, (N_EXPERTS, N_EXPERTS), 0)
    ej = lax.broadcasted_iota(jnp.int32, (N_EXPERTS, N_EXPERTS), 1)
    end = _dot_hi((ej <= ei).astype(F32), jnp.broadcast_to(padded, (N_EXPERTS, LANES)))[:, 0:1]
    start = end - padded

    expert = lax.broadcasted_iota(jnp.int32, (N_EXPERTS, TM), 0).astype(F32)
    slot = lax.broadcasted_iota(jnp.int32, (TOP_K, TM), 0)
    for i in range(n_tiles):
        cols = slice(i * TM, (i + 1) * TM)
        row_of = pos_scr[:, cols] + start
        ik = ik_ref[:, cols]
        acc = jnp.zeros((TOP_K, TM), F32)
        for k in range(TOP_K):
            pick = jnp.sum(jnp.where(expert == ik[k:k + 1, :], row_of, 0.0), axis=0, keepdims=True)
            acc = jnp.where(slot == k, pick, acc)
        dest_ref[:, cols] = acc.astype(jnp.int32)
    first_ref[...] = jnp.broadcast_to(start * (1.0 / MOE_BLK), (N_EXPERTS, LANES)).astype(jnp.int32)
    count_ref[...] = jnp.broadcast_to(padded * (1.0 / MOE_BLK), (N_EXPERTS, LANES)).astype(jnp.int32)


def _route(chosen, ik):
    full = lambda shape: pl.BlockSpec(shape, lambda i: (0, 0))
    return pl.pallas_call(
        _route_body,
        grid=(1,),
        in_specs=[full((N_EXPERTS, T_ALL)), full((TOP_K, T_ALL))],
        out_specs=[full((TOP_K, T_ALL)), full((N_EXPERTS, LANES)), full((N_EXPERTS, LANES))],
        out_shape=[jax.ShapeDtypeStruct((TOP_K, T_ALL), jnp.int32),
                   jax.ShapeDtypeStruct((N_EXPERTS, LANES), jnp.int32),
                   jax.ShapeDtypeStruct((N_EXPERTS, LANES), jnp.int32)],
        scratch_shapes=[pltpu.VMEM((N_EXPERTS, T_ALL), F32)],
        compiler_params=_cparams("arbitrary"),
        name="moe_route",
    )(chosen, ik)


def _sc_worker_id():
    return lax.axis_index("s") * SC_CORES + lax.axis_index("c")


def _sc_dispatch(h2p, dest):
    n_chunks = T_ALL // DISP_CHUNK
    k_per = TOP_K // DISP_SPLIT
    items_per_worker = n_chunks * DISP_SPLIT // SC_WORKERS
    chunk_stride = SC_WORKERS // DISP_SPLIT
    width = h2p.shape[1]
    mesh = plsc.VectorSubcoreMesh(core_axis_name="c", subcore_axis_name="s")

    @functools.partial(
        pl.kernel, mesh=mesh,
        out_type=jax.ShapeDtypeStruct((MOE_ROWS, width), jnp.int32),
        scratch_types=[pltpu.VMEM((k_per, DISP_CHUNK), jnp.int32), pltpu.VMEM((DISP_CHUNK, width), jnp.int32)],
    )
    def run(x_hbm, dest_hbm, xs_hbm, idx_v, rows_v):
        wid = _sc_worker_id()
        group = wid % DISP_SPLIT
        for i in range(items_per_worker):
            chunk = i * chunk_stride + wid // DISP_SPLIT
            tokens = pl.ds(pl.multiple_of(chunk * DISP_CHUNK, DISP_CHUNK), DISP_CHUNK)
            pltpu.sync_copy(dest_hbm.at[group, :, tokens], idx_v)
            pltpu.sync_copy(x_hbm.at[tokens], rows_v)
            for k in range(k_per):
                pltpu.sync_copy(rows_v, xs_hbm.at[idx_v.at[k]])

    return run(h2p, dest.reshape(DISP_SPLIT, k_per, T_ALL))


def _sc_collect(y, dest_flat):
    per_worker = T_ALL // SC_WORKERS
    n_chunks = per_worker // COLLECT_CHUNK
    n_steps = TOP_K * n_chunks
    width = y.shape[1]
    mesh = plsc.VectorSubcoreMesh(core_axis_name="c", subcore_axis_name="s")

    @functools.partial(
        pl.kernel, mesh=mesh,
        out_type=jax.ShapeDtypeStruct((TOP_K * T_ALL, width), y.dtype),
        scratch_types=[pltpu.VMEM((TOP_K * per_worker,), jnp.int32),
                       pltpu.VMEM((COLLECT_CHUNK, width), y.dtype), pltpu.VMEM((COLLECT_CHUNK, width), y.dtype),
                       pltpu.SemaphoreType.DMA, pltpu.SemaphoreType.DMA],
    )
    def run(y_hbm, dest_hbm, yg_hbm, idx_v, rows0, rows1, sem0, sem1):
        wid = _sc_worker_id()
        bufs = ((rows0, sem0), (rows1, sem1))
        for k in range(TOP_K):
            pltpu.sync_copy(dest_hbm.at[pl.ds(k * T_ALL + wid * per_worker, per_worker)],
                            idx_v.at[pl.ds(k * per_worker, per_worker)])

        def gather(step, buf):
            rows, sem = buf
            idx = idx_v.at[pl.ds(pl.multiple_of(step * COLLECT_CHUNK, 8), COLLECT_CHUNK)]
            return pltpu.make_async_copy(y_hbm.at[idx], rows, sem)

        def out_rows(step):
            off = (step // n_chunks) * T_ALL + wid * per_worker + (step % n_chunks) * COLLECT_CHUNK
            return yg_hbm.at[pl.ds(pl.multiple_of(off, 8), COLLECT_CHUNK)]

        gather(0, bufs[0]).start()

        @pl.loop(0, n_steps, step=2)
        def _(base):
            for j in range(2):
                step = base + j

                @pl.when(step + 1 < n_steps)
                def _():
                    gather(step + 1, bufs[1 - j]).start()

                gather(step, bufs[j]).wait()
                pltpu.sync_copy(bufs[j][0], out_rows(step))

    return run(y, dest_flat)


def _expert_body(first_ref, count_ref, xs_hbm, wg_ref, wu_ref, wd_ref, y_hbm,
                 wg_bf, wu_bf, wd_bf, x_buf, y_buf, in_sem, out_sem):
    e = pl.program_id(0)
    first = first_ref[e]
    count = count_ref[e]
    n_used = first_ref[N_EXPERTS - 1] + count_ref[N_EXPERTS - 1]
    half = D_MODEL // 2
    wg_bf[...] = wg_ref[0, 0].astype(BF16)
    wu_bf[...] = wu_ref[0, 0].astype(BF16)
    wd_bf[...] = wd_ref[0, 0].astype(BF16)

    def part_rows(g, part, n_parts):
        size = MOE_BLK // n_parts
        return pl.ds(pl.multiple_of(g * MOE_BLK + part * size, size), size), pl.ds(part * size, size)

    def in_copies(g):
        slot = g & (EXPERT_SLOTS - 1)
        out = []
        for part in range(EXPERT_IN_PARTS):
            src, dst = part_rows(g, part, EXPERT_IN_PARTS)
            out.append(pltpu.make_async_copy(xs_hbm.at[src], x_buf.at[slot, dst], in_sem.at[slot]))
        return out

    def out_copies(g):
        slot = g & (EXPERT_SLOTS - 1)
        out = []
        for part in range(EXPERT_OUT_PARTS):
            dst, src = part_rows(g, part, EXPERT_OUT_PARTS)
            out.append(pltpu.make_async_copy(y_buf.at[slot, src], y_hbm.at[dst], out_sem.at[slot]))
        return out

    @pl.when((first == 0) & (count > 0))
    def _():
        for ahead in range(EXPERT_SLOTS - 1):
            @pl.when(ahead < n_used)
            def _():
                for cp in in_copies(ahead):
                    cp.start()

    def block(b, carry):
        g = first + b
        slot = g & (EXPERT_SLOTS - 1)
        for cp in in_copies(g):
            cp.wait()

        @pl.when(g + EXPERT_SLOTS - 1 < n_used)
        def _():
            for cp in in_copies(g + EXPERT_SLOTS - 1):
                cp.start()

        @pl.when(g >= EXPERT_SLOTS)
        def _():
            for cp in out_copies(g - EXPERT_SLOTS):
                cp.wait()

        hi, lo = _unpack_bf16_pairs(x_buf[slot])

        def proj(w_bf):
            return (jnp.dot(hi, w_bf[0:half, :], preferred_element_type=F32)
                    + jnp.dot(lo, w_bf[half:, :], preferred_element_type=F32))

        hid = _silu(proj(wg_bf)) * proj(wu_bf)
        y_buf[slot] = _pack_bf16_pairs(jnp.dot(hid.astype(BF16), wd_bf[...], preferred_element_type=F32))
        for cp in out_copies(g):
            cp.start()
        return carry

    lax.fori_loop(0, count, block, 0)

    @pl.when(e == N_EXPERTS - 1)
    def _():
        for back in range(EXPERT_SLOTS, 0, -1):
            @pl.when(n_used >= back)
            def _():
                for cp in out_copies(n_used - back):
                    cp.wait()


EXPERT_SLOTS = 4
EXPERT_IN_PARTS = 2
EXPERT_OUT_PARTS = 4


def _experts(first_blk, n_blk, xs, layer, w_gate, w_up, w_down):
    w_in = pl.BlockSpec((1, 1, D_MODEL, D_EXPERT), lambda e, first, count: (layer, e, 0, 0))
    grid_spec = pltpu.PrefetchScalarGridSpec(
        num_scalar_prefetch=2,
        grid=(N_EXPERTS,),
        in_specs=[pl.BlockSpec(memory_space=pl.ANY), w_in, w_in,
                  pl.BlockSpec((1, 1, D_EXPERT, D_MODEL), lambda e, first, count: (layer, e, 0, 0))],
        out_specs=pl.BlockSpec(memory_space=pl.ANY),
        scratch_shapes=[pltpu.VMEM((D_MODEL, D_EXPERT), BF16), pltpu.VMEM((D_MODEL, D_EXPERT), BF16),
                        pltpu.VMEM((D_EXPERT, D_MODEL), BF16),
                        pltpu.VMEM((EXPERT_SLOTS, MOE_BLK, D_MODEL // 2), jnp.int32),
                        pltpu.VMEM((EXPERT_SLOTS, MOE_BLK, D_MODEL // 2), jnp.int32),
                        pltpu.SemaphoreType.DMA((EXPERT_SLOTS,)), pltpu.SemaphoreType.DMA((EXPERT_SLOTS,))],
    )
    return pl.pallas_call(
        _expert_body,
        grid_spec=grid_spec,
        out_shape=jax.ShapeDtypeStruct((MOE_ROWS, D_MODEL // 2), jnp.int32),
        compiler_params=_cparams("arbitrary"),
        name="moe_experts",
    )(first_blk, n_blk, xs, w_gate, w_up, w_down)


def _combine_body(x1_ref, h2_ref, yg_ref, gk_ref, mod_ref, sg_ref, su_ref, sd_ref, fn_ref, *o_refs, final):
    hi, lo = _unpack_bf16_pairs(h2_ref[...])
    half = D_MODEL // 2

    def proj(w_ref):
        return _dot(hi, w_ref[0:half, :]) + _dot(lo, w_ref[half:, :])

    shared = _dot(_silu(proj(sg_ref)) * proj(su_ref), sd_ref[...])
    acc_hi, acc_lo = shared[:, :half], shared[:, half:]
    gk = gk_ref[...]
    for k in range(TOP_K):
        y_hi, y_lo = _unpack_bf16_pairs(yg_ref[k])
        acc_hi = acc_hi + gk[:, k:k + 1] * y_hi.astype(F32)
        acc_lo = acc_lo + gk[:, k:k + 1] * y_lo.astype(F32)
    acc = jnp.concatenate([acc_hi, acc_lo], axis=1)
    m = mod_ref[0]
    y = x1_ref[...] + m[:, 5 * D_MODEL:6 * D_MODEL] * acc
    if not final:
        o_refs[0][...] = y
        return
    y = _rms(y, fn_ref[...])
    is_ctx = pl.program_id(0) < N_CTX_TILES

    @pl.when(is_ctx)
    def _():
        o_refs[0][...] = y

    @pl.when(jnp.logical_not(is_ctx))
    def _():
        o_refs[1][...] = y


def _combine(x1, h2p, yg, gk, mod_l, ws_gate, ws_up, ws_down, final_norm, final):
    tok = lambda shape: pl.BlockSpec(shape, lambda i: (i, 0))
    full = lambda shape: pl.BlockSpec(shape, lambda i: (0, 0))
    if final:
        out_specs, _ = _token_specs((None, None), D_MODEL)
        out_shape = [jax.ShapeDtypeStruct((T_CTX, D_MODEL), F32), jax.ShapeDtypeStruct((T_LAT, D_MODEL), F32)]
    else:
        out_specs = tok((TM, D_MODEL))
        out_shape = jax.ShapeDtypeStruct((T_ALL, D_MODEL), F32)
    return pl.pallas_call(
        functools.partial(_combine_body, final=final),
        grid=(T_ALL // TM,),
        in_specs=[tok((TM, D_MODEL)), tok((TM, D_MODEL // 2)),
                  pl.BlockSpec((TOP_K, TM, D_MODEL // 2), lambda i: (0, i, 0)),
                  tok((TM, TOP_K)),
                  pl.BlockSpec((1, 1, N_MOD * D_MODEL), lambda i: (_mod_row(i), 0, 0)),
                  full((D_MODEL, D_EXPERT)), full((D_MODEL, D_EXPERT)), full((D_EXPERT, D_MODEL)),
                  full((1, D_MODEL))],
        out_specs=out_specs,
        out_shape=out_shape,
        compiler_params=_cparams("arbitrary"),
        name="moe_combine",
    )(x1, h2p, yg, gk, mod_l, ws_gate, ws_up, ws_down, final_norm.reshape(1, D_MODEL))


def _moe(x1, h2p, chosen, gk, ik, mod_l, layer, w_gate, w_up, w_down, ws_gate, ws_up, ws_down, final_norm, final):
    dest, first_blk, n_blk = _route(chosen, ik)
    xs = _sc_dispatch(h2p, dest)
    y = _experts(first_blk[:, 0], n_blk[:, 0], xs, layer, w_gate, w_up, w_down)
    yg = _sc_collect(y, dest.reshape(-1)).reshape(TOP_K, T_ALL, D_MODEL // 2)
    return _combine(x1, h2p, yg, gk.T, mod_l, ws_gate.astype(BF16), ws_up.astype(BF16), ws_down.astype(BF16),
                    final_norm, final)


def kernel(x_prompt, x_sample, cache_a_k, cache_a_v, cache_b_k, cache_b_v, state_d_fwd, state_d_bwd, c, c_ctx, w_ada, b_ada, norm_mix, norm_ffn, w_in_attn, w_out_attn, sink_a, rpb_b, w_in_rec, w_out_rec, conv_w, conv_b, filt_w1, filt_b1, filt_w2, filt_b2, filt_w3, filt_b3, filt_freq, filt_w4, d_skip, lb_fwd, lb_bwd, norm_d, w_router, router_bias, w_gate, w_up, w_down, ws_gate, ws_up, ws_down, final_norm):
    x = (x_prompt.reshape(T_CTX, D_MODEL), x_sample.reshape(T_LAT, D_MODEL))
    cvec = jnp.concatenate([c_ctx[None, :], c], axis=0)
    mod = _ada(cvec, w_ada, b_ada).reshape(DEPTH, CVEC_PAD, 1, N_MOD * D_MODEL)

    new_kv = None
    new_state = None
    for l in range(DEPTH):
        j = l // 2
        final = l == DEPTH - 1
        if l % 2 == 0:
            qkv = _inproj(x, mod[l], norm_mix[l], w_in_attn[j].astype(BF16))
            oa_ctx, ob_ctx, *new_kv = _ctx_attn(qkv, sink_a[j])
            new_kv = tuple(new_kv)
            q_rot, k_rot = _rope(qkv)
            cache = lambda t: t[:, j].reshape(DEC_BATCH, PAST_LEN, -1)
            oa_lat = _win_attn(qkv, q_rot, k_rot, cache(cache_a_k), cache(cache_a_v), sink_a[j])
            ob_lat = _na_attn(qkv, cache(cache_b_k), cache(cache_b_v), _na_rel_rows(rpb_b[j]))
            mix_a = (oa_ctx, oa_lat)
            mix_b = (ob_ctx, ob_lat)
            w_out = w_out_attn[j]
        else:
            u = _inproj(x, mod[l], norm_mix[l], w_in_rec[j].astype(BF16))
            filt = (filt_w1[j], filt_b1[j], filt_w2[j], filt_b2[j], filt_w3[j], filt_b3[j], filt_freq[j],
                    filt_w4[j])
            y_ctx = _hyena(u, 0, BATCH, SEQ, conv_w[j], conv_b[j], d_skip[j], _hyena_filter(SEQ, filt))
            y_lat = _hyena(u, T_CTX // DEC_SEQ, DEC_BATCH, DEC_SEQ, conv_w[j], conv_b[j], d_skip[j],
                           _hyena_filter(DEC_SEQ, filt))
            zeros = jnp.zeros((BATCH, D_HEADS, D_KDIM, D_VDIM), F32)
            o_ctx, s_f, s_b = _hgrn(u, 0, BATCH, SEQ, lb_fwd, lb_bwd, norm_d[j], zeros, zeros, l)
            o_lat, _, _ = _hgrn(u, T_CTX // DEC_SEQ, DEC_BATCH, DEC_SEQ, lb_fwd, lb_bwd, norm_d[j],
                                state_d_fwd[:, j], state_d_bwd[:, j], l)
            new_state = (s_f[:, None], s_b[:, None])
            mix_a = (y_ctx, y_lat)
            mix_b = (o_ctx, o_lat)
            w_out = w_out_rec[j]
        x1, h2p, chosen, gk, ik = _outproj(mix_a, mix_b, x, mod[l], norm_ffn[l], w_out.astype(BF16), w_router[l],
                                           router_bias[l])
        x = _moe(x1, h2p, chosen, gk, ik, mod[l], l, w_gate, w_up, w_down, ws_gate[l], ws_up[l],
                 ws_down[l], final_norm, final)

    y_prompt = x[0].reshape(BATCH, SEQ, D_MODEL)
    y_sample = x[1].reshape(DEC_BATCH, DEC_SEQ, D_MODEL)
    return (y_prompt, y_sample) + new_kv + new_state
```

```python
import functools
import math

import numpy as np
import jax
import jax.numpy as jnp
from jax import lax
from jax.experimental import pallas as pl
from jax.experimental.pallas import tpu as pltpu
from jax.experimental.pallas import tpu_sc as plsc

F32 = jnp.float32
BF16 = jnp.bfloat16
HI = lax.Precision.HIGHEST

D_MODEL = 1024
BATCH = 16
SEQ = 256
DEPTH = 2
DEC_BATCH = 2
DEC_SEQ = 1024
PAST_LEN = 512
GRID_W = 64
HEAD_DIM = 64
N_MOD = 6
RMS_EPS = 1e-6
A_HEADS = 8
A_KV_HEADS = 2
A_GROUP = A_HEADS // A_KV_HEADS
WINDOW = 128
ROPE_BASE = 10000.0
B_HEADS = 8
NA_ROWS = 8
NA_COLS = 16
C_DIM = 512
C_EMB = 33
C_FFN = 64
HYENA_MIN_DECAY = math.log(1e-2) / 1.5
HYENA_MAX_DECAY = math.log(1e-2) / 0.3
D_KDIM = 128
D_VDIM = 128
D_HEADS = 4
N_EXPERTS = 64
TOP_K = 8
D_EXPERT = 256
ROUTE_SCALE = 2.5
A_Q = A_HEADS * HEAD_DIM
A_KV = A_KV_HEADS * HEAD_DIM
B_W = B_HEADS * HEAD_DIM
ATTN_IN = A_Q + 2 * A_KV + 3 * B_W
REC_IN = 3 * C_DIM + 5 * D_HEADS * D_KDIM

T_CTX = BATCH * SEQ
T_LAT = DEC_BATCH * DEC_SEQ
T_ALL = T_CTX + T_LAT
N_CVEC = 1 + DEC_BATCH
CVEC_PAD = 8
TM = 512
MASK_NEG = -1e30
GLA_CHUNK = 64
GLA_SPAN = 256
HGRN_HEADS_PER_STEP = 4
DFT_CHUNK = 256
MOE_BLK = 512
MOE_NBLK = -(-(T_ALL * TOP_K + N_EXPERTS * (MOE_BLK - 1)) // MOE_BLK)
MOE_ROWS = MOE_NBLK * MOE_BLK
SC_CORES = 2
SC_SUBCORES = 16
SC_WORKERS = SC_CORES * SC_SUBCORES
DISP_CHUNK = 128
DISP_SPLIT = 2
COLLECT_CHUNK = 64
VMEM_LIMIT = 56 * 1024 * 1024


def _cparams(*sem):
    return pltpu.CompilerParams(dimension_semantics=sem, vmem_limit_bytes=VMEM_LIMIT)


def _mod_row(i):
    return jnp.where(i < T_CTX // TM, 0, 1 + (i - T_CTX // TM) // (DEC_SEQ // TM))


def _dot(a, b):
    return jnp.dot(a.astype(BF16), b.astype(BF16), preferred_element_type=F32)


def _dot_nt(a, b):
    return lax.dot_general(a.astype(BF16), b.astype(BF16), (((1,), (1,)), ((), ())),
                           preferred_element_type=F32)


def _dot_tn(a, b):
    return lax.dot_general(a.astype(BF16), b.astype(BF16), (((0,), (0,)), ((), ())),
                           preferred_element_type=F32)


def _dot_hi(a, b):
    return jnp.dot(a, b, precision=HI, preferred_element_type=F32)


def _split_bf16(x):
    hi = x.astype(BF16)
    return hi, (x - hi.astype(F32)).astype(BF16)


def _dot_split(a, b):
    a_hi, a_lo = _split_bf16(a)
    b_hi, b_lo = _split_bf16(b)
    dot = lambda x, y: jnp.dot(x, y, preferred_element_type=F32)
    return dot(a_hi, b_hi) + dot(a_hi, b_lo) + dot(a_lo, b_hi)


def _silu(x):
    return x * jax.nn.sigmoid(x)


def _rms(x, g):
    return x * lax.rsqrt(jnp.mean(x * x, axis=-1, keepdims=True) + RMS_EPS) * g


ADA_TN = 1536
ADA_UNROLL = 4


def _ada_body(cb_ref, w_ref, b_ref, o_ref):
    tn = o_ref.shape[-1]
    n_slab = tn // LANES

    def step(k8, accs):
        r0 = pl.multiple_of(k8 * 8, 8)
        sk = [_silu(cb_ref[j, pl.ds(r0, 8), :]) for j in range(N_CVEC)]
        out = []
        for s in range(n_slab):
            wk = w_ref[0, pl.ds(r0, 8), s * LANES:(s + 1) * LANES]
            out.extend(accs[s * N_CVEC + j] + wk * sk[j] for j in range(N_CVEC))
        return tuple(out)

    accs = lax.fori_loop(0, D_MODEL // 8, step,
                         tuple(jnp.zeros((8, LANES), F32) for _ in range(n_slab * N_CVEC)), unroll=ADA_UNROLL)
    o_ref[0] = jnp.zeros((CVEC_PAD, tn), F32)
    for s in range(n_slab):
        for j in range(N_CVEC):
            o_ref[0, j:j + 1, s * LANES:(s + 1) * LANES] = (
                jnp.sum(accs[s * N_CVEC + j], axis=0, keepdims=True) + b_ref[0, :, s * LANES:(s + 1) * LANES])


def _ada(cvec, w_ada, b_ada):
    n_out = N_MOD * D_MODEL
    c_lanes = jnp.broadcast_to(cvec[:, :, None], (N_CVEC, D_MODEL, LANES))
    return pl.pallas_call(
        _ada_body,
        grid=(DEPTH, n_out // ADA_TN),
        in_specs=[pl.BlockSpec((N_CVEC, D_MODEL, LANES), lambda l, n: (0, 0, 0)),
                  pl.BlockSpec((1, D_MODEL, ADA_TN), lambda l, n: (l, 0, n)),
                  pl.BlockSpec((1, 1, ADA_TN), lambda l, n: (l, 0, n))],
        out_specs=pl.BlockSpec((1, CVEC_PAD, ADA_TN), lambda l, n: (l, 0, n)),
        out_shape=jax.ShapeDtypeStruct((DEPTH, CVEC_PAD, n_out), F32),
        compiler_params=_cparams("parallel", "parallel"),
        name="ada",
    )(c_lanes, w_ada, b_ada.reshape(DEPTH, 1, n_out))


N_CTX_TILES = T_CTX // TM


def _token_specs(x, width):
    if not isinstance(x, tuple):
        return [pl.BlockSpec((TM, width), lambda i: (i, 0))], (x,)
    return ([pl.BlockSpec((TM, width), lambda i: (jnp.minimum(i, N_CTX_TILES - 1), 0)),
             pl.BlockSpec((TM, width), lambda i: (jnp.maximum(i - N_CTX_TILES, 0), 0))], x)


def _token_tile(refs):
    if len(refs) == 1:
        return refs[0][...]
    return jnp.where(pl.program_id(0) < N_CTX_TILES, refs[0][...], refs[1][...])


def _inproj_body(*refs, n_x):
    x_refs, (mod_ref, g_ref, w_ref, o_ref) = refs[:n_x], refs[n_x:]
    m = mod_ref[0]
    h = _rms(_token_tile(x_refs), g_ref[...]) * (1.0 + m[:, D_MODEL:2 * D_MODEL]) + m[:, 0:D_MODEL]
    o_ref[...] = _dot(h, w_ref[...])


def _inproj(x, mod_l, gain, w_bf16):
    n = w_bf16.shape[1]
    x_specs, x_args = _token_specs(x, D_MODEL)
    return pl.pallas_call(
        functools.partial(_inproj_body, n_x=len(x_args)),
        grid=(T_ALL // TM,),
        in_specs=x_specs + [pl.BlockSpec((1, 1, N_MOD * D_MODEL), lambda i: (_mod_row(i), 0, 0)),
                            pl.BlockSpec((1, D_MODEL), lambda i: (0, 0)),
                            pl.BlockSpec((D_MODEL, n), lambda i: (0, 0))],
        out_specs=pl.BlockSpec((TM, n), lambda i: (i, 0)),
        out_shape=jax.ShapeDtypeStruct((T_ALL, n), F32),
        compiler_params=_cparams("parallel"),
        name="inproj",
    )(*x_args, mod_l, gain.reshape(1, D_MODEL), w_bf16)


def _head_cols(h):
    return slice(h * HEAD_DIM, (h + 1) * HEAD_DIM)


def _group_rows(ref, rows, first_col, sink_ref, hk):
    n = rows.stop - rows.start
    q = jnp.concatenate([ref[rows, first_col + g * HEAD_DIM:first_col + (g + 1) * HEAD_DIM]
                         for g in range(A_GROUP)], axis=0)
    sink = jnp.concatenate([jnp.broadcast_to(sink_ref[:, hk * A_GROUP + g:hk * A_GROUP + g + 1], (n, 1))
                            for g in range(A_GROUP)], axis=0)
    return q, sink


def _ctx_attn_body(qkv_ref, sink_ref, oa_ref, ob_ref, ak_ref, av_ref, bk_ref, bv_ref):
    scale = HEAD_DIM ** -0.5
    lane = lax.broadcasted_iota(jnp.int32, (SEQ, LANES), 1)
    in_half = [lane < HEAD_DIM, lane >= HEAD_DIM]

    def attend(q, k, v, sink):
        s = _dot_nt(q, k) * scale
        m = jnp.max(s, axis=-1, keepdims=True)
        if sink is not None:
            m = jnp.maximum(m, sink)
        p = jnp.exp(s - m)
        den = jnp.sum(p, axis=-1, keepdims=True)
        if sink is not None:
            den = den + jnp.exp(sink - m)
        return _dot(p, v) / den

    def tile(first_col, t):
        return qkv_ref[:, first_col + t * LANES:first_col + (t + 1) * LANES]

    base = A_Q + 2 * A_KV
    for hk in range(A_KV_HEADS):
        dst = pl.ds(hk, SEQ, stride=A_KV_HEADS)
        ak_ref[0, dst, :] = qkv_ref[:, A_Q + hk * HEAD_DIM:A_Q + (hk + 1) * HEAD_DIM]
        av_ref[0, dst, :] = qkv_ref[:, A_Q + A_KV + hk * HEAD_DIM:A_Q + A_KV + (hk + 1) * HEAD_DIM]
    for h in range(B_HEADS):
        dst = pl.ds(h, SEQ, stride=B_HEADS)
        bk_ref[0, dst, :] = qkv_ref[:, base + B_W + h * HEAD_DIM:base + B_W + (h + 1) * HEAD_DIM]
        bv_ref[0, dst, :] = qkv_ref[:, base + 2 * B_W + h * HEAD_DIM:base + 2 * B_W + (h + 1) * HEAD_DIM]

    k_t, v_t = tile(A_Q, 0), tile(A_Q + A_KV, 0)
    k_sw, v_sw = pltpu.roll(k_t, HEAD_DIM, axis=1), pltpu.roll(v_t, HEAD_DIM, axis=1)
    tiles_per_kv = A_GROUP // HEADS_PER_TILE
    for hk in range(A_KV_HEADS):
        q_tiles = [tile(0, hk * tiles_per_kv + j) for j in range(tiles_per_kv)]
        halves = []
        for p in range(HEADS_PER_TILE):
            q = jnp.concatenate([jnp.where(in_half[p], qt, 0.0) for qt in q_tiles], axis=0)
            heads = [(hk * tiles_per_kv + j) * HEADS_PER_TILE + p for j in range(tiles_per_kv)]
            sink = jnp.concatenate([jnp.broadcast_to(sink_ref[:, h:h + 1], (SEQ, 1)) for h in heads], axis=0)
            halves.append(attend(q, k_t if p == hk else k_sw, v_t if p == hk else v_sw, sink))
        first_half = lax.broadcasted_iota(jnp.int32, halves[0].shape, 1) < HEAD_DIM
        o = jnp.where(first_half, halves[0], halves[1])
        for j in range(tiles_per_kv):
            t = hk * tiles_per_kv + j
            oa_ref[:, t * LANES:(t + 1) * LANES] = o[j * SEQ:(j + 1) * SEQ]

    for t in range(B_HEADS // HEADS_PER_TILE):
        q_t, k_b, v_b = tile(base, t), tile(base + B_W, t), tile(base + 2 * B_W, t)
        halves = [attend(jnp.where(in_half[p], q_t, 0.0), k_b, v_b, None) for p in range(HEADS_PER_TILE)]
        ob_ref[:, t * LANES:(t + 1) * LANES] = jnp.where(in_half[0], halves[0], halves[1])


def _ctx_attn(qkv, sink):
    kv_spec = lambda heads: pl.BlockSpec((1, SEQ * heads, HEAD_DIM), lambda b: (b, 0, 0))
    kv_sd = lambda heads: jax.ShapeDtypeStruct((BATCH, SEQ * heads, HEAD_DIM), F32)
    outs = pl.pallas_call(
        _ctx_attn_body,
        grid=(BATCH,),
        in_specs=[pl.BlockSpec((SEQ, ATTN_IN), lambda b: (b, 0)),
                  pl.BlockSpec((1, A_HEADS), lambda b: (0, 0))],
        out_specs=[pl.BlockSpec((SEQ, A_Q), lambda b: (b, 0)), pl.BlockSpec((SEQ, B_W), lambda b: (b, 0)),
                   kv_spec(A_KV_HEADS), kv_spec(A_KV_HEADS), kv_spec(B_HEADS), kv_spec(B_HEADS)],
        out_shape=[jax.ShapeDtypeStruct((T_CTX, A_Q), F32), jax.ShapeDtypeStruct((T_CTX, B_W), F32),
                   kv_sd(A_KV_HEADS), kv_sd(A_KV_HEADS), kv_sd(B_HEADS), kv_sd(B_HEADS)],
        compiler_params=_cparams("parallel"),
        name="ctx_attn",
    )(qkv, sink.reshape(1, A_HEADS))
    caches = [t.reshape(BATCH, 1, SEQ, -1, HEAD_DIM) for t in outs[2:]]
    return outs[0], outs[1], *caches


@functools.lru_cache(maxsize=None)
def _rope_tables(width):
    half = HEAD_DIM // 2
    t = np.arange(DEC_SEQ)
    inv = ROPE_BASE ** (-np.arange(0, half, 2, dtype=np.float64) / half)
    ang_r = (t // GRID_W)[:, None] * inv[None, :]
    ang_c = (t % GRID_W)[:, None] * inv[None, :]
    cos = np.concatenate([np.cos(ang_r)] * 2 + [np.cos(ang_c)] * 2, axis=-1)
    sin = np.concatenate([-np.sin(ang_r), np.sin(ang_r), -np.sin(ang_c), np.sin(ang_c)], axis=-1)
    reps = width // HEAD_DIM
    return (np.tile(cos, (1, reps)).astype(np.float32), np.tile(sin, (1, reps)).astype(np.float32))


def _rope_body(q_ref, k_ref, cq_ref, sq_ref, ck_ref, sk_ref, qo_ref, ko_ref):
    quarter = HEAD_DIM // 4

    def rot(x, cos, sin):
        w = x.shape[-1]
        lane = lax.broadcasted_iota(jnp.int32, x.shape, 1)
        fwd = pltpu.roll(x, w - quarter, axis=1)
        bwd = pltpu.roll(x, quarter, axis=1)
        partner = jnp.where((lane & (2 * quarter - 1)) < quarter, fwd, bwd)
        return x * cos + partner * sin

    qo_ref[...] = rot(q_ref[...], cq_ref[...], sq_ref[...])
    ko_ref[...] = rot(k_ref[...], ck_ref[...], sk_ref[...])


def _rope(qkv):
    cq, sq = _rope_tables(A_Q)
    ck, sk = _rope_tables(A_KV)
    tab = lambda w: pl.BlockSpec((DEC_SEQ, w), lambda b: (0, 0))
    row0 = T_CTX // DEC_SEQ
    return pl.pallas_call(
        _rope_body,
        grid=(DEC_BATCH,),
        in_specs=[pl.BlockSpec((DEC_SEQ, A_Q), lambda b: (row0 + b, 0)),
                  pl.BlockSpec((DEC_SEQ, A_KV), lambda b: (row0 + b, A_Q // A_KV)),
                  tab(A_Q), tab(A_Q), tab(A_KV), tab(A_KV)],
        out_specs=[pl.BlockSpec((DEC_SEQ, A_Q), lambda b: (b, 0)),
                   pl.BlockSpec((DEC_SEQ, A_KV), lambda b: (b, 0))],
        out_shape=[jax.ShapeDtypeStruct((T_LAT, A_Q), F32), jax.ShapeDtypeStruct((T_LAT, A_KV), F32)],
        compiler_params=_cparams("parallel"),
        name="rope",
    )(qkv, qkv, jnp.asarray(cq), jnp.asarray(sq), jnp.asarray(ck), jnp.asarray(sk))


WIN_QB = 256


def _pick_head(x, h, n_heads):
    out = x[:, _head_cols(0)]
    for i in range(1, n_heads):
        out = jnp.where(h == i, x[:, _head_cols(i)], out)
    return out


def _win_attn_body(qraw_ref, qrot_ref, krot_ref, v_ref, kc_ref, vc_ref, sink_ref, o_ref):
    scale = HEAD_DIM ** -0.5
    hk = pl.program_id(1)
    tiles = A_GROUP // HEADS_PER_TILE

    def kv_in_half(x):
        swapped = pltpu.roll(x, HEAD_DIM, axis=1)
        return [jnp.where(hk == p, x, swapped) for p in range(HEADS_PER_TILE)]

    k, v, kc, vc = kv_in_half(krot_ref[...]), kv_in_half(v_ref[...]), kv_in_half(kc_ref[0]), kv_in_half(vc_ref[0])
    head_lane = lax.broadcasted_iota(jnp.int32, (1, A_HEADS), 1)

    def sink_rows(p):
        heads = [hk * A_GROUP + j * HEADS_PER_TILE + p for j in range(tiles)]
        vals = [jnp.sum(jnp.where(head_lane == h, sink_ref[...], 0.0), axis=-1, keepdims=True) for h in heads]
        return jnp.concatenate([jnp.broadcast_to(s, (WIN_QB, 1)) for s in vals], axis=0)

    sinks = [sink_rows(p) for p in range(HEADS_PER_TILE)]
    lane = lax.broadcasted_iota(jnp.int32, (tiles * WIN_QB, LANES), 1)
    in_half = [lane < HEAD_DIM, lane >= HEAD_DIM]
    for qb in range(DEC_SEQ // WIN_QB):
        q0 = qb * WIN_QB
        rows = slice(q0, q0 + WIN_QB)
        lo = max(0, q0 - WINDOW)
        hi = min(DEC_SEQ, q0 + WIN_QB + WINDOW)
        q_rot = jnp.concatenate([qrot_ref[rows, j * LANES:(j + 1) * LANES] for j in range(tiles)], axis=0)
        q_raw = jnp.concatenate([qraw_ref[rows, j * LANES:(j + 1) * LANES] for j in range(tiles)], axis=0)
        halves = []
        for p in range(HEADS_PER_TILE):
            s_loc = _dot_nt(jnp.where(in_half[p], q_rot, 0.0), k[p][lo:hi]) * scale
            qpos = q0 + (lax.broadcasted_iota(jnp.int32, s_loc.shape, 0) & (WIN_QB - 1))
            kpos = lo + lax.broadcasted_iota(jnp.int32, s_loc.shape, 1)
            s_loc = jnp.where(jnp.abs(kpos - qpos) <= WINDOW, s_loc, MASK_NEG)
            s_ctx = _dot_nt(jnp.where(in_half[p], q_raw, 0.0), kc[p]) * scale
            m = jnp.maximum(jnp.maximum(jnp.max(s_loc, axis=-1, keepdims=True),
                                        jnp.max(s_ctx, axis=-1, keepdims=True)), sinks[p])
            p_loc = jnp.exp(s_loc - m)
            p_ctx = jnp.exp(s_ctx - m)
            den = (jnp.sum(p_loc, axis=-1, keepdims=True) + jnp.sum(p_ctx, axis=-1, keepdims=True)
                   + jnp.exp(sinks[p] - m))
            halves.append((_dot(p_ctx, vc[p]) + _dot(p_loc, v[p][lo:hi])) / den)
        o = jnp.where(in_half[0], halves[0], halves[1])
        for j in range(tiles):
            o_ref[rows, j * LANES:(j + 1) * LANES] = o[j * WIN_QB:(j + 1) * WIN_QB]


def _win_attn(qkv, q_rot, k_rot, kc, vc, sink):
    row0 = T_CTX // DEC_SEQ
    gw = A_GROUP * HEAD_DIM
    return pl.pallas_call(
        _win_attn_body,
        grid=(DEC_BATCH, A_KV_HEADS),
        in_specs=[pl.BlockSpec((DEC_SEQ, gw), lambda b, h: (row0 + b, h)),
                  pl.BlockSpec((DEC_SEQ, gw), lambda b, h: (b, h)),
                  pl.BlockSpec((DEC_SEQ, A_KV), lambda b, h: (b, 0)),
                  pl.BlockSpec((DEC_SEQ, A_KV), lambda b, h: (row0 + b, (A_Q + A_KV) // A_KV)),
                  pl.BlockSpec((1, PAST_LEN, A_KV), lambda b, h: (b, 0, 0)),
                  pl.BlockSpec((1, PAST_LEN, A_KV), lambda b, h: (b, 0, 0)),
                  pl.BlockSpec((1, A_HEADS), lambda b, h: (0, 0))],
        out_specs=pl.BlockSpec((DEC_SEQ, gw), lambda b, h: (b, h)),
        out_shape=jax.ShapeDtypeStruct((T_LAT, A_Q), F32),
        compiler_params=_cparams("parallel", "parallel"),
        name="win_attn",
    )(qkv, q_rot, k_rot, qkv, kc, vc, sink.reshape(1, A_HEADS))


GRID_ROWS = DEC_SEQ // GRID_W
NA_BAND = min(NA_ROWS, GRID_ROWS)


NA_REL_ROWS = 2 * NA_ROWS - 1
NA_REL_COLS = 2 * NA_COLS - 1
LANES = 128
HEADS_PER_TILE = LANES // HEAD_DIM


def _na_rel_rows(rpb):
    pad = jnp.zeros((B_HEADS, NA_REL_ROWS, GRID_W - NA_REL_COLS), F32)
    one = jnp.concatenate([rpb, pad], axis=-1)
    nxt = jnp.concatenate([one[:, 1:], jnp.zeros((B_HEADS, 1, GRID_W), F32)], axis=1)
    both = jnp.concatenate([one, nxt], axis=-1)
    return jnp.concatenate([both, jnp.zeros((B_HEADS, 16 - NA_REL_ROWS, LANES), F32)], axis=1)


NA_HEADS_PER_STEP = LANES // HEAD_DIM


def _na_row_groups():
    groups = []
    for r in range(GRID_ROWS):
        rs = min(max(r - NA_ROWS // 2, 0), GRID_ROWS - NA_BAND)
        if groups and groups[-1][2] == rs:
            groups[-1][1] += 1
        else:
            groups.append([r, 1, rs])
    return groups


def _na_attn_body(q_ref, k_ref, v_ref, kc_ref, vc_ref, rel_ref, o_ref):
    scale = HEAD_DIM ** -0.5
    cq = lax.broadcasted_iota(jnp.int32, (GRID_W, LANES), 0)
    kcol = lax.broadcasted_iota(jnp.int32, (GRID_W, LANES), 1) & (GRID_W - 1)
    cs = jnp.clip(cq - NA_COLS // 2, 0, GRID_W - NA_COLS)
    col_ok = (kcol >= cs) & (kcol < cs + NA_COLS)
    kc = kc_ref[0]
    vc = vc_ref[0]
    tiles = {}

    def pair_tile(hh, a):
        if (hh, a) not in tiles:
            x = jnp.broadcast_to(rel_ref[hh, a:a + 1, :], (GRID_W, LANES))
            t = pltpu.roll(x, LANES - (NA_COLS - 1), axis=1, stride=1, stride_axis=0)
            tiles[hh, a] = jnp.where(col_ok, t, MASK_NEG)
        return tiles[hh, a]

    for r0, n_r, rs in _na_row_groups():
        rows = slice(r0 * GRID_W, (r0 + n_r) * GRID_W)
        band = slice(rs * GRID_W, (rs + NA_BAND) * GRID_W)
        q_t, k_t, v_t = q_ref[rows, :], k_ref[band, :], v_ref[band, :]
        head_of_lane = lax.broadcasted_iota(jnp.int32, q_t.shape, 1) >> (HEAD_DIM.bit_length() - 1)
        o = jnp.zeros(q_t.shape, F32)
        for hh in range(NA_HEADS_PER_STEP):
            bias = jnp.concatenate(
                [jnp.concatenate([pair_tile(hh, rs - r + NA_ROWS - 1 + 2 * i) for i in range(NA_BAND // 2)], axis=1)
                 for r in range(r0, r0 + n_r)], axis=0)
            q = jnp.where(head_of_lane == hh, q_t, 0.0)
            s_loc = _dot_nt(q, k_t) * scale + bias
            s_ctx = _dot_nt(q, kc) * scale
            m = jnp.maximum(jnp.max(s_loc, axis=-1, keepdims=True), jnp.max(s_ctx, axis=-1, keepdims=True))
            p_loc = jnp.exp(s_loc - m)
            p_ctx = jnp.exp(s_ctx - m)
            den = jnp.sum(p_loc, axis=-1, keepdims=True) + jnp.sum(p_ctx, axis=-1, keepdims=True)
            o = jnp.where(head_of_lane == hh, (_dot(p_ctx, vc) + _dot(p_loc, v_t)) / den, o)
        o_ref[rows, :] = o


def _na_attn(qkv, kc, vc, rel):
    row0 = T_CTX // DEC_SEQ
    col0 = (A_Q + 2 * A_KV) // LANES
    n_blk = B_W // LANES
    col = lambda j: pl.BlockSpec((DEC_SEQ, LANES), lambda b, p: (row0 + b, col0 + j * n_blk + p))
    cache = pl.BlockSpec((1, PAST_LEN, LANES), lambda b, p: (b, 0, p))
    return pl.pallas_call(
        _na_attn_body,
        grid=(DEC_BATCH, n_blk),
        in_specs=[col(0), col(1), col(2), cache, cache,
                  pl.BlockSpec((NA_HEADS_PER_STEP, 16, LANES), lambda b, p: (p, 0, 0))],
        out_specs=pl.BlockSpec((DEC_SEQ, LANES), lambda b, p: (b, p)),
        out_shape=jax.ShapeDtypeStruct((T_LAT, B_W), F32),
        compiler_params=_cparams("parallel", "parallel"),
        name="na_attn",
    )(qkv, qkv, qkv, kc, vc, rel)


@functools.lru_cache(maxsize=None)
def _dft_mats(L):
    n = 2 * L
    fc = min(L, DFT_CHUNK)
    f = np.arange(L)[:, None]
    t = np.arange(L)[None, :]
    ang = 2.0 * np.pi * ((f * t) % n) / n
    m1 = np.cos(ang)
    m2 = np.sin(ang)
    m2[0, :] = np.where(np.arange(L) % 2 == 0, 1.0, -1.0)
    wgt = np.full((L, 1), 2.0)
    wgt[0, 0] = 1.0
    nch = L // fc
    fwd = np.concatenate([m1.reshape(nch, fc, L), m2.reshape(nch, fc, L)], axis=1)
    inv = np.concatenate([(m1 * wgt / n).reshape(nch, fc, L), (m2 * wgt / n).reshape(nch, fc, L)], axis=1)
    inv = np.transpose(inv, (0, 2, 1))
    return fwd.astype(np.float32), inv.astype(np.float32)


@functools.lru_cache(maxsize=None)
def _filter_consts(L):
    t = np.linspace(0.0, 1.0, L)[:, None]
    bands = (C_EMB - 1) // 2
    ang = (2.0 * math.pi / L) * np.arange(L)[:, None] * np.linspace(1e-4, bands - 1, bands)[None, :]
    z = np.concatenate([t, np.cos(ang), -np.sin(ang)], axis=-1)
    zpad = np.zeros((L, 128))
    zpad[:, :C_EMB] = z
    deltas = np.abs(np.linspace(HYENA_MIN_DECAY, HYENA_MAX_DECAY, C_DIM))
    window = np.exp(-t * deltas[None, :])
    return zpad.astype(np.float32), window.astype(np.float32)


def _filter_body(z_ref, w1_ref, b1_ref, w2_ref, b2_ref, w3_ref, b3_ref, fr_ref, w4_ref, win_ref, fm_ref,
                 hr_ref, g_ref, hq_ref, hs_scr, hd_scr):
    c = pl.program_id(0)
    fc = hr_ref.shape[0]

    @pl.when(c == 0)
    def _():
        fr = fr_ref[...]
        hh = jnp.sin(fr * (_dot_hi(z_ref[...], w1_ref[...]) + b1_ref[...]))
        hh = jnp.sin(fr * (_dot_hi(hh, w2_ref[...]) + b2_ref[...]))
        hh = jnp.sin(fr * (_dot_hi(hh, w3_ref[...]) + b3_ref[...]))
        hh = _dot_hi(hh, w4_ref[...])
        hf = hh[:, :C_DIM] * win_ref[...]
        hb = hh[:, C_DIM:] * win_ref[...]
        hs_scr[...] = hf + hb
        hd_scr[...] = hf - hb

    fm = fm_ref[0]
    hr = _dot_split(fm[:fc], hs_scr[...])
    first = (lax.broadcasted_iota(jnp.int32, (fc, C_DIM), 0) == 0) & (c == 0)
    hr_ref[...] = hr
    g_ref[...] = jnp.where(first, 0.0, _dot_split(fm[fc:], hd_scr[...]))
    hs = hs_scr[...]
    sign = jnp.where((lax.broadcasted_iota(jnp.int32, hs.shape, 0) & 1) == 0, 1.0, -1.0)
    hq_ref[...] = jnp.where(first, jnp.sum(hs * sign, axis=0, keepdims=True), hr)


def _hyena_filter(L, filt):
    w1, b1, w2, b2, w3, b3, freq, w4 = filt
    zpad, window = _filter_consts(L)
    fwd, _ = _dft_mats(L)
    nch, fc2, _ = fwd.shape
    fc = fc2 // 2
    w1p = jnp.pad(w1, ((0, 128 - C_EMB), (0, 0)))
    full = lambda shape: pl.BlockSpec(shape, lambda c: tuple(0 for _ in shape))
    out_spec = pl.BlockSpec((fc, C_DIM), lambda c: (c, 0))
    out_sd = jax.ShapeDtypeStruct((L, C_DIM), F32)
    return pl.pallas_call(
        _filter_body,
        grid=(nch,),
        in_specs=[full((L, 128)), full((128, C_FFN)), full((1, C_FFN)), full((C_FFN, C_FFN)), full((1, C_FFN)),
                  full((C_FFN, C_FFN)), full((1, C_FFN)), full((1, C_FFN)), full((C_FFN, 2 * C_DIM)),
                  full((L, C_DIM)), pl.BlockSpec((1, fc2, L), lambda c: (c, 0, 0))],
        out_specs=[out_spec, out_spec, out_spec],
        out_shape=[out_sd, out_sd, out_sd],
        scratch_shapes=[pltpu.VMEM((L, C_DIM), F32), pltpu.VMEM((L, C_DIM), F32)],
        compiler_params=_cparams("arbitrary"),
        name="hyena_filter",
    )(jnp.asarray(zpad), w1p, b1.reshape(1, C_FFN), w2, b2.reshape(1, C_FFN), w3, b3.reshape(1, C_FFN),
      freq.reshape(1, C_FFN), w4, jnp.asarray(window), jnp.asarray(fwd))


def _hyena_body(u_ref, cw_ref, cb_ref, d_ref, fm_ref, fi_ref, hr_ref, g_ref, hq_ref, y_ref,
                x0_scr, z_scr, acc_scr):
    c = pl.program_id(1)
    L = y_ref.shape[0]
    fc = hr_ref.shape[0]

    @pl.when(c == 0)
    def _():
        row = lax.broadcasted_iota(jnp.int32, (L, C_DIM), 0)

        def short_conv(sec):
            cols = slice(sec * C_DIM, (sec + 1) * C_DIM)
            u = u_ref[:, cols]
            prev = jnp.where(row == 0, 0.0, pltpu.roll(u, 1, axis=0))
            nxt = jnp.where(row == L - 1, 0.0, pltpu.roll(u, L - 1, axis=0))
            return (prev * cw_ref[0:1, cols] + u * cw_ref[1:2, cols] + nxt * cw_ref[2:3, cols]
                    + cb_ref[:, cols])

        x0_scr[...] = short_conv(0)
        z_scr[...] = short_conv(1) * short_conv(2)
        acc_scr[...] = jnp.zeros((L, C_DIM), F32)

    ab = _dot_split(fm_ref[0], z_scr[...])
    a, b = ab[:fc], ab[fc:]
    hr, g, hq = hr_ref[...], g_ref[...], hq_ref[...]
    pq = jnp.concatenate([a * hr - b * g, a * g + b * hq], axis=0)
    acc_scr[...] += _dot_split(fi_ref[0], pq)

    @pl.when(c == pl.num_programs(1) - 1)
    def _():
        y_ref[...] = x0_scr[...] * (acc_scr[...] + z_scr[...] * d_ref[...])


def _hyena(u, row_blk0, n_seq, L, conv_w, conv_b, d_skip, spec):
    hr, g, hq = spec
    fwd, inv = _dft_mats(L)
    nch, fc2, _ = fwd.shape
    fc = fc2 // 2
    u_w = 3 * C_DIM
    return pl.pallas_call(
        _hyena_body,
        grid=(n_seq, nch),
        in_specs=[pl.BlockSpec((L, u_w), lambda b, c: (row_blk0 + b, 0)),
                  pl.BlockSpec((3, u_w), lambda b, c: (0, 0)),
                  pl.BlockSpec((1, u_w), lambda b, c: (0, 0)),
                  pl.BlockSpec((1, C_DIM), lambda b, c: (0, 0)),
                  pl.BlockSpec((1, fc2, L), lambda b, c: (c, 0, 0)),
                  pl.BlockSpec((1, L, fc2), lambda b, c: (c, 0, 0)),
                  pl.BlockSpec((fc, C_DIM), lambda b, c: (c, 0)),
                  pl.BlockSpec((fc, C_DIM), lambda b, c: (c, 0)),
                  pl.BlockSpec((fc, C_DIM), lambda b, c: (c, 0))],
        out_specs=pl.BlockSpec((L, C_DIM), lambda b, c: (b, 0)),
        out_shape=jax.ShapeDtypeStruct((n_seq * L, C_DIM), F32),
        scratch_shapes=[pltpu.VMEM((L, C_DIM), F32)] * 3,
        compiler_params=_cparams("parallel", "arbitrary"),
        name="hyena",
    )(u, conv_w, conv_b.reshape(1, u_w), d_skip.reshape(1, C_DIM), jnp.asarray(fwd), jnp.asarray(inv), hr, g, hq)


def _hgrn_body(q_ref, ff_ref, fb_ref, i_ref, g_ref, lbf_ref, lbb_ref, nd_ref, s0f_ref, s0b_ref,
               o_ref, sf_ref, sb_ref, *, layer):
    L = o_ref.shape[0]
    C = GLA_CHUNK
    S = min(L, GLA_SPAN)
    nc = S // C
    n_span = L // S
    mid = C // 2
    def lower_bound(gm):
        e = jnp.exp(gm - jnp.max(gm, axis=0, keepdims=True))
        p = e / jnp.sum(e, axis=0, keepdims=True)
        return jnp.sum(p[0:layer + 1], axis=0, keepdims=True) - p[0:1]

    def gates(fx, lb):
        f = lb + (1.0 - lb) * jax.nn.sigmoid(fx)
        return 1.0 - f, jnp.log(f)


    chunk_shift = C.bit_length() - 1
    block_shift = D_KDIM.bit_length() - 1
    ti = lax.broadcasted_iota(jnp.int32, (S, S), 0)
    si = lax.broadcasted_iota(jnp.int32, (S, S), 1)
    same_chunk = (ti >> chunk_shift) == (si >> chunk_shift)
    causal = same_chunk & (si <= ti)
    anti = same_chunk & (si >= ti)
    row_chunk = lax.broadcasted_iota(jnp.int32, (S, nc * D_KDIM), 0) >> chunk_shift
    col_chunk = lax.broadcasted_iota(jnp.int32, (S, nc * D_KDIM), 1) >> block_shift
    own_block = row_chunk == col_chunk

    def spread(x):
        return jnp.where(own_block, jnp.concatenate([x] * nc, axis=1), 0.0)

    def chunk_cumsum(mask, lg):
        tri = mask.astype(BF16)
        hi = lg.astype(BF16)
        r1 = lg - hi.astype(F32)
        mid_t = r1.astype(BF16)
        lo = (r1 - mid_t.astype(F32)).astype(BF16)
        dot = lambda t: jnp.dot(tri, t, preferred_element_type=F32)
        return dot(hi) + dot(mid_t) + dot(lo)

    def per_chunk_rows(b, pos):
        return jnp.concatenate([jnp.broadcast_to(b[n * C + pos:n * C + pos + 1], (C, D_KDIM)) for n in range(nc)],
                               axis=0)

    def one_head(q, v, kf, lgf, kb, lgb, st_f, st_b):
        local = []
        for u in range(n_span):
            rows = slice(u * S, (u + 1) * S)
            qs, vs, kfs, kbs = q[rows], v[rows], kf[rows], kb[rows]
            lgs = jnp.concatenate([lgf[rows], lgb[rows]], axis=1)
            pre = chunk_cumsum(causal, lgs)
            b_f = pre[:, :D_KDIM]
            pre_b = pre[:, D_KDIM:]
            b_b = per_chunk_rows(pre_b, C - 1) - pre_b + lgb[rows]
            ref_f, ref_b = per_chunk_rows(b_f, mid), per_chunk_rows(b_b, mid)
            sc = (jnp.where(causal, _dot_nt(qs * jnp.exp(b_f - ref_f), kfs * jnp.exp(ref_f - b_f)), 0.0)
                  + jnp.where(anti, _dot_nt(qs * jnp.exp(b_b - ref_b), kbs * jnp.exp(ref_b - b_b)), 0.0))
            k_out = jnp.concatenate([kfs * jnp.exp(per_chunk_rows(b_f, C - 1) - b_f),
                                     kbs * jnp.exp(per_chunk_rows(b_b, 0) - b_b)], axis=1)
            kv_t = _dot_tn(spread(vs), k_out)
            local.append((_dot(sc, vs), kv_t, b_f, b_b, qs))

        states_f = [[None] * nc for _ in range(n_span)]
        for u in range(n_span):
            _, kv_t, b_f, _, _ = local[u]
            for n in range(nc):
                states_f[u][n] = st_f
                st_f = st_f * jnp.exp(b_f[n * C + C - 1:n * C + C]) + kv_t[n * D_VDIM:(n + 1) * D_VDIM, :D_KDIM]
        states_b = [[None] * nc for _ in range(n_span)]
        for u in reversed(range(n_span)):
            _, kv_t, _, b_b, _ = local[u]
            for n in reversed(range(nc)):
                states_b[u][n] = st_b
                st_b = st_b * jnp.exp(b_b[n * C:n * C + 1]) + kv_t[n * D_VDIM:(n + 1) * D_VDIM, D_KDIM:]

        outs = []
        for u in range(n_span):
            intra, _, b_f, b_b, qs = local[u]
            q_in = jnp.concatenate([spread(qs * jnp.exp(b_f)), spread(qs * jnp.exp(b_b))], axis=1)
            outs.append(intra + _dot_nt(q_in, jnp.concatenate(states_f[u] + states_b[u], axis=1)))
        return (jnp.concatenate(outs, axis=0) if n_span > 1 else outs[0]), st_f, st_b

    for hh in range(o_ref.shape[1] // D_VDIM):
        cols = slice(hh * D_KDIM, (hh + 1) * D_KDIM)
        kf, lgf = gates(ff_ref[:, cols], lower_bound(lbf_ref[:, cols]))
        kb, lgb = gates(fb_ref[:, cols], lower_bound(lbb_ref[:, cols]))
        o, st_f, st_b = one_head(_silu(q_ref[:, cols]), i_ref[:, cols], kf, lgf, kb, lgb,
                                 jnp.transpose(s0f_ref[0, hh]), jnp.transpose(s0b_ref[0, hh]))
        sf_ref[0, hh] = jnp.transpose(st_f)
        sb_ref[0, hh] = jnp.transpose(st_b)
        o_ref[:, cols] = _rms(o, nd_ref[...]) * _silu(g_ref[:, cols])


def _hgrn(u, row_blk0, n_seq, L, lb_fwd, lb_bwd, norm_d, s0f, s0b, layer):
    hps = HGRN_HEADS_PER_STEP
    width = hps * D_KDIM
    col0 = 3 * C_DIM // width
    groups = D_HEADS // hps
    col = lambda j: pl.BlockSpec((L, width), lambda b, h: (row_blk0 + b, col0 + j * groups + h))
    lbs = pl.BlockSpec((DEPTH, width), lambda b, h: (0, h))
    st = pl.BlockSpec((1, hps, D_KDIM, D_VDIM), lambda b, h: (b, h, 0, 0))
    st_sd = jax.ShapeDtypeStruct((n_seq, D_HEADS, D_KDIM, D_VDIM), F32)
    return pl.pallas_call(
        functools.partial(_hgrn_body, layer=layer),
        grid=(n_seq, groups),
        in_specs=[col(0), col(1), col(2), col(3), col(4), lbs, lbs,
                  pl.BlockSpec((1, D_VDIM), lambda b, h: (0, 0)), st, st],
        out_specs=[pl.BlockSpec((L, width), lambda b, h: (b, h)), st, st],
        out_shape=[jax.ShapeDtypeStruct((n_seq * L, D_HEADS * D_VDIM), F32), st_sd, st_sd],
        compiler_params=_cparams("parallel", "parallel"),
        name="hgrn",
    )(u, u, u, u, u, lb_fwd, lb_bwd, norm_d.reshape(1, D_VDIM), s0f, s0b)


def _pack_bf16_pairs(h):
    n = h.shape[1] // 2
    hi = lax.bitcast_convert_type(h[:, :n].astype(BF16).astype(F32), jnp.int32)
    lo = lax.bitcast_convert_type(h[:, n:].astype(BF16).astype(F32), jnp.int32)
    return hi | lax.shift_right_logical(lo, 16)


def _unpack_bf16_pairs(p):
    hi = lax.bitcast_convert_type(p & jnp.int32(-65536), F32).astype(BF16)
    lo = lax.bitcast_convert_type(lax.shift_left(p, 16), F32).astype(BF16)
    return hi, lo


def _outproj_body(*refs, n_x):
    a_refs, b_refs, x_refs = refs[0:2], refs[2:4], refs[4:4 + n_x]
    mod_ref, gf_ref, w_ref, wrh_ref, wrl_ref, rb_ref, x1_ref, h2_ref, chosen_ref, gk_ref, ik_ref = refs[4 + n_x:]
    m = mod_ref[0]
    half = a_refs[0].shape[1]
    out = _dot(_token_tile(a_refs), w_ref[0:half, :]) + _dot(_token_tile(b_refs), w_ref[half:, :])
    x1 = _token_tile(x_refs) + m[:, 2 * D_MODEL:3 * D_MODEL] * out
    x1_ref[...] = x1
    h2 = _rms(x1, gf_ref[...]) * (1.0 + m[:, 4 * D_MODEL:5 * D_MODEL]) + m[:, 3 * D_MODEL:4 * D_MODEL]
    h2_ref[...] = _pack_bf16_pairs(h2)
    h_hi = h2.astype(BF16)
    h_lo = (h2 - h_hi.astype(F32)).astype(BF16)
    logits = _dot_nt(wrh_ref[...], h_hi) + _dot_nt(wrh_ref[...], h_lo) + _dot_nt(wrl_ref[...], h_hi)
    scores = jax.nn.sigmoid(logits)
    work = scores + rb_ref[...]
    expert = lax.broadcasted_iota(jnp.int32, work.shape, 0).astype(F32)
    slot = lax.broadcasted_iota(jnp.int32, (TOP_K, work.shape[1]), 0)
    chosen = jnp.zeros(work.shape, F32)
    gk = jnp.zeros((TOP_K, work.shape[1]), F32)
    ik = jnp.zeros((TOP_K, work.shape[1]), F32)
    for k in range(TOP_K):
        best = jnp.max(work, axis=0, keepdims=True)
        first = jnp.min(jnp.where(work == best, expert, float(N_EXPERTS)), axis=0, keepdims=True)
        hit = expert == first
        chosen = jnp.where(hit, 1.0, chosen)
        gk = jnp.where(slot == k, jnp.sum(jnp.where(hit, scores, 0.0), axis=0, keepdims=True), gk)
        ik = jnp.where(slot == k, first, ik)
        work = jnp.where(hit, -jnp.inf, work)
    chosen_ref[...] = chosen
    gk_ref[...] = jnp.transpose(gk / jnp.sum(gk, axis=0, keepdims=True) * ROUTE_SCALE)
    ik_ref[...] = ik


def _outproj(a, b, x, mod_l, gain_ffn, w_out_bf16, w_router, router_bias):
    half = a[0].shape[1]
    a_specs, a_args = _token_specs(a, half)
    b_specs, b_args = _token_specs(b, half)
    x_specs, x_args = _token_specs(x, D_MODEL)
    wr_t = w_router.T
    wr_hi = wr_t.astype(BF16)
    wr_lo = (wr_t - wr_hi.astype(F32)).astype(BF16)
    return pl.pallas_call(
        functools.partial(_outproj_body, n_x=len(x_args)),
        grid=(T_ALL // TM,),
        in_specs=a_specs + b_specs + x_specs + [
                  pl.BlockSpec((1, 1, N_MOD * D_MODEL), lambda i: (_mod_row(i), 0, 0)),
                  pl.BlockSpec((1, D_MODEL), lambda i: (0, 0)),
                  pl.BlockSpec((2 * half, D_MODEL), lambda i: (0, 0)),
                  pl.BlockSpec((N_EXPERTS, D_MODEL), lambda i: (0, 0)),
                  pl.BlockSpec((N_EXPERTS, D_MODEL), lambda i: (0, 0)),
                  pl.BlockSpec((N_EXPERTS, 1), lambda i: (0, 0))],
        out_specs=[pl.BlockSpec((TM, D_MODEL), lambda i: (i, 0)),
                   pl.BlockSpec((TM, D_MODEL // 2), lambda i: (i, 0)),
                   pl.BlockSpec((N_EXPERTS, TM), lambda i: (0, i)),
                   pl.BlockSpec((TM, TOP_K), lambda i: (i, 0)),
                   pl.BlockSpec((TOP_K, TM), lambda i: (0, i))],
        out_shape=[jax.ShapeDtypeStruct((T_ALL, D_MODEL), F32),
                   jax.ShapeDtypeStruct((T_ALL, D_MODEL // 2), jnp.int32),
                   jax.ShapeDtypeStruct((N_EXPERTS, T_ALL), F32),
                   jax.ShapeDtypeStruct((T_ALL, TOP_K), F32),
                   jax.ShapeDtypeStruct((TOP_K, T_ALL), F32)],
        compiler_params=_cparams("parallel"),
        name="outproj_router",
    )(*a_args, *b_args, *x_args, mod_l, gain_ffn.reshape(1, D_MODEL), w_out_bf16, wr_hi, wr_lo,
      router_bias.reshape(N_EXPERTS, 1))


def _route_body(chosen_ref, ik_ref, dest_ref, first_ref, count_ref, pos_scr):
    n_tiles = T_ALL // TM
    r = lax.broadcasted_iota(jnp.int32, (TM, TM), 0)
    c = lax.broadcasted_iota(jnp.int32, (TM, TM), 1)
    before = (r < c).astype(BF16)

    counts = jnp.zeros((N_EXPERTS, 1), F32)
    for i in range(n_tiles):
        cols = slice(i * TM, (i + 1) * TM)
        m = chosen_ref[:, cols]
        pos_scr[:, cols] = jnp.dot(m.astype(BF16), before, preferred_element_type=F32) + counts
        counts = counts + jnp.sum(m, axis=1, keepdims=True)
    padded = jnp.ceil(counts * (1.0 / MOE_BLK)) * MOE_BLK
    ei = lax.broadcasted_iota(jnp.int32, (N_EXPERTS, N_EXPERTS), 0)
    ej = lax.broadcasted_iota(jnp.int32, (N_EXPERTS, N_EXPERTS), 1)
    end = _dot_hi((ej <= ei).astype(F32), jnp.broadcast_to(padded, (N_EXPERTS, LANES)))[:, 0:1]
    start = end - padded

    expert = lax.broadcasted_iota(jnp.int32, (N_EXPERTS, TM), 0).astype(F32)
    slot = lax.broadcasted_iota(jnp.int32, (TOP_K, TM), 0)
    for i in range(n_tiles):
        cols = slice(i * TM, (i + 1) * TM)
        row_of = pos_scr[:, cols] + start
        ik = ik_ref[:, cols]
        acc = jnp.zeros((TOP_K, TM), F32)
        for k in range(TOP_K):
            pick = jnp.sum(jnp.where(expert == ik[k:k + 1, :], row_of, 0.0), axis=0, keepdims=True)
            acc = jnp.where(slot == k, pick, acc)
        dest_ref[:, cols] = acc.astype(jnp.int32)
    first_ref[...] = jnp.broadcast_to(start * (1.0 / MOE_BLK), (N_EXPERTS, LANES)).astype(jnp.int32)
    count_ref[...] = jnp.broadcast_to(padded * (1.0 / MOE_BLK), (N_EXPERTS, LANES)).astype(jnp.int32)


def _route(chosen, ik):
    full = lambda shape: pl.BlockSpec(shape, lambda i: (0, 0))
    return pl.pallas_call(
        _route_body,
        grid=(1,),
        in_specs=[full((N_EXPERTS, T_ALL)), full((TOP_K, T_ALL))],
        out_specs=[full((TOP_K, T_ALL)), full((N_EXPERTS, LANES)), full((N_EXPERTS, LANES))],
        out_shape=[jax.ShapeDtypeStruct((TOP_K, T_ALL), jnp.int32),
                   jax.ShapeDtypeStruct((N_EXPERTS, LANES), jnp.int32),
                   jax.ShapeDtypeStruct((N_EXPERTS, LANES), jnp.int32)],
        scratch_shapes=[pltpu.VMEM((N_EXPERTS, T_ALL), F32)],
        compiler_params=_cparams("arbitrary"),
        name="moe_route",
    )(chosen, ik)


def _sc_worker_id():
    return lax.axis_index("s") * SC_CORES + lax.axis_index("c")


def _sc_dispatch(h2p, dest):
    n_chunks = T_ALL // DISP_CHUNK
    k_per = TOP_K // DISP_SPLIT
    items_per_worker = n_chunks * DISP_SPLIT // SC_WORKERS
    chunk_stride = SC_WORKERS // DISP_SPLIT
    width = h2p.shape[1]
    mesh = plsc.VectorSubcoreMesh(core_axis_name="c", subcore_axis_name="s")

    @functools.partial(
        pl.kernel, mesh=mesh,
        out_type=jax.ShapeDtypeStruct((MOE_ROWS, width), jnp.int32),
        scratch_types=[pltpu.VMEM((k_per, DISP_CHUNK), jnp.int32), pltpu.VMEM((DISP_CHUNK, width), jnp.int32)],
    )
    def run(x_hbm, dest_hbm, xs_hbm, idx_v, rows_v):
        wid = _sc_worker_id()
        group = wid % DISP_SPLIT
        for i in range(items_per_worker):
            chunk = i * chunk_stride + wid // DISP_SPLIT
            tokens = pl.ds(pl.multiple_of(chunk * DISP_CHUNK, DISP_CHUNK), DISP_CHUNK)
            pltpu.sync_copy(dest_hbm.at[group, :, tokens], idx_v)
            pltpu.sync_copy(x_hbm.at[tokens], rows_v)
            for k in range(k_per):
                pltpu.sync_copy(rows_v, xs_hbm.at[idx_v.at[k]])

    return run(h2p, dest.reshape(DISP_SPLIT, k_per, T_ALL))


def _sc_collect(y, dest_flat):
    per_worker = T_ALL // SC_WORKERS
    n_chunks = per_worker // COLLECT_CHUNK
    n_steps = TOP_K * n_chunks
    width = y.shape[1]
    mesh = plsc.VectorSubcoreMesh(core_axis_name="c", subcore_axis_name="s")

    @functools.partial(
        pl.kernel, mesh=mesh,
        out_type=jax.ShapeDtypeStruct((TOP_K * T_ALL, width), y.dtype),
        scratch_types=[pltpu.VMEM((TOP_K * per_worker,), jnp.int32),
                       pltpu.VMEM((COLLECT_CHUNK, width), y.dtype), pltpu.VMEM((COLLECT_CHUNK, width), y.dtype),
                       pltpu.SemaphoreType.DMA, pltpu.SemaphoreType.DMA],
    )
    def run(y_hbm, dest_hbm, yg_hbm, idx_v, rows0, rows1, sem0, sem1):
        wid = _sc_worker_id()
        bufs = ((rows0, sem0), (rows1, sem1))
        for k in range(TOP_K):
            pltpu.sync_copy(dest_hbm.at[pl.ds(k * T_ALL + wid * per_worker, per_worker)],
                            idx_v.at[pl.ds(k * per_worker, per_worker)])

        def gather(step, buf):
            rows, sem = buf
            idx = idx_v.at[pl.ds(pl.multiple_of(step * COLLECT_CHUNK, 8), COLLECT_CHUNK)]
            return pltpu.make_async_copy(y_hbm.at[idx], rows, sem)

        def out_rows(step):
            off = (step // n_chunks) * T_ALL + wid * per_worker + (step % n_chunks) * COLLECT_CHUNK
            return yg_hbm.at[pl.ds(pl.multiple_of(off, 8), COLLECT_CHUNK)]

        gather(0, bufs[0]).start()

        @pl.loop(0, n_steps, step=2)
        def _(base):
            for j in range(2):
                step = base + j

                @pl.when(step + 1 < n_steps)
                def _():
                    gather(step + 1, bufs[1 - j]).start()

                gather(step, bufs[j]).wait()
                pltpu.sync_copy(bufs[j][0], out_rows(step))

    return run(y, dest_flat)


def _expert_body(first_ref, count_ref, xs_hbm, wg_ref, wu_ref, wd_ref, y_hbm,
                 wg_bf, wu_bf, wd_bf, x_buf, y_buf, in_sem, out_sem):
    e = pl.program_id(0)
    first = first_ref[e]
    count = count_ref[e]
    n_used = first_ref[N_EXPERTS - 1] + count_ref[N_EXPERTS - 1]
    half = D_MODEL // 2
    wg_bf[...] = wg_ref[0, 0].astype(BF16)
    wu_bf[...] = wu_ref[0, 0].astype(BF16)
    wd_bf[...] = wd_ref[0, 0].astype(BF16)

    def part_rows(g, part, n_parts):
        size = MOE_BLK // n_parts
        return pl.ds(pl.multiple_of(g * MOE_BLK + part * size, size), size), pl.ds(part * size, size)

    def in_copies(g):
        slot = g & (EXPERT_SLOTS - 1)
        out = []
        for part in range(EXPERT_IN_PARTS):
            src, dst = part_rows(g, part, EXPERT_IN_PARTS)
            out.append(pltpu.make_async_copy(xs_hbm.at[src], x_buf.at[slot, dst], in_sem.at[slot]))
        return out

    def out_copies(g):
        slot = g & (EXPERT_SLOTS - 1)
        out = []
        for part in range(EXPERT_OUT_PARTS):
            dst, src = part_rows(g, part, EXPERT_OUT_PARTS)
            out.append(pltpu.make_async_copy(y_buf.at[slot, src], y_hbm.at[dst], out_sem.at[slot]))
        return out

    @pl.when((first == 0) & (count > 0))
    def _():
        for ahead in range(EXPERT_SLOTS - 1):
            @pl.when(ahead < n_used)
            def _():
                for cp in in_copies(ahead):
                    cp.start()

    def block(b, carry):
        g = first + b
        slot = g & (EXPERT_SLOTS - 1)
        for cp in in_copies(g):
            cp.wait()

        @pl.when(g + EXPERT_SLOTS - 1 < n_used)
        def _():
            for cp in in_copies(g + EXPERT_SLOTS - 1):
                cp.start()

        @pl.when(g >= EXPERT_SLOTS)
        def _():
            for cp in out_copies(g - EXPERT_SLOTS):
                cp.wait()

        hi, lo = _unpack_bf16_pairs(x_buf[slot])

        def proj(w_bf):
            return (jnp.dot(hi, w_bf[0:half, :], preferred_element_type=F32)
                    + jnp.dot(lo, w_bf[half:, :], preferred_element_type=F32))

        hid = _silu(proj(wg_bf)) * proj(wu_bf)
        y_buf[slot] = _pack_bf16_pairs(jnp.dot(hid.astype(BF16), wd_bf[...], preferred_element_type=F32))
        for cp in out_copies(g):
            cp.start()
        return carry

    lax.fori_loop(0, count, block, 0)

    @pl.when(e == N_EXPERTS - 1)
    def _():
        for back in range(EXPERT_SLOTS, 0, -1):
            @pl.when(n_used >= back)
            def _():
                for cp in out_copies(n_used - back):
                    cp.wait()


EXPERT_SLOTS = 4
EXPERT_IN_PARTS = 2
EXPERT_OUT_PARTS = 4


def _experts(first_blk, n_blk, xs, layer, w_gate, w_up, w_down):
    w_in = pl.BlockSpec((1, 1, D_MODEL, D_EXPERT), lambda e, first, count: (layer, e, 0, 0))
    grid_spec = pltpu.PrefetchScalarGridSpec(
        num_scalar_prefetch=2,
        grid=(N_EXPERTS,),
        in_specs=[pl.BlockSpec(memory_space=pl.ANY), w_in, w_in,
                  pl.BlockSpec((1, 1, D_EXPERT, D_MODEL), lambda e, first, count: (layer, e, 0, 0))],
        out_specs=pl.BlockSpec(memory_space=pl.ANY),
        scratch_shapes=[pltpu.VMEM((D_MODEL, D_EXPERT), BF16), pltpu.VMEM((D_MODEL, D_EXPERT), BF16),
                        pltpu.VMEM((D_EXPERT, D_MODEL), BF16),
                        pltpu.VMEM((EXPERT_SLOTS, MOE_BLK, D_MODEL // 2), jnp.int32),
                        pltpu.VMEM((EXPERT_SLOTS, MOE_BLK, D_MODEL // 2), jnp.int32),
                        pltpu.SemaphoreType.DMA((EXPERT_SLOTS,)), pltpu.SemaphoreType.DMA((EXPERT_SLOTS,))],
    )
    return pl.pallas_call(
        _expert_body,
        grid_spec=grid_spec,
        out_shape=jax.ShapeDtypeStruct((MOE_ROWS, D_MODEL // 2), jnp.int32),
        compiler_params=_cparams("arbitrary"),
        name="moe_experts",
    )(first_blk, n_blk, xs, w_gate, w_up, w_down)


def _combine_body(x1_ref, h2_ref, yg_ref, gk_ref, mod_ref, sg_ref, su_ref, sd_ref, fn_ref, *o_refs, final):
    hi, lo = _unpack_bf16_pairs(h2_ref[...])
    half = D_MODEL // 2

    def proj(w_ref):
        return _dot(hi, w_ref[0:half, :]) + _dot(lo, w_ref[half:, :])

    shared = _dot(_silu(proj(sg_ref)) * proj(su_ref), sd_ref[...])
    acc_hi, acc_lo = shared[:, :half], shared[:, half:]
    gk = gk_ref[...]
    for k in range(TOP_K):
        y_hi, y_lo = _unpack_bf16_pairs(yg_ref[k])
        acc_hi = acc_hi + gk[:, k:k + 1] * y_hi.astype(F32)
        acc_lo = acc_lo + gk[:, k:k + 1] * y_lo.astype(F32)
    acc = jnp.concatenate([acc_hi, acc_lo], axis=1)
    m = mod_ref[0]
    y = x1_ref[...] + m[:, 5 * D_MODEL:6 * D_MODEL] * acc
    if not final:
        o_refs[0][...] = y
        return
    y = _rms(y, fn_ref[...])
    is_ctx = pl.program_id(0) < N_CTX_TILES

    @pl.when(is_ctx)
    def _():
        o_refs[0][...] = y

    @pl.when(jnp.logical_not(is_ctx))
    def _():
        o_refs[1][...] = y


def _combine(x1, h2p, yg, gk, mod_l, ws_gate, ws_up, ws_down, final_norm, final):
    tok = lambda shape: pl.BlockSpec(shape, lambda i: (i, 0))
    full = lambda shape: pl.BlockSpec(shape, lambda i: (0, 0))
    if final:
        out_specs, _ = _token_specs((None, None), D_MODEL)
        out_shape = [jax.ShapeDtypeStruct((T_CTX, D_MODEL), F32), jax.ShapeDtypeStruct((T_LAT, D_MODEL), F32)]
    else:
        out_specs = tok((TM, D_MODEL))
        out_shape = jax.ShapeDtypeStruct((T_ALL, D_MODEL), F32)
    return pl.pallas_call(
        functools.partial(_combine_body, final=final),
        grid=(T_ALL // TM,),
        in_specs=[tok((TM, D_MODEL)), tok((TM, D_MODEL // 2)),
                  pl.BlockSpec((TOP_K, TM, D_MODEL // 2), lambda i: (0, i, 0)),
                  tok((TM, TOP_K)),
                  pl.BlockSpec((1, 1, N_MOD * D_MODEL), lambda i: (_mod_row(i), 0, 0)),
                  full((D_MODEL, D_EXPERT)), full((D_MODEL, D_EXPERT)), full((D_EXPERT, D_MODEL)),
                  full((1, D_MODEL))],
        out_specs=out_specs,
        out_shape=out_shape,
        compiler_params=_cparams("arbitrary"),
        name="moe_combine",
    )(x1, h2p, yg, gk, mod_l, ws_gate, ws_up, ws_down, final_norm.reshape(1, D_MODEL))


def _moe(x1, h2p, chosen, gk, ik, mod_l, layer, w_gate, w_up, w_down, ws_gate, ws_up, ws_down, final_norm, final):
    dest, first_blk, n_blk = _route(chosen, ik)
    xs = _sc_dispatch(h2p, dest)
    y = _experts(first_blk[:, 0], n_blk[:, 0], xs, layer, w_gate, w_up, w_down)
    yg = _sc_collect(y, dest.reshape(-1)).reshape(TOP_K, T_ALL, D_MODEL // 2)
    return _combine(x1, h2p, yg, gk, mod_l, ws_gate.astype(BF16), ws_up.astype(BF16), ws_down.astype(BF16),
                    final_norm, final)


def kernel(x_prompt, x_sample, cache_a_k, cache_a_v, cache_b_k, cache_b_v, state_d_fwd, state_d_bwd, c, c_ctx, w_ada, b_ada, norm_mix, norm_ffn, w_in_attn, w_out_attn, sink_a, rpb_b, w_in_rec, w_out_rec, conv_w, conv_b, filt_w1, filt_b1, filt_w2, filt_b2, filt_w3, filt_b3, filt_freq, filt_w4, d_skip, lb_fwd, lb_bwd, norm_d, w_router, router_bias, w_gate, w_up, w_down, ws_gate, ws_up, ws_down, final_norm):
    x = (x_prompt.reshape(T_CTX, D_MODEL), x_sample.reshape(T_LAT, D_MODEL))
    cvec = jnp.concatenate([c_ctx[None, :], c], axis=0)
    mod = _ada(cvec, w_ada, b_ada).reshape(DEPTH, CVEC_PAD, 1, N_MOD * D_MODEL)

    new_kv = None
    new_state = None
    for l in range(DEPTH):
        j = l // 2
        final = l == DEPTH - 1
        if l % 2 == 0:
            qkv = _inproj(x, mod[l], norm_mix[l], w_in_attn[j].astype(BF16))
            oa_ctx, ob_ctx, *new_kv = _ctx_attn(qkv, sink_a[j])
            new_kv = tuple(new_kv)
            q_rot, k_rot = _rope(qkv)
            cache = lambda t: t[:, j].reshape(DEC_BATCH, PAST_LEN, -1)
            oa_lat = _win_attn(qkv, q_rot, k_rot, cache(cache_a_k), cache(cache_a_v), sink_a[j])
            ob_lat = _na_attn(qkv, cache(cache_b_k), cache(cache_b_v), _na_rel_rows(rpb_b[j]))
            mix_a = (oa_ctx, oa_lat)
            mix_b = (ob_ctx, ob_lat)
            w_out = w_out_attn[j]
        else:
            u = _inproj(x, mod[l], norm_mix[l], w_in_rec[j].astype(BF16))
            filt = (filt_w1[j], filt_b1[j], filt_w2[j], filt_b2[j], filt_w3[j], filt_b3[j], filt_freq[j],
                    filt_w4[j])
            y_ctx = _hyena(u, 0, BATCH, SEQ, conv_w[j], conv_b[j], d_skip[j], _hyena_filter(SEQ, filt))
            y_lat = _hyena(u, T_CTX // DEC_SEQ, DEC_BATCH, DEC_SEQ, conv_w[j], conv_b[j], d_skip[j],
                           _hyena_filter(DEC_SEQ, filt))
            zeros = jnp.zeros((BATCH, D_HEADS, D_KDIM, D_VDIM), F32)
            o_ctx, s_f, s_b = _hgrn(u, 0, BATCH, SEQ, lb_fwd, lb_bwd, norm_d[j], zeros, zeros, l)
            o_lat, _, _ = _hgrn(u, T_CTX // DEC_SEQ, DEC_BATCH, DEC_SEQ, lb_fwd, lb_bwd, norm_d[j],
                                state_d_fwd[:, j], state_d_bwd[:, j], l)
            new_state = (s_f[:, None], s_b[:, None])
            mix_a = (y_ctx, y_lat)
            mix_b = (o_ctx, o_lat)
            w_out = w_out_rec[j]
        x1, h2p, chosen, gk, ik = _outproj(mix_a, mix_b, x, mod[l], norm_ffn[l], w_out.astype(BF16), w_router[l],
                                           router_bias[l])
        x = _moe(x1, h2p, chosen, gk, ik, mod[l], l, w_gate, w_up, w_down, ws_gate[l], ws_up[l],
                 ws_down[l], final_norm, final)

    y_prompt = x[0].reshape(BATCH, SEQ, D_MODEL)
    y_sample = x[1].reshape(DEC_BATCH, DEC_SEQ, D_MODEL)
    return (y_prompt, y_sample) + new_kv + new_state
```

```python
import functools
import math

import numpy as np
import jax
import jax.numpy as jnp
from jax import lax
from jax.experimental import pallas as pl
from jax.experimental.pallas import tpu as pltpu
from jax.experimental.pallas import tpu_sc as plsc

F32 = jnp.float32
BF16 = jnp.bfloat16
HI = lax.Precision.HIGHEST

D_MODEL = 1024
BATCH = 16
SEQ = 256
DEPTH = 2
DEC_BATCH = 2
DEC_SEQ = 1024
PAST_LEN = 512
GRID_W = 64
HEAD_DIM = 64
N_MOD = 6
RMS_EPS = 1e-6
A_HEADS = 8
A_KV_HEADS = 2
A_GROUP = A_HEADS // A_KV_HEADS
WINDOW = 128
ROPE_BASE = 10000.0
B_HEADS = 8
NA_ROWS = 8
NA_COLS = 16
C_DIM = 512
C_EMB = 33
C_FFN = 64
HYENA_MIN_DECAY = math.log(1e-2) / 1.5
HYENA_MAX_DECAY = math.log(1e-2) / 0.3
D_KDIM = 128
D_VDIM = 128
D_HEADS = 4
N_EXPERTS = 64
TOP_K = 8
D_EXPERT = 256
ROUTE_SCALE = 2.5
A_Q = A_HEADS * HEAD_DIM
A_KV = A_KV_HEADS * HEAD_DIM
B_W = B_HEADS * HEAD_DIM
ATTN_IN = A_Q + 2 * A_KV + 3 * B_W
REC_IN = 3 * C_DIM + 5 * D_HEADS * D_KDIM

T_CTX = BATCH * SEQ
T_LAT = DEC_BATCH * DEC_SEQ
T_ALL = T_CTX + T_LAT
N_CVEC = 1 + DEC_BATCH
CVEC_PAD = 8
TM = 512
MASK_NEG = -1e30
GLA_CHUNK = 64
GLA_SPAN = 256
HGRN_HEADS_PER_STEP = 4
DFT_CHUNK = 256
MOE_BLK = 512
MOE_NBLK = -(-(T_ALL * TOP_K + N_EXPERTS * (MOE_BLK - 1)) // MOE_BLK)
MOE_ROWS = MOE_NBLK * MOE_BLK
SC_CORES = 2
SC_SUBCORES = 16
SC_WORKERS = SC_CORES * SC_SUBCORES
DISP_CHUNK = 128
DISP_SPLIT = 2
COLLECT_CHUNK = 64
VMEM_LIMIT = 56 * 1024 * 1024


def _cparams(*sem):
    return pltpu.CompilerParams(dimension_semantics=sem, vmem_limit_bytes=VMEM_LIMIT)


def _mod_row(i):
    return jnp.where(i < T_CTX // TM, 0, 1 + (i - T_CTX // TM) // (DEC_SEQ // TM))


def _dot(a, b):
    return jnp.dot(a.astype(BF16), b.astype(BF16), preferred_element_type=F32)


def _dot_nt(a, b):
    return lax.dot_general(a.astype(BF16), b.astype(BF16), (((1,), (1,)), ((), ())),
                           preferred_element_type=F32)


def _dot_tn(a, b):
    return lax.dot_general(a.astype(BF16), b.astype(BF16), (((0,), (0,)), ((), ())),
                           preferred_element_type=F32)


def _dot_hi(a, b):
    return jnp.dot(a, b, precision=HI, preferred_element_type=F32)


def _split_bf16(x):
    hi = x.astype(BF16)
    return hi, (x - hi.astype(F32)).astype(BF16)


def _dot_split(a, b):
    a_hi, a_lo = _split_bf16(a)
    b_hi, b_lo = _split_bf16(b)
    dot = lambda x, y: jnp.dot(x, y, preferred_element_type=F32)
    return dot(a_hi, b_hi) + dot(a_hi, b_lo) + dot(a_lo, b_hi)


def _silu(x):
    return x * jax.nn.sigmoid(x)


def _rms(x, g):
    return x * lax.rsqrt(jnp.mean(x * x, axis=-1, keepdims=True) + RMS_EPS) * g


ADA_TN = 1536
ADA_UNROLL = 4


def _ada_body(cb_ref, w_ref, b_ref, o_ref):
    tn = o_ref.shape[-1]
    n_slab = tn // LANES

    def step(k8, accs):
        r0 = pl.multiple_of(k8 * 8, 8)
        sk = [_silu(cb_ref[j, pl.ds(r0, 8), :]) for j in range(N_CVEC)]
        out = []
        for s in range(n_slab):
            wk = w_ref[0, pl.ds(r0, 8), s * LANES:(s + 1) * LANES]
            out.extend(accs[s * N_CVEC + j] + wk * sk[j] for j in range(N_CVEC))
        return tuple(out)

    accs = lax.fori_loop(0, D_MODEL // 8, step,
                         tuple(jnp.zeros((8, LANES), F32) for _ in range(n_slab * N_CVEC)), unroll=ADA_UNROLL)
    o_ref[0] = jnp.zeros((CVEC_PAD, tn), F32)
    for s in range(n_slab):
        for j in range(N_CVEC):
            o_ref[0, j:j + 1, s * LANES:(s + 1) * LANES] = (
                jnp.sum(accs[s * N_CVEC + j], axis=0, keepdims=True) + b_ref[0, :, s * LANES:(s + 1) * LANES])


def _ada(cvec, w_ada, b_ada):
    n_out = N_MOD * D_MODEL
    c_lanes = jnp.broadcast_to(cvec[:, :, None], (N_CVEC, D_MODEL, LANES))
    return pl.pallas_call(
        _ada_body,
        grid=(DEPTH, n_out // ADA_TN),
        in_specs=[pl.BlockSpec((N_CVEC, D_MODEL, LANES), lambda l, n: (0, 0, 0)),
                  pl.BlockSpec((1, D_MODEL, ADA_TN), lambda l, n: (l, 0, n)),
                  pl.BlockSpec((1, 1, ADA_TN), lambda l, n: (l, 0, n))],
        out_specs=pl.BlockSpec((1, CVEC_PAD, ADA_TN), lambda l, n: (l, 0, n)),
        out_shape=jax.ShapeDtypeStruct((DEPTH, CVEC_PAD, n_out), F32),
        compiler_params=_cparams("parallel", "parallel"),
        name="ada",
    )(c_lanes, w_ada, b_ada.reshape(DEPTH, 1, n_out))


N_CTX_TILES = T_CTX // TM


def _token_specs(x, width):
    if not isinstance(x, tuple):
        return [pl.BlockSpec((TM, width), lambda i: (i, 0))], (x,)
    return ([pl.BlockSpec((TM, width), lambda i: (jnp.minimum(i, N_CTX_TILES - 1), 0)),
             pl.BlockSpec((TM, width), lambda i: (jnp.maximum(i - N_CTX_TILES, 0), 0))], x)


def _token_tile(refs):
    if len(refs) == 1:
        return refs[0][...]
    return jnp.where(pl.program_id(0) < N_CTX_TILES, refs[0][...], refs[1][...])


def _inproj_body(*refs, n_x):
    x_refs, (mod_ref, g_ref, w_ref, o_ref, w_bf) = refs[:n_x], refs[n_x:]

    @pl.when(pl.program_id(0) == 0)
    def _():
        w_bf[...] = w_ref[...].astype(BF16)

    m = mod_ref[0]
    h = _rms(_token_tile(x_refs), g_ref[...]) * (1.0 + m[:, D_MODEL:2 * D_MODEL]) + m[:, 0:D_MODEL]
    o_ref[...] = _dot(h, w_bf[...])


def _inproj(x, mod_l, gain, w):
    n = w.shape[1]
    x_specs, x_args = _token_specs(x, D_MODEL)
    return pl.pallas_call(
        functools.partial(_inproj_body, n_x=len(x_args)),
        grid=(T_ALL // TM,),
        in_specs=x_specs + [pl.BlockSpec((1, 1, N_MOD * D_MODEL), lambda i: (_mod_row(i), 0, 0)),
                            pl.BlockSpec((1, D_MODEL), lambda i: (0, 0)),
                            pl.BlockSpec((D_MODEL, n), lambda i: (0, 0), pipeline_mode=pl.Buffered(1))],
        out_specs=pl.BlockSpec((TM, n), lambda i: (i, 0)),
        out_shape=jax.ShapeDtypeStruct((T_ALL, n), F32),
        scratch_shapes=[pltpu.VMEM((D_MODEL, n), BF16)],
        compiler_params=_cparams("arbitrary"),
        name="inproj",
    )(*x_args, mod_l, gain.reshape(1, D_MODEL), w)


def _head_cols(h):
    return slice(h * HEAD_DIM, (h + 1) * HEAD_DIM)


def _group_rows(ref, rows, first_col, sink_ref, hk):
    n = rows.stop - rows.start
    q = jnp.concatenate([ref[rows, first_col + g * HEAD_DIM:first_col + (g + 1) * HEAD_DIM]
                         for g in range(A_GROUP)], axis=0)
    sink = jnp.concatenate([jnp.broadcast_to(sink_ref[:, hk * A_GROUP + g:hk * A_GROUP + g + 1], (n, 1))
                            for g in range(A_GROUP)], axis=0)
    return q, sink


def _ctx_attn_body(qkv_ref, sink_ref, oa_ref, ob_ref, ak_ref, av_ref, bk_ref, bv_ref):
    scale = HEAD_DIM ** -0.5
    lane = lax.broadcasted_iota(jnp.int32, (SEQ, LANES), 1)
    in_half = [lane < HEAD_DIM, lane >= HEAD_DIM]

    def attend(q, k, v, sink):
        s = _dot_nt(q, k) * scale
        m = jnp.max(s, axis=-1, keepdims=True)
        if sink is not None:
            m = jnp.maximum(m, sink)
        p = jnp.exp(s - m)
        den = jnp.sum(p, axis=-1, keepdims=True)
        if sink is not None:
            den = den + jnp.exp(sink - m)
        return _dot(p, v) / den

    def tile(first_col, t):
        return qkv_ref[:, first_col + t * LANES:first_col + (t + 1) * LANES]

    base = A_Q + 2 * A_KV
    for hk in range(A_KV_HEADS):
        dst = pl.ds(hk, SEQ, stride=A_KV_HEADS)
        ak_ref[0, dst, :] = qkv_ref[:, A_Q + hk * HEAD_DIM:A_Q + (hk + 1) * HEAD_DIM]
        av_ref[0, dst, :] = qkv_ref[:, A_Q + A_KV + hk * HEAD_DIM:A_Q + A_KV + (hk + 1) * HEAD_DIM]
    for h in range(B_HEADS):
        dst = pl.ds(h, SEQ, stride=B_HEADS)
        bk_ref[0, dst, :] = qkv_ref[:, base + B_W + h * HEAD_DIM:base + B_W + (h + 1) * HEAD_DIM]
        bv_ref[0, dst, :] = qkv_ref[:, base + 2 * B_W + h * HEAD_DIM:base + 2 * B_W + (h + 1) * HEAD_DIM]

    k_t, v_t = tile(A_Q, 0), tile(A_Q + A_KV, 0)
    k_sw, v_sw = pltpu.roll(k_t, HEAD_DIM, axis=1), pltpu.roll(v_t, HEAD_DIM, axis=1)
    tiles_per_kv = A_GROUP // HEADS_PER_TILE
    for hk in range(A_KV_HEADS):
        q_tiles = [tile(0, hk * tiles_per_kv + j) for j in range(tiles_per_kv)]
        halves = []
        for p in range(HEADS_PER_TILE):
            q = jnp.concatenate([jnp.where(in_half[p], qt, 0.0) for qt in q_tiles], axis=0)
            heads = [(hk * tiles_per_kv + j) * HEADS_PER_TILE + p for j in range(tiles_per_kv)]
            sink = jnp.concatenate([jnp.broadcast_to(sink_ref[:, h:h + 1], (SEQ, 1)) for h in heads], axis=0)
            halves.append(attend(q, k_t if p == hk else k_sw, v_t if p == hk else v_sw, sink))
        first_half = lax.broadcasted_iota(jnp.int32, halves[0].shape, 1) < HEAD_DIM
        o = jnp.where(first_half, halves[0], halves[1])
        for j in range(tiles_per_kv):
            t = hk * tiles_per_kv + j
            oa_ref[:, t * LANES:(t + 1) * LANES] = o[j * SEQ:(j + 1) * SEQ]

    for t in range(B_HEADS // HEADS_PER_TILE):
        q_t, k_b, v_b = tile(base, t), tile(base + B_W, t), tile(base + 2 * B_W, t)
        halves = [attend(jnp.where(in_half[p], q_t, 0.0), k_b, v_b, None) for p in range(HEADS_PER_TILE)]
        ob_ref[:, t * LANES:(t + 1) * LANES] = jnp.where(in_half[0], halves[0], halves[1])


def _ctx_attn(qkv, sink):
    kv_spec = lambda heads: pl.BlockSpec((1, SEQ * heads, HEAD_DIM), lambda b: (b, 0, 0))
    kv_sd = lambda heads: jax.ShapeDtypeStruct((BATCH, SEQ * heads, HEAD_DIM), F32)
    outs = pl.pallas_call(
        _ctx_attn_body,
        grid=(BATCH,),
        in_specs=[pl.BlockSpec((SEQ, ATTN_IN), lambda b: (b, 0)),
                  pl.BlockSpec((1, A_HEADS), lambda b: (0, 0))],
        out_specs=[pl.BlockSpec((SEQ, A_Q), lambda b: (b, 0)), pl.BlockSpec((SEQ, B_W), lambda b: (b, 0)),
                   kv_spec(A_KV_HEADS), kv_spec(A_KV_HEADS), kv_spec(B_HEADS), kv_spec(B_HEADS)],
        out_shape=[jax.ShapeDtypeStruct((T_CTX, A_Q), F32), jax.ShapeDtypeStruct((T_CTX, B_W), F32),
                   kv_sd(A_KV_HEADS), kv_sd(A_KV_HEADS), kv_sd(B_HEADS), kv_sd(B_HEADS)],
        compiler_params=_cparams("parallel"),
        name="ctx_attn",
    )(qkv, sink.reshape(1, A_HEADS))
    caches = [t.reshape(BATCH, 1, SEQ, -1, HEAD_DIM) for t in outs[2:]]
    return outs[0], outs[1], *caches


@functools.lru_cache(maxsize=None)
def _rope_tables(width):
    half = HEAD_DIM // 2
    t = np.arange(DEC_SEQ)
    inv = ROPE_BASE ** (-np.arange(0, half, 2, dtype=np.float64) / half)
    ang_r = (t // GRID_W)[:, None] * inv[None, :]
    ang_c = (t % GRID_W)[:, None] * inv[None, :]
    cos = np.concatenate([np.cos(ang_r)] * 2 + [np.cos(ang_c)] * 2, axis=-1)
    sin = np.concatenate([-np.sin(ang_r), np.sin(ang_r), -np.sin(ang_c), np.sin(ang_c)], axis=-1)
    reps = width // HEAD_DIM
    return (np.tile(cos, (1, reps)).astype(np.float32), np.tile(sin, (1, reps)).astype(np.float32))


def _rope_body(q_ref, k_ref, cq_ref, sq_ref, ck_ref, sk_ref, qo_ref, ko_ref):
    quarter = HEAD_DIM // 4

    def rot(x, cos, sin):
        w = x.shape[-1]
        lane = lax.broadcasted_iota(jnp.int32, x.shape, 1)
        fwd = pltpu.roll(x, w - quarter, axis=1)
        bwd = pltpu.roll(x, quarter, axis=1)
        partner = jnp.where((lane & (2 * quarter - 1)) < quarter, fwd, bwd)
        return x * cos + partner * sin

    qo_ref[...] = rot(q_ref[...], cq_ref[...], sq_ref[...])
    ko_ref[...] = rot(k_ref[...], ck_ref[...], sk_ref[...])


def _rope(qkv):
    cq, sq = _rope_tables(A_Q)
    ck, sk = _rope_tables(A_KV)
    tab = lambda w: pl.BlockSpec((DEC_SEQ, w), lambda b: (0, 0))
    row0 = T_CTX // DEC_SEQ
    return pl.pallas_call(
        _rope_body,
        grid=(DEC_BATCH,),
        in_specs=[pl.BlockSpec((DEC_SEQ, A_Q), lambda b: (row0 + b, 0)),
                  pl.BlockSpec((DEC_SEQ, A_KV), lambda b: (row0 + b, A_Q // A_KV)),
                  tab(A_Q), tab(A_Q), tab(A_KV), tab(A_KV)],
        out_specs=[pl.BlockSpec((DEC_SEQ, A_Q), lambda b: (b, 0)),
                   pl.BlockSpec((DEC_SEQ, A_KV), lambda b: (b, 0))],
        out_shape=[jax.ShapeDtypeStruct((T_LAT, A_Q), F32), jax.ShapeDtypeStruct((T_LAT, A_KV), F32)],
        compiler_params=_cparams("parallel"),
        name="rope",
    )(qkv, qkv, jnp.asarray(cq), jnp.asarray(sq), jnp.asarray(ck), jnp.asarray(sk))


WIN_QB = 256


def _pick_head(x, h, n_heads):
    out = x[:, _head_cols(0)]
    for i in range(1, n_heads):
        out = jnp.where(h == i, x[:, _head_cols(i)], out)
    return out


def _win_attn_body(qraw_ref, qrot_ref, krot_ref, v_ref, kc_ref, vc_ref, sink_ref, o_ref):
    scale = HEAD_DIM ** -0.5
    hk = pl.program_id(1)
    tiles = A_GROUP // HEADS_PER_TILE

    def kv_in_half(x):
        swapped = pltpu.roll(x, HEAD_DIM, axis=1)
        return [jnp.where(hk == p, x, swapped) for p in range(HEADS_PER_TILE)]

    k, v, kc, vc = kv_in_half(krot_ref[...]), kv_in_half(v_ref[...]), kv_in_half(kc_ref[0]), kv_in_half(vc_ref[0])
    head_lane = lax.broadcasted_iota(jnp.int32, (1, A_HEADS), 1)

    def sink_rows(p):
        heads = [hk * A_GROUP + j * HEADS_PER_TILE + p for j in range(tiles)]
        vals = [jnp.sum(jnp.where(head_lane == h, sink_ref[...], 0.0), axis=-1, keepdims=True) for h in heads]
        return jnp.concatenate([jnp.broadcast_to(s, (WIN_QB, 1)) for s in vals], axis=0)

    sinks = [sink_rows(p) for p in range(HEADS_PER_TILE)]
    lane = lax.broadcasted_iota(jnp.int32, (tiles * WIN_QB, LANES), 1)
    in_half = [lane < HEAD_DIM, lane >= HEAD_DIM]
    for qb in range(DEC_SEQ // WIN_QB):
        q0 = qb * WIN_QB
        rows = slice(q0, q0 + WIN_QB)
        lo = max(0, q0 - WINDOW)
        hi = min(DEC_SEQ, q0 + WIN_QB + WINDOW)
        q_rot = jnp.concatenate([qrot_ref[rows, j * LANES:(j + 1) * LANES] for j in range(tiles)], axis=0)
        q_raw = jnp.concatenate([qraw_ref[rows, j * LANES:(j + 1) * LANES] for j in range(tiles)], axis=0)
        halves = []
        for p in range(HEADS_PER_TILE):
            s_loc = _dot_nt(jnp.where(in_half[p], q_rot, 0.0), k[p][lo:hi]) * scale
            qpos = q0 + (lax.broadcasted_iota(jnp.int32, s_loc.shape, 0) & (WIN_QB - 1))
            kpos = lo + lax.broadcasted_iota(jnp.int32, s_loc.shape, 1)
            s_loc = jnp.where(jnp.abs(kpos - qpos) <= WINDOW, s_loc, MASK_NEG)
            s_ctx = _dot_nt(jnp.where(in_half[p], q_raw, 0.0), kc[p]) * scale
            m = jnp.maximum(jnp.maximum(jnp.max(s_loc, axis=-1, keepdims=True),
                                        jnp.max(s_ctx, axis=-1, keepdims=True)), sinks[p])
            p_loc = jnp.exp(s_loc - m)
            p_ctx = jnp.exp(s_ctx - m)
            den = (jnp.sum(p_loc, axis=-1, keepdims=True) + jnp.sum(p_ctx, axis=-1, keepdims=True)
                   + jnp.exp(sinks[p] - m))
            halves.append((_dot(p_ctx, vc[p]) + _dot(p_loc, v[p][lo:hi])) / den)
        o = jnp.where(in_half[0], halves[0], halves[1])
        for j in range(tiles):
            o_ref[rows, j * LANES:(j + 1) * LANES] = o[j * WIN_QB:(j + 1) * WIN_QB]


def _win_attn(qkv, q_rot, k_rot, kc, vc, sink):
    row0 = T_CTX // DEC_SEQ
    gw = A_GROUP * HEAD_DIM
    return pl.pallas_call(
        _win_attn_body,
        grid=(DEC_BATCH, A_KV_HEADS),
        in_specs=[pl.BlockSpec((DEC_SEQ, gw), lambda b, h: (row0 + b, h)),
                  pl.BlockSpec((DEC_SEQ, gw), lambda b, h: (b, h)),
                  pl.BlockSpec((DEC_SEQ, A_KV), lambda b, h: (b, 0)),
                  pl.BlockSpec((DEC_SEQ, A_KV), lambda b, h: (row0 + b, (A_Q + A_KV) // A_KV)),
                  pl.BlockSpec((1, PAST_LEN, A_KV), lambda b, h: (b, 0, 0)),
                  pl.BlockSpec((1, PAST_LEN, A_KV), lambda b, h: (b, 0, 0)),
                  pl.BlockSpec((1, A_HEADS), lambda b, h: (0, 0))],
        out_specs=pl.BlockSpec((DEC_SEQ, gw), lambda b, h: (b, h)),
        out_shape=jax.ShapeDtypeStruct((T_LAT, A_Q), F32),
        compiler_params=_cparams("parallel", "parallel"),
        name="win_attn",
    )(qkv, q_rot, k_rot, qkv, kc, vc, sink.reshape(1, A_HEADS))


GRID_ROWS = DEC_SEQ // GRID_W
NA_BAND = min(NA_ROWS, GRID_ROWS)


NA_REL_ROWS = 2 * NA_ROWS - 1
NA_REL_COLS = 2 * NA_COLS - 1
LANES = 128
HEADS_PER_TILE = LANES // HEAD_DIM


def _na_rel_rows(rpb):
    pad = jnp.zeros((B_HEADS, NA_REL_ROWS, GRID_W - NA_REL_COLS), F32)
    one = jnp.concatenate([rpb, pad], axis=-1)
    nxt = jnp.concatenate([one[:, 1:], jnp.zeros((B_HEADS, 1, GRID_W), F32)], axis=1)
    both = jnp.concatenate([one, nxt], axis=-1)
    return jnp.concatenate([both, jnp.zeros((B_HEADS, 16 - NA_REL_ROWS, LANES), F32)], axis=1)


NA_HEADS_PER_STEP = LANES // HEAD_DIM


def _na_row_groups():
    groups = []
    for r in range(GRID_ROWS):
        rs = min(max(r - NA_ROWS // 2, 0), GRID_ROWS - NA_BAND)
        if groups and groups[-1][2] == rs:
            groups[-1][1] += 1
        else:
            groups.append([r, 1, rs])
    return groups


def _na_attn_body(q_ref, k_ref, v_ref, kc_ref, vc_ref, rel_ref, o_ref):
    scale = HEAD_DIM ** -0.5
    cq = lax.broadcasted_iota(jnp.int32, (GRID_W, LANES), 0)
    kcol = lax.broadcasted_iota(jnp.int32, (GRID_W, LANES), 1) & (GRID_W - 1)
    cs = jnp.clip(cq - NA_COLS // 2, 0, GRID_W - NA_COLS)
    col_ok = (kcol >= cs) & (kcol < cs + NA_COLS)
    kc = kc_ref[0]
    vc = vc_ref[0]
    tiles = {}

    def pair_tile(hh, a):
        if (hh, a) not in tiles:
            x = jnp.broadcast_to(rel_ref[hh, a:a + 1, :], (GRID_W, LANES))
            t = pltpu.roll(x, LANES - (NA_COLS - 1), axis=1, stride=1, stride_axis=0)
            tiles[hh, a] = jnp.where(col_ok, t, MASK_NEG)
        return tiles[hh, a]

    for r0, n_r, rs in _na_row_groups():
        rows = slice(r0 * GRID_W, (r0 + n_r) * GRID_W)
        band = slice(rs * GRID_W, (rs + NA_BAND) * GRID_W)
        q_t, k_t, v_t = q_ref[rows, :], k_ref[band, :], v_ref[band, :]
        head_of_lane = lax.broadcasted_iota(jnp.int32, q_t.shape, 1) >> (HEAD_DIM.bit_length() - 1)
        o = jnp.zeros(q_t.shape, F32)
        for hh in range(NA_HEADS_PER_STEP):
            bias = jnp.concatenate(
                [jnp.concatenate([pair_tile(hh, rs - r + NA_ROWS - 1 + 2 * i) for i in range(NA_BAND // 2)], axis=1)
                 for r in range(r0, r0 + n_r)], axis=0)
            q = jnp.where(head_of_lane == hh, q_t, 0.0)
            s_loc = _dot_nt(q, k_t) * scale + bias
            s_ctx = _dot_nt(q, kc) * scale
            m = jnp.maximum(jnp.max(s_loc, axis=-1, keepdims=True), jnp.max(s_ctx, axis=-1, keepdims=True))
            p_loc = jnp.exp(s_loc - m)
            p_ctx = jnp.exp(s_ctx - m)
            den = jnp.sum(p_loc, axis=-1, keepdims=True) + jnp.sum(p_ctx, axis=-1, keepdims=True)
            o = jnp.where(head_of_lane == hh, (_dot(p_ctx, vc) + _dot(p_loc, v_t)) / den, o)
        o_ref[rows, :] = o


def _na_attn(qkv, kc, vc, rel):
    row0 = T_CTX // DEC_SEQ
    col0 = (A_Q + 2 * A_KV) // LANES
    n_blk = B_W // LANES
    col = lambda j: pl.BlockSpec((DEC_SEQ, LANES), lambda b, p: (row0 + b, col0 + j * n_blk + p))
    cache = pl.BlockSpec((1, PAST_LEN, LANES), lambda b, p: (b, 0, p))
    return pl.pallas_call(
        _na_attn_body,
        grid=(DEC_BATCH, n_blk),
        in_specs=[col(0), col(1), col(2), cache, cache,
                  pl.BlockSpec((NA_HEADS_PER_STEP, 16, LANES), lambda b, p: (p, 0, 0))],
        out_specs=pl.BlockSpec((DEC_SEQ, LANES), lambda b, p: (b, p)),
        out_shape=jax.ShapeDtypeStruct((T_LAT, B_W), F32),
        compiler_params=_cparams("parallel", "parallel"),
        name="na_attn",
    )(qkv, qkv, qkv, kc, vc, rel)


@functools.lru_cache(maxsize=None)
def _dft_mats(L):
    n = 2 * L
    fc = min(L, DFT_CHUNK)
    f = np.arange(L)[:, None]
    t = np.arange(L)[None, :]
    ang = 2.0 * np.pi * ((f * t) % n) / n
    m1 = np.cos(ang)
    m2 = np.sin(ang)
    m2[0, :] = np.where(np.arange(L) % 2 == 0, 1.0, -1.0)
    wgt = np.full((L, 1), 2.0)
    wgt[0, 0] = 1.0
    nch = L // fc
    fwd = np.concatenate([m1.reshape(nch, fc, L), m2.reshape(nch, fc, L)], axis=1)
    inv = np.concatenate([(m1 * wgt / n).reshape(nch, fc, L), (m2 * wgt / n).reshape(nch, fc, L)], axis=1)
    inv = np.transpose(inv, (0, 2, 1))
    return fwd.astype(np.float32), inv.astype(np.float32)


@functools.lru_cache(maxsize=None)
def _filter_consts(L):
    t = np.linspace(0.0, 1.0, L)[:, None]
    bands = (C_EMB - 1) // 2
    ang = (2.0 * math.pi / L) * np.arange(L)[:, None] * np.linspace(1e-4, bands - 1, bands)[None, :]
    z = np.concatenate([t, np.cos(ang), -np.sin(ang)], axis=-1)
    zpad = np.zeros((L, 128))
    zpad[:, :C_EMB] = z
    deltas = np.abs(np.linspace(HYENA_MIN_DECAY, HYENA_MAX_DECAY, C_DIM))
    window = np.exp(-t * deltas[None, :])
    return zpad.astype(np.float32), window.astype(np.float32)


def _filter_body(z_ref, w1_ref, b1_ref, w2_ref, b2_ref, w3_ref, b3_ref, fr_ref, w4_ref, win_ref, fm_ref,
                 hr_ref, g_ref, hq_ref, hs_scr, hd_scr):
    c = pl.program_id(0)
    fc = hr_ref.shape[0]

    @pl.when(c == 0)
    def _():
        fr = fr_ref[...]
        hh = jnp.sin(fr * (_dot_hi(z_ref[...], w1_ref[...]) + b1_ref[...]))
        hh = jnp.sin(fr * (_dot_hi(hh, w2_ref[...]) + b2_ref[...]))
        hh = jnp.sin(fr * (_dot_hi(hh, w3_ref[...]) + b3_ref[...]))
        hh = _dot_hi(hh, w4_ref[...])
        hf = hh[:, :C_DIM] * win_ref[...]
        hb = hh[:, C_DIM:] * win_ref[...]
        hs_scr[...] = hf + hb
        hd_scr[...] = hf - hb

    fm = fm_ref[0]
    hr = _dot_split(fm[:fc], hs_scr[...])
    first = (lax.broadcasted_iota(jnp.int32, (fc, C_DIM), 0) == 0) & (c == 0)
    hr_ref[...] = hr
    g_ref[...] = jnp.where(first, 0.0, _dot_split(fm[fc:], hd_scr[...]))
    hs = hs_scr[...]
    sign = jnp.where((lax.broadcasted_iota(jnp.int32, hs.shape, 0) & 1) == 0, 1.0, -1.0)
    hq_ref[...] = jnp.where(first, jnp.sum(hs * sign, axis=0, keepdims=True), hr)


def _hyena_filter(L, filt):
    w1, b1, w2, b2, w3, b3, freq, w4 = filt
    zpad, window = _filter_consts(L)
    fwd, _ = _dft_mats(L)
    nch, fc2, _ = fwd.shape
    fc = fc2 // 2
    w1p = jnp.pad(w1, ((0, 128 - C_EMB), (0, 0)))
    full = lambda shape: pl.BlockSpec(shape, lambda c: tuple(0 for _ in shape))
    out_spec = pl.BlockSpec((fc, C_DIM), lambda c: (c, 0))
    out_sd = jax.ShapeDtypeStruct((L, C_DIM), F32)
    return pl.pallas_call(
        _filter_body,
        grid=(nch,),
        in_specs=[full((L, 128)), full((128, C_FFN)), full((1, C_FFN)), full((C_FFN, C_FFN)), full((1, C_FFN)),
                  full((C_FFN, C_FFN)), full((1, C_FFN)), full((1, C_FFN)), full((C_FFN, 2 * C_DIM)),
                  full((L, C_DIM)), pl.BlockSpec((1, fc2, L), lambda c: (c, 0, 0))],
        out_specs=[out_spec, out_spec, out_spec],
        out_shape=[out_sd, out_sd, out_sd],
        scratch_shapes=[pltpu.VMEM((L, C_DIM), F32), pltpu.VMEM((L, C_DIM), F32)],
        compiler_params=_cparams("arbitrary"),
        name="hyena_filter",
    )(jnp.asarray(zpad), w1p, b1.reshape(1, C_FFN), w2, b2.reshape(1, C_FFN), w3, b3.reshape(1, C_FFN),
      freq.reshape(1, C_FFN), w4, jnp.asarray(window), jnp.asarray(fwd))


def _hyena_body(u_ref, cw_ref, cb_ref, d_ref, fm_ref, fi_ref, hr_ref, g_ref, hq_ref, y_ref,
                x0_scr, z_scr, acc_scr):
    c = pl.program_id(1)
    L = y_ref.shape[0]
    fc = hr_ref.shape[0]

    @pl.when(c == 0)
    def _():
        row = lax.broadcasted_iota(jnp.int32, (L, C_DIM), 0)

        def short_conv(sec):
            cols = slice(sec * C_DIM, (sec + 1) * C_DIM)
            u = u_ref[:, cols]
            prev = jnp.where(row == 0, 0.0, pltpu.roll(u, 1, axis=0))
            nxt = jnp.where(row == L - 1, 0.0, pltpu.roll(u, L - 1, axis=0))
            return (prev * cw_ref[0:1, cols] + u * cw_ref[1:2, cols] + nxt * cw_ref[2:3, cols]
                    + cb_ref[:, cols])

        x0_scr[...] = short_conv(0)
        z_scr[...] = short_conv(1) * short_conv(2)
        acc_scr[...] = jnp.zeros((L, C_DIM), F32)

    ab = _dot_split(fm_ref[0], z_scr[...])
    a, b = ab[:fc], ab[fc:]
    hr, g, hq = hr_ref[...], g_ref[...], hq_ref[...]
    pq = jnp.concatenate([a * hr - b * g, a * g + b * hq], axis=0)
    acc_scr[...] += _dot_split(fi_ref[0], pq)

    @pl.when(c == pl.num_programs(1) - 1)
    def _():
        y_ref[...] = x0_scr[...] * (acc_scr[...] + z_scr[...] * d_ref[...])


def _hyena(u, row_blk0, n_seq, L, conv_w, conv_b, d_skip, spec):
    hr, g, hq = spec
    fwd, inv = _dft_mats(L)
    nch, fc2, _ = fwd.shape
    fc = fc2 // 2
    u_w = 3 * C_DIM
    return pl.pallas_call(
        _hyena_body,
        grid=(n_seq, nch),
        in_specs=[pl.BlockSpec((L, u_w), lambda b, c: (row_blk0 + b, 0)),
                  pl.BlockSpec((3, u_w), lambda b, c: (0, 0)),
                  pl.BlockSpec((1, u_w), lambda b, c: (0, 0)),
                  pl.BlockSpec((1, C_DIM), lambda b, c: (0, 0)),
                  pl.BlockSpec((1, fc2, L), lambda b, c: (c, 0, 0)),
                  pl.BlockSpec((1, L, fc2), lambda b, c: (c, 0, 0)),
                  pl.BlockSpec((fc, C_DIM), lambda b, c: (c, 0)),
                  pl.BlockSpec((fc, C_DIM), lambda b, c: (c, 0)),
                  pl.BlockSpec((fc, C_DIM), lambda b, c: (c, 0))],
        out_specs=pl.BlockSpec((L, C_DIM), lambda b, c: (b, 0)),
        out_shape=jax.ShapeDtypeStruct((n_seq * L, C_DIM), F32),
        scratch_shapes=[pltpu.VMEM((L, C_DIM), F32)] * 3,
        compiler_params=_cparams("parallel", "arbitrary"),
        name="hyena",
    )(u, conv_w, conv_b.reshape(1, u_w), d_skip.reshape(1, C_DIM), jnp.asarray(fwd), jnp.asarray(inv), hr, g, hq)


def _hgrn_body(q_ref, ff_ref, fb_ref, i_ref, g_ref, lbf_ref, lbb_ref, nd_ref, s0f_ref, s0b_ref,
               o_ref, sf_ref, sb_ref, *, layer):
    L = o_ref.shape[0]
    C = GLA_CHUNK
    S = min(L, GLA_SPAN)
    nc = S // C
    n_span = L // S
    mid = C // 2
    def lower_bound(gm):
        e = jnp.exp(gm - jnp.max(gm, axis=0, keepdims=True))
        p = e / jnp.sum(e, axis=0, keepdims=True)
        return jnp.sum(p[0:layer + 1], axis=0, keepdims=True) - p[0:1]

    def gates(fx, lb):
        f = lb + (1.0 - lb) * jax.nn.sigmoid(fx)
        return 1.0 - f, jnp.log(f)


    chunk_shift = C.bit_length() - 1
    block_shift = D_KDIM.bit_length() - 1
    ti = lax.broadcasted_iota(jnp.int32, (S, S), 0)
    si = lax.broadcasted_iota(jnp.int32, (S, S), 1)
    same_chunk = (ti >> chunk_shift) == (si >> chunk_shift)
    causal = same_chunk & (si <= ti)
    anti = same_chunk & (si >= ti)
    row_chunk = lax.broadcasted_iota(jnp.int32, (S, nc * D_KDIM), 0) >> chunk_shift
    col_chunk = lax.broadcasted_iota(jnp.int32, (S, nc * D_KDIM), 1) >> block_shift
    own_block = row_chunk == col_chunk

    def spread(x):
        return jnp.where(own_block, jnp.concatenate([x] * nc, axis=1), 0.0)

    def chunk_cumsum(mask, lg):
        tri = mask.astype(BF16)
        hi = lg.astype(BF16)
        r1 = lg - hi.astype(F32)
        mid_t = r1.astype(BF16)
        lo = (r1 - mid_t.astype(F32)).astype(BF16)
        dot = lambda t: jnp.dot(tri, t, preferred_element_type=F32)
        return dot(hi) + dot(mid_t) + dot(lo)

    def per_chunk_rows(b, pos):
        return jnp.concatenate([jnp.broadcast_to(b[n * C + pos:n * C + pos + 1], (C, D_KDIM)) for n in range(nc)],
                               axis=0)

    def one_head(q, v, kf, lgf, kb, lgb, st_f, st_b):
        local = []
        for u in range(n_span):
            rows = slice(u * S, (u + 1) * S)
            qs, vs, kfs, kbs = q[rows], v[rows], kf[rows], kb[rows]
            lgs = jnp.concatenate([lgf[rows], lgb[rows]], axis=1)
            pre = chunk_cumsum(causal, lgs)
            b_f = pre[:, :D_KDIM]
            pre_b = pre[:, D_KDIM:]
            b_b = per_chunk_rows(pre_b, C - 1) - pre_b + lgb[rows]
            ref_f, ref_b = per_chunk_rows(b_f, mid), per_chunk_rows(b_b, mid)
            sc = (jnp.where(causal, _dot_nt(qs * jnp.exp(b_f - ref_f), kfs * jnp.exp(ref_f - b_f)), 0.0)
                  + jnp.where(anti, _dot_nt(qs * jnp.exp(b_b - ref_b), kbs * jnp.exp(ref_b - b_b)), 0.0))
            k_out = jnp.concatenate([kfs * jnp.exp(per_chunk_rows(b_f, C - 1) - b_f),
                                     kbs * jnp.exp(per_chunk_rows(b_b, 0) - b_b)], axis=1)
            kv_t = _dot_tn(spread(vs), k_out)
            local.append((_dot(sc, vs), kv_t, b_f, b_b, qs))

        states_f = [[None] * nc for _ in range(n_span)]
        for u in range(n_span):
            _, kv_t, b_f, _, _ = local[u]
            for n in range(nc):
                states_f[u][n] = st_f
                st_f = st_f * jnp.exp(b_f[n * C + C - 1:n * C + C]) + kv_t[n * D_VDIM:(n + 1) * D_VDIM, :D_KDIM]
        states_b = [[None] * nc for _ in range(n_span)]
        for u in reversed(range(n_span)):
            _, kv_t, _, b_b, _ = local[u]
            for n in reversed(range(nc)):
                states_b[u][n] = st_b
                st_b = st_b * jnp.exp(b_b[n * C:n * C + 1]) + kv_t[n * D_VDIM:(n + 1) * D_VDIM, D_KDIM:]

        outs = []
        for u in range(n_span):
            intra, _, b_f, b_b, qs = local[u]
            q_in = jnp.concatenate([spread(qs * jnp.exp(b_f)), spread(qs * jnp.exp(b_b))], axis=1)
            outs.append(intra + _dot_nt(q_in, jnp.concatenate(states_f[u] + states_b[u], axis=1)))
        return (jnp.concatenate(outs, axis=0) if n_span > 1 else outs[0]), st_f, st_b

    for hh in range(o_ref.shape[1] // D_VDIM):
        cols = slice(hh * D_KDIM, (hh + 1) * D_KDIM)
        kf, lgf = gates(ff_ref[:, cols], lower_bound(lbf_ref[:, cols]))
        kb, lgb = gates(fb_ref[:, cols], lower_bound(lbb_ref[:, cols]))
        o, st_f, st_b = one_head(_silu(q_ref[:, cols]), i_ref[:, cols], kf, lgf, kb, lgb,
                                 jnp.transpose(s0f_ref[0, hh]), jnp.transpose(s0b_ref[0, hh]))
        sf_ref[0, hh] = jnp.transpose(st_f)
        sb_ref[0, hh] = jnp.transpose(st_b)
        o_ref[:, cols] = _rms(o, nd_ref[...]) * _silu(g_ref[:, cols])


def _hgrn(u, row_blk0, n_seq, L, lb_fwd, lb_bwd, norm_d, s0f, s0b, layer):
    hps = HGRN_HEADS_PER_STEP
    width = hps * D_KDIM
    col0 = 3 * C_DIM // width
    groups = D_HEADS // hps
    col = lambda j: pl.BlockSpec((L, width), lambda b, h: (row_blk0 + b, col0 + j * groups + h))
    lbs = pl.BlockSpec((DEPTH, width), lambda b, h: (0, h))
    st = pl.BlockSpec((1, hps, D_KDIM, D_VDIM), lambda b, h: (b, h, 0, 0))
    st_sd = jax.ShapeDtypeStruct((n_seq, D_HEADS, D_KDIM, D_VDIM), F32)
    return pl.pallas_call(
        functools.partial(_hgrn_body, layer=layer),
        grid=(n_seq, groups),
        in_specs=[col(0), col(1), col(2), col(3), col(4), lbs, lbs,
                  pl.BlockSpec((1, D_VDIM), lambda b, h: (0, 0)), st, st],
        out_specs=[pl.BlockSpec((L, width), lambda b, h: (b, h)), st, st],
        out_shape=[jax.ShapeDtypeStruct((n_seq * L, D_HEADS * D_VDIM), F32), st_sd, st_sd],
        compiler_params=_cparams("parallel", "parallel"),
        name="hgrn",
    )(u, u, u, u, u, lb_fwd, lb_bwd, norm_d.reshape(1, D_VDIM), s0f, s0b)


def _pack_bf16_pairs(h):
    n = h.shape[1] // 2
    hi = lax.bitcast_convert_type(h[:, :n].astype(BF16).astype(F32), jnp.int32)
    lo = lax.bitcast_convert_type(h[:, n:].astype(BF16).astype(F32), jnp.int32)
    return hi | lax.shift_right_logical(lo, 16)


def _unpack_bf16_pairs(p):
    hi = lax.bitcast_convert_type(p & jnp.int32(-65536), F32).astype(BF16)
    lo = lax.bitcast_convert_type(lax.shift_left(p, 16), F32).astype(BF16)
    return hi, lo


def _outproj_body(*refs, n_x):
    a_refs, b_refs, x_refs = refs[0:2], refs[2:4], refs[4:4 + n_x]
    mod_ref, gf_ref, w_ref, wrh_ref, wrl_ref, rb_ref, x1_ref, h2_ref, chosen_ref, gk_ref, ik_ref = refs[4 + n_x:]
    m = mod_ref[0]
    half = a_refs[0].shape[1]
    out = _dot(_token_tile(a_refs), w_ref[0:half, :]) + _dot(_token_tile(b_refs), w_ref[half:, :])
    x1 = _token_tile(x_refs) + m[:, 2 * D_MODEL:3 * D_MODEL] * out
    x1_ref[...] = x1
    h2 = _rms(x1, gf_ref[...]) * (1.0 + m[:, 4 * D_MODEL:5 * D_MODEL]) + m[:, 3 * D_MODEL:4 * D_MODEL]
    h2_ref[...] = _pack_bf16_pairs(h2)
    h_hi = h2.astype(BF16)
    h_lo = (h2 - h_hi.astype(F32)).astype(BF16)
    logits = _dot_nt(wrh_ref[...], h_hi) + _dot_nt(wrh_ref[...], h_lo) + _dot_nt(wrl_ref[...], h_hi)
    scores = jax.nn.sigmoid(logits)
    work = scores + rb_ref[...]
    expert = lax.broadcasted_iota(jnp.int32, work.shape, 0).astype(F32)
    slot = lax.broadcasted_iota(jnp.int32, (TOP_K, work.shape[1]), 0)
    chosen = jnp.zeros(work.shape, F32)
    gk = jnp.zeros((TOP_K, work.shape[1]), F32)
    ik = jnp.zeros((TOP_K, work.shape[1]), F32)
    for k in range(TOP_K):
        best = jnp.max(work, axis=0, keepdims=True)
        first = jnp.min(jnp.where(work == best, expert, float(N_EXPERTS)), axis=0, keepdims=True)
        hit = expert == first
        chosen = jnp.where(hit, 1.0, chosen)
        gk = jnp.where(slot == k, jnp.sum(jnp.where(hit, scores, 0.0), axis=0, keepdims=True), gk)
        ik = jnp.where(slot == k, first, ik)
        work = jnp.where(hit, -jnp.inf, work)
    chosen_ref[...] = chosen
    gk_ref[...] = jnp.transpose(gk / jnp.sum(gk, axis=0, keepdims=True) * ROUTE_SCALE)
    ik_ref[...] = ik


def _outproj(a, b, x, mod_l, gain_ffn, w_out, w_router, router_bias):
    half = a[0].shape[1]
    a_specs, a_args = _token_specs(a, half)
    b_specs, b_args = _token_specs(b, half)
    x_specs, x_args = _token_specs(x, D_MODEL)
    wr_t = w_router.T
    wr_hi = wr_t.astype(BF16)
    wr_lo = (wr_t - wr_hi.astype(F32)).astype(BF16)
    return pl.pallas_call(
        functools.partial(_outproj_body, n_x=len(x_args)),
        grid=(T_ALL // TM,),
        in_specs=a_specs + b_specs + x_specs + [
                  pl.BlockSpec((1, 1, N_MOD * D_MODEL), lambda i: (_mod_row(i), 0, 0)),
                  pl.BlockSpec((1, D_MODEL), lambda i: (0, 0)),
                  pl.BlockSpec((2 * half, D_MODEL), lambda i: (0, 0)),
                  pl.BlockSpec((N_EXPERTS, D_MODEL), lambda i: (0, 0)),
                  pl.BlockSpec((N_EXPERTS, D_MODEL), lambda i: (0, 0)),
                  pl.BlockSpec((N_EXPERTS, 1), lambda i: (0, 0))],
        out_specs=[pl.BlockSpec((TM, D_MODEL), lambda i: (i, 0)),
                   pl.BlockSpec((TM, D_MODEL // 2), lambda i: (i, 0)),
                   pl.BlockSpec((N_EXPERTS, TM), lambda i: (0, i)),
                   pl.BlockSpec((TM, TOP_K), lambda i: (i, 0)),
                   pl.BlockSpec((TOP_K, TM), lambda i: (0, i))],
        out_shape=[jax.ShapeDtypeStruct((T_ALL, D_MODEL), F32),
                   jax.ShapeDtypeStruct((T_ALL, D_MODEL // 2), jnp.int32),
                   jax.ShapeDtypeStruct((N_EXPERTS, T_ALL), F32),
                   jax.ShapeDtypeStruct((T_ALL, TOP_K), F32),
                   jax.ShapeDtypeStruct((TOP_K, T_ALL), F32)],
        compiler_params=_cparams("parallel"),
        name="outproj_router",
    )(*a_args, *b_args, *x_args, mod_l, gain_ffn.reshape(1, D_MODEL), w_out, wr_hi, wr_lo,
      router_bias.reshape(N_EXPERTS, 1))


def _route_body(chosen_ref, ik_ref, dest_ref, first_ref, count_ref, pos_scr):
    n_tiles = T_ALL // TM
    r = lax.broadcasted_iota(jnp.int32, (TM, TM), 0)
    c = lax.broadcasted_iota(jnp.int32, (TM, TM), 1)
    before = (r < c).astype(BF16)

    counts = jnp.zeros((N_EXPERTS, 1), F32)
    for i in range(n_tiles):
        cols = slice(i * TM, (i + 1) * TM)
        m = chosen_ref[:, cols]
        pos_scr[:, cols] = jnp.dot(m.astype(BF16), before, preferred_element_type=F32) + counts
        counts = counts + jnp.sum(m, axis=1, keepdims=True)
    padded = jnp.ceil(counts * (1.0 / MOE_BLK)) * MOE_BLK
    ei = lax.broadcasted_iota(jnp.int32, (N_EXPERTS, N_EXPERTS), 0)
    ej = lax.broadcasted_iota(jnp.int32, (N_EXPERTS, N_EXPERTS), 1)
    end = _dot_hi((ej <= ei).astype(F32), jnp.broadcast_to(padded, (N_EXPERTS, LANES)))[:, 0:1]
    start = end - padded

    expert = lax.broadcasted_iota(jnp.int32, (N_EXPERTS, TM), 0).astype(F32)
    slot = lax.broadcasted_iota(jnp.int32, (TOP_K, TM), 0)
    for i in range(n_tiles):
        cols = slice(i * TM, (i + 1) * TM)
        row_of = pos_scr[:, cols] + start
        ik = ik_ref[:, cols]
        acc = jnp.zeros((TOP_K, TM), F32)
        for k in range(TOP_K):
            pick = jnp.sum(jnp.where(expert == ik[k:k + 1, :], row_of, 0.0), axis=0, keepdims=True)
            acc = jnp.where(slot == k, pick, acc)
        dest_ref[:, cols] = acc.astype(jnp.int32)
    first_ref[...] = jnp.broadcast_to(start * (1.0 / MOE_BLK), (N_EXPERTS, LANES)).astype(jnp.int32)
    count_ref[...] = jnp.broadcast_to(padded * (1.0 / MOE_BLK), (N_EXPERTS, LANES)).astype(jnp.int32)


def _route(chosen, ik):
    full = lambda shape: pl.BlockSpec(shape, lambda i: (0, 0))
    return pl.pallas_call(
        _route_body,
        grid=(1,),
        in_specs=[full((N_EXPERTS, T_ALL)), full((TOP_K, T_ALL))],
        out_specs=[full((TOP_K, T_ALL)), full((N_EXPERTS, LANES)), full((N_EXPERTS, LANES))],
        out_shape=[jax.ShapeDtypeStruct((TOP_K, T_ALL), jnp.int32),
                   jax.ShapeDtypeStruct((N_EXPERTS, LANES), jnp.int32),
                   jax.ShapeDtypeStruct((N_EXPERTS, LANES), jnp.int32)],
        scratch_shapes=[pltpu.VMEM((N_EXPERTS, T_ALL), F32)],
        compiler_params=_cparams("arbitrary"),
        name="moe_route",
    )(chosen, ik)


def _sc_worker_id():
    return lax.axis_index("s") * SC_CORES + lax.axis_index("c")


def _sc_dispatch(h2p, dest):
    n_chunks = T_ALL // DISP_CHUNK
    k_per = TOP_K // DISP_SPLIT
    items_per_worker = n_chunks * DISP_SPLIT // SC_WORKERS
    chunk_stride = SC_WORKERS // DISP_SPLIT
    width = h2p.shape[1]
    mesh = plsc.VectorSubcoreMesh(core_axis_name="c", subcore_axis_name="s")

    @functools.partial(
        pl.kernel, mesh=mesh,
        out_type=jax.ShapeDtypeStruct((MOE_ROWS, width), jnp.int32),
        scratch_types=[pltpu.VMEM((k_per, DISP_CHUNK), jnp.int32), pltpu.VMEM((DISP_CHUNK, width), jnp.int32)],
    )
    def run(x_hbm, dest_hbm, xs_hbm, idx_v, rows_v):
        wid = _sc_worker_id()
        group = wid % DISP_SPLIT
        for i in range(items_per_worker):
            chunk = i * chunk_stride + wid // DISP_SPLIT
            tokens = pl.ds(pl.multiple_of(chunk * DISP_CHUNK, DISP_CHUNK), DISP_CHUNK)
            pltpu.sync_copy(dest_hbm.at[group, :, tokens], idx_v)
            pltpu.sync_copy(x_hbm.at[tokens], rows_v)
            for k in range(k_per):
                pltpu.sync_copy(rows_v, xs_hbm.at[idx_v.at[k]])

    return run(h2p, dest.reshape(DISP_SPLIT, k_per, T_ALL))


def _sc_collect(y, dest_flat, tok0, n_tok):
    per_worker = n_tok // SC_WORKERS
    n_chunks = per_worker // COLLECT_CHUNK
    n_steps = TOP_K * n_chunks
    width = y.shape[1]
    mesh = plsc.VectorSubcoreMesh(core_axis_name="c", subcore_axis_name="s")

    @functools.partial(
        pl.kernel, mesh=mesh,
        out_type=jax.ShapeDtypeStruct((TOP_K * n_tok, width), y.dtype),
        scratch_types=[pltpu.VMEM((TOP_K * per_worker,), jnp.int32),
                       pltpu.VMEM((COLLECT_CHUNK, width), y.dtype), pltpu.VMEM((COLLECT_CHUNK, width), y.dtype),
                       pltpu.SemaphoreType.DMA, pltpu.SemaphoreType.DMA],
    )
    def run(y_hbm, dest_hbm, yg_hbm, idx_v, rows0, rows1, sem0, sem1):
        wid = _sc_worker_id()
        bufs = ((rows0, sem0), (rows1, sem1))
        for k in range(TOP_K):
            pltpu.sync_copy(dest_hbm.at[pl.ds(k * T_ALL + tok0 + wid * per_worker, per_worker)],
                            idx_v.at[pl.ds(k * per_worker, per_worker)])

        def gather(step, buf):
            rows, sem = buf
            idx = idx_v.at[pl.ds(pl.multiple_of(step * COLLECT_CHUNK, 8), COLLECT_CHUNK)]
            return pltpu.make_async_copy(y_hbm.at[idx], rows, sem)

        def out_rows(step):
            off = (step // n_chunks) * n_tok + wid * per_worker + (step % n_chunks) * COLLECT_CHUNK
            return yg_hbm.at[pl.ds(pl.multiple_of(off, 8), COLLECT_CHUNK)]

        gather(0, bufs[0]).start()

        @pl.loop(0, n_steps, step=2)
        def _(base):
            for j in range(2):
                step = base + j

                @pl.when(step + 1 < n_steps)
                def _():
                    gather(step + 1, bufs[1 - j]).start()

                gather(step, bufs[j]).wait()
                pltpu.sync_copy(bufs[j][0], out_rows(step))

    return run(y, dest_flat)


def _expert_body(first_ref, count_ref, xs_hbm, wg_ref, wu_ref, wd_ref, y_hbm,
                 wg_bf, wu_bf, wd_bf, x_buf, y_buf, in_sem, out_sem):
    e = pl.program_id(0)
    first = first_ref[e]
    count = count_ref[e]
    n_used = first_ref[N_EXPERTS - 1] + count_ref[N_EXPERTS - 1]
    half = D_MODEL // 2
    wg_bf[...] = wg_ref[0, 0].astype(BF16)
    wu_bf[...] = wu_ref[0, 0].astype(BF16)
    wd_bf[...] = wd_ref[0, 0].astype(BF16)

    def part_rows(g, part, n_parts):
        size = MOE_BLK // n_parts
        return pl.ds(pl.multiple_of(g * MOE_BLK + part * size, size), size), pl.ds(part * size, size)

    def in_copies(g):
        slot = g & (EXPERT_SLOTS - 1)
        out = []
        for part in range(EXPERT_IN_PARTS):
            src, dst = part_rows(g, part, EXPERT_IN_PARTS)
            out.append(pltpu.make_async_copy(xs_hbm.at[src], x_buf.at[slot, dst], in_sem.at[slot]))
        return out

    def out_copies(g):
        slot = g & (EXPERT_SLOTS - 1)
        out = []
        for part in range(EXPERT_OUT_PARTS):
            dst, src = part_rows(g, part, EXPERT_OUT_PARTS)
            out.append(pltpu.make_async_copy(y_buf.at[slot, src], y_hbm.at[dst], out_sem.at[slot]))
        return out

    @pl.when((first == 0) & (count > 0))
    def _():
        for ahead in range(EXPERT_SLOTS - 1):
            @pl.when(ahead < n_used)
            def _():
                for cp in in_copies(ahead):
                    cp.start()

    def block(b, carry):
        g = first + b
        slot = g & (EXPERT_SLOTS - 1)
        for cp in in_copies(g):
            cp.wait()

        @pl.when(g + EXPERT_SLOTS - 1 < n_used)
        def _():
            for cp in in_copies(g + EXPERT_SLOTS - 1):
                cp.start()

        @pl.when(g >= EXPERT_SLOTS)
        def _():
            for cp in out_copies(g - EXPERT_SLOTS):
                cp.wait()

        hi, lo = _unpack_bf16_pairs(x_buf[slot])

        def proj(w_bf):
            return (jnp.dot(hi, w_bf[0:half, :], preferred_element_type=F32)
                    + jnp.dot(lo, w_bf[half:, :], preferred_element_type=F32))

        hid = _silu(proj(wg_bf)) * proj(wu_bf)
        y_buf[slot] = _pack_bf16_pairs(jnp.dot(hid.astype(BF16), wd_bf[...], preferred_element_type=F32))
        for cp in out_copies(g):
            cp.start()
        return carry

    lax.fori_loop(0, count, block, 0)

    @pl.when(e == N_EXPERTS - 1)
    def _():
        for back in range(EXPERT_SLOTS, 0, -1):
            @pl.when(n_used >= back)
            def _():
                for cp in out_copies(n_used - back):
                    cp.wait()


EXPERT_SLOTS = 4
EXPERT_IN_PARTS = 2
EXPERT_OUT_PARTS = 4


def _experts(first_blk, n_blk, xs, layer, w_gate, w_up, w_down):
    w_in = pl.BlockSpec((1, 1, D_MODEL, D_EXPERT), lambda e, first, count: (layer, e, 0, 0))
    grid_spec = pltpu.PrefetchScalarGridSpec(
        num_scalar_prefetch=2,
        grid=(N_EXPERTS,),
        in_specs=[pl.BlockSpec(memory_space=pl.ANY), w_in, w_in,
                  pl.BlockSpec((1, 1, D_EXPERT, D_MODEL), lambda e, first, count: (layer, e, 0, 0))],
        out_specs=pl.BlockSpec(memory_space=pl.ANY),
        scratch_shapes=[pltpu.VMEM((D_MODEL, D_EXPERT), BF16), pltpu.VMEM((D_MODEL, D_EXPERT), BF16),
                        pltpu.VMEM((D_EXPERT, D_MODEL), BF16),
                        pltpu.VMEM((EXPERT_SLOTS, MOE_BLK, D_MODEL // 2), jnp.int32),
                        pltpu.VMEM((EXPERT_SLOTS, MOE_BLK, D_MODEL // 2), jnp.int32),
                        pltpu.SemaphoreType.DMA((EXPERT_SLOTS,)), pltpu.SemaphoreType.DMA((EXPERT_SLOTS,))],
    )
    return pl.pallas_call(
        _expert_body,
        grid_spec=grid_spec,
        out_shape=jax.ShapeDtypeStruct((MOE_ROWS, D_MODEL // 2), jnp.int32),
        compiler_params=_cparams("arbitrary"),
        name="moe_experts",
    )(first_blk, n_blk, xs, w_gate, w_up, w_down)


def _combine_body(x1_ref, h2_ref, yg_ref, gk_ref, mod_ref, sg_ref, su_ref, sd_ref, fn_ref, o_ref, *, final):
    hi, lo = _unpack_bf16_pairs(h2_ref[...])
    half = D_MODEL // 2

    def proj(w_ref):
        return _dot(hi, w_ref[0:half, :]) + _dot(lo, w_ref[half:, :])

    shared = _dot(_silu(proj(sg_ref)) * proj(su_ref), sd_ref[...])
    acc_hi, acc_lo = shared[:, :half], shared[:, half:]
    gk = gk_ref[...]
    for k in range(TOP_K):
        y_hi, y_lo = _unpack_bf16_pairs(yg_ref[k])
        acc_hi = acc_hi + gk[:, k:k + 1] * y_hi.astype(F32)
        acc_lo = acc_lo + gk[:, k:k + 1] * y_lo.astype(F32)
    acc = jnp.concatenate([acc_hi, acc_lo], axis=1)
    m = mod_ref[0]
    y = x1_ref[...] + m[:, 5 * D_MODEL:6 * D_MODEL] * acc
    o_ref[...] = _rms(y, fn_ref[...]) if final else y


def _combine(x1, h2p, yg, gk, mod_l, ws_gate, ws_up, ws_down, final_norm, final, tok0, n_tok):
    tile0 = tok0 // TM
    tok = lambda shape: pl.BlockSpec(shape, lambda i: (tile0 + i, 0))
    full = lambda shape: pl.BlockSpec(shape, lambda i: (0, 0))
    return pl.pallas_call(
        functools.partial(_combine_body, final=final),
        grid=(n_tok // TM,),
        in_specs=[tok((TM, D_MODEL)), tok((TM, D_MODEL // 2)),
                  pl.BlockSpec((TOP_K, TM, D_MODEL // 2), lambda i: (0, i, 0)),
                  tok((TM, TOP_K)),
                  pl.BlockSpec((1, 1, N_MOD * D_MODEL), lambda i: (_mod_row(tile0 + i), 0, 0)),
                  full((D_MODEL, D_EXPERT)), full((D_MODEL, D_EXPERT)), full((D_EXPERT, D_MODEL)),
                  full((1, D_MODEL))],
        out_specs=pl.BlockSpec((TM, D_MODEL), lambda i: (i, 0)),
        out_shape=jax.ShapeDtypeStruct((n_tok, D_MODEL), F32),
        compiler_params=_cparams("parallel"),
        name="moe_combine",
    )(x1, h2p, yg, gk, mod_l, ws_gate, ws_up, ws_down, final_norm.reshape(1, D_MODEL))


def _moe(x1, h2p, chosen, gk, ik, mod_l, layer, w_gate, w_up, w_down, ws_gate, ws_up, ws_down, final_norm, final):
    dest, first_blk, n_blk = _route(chosen, ik)
    xs = _sc_dispatch(h2p, dest)
    y = _experts(first_blk[:, 0], n_blk[:, 0], xs, layer, w_gate, w_up, w_down)
    dest_flat = dest.reshape(-1)

    def finish(tok0, n_tok):
        yg = _sc_collect(y, dest_flat, tok0, n_tok).reshape(TOP_K, n_tok, D_MODEL // 2)
        return _combine(x1, h2p, yg, gk, mod_l, ws_gate, ws_up, ws_down, final_norm, final, tok0, n_tok)

    if not final:
        return finish(0, T_ALL)
    return finish(0, T_CTX), finish(T_CTX, T_LAT)


def kernel(x_prompt, x_sample, cache_a_k, cache_a_v, cache_b_k, cache_b_v, state_d_fwd, state_d_bwd, c, c_ctx, w_ada, b_ada, norm_mix, norm_ffn, w_in_attn, w_out_attn, sink_a, rpb_b, w_in_rec, w_out_rec, conv_w, conv_b, filt_w1, filt_b1, filt_w2, filt_b2, filt_w3, filt_b3, filt_freq, filt_w4, d_skip, lb_fwd, lb_bwd, norm_d, w_router, router_bias, w_gate, w_up, w_down, ws_gate, ws_up, ws_down, final_norm):
    x = (x_prompt.reshape(T_CTX, D_MODEL), x_sample.reshape(T_LAT, D_MODEL))
    cvec = jnp.concatenate([c_ctx[None, :], c], axis=0)
    mod = _ada(cvec, w_ada, b_ada).reshape(DEPTH, CVEC_PAD, 1, N_MOD * D_MODEL)

    new_kv = None
    new_state = None
    for l in range(DEPTH):
        j = l // 2
        final = l == DEPTH - 1
        if l % 2 == 0:
            qkv = _inproj(x, mod[l], norm_mix[l], w_in_attn[j])
            oa_ctx, ob_ctx, *new_kv = _ctx_attn(qkv, sink_a[j])
            new_kv = tuple(new_kv)
            q_rot, k_rot = _rope(qkv)
            cache = lambda t: t[:, j].reshape(DEC_BATCH, PAST_LEN, -1)
            oa_lat = _win_attn(qkv, q_rot, k_rot, cache(cache_a_k), cache(cache_a_v), sink_a[j])
            ob_lat = _na_attn(qkv, cache(cache_b_k), cache(cache_b_v), _na_rel_rows(rpb_b[j]))
            mix_a = (oa_ctx, oa_lat)
            mix_b = (ob_ctx, ob_lat)
            w_out = w_out_attn[j]
        else:
            u = _inproj(x, mod[l], norm_mix[l], w_in_rec[j])
            filt = (filt_w1[j], filt_b1[j], filt_w2[j], filt_b2[j], filt_w3[j], filt_b3[j], filt_freq[j],
                    filt_w4[j])
            y_ctx = _hyena(u, 0, BATCH, SEQ, conv_w[j], conv_b[j], d_skip[j], _hyena_filter(SEQ, filt))
            y_lat = _hyena(u, T_CTX // DEC_SEQ, DEC_BATCH, DEC_SEQ, conv_w[j], conv_b[j], d_skip[j],
                           _hyena_filter(DEC_SEQ, filt))
            zeros = jnp.zeros((BATCH, D_HEADS, D_KDIM, D_VDIM), F32)
            o_ctx, s_f, s_b = _hgrn(u, 0, BATCH, SEQ, lb_fwd, lb_bwd, norm_d[j], zeros, zeros, l)
            o_lat, _, _ = _hgrn(u, T_CTX // DEC_SEQ, DEC_BATCH, DEC_SEQ, lb_fwd, lb_bwd, norm_d[j],
                                state_d_fwd[:, j], state_d_bwd[:, j], l)
            new_state = (s_f[:, None], s_b[:, None])
            mix_a = (y_ctx, y_lat)
            mix_b = (o_ctx, o_lat)
            w_out = w_out_rec[j]
        x1, h2p, chosen, gk, ik = _outproj(mix_a, mix_b, x, mod[l], norm_ffn[l], w_out, w_router[l],
                                           router_bias[l])
        x = _moe(x1, h2p, chosen, gk, ik, mod[l], l, w_gate, w_up, w_down, ws_gate[l], ws_up[l],
                 ws_down[l], final_norm, final)

    y_prompt = x[0].reshape(BATCH, SEQ, D_MODEL)
    y_sample = x[1].reshape(DEC_BATCH, DEC_SEQ, D_MODEL)
    return (y_prompt, y_sample) + new_kv + new_state
```

```python
import functools
import math

import numpy as np
import jax
import jax.numpy as jnp
from jax import lax
from jax.experimental import pallas as pl
from jax.experimental.pallas import tpu as pltpu
from jax.experimental.pallas import tpu_sc as plsc

F32 = jnp.float32
BF16 = jnp.bfloat16
HI = lax.Precision.HIGHEST

D_MODEL = 1024
BATCH = 16
SEQ = 256
DEPTH = 2
DEC_BATCH = 2
DEC_SEQ = 1024
PAST_LEN = 512
GRID_W = 64
HEAD_DIM = 64
N_MOD = 6
RMS_EPS = 1e-6
A_HEADS = 8
A_KV_HEADS = 2
A_GROUP = A_HEADS // A_KV_HEADS
WINDOW = 128
ROPE_BASE = 10000.0
B_HEADS = 8
NA_ROWS = 8
NA_COLS = 16
C_DIM = 512
C_EMB = 33
C_FFN = 64
HYENA_MIN_DECAY = math.log(1e-2) / 1.5
HYENA_MAX_DECAY = math.log(1e-2) / 0.3
D_KDIM = 128
D_VDIM = 128
D_HEADS = 4
N_EXPERTS = 64
TOP_K = 8
D_EXPERT = 256
ROUTE_SCALE = 2.5
A_Q = A_HEADS * HEAD_DIM
A_KV = A_KV_HEADS * HEAD_DIM
B_W = B_HEADS * HEAD_DIM
ATTN_IN = A_Q + 2 * A_KV + 3 * B_W
REC_IN = 3 * C_DIM + 5 * D_HEADS * D_KDIM

T_CTX = BATCH * SEQ
T_LAT = DEC_BATCH * DEC_SEQ
T_ALL = T_CTX + T_LAT
N_CVEC = 1 + DEC_BATCH
CVEC_PAD = 8
TM = 512
MASK_NEG = -1e30
GLA_CHUNK = 64
GLA_SPAN = 256
HGRN_HEADS_PER_STEP = 4
DFT_CHUNK = 256
MOE_BLK = 512
MOE_NBLK = -(-(T_ALL * TOP_K + N_EXPERTS * (MOE_BLK - 1)) // MOE_BLK)
MOE_ROWS = MOE_NBLK * MOE_BLK
SC_CORES = 2
SC_SUBCORES = 16
SC_WORKERS = SC_CORES * SC_SUBCORES
DISP_CHUNK = 128
DISP_SPLIT = 2
COLLECT_CHUNK = 64
VMEM_LIMIT = 56 * 1024 * 1024


def _cparams(*sem):
    return pltpu.CompilerParams(dimension_semantics=sem, vmem_limit_bytes=VMEM_LIMIT)


def _mod_row(i):
    return jnp.where(i < T_CTX // TM, 0, 1 + (i - T_CTX // TM) // (DEC_SEQ // TM))


def _dot(a, b):
    return jnp.dot(a.astype(BF16), b.astype(BF16), preferred_element_type=F32)


def _dot_nt(a, b):
    return lax.dot_general(a.astype(BF16), b.astype(BF16), (((1,), (1,)), ((), ())),
                           preferred_element_type=F32)


def _dot_tn(a, b):
    return lax.dot_general(a.astype(BF16), b.astype(BF16), (((0,), (0,)), ((), ())),
                           preferred_element_type=F32)


def _dot_hi(a, b):
    return jnp.dot(a, b, precision=HI, preferred_element_type=F32)


def _split_bf16(x):
    hi = x.astype(BF16)
    return hi, (x - hi.astype(F32)).astype(BF16)


def _dot_split(a, b):
    a_hi, a_lo = _split_bf16(a)
    b_hi, b_lo = _split_bf16(b)
    dot = lambda x, y: jnp.dot(x, y, preferred_element_type=F32)
    return dot(a_hi, b_hi) + dot(a_hi, b_lo) + dot(a_lo, b_hi)


def _silu(x):
    return x * jax.nn.sigmoid(x)


def _rms(x, g):
    return x * lax.rsqrt(jnp.mean(x * x, axis=-1, keepdims=True) + RMS_EPS) * g


ADA_TN = 1536
ADA_UNROLL = 4


def _ada_body(cb_ref, w_ref, b_ref, o_ref):
    tn = o_ref.shape[-1]
    n_slab = tn // LANES

    def step(k8, accs):
        r0 = pl.multiple_of(k8 * 8, 8)
        sk = [_silu(cb_ref[j, pl.ds(r0, 8), :]) for j in range(N_CVEC)]
        out = []
        for s in range(n_slab):
            wk = w_ref[0, pl.ds(r0, 8), s * LANES:(s + 1) * LANES]
            out.extend(accs[s * N_CVEC + j] + wk * sk[j] for j in range(N_CVEC))
        return tuple(out)

    accs = lax.fori_loop(0, D_MODEL // 8, step,
                         tuple(jnp.zeros((8, LANES), F32) for _ in range(n_slab * N_CVEC)), unroll=ADA_UNROLL)
    o_ref[0] = jnp.zeros((CVEC_PAD, tn), F32)
    for s in range(n_slab):
        for j in range(N_CVEC):
            o_ref[0, j:j + 1, s * LANES:(s + 1) * LANES] = (
                jnp.sum(accs[s * N_CVEC + j], axis=0, keepdims=True) + b_ref[0, :, s * LANES:(s + 1) * LANES])


def _ada(c_lanes, layer, w_ada, b_ada):
    n_out = N_MOD * D_MODEL
    return pl.pallas_call(
        _ada_body,
        grid=(n_out // ADA_TN,),
        in_specs=[pl.BlockSpec((N_CVEC, D_MODEL, LANES), lambda n: (0, 0, 0)),
                  pl.BlockSpec((1, D_MODEL, ADA_TN), lambda n: (layer, 0, n)),
                  pl.BlockSpec((1, 1, ADA_TN), lambda n: (layer, 0, n))],
        out_specs=pl.BlockSpec((1, CVEC_PAD, ADA_TN), lambda n: (0, 0, n)),
        out_shape=jax.ShapeDtypeStruct((1, CVEC_PAD, n_out), F32),
        compiler_params=_cparams("parallel"),
        name="ada",
    )(c_lanes, w_ada, b_ada.reshape(DEPTH, 1, n_out))


N_CTX_TILES = T_CTX // TM


def _token_specs(x, width):
    if not isinstance(x, tuple):
        return [pl.BlockSpec((TM, width), lambda i: (i, 0))], (x,)
    return ([pl.BlockSpec((TM, width), lambda i: (jnp.minimum(i, N_CTX_TILES - 1), 0)),
             pl.BlockSpec((TM, width), lambda i: (jnp.maximum(i - N_CTX_TILES, 0), 0))], x)


def _token_tile(refs):
    if len(refs) == 1:
        return refs[0][...]
    return jnp.where(pl.program_id(0) < N_CTX_TILES, refs[0][...], refs[1][...])


def _inproj_body(*refs, n_x):
    x_refs, (mod_ref, g_ref, w_ref, o_ref, w_bf) = refs[:n_x], refs[n_x:]

    @pl.when(pl.program_id(0) == 0)
    def _():
        w_bf[...] = w_ref[...].astype(BF16)

    m = mod_ref[0]
    h = _rms(_token_tile(x_refs), g_ref[...]) * (1.0 + m[:, D_MODEL:2 * D_MODEL]) + m[:, 0:D_MODEL]
    o_ref[...] = _dot(h, w_bf[...])


def _inproj(x, mod_l, gain, w):
    n = w.shape[1]
    x_specs, x_args = _token_specs(x, D_MODEL)
    return pl.pallas_call(
        functools.partial(_inproj_body, n_x=len(x_args)),
        grid=(T_ALL // TM,),
        in_specs=x_specs + [pl.BlockSpec((1, 1, N_MOD * D_MODEL), lambda i: (_mod_row(i), 0, 0)),
                            pl.BlockSpec((1, D_MODEL), lambda i: (0, 0)),
                            pl.BlockSpec((D_MODEL, n), lambda i: (0, 0), pipeline_mode=pl.Buffered(1))],
        out_specs=pl.BlockSpec((TM, n), lambda i: (i, 0)),
        out_shape=jax.ShapeDtypeStruct((T_ALL, n), F32),
        scratch_shapes=[pltpu.VMEM((D_MODEL, n), BF16)],
        compiler_params=_cparams("arbitrary"),
        name="inproj",
    )(*x_args, mod_l, gain.reshape(1, D_MODEL), w)


def _head_cols(h):
    return slice(h * HEAD_DIM, (h + 1) * HEAD_DIM)


def _group_rows(ref, rows, first_col, sink_ref, hk):
    n = rows.stop - rows.start
    q = jnp.concatenate([ref[rows, first_col + g * HEAD_DIM:first_col + (g + 1) * HEAD_DIM]
                         for g in range(A_GROUP)], axis=0)
    sink = jnp.concatenate([jnp.broadcast_to(sink_ref[:, hk * A_GROUP + g:hk * A_GROUP + g + 1], (n, 1))
                            for g in range(A_GROUP)], axis=0)
    return q, sink


def _ctx_attn_body(qkv_ref, sink_ref, oa_ref, ob_ref, ak_ref, av_ref, bk_ref, bv_ref):
    scale = HEAD_DIM ** -0.5
    lane = lax.broadcasted_iota(jnp.int32, (SEQ, LANES), 1)
    in_half = [lane < HEAD_DIM, lane >= HEAD_DIM]

    def attend(q, k, v, sink):
        s = _dot_nt(q, k) * scale
        m = jnp.max(s, axis=-1, keepdims=True)
        if sink is not None:
            m = jnp.maximum(m, sink)
        p = jnp.exp(s - m)
        den = jnp.sum(p, axis=-1, keepdims=True)
        if sink is not None:
            den = den + jnp.exp(sink - m)
        return _dot(p, v) / den

    def tile(first_col, t):
        return qkv_ref[:, first_col + t * LANES:first_col + (t + 1) * LANES]

    base = A_Q + 2 * A_KV
    for hk in range(A_KV_HEADS):
        dst = pl.ds(hk, SEQ, stride=A_KV_HEADS)
        ak_ref[0, dst, :] = qkv_ref[:, A_Q + hk * HEAD_DIM:A_Q + (hk + 1) * HEAD_DIM]
        av_ref[0, dst, :] = qkv_ref[:, A_Q + A_KV + hk * HEAD_DIM:A_Q + A_KV + (hk + 1) * HEAD_DIM]
    for h in range(B_HEADS):
        dst = pl.ds(h, SEQ, stride=B_HEADS)
        bk_ref[0, dst, :] = qkv_ref[:, base + B_W + h * HEAD_DIM:base + B_W + (h + 1) * HEAD_DIM]
        bv_ref[0, dst, :] = qkv_ref[:, base + 2 * B_W + h * HEAD_DIM:base + 2 * B_W + (h + 1) * HEAD_DIM]

    k_t, v_t = tile(A_Q, 0), tile(A_Q + A_KV, 0)
    k_sw, v_sw = pltpu.roll(k_t, HEAD_DIM, axis=1), pltpu.roll(v_t, HEAD_DIM, axis=1)
    tiles_per_kv = A_GROUP // HEADS_PER_TILE
    for hk in range(A_KV_HEADS):
        q_tiles = [tile(0, hk * tiles_per_kv + j) for j in range(tiles_per_kv)]
        halves = []
        for p in range(HEADS_PER_TILE):
            q = jnp.concatenate([jnp.where(in_half[p], qt, 0.0) for qt in q_tiles], axis=0)
            heads = [(hk * tiles_per_kv + j) * HEADS_PER_TILE + p for j in range(tiles_per_kv)]
            sink = jnp.concatenate([jnp.broadcast_to(sink_ref[:, h:h + 1], (SEQ, 1)) for h in heads], axis=0)
            halves.append(attend(q, k_t if p == hk else k_sw, v_t if p == hk else v_sw, sink))
        first_half = lax.broadcasted_iota(jnp.int32, halves[0].shape, 1) < HEAD_DIM
        o = jnp.where(first_half, halves[0], halves[1])
        for j in range(tiles_per_kv):
            t = hk * tiles_per_kv + j
            oa_ref[:, t * LANES:(t + 1) * LANES] = o[j * SEQ:(j + 1) * SEQ]

    for t in range(B_HEADS // HEADS_PER_TILE):
        q_t, k_b, v_b = tile(base, t), tile(base + B_W, t), tile(base + 2 * B_W, t)
        halves = [attend(jnp.where(in_half[p], q_t, 0.0), k_b, v_b, None) for p in range(HEADS_PER_TILE)]
        ob_ref[:, t * LANES:(t + 1) * LANES] = jnp.where(in_half[0], halves[0], halves[1])


def _ctx_attn(qkv, sink):
    kv_spec = lambda heads: pl.BlockSpec((1, SEQ * heads, HEAD_DIM), lambda b: (b, 0, 0))
    kv_sd = lambda heads: jax.ShapeDtypeStruct((BATCH, SEQ * heads, HEAD_DIM), F32)
    outs = pl.pallas_call(
        _ctx_attn_body,
        grid=(BATCH,),
        in_specs=[pl.BlockSpec((SEQ, ATTN_IN), lambda b: (b, 0)),
                  pl.BlockSpec((1, A_HEADS), lambda b: (0, 0))],
        out_specs=[pl.BlockSpec((SEQ, A_Q), lambda b: (b, 0)), pl.BlockSpec((SEQ, B_W), lambda b: (b, 0)),
                   kv_spec(A_KV_HEADS), kv_spec(A_KV_HEADS), kv_spec(B_HEADS), kv_spec(B_HEADS)],
        out_shape=[jax.ShapeDtypeStruct((T_CTX, A_Q), F32), jax.ShapeDtypeStruct((T_CTX, B_W), F32),
                   kv_sd(A_KV_HEADS), kv_sd(A_KV_HEADS), kv_sd(B_HEADS), kv_sd(B_HEADS)],
        compiler_params=_cparams("parallel"),
        name="ctx_attn",
    )(qkv, sink.reshape(1, A_HEADS))
    caches = [t.reshape(BATCH, 1, SEQ, -1, HEAD_DIM) for t in outs[2:]]
    return outs[0], outs[1], *caches


@functools.lru_cache(maxsize=None)
def _rope_tables(width):
    half = HEAD_DIM // 2
    t = np.arange(DEC_SEQ)
    inv = ROPE_BASE ** (-np.arange(0, half, 2, dtype=np.float64) / half)
    ang_r = (t // GRID_W)[:, None] * inv[None, :]
    ang_c = (t % GRID_W)[:, None] * inv[None, :]
    cos = np.concatenate([np.cos(ang_r)] * 2 + [np.cos(ang_c)] * 2, axis=-1)
    sin = np.concatenate([-np.sin(ang_r), np.sin(ang_r), -np.sin(ang_c), np.sin(ang_c)], axis=-1)
    reps = width // HEAD_DIM
    return (np.tile(cos, (1, reps)).astype(np.float32), np.tile(sin, (1, reps)).astype(np.float32))


def _rope_body(q_ref, k_ref, cq_ref, sq_ref, ck_ref, sk_ref, qo_ref, ko_ref):
    quarter = HEAD_DIM // 4

    def rot(x, cos, sin):
        w = x.shape[-1]
        lane = lax.broadcasted_iota(jnp.int32, x.shape, 1)
        fwd = pltpu.roll(x, w - quarter, axis=1)
        bwd = pltpu.roll(x, quarter, axis=1)
        partner = jnp.where((lane & (2 * quarter - 1)) < quarter, fwd, bwd)
        return x * cos + partner * sin

    qo_ref[...] = rot(q_ref[...], cq_ref[...], sq_ref[...])
    ko_ref[...] = rot(k_ref[...], ck_ref[...], sk_ref[...])


def _rope(qkv):
    cq, sq = _rope_tables(A_Q)
    ck, sk = _rope_tables(A_KV)
    tab = lambda w: pl.BlockSpec((DEC_SEQ, w), lambda b: (0, 0))
    row0 = T_CTX // DEC_SEQ
    return pl.pallas_call(
        _rope_body,
        grid=(DEC_BATCH,),
        in_specs=[pl.BlockSpec((DEC_SEQ, A_Q), lambda b: (row0 + b, 0)),
                  pl.BlockSpec((DEC_SEQ, A_KV), lambda b: (row0 + b, A_Q // A_KV)),
                  tab(A_Q), tab(A_Q), tab(A_KV), tab(A_KV)],
        out_specs=[pl.BlockSpec((DEC_SEQ, A_Q), lambda b: (b, 0)),
                   pl.BlockSpec((DEC_SEQ, A_KV), lambda b: (b, 0))],
        out_shape=[jax.ShapeDtypeStruct((T_LAT, A_Q), F32), jax.ShapeDtypeStruct((T_LAT, A_KV), F32)],
        compiler_params=_cparams("parallel"),
        name="rope",
    )(qkv, qkv, jnp.asarray(cq), jnp.asarray(sq), jnp.asarray(ck), jnp.asarray(sk))


WIN_QB = 256


def _pick_head(x, h, n_heads):
    out = x[:, _head_cols(0)]
    for i in range(1, n_heads):
        out = jnp.where(h == i, x[:, _head_cols(i)], out)
    return out


def _win_attn_body(qraw_ref, qrot_ref, krot_ref, v_ref, kc_ref, vc_ref, sink_ref, o_ref):
    scale = HEAD_DIM ** -0.5
    hk = pl.program_id(1)
    tiles = A_GROUP // HEADS_PER_TILE

    def kv_in_half(x):
        swapped = pltpu.roll(x, HEAD_DIM, axis=1)
        return [jnp.where(hk == p, x, swapped) for p in range(HEADS_PER_TILE)]

    k, v, kc, vc = kv_in_half(krot_ref[...]), kv_in_half(v_ref[...]), kv_in_half(kc_ref[0]), kv_in_half(vc_ref[0])
    head_lane = lax.broadcasted_iota(jnp.int32, (1, A_HEADS), 1)

    def sink_rows(p):
        heads = [hk * A_GROUP + j * HEADS_PER_TILE + p for j in range(tiles)]
        vals = [jnp.sum(jnp.where(head_lane == h, sink_ref[...], 0.0), axis=-1, keepdims=True) for h in heads]
        return jnp.concatenate([jnp.broadcast_to(s, (WIN_QB, 1)) for s in vals], axis=0)

    sinks = [sink_rows(p) for p in range(HEADS_PER_TILE)]
    lane = lax.broadcasted_iota(jnp.int32, (tiles * WIN_QB, LANES), 1)
    in_half = [lane < HEAD_DIM, lane >= HEAD_DIM]
    for qb in range(DEC_SEQ // WIN_QB):
        q0 = qb * WIN_QB
        rows = slice(q0, q0 + WIN_QB)
        lo = max(0, q0 - WINDOW)
        hi = min(DEC_SEQ, q0 + WIN_QB + WINDOW)
        q_rot = jnp.concatenate([qrot_ref[rows, j * LANES:(j + 1) * LANES] for j in range(tiles)], axis=0)
        q_raw = jnp.concatenate([qraw_ref[rows, j * LANES:(j + 1) * LANES] for j in range(tiles)], axis=0)
        halves = []
        for p in range(HEADS_PER_TILE):
            s_loc = _dot_nt(jnp.where(in_half[p], q_rot, 0.0), k[p][lo:hi]) * scale
            qpos = q0 + (lax.broadcasted_iota(jnp.int32, s_loc.shape, 0) & (WIN_QB - 1))
            kpos = lo + lax.broadcasted_iota(jnp.int32, s_loc.shape, 1)
            s_loc = jnp.where(jnp.abs(kpos - qpos) <= WINDOW, s_loc, MASK_NEG)
            s_ctx = _dot_nt(jnp.where(in_half[p], q_raw, 0.0), kc[p]) * scale
            m = jnp.maximum(jnp.maximum(jnp.max(s_loc, axis=-1, keepdims=True),
                                        jnp.max(s_ctx, axis=-1, keepdims=True)), sinks[p])
            p_loc = jnp.exp(s_loc - m)
            p_ctx = jnp.exp(s_ctx - m)
            den = (jnp.sum(p_loc, axis=-1, keepdims=True) + jnp.sum(p_ctx, axis=-1, keepdims=True)
                   + jnp.exp(sinks[p] - m))
            halves.append((_dot(p_ctx, vc[p]) + _dot(p_loc, v[p][lo:hi])) / den)
        o = jnp.where(in_half[0], halves[0], halves[1])
        for j in range(tiles):
            o_ref[rows, j * LANES:(j + 1) * LANES] = o[j * WIN_QB:(j + 1) * WIN_QB]


def _win_attn(qkv, q_rot, k_rot, kc, vc, sink):
    row0 = T_CTX // DEC_SEQ
    gw = A_GROUP * HEAD_DIM
    return pl.pallas_call(
        _win_attn_body,
        grid=(DEC_BATCH, A_KV_HEADS),
        in_specs=[pl.BlockSpec((DEC_SEQ, gw), lambda b, h: (row0 + b, h)),
                  pl.BlockSpec((DEC_SEQ, gw), lambda b, h: (b, h)),
                  pl.BlockSpec((DEC_SEQ, A_KV), lambda b, h: (b, 0)),
                  pl.BlockSpec((DEC_SEQ, A_KV), lambda b, h: (row0 + b, (A_Q + A_KV) // A_KV)),
                  pl.BlockSpec((1, PAST_LEN, A_KV), lambda b, h: (b, 0, 0)),
                  pl.BlockSpec((1, PAST_LEN, A_KV), lambda b, h: (b, 0, 0)),
                  pl.BlockSpec((1, A_HEADS), lambda b, h: (0, 0))],
        out_specs=pl.BlockSpec((DEC_SEQ, gw), lambda b, h: (b, h)),
        out_shape=jax.ShapeDtypeStruct((T_LAT, A_Q), F32),
        compiler_params=_cparams("parallel", "parallel"),
        name="win_attn",
    )(qkv, q_rot, k_rot, qkv, kc, vc, sink.reshape(1, A_HEADS))


GRID_ROWS = DEC_SEQ // GRID_W
NA_BAND = min(NA_ROWS, GRID_ROWS)


NA_REL_ROWS = 2 * NA_ROWS - 1
NA_REL_COLS = 2 * NA_COLS - 1
LANES = 128
HEADS_PER_TILE = LANES // HEAD_DIM


def _na_rel_rows(rpb):
    pad = jnp.zeros((B_HEADS, NA_REL_ROWS, GRID_W - NA_REL_COLS), F32)
    one = jnp.concatenate([rpb, pad], axis=-1)
    nxt = jnp.concatenate([one[:, 1:], jnp.zeros((B_HEADS, 1, GRID_W), F32)], axis=1)
    both = jnp.concatenate([one, nxt], axis=-1)
    return jnp.concatenate([both, jnp.zeros((B_HEADS, 16 - NA_REL_ROWS, LANES), F32)], axis=1)


NA_HEADS_PER_STEP = LANES // HEAD_DIM


def _na_row_groups():
    groups = []
    for r in range(GRID_ROWS):
        rs = min(max(r - NA_ROWS // 2, 0), GRID_ROWS - NA_BAND)
        if groups and groups[-1][2] == rs:
            groups[-1][1] += 1
        else:
            groups.append([r, 1, rs])
    return groups


def _na_attn_body(q_ref, k_ref, v_ref, kc_ref, vc_ref, rel_ref, o_ref):
    scale = HEAD_DIM ** -0.5
    cq = lax.broadcasted_iota(jnp.int32, (GRID_W, LANES), 0)
    kcol = lax.broadcasted_iota(jnp.int32, (GRID_W, LANES), 1) & (GRID_W - 1)
    cs = jnp.clip(cq - NA_COLS // 2, 0, GRID_W - NA_COLS)
    col_ok = (kcol >= cs) & (kcol < cs + NA_COLS)
    kc = kc_ref[0]
    vc = vc_ref[0]
    tiles = {}

    def pair_tile(hh, a):
        if (hh, a) not in tiles:
            x = jnp.broadcast_to(rel_ref[hh, a:a + 1, :], (GRID_W, LANES))
            t = pltpu.roll(x, LANES - (NA_COLS - 1), axis=1, stride=1, stride_axis=0)
            tiles[hh, a] = jnp.where(col_ok, t, MASK_NEG)
        return tiles[hh, a]

    for r0, n_r, rs in _na_row_groups():
        rows = slice(r0 * GRID_W, (r0 + n_r) * GRID_W)
        band = slice(rs * GRID_W, (rs + NA_BAND) * GRID_W)
        q_t, k_t, v_t = q_ref[rows, :], k_ref[band, :], v_ref[band, :]
        head_of_lane = lax.broadcasted_iota(jnp.int32, q_t.shape, 1) >> (HEAD_DIM.bit_length() - 1)
        o = jnp.zeros(q_t.shape, F32)
        for hh in range(NA_HEADS_PER_STEP):
            bias = jnp.concatenate(
                [jnp.concatenate([pair_tile(hh, rs - r + NA_ROWS - 1 + 2 * i) for i in range(NA_BAND // 2)], axis=1)
                 for r in range(r0, r0 + n_r)], axis=0)
            q = jnp.where(head_of_lane == hh, q_t, 0.0)
            s_loc = _dot_nt(q, k_t) * scale + bias
            s_ctx = _dot_nt(q, kc) * scale
            m = jnp.maximum(jnp.max(s_loc, axis=-1, keepdims=True), jnp.max(s_ctx, axis=-1, keepdims=True))
            p_loc = jnp.exp(s_loc - m)
            p_ctx = jnp.exp(s_ctx - m)
            den = jnp.sum(p_loc, axis=-1, keepdims=True) + jnp.sum(p_ctx, axis=-1, keepdims=True)
            o = jnp.where(head_of_lane == hh, (_dot(p_ctx, vc) + _dot(p_loc, v_t)) / den, o)
        o_ref[rows, :] = o


def _na_attn(qkv, kc, vc, rel):
    row0 = T_CTX // DEC_SEQ
    col0 = (A_Q + 2 * A_KV) // LANES
    n_blk = B_W // LANES
    col = lambda j: pl.BlockSpec((DEC_SEQ, LANES), lambda b, p: (row0 + b, col0 + j * n_blk + p))
    cache = pl.BlockSpec((1, PAST_LEN, LANES), lambda b, p: (b, 0, p))
    return pl.pallas_call(
        _na_attn_body,
        grid=(DEC_BATCH, n_blk),
        in_specs=[col(0), col(1), col(2), cache, cache,
                  pl.BlockSpec((NA_HEADS_PER_STEP, 16, LANES), lambda b, p: (p, 0, 0))],
        out_specs=pl.BlockSpec((DEC_SEQ, LANES), lambda b, p: (b, p)),
        out_shape=jax.ShapeDtypeStruct((T_LAT, B_W), F32),
        compiler_params=_cparams("parallel", "parallel"),
        name="na_attn",
    )(qkv, qkv, qkv, kc, vc, rel)


@functools.lru_cache(maxsize=None)
def _dft_mats(L):
    n = 2 * L
    fc = min(L, DFT_CHUNK)
    f = np.arange(L)[:, None]
    t = np.arange(L)[None, :]
    ang = 2.0 * np.pi * ((f * t) % n) / n
    m1 = np.cos(ang)
    m2 = np.sin(ang)
    m2[0, :] = np.where(np.arange(L) % 2 == 0, 1.0, -1.0)
    wgt = np.full((L, 1), 2.0)
    wgt[0, 0] = 1.0
    nch = L // fc
    fwd = np.concatenate([m1.reshape(nch, fc, L), m2.reshape(nch, fc, L)], axis=1)
    inv = np.concatenate([(m1 * wgt / n).reshape(nch, fc, L), (m2 * wgt / n).reshape(nch, fc, L)], axis=1)
    inv = np.transpose(inv, (0, 2, 1))
    return fwd.astype(np.float32), inv.astype(np.float32)


@functools.lru_cache(maxsize=None)
def _filter_consts(L):
    t = np.linspace(0.0, 1.0, L)[:, None]
    bands = (C_EMB - 1) // 2
    ang = (2.0 * math.pi / L) * np.arange(L)[:, None] * np.linspace(1e-4, bands - 1, bands)[None, :]
    z = np.concatenate([t, np.cos(ang), -np.sin(ang)], axis=-1)
    zpad = np.zeros((L, 128))
    zpad[:, :C_EMB] = z
    deltas = np.abs(np.linspace(HYENA_MIN_DECAY, HYENA_MAX_DECAY, C_DIM))
    window = np.exp(-t * deltas[None, :])
    return zpad.astype(np.float32), window.astype(np.float32)


def _filter_body(z_ref, w1_ref, b1_ref, w2_ref, b2_ref, w3_ref, b3_ref, fr_ref, w4_ref, win_ref, fm_ref,
                 hr_ref, g_ref, hq_ref, hs_scr, hd_scr):
    c = pl.program_id(0)
    fc = hr_ref.shape[0]

    @pl.when(c == 0)
    def _():
        fr = fr_ref[...]
        hh = jnp.sin(fr * (_dot_hi(z_ref[...], w1_ref[...]) + b1_ref[...]))
        hh = jnp.sin(fr * (_dot_hi(hh, w2_ref[...]) + b2_ref[...]))
        hh = jnp.sin(fr * (_dot_hi(hh, w3_ref[...]) + b3_ref[...]))
        hh = _dot_hi(hh, w4_ref[...])
        hf = hh[:, :C_DIM] * win_ref[...]
        hb = hh[:, C_DIM:] * win_ref[...]
        hs_scr[...] = hf + hb
        hd_scr[...] = hf - hb

    fm = fm_ref[0]
    hr = _dot_split(fm[:fc], hs_scr[...])
    first = (lax.broadcasted_iota(jnp.int32, (fc, C_DIM), 0) == 0) & (c == 0)
    hr_ref[...] = hr
    g_ref[...] = jnp.where(first, 0.0, _dot_split(fm[fc:], hd_scr[...]))
    hs = hs_scr[...]
    sign = jnp.where((lax.broadcasted_iota(jnp.int32, hs.shape, 0) & 1) == 0, 1.0, -1.0)
    hq_ref[...] = jnp.where(first, jnp.sum(hs * sign, axis=0, keepdims=True), hr)


def _hyena_filter(L, filt):
    w1, b1, w2, b2, w3, b3, freq, w4 = filt
    zpad, window = _filter_consts(L)
    fwd, _ = _dft_mats(L)
    nch, fc2, _ = fwd.shape
    fc = fc2 // 2
    w1p = jnp.pad(w1, ((0, 128 - C_EMB), (0, 0)))
    full = lambda shape: pl.BlockSpec(shape, lambda c: tuple(0 for _ in shape))
    out_spec = pl.BlockSpec((fc, C_DIM), lambda c: (c, 0))
    out_sd = jax.ShapeDtypeStruct((L, C_DIM), F32)
    return pl.pallas_call(
        _filter_body,
        grid=(nch,),
        in_specs=[full((L, 128)), full((128, C_FFN)), full((1, C_FFN)), full((C_FFN, C_FFN)), full((1, C_FFN)),
                  full((C_FFN, C_FFN)), full((1, C_FFN)), full((1, C_FFN)), full((C_FFN, 2 * C_DIM)),
                  full((L, C_DIM)), pl.BlockSpec((1, fc2, L), lambda c: (c, 0, 0))],
        out_specs=[out_spec, out_spec, out_spec],
        out_shape=[out_sd, out_sd, out_sd],
        scratch_shapes=[pltpu.VMEM((L, C_DIM), F32), pltpu.VMEM((L, C_DIM), F32)],
        compiler_params=_cparams("arbitrary"),
        name="hyena_filter",
    )(jnp.asarray(zpad), w1p, b1.reshape(1, C_FFN), w2, b2.reshape(1, C_FFN), w3, b3.reshape(1, C_FFN),
      freq.reshape(1, C_FFN), w4, jnp.asarray(window), jnp.asarray(fwd))


def _hyena_body(u_ref, cw_ref, cb_ref, d_ref, fm_ref, fi_ref, hr_ref, g_ref, hq_ref, y_ref,
                x0_scr, z_scr, acc_scr):
    c = pl.program_id(1)
    L = y_ref.shape[0]
    fc = hr_ref.shape[0]

    @pl.when(c == 0)
    def _():
        row = lax.broadcasted_iota(jnp.int32, (L, C_DIM), 0)

        def short_conv(sec):
            cols = slice(sec * C_DIM, (sec + 1) * C_DIM)
            u = u_ref[:, cols]
            prev = jnp.where(row == 0, 0.0, pltpu.roll(u, 1, axis=0))
            nxt = jnp.where(row == L - 1, 0.0, pltpu.roll(u, L - 1, axis=0))
            return (prev * cw_ref[0:1, cols] + u * cw_ref[1:2, cols] + nxt * cw_ref[2:3, cols]
                    + cb_ref[:, cols])

        x0_scr[...] = short_conv(0)
        z_scr[...] = short_conv(1) * short_conv(2)
        acc_scr[...] = jnp.zeros((L, C_DIM), F32)

    ab = _dot_split(fm_ref[0], z_scr[...])
    a, b = ab[:fc], ab[fc:]
    hr, g, hq = hr_ref[...], g_ref[...], hq_ref[...]
    pq = jnp.concatenate([a * hr - b * g, a * g + b * hq], axis=0)
    acc_scr[...] += _dot_split(fi_ref[0], pq)

    @pl.when(c == pl.num_programs(1) - 1)
    def _():
        y_ref[...] = x0_scr[...] * (acc_scr[...] + z_scr[...] * d_ref[...])


def _hyena(u, row_blk0, n_seq, L, conv_w, conv_b, d_skip, spec):
    hr, g, hq = spec
    fwd, inv = _dft_mats(L)
    nch, fc2, _ = fwd.shape
    fc = fc2 // 2
    u_w = 3 * C_DIM
    return pl.pallas_call(
        _hyena_body,
        grid=(n_seq, nch),
        in_specs=[pl.BlockSpec((L, u_w), lambda b, c: (row_blk0 + b, 0)),
                  pl.BlockSpec((3, u_w), lambda b, c: (0, 0)),
                  pl.BlockSpec((1, u_w), lambda b, c: (0, 0)),
                  pl.BlockSpec((1, C_DIM), lambda b, c: (0, 0)),
                  pl.BlockSpec((1, fc2, L), lambda b, c: (c, 0, 0)),
                  pl.BlockSpec((1, L, fc2), lambda b, c: (c, 0, 0)),
                  pl.BlockSpec((fc, C_DIM), lambda b, c: (c, 0)),
                  pl.BlockSpec((fc, C_DIM), lambda b, c: (c, 0)),
                  pl.BlockSpec((fc, C_DIM), lambda b, c: (c, 0))],
        out_specs=pl.BlockSpec((L, C_DIM), lambda b, c: (b, 0)),
        out_shape=jax.ShapeDtypeStruct((n_seq * L, C_DIM), F32),
        scratch_shapes=[pltpu.VMEM((L, C_DIM), F32)] * 3,
        compiler_params=_cparams("parallel", "arbitrary"),
        name="hyena",
    )(u, conv_w, conv_b.reshape(1, u_w), d_skip.reshape(1, C_DIM), jnp.asarray(fwd), jnp.asarray(inv), hr, g, hq)


def _hgrn_body(q_ref, ff_ref, fb_ref, i_ref, g_ref, lbf_ref, lbb_ref, nd_ref, s0f_ref, s0b_ref,
               o_ref, sf_ref, sb_ref, *, layer):
    L = o_ref.shape[0]
    C = GLA_CHUNK
    S = min(L, GLA_SPAN)
    nc = S // C
    n_span = L // S
    mid = C // 2
    def lower_bound(gm):
        e = jnp.exp(gm - jnp.max(gm, axis=0, keepdims=True))
        p = e / jnp.sum(e, axis=0, keepdims=True)
        return jnp.sum(p[0:layer + 1], axis=0, keepdims=True) - p[0:1]

    def gates(fx, lb):
        f = lb + (1.0 - lb) * jax.nn.sigmoid(fx)
        return 1.0 - f, jnp.log(f)


    chunk_shift = C.bit_length() - 1
    block_shift = D_KDIM.bit_length() - 1
    ti = lax.broadcasted_iota(jnp.int32, (S, S), 0)
    si = lax.broadcasted_iota(jnp.int32, (S, S), 1)
    same_chunk = (ti >> chunk_shift) == (si >> chunk_shift)
    causal = same_chunk & (si <= ti)
    anti = same_chunk & (si >= ti)
    row_chunk = lax.broadcasted_iota(jnp.int32, (S, nc * D_KDIM), 0) >> chunk_shift
    col_chunk = lax.broadcasted_iota(jnp.int32, (S, nc * D_KDIM), 1) >> block_shift
    own_block = row_chunk == col_chunk

    def spread(x):
        return jnp.where(own_block, jnp.concatenate([x] * nc, axis=1), 0.0)

    def chunk_cumsum(mask, lg):
        tri = mask.astype(BF16)
        hi = lg.astype(BF16)
        r1 = lg - hi.astype(F32)
        mid_t = r1.astype(BF16)
        lo = (r1 - mid_t.astype(F32)).astype(BF16)
        dot = lambda t: jnp.dot(tri, t, preferred_element_type=F32)
        return dot(hi) + dot(mid_t) + dot(lo)

    def per_chunk_rows(b, pos):
        return jnp.concatenate([jnp.broadcast_to(b[n * C + pos:n * C + pos + 1], (C, D_KDIM)) for n in range(nc)],
                               axis=0)

    def one_head(q, v, kf, lgf, kb, lgb, st_f, st_b):
        local = []
        for u in range(n_span):
            rows = slice(u * S, (u + 1) * S)
            qs, vs, kfs, kbs = q[rows], v[rows], kf[rows], kb[rows]
            lgs = jnp.concatenate([lgf[rows], lgb[rows]], axis=1)
            pre = chunk_cumsum(causal, lgs)
            b_f = pre[:, :D_KDIM]
            pre_b = pre[:, D_KDIM:]
            b_b = per_chunk_rows(pre_b, C - 1) - pre_b + lgb[rows]
            ref_f, ref_b = per_chunk_rows(b_f, mid), per_chunk_rows(b_b, mid)
            sc = (jnp.where(causal, _dot_nt(qs * jnp.exp(b_f - ref_f), kfs * jnp.exp(ref_f - b_f)), 0.0)
                  + jnp.where(anti, _dot_nt(qs * jnp.exp(b_b - ref_b), kbs * jnp.exp(ref_b - b_b)), 0.0))
            k_out = jnp.concatenate([kfs * jnp.exp(per_chunk_rows(b_f, C - 1) - b_f),
                                     kbs * jnp.exp(per_chunk_rows(b_b, 0) - b_b)], axis=1)
            kv_t = _dot_tn(spread(vs), k_out)
            local.append((_dot(sc, vs), kv_t, b_f, b_b, qs))

        states_f = [[None] * nc for _ in range(n_span)]
        for u in range(n_span):
            _, kv_t, b_f, _, _ = local[u]
            for n in range(nc):
                states_f[u][n] = st_f
                st_f = st_f * jnp.exp(b_f[n * C + C - 1:n * C + C]) + kv_t[n * D_VDIM:(n + 1) * D_VDIM, :D_KDIM]
        states_b = [[None] * nc for _ in range(n_span)]
        for u in reversed(range(n_span)):
            _, kv_t, _, b_b, _ = local[u]
            for n in reversed(range(nc)):
                states_b[u][n] = st_b
                st_b = st_b * jnp.exp(b_b[n * C:n * C + 1]) + kv_t[n * D_VDIM:(n + 1) * D_VDIM, D_KDIM:]

        outs = []
        for u in range(n_span):
            intra, _, b_f, b_b, qs = local[u]
            q_in = jnp.concatenate([spread(qs * jnp.exp(b_f)), spread(qs * jnp.exp(b_b))], axis=1)
            outs.append(intra + _dot_nt(q_in, jnp.concatenate(states_f[u] + states_b[u], axis=1)))
        return (jnp.concatenate(outs, axis=0) if n_span > 1 else outs[0]), st_f, st_b

    for hh in range(o_ref.shape[1] // D_VDIM):
        cols = slice(hh * D_KDIM, (hh + 1) * D_KDIM)
        kf, lgf = gates(ff_ref[:, cols], lower_bound(lbf_ref[:, cols]))
        kb, lgb = gates(fb_ref[:, cols], lower_bound(lbb_ref[:, cols]))
        o, st_f, st_b = one_head(_silu(q_ref[:, cols]), i_ref[:, cols], kf, lgf, kb, lgb,
                                 jnp.transpose(s0f_ref[0, hh]), jnp.transpose(s0b_ref[0, hh]))
        sf_ref[0, hh] = jnp.transpose(st_f)
        sb_ref[0, hh] = jnp.transpose(st_b)
        o_ref[:, cols] = _rms(o, nd_ref[...]) * _silu(g_ref[:, cols])


def _hgrn(u, row_blk0, n_seq, L, lb_fwd, lb_bwd, norm_d, s0f, s0b, layer):
    hps = HGRN_HEADS_PER_STEP
    width = hps * D_KDIM
    col0 = 3 * C_DIM // width
    groups = D_HEADS // hps
    col = lambda j: pl.BlockSpec((L, width), lambda b, h: (row_blk0 + b, col0 + j * groups + h))
    lbs = pl.BlockSpec((DEPTH, width), lambda b, h: (0, h))
    st = pl.BlockSpec((1, hps, D_KDIM, D_VDIM), lambda b, h: (b, h, 0, 0))
    st_sd = jax.ShapeDtypeStruct((n_seq, D_HEADS, D_KDIM, D_VDIM), F32)
    return pl.pallas_call(
        functools.partial(_hgrn_body, layer=layer),
        grid=(n_seq, groups),
        in_specs=[col(0), col(1), col(2), col(3), col(4), lbs, lbs,
                  pl.BlockSpec((1, D_VDIM), lambda b, h: (0, 0)), st, st],
        out_specs=[pl.BlockSpec((L, width), lambda b, h: (b, h)), st, st],
        out_shape=[jax.ShapeDtypeStruct((n_seq * L, D_HEADS * D_VDIM), F32), st_sd, st_sd],
        compiler_params=_cparams("parallel", "parallel"),
        name="hgrn",
    )(u, u, u, u, u, lb_fwd, lb_bwd, norm_d.reshape(1, D_VDIM), s0f, s0b)


def _pack_bf16_pairs(h):
    n = h.shape[1] // 2
    hi = lax.bitcast_convert_type(h[:, :n].astype(BF16).astype(F32), jnp.int32)
    lo = lax.bitcast_convert_type(h[:, n:].astype(BF16).astype(F32), jnp.int32)
    return hi | lax.shift_right_logical(lo, 16)


def _unpack_bf16_pairs(p):
    hi = lax.bitcast_convert_type(p & jnp.int32(-65536), F32).astype(BF16)
    lo = lax.bitcast_convert_type(lax.shift_left(p, 16), F32).astype(BF16)
    return hi, lo


def _outproj_body(*refs, n_x):
    a_refs, b_refs, x_refs = refs[0:2], refs[2:4], refs[4:4 + n_x]
    mod_ref, gf_ref, w_ref, wrh_ref, wrl_ref, rb_ref, x1_ref, h2_ref, chosen_ref, gk_ref, ik_ref = refs[4 + n_x:]
    m = mod_ref[0]
    half = a_refs[0].shape[1]
    out = _dot(_token_tile(a_refs), w_ref[0:half, :]) + _dot(_token_tile(b_refs), w_ref[half:, :])
    x1 = _token_tile(x_refs) + m[:, 2 * D_MODEL:3 * D_MODEL] * out
    x1_ref[...] = x1
    h2 = _rms(x1, gf_ref[...]) * (1.0 + m[:, 4 * D_MODEL:5 * D_MODEL]) + m[:, 3 * D_MODEL:4 * D_MODEL]
    h2_ref[...] = _pack_bf16_pairs(h2)
    h_hi = h2.astype(BF16)
    h_lo = (h2 - h_hi.astype(F32)).astype(BF16)
    logits = _dot_nt(wrh_ref[...], h_hi) + _dot_nt(wrh_ref[...], h_lo) + _dot_nt(wrl_ref[...], h_hi)
    scores = jax.nn.sigmoid(logits)
    work = scores + rb_ref[...]
    expert = lax.broadcasted_iota(jnp.int32, work.shape, 0).astype(F32)
    slot = lax.broadcasted_iota(jnp.int32, (TOP_K, work.shape[1]), 0)
    chosen = jnp.zeros(work.shape, F32)
    gk = jnp.zeros((TOP_K, work.shape[1]), F32)
    ik = jnp.zeros((TOP_K, work.shape[1]), F32)
    for k in range(TOP_K):
        best = jnp.max(work, axis=0, keepdims=True)
        first = jnp.min(jnp.where(work == best, expert, float(N_EXPERTS)), axis=0, keepdims=True)
        hit = expert == first
        chosen = jnp.where(hit, 1.0, chosen)
        gk = jnp.where(slot == k, jnp.sum(jnp.where(hit, scores, 0.0), axis=0, keepdims=True), gk)
        ik = jnp.where(slot == k, first, ik)
        work = jnp.where(hit, -jnp.inf, work)
    chosen_ref[...] = chosen
    gk_ref[...] = jnp.transpose(gk / jnp.sum(gk, axis=0, keepdims=True) * ROUTE_SCALE)
    ik_ref[...] = ik


def _outproj(a, b, x, mod_l, gain_ffn, w_out, w_router, router_bias):
    half = a[0].shape[1]
    a_specs, a_args = _token_specs(a, half)
    b_specs, b_args = _token_specs(b, half)
    x_specs, x_args = _token_specs(x, D_MODEL)
    wr_t = w_router.T
    wr_hi = wr_t.astype(BF16)
    wr_lo = (wr_t - wr_hi.astype(F32)).astype(BF16)
    return pl.pallas_call(
        functools.partial(_outproj_body, n_x=len(x_args)),
        grid=(T_ALL // TM,),
        in_specs=a_specs + b_specs + x_specs + [
                  pl.BlockSpec((1, 1, N_MOD * D_MODEL), lambda i: (_mod_row(i), 0, 0)),
                  pl.BlockSpec((1, D_MODEL), lambda i: (0, 0)),
                  pl.BlockSpec((2 * half, D_MODEL), lambda i: (0, 0)),
                  pl.BlockSpec((N_EXPERTS, D_MODEL), lambda i: (0, 0)),
                  pl.BlockSpec((N_EXPERTS, D_MODEL), lambda i: (0, 0)),
                  pl.BlockSpec((N_EXPERTS, 1), lambda i: (0, 0))],
        out_specs=[pl.BlockSpec((TM, D_MODEL), lambda i: (i, 0)),
                   pl.BlockSpec((TM, D_MODEL // 2), lambda i: (i, 0)),
                   pl.BlockSpec((N_EXPERTS, TM), lambda i: (0, i)),
                   pl.BlockSpec((TM, TOP_K), lambda i: (i, 0)),
                   pl.BlockSpec((TOP_K, TM), lambda i: (0, i))],
        out_shape=[jax.ShapeDtypeStruct((T_ALL, D_MODEL), F32),
                   jax.ShapeDtypeStruct((T_ALL, D_MODEL // 2), jnp.int32),
                   jax.ShapeDtypeStruct((N_EXPERTS, T_ALL), F32),
                   jax.ShapeDtypeStruct((T_ALL, TOP_K), F32),
                   jax.ShapeDtypeStruct((TOP_K, T_ALL), F32)],
        compiler_params=_cparams("parallel"),
        name="outproj_router",
    )(*a_args, *b_args, *x_args, mod_l, gain_ffn.reshape(1, D_MODEL), w_out, wr_hi, wr_lo,
      router_bias.reshape(N_EXPERTS, 1))


def _route_body(chosen_ref, ik_ref, dest_ref, first_ref, count_ref, short_ref, pos_scr):
    n_tiles = T_ALL // TM
    r = lax.broadcasted_iota(jnp.int32, (TM, TM), 0)
    c = lax.broadcasted_iota(jnp.int32, (TM, TM), 1)
    before = (r < c).astype(BF16)

    counts = jnp.zeros((N_EXPERTS, 1), F32)
    for i in range(n_tiles):
        cols = slice(i * TM, (i + 1) * TM)
        m = chosen_ref[:, cols]
        pos_scr[:, cols] = jnp.dot(m.astype(BF16), before, preferred_element_type=F32) + counts
        counts = counts + jnp.sum(m, axis=1, keepdims=True)
    padded = jnp.ceil(counts * (1.0 / MOE_BLK)) * MOE_BLK
    ei = lax.broadcasted_iota(jnp.int32, (N_EXPERTS, N_EXPERTS), 0)
    ej = lax.broadcasted_iota(jnp.int32, (N_EXPERTS, N_EXPERTS), 1)
    end = _dot_hi((ej <= ei).astype(F32), jnp.broadcast_to(padded, (N_EXPERTS, LANES)))[:, 0:1]
    start = end - padded

    expert = lax.broadcasted_iota(jnp.int32, (N_EXPERTS, TM), 0).astype(F32)
    slot = lax.broadcasted_iota(jnp.int32, (TOP_K, TM), 0)
    for i in range(n_tiles):
        cols = slice(i * TM, (i + 1) * TM)
        row_of = pos_scr[:, cols] + start
        ik = ik_ref[:, cols]
        acc = jnp.zeros((TOP_K, TM), F32)
        for k in range(TOP_K):
            pick = jnp.sum(jnp.where(expert == ik[k:k + 1, :], row_of, 0.0), axis=0, keepdims=True)
            acc = jnp.where(slot == k, pick, acc)
        dest_ref[:, cols] = acc.astype(jnp.int32)
    first_ref[...] = jnp.broadcast_to(start * (1.0 / MOE_BLK), (N_EXPERTS, LANES)).astype(jnp.int32)
    count_ref[...] = jnp.broadcast_to(padded * (1.0 / MOE_BLK), (N_EXPERTS, LANES)).astype(jnp.int32)
    in_last = counts - (padded - MOE_BLK)
    short = jnp.where((counts > 0.0) & (in_last <= MOE_BLK // 2), 1.0, 0.0)
    short_ref[...] = jnp.broadcast_to(short, (N_EXPERTS, LANES)).astype(jnp.int32)


def _route(chosen, ik):
    full = lambda shape: pl.BlockSpec(shape, lambda i: (0, 0))
    return pl.pallas_call(
        _route_body,
        grid=(1,),
        in_specs=[full((N_EXPERTS, T_ALL)), full((TOP_K, T_ALL))],
        out_specs=[full((TOP_K, T_ALL))] + [full((N_EXPERTS, LANES))] * 3,
        out_shape=[jax.ShapeDtypeStruct((TOP_K, T_ALL), jnp.int32)]
                  + [jax.ShapeDtypeStruct((N_EXPERTS, LANES), jnp.int32)] * 3,
        scratch_shapes=[pltpu.VMEM((N_EXPERTS, T_ALL), F32)],
        compiler_params=_cparams("arbitrary"),
        name="moe_route",
    )(chosen, ik)


def _sc_worker_id():
    return lax.axis_index("s") * SC_CORES + lax.axis_index("c")


def _sc_dispatch(h2p, dest):
    n_chunks = T_ALL // DISP_CHUNK
    k_per = TOP_K // DISP_SPLIT
    items_per_worker = n_chunks * DISP_SPLIT // SC_WORKERS
    chunk_stride = SC_WORKERS // DISP_SPLIT
    width = h2p.shape[1]
    mesh = plsc.VectorSubcoreMesh(core_axis_name="c", subcore_axis_name="s")

    @functools.partial(
        pl.kernel, mesh=mesh,
        out_type=jax.ShapeDtypeStruct((MOE_ROWS, width), jnp.int32),
        scratch_types=[pltpu.VMEM((k_per, DISP_CHUNK), jnp.int32), pltpu.VMEM((DISP_CHUNK, width), jnp.int32)],
    )
    def run(x_hbm, dest_hbm, xs_hbm, idx_v, rows_v):
        wid = _sc_worker_id()
        group = wid % DISP_SPLIT
        for i in range(items_per_worker):
            chunk = i * chunk_stride + wid // DISP_SPLIT
            tokens = pl.ds(pl.multiple_of(chunk * DISP_CHUNK, DISP_CHUNK), DISP_CHUNK)
            pltpu.sync_copy(dest_hbm.at[group, :, tokens], idx_v)
            pltpu.sync_copy(x_hbm.at[tokens], rows_v)
            for k in range(k_per):
                pltpu.sync_copy(rows_v, xs_hbm.at[idx_v.at[k]])

    return run(h2p, dest.reshape(DISP_SPLIT, k_per, T_ALL))


def _sc_collect(y, dest_flat):
    per_worker = T_ALL // SC_WORKERS
    n_chunks = per_worker // COLLECT_CHUNK
    n_steps = TOP_K * n_chunks
    width = y.shape[1]
    mesh = plsc.VectorSubcoreMesh(core_axis_name="c", subcore_axis_name="s")

    @functools.partial(
        pl.kernel, mesh=mesh,
        out_type=jax.ShapeDtypeStruct((TOP_K * T_ALL, width), y.dtype),
        scratch_types=[pltpu.VMEM((TOP_K * per_worker,), jnp.int32),
                       pltpu.VMEM((COLLECT_CHUNK, width), y.dtype), pltpu.VMEM((COLLECT_CHUNK, width), y.dtype),
                       pltpu.SemaphoreType.DMA, pltpu.SemaphoreType.DMA],
    )
    def run(y_hbm, dest_hbm, yg_hbm, idx_v, rows0, rows1, sem0, sem1):
        wid = _sc_worker_id()
        bufs = ((rows0, sem0), (rows1, sem1))
        for k in range(TOP_K):
            pltpu.sync_copy(dest_hbm.at[pl.ds(k * T_ALL + wid * per_worker, per_worker)],
                            idx_v.at[pl.ds(k * per_worker, per_worker)])

        def gather(step, buf):
            rows, sem = buf
            idx = idx_v.at[pl.ds(pl.multiple_of(step * COLLECT_CHUNK, 8), COLLECT_CHUNK)]
            return pltpu.make_async_copy(y_hbm.at[idx], rows, sem)

        def out_rows(step):
            off = (step // n_chunks) * T_ALL + wid * per_worker + (step % n_chunks) * COLLECT_CHUNK
            return yg_hbm.at[pl.ds(pl.multiple_of(off, 8), COLLECT_CHUNK)]

        gather(0, bufs[0]).start()

        @pl.loop(0, n_steps, step=2)
        def _(base):
            for j in range(2):
                step = base + j

                @pl.when(step + 1 < n_steps)
                def _():
                    gather(step + 1, bufs[1 - j]).start()

                gather(step, bufs[j]).wait()
                pltpu.sync_copy(bufs[j][0], out_rows(step))

    return run(y, dest_flat)


def _expert_body(first_ref, count_ref, short_ref, xs_hbm, wg_ref, wu_ref, wd_ref, y_hbm,
                 wg_bf, wu_bf, wd_bf, x_buf, y_buf, in_sem, out_sem):
    e = pl.program_id(0)
    first = first_ref[e]
    count = count_ref[e]
    n_used = first_ref[N_EXPERTS - 1] + count_ref[N_EXPERTS - 1]
    half = D_MODEL // 2
    wg_bf[...] = wg_ref[0, 0].astype(BF16)
    wu_bf[...] = wu_ref[0, 0].astype(BF16)
    wd_bf[...] = wd_ref[0, 0].astype(BF16)

    def part_rows(g, part, n_parts):
        size = MOE_BLK // n_parts
        return pl.ds(pl.multiple_of(g * MOE_BLK + part * size, size), size), pl.ds(part * size, size)

    def in_copies(g):
        slot = g & (EXPERT_SLOTS - 1)
        out = []
        for part in range(EXPERT_IN_PARTS):
            src, dst = part_rows(g, part, EXPERT_IN_PARTS)
            out.append(pltpu.make_async_copy(xs_hbm.at[src], x_buf.at[slot, dst], in_sem.at[slot]))
        return out

    def out_copies(g):
        slot = g & (EXPERT_SLOTS - 1)
        out = []
        for part in range(EXPERT_OUT_PARTS):
            dst, src = part_rows(g, part, EXPERT_OUT_PARTS)
            out.append(pltpu.make_async_copy(y_buf.at[slot, src], y_hbm.at[dst], out_sem.at[slot]))
        return out

    @pl.when((first == 0) & (count > 0))
    def _():
        for ahead in range(EXPERT_SLOTS - 1):
            @pl.when(ahead < n_used)
            def _():
                for cp in in_copies(ahead):
                    cp.start()

    def block(b, carry):
        g = first + b
        slot = g & (EXPERT_SLOTS - 1)
        for cp in in_copies(g):
            cp.wait()

        @pl.when(g + EXPERT_SLOTS - 1 < n_used)
        def _():
            for cp in in_copies(g + EXPERT_SLOTS - 1):
                cp.start()

        @pl.when(g >= EXPERT_SLOTS)
        def _():
            for cp in out_copies(g - EXPERT_SLOTS):
                cp.wait()

        def ffn(n_rows):
            hi, lo = _unpack_bf16_pairs(x_buf[slot, 0:n_rows])

            def proj(w_bf):
                return (jnp.dot(hi, w_bf[0:half, :], preferred_element_type=F32)
                        + jnp.dot(lo, w_bf[half:, :], preferred_element_type=F32))

            hid = _silu(proj(wg_bf)) * proj(wu_bf)
            y_buf[slot, 0:n_rows] = _pack_bf16_pairs(
                jnp.dot(hid.astype(BF16), wd_bf[...], preferred_element_type=F32))

        short = (b == count - 1) & (short_ref[e] == 1)

        @pl.when(short)
        def _():
            ffn(MOE_BLK // 2)
            y_buf[slot, MOE_BLK // 2:MOE_BLK] = jnp.zeros((MOE_BLK // 2, D_MODEL // 2), jnp.int32)

        @pl.when(jnp.logical_not(short))
        def _():
            ffn(MOE_BLK)

        for cp in out_copies(g):
            cp.start()
        return carry

    lax.fori_loop(0, count, block, 0)

    @pl.when(e == N_EXPERTS - 1)
    def _():
        for back in range(EXPERT_SLOTS, 0, -1):
            @pl.when(n_used >= back)
            def _():
                for cp in out_copies(n_used - back):
                    cp.wait()


EXPERT_SLOTS = 4
EXPERT_IN_PARTS = 2
EXPERT_OUT_PARTS = 4


def _experts(first_blk, n_blk, short_last, xs, layer, w_gate, w_up, w_down):
    w_in = pl.BlockSpec((1, 1, D_MODEL, D_EXPERT), lambda e, *_: (layer, e, 0, 0))
    grid_spec = pltpu.PrefetchScalarGridSpec(
        num_scalar_prefetch=3,
        grid=(N_EXPERTS,),
        in_specs=[pl.BlockSpec(memory_space=pl.ANY), w_in, w_in,
                  pl.BlockSpec((1, 1, D_EXPERT, D_MODEL), lambda e, *_: (layer, e, 0, 0))],
        out_specs=pl.BlockSpec(memory_space=pl.ANY),
        scratch_shapes=[pltpu.VMEM((D_MODEL, D_EXPERT), BF16), pltpu.VMEM((D_MODEL, D_EXPERT), BF16),
                        pltpu.VMEM((D_EXPERT, D_MODEL), BF16),
                        pltpu.VMEM((EXPERT_SLOTS, MOE_BLK, D_MODEL // 2), jnp.int32),
                        pltpu.VMEM((EXPERT_SLOTS, MOE_BLK, D_MODEL // 2), jnp.int32),
                        pltpu.SemaphoreType.DMA((EXPERT_SLOTS,)), pltpu.SemaphoreType.DMA((EXPERT_SLOTS,))],
    )
    return pl.pallas_call(
        _expert_body,
        grid_spec=grid_spec,
        out_shape=jax.ShapeDtypeStruct((MOE_ROWS, D_MODEL // 2), jnp.int32),
        compiler_params=_cparams("arbitrary"),
        name="moe_experts",
    )(first_blk, n_blk, short_last, xs, w_gate, w_up, w_down)


def _combine_body(x1_ref, h2_ref, yg_ref, gk_ref, mod_ref, sg_ref, su_ref, sd_ref, fn_ref, *o_refs, final):
    hi, lo = _unpack_bf16_pairs(h2_ref[...])
    half = D_MODEL // 2

    def proj(w_ref):
        return _dot(hi, w_ref[0:half, :]) + _dot(lo, w_ref[half:, :])

    shared = _dot(_silu(proj(sg_ref)) * proj(su_ref), sd_ref[...])
    acc_hi, acc_lo = shared[:, :half], shared[:, half:]
    gk = gk_ref[...]
    for k in range(TOP_K):
        y_hi, y_lo = _unpack_bf16_pairs(yg_ref[k])
        acc_hi = acc_hi + gk[:, k:k + 1] * y_hi.astype(F32)
        acc_lo = acc_lo + gk[:, k:k + 1] * y_lo.astype(F32)
    acc = jnp.concatenate([acc_hi, acc_lo], axis=1)
    m = mod_ref[0]
    y = x1_ref[...] + m[:, 5 * D_MODEL:6 * D_MODEL] * acc
    if not final:
        o_refs[0][...] = y
        return
    y = _rms(y, fn_ref[...])
    is_ctx = pl.program_id(0) < N_CTX_TILES

    @pl.when(is_ctx)
    def _():
        o_refs[0][...] = y

    @pl.when(jnp.logical_not(is_ctx))
    def _():
        o_refs[1][...] = y


def _combine(x1, h2p, yg, gk, mod_l, ws_gate, ws_up, ws_down, final_norm, final):
    tok = lambda shape: pl.BlockSpec(shape, lambda i: (i, 0))
    full = lambda shape: pl.BlockSpec(shape, lambda i: (0, 0))
    if final:
        out_specs, _ = _token_specs((None, None), D_MODEL)
        out_shape = [jax.ShapeDtypeStruct((T_CTX, D_MODEL), F32), jax.ShapeDtypeStruct((T_LAT, D_MODEL), F32)]
    else:
        out_specs = tok((TM, D_MODEL))
        out_shape = jax.ShapeDtypeStruct((T_ALL, D_MODEL), F32)
    return pl.pallas_call(
        functools.partial(_combine_body, final=final),
        grid=(T_ALL // TM,),
        in_specs=[tok((TM, D_MODEL)), tok((TM, D_MODEL // 2)),
                  pl.BlockSpec((TOP_K, TM, D_MODEL // 2), lambda i: (0, i, 0)),
                  tok((TM, TOP_K)),
                  pl.BlockSpec((1, 1, N_MOD * D_MODEL), lambda i: (_mod_row(i), 0, 0)),
                  full((D_MODEL, D_EXPERT)), full((D_MODEL, D_EXPERT)), full((D_EXPERT, D_MODEL)),
                  full((1, D_MODEL))],
        out_specs=out_specs,
        out_shape=out_shape,
        compiler_params=_cparams("arbitrary"),
        name="moe_combine",
    )(x1, h2p, yg, gk, mod_l, ws_gate, ws_up, ws_down, final_norm.reshape(1, D_MODEL))


def _moe(x1, h2p, chosen, gk, ik, mod_l, layer, w_gate, w_up, w_down, ws_gate, ws_up, ws_down, final_norm, final):
    dest, first_blk, n_blk, short_last = _route(chosen, ik)
    xs = _sc_dispatch(h2p, dest)
    y = _experts(first_blk[:, 0], n_blk[:, 0], short_last[:, 0], xs, layer, w_gate, w_up, w_down)
    yg = _sc_collect(y, dest.reshape(-1)).reshape(TOP_K, T_ALL, D_MODEL // 2)
    return _combine(x1, h2p, yg, gk, mod_l, ws_gate, ws_up, ws_down, final_norm, final)


def kernel(x_prompt, x_sample, cache_a_k, cache_a_v, cache_b_k, cache_b_v, state_d_fwd, state_d_bwd, c, c_ctx, w_ada, b_ada, norm_mix, norm_ffn, w_in_attn, w_out_attn, sink_a, rpb_b, w_in_rec, w_out_rec, conv_w, conv_b, filt_w1, filt_b1, filt_w2, filt_b2, filt_w3, filt_b3, filt_freq, filt_w4, d_skip, lb_fwd, lb_bwd, norm_d, w_router, router_bias, w_gate, w_up, w_down, ws_gate, ws_up, ws_down, final_norm):
    x = (x_prompt.reshape(T_CTX, D_MODEL), x_sample.reshape(T_LAT, D_MODEL))
    cvec = jnp.concatenate([c_ctx[None, :], c], axis=0)
    c_lanes = jnp.broadcast_to(cvec[:, :, None], (N_CVEC, D_MODEL, LANES))
    mod = [_ada(c_lanes, l, w_ada, b_ada).reshape(CVEC_PAD, 1, N_MOD * D_MODEL) for l in range(DEPTH)]

    new_kv = None
    new_state = None
    for l in range(DEPTH):
        j = l // 2
        final = l == DEPTH - 1
        if l % 2 == 0:
            qkv = _inproj(x, mod[l], norm_mix[l], w_in_attn[j])
            oa_ctx, ob_ctx, *new_kv = _ctx_attn(qkv, sink_a[j])
            new_kv = tuple(new_kv)
            q_rot, k_rot = _rope(qkv)
            cache = lambda t: t[:, j].reshape(DEC_BATCH, PAST_LEN, -1)
            oa_lat = _win_attn(qkv, q_rot, k_rot, cache(cache_a_k), cache(cache_a_v), sink_a[j])
            ob_lat = _na_attn(qkv, cache(cache_b_k), cache(cache_b_v), _na_rel_rows(rpb_b[j]))
            mix_a = (oa_ctx, oa_lat)
            mix_b = (ob_ctx, ob_lat)
            w_out = w_out_attn[j]
        else:
            u = _inproj(x, mod[l], norm_mix[l], w_in_rec[j])
            filt = (filt_w1[j], filt_b1[j], filt_w2[j], filt_b2[j], filt_w3[j], filt_b3[j], filt_freq[j],
                    filt_w4[j])
            y_ctx = _hyena(u, 0, BATCH, SEQ, conv_w[j], conv_b[j], d_skip[j], _hyena_filter(SEQ, filt))
            y_lat = _hyena(u, T_CTX // DEC_SEQ, DEC_BATCH, DEC_SEQ, conv_w[j], conv_b[j], d_skip[j],
                           _hyena_filter(DEC_SEQ, filt))
            zeros = jnp.zeros((BATCH, D_HEADS, D_KDIM, D_VDIM), F32)
            o_ctx, s_f, s_b = _hgrn(u, 0, BATCH, SEQ, lb_fwd, lb_bwd, norm_d[j], zeros, zeros, l)
            o_lat, _, _ = _hgrn(u, T_CTX // DEC_SEQ, DEC_BATCH, DEC_SEQ, lb_fwd, lb_bwd, norm_d[j],
                                state_d_fwd[:, j], state_d_bwd[:, j], l)
            new_state = (s_f[:, None], s_b[:, None])
            mix_a = (y_ctx, y_lat)
            mix_b = (o_ctx, o_lat)
            w_out = w_out_rec[j]
        x1, h2p, chosen, gk, ik = _outproj(mix_a, mix_b, x, mod[l], norm_ffn[l], w_out, w_router[l],
                                           router_bias[l])
        x = _moe(x1, h2p, chosen, gk, ik, mod[l], l, w_gate, w_up, w_down, ws_gate[l], ws_up[l],
                 ws_down[l], final_norm, final)

    y_prompt = x[0].reshape(BATCH, SEQ, D_MODEL)
    y_sample = x[1].reshape(DEC_BATCH, DEC_SEQ, D_MODEL)
    return (y_prompt, y_sample) + new_kv + new_state
```

```python
import functools
import math

import numpy as np
import jax
import jax.numpy as jnp
from jax import lax
from jax.experimental import pallas as pl
from jax.experimental.pallas import tpu as pltpu
from jax.experimental.pallas import tpu_sc as plsc

F32 = jnp.float32
BF16 = jnp.bfloat16
HI = lax.Precision.HIGHEST

D_MODEL = 1024
BATCH = 16
SEQ = 256
DEPTH = 2
DEC_BATCH = 2
DEC_SEQ = 1024
PAST_LEN = 512
GRID_W = 64
HEAD_DIM = 64
N_MOD = 6
RMS_EPS = 1e-6
A_HEADS = 8
A_KV_HEADS = 2
A_GROUP = A_HEADS // A_KV_HEADS
WINDOW = 128
ROPE_BASE = 10000.0
B_HEADS = 8
NA_ROWS = 8
NA_COLS = 16
C_DIM = 512
C_EMB = 33
C_FFN = 64
HYENA_MIN_DECAY = math.log(1e-2) / 1.5
HYENA_MAX_DECAY = math.log(1e-2) / 0.3
D_KDIM = 128
D_VDIM = 128
D_HEADS = 4
N_EXPERTS = 64
TOP_K = 8
D_EXPERT = 256
ROUTE_SCALE = 2.5
A_Q = A_HEADS * HEAD_DIM
A_KV = A_KV_HEADS * HEAD_DIM
B_W = B_HEADS * HEAD_DIM
ATTN_IN = A_Q + 2 * A_KV + 3 * B_W
REC_IN = 3 * C_DIM + 5 * D_HEADS * D_KDIM

T_CTX = BATCH * SEQ
T_LAT = DEC_BATCH * DEC_SEQ
T_ALL = T_CTX + T_LAT
N_CVEC = 1 + DEC_BATCH
CVEC_PAD = 8
TM = 512
MASK_NEG = -1e30
GLA_CHUNK = 64
GLA_SPAN = 256
HGRN_HEADS_PER_STEP = 4
DFT_CHUNK = 256
MOE_BLK = 512
MOE_NBLK = -(-(T_ALL * TOP_K + N_EXPERTS * (MOE_BLK - 1)) // MOE_BLK)
MOE_ROWS = MOE_NBLK * MOE_BLK
SC_CORES = 2
SC_SUBCORES = 16
SC_WORKERS = SC_CORES * SC_SUBCORES
DISP_CHUNK = 128
DISP_SPLIT = 2
COLLECT_CHUNK = 64
VMEM_LIMIT = 56 * 1024 * 1024


def _cparams(*sem):
    return pltpu.CompilerParams(dimension_semantics=sem, vmem_limit_bytes=VMEM_LIMIT)


def _mod_row(i):
    return jnp.where(i < T_CTX // TM, 0, 1 + (i - T_CTX // TM) // (DEC_SEQ // TM))


def _dot(a, b):
    return jnp.dot(a.astype(BF16), b.astype(BF16), preferred_element_type=F32)


def _dot_nt(a, b):
    return lax.dot_general(a.astype(BF16), b.astype(BF16), (((1,), (1,)), ((), ())),
                           preferred_element_type=F32)


def _dot_tn(a, b):
    return lax.dot_general(a.astype(BF16), b.astype(BF16), (((0,), (0,)), ((), ())),
                           preferred_element_type=F32)


def _dot_hi(a, b):
    return jnp.dot(a, b, precision=HI, preferred_element_type=F32)


def _split_bf16(x):
    hi = x.astype(BF16)
    return hi, (x - hi.astype(F32)).astype(BF16)


def _dot_split(a, b):
    a_hi, a_lo = _split_bf16(a)
    b_hi, b_lo = _split_bf16(b)
    dot = lambda x, y: jnp.dot(x, y, preferred_element_type=F32)
    return dot(a_hi, b_hi) + dot(a_hi, b_lo) + dot(a_lo, b_hi)


def _silu(x):
    return x * jax.nn.sigmoid(x)


def _rms(x, g):
    return x * lax.rsqrt(jnp.mean(x * x, axis=-1, keepdims=True) + RMS_EPS) * g


ADA_TN = 1536
ADA_UNROLL = 4


def _ada_body(cb_ref, w_ref, b_ref, o_ref):
    tn = o_ref.shape[-1]
    n_slab = tn // LANES

    def step(k8, accs):
        r0 = pl.multiple_of(k8 * 8, 8)
        sk = [_silu(cb_ref[j, pl.ds(r0, 8), :]) for j in range(N_CVEC)]
        out = []
        for s in range(n_slab):
            wk = w_ref[0, pl.ds(r0, 8), s * LANES:(s + 1) * LANES]
            out.extend(accs[s * N_CVEC + j] + wk * sk[j] for j in range(N_CVEC))
        return tuple(out)

    accs = lax.fori_loop(0, D_MODEL // 8, step,
                         tuple(jnp.zeros((8, LANES), F32) for _ in range(n_slab * N_CVEC)), unroll=ADA_UNROLL)
    o_ref[0] = jnp.zeros((CVEC_PAD, tn), F32)
    for s in range(n_slab):
        for j in range(N_CVEC):
            o_ref[0, j:j + 1, s * LANES:(s + 1) * LANES] = (
                jnp.sum(accs[s * N_CVEC + j], axis=0, keepdims=True) + b_ref[0, :, s * LANES:(s + 1) * LANES])


def _ada(c_lanes, layer, w_ada, b_ada):
    n_out = N_MOD * D_MODEL
    return pl.pallas_call(
        _ada_body,
        grid=(n_out // ADA_TN,),
        in_specs=[pl.BlockSpec((N_CVEC, D_MODEL, LANES), lambda n: (0, 0, 0)),
                  pl.BlockSpec((1, D_MODEL, ADA_TN), lambda n: (layer, 0, n)),
                  pl.BlockSpec((1, 1, ADA_TN), lambda n: (layer, 0, n))],
        out_specs=pl.BlockSpec((1, CVEC_PAD, ADA_TN), lambda n: (0, 0, n)),
        out_shape=jax.ShapeDtypeStruct((1, CVEC_PAD, n_out), F32),
        compiler_params=_cparams("parallel"),
        name="ada",
    )(c_lanes, w_ada, b_ada.reshape(DEPTH, 1, n_out))


N_CTX_TILES = T_CTX // TM


def _token_specs(x, width):
    if not isinstance(x, tuple):
        return [pl.BlockSpec((TM, width), lambda i: (i, 0))], (x,)
    return ([pl.BlockSpec((TM, width), lambda i: (jnp.minimum(i, N_CTX_TILES - 1), 0)),
             pl.BlockSpec((TM, width), lambda i: (jnp.maximum(i - N_CTX_TILES, 0), 0))], x)


def _token_tile(refs):
    if len(refs) == 1:
        return refs[0][...]
    return jnp.where(pl.program_id(0) < N_CTX_TILES, refs[0][...], refs[1][...])


def _inproj_body(*refs, n_x):
    x_refs, (mod_ref, g_ref, w_ref, o_ref, w_bf) = refs[:n_x], refs[n_x:]

    @pl.when(pl.program_id(0) == 0)
    def _():
        w_bf[...] = w_ref[...].astype(BF16)

    m = mod_ref[0]
    h = _rms(_token_tile(x_refs), g_ref[...]) * (1.0 + m[:, D_MODEL:2 * D_MODEL]) + m[:, 0:D_MODEL]
    o_ref[...] = _dot(h, w_bf[...])


def _inproj(x, mod_l, gain, w):
    n = w.shape[1]
    x_specs, x_args = _token_specs(x, D_MODEL)
    return pl.pallas_call(
        functools.partial(_inproj_body, n_x=len(x_args)),
        grid=(T_ALL // TM,),
        in_specs=x_specs + [pl.BlockSpec((1, 1, N_MOD * D_MODEL), lambda i: (_mod_row(i), 0, 0)),
                            pl.BlockSpec((1, D_MODEL), lambda i: (0, 0)),
                            pl.BlockSpec((D_MODEL, n), lambda i: (0, 0), pipeline_mode=pl.Buffered(1))],
        out_specs=pl.BlockSpec((TM, n), lambda i: (i, 0)),
        out_shape=jax.ShapeDtypeStruct((T_ALL, n), F32),
        scratch_shapes=[pltpu.VMEM((D_MODEL, n), BF16)],
        compiler_params=_cparams("arbitrary"),
        name="inproj",
    )(*x_args, mod_l, gain.reshape(1, D_MODEL), w)


def _head_cols(h):
    return slice(h * HEAD_DIM, (h + 1) * HEAD_DIM)


def _group_rows(ref, rows, first_col, sink_ref, hk):
    n = rows.stop - rows.start
    q = jnp.concatenate([ref[rows, first_col + g * HEAD_DIM:first_col + (g + 1) * HEAD_DIM]
                         for g in range(A_GROUP)], axis=0)
    sink = jnp.concatenate([jnp.broadcast_to(sink_ref[:, hk * A_GROUP + g:hk * A_GROUP + g + 1], (n, 1))
                            for g in range(A_GROUP)], axis=0)
    return q, sink


def _ctx_attn_body(qkv_ref, sink_ref, oa_ref, ob_ref, ak_ref, av_ref, bk_ref, bv_ref):
    scale = HEAD_DIM ** -0.5
    lane = lax.broadcasted_iota(jnp.int32, (SEQ, LANES), 1)
    in_half = [lane < HEAD_DIM, lane >= HEAD_DIM]

    def attend(q, k, v, sink):
        s = _dot_nt(q, k) * scale
        m = jnp.max(s, axis=-1, keepdims=True)
        if sink is not None:
            m = jnp.maximum(m, sink)
        p = jnp.exp(s - m)
        den = jnp.sum(p, axis=-1, keepdims=True)
        if sink is not None:
            den = den + jnp.exp(sink - m)
        return _dot(p, v) / den

    def tile(first_col, t):
        return qkv_ref[:, first_col + t * LANES:first_col + (t + 1) * LANES]

    base = A_Q + 2 * A_KV
    for hk in range(A_KV_HEADS):
        dst = pl.ds(hk, SEQ, stride=A_KV_HEADS)
        ak_ref[0, dst, :] = qkv_ref[:, A_Q + hk * HEAD_DIM:A_Q + (hk + 1) * HEAD_DIM]
        av_ref[0, dst, :] = qkv_ref[:, A_Q + A_KV + hk * HEAD_DIM:A_Q + A_KV + (hk + 1) * HEAD_DIM]
    for h in range(B_HEADS):
        dst = pl.ds(h, SEQ, stride=B_HEADS)
        bk_ref[0, dst, :] = qkv_ref[:, base + B_W + h * HEAD_DIM:base + B_W + (h + 1) * HEAD_DIM]
        bv_ref[0, dst, :] = qkv_ref[:, base + 2 * B_W + h * HEAD_DIM:base + 2 * B_W + (h + 1) * HEAD_DIM]

    k_t, v_t = tile(A_Q, 0), tile(A_Q + A_KV, 0)
    k_sw, v_sw = pltpu.roll(k_t, HEAD_DIM, axis=1), pltpu.roll(v_t, HEAD_DIM, axis=1)
    tiles_per_kv = A_GROUP // HEADS_PER_TILE
    for hk in range(A_KV_HEADS):
        q_tiles = [tile(0, hk * tiles_per_kv + j) for j in range(tiles_per_kv)]
        halves = []
        for p in range(HEADS_PER_TILE):
            q = jnp.concatenate([jnp.where(in_half[p], qt, 0.0) for qt in q_tiles], axis=0)
            heads = [(hk * tiles_per_kv + j) * HEADS_PER_TILE + p for j in range(tiles_per_kv)]
            sink = jnp.concatenate([jnp.broadcast_to(sink_ref[:, h:h + 1], (SEQ, 1)) for h in heads], axis=0)
            halves.append(attend(q, k_t if p == hk else k_sw, v_t if p == hk else v_sw, sink))
        first_half = lax.broadcasted_iota(jnp.int32, halves[0].shape, 1) < HEAD_DIM
        o = jnp.where(first_half, halves[0], halves[1])
        for j in range(tiles_per_kv):
            t = hk * tiles_per_kv + j
            oa_ref[:, t * LANES:(t + 1) * LANES] = o[j * SEQ:(j + 1) * SEQ]

    for t in range(B_HEADS // HEADS_PER_TILE):
        q_t, k_b, v_b = tile(base, t), tile(base + B_W, t), tile(base + 2 * B_W, t)
        halves = [attend(jnp.where(in_half[p], q_t, 0.0), k_b, v_b, None) for p in range(HEADS_PER_TILE)]
        ob_ref[:, t * LANES:(t + 1) * LANES] = jnp.where(in_half[0], halves[0], halves[1])


def _ctx_attn(qkv, sink):
    kv_spec = lambda heads: pl.BlockSpec((1, SEQ * heads, HEAD_DIM), lambda b: (b, 0, 0))
    kv_sd = lambda heads: jax.ShapeDtypeStruct((BATCH, SEQ * heads, HEAD_DIM), F32)
    outs = pl.pallas_call(
        _ctx_attn_body,
        grid=(BATCH,),
        in_specs=[pl.BlockSpec((SEQ, ATTN_IN), lambda b: (b, 0)),
                  pl.BlockSpec((1, A_HEADS), lambda b: (0, 0))],
        out_specs=[pl.BlockSpec((SEQ, A_Q), lambda b: (b, 0)), pl.BlockSpec((SEQ, B_W), lambda b: (b, 0)),
                   kv_spec(A_KV_HEADS), kv_spec(A_KV_HEADS), kv_spec(B_HEADS), kv_spec(B_HEADS)],
        out_shape=[jax.ShapeDtypeStruct((T_CTX, A_Q), F32), jax.ShapeDtypeStruct((T_CTX, B_W), F32),
                   kv_sd(A_KV_HEADS), kv_sd(A_KV_HEADS), kv_sd(B_HEADS), kv_sd(B_HEADS)],
        compiler_params=_cparams("parallel"),
        name="ctx_attn",
    )(qkv, sink.reshape(1, A_HEADS))
    caches = [t.reshape(BATCH, 1, SEQ, -1, HEAD_DIM) for t in outs[2:]]
    return outs[0], outs[1], *caches


@functools.lru_cache(maxsize=None)
def _rope_tables(width):
    half = HEAD_DIM // 2
    t = np.arange(DEC_SEQ)
    inv = ROPE_BASE ** (-np.arange(0, half, 2, dtype=np.float64) / half)
    ang_r = (t // GRID_W)[:, None] * inv[None, :]
    ang_c = (t % GRID_W)[:, None] * inv[None, :]
    cos = np.concatenate([np.cos(ang_r)] * 2 + [np.cos(ang_c)] * 2, axis=-1)
    sin = np.concatenate([-np.sin(ang_r), np.sin(ang_r), -np.sin(ang_c), np.sin(ang_c)], axis=-1)
    reps = width // HEAD_DIM
    return (np.tile(cos, (1, reps)).astype(np.float32), np.tile(sin, (1, reps)).astype(np.float32))


def _rope_body(q_ref, k_ref, cq_ref, sq_ref, ck_ref, sk_ref, qo_ref, ko_ref):
    quarter = HEAD_DIM // 4

    def rot(x, cos, sin):
        w = x.shape[-1]
        lane = lax.broadcasted_iota(jnp.int32, x.shape, 1)
        fwd = pltpu.roll(x, w - quarter, axis=1)
        bwd = pltpu.roll(x, quarter, axis=1)
        partner = jnp.where((lane & (2 * quarter - 1)) < quarter, fwd, bwd)
        return x * cos + partner * sin

    qo_ref[...] = rot(q_ref[...], cq_ref[...], sq_ref[...])
    ko_ref[...] = rot(k_ref[...], ck_ref[...], sk_ref[...])


def _rope(qkv):
    cq, sq = _rope_tables(A_Q)
    ck, sk = _rope_tables(A_KV)
    tab = lambda w: pl.BlockSpec((DEC_SEQ, w), lambda b: (0, 0))
    row0 = T_CTX // DEC_SEQ
    return pl.pallas_call(
        _rope_body,
        grid=(DEC_BATCH,),
        in_specs=[pl.BlockSpec((DEC_SEQ, A_Q), lambda b: (row0 + b, 0)),
                  pl.BlockSpec((DEC_SEQ, A_KV), lambda b: (row0 + b, A_Q // A_KV)),
                  tab(A_Q), tab(A_Q), tab(A_KV), tab(A_KV)],
        out_specs=[pl.BlockSpec((DEC_SEQ, A_Q), lambda b: (b, 0)),
                   pl.BlockSpec((DEC_SEQ, A_KV), lambda b: (b, 0))],
        out_shape=[jax.ShapeDtypeStruct((T_LAT, A_Q), F32), jax.ShapeDtypeStruct((T_LAT, A_KV), F32)],
        compiler_params=_cparams("parallel"),
        name="rope",
    )(qkv, qkv, jnp.asarray(cq), jnp.asarray(sq), jnp.asarray(ck), jnp.asarray(sk))


WIN_QB = 256


def _pick_head(x, h, n_heads):
    out = x[:, _head_cols(0)]
    for i in range(1, n_heads):
        out = jnp.where(h == i, x[:, _head_cols(i)], out)
    return out


def _win_attn_body(qraw_ref, qrot_ref, krot_ref, v_ref, kc_ref, vc_ref, sink_ref, o_ref):
    scale = HEAD_DIM ** -0.5
    hk = pl.program_id(1)
    tiles = A_GROUP // HEADS_PER_TILE

    def kv_in_half(x):
        swapped = pltpu.roll(x, HEAD_DIM, axis=1)
        return [jnp.where(hk == p, x, swapped) for p in range(HEADS_PER_TILE)]

    k, v, kc, vc = kv_in_half(krot_ref[...]), kv_in_half(v_ref[...]), kv_in_half(kc_ref[0]), kv_in_half(vc_ref[0])
    head_lane = lax.broadcasted_iota(jnp.int32, (1, A_HEADS), 1)

    def sink_rows(p):
        heads = [hk * A_GROUP + j * HEADS_PER_TILE + p for j in range(tiles)]
        vals = [jnp.sum(jnp.where(head_lane == h, sink_ref[...], 0.0), axis=-1, keepdims=True) for h in heads]
        return jnp.concatenate([jnp.broadcast_to(s, (WIN_QB, 1)) for s in vals], axis=0)

    sinks = [sink_rows(p) for p in range(HEADS_PER_TILE)]
    lane = lax.broadcasted_iota(jnp.int32, (tiles * WIN_QB, LANES), 1)
    in_half = [lane < HEAD_DIM, lane >= HEAD_DIM]
    for qb in range(DEC_SEQ // WIN_QB):
        q0 = qb * WIN_QB
        rows = slice(q0, q0 + WIN_QB)
        lo = max(0, q0 - WINDOW)
        hi = min(DEC_SEQ, q0 + WIN_QB + WINDOW)
        q_rot = jnp.concatenate([qrot_ref[rows, j * LANES:(j + 1) * LANES] for j in range(tiles)], axis=0)
        q_raw = jnp.concatenate([qraw_ref[rows, j * LANES:(j + 1) * LANES] for j in range(tiles)], axis=0)
        halves = []
        for p in range(HEADS_PER_TILE):
            s_loc = _dot_nt(jnp.where(in_half[p], q_rot, 0.0), k[p][lo:hi]) * scale
            qpos = q0 + (lax.broadcasted_iota(jnp.int32, s_loc.shape, 0) & (WIN_QB - 1))
            kpos = lo + lax.broadcasted_iota(jnp.int32, s_loc.shape, 1)
            s_loc = jnp.where(jnp.abs(kpos - qpos) <= WINDOW, s_loc, MASK_NEG)
            s_ctx = _dot_nt(jnp.where(in_half[p], q_raw, 0.0), kc[p]) * scale
            m = jnp.maximum(jnp.maximum(jnp.max(s_loc, axis=-1, keepdims=True),
                                        jnp.max(s_ctx, axis=-1, keepdims=True)), sinks[p])
            p_loc = jnp.exp(s_loc - m)
            p_ctx = jnp.exp(s_ctx - m)
            den = (jnp.sum(p_loc, axis=-1, keepdims=True) + jnp.sum(p_ctx, axis=-1, keepdims=True)
                   + jnp.exp(sinks[p] - m))
            halves.append((_dot(p_ctx, vc[p]) + _dot(p_loc, v[p][lo:hi])) / den)
        o = jnp.where(in_half[0], halves[0], halves[1])
        for j in range(tiles):
            o_ref[rows, j * LANES:(j + 1) * LANES] = o[j * WIN_QB:(j + 1) * WIN_QB]


def _win_attn(qkv, q_rot, k_rot, kc, vc, sink):
    row0 = T_CTX // DEC_SEQ
    gw = A_GROUP * HEAD_DIM
    return pl.pallas_call(
        _win_attn_body,
        grid=(DEC_BATCH, A_KV_HEADS),
        in_specs=[pl.BlockSpec((DEC_SEQ, gw), lambda b, h: (row0 + b, h)),
                  pl.BlockSpec((DEC_SEQ, gw), lambda b, h: (b, h)),
                  pl.BlockSpec((DEC_SEQ, A_KV), lambda b, h: (b, 0)),
                  pl.BlockSpec((DEC_SEQ, A_KV), lambda b, h: (row0 + b, (A_Q + A_KV) // A_KV)),
                  pl.BlockSpec((1, PAST_LEN, A_KV), lambda b, h: (b, 0, 0)),
                  pl.BlockSpec((1, PAST_LEN, A_KV), lambda b, h: (b, 0, 0)),
                  pl.BlockSpec((1, A_HEADS), lambda b, h: (0, 0))],
        out_specs=pl.BlockSpec((DEC_SEQ, gw), lambda b, h: (b, h)),
        out_shape=jax.ShapeDtypeStruct((T_LAT, A_Q), F32),
        compiler_params=_cparams("parallel", "parallel"),
        name="win_attn",
    )(qkv, q_rot, k_rot, qkv, kc, vc, sink.reshape(1, A_HEADS))


GRID_ROWS = DEC_SEQ // GRID_W
NA_BAND = min(NA_ROWS, GRID_ROWS)


NA_REL_ROWS = 2 * NA_ROWS - 1
NA_REL_COLS = 2 * NA_COLS - 1
LANES = 128
HEADS_PER_TILE = LANES // HEAD_DIM


def _na_rel_rows(rpb):
    pad = jnp.zeros((B_HEADS, NA_REL_ROWS, GRID_W - NA_REL_COLS), F32)
    one = jnp.concatenate([rpb, pad], axis=-1)
    nxt = jnp.concatenate([one[:, 1:], jnp.zeros((B_HEADS, 1, GRID_W), F32)], axis=1)
    both = jnp.concatenate([one, nxt], axis=-1)
    return jnp.concatenate([both, jnp.zeros((B_HEADS, 16 - NA_REL_ROWS, LANES), F32)], axis=1)


NA_HEADS_PER_STEP = LANES // HEAD_DIM


def _na_row_groups():
    groups = []
    for r in range(GRID_ROWS):
        rs = min(max(r - NA_ROWS // 2, 0), GRID_ROWS - NA_BAND)
        if groups and groups[-1][2] == rs:
            groups[-1][1] += 1
        else:
            groups.append([r, 1, rs])
    return groups


def _na_attn_body(q_ref, k_ref, v_ref, kc_ref, vc_ref, rel_ref, o_ref):
    scale = HEAD_DIM ** -0.5
    cq = lax.broadcasted_iota(jnp.int32, (GRID_W, LANES), 0)
    kcol = lax.broadcasted_iota(jnp.int32, (GRID_W, LANES), 1) & (GRID_W - 1)
    cs = jnp.clip(cq - NA_COLS // 2, 0, GRID_W - NA_COLS)
    col_ok = (kcol >= cs) & (kcol < cs + NA_COLS)
    kc = kc_ref[0]
    vc = vc_ref[0]
    tiles = {}

    def pair_tile(hh, a):
        if (hh, a) not in tiles:
            x = jnp.broadcast_to(rel_ref[hh, a:a + 1, :], (GRID_W, LANES))
            t = pltpu.roll(x, LANES - (NA_COLS - 1), axis=1, stride=1, stride_axis=0)
            tiles[hh, a] = jnp.where(col_ok, t, MASK_NEG)
        return tiles[hh, a]

    for r0, n_r, rs in _na_row_groups():
        rows = slice(r0 * GRID_W, (r0 + n_r) * GRID_W)
        band = slice(rs * GRID_W, (rs + NA_BAND) * GRID_W)
        q_t, k_t, v_t = q_ref[rows, :], k_ref[band, :], v_ref[band, :]
        head_of_lane = lax.broadcasted_iota(jnp.int32, q_t.shape, 1) >> (HEAD_DIM.bit_length() - 1)
        o = jnp.zeros(q_t.shape, F32)
        for hh in range(NA_HEADS_PER_STEP):
            bias = jnp.concatenate(
                [jnp.concatenate([pair_tile(hh, rs - r + NA_ROWS - 1 + 2 * i) for i in range(NA_BAND // 2)], axis=1)
                 for r in range(r0, r0 + n_r)], axis=0)
            q = jnp.where(head_of_lane == hh, q_t, 0.0)
            s_loc = _dot_nt(q, k_t) * scale + bias
            s_ctx = _dot_nt(q, kc) * scale
            m = jnp.maximum(jnp.max(s_loc, axis=-1, keepdims=True), jnp.max(s_ctx, axis=-1, keepdims=True))
            p_loc = jnp.exp(s_loc - m)
            p_ctx = jnp.exp(s_ctx - m)
            den = jnp.sum(p_loc, axis=-1, keepdims=True) + jnp.sum(p_ctx, axis=-1, keepdims=True)
            o = jnp.where(head_of_lane == hh, (_dot(p_ctx, vc) + _dot(p_loc, v_t)) / den, o)
        o_ref[rows, :] = o


def _na_attn(qkv, kc, vc, rel):
    row0 = T_CTX // DEC_SEQ
    col0 = (A_Q + 2 * A_KV) // LANES
    n_blk = B_W // LANES
    col = lambda j: pl.BlockSpec((DEC_SEQ, LANES), lambda b, p: (row0 + b, col0 + j * n_blk + p))
    cache = pl.BlockSpec((1, PAST_LEN, LANES), lambda b, p: (b, 0, p))
    return pl.pallas_call(
        _na_attn_body,
        grid=(DEC_BATCH, n_blk),
        in_specs=[col(0), col(1), col(2), cache, cache,
                  pl.BlockSpec((NA_HEADS_PER_STEP, 16, LANES), lambda b, p: (p, 0, 0))],
        out_specs=pl.BlockSpec((DEC_SEQ, LANES), lambda b, p: (b, p)),
        out_shape=jax.ShapeDtypeStruct((T_LAT, B_W), F32),
        compiler_params=_cparams("parallel", "parallel"),
        name="na_attn",
    )(qkv, qkv, qkv, kc, vc, rel)


@functools.lru_cache(maxsize=None)
def _dft_mats(L):
    n = 2 * L
    fc = min(L, DFT_CHUNK)
    f = np.arange(L)[:, None]
    t = np.arange(L)[None, :]
    ang = 2.0 * np.pi * ((f * t) % n) / n
    m1 = np.cos(ang)
    m2 = np.sin(ang)
    m2[0, :] = np.where(np.arange(L) % 2 == 0, 1.0, -1.0)
    wgt = np.full((L, 1), 2.0)
    wgt[0, 0] = 1.0
    nch = L // fc
    fwd = np.concatenate([m1.reshape(nch, fc, L), m2.reshape(nch, fc, L)], axis=1)
    inv = np.concatenate([(m1 * wgt / n).reshape(nch, fc, L), (m2 * wgt / n).reshape(nch, fc, L)], axis=1)
    inv = np.transpose(inv, (0, 2, 1))
    return fwd.astype(np.float32), inv.astype(np.float32)


@functools.lru_cache(maxsize=None)
def _filter_consts(L):
    t = np.linspace(0.0, 1.0, L)[:, None]
    bands = (C_EMB - 1) // 2
    ang = (2.0 * math.pi / L) * np.arange(L)[:, None] * np.linspace(1e-4, bands - 1, bands)[None, :]
    z = np.concatenate([t, np.cos(ang), -np.sin(ang)], axis=-1)
    zpad = np.zeros((L, 128))
    zpad[:, :C_EMB] = z
    deltas = np.abs(np.linspace(HYENA_MIN_DECAY, HYENA_MAX_DECAY, C_DIM))
    window = np.exp(-t * deltas[None, :])
    return zpad.astype(np.float32), window.astype(np.float32)


def _filter_body(z_ref, w1_ref, b1_ref, w2_ref, b2_ref, w3_ref, b3_ref, fr_ref, w4_ref, win_ref, fm_ref,
                 hr_ref, g_ref, hq_ref, hs_scr, hd_scr):
    c = pl.program_id(0)
    fc = hr_ref.shape[0]

    @pl.when(c == 0)
    def _():
        fr = fr_ref[...]
        hh = jnp.sin(fr * (_dot_hi(z_ref[...], w1_ref[...]) + b1_ref[...]))
        hh = jnp.sin(fr * (_dot_hi(hh, w2_ref[...]) + b2_ref[...]))
        hh = jnp.sin(fr * (_dot_hi(hh, w3_ref[...]) + b3_ref[...]))
        hh = _dot_hi(hh, w4_ref[...])
        hf = hh[:, :C_DIM] * win_ref[...]
        hb = hh[:, C_DIM:] * win_ref[...]
        hs_scr[...] = hf + hb
        hd_scr[...] = hf - hb

    fm = fm_ref[0]
    hr = _dot_split(fm[:fc], hs_scr[...])
    first = (lax.broadcasted_iota(jnp.int32, (fc, C_DIM), 0) == 0) & (c == 0)
    hr_ref[...] = hr
    g_ref[...] = jnp.where(first, 0.0, _dot_split(fm[fc:], hd_scr[...]))
    hs = hs_scr[...]
    sign = jnp.where((lax.broadcasted_iota(jnp.int32, hs.shape, 0) & 1) == 0, 1.0, -1.0)
    hq_ref[...] = jnp.where(first, jnp.sum(hs * sign, axis=0, keepdims=True), hr)


def _hyena_filter(L, filt):
    w1, b1, w2, b2, w3, b3, freq, w4 = filt
    zpad, window = _filter_consts(L)
    fwd, _ = _dft_mats(L)
    nch, fc2, _ = fwd.shape
    fc = fc2 // 2
    w1p = jnp.pad(w1, ((0, 128 - C_EMB), (0, 0)))
    full = lambda shape: pl.BlockSpec(shape, lambda c: tuple(0 for _ in shape))
    out_spec = pl.BlockSpec((fc, C_DIM), lambda c: (c, 0))
    out_sd = jax.ShapeDtypeStruct((L, C_DIM), F32)
    return pl.pallas_call(
        _filter_body,
        grid=(nch,),
        in_specs=[full((L, 128)), full((128, C_FFN)), full((1, C_FFN)), full((C_FFN, C_FFN)), full((1, C_FFN)),
                  full((C_FFN, C_FFN)), full((1, C_FFN)), full((1, C_FFN)), full((C_FFN, 2 * C_DIM)),
                  full((L, C_DIM)), pl.BlockSpec((1, fc2, L), lambda c: (c, 0, 0))],
        out_specs=[out_spec, out_spec, out_spec],
        out_shape=[out_sd, out_sd, out_sd],
        scratch_shapes=[pltpu.VMEM((L, C_DIM), F32), pltpu.VMEM((L, C_DIM), F32)],
        compiler_params=_cparams("arbitrary"),
        name="hyena_filter",
    )(jnp.asarray(zpad), w1p, b1.reshape(1, C_FFN), w2, b2.reshape(1, C_FFN), w3, b3.reshape(1, C_FFN),
      freq.reshape(1, C_FFN), w4, jnp.asarray(window), jnp.asarray(fwd))


def _hyena_body(u_ref, cw_ref, cb_ref, d_ref, fm_ref, fi_ref, hr_ref, g_ref, hq_ref, y_ref,
                x0_scr, z_scr, acc_scr):
    c = pl.program_id(1)
    L = y_ref.shape[0]
    fc = hr_ref.shape[0]

    @pl.when(c == 0)
    def _():
        row = lax.broadcasted_iota(jnp.int32, (L, C_DIM), 0)

        def short_conv(sec):
            cols = slice(sec * C_DIM, (sec + 1) * C_DIM)
            u = u_ref[:, cols]
            prev = jnp.where(row == 0, 0.0, pltpu.roll(u, 1, axis=0))
            nxt = jnp.where(row == L - 1, 0.0, pltpu.roll(u, L - 1, axis=0))
            return (prev * cw_ref[0:1, cols] + u * cw_ref[1:2, cols] + nxt * cw_ref[2:3, cols]
                    + cb_ref[:, cols])

        x0_scr[...] = short_conv(0)
        z_scr[...] = short_conv(1) * short_conv(2)
        acc_scr[...] = jnp.zeros((L, C_DIM), F32)

    ab = _dot_split(fm_ref[0], z_scr[...])
    a, b = ab[:fc], ab[fc:]
    hr, g, hq = hr_ref[...], g_ref[...], hq_ref[...]
    pq = jnp.concatenate([a * hr - b * g, a * g + b * hq], axis=0)
    acc_scr[...] += _dot_split(fi_ref[0], pq)

    @pl.when(c == pl.num_programs(1) - 1)
    def _():
        y_ref[...] = x0_scr[...] * (acc_scr[...] + z_scr[...] * d_ref[...])


def _hyena(u, row_blk0, n_seq, L, conv_w, conv_b, d_skip, spec):
    hr, g, hq = spec
    fwd, inv = _dft_mats(L)
    nch, fc2, _ = fwd.shape
    fc = fc2 // 2
    u_w = 3 * C_DIM
    return pl.pallas_call(
        _hyena_body,
        grid=(n_seq, nch),
        in_specs=[pl.BlockSpec((L, u_w), lambda b, c: (row_blk0 + b, 0)),
                  pl.BlockSpec((3, u_w), lambda b, c: (0, 0)),
                  pl.BlockSpec((1, u_w), lambda b, c: (0, 0)),
                  pl.BlockSpec((1, C_DIM), lambda b, c: (0, 0)),
                  pl.BlockSpec((1, fc2, L), lambda b, c: (c, 0, 0)),
                  pl.BlockSpec((1, L, fc2), lambda b, c: (c, 0, 0)),
                  pl.BlockSpec((fc, C_DIM), lambda b, c: (c, 0)),
                  pl.BlockSpec((fc, C_DIM), lambda b, c: (c, 0)),
                  pl.BlockSpec((fc, C_DIM), lambda b, c: (c, 0))],
        out_specs=pl.BlockSpec((L, C_DIM), lambda b, c: (b, 0)),
        out_shape=jax.ShapeDtypeStruct((n_seq * L, C_DIM), F32),
        scratch_shapes=[pltpu.VMEM((L, C_DIM), F32)] * 3,
        compiler_params=_cparams("parallel", "arbitrary"),
        name="hyena",
    )(u, conv_w, conv_b.reshape(1, u_w), d_skip.reshape(1, C_DIM), jnp.asarray(fwd), jnp.asarray(inv), hr, g, hq)


def _hgrn_body(q_ref, ff_ref, fb_ref, i_ref, g_ref, lbf_ref, lbb_ref, nd_ref, s0f_ref, s0b_ref,
               o_ref, sf_ref, sb_ref, *, layer):
    L = o_ref.shape[0]
    C = GLA_CHUNK
    S = min(L, GLA_SPAN)
    nc = S // C
    n_span = L // S
    mid = C // 2
    def lower_bound(gm):
        e = jnp.exp(gm - jnp.max(gm, axis=0, keepdims=True))
        p = e / jnp.sum(e, axis=0, keepdims=True)
        return jnp.sum(p[0:layer + 1], axis=0, keepdims=True) - p[0:1]

    def gates(fx, lb):
        f = lb + (1.0 - lb) * jax.nn.sigmoid(fx)
        return 1.0 - f, jnp.log(f)


    chunk_shift = C.bit_length() - 1
    block_shift = D_KDIM.bit_length() - 1
    ti = lax.broadcasted_iota(jnp.int32, (S, S), 0)
    si = lax.broadcasted_iota(jnp.int32, (S, S), 1)
    same_chunk = (ti >> chunk_shift) == (si >> chunk_shift)
    causal = same_chunk & (si <= ti)
    anti = same_chunk & (si >= ti)
    row_chunk = lax.broadcasted_iota(jnp.int32, (S, nc * D_KDIM), 0) >> chunk_shift
    col_chunk = lax.broadcasted_iota(jnp.int32, (S, nc * D_KDIM), 1) >> block_shift
    own_block = row_chunk == col_chunk

    def spread(x):
        return jnp.where(own_block, jnp.concatenate([x] * nc, axis=1), 0.0)

    def chunk_cumsum(mask, lg):
        tri = mask.astype(BF16)
        hi = lg.astype(BF16)
        r1 = lg - hi.astype(F32)
        mid_t = r1.astype(BF16)
        lo = (r1 - mid_t.astype(F32)).astype(BF16)
        dot = lambda t: jnp.dot(tri, t, preferred_element_type=F32)
        return dot(hi) + dot(mid_t) + dot(lo)

    def per_chunk_rows(b, pos):
        return jnp.concatenate([jnp.broadcast_to(b[n * C + pos:n * C + pos + 1], (C, D_KDIM)) for n in range(nc)],
                               axis=0)

    def one_head(q, v, kf, lgf, kb, lgb, st_f, st_b):
        local = []
        for u in range(n_span):
            rows = slice(u * S, (u + 1) * S)
            qs, vs, kfs, kbs = q[rows], v[rows], kf[rows], kb[rows]
            lgs = jnp.concatenate([lgf[rows], lgb[rows]], axis=1)
            pre = chunk_cumsum(causal, lgs)
            b_f = pre[:, :D_KDIM]
            pre_b = pre[:, D_KDIM:]
            b_b = per_chunk_rows(pre_b, C - 1) - pre_b + lgb[rows]
            ref_f, ref_b = per_chunk_rows(b_f, mid), per_chunk_rows(b_b, mid)
            sc = (jnp.where(causal, _dot_nt(qs * jnp.exp(b_f - ref_f), kfs * jnp.exp(ref_f - b_f)), 0.0)
                  + jnp.where(anti, _dot_nt(qs * jnp.exp(b_b - ref_b), kbs * jnp.exp(ref_b - b_b)), 0.0))
            k_out = jnp.concatenate([kfs * jnp.exp(per_chunk_rows(b_f, C - 1) - b_f),
                                     kbs * jnp.exp(per_chunk_rows(b_b, 0) - b_b)], axis=1)
            kv_t = _dot_tn(spread(vs), k_out)
            local.append((_dot(sc, vs), kv_t, b_f, b_b, qs))

        states_f = [[None] * nc for _ in range(n_span)]
        for u in range(n_span):
            _, kv_t, b_f, _, _ = local[u]
            for n in range(nc):
                states_f[u][n] = st_f
                st_f = st_f * jnp.exp(b_f[n * C + C - 1:n * C + C]) + kv_t[n * D_VDIM:(n + 1) * D_VDIM, :D_KDIM]
        states_b = [[None] * nc for _ in range(n_span)]
        for u in reversed(range(n_span)):
            _, kv_t, _, b_b, _ = local[u]
            for n in reversed(range(nc)):
                states_b[u][n] = st_b
                st_b = st_b * jnp.exp(b_b[n * C:n * C + 1]) + kv_t[n * D_VDIM:(n + 1) * D_VDIM, D_KDIM:]

        outs = []
        for u in range(n_span):
            intra, _, b_f, b_b, qs = local[u]
            q_in = jnp.concatenate([spread(qs * jnp.exp(b_f)), spread(qs * jnp.exp(b_b))], axis=1)
            outs.append(intra + _dot_nt(q_in, jnp.concatenate(states_f[u] + states_b[u], axis=1)))
        return (jnp.concatenate(outs, axis=0) if n_span > 1 else outs[0]), st_f, st_b

    for hh in range(o_ref.shape[1] // D_VDIM):
        cols = slice(hh * D_KDIM, (hh + 1) * D_KDIM)
        kf, lgf = gates(ff_ref[:, cols], lower_bound(lbf_ref[:, cols]))
        kb, lgb = gates(fb_ref[:, cols], lower_bound(lbb_ref[:, cols]))
        o, st_f, st_b = one_head(_silu(q_ref[:, cols]), i_ref[:, cols], kf, lgf, kb, lgb,
                                 jnp.transpose(s0f_ref[0, hh]), jnp.transpose(s0b_ref[0, hh]))
        sf_ref[0, hh] = jnp.transpose(st_f)
        sb_ref[0, hh] = jnp.transpose(st_b)
        o_ref[:, cols] = _rms(o, nd_ref[...]) * _silu(g_ref[:, cols])


def _hgrn(u, row_blk0, n_seq, L, lb_fwd, lb_bwd, norm_d, s0f, s0b, layer):
    hps = HGRN_HEADS_PER_STEP
    width = hps * D_KDIM
    col0 = 3 * C_DIM // width
    groups = D_HEADS // hps
    col = lambda j: pl.BlockSpec((L, width), lambda b, h: (row_blk0 + b, col0 + j * groups + h))
    lbs = pl.BlockSpec((DEPTH, width), lambda b, h: (0, h))
    st = pl.BlockSpec((1, hps, D_KDIM, D_VDIM), lambda b, h: (b, h, 0, 0))
    st_sd = jax.ShapeDtypeStruct((n_seq, D_HEADS, D_KDIM, D_VDIM), F32)
    return pl.pallas_call(
        functools.partial(_hgrn_body, layer=layer),
        grid=(n_seq, groups),
        in_specs=[col(0), col(1), col(2), col(3), col(4), lbs, lbs,
                  pl.BlockSpec((1, D_VDIM), lambda b, h: (0, 0)), st, st],
        out_specs=[pl.BlockSpec((L, width), lambda b, h: (b, h)), st, st],
        out_shape=[jax.ShapeDtypeStruct((n_seq * L, D_HEADS * D_VDIM), F32), st_sd, st_sd],
        compiler_params=_cparams("parallel", "parallel"),
        name="hgrn",
    )(u, u, u, u, u, lb_fwd, lb_bwd, norm_d.reshape(1, D_VDIM), s0f, s0b)


def _pack_bf16_pairs(h):
    n = h.shape[1] // 2
    hi = lax.bitcast_convert_type(h[:, :n].astype(BF16).astype(F32), jnp.int32)
    lo = lax.bitcast_convert_type(h[:, n:].astype(BF16).astype(F32), jnp.int32)
    return hi | lax.shift_right_logical(lo, 16)


def _unpack_bf16_pairs(p):
    hi = lax.bitcast_convert_type(p & jnp.int32(-65536), F32).astype(BF16)
    lo = lax.bitcast_convert_type(lax.shift_left(p, 16), F32).astype(BF16)
    return hi, lo


def _outproj_body(*refs, n_x):
    a_refs, b_refs, x_refs = refs[0:2], refs[2:4], refs[4:4 + n_x]
    mod_ref, gf_ref, w_ref, wrh_ref, wrl_ref, rb_ref, x1_ref, h2_ref, chosen_ref, gk_ref, ik_ref = refs[4 + n_x:]
    m = mod_ref[0]
    half = a_refs[0].shape[1]
    out = _dot(_token_tile(a_refs), w_ref[0:half, :]) + _dot(_token_tile(b_refs), w_ref[half:, :])
    x1 = _token_tile(x_refs) + m[:, 2 * D_MODEL:3 * D_MODEL] * out
    x1_ref[...] = x1
    h2 = _rms(x1, gf_ref[...]) * (1.0 + m[:, 4 * D_MODEL:5 * D_MODEL]) + m[:, 3 * D_MODEL:4 * D_MODEL]
    h2_ref[...] = _pack_bf16_pairs(h2)
    h_hi = h2.astype(BF16)
    h_lo = (h2 - h_hi.astype(F32)).astype(BF16)
    logits = _dot_nt(wrh_ref[...], h_hi) + _dot_nt(wrh_ref[...], h_lo) + _dot_nt(wrl_ref[...], h_hi)
    scores = jax.nn.sigmoid(logits)
    work = scores + rb_ref[...]
    expert = lax.broadcasted_iota(jnp.int32, work.shape, 0).astype(F32)
    slot = lax.broadcasted_iota(jnp.int32, (TOP_K, work.shape[1]), 0)
    chosen = jnp.zeros(work.shape, F32)
    gk = jnp.zeros((TOP_K, work.shape[1]), F32)
    ik = jnp.zeros((TOP_K, work.shape[1]), F32)
    for k in range(TOP_K):
        best = jnp.max(work, axis=0, keepdims=True)
        first = jnp.min(jnp.where(work == best, expert, float(N_EXPERTS)), axis=0, keepdims=True)
        hit = expert == first
        chosen = jnp.where(hit, 1.0, chosen)
        gk = jnp.where(slot == k, jnp.sum(jnp.where(hit, scores, 0.0), axis=0, keepdims=True), gk)
        ik = jnp.where(slot == k, first, ik)
        work = jnp.where(hit, -jnp.inf, work)
    chosen_ref[...] = chosen
    gk_ref[...] = jnp.transpose(gk / jnp.sum(gk, axis=0, keepdims=True) * ROUTE_SCALE)
    ik_ref[...] = ik


def _outproj(a, b, x, mod_l, gain_ffn, w_out, w_router, router_bias):
    half = a[0].shape[1]
    a_specs, a_args = _token_specs(a, half)
    b_specs, b_args = _token_specs(b, half)
    x_specs, x_args = _token_specs(x, D_MODEL)
    wr_t = w_router.T
    wr_hi = wr_t.astype(BF16)
    wr_lo = (wr_t - wr_hi.astype(F32)).astype(BF16)
    return pl.pallas_call(
        functools.partial(_outproj_body, n_x=len(x_args)),
        grid=(T_ALL // TM,),
        in_specs=a_specs + b_specs + x_specs + [
                  pl.BlockSpec((1, 1, N_MOD * D_MODEL), lambda i: (_mod_row(i), 0, 0)),
                  pl.BlockSpec((1, D_MODEL), lambda i: (0, 0)),
                  pl.BlockSpec((2 * half, D_MODEL), lambda i: (0, 0)),
                  pl.BlockSpec((N_EXPERTS, D_MODEL), lambda i: (0, 0)),
                  pl.BlockSpec((N_EXPERTS, D_MODEL), lambda i: (0, 0)),
                  pl.BlockSpec((N_EXPERTS, 1), lambda i: (0, 0))],
        out_specs=[pl.BlockSpec((TM, D_MODEL), lambda i: (i, 0)),
                   pl.BlockSpec((TM, D_MODEL // 2), lambda i: (i, 0)),
                   pl.BlockSpec((N_EXPERTS, TM), lambda i: (0, i)),
                   pl.BlockSpec((TM, TOP_K), lambda i: (i, 0)),
                   pl.BlockSpec((TOP_K, TM), lambda i: (0, i))],
        out_shape=[jax.ShapeDtypeStruct((T_ALL, D_MODEL), F32),
                   jax.ShapeDtypeStruct((T_ALL, D_MODEL // 2), jnp.int32),
                   jax.ShapeDtypeStruct((N_EXPERTS, T_ALL), F32),
                   jax.ShapeDtypeStruct((T_ALL, TOP_K), F32),
                   jax.ShapeDtypeStruct((TOP_K, T_ALL), F32)],
        compiler_params=_cparams("parallel"),
        name="outproj_router",
    )(*a_args, *b_args, *x_args, mod_l, gain_ffn.reshape(1, D_MODEL), w_out, wr_hi, wr_lo,
      router_bias.reshape(N_EXPERTS, 1))


def _route_body(chosen_ref, ik_ref, dest_ref, first_ref, count_ref, short_ref, pos_scr):
    n_tiles = T_ALL // TM
    r = lax.broadcasted_iota(jnp.int32, (TM, TM), 0)
    c = lax.broadcasted_iota(jnp.int32, (TM, TM), 1)
    before = (r < c).astype(BF16)

    counts = jnp.zeros((N_EXPERTS, 1), F32)
    for i in range(n_tiles):
        cols = slice(i * TM, (i + 1) * TM)
        m = chosen_ref[:, cols]
        pos_scr[:, cols] = jnp.dot(m.astype(BF16), before, preferred_element_type=F32) + counts
        counts = counts + jnp.sum(m, axis=1, keepdims=True)
    padded = jnp.ceil(counts * (1.0 / MOE_BLK)) * MOE_BLK
    ei = lax.broadcasted_iota(jnp.int32, (N_EXPERTS, N_EXPERTS), 0)
    ej = lax.broadcasted_iota(jnp.int32, (N_EXPERTS, N_EXPERTS), 1)
    end = _dot_hi((ej <= ei).astype(F32), jnp.broadcast_to(padded, (N_EXPERTS, LANES)))[:, 0:1]
    start = end - padded

    expert = lax.broadcasted_iota(jnp.int32, (N_EXPERTS, TM), 0).astype(F32)
    slot = lax.broadcasted_iota(jnp.int32, (TOP_K, TM), 0)
    for i in range(n_tiles):
        cols = slice(i * TM, (i + 1) * TM)
        row_of = pos_scr[:, cols] + start
        ik = ik_ref[:, cols]
        acc = jnp.zeros((TOP_K, TM), F32)
        for k in range(TOP_K):
            pick = jnp.sum(jnp.where(expert == ik[k:k + 1, :], row_of, 0.0), axis=0, keepdims=True)
            acc = jnp.where(slot == k, pick, acc)
        dest_ref[:, cols] = acc.astype(jnp.int32)
    first_ref[...] = jnp.broadcast_to(start * (1.0 / MOE_BLK), (N_EXPERTS, LANES)).astype(jnp.int32)
    count_ref[...] = jnp.broadcast_to(padded * (1.0 / MOE_BLK), (N_EXPERTS, LANES)).astype(jnp.int32)
    in_last = counts - (padded - MOE_BLK)
    short = jnp.where((counts > 0.0) & (in_last <= MOE_BLK // 2), 1.0, 0.0)
    short_ref[...] = jnp.broadcast_to(short, (N_EXPERTS, LANES)).astype(jnp.int32)


def _route(chosen, ik):
    full = lambda shape: pl.BlockSpec(shape, lambda i: (0, 0))
    return pl.pallas_call(
        _route_body,
        grid=(1,),
        in_specs=[full((N_EXPERTS, T_ALL)), full((TOP_K, T_ALL))],
        out_specs=[full((TOP_K, T_ALL))] + [full((N_EXPERTS, LANES))] * 3,
        out_shape=[jax.ShapeDtypeStruct((TOP_K, T_ALL), jnp.int32)]
                  + [jax.ShapeDtypeStruct((N_EXPERTS, LANES), jnp.int32)] * 3,
        scratch_shapes=[pltpu.VMEM((N_EXPERTS, T_ALL), F32)],
        compiler_params=_cparams("arbitrary"),
        name="moe_route",
    )(chosen, ik)


def _sc_worker_id():
    return lax.axis_index("s") * SC_CORES + lax.axis_index("c")


def _sc_dispatch(h2p, dest):
    n_chunks = T_ALL // DISP_CHUNK
    k_per = TOP_K // DISP_SPLIT
    items_per_worker = n_chunks * DISP_SPLIT // SC_WORKERS
    chunk_stride = SC_WORKERS // DISP_SPLIT
    width = h2p.shape[1]
    mesh = plsc.VectorSubcoreMesh(core_axis_name="c", subcore_axis_name="s")

    @functools.partial(
        pl.kernel, mesh=mesh,
        out_type=jax.ShapeDtypeStruct((MOE_ROWS, width), jnp.int32),
        scratch_types=[pltpu.VMEM((k_per, DISP_CHUNK), jnp.int32), pltpu.VMEM((DISP_CHUNK, width), jnp.int32),
                       pltpu.SemaphoreType.DMA],
    )
    def run(x_hbm, dest_hbm, xs_hbm, idx_v, rows_v, sem):
        wid = _sc_worker_id()
        group = wid % DISP_SPLIT
        for i in range(items_per_worker):
            chunk = i * chunk_stride + wid // DISP_SPLIT
            tokens = pl.ds(pl.multiple_of(chunk * DISP_CHUNK, DISP_CHUNK), DISP_CHUNK)
            pltpu.sync_copy(dest_hbm.at[group, :, tokens], idx_v)
            pltpu.sync_copy(x_hbm.at[tokens], rows_v)
            scatters = [pltpu.make_async_copy(rows_v, xs_hbm.at[idx_v.at[k]], sem) for k in range(k_per)]
            for cp in scatters:
                cp.start()
            for cp in scatters:
                cp.wait()

    return run(h2p, dest.reshape(DISP_SPLIT, k_per, T_ALL))


def _sc_collect(y, dest_flat):
    per_worker = T_ALL // SC_WORKERS
    n_chunks = per_worker // COLLECT_CHUNK
    n_steps = TOP_K * n_chunks
    width = y.shape[1]
    mesh = plsc.VectorSubcoreMesh(core_axis_name="c", subcore_axis_name="s")

    @functools.partial(
        pl.kernel, mesh=mesh,
        out_type=jax.ShapeDtypeStruct((TOP_K * T_ALL, width), y.dtype),
        scratch_types=[pltpu.VMEM((TOP_K * per_worker,), jnp.int32),
                       pltpu.VMEM((COLLECT_CHUNK, width), y.dtype), pltpu.VMEM((COLLECT_CHUNK, width), y.dtype),
                       pltpu.SemaphoreType.DMA, pltpu.SemaphoreType.DMA],
    )
    def run(y_hbm, dest_hbm, yg_hbm, idx_v, rows0, rows1, sem0, sem1):
        wid = _sc_worker_id()
        bufs = ((rows0, sem0), (rows1, sem1))
        for k in range(TOP_K):
            pltpu.sync_copy(dest_hbm.at[pl.ds(k * T_ALL + wid * per_worker, per_worker)],
                            idx_v.at[pl.ds(k * per_worker, per_worker)])

        def gather(step, buf):
            rows, sem = buf
            idx = idx_v.at[pl.ds(pl.multiple_of(step * COLLECT_CHUNK, 8), COLLECT_CHUNK)]
            return pltpu.make_async_copy(y_hbm.at[idx], rows, sem)

        def out_rows(step):
            off = (step // n_chunks) * T_ALL + wid * per_worker + (step % n_chunks) * COLLECT_CHUNK
            return yg_hbm.at[pl.ds(pl.multiple_of(off, 8), COLLECT_CHUNK)]

        gather(0, bufs[0]).start()

        @pl.loop(0, n_steps, step=2)
        def _(base):
            for j in range(2):
                step = base + j

                @pl.when(step + 1 < n_steps)
                def _():
                    gather(step + 1, bufs[1 - j]).start()

                gather(step, bufs[j]).wait()
                pltpu.sync_copy(bufs[j][0], out_rows(step))

    return run(y, dest_flat)


def _expert_body(first_ref, count_ref, short_ref, xs_hbm, wg_ref, wu_ref, wd_ref, y_hbm,
                 wg_bf, wu_bf, wd_bf, x_buf, y_buf, in_sem, out_sem):
    e = pl.program_id(0)
    first = first_ref[e]
    count = count_ref[e]
    n_used = first_ref[N_EXPERTS - 1] + count_ref[N_EXPERTS - 1]
    half = D_MODEL // 2
    wg_bf[...] = wg_ref[0, 0].astype(BF16)
    wu_bf[...] = wu_ref[0, 0].astype(BF16)
    wd_bf[...] = wd_ref[0, 0].astype(BF16)

    def part_rows(g, part, n_parts):
        size = MOE_BLK // n_parts
        return pl.ds(pl.multiple_of(g * MOE_BLK + part * size, size), size), pl.ds(part * size, size)

    def in_copies(g):
        slot = g & (EXPERT_SLOTS - 1)
        out = []
        for part in range(EXPERT_IN_PARTS):
            src, dst = part_rows(g, part, EXPERT_IN_PARTS)
            out.append(pltpu.make_async_copy(xs_hbm.at[src], x_buf.at[slot, dst], in_sem.at[slot]))
        return out

    def out_copies(g):
        slot = g & (EXPERT_SLOTS - 1)
        out = []
        for part in range(EXPERT_OUT_PARTS):
            dst, src = part_rows(g, part, EXPERT_OUT_PARTS)
            out.append(pltpu.make_async_copy(y_buf.at[slot, src], y_hbm.at[dst], out_sem.at[slot]))
        return out

    @pl.when((first == 0) & (count > 0))
    def _():
        for ahead in range(EXPERT_SLOTS - 1):
            @pl.when(ahead < n_used)
            def _():
                for cp in in_copies(ahead):
                    cp.start()

    def block(b, carry):
        g = first + b
        slot = g & (EXPERT_SLOTS - 1)
        for cp in in_copies(g):
            cp.wait()

        @pl.when(g + EXPERT_SLOTS - 1 < n_used)
        def _():
            for cp in in_copies(g + EXPERT_SLOTS - 1):
                cp.start()

        @pl.when(g >= EXPERT_SLOTS)
        def _():
            for cp in out_copies(g - EXPERT_SLOTS):
                cp.wait()

        def ffn(n_rows):
            hi, lo = _unpack_bf16_pairs(x_buf[slot, 0:n_rows])

            def proj(w_bf):
                return (jnp.dot(hi, w_bf[0:half, :], preferred_element_type=F32)
                        + jnp.dot(lo, w_bf[half:, :], preferred_element_type=F32))

            hid = _silu(proj(wg_bf)) * proj(wu_bf)
            y_buf[slot, 0:n_rows] = _pack_bf16_pairs(
                jnp.dot(hid.astype(BF16), wd_bf[...], preferred_element_type=F32))

        short = (b == count - 1) & (short_ref[e] == 1)

        @pl.when(short)
        def _():
            ffn(MOE_BLK // 2)
            y_buf[slot, MOE_BLK // 2:MOE_BLK] = jnp.zeros((MOE_BLK // 2, D_MODEL // 2), jnp.int32)

        @pl.when(jnp.logical_not(short))
        def _():
            ffn(MOE_BLK)

        for cp in out_copies(g):
            cp.start()
        return carry

    lax.fori_loop(0, count, block, 0)

    @pl.when(e == N_EXPERTS - 1)
    def _():
        for back in range(EXPERT_SLOTS, 0, -1):
            @pl.when(n_used >= back)
            def _():
                for cp in out_copies(n_used - back):
                    cp.wait()


EXPERT_SLOTS = 4
EXPERT_IN_PARTS = 2
EXPERT_OUT_PARTS = 4


def _experts(first_blk, n_blk, short_last, xs, layer, w_gate, w_up, w_down):
    w_in = pl.BlockSpec((1, 1, D_MODEL, D_EXPERT), lambda e, *_: (layer, e, 0, 0))
    grid_spec = pltpu.PrefetchScalarGridSpec(
        num_scalar_prefetch=3,
        grid=(N_EXPERTS,),
        in_specs=[pl.BlockSpec(memory_space=pl.ANY), w_in, w_in,
                  pl.BlockSpec((1, 1, D_EXPERT, D_MODEL), lambda e, *_: (layer, e, 0, 0))],
        out_specs=pl.BlockSpec(memory_space=pl.ANY),
        scratch_shapes=[pltpu.VMEM((D_MODEL, D_EXPERT), BF16), pltpu.VMEM((D_MODEL, D_EXPERT), BF16),
                        pltpu.VMEM((D_EXPERT, D_MODEL), BF16),
                        pltpu.VMEM((EXPERT_SLOTS, MOE_BLK, D_MODEL // 2), jnp.int32),
                        pltpu.VMEM((EXPERT_SLOTS, MOE_BLK, D_MODEL // 2), jnp.int32),
                        pltpu.SemaphoreType.DMA((EXPERT_SLOTS,)), pltpu.SemaphoreType.DMA((EXPERT_SLOTS,))],
    )
    return pl.pallas_call(
        _expert_body,
        grid_spec=grid_spec,
        out_shape=jax.ShapeDtypeStruct((MOE_ROWS, D_MODEL // 2), jnp.int32),
        compiler_params=_cparams("arbitrary"),
        name="moe_experts",
    )(first_blk, n_blk, short_last, xs, w_gate, w_up, w_down)


def _combine_body(x1_ref, h2_ref, yg_ref, gk_ref, mod_ref, sg_ref, su_ref, sd_ref, fn_ref, *o_refs, final):
    hi, lo = _unpack_bf16_pairs(h2_ref[...])
    half = D_MODEL // 2

    def proj(w_ref):
        return _dot(hi, w_ref[0:half, :]) + _dot(lo, w_ref[half:, :])

    shared = _dot(_silu(proj(sg_ref)) * proj(su_ref), sd_ref[...])
    acc_hi, acc_lo = shared[:, :half], shared[:, half:]
    gk = gk_ref[...]
    for k in range(TOP_K):
        y_hi, y_lo = _unpack_bf16_pairs(yg_ref[k])
        acc_hi = acc_hi + gk[:, k:k + 1] * y_hi.astype(F32)
        acc_lo = acc_lo + gk[:, k:k + 1] * y_lo.astype(F32)
    acc = jnp.concatenate([acc_hi, acc_lo], axis=1)
    m = mod_ref[0]
    y = x1_ref[...] + m[:, 5 * D_MODEL:6 * D_MODEL] * acc
    if not final:
        o_refs[0][...] = y
        return
    y = _rms(y, fn_ref[...])
    is_ctx = pl.program_id(0) < N_CTX_TILES

    @pl.when(is_ctx)
    def _():
        o_refs[0][...] = y

    @pl.when(jnp.logical_not(is_ctx))
    def _():
        o_refs[1][...] = y


def _combine(x1, h2p, yg, gk, mod_l, ws_gate, ws_up, ws_down, final_norm, final):
    tok = lambda shape: pl.BlockSpec(shape, lambda i: (i, 0))
    full = lambda shape: pl.BlockSpec(shape, lambda i: (0, 0))
    if final:
        out_specs, _ = _token_specs((None, None), D_MODEL)
        out_shape = [jax.ShapeDtypeStruct((T_CTX, D_MODEL), F32), jax.ShapeDtypeStruct((T_LAT, D_MODEL), F32)]
    else:
        out_specs = tok((TM, D_MODEL))
        out_shape = jax.ShapeDtypeStruct((T_ALL, D_MODEL), F32)
    return pl.pallas_call(
        functools.partial(_combine_body, final=final),
        grid=(T_ALL // TM,),
        in_specs=[tok((TM, D_MODEL)), tok((TM, D_MODEL // 2)),
                  pl.BlockSpec((TOP_K, TM, D_MODEL // 2), lambda i: (0, i, 0)),
                  tok((TM, TOP_K)),
                  pl.BlockSpec((1, 1, N_MOD * D_MODEL), lambda i: (_mod_row(i), 0, 0)),
                  full((D_MODEL, D_EXPERT)), full((D_MODEL, D_EXPERT)), full((D_EXPERT, D_MODEL)),
                  full((1, D_MODEL))],
        out_specs=out_specs,
        out_shape=out_shape,
        compiler_params=_cparams("arbitrary"),
        name="moe_combine",
    )(x1, h2p, yg, gk, mod_l, ws_gate, ws_up, ws_down, final_norm.reshape(1, D_MODEL))


def _moe(x1, h2p, chosen, gk, ik, mod_l, layer, w_gate, w_up, w_down, ws_gate, ws_up, ws_down, final_norm, final):
    dest, first_blk, n_blk, short_last = _route(chosen, ik)
    xs = _sc_dispatch(h2p, dest)
    y = _experts(first_blk[:, 0], n_blk[:, 0], short_last[:, 0], xs, layer, w_gate, w_up, w_down)
    yg = _sc_collect(y, dest.reshape(-1)).reshape(TOP_K, T_ALL, D_MODEL // 2)
    return _combine(x1, h2p, yg, gk, mod_l, ws_gate, ws_up, ws_down, final_norm, final)


def kernel(x_prompt, x_sample, cache_a_k, cache_a_v, cache_b_k, cache_b_v, state_d_fwd, state_d_bwd, c, c_ctx, w_ada, b_ada, norm_mix, norm_ffn, w_in_attn, w_out_attn, sink_a, rpb_b, w_in_rec, w_out_rec, conv_w, conv_b, filt_w1, filt_b1, filt_w2, filt_b2, filt_w3, filt_b3, filt_freq, filt_w4, d_skip, lb_fwd, lb_bwd, norm_d, w_router, router_bias, w_gate, w_up, w_down, ws_gate, ws_up, ws_down, final_norm):
    x = (x_prompt.reshape(T_CTX, D_MODEL), x_sample.reshape(T_LAT, D_MODEL))
    cvec = jnp.concatenate([c_ctx[None, :], c], axis=0)
    c_lanes = jnp.broadcast_to(cvec[:, :, None], (N_CVEC, D_MODEL, LANES))
    mod = [_ada(c_lanes, l, w_ada, b_ada).reshape(CVEC_PAD, 1, N_MOD * D_MODEL) for l in range(DEPTH)]

    new_kv = None
    new_state = None
    for l in range(DEPTH):
        j = l // 2
        final = l == DEPTH - 1
        if l % 2 == 0:
            qkv = _inproj(x, mod[l], norm_mix[l], w_in_attn[j])
            oa_ctx, ob_ctx, *new_kv = _ctx_attn(qkv, sink_a[j])
            new_kv = tuple(new_kv)
            q_rot, k_rot = _rope(qkv)
            cache = lambda t: t[:, j].reshape(DEC_BATCH, PAST_LEN, -1)
            oa_lat = _win_attn(qkv, q_rot, k_rot, cache(cache_a_k), cache(cache_a_v), sink_a[j])
            ob_lat = _na_attn(qkv, cache(cache_b_k), cache(cache_b_v), _na_rel_rows(rpb_b[j]))
            mix_a = (oa_ctx, oa_lat)
            mix_b = (ob_ctx, ob_lat)
            w_out = w_out_attn[j]
        else:
            u = _inproj(x, mod[l], norm_mix[l], w_in_rec[j])
            filt = (filt_w1[j], filt_b1[j], filt_w2[j], filt_b2[j], filt_w3[j], filt_b3[j], filt_freq[j],
                    filt_w4[j])
            y_ctx = _hyena(u, 0, BATCH, SEQ, conv_w[j], conv_b[j], d_skip[j], _hyena_filter(SEQ, filt))
            y_lat = _hyena(u, T_CTX // DEC_SEQ, DEC_BATCH, DEC_SEQ, conv_w[j], conv_b[j], d_skip[j],
                           _hyena_filter(DEC_SEQ, filt))
            zeros = jnp.zeros((BATCH, D_HEADS, D_KDIM, D_VDIM), F32)
            o_ctx, s_f, s_b = _hgrn(u, 0, BATCH, SEQ, lb_fwd, lb_bwd, norm_d[j], zeros, zeros, l)
            o_lat, _, _ = _hgrn(u, T_CTX // DEC_SEQ, DEC_BATCH, DEC_SEQ, lb_fwd, lb_bwd, norm_d[j],
                                state_d_fwd[:, j], state_d_bwd[:, j], l)
            new_state = (s_f[:, None], s_b[:, None])
            mix_a = (y_ctx, y_lat)
            mix_b = (o_ctx, o_lat)
            w_out = w_out_rec[j]
        x1, h2p, chosen, gk, ik = _outproj(mix_a, mix_b, x, mod[l], norm_ffn[l], w_out, w_router[l],
                                           router_bias[l])
        x = _moe(x1, h2p, chosen, gk, ik, mod[l], l, w_gate, w_up, w_down, ws_gate[l], ws_up[l],
                 ws_down[l], final_norm, final)

    y_prompt = x[0].reshape(BATCH, SEQ, D_MODEL)
    y_sample = x[1].reshape(DEC_BATCH, DEC_SEQ, D_MODEL)
    return (y_prompt, y_sample) + new_kv + new_state
```

```python
import functools
import math

import numpy as np
import jax
import jax.numpy as jnp
from jax import lax
from jax.experimental import pallas as pl
from jax.experimental.pallas import tpu as pltpu
from jax.experimental.pallas import tpu_sc as plsc

F32 = jnp.float32
BF16 = jnp.bfloat16
HI = lax.Precision.HIGHEST

D_MODEL = 1024
BATCH = 16
SEQ = 256
DEPTH = 2
DEC_BATCH = 2
DEC_SEQ = 1024
PAST_LEN = 512
GRID_W = 64
HEAD_DIM = 64
N_MOD = 6
RMS_EPS = 1e-6
A_HEADS = 8
A_KV_HEADS = 2
A_GROUP = A_HEADS // A_KV_HEADS
WINDOW = 128
ROPE_BASE = 10000.0
B_HEADS = 8
NA_ROWS = 8
NA_COLS = 16
C_DIM = 512
C_EMB = 33
C_FFN = 64
HYENA_MIN_DECAY = math.log(1e-2) / 1.5
HYENA_MAX_DECAY = math.log(1e-2) / 0.3
D_KDIM = 128
D_VDIM = 128
D_HEADS = 4
N_EXPERTS = 64
TOP_K = 8
D_EXPERT = 256
ROUTE_SCALE = 2.5
A_Q = A_HEADS * HEAD_DIM
A_KV = A_KV_HEADS * HEAD_DIM
B_W = B_HEADS * HEAD_DIM
ATTN_IN = A_Q + 2 * A_KV + 3 * B_W
REC_IN = 3 * C_DIM + 5 * D_HEADS * D_KDIM

T_CTX = BATCH * SEQ
T_LAT = DEC_BATCH * DEC_SEQ
T_ALL = T_CTX + T_LAT
N_CVEC = 1 + DEC_BATCH
CVEC_PAD = 8
TM = 512
MASK_NEG = -1e30
GLA_CHUNK = 64
GLA_SPAN = 256
HGRN_HEADS_PER_STEP = 4
DFT_CHUNK = 256
MOE_BLK = 512
MOE_REGIONS = 2
K_PER_REGION = TOP_K // MOE_REGIONS
MOE_NBLK = -(-(T_ALL * K_PER_REGION + N_EXPERTS * (MOE_BLK - 1)) // MOE_BLK)
MOE_ROWS = MOE_NBLK * MOE_BLK
SC_CORES = 2
SC_SUBCORES = 16
SC_WORKERS = SC_CORES * SC_SUBCORES
DISP_CHUNK = 128
DISP_SPLIT = 2
COLLECT_CHUNK = 64
VMEM_LIMIT = 56 * 1024 * 1024


def _cparams(*sem):
    return pltpu.CompilerParams(dimension_semantics=sem, vmem_limit_bytes=VMEM_LIMIT)


def _mod_row(i):
    return jnp.where(i < T_CTX // TM, 0, 1 + (i - T_CTX // TM) // (DEC_SEQ // TM))


def _dot(a, b):
    return jnp.dot(a.astype(BF16), b.astype(BF16), preferred_element_type=F32)


def _dot_nt(a, b):
    return lax.dot_general(a.astype(BF16), b.astype(BF16), (((1,), (1,)), ((), ())),
                           preferred_element_type=F32)


def _dot_tn(a, b):
    return lax.dot_general(a.astype(BF16), b.astype(BF16), (((0,), (0,)), ((), ())),
                           preferred_element_type=F32)


def _dot_hi(a, b):
    return jnp.dot(a, b, precision=HI, preferred_element_type=F32)


def _split_bf16(x):
    hi = x.astype(BF16)
    return hi, (x - hi.astype(F32)).astype(BF16)


def _dot_split(a, b):
    a_hi, a_lo = _split_bf16(a)
    b_hi, b_lo = _split_bf16(b)
    dot = lambda x, y: jnp.dot(x, y, preferred_element_type=F32)
    return dot(a_hi, b_hi) + dot(a_hi, b_lo) + dot(a_lo, b_hi)


def _silu(x):
    return x * jax.nn.sigmoid(x)


def _rms(x, g):
    return x * lax.rsqrt(jnp.mean(x * x, axis=-1, keepdims=True) + RMS_EPS) * g


ADA_TN = 1536
ADA_UNROLL = 4


def _ada_body(cb_ref, w_ref, b_ref, o_ref):
    tn = o_ref.shape[-1]
    n_slab = tn // LANES

    def step(k8, accs):
        r0 = pl.multiple_of(k8 * 8, 8)
        sk = [_silu(cb_ref[j, pl.ds(r0, 8), :]) for j in range(N_CVEC)]
        out = []
        for s in range(n_slab):
            wk = w_ref[0, pl.ds(r0, 8), s * LANES:(s + 1) * LANES]
            out.extend(accs[s * N_CVEC + j] + wk * sk[j] for j in range(N_CVEC))
        return tuple(out)

    accs = lax.fori_loop(0, D_MODEL // 8, step,
                         tuple(jnp.zeros((8, LANES), F32) for _ in range(n_slab * N_CVEC)), unroll=ADA_UNROLL)
    o_ref[0] = jnp.zeros((CVEC_PAD, tn), F32)
    for s in range(n_slab):
        for j in range(N_CVEC):
            o_ref[0, j:j + 1, s * LANES:(s + 1) * LANES] = (
                jnp.sum(accs[s * N_CVEC + j], axis=0, keepdims=True) + b_ref[0, :, s * LANES:(s + 1) * LANES])


def _ada(c_lanes, layer, w_ada, b_ada):
    n_out = N_MOD * D_MODEL
    return pl.pallas_call(
        _ada_body,
        grid=(n_out // ADA_TN,),
        in_specs=[pl.BlockSpec((N_CVEC, D_MODEL, LANES), lambda n: (0, 0, 0)),
                  pl.BlockSpec((1, D_MODEL, ADA_TN), lambda n: (layer, 0, n)),
                  pl.BlockSpec((1, 1, ADA_TN), lambda n: (layer, 0, n))],
        out_specs=pl.BlockSpec((1, CVEC_PAD, ADA_TN), lambda n: (0, 0, n)),
        out_shape=jax.ShapeDtypeStruct((1, CVEC_PAD, n_out), F32),
        compiler_params=_cparams("parallel"),
        name="ada",
    )(c_lanes, w_ada, b_ada.reshape(DEPTH, 1, n_out))


N_CTX_TILES = T_CTX // TM


def _token_specs(x, width):
    if not isinstance(x, tuple):
        return [pl.BlockSpec((TM, width), lambda i: (i, 0))], (x,)
    return ([pl.BlockSpec((TM, width), lambda i: (jnp.minimum(i, N_CTX_TILES - 1), 0)),
             pl.BlockSpec((TM, width), lambda i: (jnp.maximum(i - N_CTX_TILES, 0), 0))], x)


def _token_tile(refs):
    if len(refs) == 1:
        return refs[0][...]
    return jnp.where(pl.program_id(0) < N_CTX_TILES, refs[0][...], refs[1][...])


def _inproj_body(*refs, n_x):
    x_refs, (mod_ref, g_ref, w_ref, o_ref, w_bf) = refs[:n_x], refs[n_x:]

    @pl.when(pl.program_id(0) == 0)
    def _():
        w_bf[...] = w_ref[...].astype(BF16)

    m = mod_ref[0]
    h = _rms(_token_tile(x_refs), g_ref[...]) * (1.0 + m[:, D_MODEL:2 * D_MODEL]) + m[:, 0:D_MODEL]
    o_ref[...] = _dot(h, w_bf[...])


def _inproj(x, mod_l, gain, w):
    n = w.shape[1]
    x_specs, x_args = _token_specs(x, D_MODEL)
    return pl.pallas_call(
        functools.partial(_inproj_body, n_x=len(x_args)),
        grid=(T_ALL // TM,),
        in_specs=x_specs + [pl.BlockSpec((1, 1, N_MOD * D_MODEL), lambda i: (_mod_row(i), 0, 0)),
                            pl.BlockSpec((1, D_MODEL), lambda i: (0, 0)),
                            pl.BlockSpec((D_MODEL, n), lambda i: (0, 0), pipeline_mode=pl.Buffered(1))],
        out_specs=pl.BlockSpec((TM, n), lambda i: (i, 0)),
        out_shape=jax.ShapeDtypeStruct((T_ALL, n), F32),
        scratch_shapes=[pltpu.VMEM((D_MODEL, n), BF16)],
        compiler_params=_cparams("arbitrary"),
        name="inproj",
    )(*x_args, mod_l, gain.reshape(1, D_MODEL), w)


def _head_cols(h):
    return slice(h * HEAD_DIM, (h + 1) * HEAD_DIM)


def _group_rows(ref, rows, first_col, sink_ref, hk):
    n = rows.stop - rows.start
    q = jnp.concatenate([ref[rows, first_col + g * HEAD_DIM:first_col + (g + 1) * HEAD_DIM]
                         for g in range(A_GROUP)], axis=0)
    sink = jnp.concatenate([jnp.broadcast_to(sink_ref[:, hk * A_GROUP + g:hk * A_GROUP + g + 1], (n, 1))
                            for g in range(A_GROUP)], axis=0)
    return q, sink


def _ctx_attn_body(qkv_ref, sink_ref, oa_ref, ob_ref, ak_ref, av_ref, bk_ref, bv_ref):
    scale = HEAD_DIM ** -0.5
    lane = lax.broadcasted_iota(jnp.int32, (SEQ, LANES), 1)
    in_half = [lane < HEAD_DIM, lane >= HEAD_DIM]

    def attend(q, k, v, sink):
        s = _dot_nt(q, k) * scale
        m = jnp.max(s, axis=-1, keepdims=True)
        if sink is not None:
            m = jnp.maximum(m, sink)
        p = jnp.exp(s - m)
        den = jnp.sum(p, axis=-1, keepdims=True)
        if sink is not None:
            den = den + jnp.exp(sink - m)
        return _dot(p, v) / den

    def tile(first_col, t):
        return qkv_ref[:, first_col + t * LANES:first_col + (t + 1) * LANES]

    base = A_Q + 2 * A_KV
    for hk in range(A_KV_HEADS):
        dst = pl.ds(hk, SEQ, stride=A_KV_HEADS)
        ak_ref[0, dst, :] = qkv_ref[:, A_Q + hk * HEAD_DIM:A_Q + (hk + 1) * HEAD_DIM]
        av_ref[0, dst, :] = qkv_ref[:, A_Q + A_KV + hk * HEAD_DIM:A_Q + A_KV + (hk + 1) * HEAD_DIM]
    for h in range(B_HEADS):
        dst = pl.ds(h, SEQ, stride=B_HEADS)
        bk_ref[0, dst, :] = qkv_ref[:, base + B_W + h * HEAD_DIM:base + B_W + (h + 1) * HEAD_DIM]
        bv_ref[0, dst, :] = qkv_ref[:, base + 2 * B_W + h * HEAD_DIM:base + 2 * B_W + (h + 1) * HEAD_DIM]

    k_t, v_t = tile(A_Q, 0), tile(A_Q + A_KV, 0)
    k_sw, v_sw = pltpu.roll(k_t, HEAD_DIM, axis=1), pltpu.roll(v_t, HEAD_DIM, axis=1)
    tiles_per_kv = A_GROUP // HEADS_PER_TILE
    for hk in range(A_KV_HEADS):
        q_tiles = [tile(0, hk * tiles_per_kv + j) for j in range(tiles_per_kv)]
        halves = []
        for p in range(HEADS_PER_TILE):
            q = jnp.concatenate([jnp.where(in_half[p], qt, 0.0) for qt in q_tiles], axis=0)
            heads = [(hk * tiles_per_kv + j) * HEADS_PER_TILE + p for j in range(tiles_per_kv)]
            sink = jnp.concatenate([jnp.broadcast_to(sink_ref[:, h:h + 1], (SEQ, 1)) for h in heads], axis=0)
            halves.append(attend(q, k_t if p == hk else k_sw, v_t if p == hk else v_sw, sink))
        first_half = lax.broadcasted_iota(jnp.int32, halves[0].shape, 1) < HEAD_DIM
        o = jnp.where(first_half, halves[0], halves[1])
        for j in range(tiles_per_kv):
            t = hk * tiles_per_kv + j
            oa_ref[:, t * LANES:(t + 1) * LANES] = o[j * SEQ:(j + 1) * SEQ]

    for t in range(B_HEADS // HEADS_PER_TILE):
        q_t, k_b, v_b = tile(base, t), tile(base + B_W, t), tile(base + 2 * B_W, t)
        halves = [attend(jnp.where(in_half[p], q_t, 0.0), k_b, v_b, None) for p in range(HEADS_PER_TILE)]
        ob_ref[:, t * LANES:(t + 1) * LANES] = jnp.where(in_half[0], halves[0], halves[1])


def _ctx_attn(qkv, sink):
    kv_spec = lambda heads: pl.BlockSpec((1, SEQ * heads, HEAD_DIM), lambda b: (b, 0, 0))
    kv_sd = lambda heads: jax.ShapeDtypeStruct((BATCH, SEQ * heads, HEAD_DIM), F32)
    outs = pl.pallas_call(
        _ctx_attn_body,
        grid=(BATCH,),
        in_specs=[pl.BlockSpec((SEQ, ATTN_IN), lambda b: (b, 0)),
                  pl.BlockSpec((1, A_HEADS), lambda b: (0, 0))],
        out_specs=[pl.BlockSpec((SEQ, A_Q), lambda b: (b, 0)), pl.BlockSpec((SEQ, B_W), lambda b: (b, 0)),
                   kv_spec(A_KV_HEADS), kv_spec(A_KV_HEADS), kv_spec(B_HEADS), kv_spec(B_HEADS)],
        out_shape=[jax.ShapeDtypeStruct((T_CTX, A_Q), F32), jax.ShapeDtypeStruct((T_CTX, B_W), F32),
                   kv_sd(A_KV_HEADS), kv_sd(A_KV_HEADS), kv_sd(B_HEADS), kv_sd(B_HEADS)],
        compiler_params=_cparams("parallel"),
        name="ctx_attn",
    )(qkv, sink.reshape(1, A_HEADS))
    caches = [t.reshape(BATCH, 1, SEQ, -1, HEAD_DIM) for t in outs[2:]]
    return outs[0], outs[1], *caches


@functools.lru_cache(maxsize=None)
def _rope_tables(width):
    half = HEAD_DIM // 2
    t = np.arange(DEC_SEQ)
    inv = ROPE_BASE ** (-np.arange(0, half, 2, dtype=np.float64) / half)
    ang_r = (t // GRID_W)[:, None] * inv[None, :]
    ang_c = (t % GRID_W)[:, None] * inv[None, :]
    cos = np.concatenate([np.cos(ang_r)] * 2 + [np.cos(ang_c)] * 2, axis=-1)
    sin = np.concatenate([-np.sin(ang_r), np.sin(ang_r), -np.sin(ang_c), np.sin(ang_c)], axis=-1)
    reps = width // HEAD_DIM
    return (np.tile(cos, (1, reps)).astype(np.float32), np.tile(sin, (1, reps)).astype(np.float32))


def _rope_body(q_ref, k_ref, cq_ref, sq_ref, ck_ref, sk_ref, qo_ref, ko_ref):
    quarter = HEAD_DIM // 4

    def rot(x, cos, sin):
        w = x.shape[-1]
        lane = lax.broadcasted_iota(jnp.int32, x.shape, 1)
        fwd = pltpu.roll(x, w - quarter, axis=1)
        bwd = pltpu.roll(x, quarter, axis=1)
        partner = jnp.where((lane & (2 * quarter - 1)) < quarter, fwd, bwd)
        return x * cos + partner * sin

    qo_ref[...] = rot(q_ref[...], cq_ref[...], sq_ref[...])
    ko_ref[...] = rot(k_ref[...], ck_ref[...], sk_ref[...])


def _rope(qkv):
    cq, sq = _rope_tables(A_Q)
    ck, sk = _rope_tables(A_KV)
    tab = lambda w: pl.BlockSpec((DEC_SEQ, w), lambda b: (0, 0))
    row0 = T_CTX // DEC_SEQ
    return pl.pallas_call(
        _rope_body,
        grid=(DEC_BATCH,),
        in_specs=[pl.BlockSpec((DEC_SEQ, A_Q), lambda b: (row0 + b, 0)),
                  pl.BlockSpec((DEC_SEQ, A_KV), lambda b: (row0 + b, A_Q // A_KV)),
                  tab(A_Q), tab(A_Q), tab(A_KV), tab(A_KV)],
        out_specs=[pl.BlockSpec((DEC_SEQ, A_Q), lambda b: (b, 0)),
                   pl.BlockSpec((DEC_SEQ, A_KV), lambda b: (b, 0))],
        out_shape=[jax.ShapeDtypeStruct((T_LAT, A_Q), F32), jax.ShapeDtypeStruct((T_LAT, A_KV), F32)],
        compiler_params=_cparams("parallel"),
        name="rope",
    )(qkv, qkv, jnp.asarray(cq), jnp.asarray(sq), jnp.asarray(ck), jnp.asarray(sk))


WIN_QB = 256


def _pick_head(x, h, n_heads):
    out = x[:, _head_cols(0)]
    for i in range(1, n_heads):
        out = jnp.where(h == i, x[:, _head_cols(i)], out)
    return out


def _win_attn_body(qraw_ref, qrot_ref, krot_ref, v_ref, kc_ref, vc_ref, sink_ref, o_ref):
    scale = HEAD_DIM ** -0.5
    hk = pl.program_id(1)
    tiles = A_GROUP // HEADS_PER_TILE

    def kv_in_half(x):
        swapped = pltpu.roll(x, HEAD_DIM, axis=1)
        return [jnp.where(hk == p, x, swapped) for p in range(HEADS_PER_TILE)]

    k, v, kc, vc = kv_in_half(krot_ref[...]), kv_in_half(v_ref[...]), kv_in_half(kc_ref[0]), kv_in_half(vc_ref[0])
    head_lane = lax.broadcasted_iota(jnp.int32, (1, A_HEADS), 1)

    def sink_rows(p):
        heads = [hk * A_GROUP + j * HEADS_PER_TILE + p for j in range(tiles)]
        vals = [jnp.sum(jnp.where(head_lane == h, sink_ref[...], 0.0), axis=-1, keepdims=True) for h in heads]
        return jnp.concatenate([jnp.broadcast_to(s, (WIN_QB, 1)) for s in vals], axis=0)

    sinks = [sink_rows(p) for p in range(HEADS_PER_TILE)]
    lane = lax.broadcasted_iota(jnp.int32, (tiles * WIN_QB, LANES), 1)
    in_half = [lane < HEAD_DIM, lane >= HEAD_DIM]
    for qb in range(DEC_SEQ // WIN_QB):
        q0 = qb * WIN_QB
        rows = slice(q0, q0 + WIN_QB)
        lo = max(0, q0 - WINDOW)
        hi = min(DEC_SEQ, q0 + WIN_QB + WINDOW)
        q_rot = jnp.concatenate([qrot_ref[rows, j * LANES:(j + 1) * LANES] for j in range(tiles)], axis=0)
        q_raw = jnp.concatenate([qraw_ref[rows, j * LANES:(j + 1) * LANES] for j in range(tiles)], axis=0)
        halves = []
        for p in range(HEADS_PER_TILE):
            s_loc = _dot_nt(jnp.where(in_half[p], q_rot, 0.0), k[p][lo:hi]) * scale
            qpos = q0 + (lax.broadcasted_iota(jnp.int32, s_loc.shape, 0) & (WIN_QB - 1))
            kpos = lo + lax.broadcasted_iota(jnp.int32, s_loc.shape, 1)
            s_loc = jnp.where(jnp.abs(kpos - qpos) <= WINDOW, s_loc, MASK_NEG)
            s_ctx = _dot_nt(jnp.where(in_half[p], q_raw, 0.0), kc[p]) * scale
            m = jnp.maximum(jnp.maximum(jnp.max(s_loc, axis=-1, keepdims=True),
                                        jnp.max(s_ctx, axis=-1, keepdims=True)), sinks[p])
            p_loc = jnp.exp(s_loc - m)
            p_ctx = jnp.exp(s_ctx - m)
            den = (jnp.sum(p_loc, axis=-1, keepdims=True) + jnp.sum(p_ctx, axis=-1, keepdims=True)
                   + jnp.exp(sinks[p] - m))
            halves.append((_dot(p_ctx, vc[p]) + _dot(p_loc, v[p][lo:hi])) / den)
        o = jnp.where(in_half[0], halves[0], halves[1])
        for j in range(tiles):
            o_ref[rows, j * LANES:(j + 1) * LANES] = o[j * WIN_QB:(j + 1) * WIN_QB]


def _win_attn(qkv, q_rot, k_rot, kc, vc, sink):
    row0 = T_CTX // DEC_SEQ
    gw = A_GROUP * HEAD_DIM
    return pl.pallas_call(
        _win_attn_body,
        grid=(DEC_BATCH, A_KV_HEADS),
        in_specs=[pl.BlockSpec((DEC_SEQ, gw), lambda b, h: (row0 + b, h)),
                  pl.BlockSpec((DEC_SEQ, gw), lambda b, h: (b, h)),
                  pl.BlockSpec((DEC_SEQ, A_KV), lambda b, h: (b, 0)),
                  pl.BlockSpec((DEC_SEQ, A_KV), lambda b, h: (row0 + b, (A_Q + A_KV) // A_KV)),
                  pl.BlockSpec((1, PAST_LEN, A_KV), lambda b, h: (b, 0, 0)),
                  pl.BlockSpec((1, PAST_LEN, A_KV), lambda b, h: (b, 0, 0)),
                  pl.BlockSpec((1, A_HEADS), lambda b, h: (0, 0))],
        out_specs=pl.BlockSpec((DEC_SEQ, gw), lambda b, h: (b, h)),
        out_shape=jax.ShapeDtypeStruct((T_LAT, A_Q), F32),
        compiler_params=_cparams("parallel", "parallel"),
        name="win_attn",
    )(qkv, q_rot, k_rot, qkv, kc, vc, sink.reshape(1, A_HEADS))


GRID_ROWS = DEC_SEQ // GRID_W
NA_BAND = min(NA_ROWS, GRID_ROWS)


NA_REL_ROWS = 2 * NA_ROWS - 1
NA_REL_COLS = 2 * NA_COLS - 1
LANES = 128
HEADS_PER_TILE = LANES // HEAD_DIM


def _na_rel_rows(rpb):
    pad = jnp.zeros((B_HEADS, NA_REL_ROWS, GRID_W - NA_REL_COLS), F32)
    one = jnp.concatenate([rpb, pad], axis=-1)
    nxt = jnp.concatenate([one[:, 1:], jnp.zeros((B_HEADS, 1, GRID_W), F32)], axis=1)
    both = jnp.concatenate([one, nxt], axis=-1)
    return jnp.concatenate([both, jnp.zeros((B_HEADS, 16 - NA_REL_ROWS, LANES), F32)], axis=1)


NA_HEADS_PER_STEP = LANES // HEAD_DIM


def _na_row_groups():
    groups = []
    for r in range(GRID_ROWS):
        rs = min(max(r - NA_ROWS // 2, 0), GRID_ROWS - NA_BAND)
        if groups and groups[-1][2] == rs:
            groups[-1][1] += 1
        else:
            groups.append([r, 1, rs])
    return groups


def _na_attn_body(q_ref, k_ref, v_ref, kc_ref, vc_ref, rel_ref, o_ref):
    scale = HEAD_DIM ** -0.5
    cq = lax.broadcasted_iota(jnp.int32, (GRID_W, LANES), 0)
    kcol = lax.broadcasted_iota(jnp.int32, (GRID_W, LANES), 1) & (GRID_W - 1)
    cs = jnp.clip(cq - NA_COLS // 2, 0, GRID_W - NA_COLS)
    col_ok = (kcol >= cs) & (kcol < cs + NA_COLS)
    kc = kc_ref[0]
    vc = vc_ref[0]
    tiles = {}

    def pair_tile(hh, a):
        if (hh, a) not in tiles:
            x = jnp.broadcast_to(rel_ref[hh, a:a + 1, :], (GRID_W, LANES))
            t = pltpu.roll(x, LANES - (NA_COLS - 1), axis=1, stride=1, stride_axis=0)
            tiles[hh, a] = jnp.where(col_ok, t, MASK_NEG)
        return tiles[hh, a]

    for r0, n_r, rs in _na_row_groups():
        rows = slice(r0 * GRID_W, (r0 + n_r) * GRID_W)
        band = slice(rs * GRID_W, (rs + NA_BAND) * GRID_W)
        q_t, k_t, v_t = q_ref[rows, :], k_ref[band, :], v_ref[band, :]
        head_of_lane = lax.broadcasted_iota(jnp.int32, q_t.shape, 1) >> (HEAD_DIM.bit_length() - 1)
        o = jnp.zeros(q_t.shape, F32)
        for hh in range(NA_HEADS_PER_STEP):
            bias = jnp.concatenate(
                [jnp.concatenate([pair_tile(hh, rs - r + NA_ROWS - 1 + 2 * i) for i in range(NA_BAND // 2)], axis=1)
                 for r in range(r0, r0 + n_r)], axis=0)
            q = jnp.where(head_of_lane == hh, q_t, 0.0)
            s_loc = _dot_nt(q, k_t) * scale + bias
            s_ctx = _dot_nt(q, kc) * scale
            m = jnp.maximum(jnp.max(s_loc, axis=-1, keepdims=True), jnp.max(s_ctx, axis=-1, keepdims=True))
            p_loc = jnp.exp(s_loc - m)
            p_ctx = jnp.exp(s_ctx - m)
            den = jnp.sum(p_loc, axis=-1, keepdims=True) + jnp.sum(p_ctx, axis=-1, keepdims=True)
            o = jnp.where(head_of_lane == hh, (_dot(p_ctx, vc) + _dot(p_loc, v_t)) / den, o)
        o_ref[rows, :] = o


def _na_attn(qkv, kc, vc, rel):
    row0 = T_CTX // DEC_SEQ
    col0 = (A_Q + 2 * A_KV) // LANES
    n_blk = B_W // LANES
    col = lambda j: pl.BlockSpec((DEC_SEQ, LANES), lambda b, p: (row0 + b, col0 + j * n_blk + p))
    cache = pl.BlockSpec((1, PAST_LEN, LANES), lambda b, p: (b, 0, p))
    return pl.pallas_call(
        _na_attn_body,
        grid=(DEC_BATCH, n_blk),
        in_specs=[col(0), col(1), col(2), cache, cache,
                  pl.BlockSpec((NA_HEADS_PER_STEP, 16, LANES), lambda b, p: (p, 0, 0))],
        out_specs=pl.BlockSpec((DEC_SEQ, LANES), lambda b, p: (b, p)),
        out_shape=jax.ShapeDtypeStruct((T_LAT, B_W), F32),
        compiler_params=_cparams("parallel", "parallel"),
        name="na_attn",
    )(qkv, qkv, qkv, kc, vc, rel)


@functools.lru_cache(maxsize=None)
def _dft_mats(L):
    n = 2 * L
    fc = min(L, DFT_CHUNK)
    f = np.arange(L)[:, None]
    t = np.arange(L)[None, :]
    ang = 2.0 * np.pi * ((f * t) % n) / n
    m1 = np.cos(ang)
    m2 = np.sin(ang)
    m2[0, :] = np.where(np.arange(L) % 2 == 0, 1.0, -1.0)
    wgt = np.full((L, 1), 2.0)
    wgt[0, 0] = 1.0
    nch = L // fc
    fwd = np.concatenate([m1.reshape(nch, fc, L), m2.reshape(nch, fc, L)], axis=1)
    inv = np.concatenate([(m1 * wgt / n).reshape(nch, fc, L), (m2 * wgt / n).reshape(nch, fc, L)], axis=1)
    inv = np.transpose(inv, (0, 2, 1))
    return fwd.astype(np.float32), inv.astype(np.float32)


@functools.lru_cache(maxsize=None)
def _filter_consts(L):
    t = np.linspace(0.0, 1.0, L)[:, None]
    bands = (C_EMB - 1) // 2
    ang = (2.0 * math.pi / L) * np.arange(L)[:, None] * np.linspace(1e-4, bands - 1, bands)[None, :]
    z = np.concatenate([t, np.cos(ang), -np.sin(ang)], axis=-1)
    zpad = np.zeros((L, 128))
    zpad[:, :C_EMB] = z
    deltas = np.abs(np.linspace(HYENA_MIN_DECAY, HYENA_MAX_DECAY, C_DIM))
    window = np.exp(-t * deltas[None, :])
    return zpad.astype(np.float32), window.astype(np.float32)


def _filter_body(z_ref, w1_ref, b1_ref, w2_ref, b2_ref, w3_ref, b3_ref, fr_ref, w4_ref, win_ref, fm_ref,
                 hr_ref, g_ref, hq_ref, hs_scr, hd_scr):
    c = pl.program_id(0)
    fc = hr_ref.shape[0]

    @pl.when(c == 0)
    def _():
        fr = fr_ref[...]
        hh = jnp.sin(fr * (_dot_hi(z_ref[...], w1_ref[...]) + b1_ref[...]))
        hh = jnp.sin(fr * (_dot_hi(hh, w2_ref[...]) + b2_ref[...]))
        hh = jnp.sin(fr * (_dot_hi(hh, w3_ref[...]) + b3_ref[...]))
        hh = _dot_hi(hh, w4_ref[...])
        hf = hh[:, :C_DIM] * win_ref[...]
        hb = hh[:, C_DIM:] * win_ref[...]
        hs_scr[...] = hf + hb
        hd_scr[...] = hf - hb

    fm = fm_ref[0]
    hr = _dot_split(fm[:fc], hs_scr[...])
    first = (lax.broadcasted_iota(jnp.int32, (fc, C_DIM), 0) == 0) & (c == 0)
    hr_ref[...] = hr
    g_ref[...] = jnp.where(first, 0.0, _dot_split(fm[fc:], hd_scr[...]))
    hs = hs_scr[...]
    sign = jnp.where((lax.broadcasted_iota(jnp.int32, hs.shape, 0) & 1) == 0, 1.0, -1.0)
    hq_ref[...] = jnp.where(first, jnp.sum(hs * sign, axis=0, keepdims=True), hr)


def _hyena_filter(L, filt):
    w1, b1, w2, b2, w3, b3, freq, w4 = filt
    zpad, window = _filter_consts(L)
    fwd, _ = _dft_mats(L)
    nch, fc2, _ = fwd.shape
    fc = fc2 // 2
    w1p = jnp.pad(w1, ((0, 128 - C_EMB), (0, 0)))
    full = lambda shape: pl.BlockSpec(shape, lambda c: tuple(0 for _ in shape))
    out_spec = pl.BlockSpec((fc, C_DIM), lambda c: (c, 0))
    out_sd = jax.ShapeDtypeStruct((L, C_DIM), F32)
    return pl.pallas_call(
        _filter_body,
        grid=(nch,),
        in_specs=[full((L, 128)), full((128, C_FFN)), full((1, C_FFN)), full((C_FFN, C_FFN)), full((1, C_FFN)),
                  full((C_FFN, C_FFN)), full((1, C_FFN)), full((1, C_FFN)), full((C_FFN, 2 * C_DIM)),
                  full((L, C_DIM)), pl.BlockSpec((1, fc2, L), lambda c: (c, 0, 0))],
        out_specs=[out_spec, out_spec, out_spec],
        out_shape=[out_sd, out_sd, out_sd],
        scratch_shapes=[pltpu.VMEM((L, C_DIM), F32), pltpu.VMEM((L, C_DIM), F32)],
        compiler_params=_cparams("arbitrary"),
        name="hyena_filter",
    )(jnp.asarray(zpad), w1p, b1.reshape(1, C_FFN), w2, b2.reshape(1, C_FFN), w3, b3.reshape(1, C_FFN),
      freq.reshape(1, C_FFN), w4, jnp.asarray(window), jnp.asarray(fwd))


def _hyena_body(u_ref, cw_ref, cb_ref, d_ref, fm_ref, fi_ref, hr_ref, g_ref, hq_ref, y_ref,
                x0_scr, z_scr, acc_scr):
    c = pl.program_id(1)
    L = y_ref.shape[0]
    fc = hr_ref.shape[0]

    @pl.when(c == 0)
    def _():
        row = lax.broadcasted_iota(jnp.int32, (L, C_DIM), 0)

        def short_conv(sec):
            cols = slice(sec * C_DIM, (sec + 1) * C_DIM)
            u = u_ref[:, cols]
            prev = jnp.where(row == 0, 0.0, pltpu.roll(u, 1, axis=0))
            nxt = jnp.where(row == L - 1, 0.0, pltpu.roll(u, L - 1, axis=0))
            return (prev * cw_ref[0:1, cols] + u * cw_ref[1:2, cols] + nxt * cw_ref[2:3, cols]
                    + cb_ref[:, cols])

        x0_scr[...] = short_conv(0)
        z_scr[...] = short_conv(1) * short_conv(2)
        acc_scr[...] = jnp.zeros((L, C_DIM), F32)

    ab = _dot_split(fm_ref[0], z_scr[...])
    a, b = ab[:fc], ab[fc:]
    hr, g, hq = hr_ref[...], g_ref[...], hq_ref[...]
    pq = jnp.concatenate([a * hr - b * g, a * g + b * hq], axis=0)
    acc_scr[...] += _dot_split(fi_ref[0], pq)

    @pl.when(c == pl.num_programs(1) - 1)
    def _():
        y_ref[...] = x0_scr[...] * (acc_scr[...] + z_scr[...] * d_ref[...])


def _hyena(u, row_blk0, n_seq, L, conv_w, conv_b, d_skip, spec):
    hr, g, hq = spec
    fwd, inv = _dft_mats(L)
    nch, fc2, _ = fwd.shape
    fc = fc2 // 2
    u_w = 3 * C_DIM
    return pl.pallas_call(
        _hyena_body,
        grid=(n_seq, nch),
        in_specs=[pl.BlockSpec((L, u_w), lambda b, c: (row_blk0 + b, 0)),
                  pl.BlockSpec((3, u_w), lambda b, c: (0, 0)),
                  pl.BlockSpec((1, u_w), lambda b, c: (0, 0)),
                  pl.BlockSpec((1, C_DIM), lambda b, c: (0, 0)),
                  pl.BlockSpec((1, fc2, L), lambda b, c: (c, 0, 0)),
                  pl.BlockSpec((1, L, fc2), lambda b, c: (c, 0, 0)),
                  pl.BlockSpec((fc, C_DIM), lambda b, c: (c, 0)),
                  pl.BlockSpec((fc, C_DIM), lambda b, c: (c, 0)),
                  pl.BlockSpec((fc, C_DIM), lambda b, c: (c, 0))],
        out_specs=pl.BlockSpec((L, C_DIM), lambda b, c: (b, 0)),
        out_shape=jax.ShapeDtypeStruct((n_seq * L, C_DIM), F32),
        scratch_shapes=[pltpu.VMEM((L, C_DIM), F32)] * 3,
        compiler_params=_cparams("parallel", "arbitrary"),
        name="hyena",
    )(u, conv_w, conv_b.reshape(1, u_w), d_skip.reshape(1, C_DIM), jnp.asarray(fwd), jnp.asarray(inv), hr, g, hq)


def _hgrn_body(q_ref, ff_ref, fb_ref, i_ref, g_ref, lbf_ref, lbb_ref, nd_ref, s0f_ref, s0b_ref,
               o_ref, sf_ref, sb_ref, *, layer):
    L = o_ref.shape[0]
    C = GLA_CHUNK
    S = min(L, GLA_SPAN)
    nc = S // C
    n_span = L // S
    mid = C // 2
    def lower_bound(gm):
        e = jnp.exp(gm - jnp.max(gm, axis=0, keepdims=True))
        p = e / jnp.sum(e, axis=0, keepdims=True)
        return jnp.sum(p[0:layer + 1], axis=0, keepdims=True) - p[0:1]

    def gates(fx, lb):
        f = lb + (1.0 - lb) * jax.nn.sigmoid(fx)
        return 1.0 - f, jnp.log(f)


    chunk_shift = C.bit_length() - 1
    block_shift = D_KDIM.bit_length() - 1
    ti = lax.broadcasted_iota(jnp.int32, (S, S), 0)
    si = lax.broadcasted_iota(jnp.int32, (S, S), 1)
    same_chunk = (ti >> chunk_shift) == (si >> chunk_shift)
    causal = same_chunk & (si <= ti)
    anti = same_chunk & (si >= ti)
    row_chunk = lax.broadcasted_iota(jnp.int32, (S, nc * D_KDIM), 0) >> chunk_shift
    col_chunk = lax.broadcasted_iota(jnp.int32, (S, nc * D_KDIM), 1) >> block_shift
    own_block = row_chunk == col_chunk

    def spread(x):
        return jnp.where(own_block, jnp.concatenate([x] * nc, axis=1), 0.0)

    def chunk_cumsum(mask, lg):
        tri = mask.astype(BF16)
        hi = lg.astype(BF16)
        r1 = lg - hi.astype(F32)
        mid_t = r1.astype(BF16)
        lo = (r1 - mid_t.astype(F32)).astype(BF16)
        dot = lambda t: jnp.dot(tri, t, preferred_element_type=F32)
        return dot(hi) + dot(mid_t) + dot(lo)

    def per_chunk_rows(b, pos):
        return jnp.concatenate([jnp.broadcast_to(b[n * C + pos:n * C + pos + 1], (C, D_KDIM)) for n in range(nc)],
                               axis=0)

    def one_head(q, v, kf, lgf, kb, lgb, st_f, st_b):
        local = []
        for u in range(n_span):
            rows = slice(u * S, (u + 1) * S)
            qs, vs, kfs, kbs = q[rows], v[rows], kf[rows], kb[rows]
            lgs = jnp.concatenate([lgf[rows], lgb[rows]], axis=1)
            pre = chunk_cumsum(causal, lgs)
            b_f = pre[:, :D_KDIM]
            pre_b = pre[:, D_KDIM:]
            b_b = per_chunk_rows(pre_b, C - 1) - pre_b + lgb[rows]
            ref_f, ref_b = per_chunk_rows(b_f, mid), per_chunk_rows(b_b, mid)
            sc = (jnp.where(causal, _dot_nt(qs * jnp.exp(b_f - ref_f), kfs * jnp.exp(ref_f - b_f)), 0.0)
                  + jnp.where(anti, _dot_nt(qs * jnp.exp(b_b - ref_b), kbs * jnp.exp(ref_b - b_b)), 0.0))
            k_out = jnp.concatenate([kfs * jnp.exp(per_chunk_rows(b_f, C - 1) - b_f),
                                     kbs * jnp.exp(per_chunk_rows(b_b, 0) - b_b)], axis=1)
            kv_t = _dot_tn(spread(vs), k_out)
            local.append((_dot(sc, vs), kv_t, b_f, b_b, qs))

        states_f = [[None] * nc for _ in range(n_span)]
        for u in range(n_span):
            _, kv_t, b_f, _, _ = local[u]
            for n in range(nc):
                states_f[u][n] = st_f
                st_f = st_f * jnp.exp(b_f[n * C + C - 1:n * C + C]) + kv_t[n * D_VDIM:(n + 1) * D_VDIM, :D_KDIM]
        states_b = [[None] * nc for _ in range(n_span)]
        for u in reversed(range(n_span)):
            _, kv_t, _, b_b, _ = local[u]
            for n in reversed(range(nc)):
                states_b[u][n] = st_b
                st_b = st_b * jnp.exp(b_b[n * C:n * C + 1]) + kv_t[n * D_VDIM:(n + 1) * D_VDIM, D_KDIM:]

        outs = []
        for u in range(n_span):
            intra, _, b_f, b_b, qs = local[u]
            q_in = jnp.concatenate([spread(qs * jnp.exp(b_f)), spread(qs * jnp.exp(b_b))], axis=1)
            outs.append(intra + _dot_nt(q_in, jnp.concatenate(states_f[u] + states_b[u], axis=1)))
        return (jnp.concatenate(outs, axis=0) if n_span > 1 else outs[0]), st_f, st_b

    for hh in range(o_ref.shape[1] // D_VDIM):
        cols = slice(hh * D_KDIM, (hh + 1) * D_KDIM)
        kf, lgf = gates(ff_ref[:, cols], lower_bound(lbf_ref[:, cols]))
        kb, lgb = gates(fb_ref[:, cols], lower_bound(lbb_ref[:, cols]))
        o, st_f, st_b = one_head(_silu(q_ref[:, cols]), i_ref[:, cols], kf, lgf, kb, lgb,
                                 jnp.transpose(s0f_ref[0, hh]), jnp.transpose(s0b_ref[0, hh]))
        sf_ref[0, hh] = jnp.transpose(st_f)
        sb_ref[0, hh] = jnp.transpose(st_b)
        o_ref[:, cols] = _rms(o, nd_ref[...]) * _silu(g_ref[:, cols])


def _hgrn(u, row_blk0, n_seq, L, lb_fwd, lb_bwd, norm_d, s0f, s0b, layer):
    hps = HGRN_HEADS_PER_STEP
    width = hps * D_KDIM
    col0 = 3 * C_DIM // width
    groups = D_HEADS // hps
    col = lambda j: pl.BlockSpec((L, width), lambda b, h: (row_blk0 + b, col0 + j * groups + h))
    lbs = pl.BlockSpec((DEPTH, width), lambda b, h: (0, h))
    st = pl.BlockSpec((1, hps, D_KDIM, D_VDIM), lambda b, h: (b, h, 0, 0))
    st_sd = jax.ShapeDtypeStruct((n_seq, D_HEADS, D_KDIM, D_VDIM), F32)
    return pl.pallas_call(
        functools.partial(_hgrn_body, layer=layer),
        grid=(n_seq, groups),
        in_specs=[col(0), col(1), col(2), col(3), col(4), lbs, lbs,
                  pl.BlockSpec((1, D_VDIM), lambda b, h: (0, 0)), st, st],
        out_specs=[pl.BlockSpec((L, width), lambda b, h: (b, h)), st, st],
        out_shape=[jax.ShapeDtypeStruct((n_seq * L, D_HEADS * D_VDIM), F32), st_sd, st_sd],
        compiler_params=_cparams("parallel", "parallel"),
        name="hgrn",
    )(u, u, u, u, u, lb_fwd, lb_bwd, norm_d.reshape(1, D_VDIM), s0f, s0b)


def _pack_bf16_pairs(h):
    n = h.shape[1] // 2
    hi = lax.bitcast_convert_type(h[:, :n].astype(BF16).astype(F32), jnp.int32)
    lo = lax.bitcast_convert_type(h[:, n:].astype(BF16).astype(F32), jnp.int32)
    return hi | lax.shift_right_logical(lo, 16)


def _unpack_bf16_pairs(p):
    hi = lax.bitcast_convert_type(p & jnp.int32(-65536), F32).astype(BF16)
    lo = lax.bitcast_convert_type(lax.shift_left(p, 16), F32).astype(BF16)
    return hi, lo


def _outproj_body(*refs, n_x):
    a_refs, b_refs, x_refs = refs[0:2], refs[2:4], refs[4:4 + n_x]
    mod_ref, gf_ref, w_ref, wrh_ref, wrl_ref, rb_ref, x1_ref, h2_ref, chosen_ref, gk_ref, ik_ref = refs[4 + n_x:]
    m = mod_ref[0]
    half = a_refs[0].shape[1]
    out = _dot(_token_tile(a_refs), w_ref[0:half, :]) + _dot(_token_tile(b_refs), w_ref[half:, :])
    x1 = _token_tile(x_refs) + m[:, 2 * D_MODEL:3 * D_MODEL] * out
    x1_ref[...] = x1
    h2 = _rms(x1, gf_ref[...]) * (1.0 + m[:, 4 * D_MODEL:5 * D_MODEL]) + m[:, 3 * D_MODEL:4 * D_MODEL]
    h2_ref[...] = _pack_bf16_pairs(h2)
    h_hi = h2.astype(BF16)
    h_lo = (h2 - h_hi.astype(F32)).astype(BF16)
    logits = _dot_nt(wrh_ref[...], h_hi) + _dot_nt(wrh_ref[...], h_lo) + _dot_nt(wrl_ref[...], h_hi)
    scores = jax.nn.sigmoid(logits)
    work = scores + rb_ref[...]
    expert = lax.broadcasted_iota(jnp.int32, work.shape, 0).astype(F32)
    slot = lax.broadcasted_iota(jnp.int32, (TOP_K, work.shape[1]), 0)
    chosen = [jnp.zeros(work.shape, F32) for _ in range(MOE_REGIONS)]
    gk = jnp.zeros((TOP_K, work.shape[1]), F32)
    ik = jnp.zeros((TOP_K, work.shape[1]), F32)
    for k in range(TOP_K):
        best = jnp.max(work, axis=0, keepdims=True)
        first = jnp.min(jnp.where(work == best, expert, float(N_EXPERTS)), axis=0, keepdims=True)
        hit = expert == first
        chosen[k // K_PER_REGION] = jnp.where(hit, 1.0, chosen[k // K_PER_REGION])
        gk = jnp.where(slot == k, jnp.sum(jnp.where(hit, scores, 0.0), axis=0, keepdims=True), gk)
        ik = jnp.where(slot == k, first, ik)
        work = jnp.where(hit, -jnp.inf, work)
    for r in range(MOE_REGIONS):
        chosen_ref[r] = chosen[r]
    gk_ref[...] = jnp.transpose(gk / jnp.sum(gk, axis=0, keepdims=True) * ROUTE_SCALE)
    ik_ref[...] = ik


def _outproj(a, b, x, mod_l, gain_ffn, w_out, w_router, router_bias):
    half = a[0].shape[1]
    a_specs, a_args = _token_specs(a, half)
    b_specs, b_args = _token_specs(b, half)
    x_specs, x_args = _token_specs(x, D_MODEL)
    wr_t = w_router.T
    wr_hi = wr_t.astype(BF16)
    wr_lo = (wr_t - wr_hi.astype(F32)).astype(BF16)
    return pl.pallas_call(
        functools.partial(_outproj_body, n_x=len(x_args)),
        grid=(T_ALL // TM,),
        in_specs=a_specs + b_specs + x_specs + [
                  pl.BlockSpec((1, 1, N_MOD * D_MODEL), lambda i: (_mod_row(i), 0, 0)),
                  pl.BlockSpec((1, D_MODEL), lambda i: (0, 0)),
                  pl.BlockSpec((2 * half, D_MODEL), lambda i: (0, 0)),
                  pl.BlockSpec((N_EXPERTS, D_MODEL), lambda i: (0, 0)),
                  pl.BlockSpec((N_EXPERTS, D_MODEL), lambda i: (0, 0)),
                  pl.BlockSpec((N_EXPERTS, 1), lambda i: (0, 0))],
        out_specs=[pl.BlockSpec((TM, D_MODEL), lambda i: (i, 0)),
                   pl.BlockSpec((TM, D_MODEL // 2), lambda i: (i, 0)),
                   pl.BlockSpec((MOE_REGIONS, N_EXPERTS, TM), lambda i: (0, 0, i)),
                   pl.BlockSpec((TM, TOP_K), lambda i: (i, 0)),
                   pl.BlockSpec((TOP_K, TM), lambda i: (0, i))],
        out_shape=[jax.ShapeDtypeStruct((T_ALL, D_MODEL), F32),
                   jax.ShapeDtypeStruct((T_ALL, D_MODEL // 2), jnp.int32),
                   jax.ShapeDtypeStruct((MOE_REGIONS, N_EXPERTS, T_ALL), F32),
                   jax.ShapeDtypeStruct((T_ALL, TOP_K), F32),
                   jax.ShapeDtypeStruct((TOP_K, T_ALL), F32)],
        compiler_params=_cparams("parallel"),
        name="outproj_router",
    )(*a_args, *b_args, *x_args, mod_l, gain_ffn.reshape(1, D_MODEL), w_out, wr_hi, wr_lo,
      router_bias.reshape(N_EXPERTS, 1))


def _route_body(chosen_ref, ik_ref, dest_ref, first_ref, count_ref, short_ref, pos_scr):
    n_tiles = T_ALL // TM
    r = lax.broadcasted_iota(jnp.int32, (TM, TM), 0)
    c = lax.broadcasted_iota(jnp.int32, (TM, TM), 1)
    before = (r < c).astype(BF16)

    counts = jnp.zeros((N_EXPERTS, 1), F32)
    for i in range(n_tiles):
        cols = slice(i * TM, (i + 1) * TM)
        m = chosen_ref[0, :, cols]
        pos_scr[:, cols] = jnp.dot(m.astype(BF16), before, preferred_element_type=F32) + counts
        counts = counts + jnp.sum(m, axis=1, keepdims=True)
    padded = jnp.ceil(counts * (1.0 / MOE_BLK)) * MOE_BLK
    ei = lax.broadcasted_iota(jnp.int32, (N_EXPERTS, N_EXPERTS), 0)
    ej = lax.broadcasted_iota(jnp.int32, (N_EXPERTS, N_EXPERTS), 1)
    end = _dot_hi((ej <= ei).astype(F32), jnp.broadcast_to(padded, (N_EXPERTS, LANES)))[:, 0:1]
    start = end - padded

    expert = lax.broadcasted_iota(jnp.int32, (N_EXPERTS, TM), 0).astype(F32)
    slot = lax.broadcasted_iota(jnp.int32, (K_PER_REGION, TM), 0)
    for i in range(n_tiles):
        cols = slice(i * TM, (i + 1) * TM)
        row_of = pos_scr[:, cols] + start
        ik = ik_ref[0, :, cols]
        acc = jnp.zeros((K_PER_REGION, TM), F32)
        for k in range(K_PER_REGION):
            pick = jnp.sum(jnp.where(expert == ik[k:k + 1, :], row_of, 0.0), axis=0, keepdims=True)
            acc = jnp.where(slot == k, pick, acc)
        dest_ref[0, :, cols] = acc.astype(jnp.int32)
    first_ref[0] = jnp.broadcast_to(start * (1.0 / MOE_BLK), (N_EXPERTS, LANES)).astype(jnp.int32)
    count_ref[0] = jnp.broadcast_to(padded * (1.0 / MOE_BLK), (N_EXPERTS, LANES)).astype(jnp.int32)
    in_last = counts - (padded - MOE_BLK)
    short = jnp.where((counts > 0.0) & (in_last <= MOE_BLK // 2), 1.0, 0.0)
    short_ref[0] = jnp.broadcast_to(short, (N_EXPERTS, LANES)).astype(jnp.int32)


def _route(chosen, ik):
    per_region = lambda rows, cols: pl.BlockSpec((1, rows, cols), lambda r: (r, 0, 0))
    table = jax.ShapeDtypeStruct((MOE_REGIONS, N_EXPERTS, LANES), jnp.int32)
    return pl.pallas_call(
        _route_body,
        grid=(MOE_REGIONS,),
        in_specs=[per_region(N_EXPERTS, T_ALL), per_region(K_PER_REGION, T_ALL)],
        out_specs=[per_region(K_PER_REGION, T_ALL)] + [per_region(N_EXPERTS, LANES)] * 3,
        out_shape=[jax.ShapeDtypeStruct((MOE_REGIONS, K_PER_REGION, T_ALL), jnp.int32), table, table, table],
        scratch_shapes=[pltpu.VMEM((N_EXPERTS, T_ALL), F32)],
        compiler_params=_cparams("arbitrary"),
        name="moe_route",
    )(chosen, ik.reshape(MOE_REGIONS, K_PER_REGION, T_ALL))


def _sc_worker_id():
    return lax.axis_index("s") * SC_CORES + lax.axis_index("c")


def _sc_dispatch(h2p, dest):
    n_chunks = T_ALL // DISP_CHUNK
    k_per = dest.shape[0] // DISP_SPLIT
    items_per_worker = n_chunks * DISP_SPLIT // SC_WORKERS
    chunk_stride = SC_WORKERS // DISP_SPLIT
    width = h2p.shape[1]
    mesh = plsc.VectorSubcoreMesh(core_axis_name="c", subcore_axis_name="s")

    @functools.partial(
        pl.kernel, mesh=mesh,
        out_type=jax.ShapeDtypeStruct((MOE_ROWS, width), jnp.int32),
        scratch_types=[pltpu.VMEM((k_per, DISP_CHUNK), jnp.int32), pltpu.VMEM((DISP_CHUNK, width), jnp.int32),
                       pltpu.SemaphoreType.DMA],
    )
    def run(x_hbm, dest_hbm, xs_hbm, idx_v, rows_v, sem):
        wid = _sc_worker_id()
        group = wid % DISP_SPLIT
        for i in range(items_per_worker):
            chunk = i * chunk_stride + wid // DISP_SPLIT
            tokens = pl.ds(pl.multiple_of(chunk * DISP_CHUNK, DISP_CHUNK), DISP_CHUNK)
            pltpu.sync_copy(dest_hbm.at[group, :, tokens], idx_v)
            pltpu.sync_copy(x_hbm.at[tokens], rows_v)
            scatters = [pltpu.make_async_copy(rows_v, xs_hbm.at[idx_v.at[k]], sem) for k in range(k_per)]
            for cp in scatters:
                cp.start()
            for cp in scatters:
                cp.wait()

    return run(h2p, dest.reshape(DISP_SPLIT, k_per, T_ALL))


def _sc_collect(y, dest_flat):
    n_k = dest_flat.shape[0] // T_ALL
    per_worker = T_ALL // SC_WORKERS
    n_chunks = per_worker // COLLECT_CHUNK
    n_steps = n_k * n_chunks
    width = y.shape[1]
    mesh = plsc.VectorSubcoreMesh(core_axis_name="c", subcore_axis_name="s")

    @functools.partial(
        pl.kernel, mesh=mesh,
        out_type=jax.ShapeDtypeStruct((n_k * T_ALL, width), y.dtype),
        scratch_types=[pltpu.VMEM((n_k * per_worker,), jnp.int32),
                       pltpu.VMEM((COLLECT_CHUNK, width), y.dtype), pltpu.VMEM((COLLECT_CHUNK, width), y.dtype),
                       pltpu.SemaphoreType.DMA, pltpu.SemaphoreType.DMA],
    )
    def run(y_hbm, dest_hbm, yg_hbm, idx_v, rows0, rows1, sem0, sem1):
        wid = _sc_worker_id()
        bufs = ((rows0, sem0), (rows1, sem1))
        for k in range(n_k):
            pltpu.sync_copy(dest_hbm.at[pl.ds(k * T_ALL + wid * per_worker, per_worker)],
                            idx_v.at[pl.ds(k * per_worker, per_worker)])

        def gather(step, buf):
            rows, sem = buf
            idx = idx_v.at[pl.ds(pl.multiple_of(step * COLLECT_CHUNK, 8), COLLECT_CHUNK)]
            return pltpu.make_async_copy(y_hbm.at[idx], rows, sem)

        def out_rows(step):
            off = (step // n_chunks) * T_ALL + wid * per_worker + (step % n_chunks) * COLLECT_CHUNK
            return yg_hbm.at[pl.ds(pl.multiple_of(off, 8), COLLECT_CHUNK)]

        gather(0, bufs[0]).start()

        @pl.loop(0, n_steps, step=2)
        def _(base):
            for j in range(2):
                step = base + j

                @pl.when(step + 1 < n_steps)
                def _():
                    gather(step + 1, bufs[1 - j]).start()

                gather(step, bufs[j]).wait()
                pltpu.sync_copy(bufs[j][0], out_rows(step))

    return run(y, dest_flat)


def _expert_body(first_ref, count_ref, short_ref, xs_hbm, wg_ref, wu_ref, wd_ref, y_hbm,
                 wg_bf, wu_bf, wd_bf, x_buf, y_buf, in_sem, out_sem):
    e = pl.program_id(0)
    first = first_ref[e]
    count = count_ref[e]
    n_used = first_ref[N_EXPERTS - 1] + count_ref[N_EXPERTS - 1]
    half = D_MODEL // 2
    wg_bf[...] = wg_ref[0, 0].astype(BF16)
    wu_bf[...] = wu_ref[0, 0].astype(BF16)
    wd_bf[...] = wd_ref[0, 0].astype(BF16)

    def part_rows(g, part, n_parts):
        size = MOE_BLK // n_parts
        return pl.ds(pl.multiple_of(g * MOE_BLK + part * size, size), size), pl.ds(part * size, size)

    def in_copies(g):
        slot = g & (EXPERT_SLOTS - 1)
        out = []
        for part in range(EXPERT_IN_PARTS):
            src, dst = part_rows(g, part, EXPERT_IN_PARTS)
            out.append(pltpu.make_async_copy(xs_hbm.at[src], x_buf.at[slot, dst], in_sem.at[slot]))
        return out

    def out_copies(g):
        slot = g & (EXPERT_SLOTS - 1)
        out = []
        for part in range(EXPERT_OUT_PARTS):
            dst, src = part_rows(g, part, EXPERT_OUT_PARTS)
            out.append(pltpu.make_async_copy(y_buf.at[slot, src], y_hbm.at[dst], out_sem.at[slot]))
        return out

    @pl.when((first == 0) & (count > 0))
    def _():
        for ahead in range(EXPERT_SLOTS - 1):
            @pl.when(ahead < n_used)
            def _():
                for cp in in_copies(ahead):
                    cp.start()

    def block(b, carry):
        g = first + b
        slot = g & (EXPERT_SLOTS - 1)
        for cp in in_copies(g):
            cp.wait()

        @pl.when(g + EXPERT_SLOTS - 1 < n_used)
        def _():
            for cp in in_copies(g + EXPERT_SLOTS - 1):
                cp.start()

        @pl.when(g >= EXPERT_SLOTS)
        def _():
            for cp in out_copies(g - EXPERT_SLOTS):
                cp.wait()

        def ffn(n_rows):
            hi, lo = _unpack_bf16_pairs(x_buf[slot, 0:n_rows])

            def proj(w_bf):
                return (jnp.dot(hi, w_bf[0:half, :], preferred_element_type=F32)
                        + jnp.dot(lo, w_bf[half:, :], preferred_element_type=F32))

            hid = _silu(proj(wg_bf)) * proj(wu_bf)
            y_buf[slot, 0:n_rows] = _pack_bf16_pairs(
                jnp.dot(hid.astype(BF16), wd_bf[...], preferred_element_type=F32))

        short = (b == count - 1) & (short_ref[e] == 1)

        @pl.when(short)
        def _():
            ffn(MOE_BLK // 2)
            y_buf[slot, MOE_BLK // 2:MOE_BLK] = jnp.zeros((MOE_BLK // 2, D_MODEL // 2), jnp.int32)

        @pl.when(jnp.logical_not(short))
        def _():
            ffn(MOE_BLK)

        for cp in out_copies(g):
            cp.start()
        return carry

    lax.fori_loop(0, count, block, 0)

    @pl.when(e == N_EXPERTS - 1)
    def _():
        for back in range(EXPERT_SLOTS, 0, -1):
            @pl.when(n_used >= back)
            def _():
                for cp in out_copies(n_used - back):
                    cp.wait()


EXPERT_SLOTS = 4
EXPERT_IN_PARTS = 2
EXPERT_OUT_PARTS = 4


def _experts(first_blk, n_blk, short_last, xs, layer, w_gate, w_up, w_down):
    w_in = pl.BlockSpec((1, 1, D_MODEL, D_EXPERT), lambda e, *_: (layer, e, 0, 0))
    grid_spec = pltpu.PrefetchScalarGridSpec(
        num_scalar_prefetch=3,
        grid=(N_EXPERTS,),
        in_specs=[pl.BlockSpec(memory_space=pl.ANY), w_in, w_in,
                  pl.BlockSpec((1, 1, D_EXPERT, D_MODEL), lambda e, *_: (layer, e, 0, 0))],
        out_specs=pl.BlockSpec(memory_space=pl.ANY),
        scratch_shapes=[pltpu.VMEM((D_MODEL, D_EXPERT), BF16), pltpu.VMEM((D_MODEL, D_EXPERT), BF16),
                        pltpu.VMEM((D_EXPERT, D_MODEL), BF16),
                        pltpu.VMEM((EXPERT_SLOTS, MOE_BLK, D_MODEL // 2), jnp.int32),
                        pltpu.VMEM((EXPERT_SLOTS, MOE_BLK, D_MODEL // 2), jnp.int32),
                        pltpu.SemaphoreType.DMA((EXPERT_SLOTS,)), pltpu.SemaphoreType.DMA((EXPERT_SLOTS,))],
    )
    return pl.pallas_call(
        _expert_body,
        grid_spec=grid_spec,
        out_shape=jax.ShapeDtypeStruct((MOE_ROWS, D_MODEL // 2), jnp.int32),
        compiler_params=_cparams("arbitrary"),
        name="moe_experts",
    )(first_blk, n_blk, short_last, xs, w_gate, w_up, w_down)


def _combine_body(x1_ref, h2_ref, *refs, final):
    yg_refs = refs[:MOE_REGIONS]
    gk_ref, mod_ref, sg_ref, su_ref, sd_ref, fn_ref, *o_refs = refs[MOE_REGIONS:]
    hi, lo = _unpack_bf16_pairs(h2_ref[...])
    half = D_MODEL // 2

    def proj(w_ref):
        return _dot(hi, w_ref[0:half, :]) + _dot(lo, w_ref[half:, :])

    shared = _dot(_silu(proj(sg_ref)) * proj(su_ref), sd_ref[...])
    acc_hi, acc_lo = shared[:, :half], shared[:, half:]
    gk = gk_ref[...]
    for k in range(TOP_K):
        y_hi, y_lo = _unpack_bf16_pairs(yg_refs[k // K_PER_REGION][k % K_PER_REGION])
        acc_hi = acc_hi + gk[:, k:k + 1] * y_hi.astype(F32)
        acc_lo = acc_lo + gk[:, k:k + 1] * y_lo.astype(F32)
    acc = jnp.concatenate([acc_hi, acc_lo], axis=1)
    m = mod_ref[0]
    y = x1_ref[...] + m[:, 5 * D_MODEL:6 * D_MODEL] * acc
    if not final:
        o_refs[0][...] = y
        return
    y = _rms(y, fn_ref[...])
    is_ctx = pl.program_id(0) < N_CTX_TILES

    @pl.when(is_ctx)
    def _():
        o_refs[0][...] = y

    @pl.when(jnp.logical_not(is_ctx))
    def _():
        o_refs[1][...] = y


def _combine(x1, h2p, yg, gk, mod_l, ws_gate, ws_up, ws_down, final_norm, final):
    tok = lambda shape: pl.BlockSpec(shape, lambda i: (i, 0))
    full = lambda shape: pl.BlockSpec(shape, lambda i: (0, 0))
    if final:
        out_specs, _ = _token_specs((None, None), D_MODEL)
        out_shape = [jax.ShapeDtypeStruct((T_CTX, D_MODEL), F32), jax.ShapeDtypeStruct((T_LAT, D_MODEL), F32)]
    else:
        out_specs = tok((TM, D_MODEL))
        out_shape = jax.ShapeDtypeStruct((T_ALL, D_MODEL), F32)
    return pl.pallas_call(
        functools.partial(_combine_body, final=final),
        grid=(T_ALL // TM,),
        in_specs=[tok((TM, D_MODEL)), tok((TM, D_MODEL // 2))]
                 + [pl.BlockSpec((K_PER_REGION, TM, D_MODEL // 2), lambda i: (0, i, 0))] * MOE_REGIONS
                 + [tok((TM, TOP_K)),
                  pl.BlockSpec((1, 1, N_MOD * D_MODEL), lambda i: (_mod_row(i), 0, 0)),
                  full((D_MODEL, D_EXPERT)), full((D_MODEL, D_EXPERT)), full((D_EXPERT, D_MODEL)),
                  full((1, D_MODEL))],
        out_specs=out_specs,
        out_shape=out_shape,
        compiler_params=_cparams("arbitrary"),
        name="moe_combine",
    )(x1, h2p, *yg, gk, mod_l, ws_gate, ws_up, ws_down, final_norm.reshape(1, D_MODEL))


def _moe(x1, h2p, chosen, gk, ik, mod_l, layer, w_gate, w_up, w_down, ws_gate, ws_up, ws_down, final_norm, final):
    dest, first_blk, n_blk, short_last = _route(chosen, ik)
    yg = []
    for r in range(MOE_REGIONS):
        xs = _sc_dispatch(h2p, dest[r])
        y = _experts(first_blk[r, :, 0], n_blk[r, :, 0], short_last[r, :, 0], xs, layer, w_gate, w_up, w_down)
        yg.append(_sc_collect(y, dest[r].reshape(-1)).reshape(K_PER_REGION, T_ALL, D_MODEL // 2))
    return _combine(x1, h2p, yg, gk, mod_l, ws_gate, ws_up, ws_down, final_norm, final)


def kernel(x_prompt, x_sample, cache_a_k, cache_a_v, cache_b_k, cache_b_v, state_d_fwd, state_d_bwd, c, c_ctx, w_ada, b_ada, norm_mix, norm_ffn, w_in_attn, w_out_attn, sink_a, rpb_b, w_in_rec, w_out_rec, conv_w, conv_b, filt_w1, filt_b1, filt_w2, filt_b2, filt_w3, filt_b3, filt_freq, filt_w4, d_skip, lb_fwd, lb_bwd, norm_d, w_router, router_bias, w_gate, w_up, w_down, ws_gate, ws_up, ws_down, final_norm):
    x = (x_prompt.reshape(T_CTX, D_MODEL), x_sample.reshape(T_LAT, D_MODEL))
    cvec = jnp.concatenate([c_ctx[None, :], c], axis=0)
    c_lanes = jnp.broadcast_to(cvec[:, :, None], (N_CVEC, D_MODEL, LANES))
    mod = [_ada(c_lanes, l, w_ada, b_ada).reshape(CVEC_PAD, 1, N_MOD * D_MODEL) for l in range(DEPTH)]

    new_kv = None
    new_state = None
    for l in range(DEPTH):
        j = l // 2
        final = l == DEPTH - 1
        if l % 2 == 0:
            qkv = _inproj(x, mod[l], norm_mix[l], w_in_attn[j])
            oa_ctx, ob_ctx, *new_kv = _ctx_attn(qkv, sink_a[j])
            new_kv = tuple(new_kv)
            q_rot, k_rot = _rope(qkv)
            cache = lambda t: t[:, j].reshape(DEC_BATCH, PAST_LEN, -1)
            oa_lat = _win_attn(qkv, q_rot, k_rot, cache(cache_a_k), cache(cache_a_v), sink_a[j])
            ob_lat = _na_attn(qkv, cache(cache_b_k), cache(cache_b_v), _na_rel_rows(rpb_b[j]))
            mix_a = (oa_ctx, oa_lat)
            mix_b = (ob_ctx, ob_lat)
            w_out = w_out_attn[j]
        else:
            u = _inproj(x, mod[l], norm_mix[l], w_in_rec[j])
            filt = (filt_w1[j], filt_b1[j], filt_w2[j], filt_b2[j], filt_w3[j], filt_b3[j], filt_freq[j],
                    filt_w4[j])
            y_ctx = _hyena(u, 0, BATCH, SEQ, conv_w[j], conv_b[j], d_skip[j], _hyena_filter(SEQ, filt))
            y_lat = _hyena(u, T_CTX // DEC_SEQ, DEC_BATCH, DEC_SEQ, conv_w[j], conv_b[j], d_skip[j],
                           _hyena_filter(DEC_SEQ, filt))
            zeros = jnp.zeros((BATCH, D_HEADS, D_KDIM, D_VDIM), F32)
            o_ctx, s_f, s_b = _hgrn(u, 0, BATCH, SEQ, lb_fwd, lb_bwd, norm_d[j], zeros, zeros, l)
            o_lat, _, _ = _hgrn(u, T_CTX // DEC_SEQ, DEC_BATCH, DEC_SEQ, lb_fwd, lb_bwd, norm_d[j],
                                state_d_fwd[:, j], state_d_bwd[:, j], l)
            new_state = (s_f[:, None], s_b[:, None])
            mix_a = (y_ctx, y_lat)
            mix_b = (o_ctx, o_lat)
            w_out = w_out_rec[j]
        x1, h2p, chosen, gk, ik = _outproj(mix_a, mix_b, x, mod[l], norm_ffn[l], w_out, w_router[l],
                                           router_bias[l])
        x = _moe(x1, h2p, chosen, gk, ik, mod[l], l, w_gate, w_up, w_down, ws_gate[l], ws_up[l],
                 ws_down[l], final_norm, final)

    y_prompt = x[0].reshape(BATCH, SEQ, D_MODEL)
    y_sample = x[1].reshape(DEC_BATCH, DEC_SEQ, D_MODEL)
    return (y_prompt, y_sample) + new_kv + new_state
```

```python
import functools
import math

import numpy as np
import jax
import jax.numpy as jnp
from jax import lax
from jax.experimental import pallas as pl
from jax.experimental.pallas import tpu as pltpu
from jax.experimental.pallas import tpu_sc as plsc

F32 = jnp.float32
BF16 = jnp.bfloat16
HI = lax.Precision.HIGHEST

D_MODEL = 1024
BATCH = 16
SEQ = 256
DEPTH = 2
DEC_BATCH = 2
DEC_SEQ = 1024
PAST_LEN = 512
GRID_W = 64
HEAD_DIM = 64
N_MOD = 6
RMS_EPS = 1e-6
A_HEADS = 8
A_KV_HEADS = 2
A_GROUP = A_HEADS // A_KV_HEADS
WINDOW = 128
ROPE_BASE = 10000.0
B_HEADS = 8
NA_ROWS = 8
NA_COLS = 16
C_DIM = 512
C_EMB = 33
C_FFN = 64
HYENA_MIN_DECAY = math.log(1e-2) / 1.5
HYENA_MAX_DECAY = math.log(1e-2) / 0.3
D_KDIM = 128
D_VDIM = 128
D_HEADS = 4
N_EXPERTS = 64
TOP_K = 8
D_EXPERT = 256
ROUTE_SCALE = 2.5
A_Q = A_HEADS * HEAD_DIM
A_KV = A_KV_HEADS * HEAD_DIM
B_W = B_HEADS * HEAD_DIM
ATTN_IN = A_Q + 2 * A_KV + 3 * B_W
REC_IN = 3 * C_DIM + 5 * D_HEADS * D_KDIM

T_CTX = BATCH * SEQ
T_LAT = DEC_BATCH * DEC_SEQ
T_ALL = T_CTX + T_LAT
N_CVEC = 1 + DEC_BATCH
CVEC_PAD = 8
TM = 512
MASK_NEG = -1e30
GLA_CHUNK = 64
GLA_SPAN = 256
HGRN_HEADS_PER_STEP = 4
DFT_CHUNK = 256
MOE_BLK = 512
MOE_REGIONS = 1
K_PER_REGION = TOP_K // MOE_REGIONS
MOE_NBLK = -(-(T_ALL * K_PER_REGION + N_EXPERTS * (MOE_BLK - 1)) // MOE_BLK)
MOE_ROWS = MOE_NBLK * MOE_BLK
SC_CORES = 2
SC_SUBCORES = 16
SC_WORKERS = SC_CORES * SC_SUBCORES
DISP_CHUNK = 128
DISP_SPLIT = 2
COLLECT_CHUNK = 64
VMEM_LIMIT = 56 * 1024 * 1024


def _cparams(*sem):
    return pltpu.CompilerParams(dimension_semantics=sem, vmem_limit_bytes=VMEM_LIMIT)


def _mod_row(i):
    return jnp.where(i < T_CTX // TM, 0, 1 + (i - T_CTX // TM) // (DEC_SEQ // TM))


def _dot(a, b):
    return jnp.dot(a.astype(BF16), b.astype(BF16), preferred_element_type=F32)


def _dot_nt(a, b):
    return lax.dot_general(a.astype(BF16), b.astype(BF16), (((1,), (1,)), ((), ())),
                           preferred_element_type=F32)


def _dot_tn(a, b):
    return lax.dot_general(a.astype(BF16), b.astype(BF16), (((0,), (0,)), ((), ())),
                           preferred_element_type=F32)


def _dot_hi(a, b):
    return jnp.dot(a, b, precision=HI, preferred_element_type=F32)


def _split_bf16(x):
    hi = x.astype(BF16)
    return hi, (x - hi.astype(F32)).astype(BF16)


def _dot_split(a, b):
    a_hi, a_lo = _split_bf16(a)
    b_hi, b_lo = _split_bf16(b)
    dot = lambda x, y: jnp.dot(x, y, preferred_element_type=F32)
    return dot(a_hi, b_hi) + dot(a_hi, b_lo) + dot(a_lo, b_hi)


def _silu(x):
    return x * jax.nn.sigmoid(x)


def _rms(x, g):
    return x * lax.rsqrt(jnp.mean(x * x, axis=-1, keepdims=True) + RMS_EPS) * g


ADA_TN = 1536
ADA_UNROLL = 4


def _ada_body(cb_ref, w_ref, b_ref, o_ref):
    tn = o_ref.shape[-1]
    n_slab = tn // LANES

    def step(k8, accs):
        r0 = pl.multiple_of(k8 * 8, 8)
        sk = [_silu(cb_ref[j, pl.ds(r0, 8), :]) for j in range(N_CVEC)]
        out = []
        for s in range(n_slab):
            wk = w_ref[0, pl.ds(r0, 8), s * LANES:(s + 1) * LANES]
            out.extend(accs[s * N_CVEC + j] + wk * sk[j] for j in range(N_CVEC))
        return tuple(out)

    accs = lax.fori_loop(0, D_MODEL // 8, step,
                         tuple(jnp.zeros((8, LANES), F32) for _ in range(n_slab * N_CVEC)), unroll=ADA_UNROLL)
    o_ref[0] = jnp.zeros((CVEC_PAD, tn), F32)
    for s in range(n_slab):
        for j in range(N_CVEC):
            o_ref[0, j:j + 1, s * LANES:(s + 1) * LANES] = (
                jnp.sum(accs[s * N_CVEC + j], axis=0, keepdims=True) + b_ref[0, :, s * LANES:(s + 1) * LANES])


def _ada(c_lanes, layer, w_ada, b_ada):
    n_out = N_MOD * D_MODEL
    return pl.pallas_call(
        _ada_body,
        grid=(n_out // ADA_TN,),
        in_specs=[pl.BlockSpec((N_CVEC, D_MODEL, LANES), lambda n: (0, 0, 0)),
                  pl.BlockSpec((1, D_MODEL, ADA_TN), lambda n: (layer, 0, n)),
                  pl.BlockSpec((1, 1, ADA_TN), lambda n: (layer, 0, n))],
        out_specs=pl.BlockSpec((1, CVEC_PAD, ADA_TN), lambda n: (0, 0, n)),
        out_shape=jax.ShapeDtypeStruct((1, CVEC_PAD, n_out), F32),
        compiler_params=_cparams("parallel"),
        name="ada",
    )(c_lanes, w_ada, b_ada.reshape(DEPTH, 1, n_out))


N_CTX_TILES = T_CTX // TM


def _token_specs(x, width):
    if not isinstance(x, tuple):
        return [pl.BlockSpec((TM, width), lambda i: (i, 0))], (x,)
    return ([pl.BlockSpec((TM, width), lambda i: (jnp.minimum(i, N_CTX_TILES - 1), 0)),
             pl.BlockSpec((TM, width), lambda i: (jnp.maximum(i - N_CTX_TILES, 0), 0))], x)


def _token_tile(refs):
    if len(refs) == 1:
        return refs[0][...]
    return jnp.where(pl.program_id(0) < N_CTX_TILES, refs[0][...], refs[1][...])


def _inproj_body(*refs, n_x):
    x_refs, (mod_ref, g_ref, w_ref, o_ref, w_bf) = refs[:n_x], refs[n_x:]

    @pl.when(pl.program_id(0) == 0)
    def _():
        w_bf[...] = w_ref[...].astype(BF16)

    m = mod_ref[0]
    h = _rms(_token_tile(x_refs), g_ref[...]) * (1.0 + m[:, D_MODEL:2 * D_MODEL]) + m[:, 0:D_MODEL]
    o_ref[...] = _dot(h, w_bf[...])


def _inproj(x, mod_l, gain, w):
    n = w.shape[1]
    x_specs, x_args = _token_specs(x, D_MODEL)
    return pl.pallas_call(
        functools.partial(_inproj_body, n_x=len(x_args)),
        grid=(T_ALL // TM,),
        in_specs=x_specs + [pl.BlockSpec((1, 1, N_MOD * D_MODEL), lambda i: (_mod_row(i), 0, 0)),
                            pl.BlockSpec((1, D_MODEL), lambda i: (0, 0)),
                            pl.BlockSpec((D_MODEL, n), lambda i: (0, 0), pipeline_mode=pl.Buffered(1))],
        out_specs=pl.BlockSpec((TM, n), lambda i: (i, 0)),
        out_shape=jax.ShapeDtypeStruct((T_ALL, n), F32),
        scratch_shapes=[pltpu.VMEM((D_MODEL, n), BF16)],
        compiler_params=_cparams("arbitrary"),
        name="inproj",
    )(*x_args, mod_l, gain.reshape(1, D_MODEL), w)


def _head_cols(h):
    return slice(h * HEAD_DIM, (h + 1) * HEAD_DIM)


def _group_rows(ref, rows, first_col, sink_ref, hk):
    n = rows.stop - rows.start
    q = jnp.concatenate([ref[rows, first_col + g * HEAD_DIM:first_col + (g + 1) * HEAD_DIM]
                         for g in range(A_GROUP)], axis=0)
    sink = jnp.concatenate([jnp.broadcast_to(sink_ref[:, hk * A_GROUP + g:hk * A_GROUP + g + 1], (n, 1))
                            for g in range(A_GROUP)], axis=0)
    return q, sink


def _ctx_attn_body(qkv_ref, sink_ref, oa_ref, ob_ref, ak_ref, av_ref, bk_ref, bv_ref):
    scale = HEAD_DIM ** -0.5
    lane = lax.broadcasted_iota(jnp.int32, (SEQ, LANES), 1)
    in_half = [lane < HEAD_DIM, lane >= HEAD_DIM]

    def attend(q, k, v, sink):
        s = _dot_nt(q, k) * scale
        m = jnp.max(s, axis=-1, keepdims=True)
        if sink is not None:
            m = jnp.maximum(m, sink)
        p = jnp.exp(s - m)
        den = jnp.sum(p, axis=-1, keepdims=True)
        if sink is not None:
            den = den + jnp.exp(sink - m)
        return _dot(p, v) / den

    def tile(first_col, t):
        return qkv_ref[:, first_col + t * LANES:first_col + (t + 1) * LANES]

    base = A_Q + 2 * A_KV
    for hk in range(A_KV_HEADS):
        dst = pl.ds(hk, SEQ, stride=A_KV_HEADS)
        ak_ref[0, dst, :] = qkv_ref[:, A_Q + hk * HEAD_DIM:A_Q + (hk + 1) * HEAD_DIM]
        av_ref[0, dst, :] = qkv_ref[:, A_Q + A_KV + hk * HEAD_DIM:A_Q + A_KV + (hk + 1) * HEAD_DIM]
    for h in range(B_HEADS):
        dst = pl.ds(h, SEQ, stride=B_HEADS)
        bk_ref[0, dst, :] = qkv_ref[:, base + B_W + h * HEAD_DIM:base + B_W + (h + 1) * HEAD_DIM]
        bv_ref[0, dst, :] = qkv_ref[:, base + 2 * B_W + h * HEAD_DIM:base + 2 * B_W + (h + 1) * HEAD_DIM]

    k_t, v_t = tile(A_Q, 0), tile(A_Q + A_KV, 0)
    k_sw, v_sw = pltpu.roll(k_t, HEAD_DIM, axis=1), pltpu.roll(v_t, HEAD_DIM, axis=1)
    tiles_per_kv = A_GROUP // HEADS_PER_TILE
    for hk in range(A_KV_HEADS):
        q_tiles = [tile(0, hk * tiles_per_kv + j) for j in range(tiles_per_kv)]
        halves = []
        for p in range(HEADS_PER_TILE):
            q = jnp.concatenate([jnp.where(in_half[p], qt, 0.0) for qt in q_tiles], axis=0)
            heads = [(hk * tiles_per_kv + j) * HEADS_PER_TILE + p for j in range(tiles_per_kv)]
            sink = jnp.concatenate([jnp.broadcast_to(sink_ref[:, h:h + 1], (SEQ, 1)) for h in heads], axis=0)
            halves.append(attend(q, k_t if p == hk else k_sw, v_t if p == hk else v_sw, sink))
        first_half = lax.broadcasted_iota(jnp.int32, halves[0].shape, 1) < HEAD_DIM
        o = jnp.where(first_half, halves[0], halves[1])
        for j in range(tiles_per_kv):
            t = hk * tiles_per_kv + j
            oa_ref[:, t * LANES:(t + 1) * LANES] = o[j * SEQ:(j + 1) * SEQ]

    for t in range(B_HEADS // HEADS_PER_TILE):
        q_t, k_b, v_b = tile(base, t), tile(base + B_W, t), tile(base + 2 * B_W, t)
        halves = [attend(jnp.where(in_half[p], q_t, 0.0), k_b, v_b, None) for p in range(HEADS_PER_TILE)]
        ob_ref[:, t * LANES:(t + 1) * LANES] = jnp.where(in_half[0], halves[0], halves[1])


def _ctx_attn(qkv, sink):
    kv_spec = lambda heads: pl.BlockSpec((1, SEQ * heads, HEAD_DIM), lambda b: (b, 0, 0))
    kv_sd = lambda heads: jax.ShapeDtypeStruct((BATCH, SEQ * heads, HEAD_DIM), F32)
    outs = pl.pallas_call(
        _ctx_attn_body,
        grid=(BATCH,),
        in_specs=[pl.BlockSpec((SEQ, ATTN_IN), lambda b: (b, 0)),
                  pl.BlockSpec((1, A_HEADS), lambda b: (0, 0))],
        out_specs=[pl.BlockSpec((SEQ, A_Q), lambda b: (b, 0)), pl.BlockSpec((SEQ, B_W), lambda b: (b, 0)),
                   kv_spec(A_KV_HEADS), kv_spec(A_KV_HEADS), kv_spec(B_HEADS), kv_spec(B_HEADS)],
        out_shape=[jax.ShapeDtypeStruct((T_CTX, A_Q), F32), jax.ShapeDtypeStruct((T_CTX, B_W), F32),
                   kv_sd(A_KV_HEADS), kv_sd(A_KV_HEADS), kv_sd(B_HEADS), kv_sd(B_HEADS)],
        compiler_params=_cparams("parallel"),
        name="ctx_attn",
    )(qkv, sink.reshape(1, A_HEADS))
    caches = [t.reshape(BATCH, 1, SEQ, -1, HEAD_DIM) for t in outs[2:]]
    return outs[0], outs[1], *caches


@functools.lru_cache(maxsize=None)
def _rope_tables(width):
    half = HEAD_DIM // 2
    t = np.arange(DEC_SEQ)
    inv = ROPE_BASE ** (-np.arange(0, half, 2, dtype=np.float64) / half)
    ang_r = (t // GRID_W)[:, None] * inv[None, :]
    ang_c = (t % GRID_W)[:, None] * inv[None, :]
    cos = np.concatenate([np.cos(ang_r)] * 2 + [np.cos(ang_c)] * 2, axis=-1)
    sin = np.concatenate([-np.sin(ang_r), np.sin(ang_r), -np.sin(ang_c), np.sin(ang_c)], axis=-1)
    reps = width // HEAD_DIM
    return (np.tile(cos, (1, reps)).astype(np.float32), np.tile(sin, (1, reps)).astype(np.float32))


def _rope_body(q_ref, k_ref, cq_ref, sq_ref, ck_ref, sk_ref, qo_ref, ko_ref):
    quarter = HEAD_DIM // 4

    def rot(x, cos, sin):
        w = x.shape[-1]
        lane = lax.broadcasted_iota(jnp.int32, x.shape, 1)
        fwd = pltpu.roll(x, w - quarter, axis=1)
        bwd = pltpu.roll(x, quarter, axis=1)
        partner = jnp.where((lane & (2 * quarter - 1)) < quarter, fwd, bwd)
        return x * cos + partner * sin

    qo_ref[...] = rot(q_ref[...], cq_ref[...], sq_ref[...])
    ko_ref[...] = rot(k_ref[...], ck_ref[...], sk_ref[...])


def _rope(qkv):
    cq, sq = _rope_tables(A_Q)
    ck, sk = _rope_tables(A_KV)
    tab = lambda w: pl.BlockSpec((DEC_SEQ, w), lambda b: (0, 0))
    row0 = T_CTX // DEC_SEQ
    return pl.pallas_call(
        _rope_body,
        grid=(DEC_BATCH,),
        in_specs=[pl.BlockSpec((DEC_SEQ, A_Q), lambda b: (row0 + b, 0)),
                  pl.BlockSpec((DEC_SEQ, A_KV), lambda b: (row0 + b, A_Q // A_KV)),
                  tab(A_Q), tab(A_Q), tab(A_KV), tab(A_KV)],
        out_specs=[pl.BlockSpec((DEC_SEQ, A_Q), lambda b: (b, 0)),
                   pl.BlockSpec((DEC_SEQ, A_KV), lambda b: (b, 0))],
        out_shape=[jax.ShapeDtypeStruct((T_LAT, A_Q), F32), jax.ShapeDtypeStruct((T_LAT, A_KV), F32)],
        compiler_params=_cparams("parallel"),
        name="rope",
    )(qkv, qkv, jnp.asarray(cq), jnp.asarray(sq), jnp.asarray(ck), jnp.asarray(sk))


WIN_QB = 256


def _pick_head(x, h, n_heads):
    out = x[:, _head_cols(0)]
    for i in range(1, n_heads):
        out = jnp.where(h == i, x[:, _head_cols(i)], out)
    return out


def _win_attn_body(qraw_ref, qrot_ref, krot_ref, v_ref, kc_ref, vc_ref, sink_ref, o_ref):
    scale = HEAD_DIM ** -0.5
    hk = pl.program_id(1)
    tiles = A_GROUP // HEADS_PER_TILE

    def kv_in_half(x):
        swapped = pltpu.roll(x, HEAD_DIM, axis=1)
        return [jnp.where(hk == p, x, swapped) for p in range(HEADS_PER_TILE)]

    k, v, kc, vc = kv_in_half(krot_ref[...]), kv_in_half(v_ref[...]), kv_in_half(kc_ref[0]), kv_in_half(vc_ref[0])
    head_lane = lax.broadcasted_iota(jnp.int32, (1, A_HEADS), 1)

    def sink_rows(p):
        heads = [hk * A_GROUP + j * HEADS_PER_TILE + p for j in range(tiles)]
        vals = [jnp.sum(jnp.where(head_lane == h, sink_ref[...], 0.0), axis=-1, keepdims=True) for h in heads]
        return jnp.concatenate([jnp.broadcast_to(s, (WIN_QB, 1)) for s in vals], axis=0)

    sinks = [sink_rows(p) for p in range(HEADS_PER_TILE)]
    lane = lax.broadcasted_iota(jnp.int32, (tiles * WIN_QB, LANES), 1)
    in_half = [lane < HEAD_DIM, lane >= HEAD_DIM]
    for qb in range(DEC_SEQ // WIN_QB):
        q0 = qb * WIN_QB
        rows = slice(q0, q0 + WIN_QB)
        lo = max(0, q0 - WINDOW)
        hi = min(DEC_SEQ, q0 + WIN_QB + WINDOW)
        q_rot = jnp.concatenate([qrot_ref[rows, j * LANES:(j + 1) * LANES] for j in range(tiles)], axis=0)
        q_raw = jnp.concatenate([qraw_ref[rows, j * LANES:(j + 1) * LANES] for j in range(tiles)], axis=0)
        halves = []
        for p in range(HEADS_PER_TILE):
            s_loc = _dot_nt(jnp.where(in_half[p], q_rot, 0.0), k[p][lo:hi]) * scale
            qpos = q0 + (lax.broadcasted_iota(jnp.int32, s_loc.shape, 0) & (WIN_QB - 1))
            kpos = lo + lax.broadcasted_iota(jnp.int32, s_loc.shape, 1)
            s_loc = jnp.where(jnp.abs(kpos - qpos) <= WINDOW, s_loc, MASK_NEG)
            s_ctx = _dot_nt(jnp.where(in_half[p], q_raw, 0.0), kc[p]) * scale
            m = jnp.maximum(jnp.maximum(jnp.max(s_loc, axis=-1, keepdims=True),
                                        jnp.max(s_ctx, axis=-1, keepdims=True)), sinks[p])
            p_loc = jnp.exp(s_loc - m)
            p_ctx = jnp.exp(s_ctx - m)
            den = (jnp.sum(p_loc, axis=-1, keepdims=True) + jnp.sum(p_ctx, axis=-1, keepdims=True)
                   + jnp.exp(sinks[p] - m))
            halves.append((_dot(p_ctx, vc[p]) + _dot(p_loc, v[p][lo:hi])) / den)
        o = jnp.where(in_half[0], halves[0], halves[1])
        for j in range(tiles):
            o_ref[rows, j * LANES:(j + 1) * LANES] = o[j * WIN_QB:(j + 1) * WIN_QB]


def _win_attn(qkv, q_rot, k_rot, kc, vc, sink):
    row0 = T_CTX // DEC_SEQ
    gw = A_GROUP * HEAD_DIM
    return pl.pallas_call(
        _win_attn_body,
        grid=(DEC_BATCH, A_KV_HEADS),
        in_specs=[pl.BlockSpec((DEC_SEQ, gw), lambda b, h: (row0 + b, h)),
                  pl.BlockSpec((DEC_SEQ, gw), lambda b, h: (b, h)),
                  pl.BlockSpec((DEC_SEQ, A_KV), lambda b, h: (b, 0)),
                  pl.BlockSpec((DEC_SEQ, A_KV), lambda b, h: (row0 + b, (A_Q + A_KV) // A_KV)),
                  pl.BlockSpec((1, PAST_LEN, A_KV), lambda b, h: (b, 0, 0)),
                  pl.BlockSpec((1, PAST_LEN, A_KV), lambda b, h: (b, 0, 0)),
                  pl.BlockSpec((1, A_HEADS), lambda b, h: (0, 0))],
        out_specs=pl.BlockSpec((DEC_SEQ, gw), lambda b, h: (b, h)),
        out_shape=jax.ShapeDtypeStruct((T_LAT, A_Q), F32),
        compiler_params=_cparams("parallel", "parallel"),
        name="win_attn",
    )(qkv, q_rot, k_rot, qkv, kc, vc, sink.reshape(1, A_HEADS))


GRID_ROWS = DEC_SEQ // GRID_W
NA_BAND = min(NA_ROWS, GRID_ROWS)


NA_REL_ROWS = 2 * NA_ROWS - 1
NA_REL_COLS = 2 * NA_COLS - 1
LANES = 128
HEADS_PER_TILE = LANES // HEAD_DIM


def _na_rel_rows(rpb):
    pad = jnp.zeros((B_HEADS, NA_REL_ROWS, GRID_W - NA_REL_COLS), F32)
    one = jnp.concatenate([rpb, pad], axis=-1)
    nxt = jnp.concatenate([one[:, 1:], jnp.zeros((B_HEADS, 1, GRID_W), F32)], axis=1)
    both = jnp.concatenate([one, nxt], axis=-1)
    return jnp.concatenate([both, jnp.zeros((B_HEADS, 16 - NA_REL_ROWS, LANES), F32)], axis=1)


NA_HEADS_PER_STEP = LANES // HEAD_DIM


def _na_row_groups():
    groups = []
    for r in range(GRID_ROWS):
        rs = min(max(r - NA_ROWS // 2, 0), GRID_ROWS - NA_BAND)
        if groups and groups[-1][2] == rs:
            groups[-1][1] += 1
        else:
            groups.append([r, 1, rs])
    return groups


def _na_attn_body(q_ref, k_ref, v_ref, kc_ref, vc_ref, rel_ref, o_ref):
    scale = HEAD_DIM ** -0.5
    cq = lax.broadcasted_iota(jnp.int32, (GRID_W, LANES), 0)
    kcol = lax.broadcasted_iota(jnp.int32, (GRID_W, LANES), 1) & (GRID_W - 1)
    cs = jnp.clip(cq - NA_COLS // 2, 0, GRID_W - NA_COLS)
    col_ok = (kcol >= cs) & (kcol < cs + NA_COLS)
    kc = kc_ref[0]
    vc = vc_ref[0]
    tiles = {}

    def pair_tile(hh, a):
        if (hh, a) not in tiles:
            x = jnp.broadcast_to(rel_ref[hh, a:a + 1, :], (GRID_W, LANES))
            t = pltpu.roll(x, LANES - (NA_COLS - 1), axis=1, stride=1, stride_axis=0)
            tiles[hh, a] = jnp.where(col_ok, t, MASK_NEG)
        return tiles[hh, a]

    for r0, n_r, rs in _na_row_groups():
        rows = slice(r0 * GRID_W, (r0 + n_r) * GRID_W)
        band = slice(rs * GRID_W, (rs + NA_BAND) * GRID_W)
        q_t, k_t, v_t = q_ref[rows, :], k_ref[band, :], v_ref[band, :]
        head_of_lane = lax.broadcasted_iota(jnp.int32, q_t.shape, 1) >> (HEAD_DIM.bit_length() - 1)
        o = jnp.zeros(q_t.shape, F32)
        for hh in range(NA_HEADS_PER_STEP):
            bias = jnp.concatenate(
                [jnp.concatenate([pair_tile(hh, rs - r + NA_ROWS - 1 + 2 * i) for i in range(NA_BAND // 2)], axis=1)
                 for r in range(r0, r0 + n_r)], axis=0)
            q = jnp.where(head_of_lane == hh, q_t, 0.0)
            s_loc = _dot_nt(q, k_t) * scale + bias
            s_ctx = _dot_nt(q, kc) * scale
            m = jnp.maximum(jnp.max(s_loc, axis=-1, keepdims=True), jnp.max(s_ctx, axis=-1, keepdims=True))
            p_loc = jnp.exp(s_loc - m)
            p_ctx = jnp.exp(s_ctx - m)
            den = jnp.sum(p_loc, axis=-1, keepdims=True) + jnp.sum(p_ctx, axis=-1, keepdims=True)
            o = jnp.where(head_of_lane == hh, (_dot(p_ctx, vc) + _dot(p_loc, v_t)) / den, o)
        o_ref[rows, :] = o


def _na_attn(qkv, kc, vc, rel):
    row0 = T_CTX // DEC_SEQ
    col0 = (A_Q + 2 * A_KV) // LANES
    n_blk = B_W // LANES
    col = lambda j: pl.BlockSpec((DEC_SEQ, LANES), lambda b, p: (row0 + b, col0 + j * n_blk + p))
    cache = pl.BlockSpec((1, PAST_LEN, LANES), lambda b, p: (b, 0, p))
    return pl.pallas_call(
        _na_attn_body,
        grid=(DEC_BATCH, n_blk),
        in_specs=[col(0), col(1), col(2), cache, cache,
                  pl.BlockSpec((NA_HEADS_PER_STEP, 16, LANES), lambda b, p: (p, 0, 0))],
        out_specs=pl.BlockSpec((DEC_SEQ, LANES), lambda b, p: (b, p)),
        out_shape=jax.ShapeDtypeStruct((T_LAT, B_W), F32),
        compiler_params=_cparams("parallel", "parallel"),
        name="na_attn",
    )(qkv, qkv, qkv, kc, vc, rel)


@functools.lru_cache(maxsize=None)
def _dft_mats(L):
    n = 2 * L
    fc = min(L, DFT_CHUNK)
    f = np.arange(L)[:, None]
    t = np.arange(L)[None, :]
    ang = 2.0 * np.pi * ((f * t) % n) / n
    m1 = np.cos(ang)
    m2 = np.sin(ang)
    m2[0, :] = np.where(np.arange(L) % 2 == 0, 1.0, -1.0)
    wgt = np.full((L, 1), 2.0)
    wgt[0, 0] = 1.0
    nch = L // fc
    fwd = np.concatenate([m1.reshape(nch, fc, L), m2.reshape(nch, fc, L)], axis=1)
    inv = np.concatenate([(m1 * wgt / n).reshape(nch, fc, L), (m2 * wgt / n).reshape(nch, fc, L)], axis=1)
    inv = np.transpose(inv, (0, 2, 1))
    return fwd.astype(np.float32), inv.astype(np.float32)


@functools.lru_cache(maxsize=None)
def _filter_consts(L):
    t = np.linspace(0.0, 1.0, L)[:, None]
    bands = (C_EMB - 1) // 2
    ang = (2.0 * math.pi / L) * np.arange(L)[:, None] * np.linspace(1e-4, bands - 1, bands)[None, :]
    z = np.concatenate([t, np.cos(ang), -np.sin(ang)], axis=-1)
    zpad = np.zeros((L, 128))
    zpad[:, :C_EMB] = z
    deltas = np.abs(np.linspace(HYENA_MIN_DECAY, HYENA_MAX_DECAY, C_DIM))
    window = np.exp(-t * deltas[None, :])
    return zpad.astype(np.float32), window.astype(np.float32)


def _filter_body(z_ref, w1_ref, b1_ref, w2_ref, b2_ref, w3_ref, b3_ref, fr_ref, w4_ref, win_ref, fm_ref,
                 hr_ref, g_ref, hq_ref, hs_scr, hd_scr):
    c = pl.program_id(0)
    fc = hr_ref.shape[0]

    @pl.when(c == 0)
    def _():
        fr = fr_ref[...]
        hh = jnp.sin(fr * (_dot_hi(z_ref[...], w1_ref[...]) + b1_ref[...]))
        hh = jnp.sin(fr * (_dot_hi(hh, w2_ref[...]) + b2_ref[...]))
        hh = jnp.sin(fr * (_dot_hi(hh, w3_ref[...]) + b3_ref[...]))
        hh = _dot_hi(hh, w4_ref[...])
        hf = hh[:, :C_DIM] * win_ref[...]
        hb = hh[:, C_DIM:] * win_ref[...]
        hs_scr[...] = hf + hb
        hd_scr[...] = hf - hb

    fm = fm_ref[0]
    hr = _dot_split(fm[:fc], hs_scr[...])
    first = (lax.broadcasted_iota(jnp.int32, (fc, C_DIM), 0) == 0) & (c == 0)
    hr_ref[...] = hr
    g_ref[...] = jnp.where(first, 0.0, _dot_split(fm[fc:], hd_scr[...]))
    hs = hs_scr[...]
    sign = jnp.where((lax.broadcasted_iota(jnp.int32, hs.shape, 0) & 1) == 0, 1.0, -1.0)
    hq_ref[...] = jnp.where(first, jnp.sum(hs * sign, axis=0, keepdims=True), hr)


def _hyena_filter(L, filt):
    w1, b1, w2, b2, w3, b3, freq, w4 = filt
    zpad, window = _filter_consts(L)
    fwd, _ = _dft_mats(L)
    nch, fc2, _ = fwd.shape
    fc = fc2 // 2
    w1p = jnp.pad(w1, ((0, 128 - C_EMB), (0, 0)))
    full = lambda shape: pl.BlockSpec(shape, lambda c: tuple(0 for _ in shape))
    out_spec = pl.BlockSpec((fc, C_DIM), lambda c: (c, 0))
    out_sd = jax.ShapeDtypeStruct((L, C_DIM), F32)
    return pl.pallas_call(
        _filter_body,
        grid=(nch,),
        in_specs=[full((L, 128)), full((128, C_FFN)), full((1, C_FFN)), full((C_FFN, C_FFN)), full((1, C_FFN)),
                  full((C_FFN, C_FFN)), full((1, C_FFN)), full((1, C_FFN)), full((C_FFN, 2 * C_DIM)),
                  full((L, C_DIM)), pl.BlockSpec((1, fc2, L), lambda c: (c, 0, 0))],
        out_specs=[out_spec, out_spec, out_spec],
        out_shape=[out_sd, out_sd, out_sd],
        scratch_shapes=[pltpu.VMEM((L, C_DIM), F32), pltpu.VMEM((L, C_DIM), F32)],
        compiler_params=_cparams("arbitrary"),
        name="hyena_filter",
    )(jnp.asarray(zpad), w1p, b1.reshape(1, C_FFN), w2, b2.reshape(1, C_FFN), w3, b3.reshape(1, C_FFN),
      freq.reshape(1, C_FFN), w4, jnp.asarray(window), jnp.asarray(fwd))


def _hyena_body(u_ref, cw_ref, cb_ref, d_ref, fm_ref, fi_ref, hr_ref, g_ref, hq_ref, y_ref,
                x0_scr, z_scr, acc_scr):
    c = pl.program_id(1)
    L = y_ref.shape[0]
    fc = hr_ref.shape[0]

    @pl.when(c == 0)
    def _():
        row = lax.broadcasted_iota(jnp.int32, (L, C_DIM), 0)

        def short_conv(sec):
            cols = slice(sec * C_DIM, (sec + 1) * C_DIM)
            u = u_ref[:, cols]
            prev = jnp.where(row == 0, 0.0, pltpu.roll(u, 1, axis=0))
            nxt = jnp.where(row == L - 1, 0.0, pltpu.roll(u, L - 1, axis=0))
            return (prev * cw_ref[0:1, cols] + u * cw_ref[1:2, cols] + nxt * cw_ref[2:3, cols]
                    + cb_ref[:, cols])

        x0_scr[...] = short_conv(0)
        z_scr[...] = short_conv(1) * short_conv(2)
        acc_scr[...] = jnp.zeros((L, C_DIM), F32)

    ab = _dot_split(fm_ref[0], z_scr[...])
    a, b = ab[:fc], ab[fc:]
    hr, g, hq = hr_ref[...], g_ref[...], hq_ref[...]
    pq = jnp.concatenate([a * hr - b * g, a * g + b * hq], axis=0)
    acc_scr[...] += _dot_split(fi_ref[0], pq)

    @pl.when(c == pl.num_programs(1) - 1)
    def _():
        y_ref[...] = x0_scr[...] * (acc_scr[...] + z_scr[...] * d_ref[...])


def _hyena(u, row_blk0, n_seq, L, conv_w, conv_b, d_skip, spec):
    hr, g, hq = spec
    fwd, inv = _dft_mats(L)
    nch, fc2, _ = fwd.shape
    fc = fc2 // 2
    u_w = 3 * C_DIM
    return pl.pallas_call(
        _hyena_body,
        grid=(n_seq, nch),
        in_specs=[pl.BlockSpec((L, u_w), lambda b, c: (row_blk0 + b, 0)),
                  pl.BlockSpec((3, u_w), lambda b, c: (0, 0)),
                  pl.BlockSpec((1, u_w), lambda b, c: (0, 0)),
                  pl.BlockSpec((1, C_DIM), lambda b, c: (0, 0)),
                  pl.BlockSpec((1, fc2, L), lambda b, c: (c, 0, 0)),
                  pl.BlockSpec((1, L, fc2), lambda b, c: (c, 0, 0)),
                  pl.BlockSpec((fc, C_DIM), lambda b, c: (c, 0)),
                  pl.BlockSpec((fc, C_DIM), lambda b, c: (c, 0)),
                  pl.BlockSpec((fc, C_DIM), lambda b, c: (c, 0))],
        out_specs=pl.BlockSpec((L, C_DIM), lambda b, c: (b, 0)),
        out_shape=jax.ShapeDtypeStruct((n_seq * L, C_DIM), F32),
        scratch_shapes=[pltpu.VMEM((L, C_DIM), F32)] * 3,
        compiler_params=_cparams("parallel", "arbitrary"),
        name="hyena",
    )(u, conv_w, conv_b.reshape(1, u_w), d_skip.reshape(1, C_DIM), jnp.asarray(fwd), jnp.asarray(inv), hr, g, hq)


def _hgrn_body(q_ref, ff_ref, fb_ref, i_ref, g_ref, lbf_ref, lbb_ref, nd_ref, s0f_ref, s0b_ref,
               o_ref, sf_ref, sb_ref, *, layer):
    L = o_ref.shape[0]
    C = GLA_CHUNK
    S = min(L, GLA_SPAN)
    nc = S // C
    n_span = L // S
    mid = C // 2
    def lower_bound(gm):
        e = jnp.exp(gm - jnp.max(gm, axis=0, keepdims=True))
        p = e / jnp.sum(e, axis=0, keepdims=True)
        return jnp.sum(p[0:layer + 1], axis=0, keepdims=True) - p[0:1]

    def gates(fx, lb):
        f = lb + (1.0 - lb) * jax.nn.sigmoid(fx)
        return 1.0 - f, jnp.log(f)


    chunk_shift = C.bit_length() - 1
    block_shift = D_KDIM.bit_length() - 1
    ti = lax.broadcasted_iota(jnp.int32, (S, S), 0)
    si = lax.broadcasted_iota(jnp.int32, (S, S), 1)
    same_chunk = (ti >> chunk_shift) == (si >> chunk_shift)
    causal = same_chunk & (si <= ti)
    anti = same_chunk & (si >= ti)
    row_chunk = lax.broadcasted_iota(jnp.int32, (S, nc * D_KDIM), 0) >> chunk_shift
    col_chunk = lax.broadcasted_iota(jnp.int32, (S, nc * D_KDIM), 1) >> block_shift
    own_block = row_chunk == col_chunk

    def spread(x):
        return jnp.where(own_block, jnp.concatenate([x] * nc, axis=1), 0.0)

    def chunk_cumsum(mask, lg):
        tri = mask.astype(BF16)
        hi = lg.astype(BF16)
        r1 = lg - hi.astype(F32)
        mid_t = r1.astype(BF16)
        lo = (r1 - mid_t.astype(F32)).astype(BF16)
        dot = lambda t: jnp.dot(tri, t, preferred_element_type=F32)
        return dot(hi) + dot(mid_t) + dot(lo)

    def per_chunk_rows(b, pos):
        return jnp.concatenate([jnp.broadcast_to(b[n * C + pos:n * C + pos + 1], (C, D_KDIM)) for n in range(nc)],
                               axis=0)

    def one_head(q, v, kf, lgf, kb, lgb, st_f, st_b):
        local = []
        for u in range(n_span):
            rows = slice(u * S, (u + 1) * S)
            qs, vs, kfs, kbs = q[rows], v[rows], kf[rows], kb[rows]
            lgs = jnp.concatenate([lgf[rows], lgb[rows]], axis=1)
            pre = chunk_cumsum(causal, lgs)
            b_f = pre[:, :D_KDIM]
            pre_b = pre[:, D_KDIM:]
            b_b = per_chunk_rows(pre_b, C - 1) - pre_b + lgb[rows]
            ref_f, ref_b = per_chunk_rows(b_f, mid), per_chunk_rows(b_b, mid)
            sc = (jnp.where(causal, _dot_nt(qs * jnp.exp(b_f - ref_f), kfs * jnp.exp(ref_f - b_f)), 0.0)
                  + jnp.where(anti, _dot_nt(qs * jnp.exp(b_b - ref_b), kbs * jnp.exp(ref_b - b_b)), 0.0))
            k_out = jnp.concatenate([kfs * jnp.exp(per_chunk_rows(b_f, C - 1) - b_f),
                                     kbs * jnp.exp(per_chunk_rows(b_b, 0) - b_b)], axis=1)
            kv_t = _dot_tn(spread(vs), k_out)
            local.append((_dot(sc, vs), kv_t, b_f, b_b, qs))

        states_f = [[None] * nc for _ in range(n_span)]
        for u in range(n_span):
            _, kv_t, b_f, _, _ = local[u]
            for n in range(nc):
                states_f[u][n] = st_f
                st_f = st_f * jnp.exp(b_f[n * C + C - 1:n * C + C]) + kv_t[n * D_VDIM:(n + 1) * D_VDIM, :D_KDIM]
        states_b = [[None] * nc for _ in range(n_span)]
        for u in reversed(range(n_span)):
            _, kv_t, _, b_b, _ = local[u]
            for n in reversed(range(nc)):
                states_b[u][n] = st_b
                st_b = st_b * jnp.exp(b_b[n * C:n * C + 1]) + kv_t[n * D_VDIM:(n + 1) * D_VDIM, D_KDIM:]

        outs = []
        for u in range(n_span):
            intra, _, b_f, b_b, qs = local[u]
            q_in = jnp.concatenate([spread(qs * jnp.exp(b_f)), spread(qs * jnp.exp(b_b))], axis=1)
            outs.append(intra + _dot_nt(q_in, jnp.concatenate(states_f[u] + states_b[u], axis=1)))
        return (jnp.concatenate(outs, axis=0) if n_span > 1 else outs[0]), st_f, st_b

    for hh in range(o_ref.shape[1] // D_VDIM):
        cols = slice(hh * D_KDIM, (hh + 1) * D_KDIM)
        kf, lgf = gates(ff_ref[:, cols], lower_bound(lbf_ref[:, cols]))
        kb, lgb = gates(fb_ref[:, cols], lower_bound(lbb_ref[:, cols]))
        o, st_f, st_b = one_head(_silu(q_ref[:, cols]), i_ref[:, cols], kf, lgf, kb, lgb,
                                 jnp.transpose(s0f_ref[0, hh]), jnp.transpose(s0b_ref[0, hh]))
        sf_ref[0, hh] = jnp.transpose(st_f)
        sb_ref[0, hh] = jnp.transpose(st_b)
        o_ref[:, cols] = _rms(o, nd_ref[...]) * _silu(g_ref[:, cols])


def _hgrn(u, row_blk0, n_seq, L, lb_fwd, lb_bwd, norm_d, s0f, s0b, layer):
    hps = HGRN_HEADS_PER_STEP
    width = hps * D_KDIM
    col0 = 3 * C_DIM // width
    groups = D_HEADS // hps
    col = lambda j: pl.BlockSpec((L, width), lambda b, h: (row_blk0 + b, col0 + j * groups + h))
    lbs = pl.BlockSpec((DEPTH, width), lambda b, h: (0, h))
    st = pl.BlockSpec((1, hps, D_KDIM, D_VDIM), lambda b, h: (b, h, 0, 0))
    st_sd = jax.ShapeDtypeStruct((n_seq, D_HEADS, D_KDIM, D_VDIM), F32)
    return pl.pallas_call(
        functools.partial(_hgrn_body, layer=layer),
        grid=(n_seq, groups),
        in_specs=[col(0), col(1), col(2), col(3), col(4), lbs, lbs,
                  pl.BlockSpec((1, D_VDIM), lambda b, h: (0, 0)), st, st],
        out_specs=[pl.BlockSpec((L, width), lambda b, h: (b, h)), st, st],
        out_shape=[jax.ShapeDtypeStruct((n_seq * L, D_HEADS * D_VDIM), F32), st_sd, st_sd],
        compiler_params=_cparams("parallel", "parallel"),
        name="hgrn",
    )(u, u, u, u, u, lb_fwd, lb_bwd, norm_d.reshape(1, D_VDIM), s0f, s0b)


def _pack_bf16_pairs(h):
    n = h.shape[1] // 2
    hi = lax.bitcast_convert_type(h[:, :n].astype(BF16).astype(F32), jnp.int32)
    lo = lax.bitcast_convert_type(h[:, n:].astype(BF16).astype(F32), jnp.int32)
    return hi | lax.shift_right_logical(lo, 16)


def _unpack_bf16_pairs(p):
    hi = lax.bitcast_convert_type(p & jnp.int32(-65536), F32).astype(BF16)
    lo = lax.bitcast_convert_type(lax.shift_left(p, 16), F32).astype(BF16)
    return hi, lo


def _outproj_body(*refs, n_x):
    a_refs, b_refs, x_refs = refs[0:2], refs[2:4], refs[4:4 + n_x]
    mod_ref, gf_ref, w_ref, wrh_ref, wrl_ref, rb_ref, x1_ref, h2_ref, chosen_ref, gk_ref, ik_ref = refs[4 + n_x:]
    m = mod_ref[0]
    half = a_refs[0].shape[1]
    out = _dot(_token_tile(a_refs), w_ref[0:half, :]) + _dot(_token_tile(b_refs), w_ref[half:, :])
    x1 = _token_tile(x_refs) + m[:, 2 * D_MODEL:3 * D_MODEL] * out
    x1_ref[...] = x1
    h2 = _rms(x1, gf_ref[...]) * (1.0 + m[:, 4 * D_MODEL:5 * D_MODEL]) + m[:, 3 * D_MODEL:4 * D_MODEL]
    h2_ref[...] = _pack_bf16_pairs(h2)
    h_hi = h2.astype(BF16)
    h_lo = (h2 - h_hi.astype(F32)).astype(BF16)
    logits = _dot_nt(wrh_ref[...], h_hi) + _dot_nt(wrh_ref[...], h_lo) + _dot_nt(wrl_ref[...], h_hi)
    scores = jax.nn.sigmoid(logits)
    work = scores + rb_ref[...]
    expert = lax.broadcasted_iota(jnp.int32, work.shape, 0).astype(F32)
    slot = lax.broadcasted_iota(jnp.int32, (TOP_K, work.shape[1]), 0)
    chosen = [jnp.zeros(work.shape, F32) for _ in range(MOE_REGIONS)]
    gk = jnp.zeros((TOP_K, work.shape[1]), F32)
    ik = jnp.zeros((TOP_K, work.shape[1]), F32)
    for k in range(TOP_K):
        best = jnp.max(work, axis=0, keepdims=True)
        first = jnp.min(jnp.where(work == best, expert, float(N_EXPERTS)), axis=0, keepdims=True)
        hit = expert == first
        chosen[k // K_PER_REGION] = jnp.where(hit, 1.0, chosen[k // K_PER_REGION])
        gk = jnp.where(slot == k, jnp.sum(jnp.where(hit, scores, 0.0), axis=0, keepdims=True), gk)
        ik = jnp.where(slot == k, first, ik)
        work = jnp.where(hit, -jnp.inf, work)
    for r in range(MOE_REGIONS):
        chosen_ref[r] = chosen[r]
    gk_ref[...] = jnp.transpose(gk / jnp.sum(gk, axis=0, keepdims=True) * ROUTE_SCALE)
    ik_ref[...] = ik


def _outproj(a, b, x, mod_l, gain_ffn, w_out, w_router, router_bias):
    half = a[0].shape[1]
    a_specs, a_args = _token_specs(a, half)
    b_specs, b_args = _token_specs(b, half)
    x_specs, x_args = _token_specs(x, D_MODEL)
    wr_t = w_router.T
    wr_hi = wr_t.astype(BF16)
    wr_lo = (wr_t - wr_hi.astype(F32)).astype(BF16)
    return pl.pallas_call(
        functools.partial(_outproj_body, n_x=len(x_args)),
        grid=(T_ALL // TM,),
        in_specs=a_specs + b_specs + x_specs + [
                  pl.BlockSpec((1, 1, N_MOD * D_MODEL), lambda i: (_mod_row(i), 0, 0)),
                  pl.BlockSpec((1, D_MODEL), lambda i: (0, 0)),
                  pl.BlockSpec((2 * half, D_MODEL), lambda i: (0, 0)),
                  pl.BlockSpec((N_EXPERTS, D_MODEL), lambda i: (0, 0)),
                  pl.BlockSpec((N_EXPERTS, D_MODEL), lambda i: (0, 0)),
                  pl.BlockSpec((N_EXPERTS, 1), lambda i: (0, 0))],
        out_specs=[pl.BlockSpec((TM, D_MODEL), lambda i: (i, 0)),
                   pl.BlockSpec((TM, D_MODEL // 2), lambda i: (i, 0)),
                   pl.BlockSpec((MOE_REGIONS, N_EXPERTS, TM), lambda i: (0, 0, i)),
                   pl.BlockSpec((TM, TOP_K), lambda i: (i, 0)),
                   pl.BlockSpec((TOP_K, TM), lambda i: (0, i))],
        out_shape=[jax.ShapeDtypeStruct((T_ALL, D_MODEL), F32),
                   jax.ShapeDtypeStruct((T_ALL, D_MODEL // 2), jnp.int32),
                   jax.ShapeDtypeStruct((MOE_REGIONS, N_EXPERTS, T_ALL), F32),
                   jax.ShapeDtypeStruct((T_ALL, TOP_K), F32),
                   jax.ShapeDtypeStruct((TOP_K, T_ALL), F32)],
        compiler_params=_cparams("parallel"),
        name="outproj_router",
    )(*a_args, *b_args, *x_args, mod_l, gain_ffn.reshape(1, D_MODEL), w_out, wr_hi, wr_lo,
      router_bias.reshape(N_EXPERTS, 1))


def _route_body(chosen_ref, ik_ref, dest_ref, first_ref, count_ref, short_ref, pos_scr):
    n_tiles = T_ALL // TM
    r = lax.broadcasted_iota(jnp.int32, (TM, TM), 0)
    c = lax.broadcasted_iota(jnp.int32, (TM, TM), 1)
    before = (r < c).astype(BF16)

    counts = jnp.zeros((N_EXPERTS, 1), F32)
    for i in range(n_tiles):
        cols = slice(i * TM, (i + 1) * TM)
        m = chosen_ref[0, :, cols]
        pos_scr[:, cols] = jnp.dot(m.astype(BF16), before, preferred_element_type=F32) + counts
        counts = counts + jnp.sum(m, axis=1, keepdims=True)
    padded = jnp.ceil(counts * (1.0 / MOE_BLK)) * MOE_BLK
    ei = lax.broadcasted_iota(jnp.int32, (N_EXPERTS, N_EXPERTS), 0)
    ej = lax.broadcasted_iota(jnp.int32, (N_EXPERTS, N_EXPERTS), 1)
    end = _dot_hi((ej <= ei).astype(F32), jnp.broadcast_to(padded, (N_EXPERTS, LANES)))[:, 0:1]
    start = end - padded

    expert = lax.broadcasted_iota(jnp.int32, (N_EXPERTS, TM), 0).astype(F32)
    slot = lax.broadcasted_iota(jnp.int32, (K_PER_REGION, TM), 0)
    for i in range(n_tiles):
        cols = slice(i * TM, (i + 1) * TM)
        row_of = pos_scr[:, cols] + start
        ik = ik_ref[0, :, cols]
        acc = jnp.zeros((K_PER_REGION, TM), F32)
        for k in range(K_PER_REGION):
            pick = jnp.sum(jnp.where(expert == ik[k:k + 1, :], row_of, 0.0), axis=0, keepdims=True)
            acc = jnp.where(slot == k, pick, acc)
        dest_ref[0, :, cols] = acc.astype(jnp.int32)
    first_ref[0] = jnp.broadcast_to(start * (1.0 / MOE_BLK), (N_EXPERTS, LANES)).astype(jnp.int32)
    count_ref[0] = jnp.broadcast_to(padded * (1.0 / MOE_BLK), (N_EXPERTS, LANES)).astype(jnp.int32)
    in_last = counts - (padded - MOE_BLK)
    short = jnp.where((counts > 0.0) & (in_last <= MOE_BLK // 2), 1.0, 0.0)
    short_ref[0] = jnp.broadcast_to(short, (N_EXPERTS, LANES)).astype(jnp.int32)


def _route(chosen, ik):
    per_region = lambda rows, cols: pl.BlockSpec((1, rows, cols), lambda r: (r, 0, 0))
    table = jax.ShapeDtypeStruct((MOE_REGIONS, N_EXPERTS, LANES), jnp.int32)
    return pl.pallas_call(
        _route_body,
        grid=(MOE_REGIONS,),
        in_specs=[per_region(N_EXPERTS, T_ALL), per_region(K_PER_REGION, T_ALL)],
        out_specs=[per_region(K_PER_REGION, T_ALL)] + [per_region(N_EXPERTS, LANES)] * 3,
        out_shape=[jax.ShapeDtypeStruct((MOE_REGIONS, K_PER_REGION, T_ALL), jnp.int32), table, table, table],
        scratch_shapes=[pltpu.VMEM((N_EXPERTS, T_ALL), F32)],
        compiler_params=_cparams("arbitrary"),
        name="moe_route",
    )(chosen, ik.reshape(MOE_REGIONS, K_PER_REGION, T_ALL))


def _sc_worker_id():
    return lax.axis_index("s") * SC_CORES + lax.axis_index("c")


def _sc_dispatch(h2p, dest):
    n_chunks = T_ALL // DISP_CHUNK
    k_per = dest.shape[0] // DISP_SPLIT
    items_per_worker = n_chunks * DISP_SPLIT // SC_WORKERS
    chunk_stride = SC_WORKERS // DISP_SPLIT
    width = h2p.shape[1]
    mesh = plsc.VectorSubcoreMesh(core_axis_name="c", subcore_axis_name="s")

    @functools.partial(
        pl.kernel, mesh=mesh,
        out_type=jax.ShapeDtypeStruct((MOE_ROWS, width), jnp.int32),
        scratch_types=[pltpu.VMEM((k_per, DISP_CHUNK), jnp.int32), pltpu.VMEM((DISP_CHUNK, width), jnp.int32),
                       pltpu.SemaphoreType.DMA],
    )
    def run(x_hbm, dest_hbm, xs_hbm, idx_v, rows_v, sem):
        wid = _sc_worker_id()
        group = wid % DISP_SPLIT
        for i in range(items_per_worker):
            chunk = i * chunk_stride + wid // DISP_SPLIT
            tokens = pl.ds(pl.multiple_of(chunk * DISP_CHUNK, DISP_CHUNK), DISP_CHUNK)
            pltpu.sync_copy(dest_hbm.at[group, :, tokens], idx_v)
            pltpu.sync_copy(x_hbm.at[tokens], rows_v)
            scatters = [pltpu.make_async_copy(rows_v, xs_hbm.at[idx_v.at[k]], sem) for k in range(k_per)]
            for cp in scatters:
                cp.start()
            for cp in scatters:
                cp.wait()

    return run(h2p, dest.reshape(DISP_SPLIT, k_per, T_ALL))


def _sc_collect(y, dest_flat):
    n_k = dest_flat.shape[0] // T_ALL
    per_worker = T_ALL // SC_WORKERS
    n_chunks = per_worker // COLLECT_CHUNK
    n_steps = n_k * n_chunks
    width = y.shape[1]
    mesh = plsc.VectorSubcoreMesh(core_axis_name="c", subcore_axis_name="s")

    @functools.partial(
        pl.kernel, mesh=mesh,
        out_type=jax.ShapeDtypeStruct((n_k * T_ALL, width), y.dtype),
        scratch_types=[pltpu.VMEM((n_k * per_worker,), jnp.int32),
                       pltpu.VMEM((COLLECT_CHUNK, width), y.dtype), pltpu.VMEM((COLLECT_CHUNK, width), y.dtype),
                       pltpu.SemaphoreType.DMA, pltpu.SemaphoreType.DMA],
    )
    def run(y_hbm, dest_hbm, yg_hbm, idx_v, rows0, rows1, sem0, sem1):
        wid = _sc_worker_id()
        bufs = ((rows0, sem0), (rows1, sem1))
        for k in range(n_k):
            pltpu.sync_copy(dest_hbm.at[pl.ds(k * T_ALL + wid * per_worker, per_worker)],
                            idx_v.at[pl.ds(k * per_worker, per_worker)])

        def gather(step, buf):
            rows, sem = buf
            idx = idx_v.at[pl.ds(pl.multiple_of(step * COLLECT_CHUNK, 8), COLLECT_CHUNK)]
            return pltpu.make_async_copy(y_hbm.at[idx], rows, sem)

        def out_rows(step):
            off = (step // n_chunks) * T_ALL + wid * per_worker + (step % n_chunks) * COLLECT_CHUNK
            return yg_hbm.at[pl.ds(pl.multiple_of(off, 8), COLLECT_CHUNK)]

        gather(0, bufs[0]).start()

        @pl.loop(0, n_steps, step=2)
        def _(base):
            for j in range(2):
                step = base + j

                @pl.when(step + 1 < n_steps)
                def _():
                    gather(step + 1, bufs[1 - j]).start()

                gather(step, bufs[j]).wait()
                pltpu.sync_copy(bufs[j][0], out_rows(step))

    return run(y, dest_flat)


def _expert_body(first_ref, count_ref, short_ref, xs_hbm, wg_hbm, wu_hbm, wd_hbm, y_hbm,
                 wg_f32, wu_f32, wd_f32, wg_bf, wu_bf, wd_bf, x_buf, y_buf, w_sem, in_sem, out_sem, *, layer):
    e = pl.program_id(0)
    first = first_ref[e]
    count = count_ref[e]
    n_used = first_ref[N_EXPERTS - 1] + count_ref[N_EXPERTS - 1]
    half = D_MODEL // 2

    def weight_copies(ex):
        slot = lax.rem(ex, EXPERT_W_SLOTS)
        return [pltpu.make_async_copy(src.at[layer, ex], dst.at[slot], w_sem.at[slot])
                for src, dst in ((wg_hbm, wg_f32), (wu_hbm, wu_f32), (wd_hbm, wd_f32))]

    @pl.when(e == 0)
    def _():
        for ahead in range(EXPERT_W_SLOTS - 1):
            for cp in weight_copies(ahead):
                cp.start()

    for cp in weight_copies(e):
        cp.wait()

    @pl.when(e + EXPERT_W_SLOTS - 1 < N_EXPERTS)
    def _():
        for cp in weight_copies(e + EXPERT_W_SLOTS - 1):
            cp.start()

    w_slot = lax.rem(e, EXPERT_W_SLOTS)
    wg_bf[...] = wg_f32[w_slot].astype(BF16)
    wu_bf[...] = wu_f32[w_slot].astype(BF16)
    wd_bf[...] = wd_f32[w_slot].astype(BF16)

    def part_rows(g, part, n_parts):
        size = MOE_BLK // n_parts
        return pl.ds(pl.multiple_of(g * MOE_BLK + part * size, size), size), pl.ds(part * size, size)

    def in_copies(g):
        slot = g & (EXPERT_SLOTS - 1)
        out = []
        for part in range(EXPERT_IN_PARTS):
            src, dst = part_rows(g, part, EXPERT_IN_PARTS)
            out.append(pltpu.make_async_copy(xs_hbm.at[src], x_buf.at[slot, dst], in_sem.at[slot]))
        return out

    def out_copies(g):
        slot = g & (EXPERT_SLOTS - 1)
        out = []
        for part in range(EXPERT_OUT_PARTS):
            dst, src = part_rows(g, part, EXPERT_OUT_PARTS)
            out.append(pltpu.make_async_copy(y_buf.at[slot, src], y_hbm.at[dst], out_sem.at[slot]))
        return out

    @pl.when((first == 0) & (count > 0))
    def _():
        for ahead in range(EXPERT_SLOTS - 1):
            @pl.when(ahead < n_used)
            def _():
                for cp in in_copies(ahead):
                    cp.start()

    def block(b, carry):
        g = first + b
        slot = g & (EXPERT_SLOTS - 1)
        for cp in in_copies(g):
            cp.wait()

        @pl.when(g + EXPERT_SLOTS - 1 < n_used)
        def _():
            for cp in in_copies(g + EXPERT_SLOTS - 1):
                cp.start()

        @pl.when(g >= EXPERT_SLOTS)
        def _():
            for cp in out_copies(g - EXPERT_SLOTS):
                cp.wait()

        def ffn(n_rows):
            hi, lo = _unpack_bf16_pairs(x_buf[slot, 0:n_rows])

            def proj(w_bf):
                return (jnp.dot(hi, w_bf[0:half, :], preferred_element_type=F32)
                        + jnp.dot(lo, w_bf[half:, :], preferred_element_type=F32))

            hid = _silu(proj(wg_bf)) * proj(wu_bf)
            y_buf[slot, 0:n_rows] = _pack_bf16_pairs(
                jnp.dot(hid.astype(BF16), wd_bf[...], preferred_element_type=F32))

        short = (b == count - 1) & (short_ref[e] == 1)

        @pl.when(short)
        def _():
            ffn(MOE_BLK // 2)
            y_buf[slot, MOE_BLK // 2:MOE_BLK] = jnp.zeros((MOE_BLK // 2, D_MODEL // 2), jnp.int32)

        @pl.when(jnp.logical_not(short))
        def _():
            ffn(MOE_BLK)

        for cp in out_copies(g):
            cp.start()
        return carry

    lax.fori_loop(0, count, block, 0)

    @pl.when(e == N_EXPERTS - 1)
    def _():
        for back in range(EXPERT_SLOTS, 0, -1):
            @pl.when(n_used >= back)
            def _():
                for cp in out_copies(n_used - back):
                    cp.wait()


EXPERT_SLOTS = 4
EXPERT_W_SLOTS = 3
EXPERT_IN_PARTS = 2
EXPERT_OUT_PARTS = 4


def _experts(first_blk, n_blk, short_last, xs, layer, w_gate, w_up, w_down):
    anywhere = pl.BlockSpec(memory_space=pl.ANY)
    grid_spec = pltpu.PrefetchScalarGridSpec(
        num_scalar_prefetch=3,
        grid=(N_EXPERTS,),
        in_specs=[anywhere] * 4,
        out_specs=anywhere,
        scratch_shapes=[pltpu.VMEM((EXPERT_W_SLOTS, D_MODEL, D_EXPERT), F32),
                        pltpu.VMEM((EXPERT_W_SLOTS, D_MODEL, D_EXPERT), F32),
                        pltpu.VMEM((EXPERT_W_SLOTS, D_EXPERT, D_MODEL), F32),
                        pltpu.VMEM((D_MODEL, D_EXPERT), BF16), pltpu.VMEM((D_MODEL, D_EXPERT), BF16),
                        pltpu.VMEM((D_EXPERT, D_MODEL), BF16),
                        pltpu.VMEM((EXPERT_SLOTS, MOE_BLK, D_MODEL // 2), jnp.int32),
                        pltpu.VMEM((EXPERT_SLOTS, MOE_BLK, D_MODEL // 2), jnp.int32),
                        pltpu.SemaphoreType.DMA((EXPERT_W_SLOTS,)),
                        pltpu.SemaphoreType.DMA((EXPERT_SLOTS,)), pltpu.SemaphoreType.DMA((EXPERT_SLOTS,))],
    )
    return pl.pallas_call(
        functools.partial(_expert_body, layer=layer),
        grid_spec=grid_spec,
        out_shape=jax.ShapeDtypeStruct((MOE_ROWS, D_MODEL // 2), jnp.int32),
        compiler_params=_cparams("arbitrary"),
        name="moe_experts",
    )(first_blk, n_blk, short_last, xs, w_gate, w_up, w_down)


def _combine_body(x1_ref, h2_ref, *refs, final):
    yg_refs = refs[:MOE_REGIONS]
    gk_ref, mod_ref, sg_ref, su_ref, sd_ref, fn_ref, *o_refs = refs[MOE_REGIONS:]
    hi, lo = _unpack_bf16_pairs(h2_ref[...])
    half = D_MODEL // 2

    def proj(w_ref):
        return _dot(hi, w_ref[0:half, :]) + _dot(lo, w_ref[half:, :])

    shared = _dot(_silu(proj(sg_ref)) * proj(su_ref), sd_ref[...])
    acc_hi, acc_lo = shared[:, :half], shared[:, half:]
    gk = gk_ref[...]
    for k in range(TOP_K):
        y_hi, y_lo = _unpack_bf16_pairs(yg_refs[k // K_PER_REGION][k % K_PER_REGION])
        acc_hi = acc_hi + gk[:, k:k + 1] * y_hi.astype(F32)
        acc_lo = acc_lo + gk[:, k:k + 1] * y_lo.astype(F32)
    acc = jnp.concatenate([acc_hi, acc_lo], axis=1)
    m = mod_ref[0]
    y = x1_ref[...] + m[:, 5 * D_MODEL:6 * D_MODEL] * acc
    if not final:
        o_refs[0][...] = y
        return
    y = _rms(y, fn_ref[...])
    is_ctx = pl.program_id(0) < N_CTX_TILES

    @pl.when(is_ctx)
    def _():
        o_refs[0][...] = y

    @pl.when(jnp.logical_not(is_ctx))
    def _():
        o_refs[1][...] = y


def _combine(x1, h2p, yg, gk, mod_l, ws_gate, ws_up, ws_down, final_norm, final):
    tok = lambda shape: pl.BlockSpec(shape, lambda i: (i, 0))
    full = lambda shape: pl.BlockSpec(shape, lambda i: (0, 0))
    if final:
        out_specs, _ = _token_specs((None, None), D_MODEL)
        out_shape = [jax.ShapeDtypeStruct((T_CTX, D_MODEL), F32), jax.ShapeDtypeStruct((T_LAT, D_MODEL), F32)]
    else:
        out_specs = tok((TM, D_MODEL))
        out_shape = jax.ShapeDtypeStruct((T_ALL, D_MODEL), F32)
    return pl.pallas_call(
        functools.partial(_combine_body, final=final),
        grid=(T_ALL // TM,),
        in_specs=[tok((TM, D_MODEL)), tok((TM, D_MODEL // 2))]
                 + [pl.BlockSpec((K_PER_REGION, TM, D_MODEL // 2), lambda i: (0, i, 0))] * MOE_REGIONS
                 + [tok((TM, TOP_K)),
                  pl.BlockSpec((1, 1, N_MOD * D_MODEL), lambda i: (_mod_row(i), 0, 0)),
                  full((D_MODEL, D_EXPERT)), full((D_MODEL, D_EXPERT)), full((D_EXPERT, D_MODEL)),
                  full((1, D_MODEL))],
        out_specs=out_specs,
        out_shape=out_shape,
        compiler_params=_cparams("arbitrary"),
        name="moe_combine",
    )(x1, h2p, *yg, gk, mod_l, ws_gate, ws_up, ws_down, final_norm.reshape(1, D_MODEL))


def _moe(x1, h2p, chosen, gk, ik, mod_l, layer, w_gate, w_up, w_down, ws_gate, ws_up, ws_down, final_norm, final):
    dest, first_blk, n_blk, short_last = _route(chosen, ik)
    yg = []
    for r in range(MOE_REGIONS):
        xs = _sc_dispatch(h2p, dest[r])
        y = _experts(first_blk[r, :, 0], n_blk[r, :, 0], short_last[r, :, 0], xs, layer, w_gate, w_up, w_down)
        yg.append(_sc_collect(y, dest[r].reshape(-1)).reshape(K_PER_REGION, T_ALL, D_MODEL // 2))
    return _combine(x1, h2p, yg, gk, mod_l, ws_gate, ws_up, ws_down, final_norm, final)


def kernel(x_prompt, x_sample, cache_a_k, cache_a_v, cache_b_k, cache_b_v, state_d_fwd, state_d_bwd, c, c_ctx, w_ada, b_ada, norm_mix, norm_ffn, w_in_attn, w_out_attn, sink_a, rpb_b, w_in_rec, w_out_rec, conv_w, conv_b, filt_w1, filt_b1, filt_w2, filt_b2, filt_w3, filt_b3, filt_freq, filt_w4, d_skip, lb_fwd, lb_bwd, norm_d, w_router, router_bias, w_gate, w_up, w_down, ws_gate, ws_up, ws_down, final_norm):
    x = (x_prompt.reshape(T_CTX, D_MODEL), x_sample.reshape(T_LAT, D_MODEL))
    cvec = jnp.concatenate([c_ctx[None, :], c], axis=0)
    c_lanes = jnp.broadcast_to(cvec[:, :, None], (N_CVEC, D_MODEL, LANES))
    mod = [_ada(c_lanes, l, w_ada, b_ada).reshape(CVEC_PAD, 1, N_MOD * D_MODEL) for l in range(DEPTH)]

    new_kv = None
    new_state = None
    for l in range(DEPTH):
        j = l // 2
        final = l == DEPTH - 1
        if l % 2 == 0:
            qkv = _inproj(x, mod[l], norm_mix[l], w_in_attn[j])
            oa_ctx, ob_ctx, *new_kv = _ctx_attn(qkv, sink_a[j])
            new_kv = tuple(new_kv)
            q_rot, k_rot = _rope(qkv)
            cache = lambda t: t[:, j].reshape(DEC_BATCH, PAST_LEN, -1)
            oa_lat = _win_attn(qkv, q_rot, k_rot, cache(cache_a_k), cache(cache_a_v), sink_a[j])
            ob_lat = _na_attn(qkv, cache(cache_b_k), cache(cache_b_v), _na_rel_rows(rpb_b[j]))
            mix_a = (oa_ctx, oa_lat)
            mix_b = (ob_ctx, ob_lat)
            w_out = w_out_attn[j]
        else:
            u = _inproj(x, mod[l], norm_mix[l], w_in_rec[j])
            filt = (filt_w1[j], filt_b1[j], filt_w2[j], filt_b2[j], filt_w3[j], filt_b3[j], filt_freq[j],
                    filt_w4[j])
            y_ctx = _hyena(u, 0, BATCH, SEQ, conv_w[j], conv_b[j], d_skip[j], _hyena_filter(SEQ, filt))
            y_lat = _hyena(u, T_CTX // DEC_SEQ, DEC_BATCH, DEC_SEQ, conv_w[j], conv_b[j], d_skip[j],
                           _hyena_filter(DEC_SEQ, filt))
            zeros = jnp.zeros((BATCH, D_HEADS, D_KDIM, D_VDIM), F32)
            o_ctx, s_f, s_b = _hgrn(u, 0, BATCH, SEQ, lb_fwd, lb_bwd, norm_d[j], zeros, zeros, l)
            o_lat, _, _ = _hgrn(u, T_CTX // DEC_SEQ, DEC_BATCH, DEC_SEQ, lb_fwd, lb_bwd, norm_d[j],
                                state_d_fwd[:, j], state_d_bwd[:, j], l)
            new_state = (s_f[:, None], s_b[:, None])
            mix_a = (y_ctx, y_lat)
            mix_b = (o_ctx, o_lat)
            w_out = w_out_rec[j]
        x1, h2p, chosen, gk, ik = _outproj(mix_a, mix_b, x, mod[l], norm_ffn[l], w_out, w_router[l],
                                           router_bias[l])
        x = _moe(x1, h2p, chosen, gk, ik, mod[l], l, w_gate, w_up, w_down, ws_gate[l], ws_up[l],
                 ws_down[l], final_norm, final)

    y_prompt = x[0].reshape(BATCH, SEQ, D_MODEL)
    y_sample = x[1].reshape(DEC_BATCH, DEC_SEQ, D_MODEL)
    return (y_prompt, y_sample) + new_kv + new_state
```

```python
import functools
import math

import numpy as np
import jax
import jax.numpy as jnp
from jax import lax
from jax.experimental import pallas as pl
from jax.experimental.pallas import tpu as pltpu
from jax.experimental.pallas import tpu_sc as plsc

F32 = jnp.float32
BF16 = jnp.bfloat16
HI = lax.Precision.HIGHEST

D_MODEL = 1024
BATCH = 16
SEQ = 256
DEPTH = 2
DEC_BATCH = 2
DEC_SEQ = 1024
PAST_LEN = 512
GRID_W = 64
HEAD_DIM = 64
N_MOD = 6
RMS_EPS = 1e-6
A_HEADS = 8
A_KV_HEADS = 2
A_GROUP = A_HEADS // A_KV_HEADS
WINDOW = 128
ROPE_BASE = 10000.0
B_HEADS = 8
NA_ROWS = 8
NA_COLS = 16
C_DIM = 512
C_EMB = 33
C_FFN = 64
HYENA_MIN_DECAY = math.log(1e-2) / 1.5
HYENA_MAX_DECAY = math.log(1e-2) / 0.3
D_KDIM = 128
D_VDIM = 128
D_HEADS = 4
N_EXPERTS = 64
TOP_K = 8
D_EXPERT = 256
ROUTE_SCALE = 2.5
A_Q = A_HEADS * HEAD_DIM
A_KV = A_KV_HEADS * HEAD_DIM
B_W = B_HEADS * HEAD_DIM
ATTN_IN = A_Q + 2 * A_KV + 3 * B_W
REC_IN = 3 * C_DIM + 5 * D_HEADS * D_KDIM

T_CTX = BATCH * SEQ
T_LAT = DEC_BATCH * DEC_SEQ
T_ALL = T_CTX + T_LAT
N_CVEC = 1 + DEC_BATCH
CVEC_PAD = 8
TM = 512
MASK_NEG = -1e30
GLA_CHUNK = 64
GLA_SPAN = 256
HGRN_HEADS_PER_STEP = 4
DFT_CHUNK = 256
MOE_BLK = 512
MOE_REGIONS = 2
K_PER_REGION = TOP_K // MOE_REGIONS
MOE_NBLK = -(-(T_ALL * K_PER_REGION + N_EXPERTS * (MOE_BLK - 1)) // MOE_BLK)
MOE_ROWS = MOE_NBLK * MOE_BLK
SC_CORES = 2
SC_SUBCORES = 16
SC_WORKERS = SC_CORES * SC_SUBCORES
DISP_CHUNK = 128
DISP_SPLIT = 2
COLLECT_CHUNK = 64
VMEM_LIMIT = 56 * 1024 * 1024


def _cparams(*sem):
    return pltpu.CompilerParams(dimension_semantics=sem, vmem_limit_bytes=VMEM_LIMIT)


def _mod_row(i):
    return jnp.where(i < T_CTX // TM, 0, 1 + (i - T_CTX // TM) // (DEC_SEQ // TM))


def _dot(a, b):
    return jnp.dot(a.astype(BF16), b.astype(BF16), preferred_element_type=F32)


def _dot_nt(a, b):
    return lax.dot_general(a.astype(BF16), b.astype(BF16), (((1,), (1,)), ((), ())),
                           preferred_element_type=F32)


def _dot_tn(a, b):
    return lax.dot_general(a.astype(BF16), b.astype(BF16), (((0,), (0,)), ((), ())),
                           preferred_element_type=F32)


def _dot_hi(a, b):
    return jnp.dot(a, b, precision=HI, preferred_element_type=F32)


def _split_bf16(x):
    hi = x.astype(BF16)
    return hi, (x - hi.astype(F32)).astype(BF16)


def _dot_split(a, b):
    a_hi, a_lo = _split_bf16(a)
    b_hi, b_lo = _split_bf16(b)
    dot = lambda x, y: jnp.dot(x, y, preferred_element_type=F32)
    return dot(a_hi, b_hi) + dot(a_hi, b_lo) + dot(a_lo, b_hi)


def _silu(x):
    return x * jax.nn.sigmoid(x)


def _rms(x, g):
    return x * lax.rsqrt(jnp.mean(x * x, axis=-1, keepdims=True) + RMS_EPS) * g


ADA_TN = 1536
ADA_UNROLL = 4


def _ada_body(cb_ref, w_ref, b_ref, o_ref):
    tn = o_ref.shape[-1]
    n_slab = tn // LANES

    def step(k8, accs):
        r0 = pl.multiple_of(k8 * 8, 8)
        sk = [_silu(cb_ref[j, pl.ds(r0, 8), :]) for j in range(N_CVEC)]
        out = []
        for s in range(n_slab):
            wk = w_ref[0, pl.ds(r0, 8), s * LANES:(s + 1) * LANES]
            out.extend(accs[s * N_CVEC + j] + wk * sk[j] for j in range(N_CVEC))
        return tuple(out)

    accs = lax.fori_loop(0, D_MODEL // 8, step,
                         tuple(jnp.zeros((8, LANES), F32) for _ in range(n_slab * N_CVEC)), unroll=ADA_UNROLL)
    o_ref[0] = jnp.zeros((CVEC_PAD, tn), F32)
    for s in range(n_slab):
        for j in range(N_CVEC):
            o_ref[0, j:j + 1, s * LANES:(s + 1) * LANES] = (
                jnp.sum(accs[s * N_CVEC + j], axis=0, keepdims=True) + b_ref[0, :, s * LANES:(s + 1) * LANES])


def _ada(c_lanes, layer, w_ada, b_ada):
    n_out = N_MOD * D_MODEL
    return pl.pallas_call(
        _ada_body,
        grid=(n_out // ADA_TN,),
        in_specs=[pl.BlockSpec((N_CVEC, D_MODEL, LANES), lambda n: (0, 0, 0)),
                  pl.BlockSpec((1, D_MODEL, ADA_TN), lambda n: (layer, 0, n)),
                  pl.BlockSpec((1, 1, ADA_TN), lambda n: (layer, 0, n))],
        out_specs=pl.BlockSpec((1, CVEC_PAD, ADA_TN), lambda n: (0, 0, n)),
        out_shape=jax.ShapeDtypeStruct((1, CVEC_PAD, n_out), F32),
        compiler_params=_cparams("parallel"),
        name="ada",
    )(c_lanes, w_ada, b_ada.reshape(DEPTH, 1, n_out))


N_CTX_TILES = T_CTX // TM


def _token_specs(x, width):
    if not isinstance(x, tuple):
        return [pl.BlockSpec((TM, width), lambda i: (i, 0))], (x,)
    return ([pl.BlockSpec((TM, width), lambda i: (jnp.minimum(i, N_CTX_TILES - 1), 0)),
             pl.BlockSpec((TM, width), lambda i: (jnp.maximum(i - N_CTX_TILES, 0), 0))], x)


def _token_tile(refs):
    if len(refs) == 1:
        return refs[0][...]
    return jnp.where(pl.program_id(0) < N_CTX_TILES, refs[0][...], refs[1][...])


def _inproj_body(*refs, n_x):
    x_refs, (mod_ref, g_ref, w_ref, o_ref, w_bf) = refs[:n_x], refs[n_x:]

    @pl.when(pl.program_id(0) == 0)
    def _():
        w_bf[...] = w_ref[...].astype(BF16)

    m = mod_ref[0]
    h = _rms(_token_tile(x_refs), g_ref[...]) * (1.0 + m[:, D_MODEL:2 * D_MODEL]) + m[:, 0:D_MODEL]
    o_ref[...] = _dot(h, w_bf[...])


def _inproj(x, mod_l, gain, w):
    n = w.shape[1]
    x_specs, x_args = _token_specs(x, D_MODEL)
    return pl.pallas_call(
        functools.partial(_inproj_body, n_x=len(x_args)),
        grid=(T_ALL // TM,),
        in_specs=x_specs + [pl.BlockSpec((1, 1, N_MOD * D_MODEL), lambda i: (_mod_row(i), 0, 0)),
                            pl.BlockSpec((1, D_MODEL), lambda i: (0, 0)),
                            pl.BlockSpec((D_MODEL, n), lambda i: (0, 0), pipeline_mode=pl.Buffered(1))],
        out_specs=pl.BlockSpec((TM, n), lambda i: (i, 0)),
        out_shape=jax.ShapeDtypeStruct((T_ALL, n), F32),
        scratch_shapes=[pltpu.VMEM((D_MODEL, n), BF16)],
        compiler_params=_cparams("arbitrary"),
        name="inproj",
    )(*x_args, mod_l, gain.reshape(1, D_MODEL), w)


def _head_cols(h):
    return slice(h * HEAD_DIM, (h + 1) * HEAD_DIM)


def _group_rows(ref, rows, first_col, sink_ref, hk):
    n = rows.stop - rows.start
    q = jnp.concatenate([ref[rows, first_col + g * HEAD_DIM:first_col + (g + 1) * HEAD_DIM]
                         for g in range(A_GROUP)], axis=0)
    sink = jnp.concatenate([jnp.broadcast_to(sink_ref[:, hk * A_GROUP + g:hk * A_GROUP + g + 1], (n, 1))
                            for g in range(A_GROUP)], axis=0)
    return q, sink


def _ctx_attn_body(qkv_ref, sink_ref, oa_ref, ob_ref, ak_ref, av_ref, bk_ref, bv_ref):
    scale = HEAD_DIM ** -0.5
    lane = lax.broadcasted_iota(jnp.int32, (SEQ, LANES), 1)
    in_half = [lane < HEAD_DIM, lane >= HEAD_DIM]

    def attend(q, k, v, sink):
        s = _dot_nt(q, k) * scale
        m = jnp.max(s, axis=-1, keepdims=True)
        if sink is not None:
            m = jnp.maximum(m, sink)
        p = jnp.exp(s - m)
        den = jnp.sum(p, axis=-1, keepdims=True)
        if sink is not None:
            den = den + jnp.exp(sink - m)
        return _dot(p, v) / den

    def tile(first_col, t):
        return qkv_ref[:, first_col + t * LANES:first_col + (t + 1) * LANES]

    base = A_Q + 2 * A_KV
    for hk in range(A_KV_HEADS):
        dst = pl.ds(hk, SEQ, stride=A_KV_HEADS)
        ak_ref[0, dst, :] = qkv_ref[:, A_Q + hk * HEAD_DIM:A_Q + (hk + 1) * HEAD_DIM]
        av_ref[0, dst, :] = qkv_ref[:, A_Q + A_KV + hk * HEAD_DIM:A_Q + A_KV + (hk + 1) * HEAD_DIM]
    for h in range(B_HEADS):
        dst = pl.ds(h, SEQ, stride=B_HEADS)
        bk_ref[0, dst, :] = qkv_ref[:, base + B_W + h * HEAD_DIM:base + B_W + (h + 1) * HEAD_DIM]
        bv_ref[0, dst, :] = qkv_ref[:, base + 2 * B_W + h * HEAD_DIM:base + 2 * B_W + (h + 1) * HEAD_DIM]

    k_t, v_t = tile(A_Q, 0), tile(A_Q + A_KV, 0)
    k_sw, v_sw = pltpu.roll(k_t, HEAD_DIM, axis=1), pltpu.roll(v_t, HEAD_DIM, axis=1)
    tiles_per_kv = A_GROUP // HEADS_PER_TILE
    for hk in range(A_KV_HEADS):
        q_tiles = [tile(0, hk * tiles_per_kv + j) for j in range(tiles_per_kv)]
        halves = []
        for p in range(HEADS_PER_TILE):
            q = jnp.concatenate([jnp.where(in_half[p], qt, 0.0) for qt in q_tiles], axis=0)
            heads = [(hk * tiles_per_kv + j) * HEADS_PER_TILE + p for j in range(tiles_per_kv)]
            sink = jnp.concatenate([jnp.broadcast_to(sink_ref[:, h:h + 1], (SEQ, 1)) for h in heads], axis=0)
            halves.append(attend(q, k_t if p == hk else k_sw, v_t if p == hk else v_sw, sink))
        first_half = lax.broadcasted_iota(jnp.int32, halves[0].shape, 1) < HEAD_DIM
        o = jnp.where(first_half, halves[0], halves[1])
        for j in range(tiles_per_kv):
            t = hk * tiles_per_kv + j
            oa_ref[:, t * LANES:(t + 1) * LANES] = o[j * SEQ:(j + 1) * SEQ]

    for t in range(B_HEADS // HEADS_PER_TILE):
        q_t, k_b, v_b = tile(base, t), tile(base + B_W, t), tile(base + 2 * B_W, t)
        halves = [attend(jnp.where(in_half[p], q_t, 0.0), k_b, v_b, None) for p in range(HEADS_PER_TILE)]
        ob_ref[:, t * LANES:(t + 1) * LANES] = jnp.where(in_half[0], halves[0], halves[1])


def _ctx_attn(qkv, sink):
    kv_spec = lambda heads: pl.BlockSpec((1, SEQ * heads, HEAD_DIM), lambda b: (b, 0, 0))
    kv_sd = lambda heads: jax.ShapeDtypeStruct((BATCH, SEQ * heads, HEAD_DIM), F32)
    outs = pl.pallas_call(
        _ctx_attn_body,
        grid=(BATCH,),
        in_specs=[pl.BlockSpec((SEQ, ATTN_IN), lambda b: (b, 0)),
                  pl.BlockSpec((1, A_HEADS), lambda b: (0, 0))],
        out_specs=[pl.BlockSpec((SEQ, A_Q), lambda b: (b, 0)), pl.BlockSpec((SEQ, B_W), lambda b: (b, 0)),
                   kv_spec(A_KV_HEADS), kv_spec(A_KV_HEADS), kv_spec(B_HEADS), kv_spec(B_HEADS)],
        out_shape=[jax.ShapeDtypeStruct((T_CTX, A_Q), F32), jax.ShapeDtypeStruct((T_CTX, B_W), F32),
                   kv_sd(A_KV_HEADS), kv_sd(A_KV_HEADS), kv_sd(B_HEADS), kv_sd(B_HEADS)],
        compiler_params=_cparams("parallel"),
        name="ctx_attn",
    )(qkv, sink.reshape(1, A_HEADS))
    caches = [t.reshape(BATCH, 1, SEQ, -1, HEAD_DIM) for t in outs[2:]]
    return outs[0], outs[1], *caches


@functools.lru_cache(maxsize=None)
def _rope_tables(width):
    half = HEAD_DIM // 2
    t = np.arange(DEC_SEQ)
    inv = ROPE_BASE ** (-np.arange(0, half, 2, dtype=np.float64) / half)
    ang_r = (t // GRID_W)[:, None] * inv[None, :]
    ang_c = (t % GRID_W)[:, None] * inv[None, :]
    cos = np.concatenate([np.cos(ang_r)] * 2 + [np.cos(ang_c)] * 2, axis=-1)
    sin = np.concatenate([-np.sin(ang_r), np.sin(ang_r), -np.sin(ang_c), np.sin(ang_c)], axis=-1)
    reps = width // HEAD_DIM
    return (np.tile(cos, (1, reps)).astype(np.float32), np.tile(sin, (1, reps)).astype(np.float32))


def _rope_body(q_ref, k_ref, cq_ref, sq_ref, ck_ref, sk_ref, qo_ref, ko_ref):
    quarter = HEAD_DIM // 4

    def rot(x, cos, sin):
        w = x.shape[-1]
        lane = lax.broadcasted_iota(jnp.int32, x.shape, 1)
        fwd = pltpu.roll(x, w - quarter, axis=1)
        bwd = pltpu.roll(x, quarter, axis=1)
        partner = jnp.where((lane & (2 * quarter - 1)) < quarter, fwd, bwd)
        return x * cos + partner * sin

    qo_ref[...] = rot(q_ref[...], cq_ref[...], sq_ref[...])
    ko_ref[...] = rot(k_ref[...], ck_ref[...], sk_ref[...])


def _rope(qkv):
    cq, sq = _rope_tables(A_Q)
    ck, sk = _rope_tables(A_KV)
    tab = lambda w: pl.BlockSpec((DEC_SEQ, w), lambda b: (0, 0))
    row0 = T_CTX // DEC_SEQ
    return pl.pallas_call(
        _rope_body,
        grid=(DEC_BATCH,),
        in_specs=[pl.BlockSpec((DEC_SEQ, A_Q), lambda b: (row0 + b, 0)),
                  pl.BlockSpec((DEC_SEQ, A_KV), lambda b: (row0 + b, A_Q // A_KV)),
                  tab(A_Q), tab(A_Q), tab(A_KV), tab(A_KV)],
        out_specs=[pl.BlockSpec((DEC_SEQ, A_Q), lambda b: (b, 0)),
                   pl.BlockSpec((DEC_SEQ, A_KV), lambda b: (b, 0))],
        out_shape=[jax.ShapeDtypeStruct((T_LAT, A_Q), F32), jax.ShapeDtypeStruct((T_LAT, A_KV), F32)],
        compiler_params=_cparams("parallel"),
        name="rope",
    )(qkv, qkv, jnp.asarray(cq), jnp.asarray(sq), jnp.asarray(ck), jnp.asarray(sk))


WIN_QB = 256


def _pick_head(x, h, n_heads):
    out = x[:, _head_cols(0)]
    for i in range(1, n_heads):
        out = jnp.where(h == i, x[:, _head_cols(i)], out)
    return out


def _win_attn_body(qraw_ref, qrot_ref, krot_ref, v_ref, kc_ref, vc_ref, sink_ref, o_ref):
    scale = HEAD_DIM ** -0.5
    hk = pl.program_id(1)
    tiles = A_GROUP // HEADS_PER_TILE

    def kv_in_half(x):
        swapped = pltpu.roll(x, HEAD_DIM, axis=1)
        return [jnp.where(hk == p, x, swapped) for p in range(HEADS_PER_TILE)]

    k, v, kc, vc = kv_in_half(krot_ref[...]), kv_in_half(v_ref[...]), kv_in_half(kc_ref[0]), kv_in_half(vc_ref[0])
    head_lane = lax.broadcasted_iota(jnp.int32, (1, A_HEADS), 1)

    def sink_rows(p):
        heads = [hk * A_GROUP + j * HEADS_PER_TILE + p for j in range(tiles)]
        vals = [jnp.sum(jnp.where(head_lane == h, sink_ref[...], 0.0), axis=-1, keepdims=True) for h in heads]
        return jnp.concatenate([jnp.broadcast_to(s, (WIN_QB, 1)) for s in vals], axis=0)

    sinks = [sink_rows(p) for p in range(HEADS_PER_TILE)]
    lane = lax.broadcasted_iota(jnp.int32, (tiles * WIN_QB, LANES), 1)
    in_half = [lane < HEAD_DIM, lane >= HEAD_DIM]
    for qb in range(DEC_SEQ // WIN_QB):
        q0 = qb * WIN_QB
        rows = slice(q0, q0 + WIN_QB)
        lo = max(0, q0 - WINDOW)
        hi = min(DEC_SEQ, q0 + WIN_QB + WINDOW)
        q_rot = jnp.concatenate([qrot_ref[rows, j * LANES:(j + 1) * LANES] for j in range(tiles)], axis=0)
        q_raw = jnp.concatenate([qraw_ref[rows, j * LANES:(j + 1) * LANES] for j in range(tiles)], axis=0)
        halves = []
        for p in range(HEADS_PER_TILE):
            s_loc = _dot_nt(jnp.where(in_half[p], q_rot, 0.0), k[p][lo:hi]) * scale
            qpos = q0 + (lax.broadcasted_iota(jnp.int32, s_loc.shape, 0) & (WIN_QB - 1))
            kpos = lo + lax.broadcasted_iota(jnp.int32, s_loc.shape, 1)
            s_loc = jnp.where(jnp.abs(kpos - qpos) <= WINDOW, s_loc, MASK_NEG)
            s_ctx = _dot_nt(jnp.where(in_half[p], q_raw, 0.0), kc[p]) * scale
            m = jnp.maximum(jnp.maximum(jnp.max(s_loc, axis=-1, keepdims=True),
                                        jnp.max(s_ctx, axis=-1, keepdims=True)), sinks[p])
            p_loc = jnp.exp(s_loc - m)
            p_ctx = jnp.exp(s_ctx - m)
            den = (jnp.sum(p_loc, axis=-1, keepdims=True) + jnp.sum(p_ctx, axis=-1, keepdims=True)
                   + jnp.exp(sinks[p] - m))
            halves.append((_dot(p_ctx, vc[p]) + _dot(p_loc, v[p][lo:hi])) / den)
        o = jnp.where(in_half[0], halves[0], halves[1])
        for j in range(tiles):
            o_ref[rows, j * LANES:(j + 1) * LANES] = o[j * WIN_QB:(j + 1) * WIN_QB]


def _win_attn(qkv, q_rot, k_rot, kc, vc, sink):
    row0 = T_CTX // DEC_SEQ
    gw = A_GROUP * HEAD_DIM
    return pl.pallas_call(
        _win_attn_body,
        grid=(DEC_BATCH, A_KV_HEADS),
        in_specs=[pl.BlockSpec((DEC_SEQ, gw), lambda b, h: (row0 + b, h)),
                  pl.BlockSpec((DEC_SEQ, gw), lambda b, h: (b, h)),
                  pl.BlockSpec((DEC_SEQ, A_KV), lambda b, h: (b, 0)),
                  pl.BlockSpec((DEC_SEQ, A_KV), lambda b, h: (row0 + b, (A_Q + A_KV) // A_KV)),
                  pl.BlockSpec((1, PAST_LEN, A_KV), lambda b, h: (b, 0, 0)),
                  pl.BlockSpec((1, PAST_LEN, A_KV), lambda b, h: (b, 0, 0)),
                  pl.BlockSpec((1, A_HEADS), lambda b, h: (0, 0))],
        out_specs=pl.BlockSpec((DEC_SEQ, gw), lambda b, h: (b, h)),
        out_shape=jax.ShapeDtypeStruct((T_LAT, A_Q), F32),
        compiler_params=_cparams("parallel", "parallel"),
        name="win_attn",
    )(qkv, q_rot, k_rot, qkv, kc, vc, sink.reshape(1, A_HEADS))


GRID_ROWS = DEC_SEQ // GRID_W
NA_BAND = min(NA_ROWS, GRID_ROWS)


NA_REL_ROWS = 2 * NA_ROWS - 1
NA_REL_COLS = 2 * NA_COLS - 1
LANES = 128
HEADS_PER_TILE = LANES // HEAD_DIM


def _na_rel_rows(rpb):
    pad = jnp.zeros((B_HEADS, NA_REL_ROWS, GRID_W - NA_REL_COLS), F32)
    one = jnp.concatenate([rpb, pad], axis=-1)
    nxt = jnp.concatenate([one[:, 1:], jnp.zeros((B_HEADS, 1, GRID_W), F32)], axis=1)
    both = jnp.concatenate([one, nxt], axis=-1)
    return jnp.concatenate([both, jnp.zeros((B_HEADS, 16 - NA_REL_ROWS, LANES), F32)], axis=1)


NA_HEADS_PER_STEP = LANES // HEAD_DIM


def _na_row_groups():
    groups = []
    for r in range(GRID_ROWS):
        rs = min(max(r - NA_ROWS // 2, 0), GRID_ROWS - NA_BAND)
        if groups and groups[-1][2] == rs:
            groups[-1][1] += 1
        else:
            groups.append([r, 1, rs])
    return groups


def _na_attn_body(q_ref, k_ref, v_ref, kc_ref, vc_ref, rel_ref, o_ref):
    scale = HEAD_DIM ** -0.5
    cq = lax.broadcasted_iota(jnp.int32, (GRID_W, LANES), 0)
    kcol = lax.broadcasted_iota(jnp.int32, (GRID_W, LANES), 1) & (GRID_W - 1)
    cs = jnp.clip(cq - NA_COLS // 2, 0, GRID_W - NA_COLS)
    col_ok = (kcol >= cs) & (kcol < cs + NA_COLS)
    kc = kc_ref[0]
    vc = vc_ref[0]
    tiles = {}

    def pair_tile(hh, a):
        if (hh, a) not in tiles:
            x = jnp.broadcast_to(rel_ref[hh, a:a + 1, :], (GRID_W, LANES))
            t = pltpu.roll(x, LANES - (NA_COLS - 1), axis=1, stride=1, stride_axis=0)
            tiles[hh, a] = jnp.where(col_ok, t, MASK_NEG)
        return tiles[hh, a]

    for r0, n_r, rs in _na_row_groups():
        rows = slice(r0 * GRID_W, (r0 + n_r) * GRID_W)
        band = slice(rs * GRID_W, (rs + NA_BAND) * GRID_W)
        q_t, k_t, v_t = q_ref[rows, :], k_ref[band, :], v_ref[band, :]
        head_of_lane = lax.broadcasted_iota(jnp.int32, q_t.shape, 1) >> (HEAD_DIM.bit_length() - 1)
        o = jnp.zeros(q_t.shape, F32)
        for hh in range(NA_HEADS_PER_STEP):
            bias = jnp.concatenate(
                [jnp.concatenate([pair_tile(hh, rs - r + NA_ROWS - 1 + 2 * i) for i in range(NA_BAND // 2)], axis=1)
                 for r in range(r0, r0 + n_r)], axis=0)
            q = jnp.where(head_of_lane == hh, q_t, 0.0)
            s_loc = _dot_nt(q, k_t) * scale + bias
            s_ctx = _dot_nt(q, kc) * scale
            m = jnp.maximum(jnp.max(s_loc, axis=-1, keepdims=True), jnp.max(s_ctx, axis=-1, keepdims=True))
            p_loc = jnp.exp(s_loc - m)
            p_ctx = jnp.exp(s_ctx - m)
            den = jnp.sum(p_loc, axis=-1, keepdims=True) + jnp.sum(p_ctx, axis=-1, keepdims=True)
            o = jnp.where(head_of_lane == hh, (_dot(p_ctx, vc) + _dot(p_loc, v_t)) / den, o)
        o_ref[rows, :] = o


def _na_attn(qkv, kc, vc, rel):
    row0 = T_CTX // DEC_SEQ
    col0 = (A_Q + 2 * A_KV) // LANES
    n_blk = B_W // LANES
    col = lambda j: pl.BlockSpec((DEC_SEQ, LANES), lambda b, p: (row0 + b, col0 + j * n_blk + p))
    cache = pl.BlockSpec((1, PAST_LEN, LANES), lambda b, p: (b, 0, p))
    return pl.pallas_call(
        _na_attn_body,
        grid=(DEC_BATCH, n_blk),
        in_specs=[col(0), col(1), col(2), cache, cache,
                  pl.BlockSpec((NA_HEADS_PER_STEP, 16, LANES), lambda b, p: (p, 0, 0))],
        out_specs=pl.BlockSpec((DEC_SEQ, LANES), lambda b, p: (b, p)),
        out_shape=jax.ShapeDtypeStruct((T_LAT, B_W), F32),
        compiler_params=_cparams("parallel", "parallel"),
        name="na_attn",
    )(qkv, qkv, qkv, kc, vc, rel)


@functools.lru_cache(maxsize=None)
def _dft_mats(L):
    n = 2 * L
    fc = min(L, DFT_CHUNK)
    f = np.arange(L)[:, None]
    t = np.arange(L)[None, :]
    ang = 2.0 * np.pi * ((f * t) % n) / n
    m1 = np.cos(ang)
    m2 = np.sin(ang)
    m2[0, :] = np.where(np.arange(L) % 2 == 0, 1.0, -1.0)
    wgt = np.full((L, 1), 2.0)
    wgt[0, 0] = 1.0
    nch = L // fc
    fwd = np.concatenate([m1.reshape(nch, fc, L), m2.reshape(nch, fc, L)], axis=1)
    inv = np.concatenate([(m1 * wgt / n).reshape(nch, fc, L), (m2 * wgt / n).reshape(nch, fc, L)], axis=1)
    inv = np.transpose(inv, (0, 2, 1))
    return fwd.astype(np.float32), inv.astype(np.float32)


@functools.lru_cache(maxsize=None)
def _filter_consts(L):
    t = np.linspace(0.0, 1.0, L)[:, None]
    bands = (C_EMB - 1) // 2
    ang = (2.0 * math.pi / L) * np.arange(L)[:, None] * np.linspace(1e-4, bands - 1, bands)[None, :]
    z = np.concatenate([t, np.cos(ang), -np.sin(ang)], axis=-1)
    zpad = np.zeros((L, 128))
    zpad[:, :C_EMB] = z
    deltas = np.abs(np.linspace(HYENA_MIN_DECAY, HYENA_MAX_DECAY, C_DIM))
    window = np.exp(-t * deltas[None, :])
    return zpad.astype(np.float32), window.astype(np.float32)


def _filter_body(z_ref, w1_ref, b1_ref, w2_ref, b2_ref, w3_ref, b3_ref, fr_ref, w4_ref, win_ref, fm_ref,
                 hr_ref, g_ref, hq_ref, hs_scr, hd_scr):
    c = pl.program_id(0)
    fc = hr_ref.shape[0]

    @pl.when(c == 0)
    def _():
        fr = fr_ref[...]
        hh = jnp.sin(fr * (_dot_hi(z_ref[...], w1_ref[...]) + b1_ref[...]))
        hh = jnp.sin(fr * (_dot_hi(hh, w2_ref[...]) + b2_ref[...]))
        hh = jnp.sin(fr * (_dot_hi(hh, w3_ref[...]) + b3_ref[...]))
        hh = _dot_hi(hh, w4_ref[...])
        hf = hh[:, :C_DIM] * win_ref[...]
        hb = hh[:, C_DIM:] * win_ref[...]
        hs_scr[...] = hf + hb
        hd_scr[...] = hf - hb

    fm = fm_ref[0]
    hr = _dot_split(fm[:fc], hs_scr[...])
    first = (lax.broadcasted_iota(jnp.int32, (fc, C_DIM), 0) == 0) & (c == 0)
    hr_ref[...] = hr
    g_ref[...] = jnp.where(first, 0.0, _dot_split(fm[fc:], hd_scr[...]))
    hs = hs_scr[...]
    sign = jnp.where((lax.broadcasted_iota(jnp.int32, hs.shape, 0) & 1) == 0, 1.0, -1.0)
    hq_ref[...] = jnp.where(first, jnp.sum(hs * sign, axis=0, keepdims=True), hr)


def _hyena_filter(L, filt):
    w1, b1, w2, b2, w3, b3, freq, w4 = filt
    zpad, window = _filter_consts(L)
    fwd, _ = _dft_mats(L)
    nch, fc2, _ = fwd.shape
    fc = fc2 // 2
    w1p = jnp.pad(w1, ((0, 128 - C_EMB), (0, 0)))
    full = lambda shape: pl.BlockSpec(shape, lambda c: tuple(0 for _ in shape))
    out_spec = pl.BlockSpec((fc, C_DIM), lambda c: (c, 0))
    out_sd = jax.ShapeDtypeStruct((L, C_DIM), F32)
    return pl.pallas_call(
        _filter_body,
        grid=(nch,),
        in_specs=[full((L, 128)), full((128, C_FFN)), full((1, C_FFN)), full((C_FFN, C_FFN)), full((1, C_FFN)),
                  full((C_FFN, C_FFN)), full((1, C_FFN)), full((1, C_FFN)), full((C_FFN, 2 * C_DIM)),
                  full((L, C_DIM)), pl.BlockSpec((1, fc2, L), lambda c: (c, 0, 0))],
        out_specs=[out_spec, out_spec, out_spec],
        out_shape=[out_sd, out_sd, out_sd],
        scratch_shapes=[pltpu.VMEM((L, C_DIM), F32), pltpu.VMEM((L, C_DIM), F32)],
        compiler_params=_cparams("arbitrary"),
        name="hyena_filter",
    )(jnp.asarray(zpad), w1p, b1.reshape(1, C_FFN), w2, b2.reshape(1, C_FFN), w3, b3.reshape(1, C_FFN),
      freq.reshape(1, C_FFN), w4, jnp.asarray(window), jnp.asarray(fwd))


def _hyena_body(u_ref, cw_ref, cb_ref, d_ref, fm_ref, fi_ref, hr_ref, g_ref, hq_ref, y_ref,
                x0_scr, z_scr, acc_scr):
    c = pl.program_id(1)
    L = y_ref.shape[0]
    fc = hr_ref.shape[0]

    @pl.when(c == 0)
    def _():
        row = lax.broadcasted_iota(jnp.int32, (L, C_DIM), 0)

        def short_conv(sec):
            cols = slice(sec * C_DIM, (sec + 1) * C_DIM)
            u = u_ref[:, cols]
            prev = jnp.where(row == 0, 0.0, pltpu.roll(u, 1, axis=0))
            nxt = jnp.where(row == L - 1, 0.0, pltpu.roll(u, L - 1, axis=0))
            return (prev * cw_ref[0:1, cols] + u * cw_ref[1:2, cols] + nxt * cw_ref[2:3, cols]
                    + cb_ref[:, cols])

        x0_scr[...] = short_conv(0)
        z_scr[...] = short_conv(1) * short_conv(2)
        acc_scr[...] = jnp.zeros((L, C_DIM), F32)

    ab = _dot_split(fm_ref[0], z_scr[...])
    a, b = ab[:fc], ab[fc:]
    hr, g, hq = hr_ref[...], g_ref[...], hq_ref[...]
    pq = jnp.concatenate([a * hr - b * g, a * g + b * hq], axis=0)
    acc_scr[...] += _dot_split(fi_ref[0], pq)

    @pl.when(c == pl.num_programs(1) - 1)
    def _():
        y_ref[...] = x0_scr[...] * (acc_scr[...] + z_scr[...] * d_ref[...])


def _hyena(u, row_blk0, n_seq, L, conv_w, conv_b, d_skip, spec):
    hr, g, hq = spec
    fwd, inv = _dft_mats(L)
    nch, fc2, _ = fwd.shape
    fc = fc2 // 2
    u_w = 3 * C_DIM
    return pl.pallas_call(
        _hyena_body,
        grid=(n_seq, nch),
        in_specs=[pl.BlockSpec((L, u_w), lambda b, c: (row_blk0 + b, 0)),
                  pl.BlockSpec((3, u_w), lambda b, c: (0, 0)),
                  pl.BlockSpec((1, u_w), lambda b, c: (0, 0)),
                  pl.BlockSpec((1, C_DIM), lambda b, c: (0, 0)),
                  pl.BlockSpec((1, fc2, L), lambda b, c: (c, 0, 0)),
                  pl.BlockSpec((1, L, fc2), lambda b, c: (c, 0, 0)),
                  pl.BlockSpec((fc, C_DIM), lambda b, c: (c, 0)),
                  pl.BlockSpec((fc, C_DIM), lambda b, c: (c, 0)),
                  pl.BlockSpec((fc, C_DIM), lambda b, c: (c, 0))],
        out_specs=pl.BlockSpec((L, C_DIM), lambda b, c: (b, 0)),
        out_shape=jax.ShapeDtypeStruct((n_seq * L, C_DIM), F32),
        scratch_shapes=[pltpu.VMEM((L, C_DIM), F32)] * 3,
        compiler_params=_cparams("parallel", "arbitrary"),
        name="hyena",
    )(u, conv_w, conv_b.reshape(1, u_w), d_skip.reshape(1, C_DIM), jnp.asarray(fwd), jnp.asarray(inv), hr, g, hq)


def _hgrn_body(q_ref, ff_ref, fb_ref, i_ref, g_ref, lbf_ref, lbb_ref, nd_ref, s0f_ref, s0b_ref,
               o_ref, sf_ref, sb_ref, *, layer):
    L = o_ref.shape[0]
    C = GLA_CHUNK
    S = min(L, GLA_SPAN)
    nc = S // C
    n_span = L // S
    mid = C // 2
    def lower_bound(gm):
        e = jnp.exp(gm - jnp.max(gm, axis=0, keepdims=True))
        p = e / jnp.sum(e, axis=0, keepdims=True)
        return jnp.sum(p[0:layer + 1], axis=0, keepdims=True) - p[0:1]

    def gates(fx, lb):
        f = lb + (1.0 - lb) * jax.nn.sigmoid(fx)
        return 1.0 - f, jnp.log(f)


    chunk_shift = C.bit_length() - 1
    block_shift = D_KDIM.bit_length() - 1
    ti = lax.broadcasted_iota(jnp.int32, (S, S), 0)
    si = lax.broadcasted_iota(jnp.int32, (S, S), 1)
    same_chunk = (ti >> chunk_shift) == (si >> chunk_shift)
    causal = same_chunk & (si <= ti)
    anti = same_chunk & (si >= ti)
    row_chunk = lax.broadcasted_iota(jnp.int32, (S, nc * D_KDIM), 0) >> chunk_shift
    col_chunk = lax.broadcasted_iota(jnp.int32, (S, nc * D_KDIM), 1) >> block_shift
    own_block = row_chunk == col_chunk

    def spread(x):
        return jnp.where(own_block, jnp.concatenate([x] * nc, axis=1), 0.0)

    def chunk_cumsum(mask, lg):
        tri = mask.astype(BF16)
        hi = lg.astype(BF16)
        r1 = lg - hi.astype(F32)
        mid_t = r1.astype(BF16)
        lo = (r1 - mid_t.astype(F32)).astype(BF16)
        dot = lambda t: jnp.dot(tri, t, preferred_element_type=F32)
        return dot(hi) + dot(mid_t) + dot(lo)

    def per_chunk_rows(b, pos):
        return jnp.concatenate([jnp.broadcast_to(b[n * C + pos:n * C + pos + 1], (C, D_KDIM)) for n in range(nc)],
                               axis=0)

    def one_head(q, v, kf, lgf, kb, lgb, st_f, st_b):
        local = []
        for u in range(n_span):
            rows = slice(u * S, (u + 1) * S)
            qs, vs, kfs, kbs = q[rows], v[rows], kf[rows], kb[rows]
            lgs = jnp.concatenate([lgf[rows], lgb[rows]], axis=1)
            pre = chunk_cumsum(causal, lgs)
            b_f = pre[:, :D_KDIM]
            pre_b = pre[:, D_KDIM:]
            b_b = per_chunk_rows(pre_b, C - 1) - pre_b + lgb[rows]
            ref_f, ref_b = per_chunk_rows(b_f, mid), per_chunk_rows(b_b, mid)
            sc = (jnp.where(causal, _dot_nt(qs * jnp.exp(b_f - ref_f), kfs * jnp.exp(ref_f - b_f)), 0.0)
                  + jnp.where(anti, _dot_nt(qs * jnp.exp(b_b - ref_b), kbs * jnp.exp(ref_b - b_b)), 0.0))
            k_out = jnp.concatenate([kfs * jnp.exp(per_chunk_rows(b_f, C - 1) - b_f),
                                     kbs * jnp.exp(per_chunk_rows(b_b, 0) - b_b)], axis=1)
            kv_t = _dot_tn(spread(vs), k_out)
            local.append((_dot(sc, vs), kv_t, b_f, b_b, qs))

        states_f = [[None] * nc for _ in range(n_span)]
        for u in range(n_span):
            _, kv_t, b_f, _, _ = local[u]
            for n in range(nc):
                states_f[u][n] = st_f
                st_f = st_f * jnp.exp(b_f[n * C + C - 1:n * C + C]) + kv_t[n * D_VDIM:(n + 1) * D_VDIM, :D_KDIM]
        states_b = [[None] * nc for _ in range(n_span)]
        for u in reversed(range(n_span)):
            _, kv_t, _, b_b, _ = local[u]
            for n in reversed(range(nc)):
                states_b[u][n] = st_b
                st_b = st_b * jnp.exp(b_b[n * C:n * C + 1]) + kv_t[n * D_VDIM:(n + 1) * D_VDIM, D_KDIM:]

        outs = []
        for u in range(n_span):
            intra, _, b_f, b_b, qs = local[u]
            q_in = jnp.concatenate([spread(qs * jnp.exp(b_f)), spread(qs * jnp.exp(b_b))], axis=1)
            outs.append(intra + _dot_nt(q_in, jnp.concatenate(states_f[u] + states_b[u], axis=1)))
        return (jnp.concatenate(outs, axis=0) if n_span > 1 else outs[0]), st_f, st_b

    for hh in range(o_ref.shape[1] // D_VDIM):
        cols = slice(hh * D_KDIM, (hh + 1) * D_KDIM)
        kf, lgf = gates(ff_ref[:, cols], lower_bound(lbf_ref[:, cols]))
        kb, lgb = gates(fb_ref[:, cols], lower_bound(lbb_ref[:, cols]))
        o, st_f, st_b = one_head(_silu(q_ref[:, cols]), i_ref[:, cols], kf, lgf, kb, lgb,
                                 jnp.transpose(s0f_ref[0, hh]), jnp.transpose(s0b_ref[0, hh]))
        sf_ref[0, hh] = jnp.transpose(st_f)
        sb_ref[0, hh] = jnp.transpose(st_b)
        o_ref[:, cols] = _rms(o, nd_ref[...]) * _silu(g_ref[:, cols])


def _hgrn(u, row_blk0, n_seq, L, lb_fwd, lb_bwd, norm_d, s0f, s0b, layer):
    hps = HGRN_HEADS_PER_STEP
    width = hps * D_KDIM
    col0 = 3 * C_DIM // width
    groups = D_HEADS // hps
    col = lambda j: pl.BlockSpec((L, width), lambda b, h: (row_blk0 + b, col0 + j * groups + h))
    lbs = pl.BlockSpec((DEPTH, width), lambda b, h: (0, h))
    st = pl.BlockSpec((1, hps, D_KDIM, D_VDIM), lambda b, h: (b, h, 0, 0))
    st_sd = jax.ShapeDtypeStruct((n_seq, D_HEADS, D_KDIM, D_VDIM), F32)
    return pl.pallas_call(
        functools.partial(_hgrn_body, layer=layer),
        grid=(n_seq, groups),
        in_specs=[col(0), col(1), col(2), col(3), col(4), lbs, lbs,
                  pl.BlockSpec((1, D_VDIM), lambda b, h: (0, 0)), st, st],
        out_specs=[pl.BlockSpec((L, width), lambda b, h: (b, h)), st, st],
        out_shape=[jax.ShapeDtypeStruct((n_seq * L, D_HEADS * D_VDIM), F32), st_sd, st_sd],
        compiler_params=_cparams("parallel", "parallel"),
        name="hgrn",
    )(u, u, u, u, u, lb_fwd, lb_bwd, norm_d.reshape(1, D_VDIM), s0f, s0b)


def _pack_bf16_pairs(h):
    n = h.shape[1] // 2
    hi = lax.bitcast_convert_type(h[:, :n].astype(BF16).astype(F32), jnp.int32)
    lo = lax.bitcast_convert_type(h[:, n:].astype(BF16).astype(F32), jnp.int32)
    return hi | lax.shift_right_logical(lo, 16)


def _unpack_bf16_pairs(p):
    hi = lax.bitcast_convert_type(p & jnp.int32(-65536), F32).astype(BF16)
    lo = lax.bitcast_convert_type(lax.shift_left(p, 16), F32).astype(BF16)
    return hi, lo


def _outproj_body(*refs, n_x):
    a_refs, b_refs, x_refs = refs[0:2], refs[2:4], refs[4:4 + n_x]
    mod_ref, gf_ref, w_ref, wrh_ref, wrl_ref, rb_ref, x1_ref, h2_ref, chosen_ref, gk_ref, ik_ref = refs[4 + n_x:]
    m = mod_ref[0]
    half = a_refs[0].shape[1]
    out = _dot(_token_tile(a_refs), w_ref[0:half, :]) + _dot(_token_tile(b_refs), w_ref[half:, :])
    x1 = _token_tile(x_refs) + m[:, 2 * D_MODEL:3 * D_MODEL] * out
    x1_ref[...] = x1
    h2 = _rms(x1, gf_ref[...]) * (1.0 + m[:, 4 * D_MODEL:5 * D_MODEL]) + m[:, 3 * D_MODEL:4 * D_MODEL]
    h2_ref[...] = _pack_bf16_pairs(h2)
    h_hi = h2.astype(BF16)
    h_lo = (h2 - h_hi.astype(F32)).astype(BF16)
    logits = _dot_nt(wrh_ref[...], h_hi) + _dot_nt(wrh_ref[...], h_lo) + _dot_nt(wrl_ref[...], h_hi)
    scores = jax.nn.sigmoid(logits)
    work = scores + rb_ref[...]
    expert = lax.broadcasted_iota(jnp.int32, work.shape, 0).astype(F32)
    slot = lax.broadcasted_iota(jnp.int32, (TOP_K, work.shape[1]), 0)
    chosen = [jnp.zeros(work.shape, F32) for _ in range(MOE_REGIONS)]
    gk = jnp.zeros((TOP_K, work.shape[1]), F32)
    ik = jnp.zeros((TOP_K, work.shape[1]), F32)
    for k in range(TOP_K):
        best = jnp.max(work, axis=0, keepdims=True)
        first = jnp.min(jnp.where(work == best, expert, float(N_EXPERTS)), axis=0, keepdims=True)
        hit = expert == first
        chosen[k // K_PER_REGION] = jnp.where(hit, 1.0, chosen[k // K_PER_REGION])
        gk = jnp.where(slot == k, jnp.sum(jnp.where(hit, scores, 0.0), axis=0, keepdims=True), gk)
        ik = jnp.where(slot == k, first, ik)
        work = jnp.where(hit, -jnp.inf, work)
    for r in range(MOE_REGIONS):
        chosen_ref[r] = chosen[r]
    gk_ref[...] = jnp.transpose(gk / jnp.sum(gk, axis=0, keepdims=True) * ROUTE_SCALE)
    ik_ref[...] = ik


def _outproj(a, b, x, mod_l, gain_ffn, w_out, w_router, router_bias):
    half = a[0].shape[1]
    a_specs, a_args = _token_specs(a, half)
    b_specs, b_args = _token_specs(b, half)
    x_specs, x_args = _token_specs(x, D_MODEL)
    wr_t = w_router.T
    wr_hi = wr_t.astype(BF16)
    wr_lo = (wr_t - wr_hi.astype(F32)).astype(BF16)
    return pl.pallas_call(
        functools.partial(_outproj_body, n_x=len(x_args)),
        grid=(T_ALL // TM,),
        in_specs=a_specs + b_specs + x_specs + [
                  pl.BlockSpec((1, 1, N_MOD * D_MODEL), lambda i: (_mod_row(i), 0, 0)),
                  pl.BlockSpec((1, D_MODEL), lambda i: (0, 0)),
                  pl.BlockSpec((2 * half, D_MODEL), lambda i: (0, 0)),
                  pl.BlockSpec((N_EXPERTS, D_MODEL), lambda i: (0, 0)),
                  pl.BlockSpec((N_EXPERTS, D_MODEL), lambda i: (0, 0)),
                  pl.BlockSpec((N_EXPERTS, 1), lambda i: (0, 0))],
        out_specs=[pl.BlockSpec((TM, D_MODEL), lambda i: (i, 0)),
                   pl.BlockSpec((TM, D_MODEL // 2), lambda i: (i, 0)),
                   pl.BlockSpec((MOE_REGIONS, N_EXPERTS, TM), lambda i: (0, 0, i)),
                   pl.BlockSpec((TM, TOP_K), lambda i: (i, 0)),
                   pl.BlockSpec((TOP_K, TM), lambda i: (0, i))],
        out_shape=[jax.ShapeDtypeStruct((T_ALL, D_MODEL), F32),
                   jax.ShapeDtypeStruct((T_ALL, D_MODEL // 2), jnp.int32),
                   jax.ShapeDtypeStruct((MOE_REGIONS, N_EXPERTS, T_ALL), F32),
                   jax.ShapeDtypeStruct((T_ALL, TOP_K), F32),
                   jax.ShapeDtypeStruct((TOP_K, T_ALL), F32)],
        compiler_params=_cparams("parallel"),
        name="outproj_router",
    )(*a_args, *b_args, *x_args, mod_l, gain_ffn.reshape(1, D_MODEL), w_out, wr_hi, wr_lo,
      router_bias.reshape(N_EXPERTS, 1))


def _route_body(chosen_ref, ik_ref, dest_ref, first_ref, count_ref, short_ref, pos_scr):
    n_tiles = T_ALL // TM
    r = lax.broadcasted_iota(jnp.int32, (TM, TM), 0)
    c = lax.broadcasted_iota(jnp.int32, (TM, TM), 1)
    before = (r < c).astype(BF16)

    counts = jnp.zeros((N_EXPERTS, 1), F32)
    for i in range(n_tiles):
        cols = slice(i * TM, (i + 1) * TM)
        m = chosen_ref[0, :, cols]
        pos_scr[:, cols] = jnp.dot(m.astype(BF16), before, preferred_element_type=F32) + counts
        counts = counts + jnp.sum(m, axis=1, keepdims=True)
    padded = jnp.ceil(counts * (1.0 / MOE_BLK)) * MOE_BLK
    ei = lax.broadcasted_iota(jnp.int32, (N_EXPERTS, N_EXPERTS), 0)
    ej = lax.broadcasted_iota(jnp.int32, (N_EXPERTS, N_EXPERTS), 1)
    end = _dot_hi((ej <= ei).astype(F32), jnp.broadcast_to(padded, (N_EXPERTS, LANES)))[:, 0:1]
    start = end - padded

    expert = lax.broadcasted_iota(jnp.int32, (N_EXPERTS, TM), 0).astype(F32)
    slot = lax.broadcasted_iota(jnp.int32, (K_PER_REGION, TM), 0)
    for i in range(n_tiles):
        cols = slice(i * TM, (i + 1) * TM)
        row_of = pos_scr[:, cols] + start
        ik = ik_ref[0, :, cols]
        acc = jnp.zeros((K_PER_REGION, TM), F32)
        for k in range(K_PER_REGION):
            pick = jnp.sum(jnp.where(expert == ik[k:k + 1, :], row_of, 0.0), axis=0, keepdims=True)
            acc = jnp.where(slot == k, pick, acc)
        dest_ref[0, :, cols] = acc.astype(jnp.int32)
    first_ref[0] = jnp.broadcast_to(start * (1.0 / MOE_BLK), (N_EXPERTS, LANES)).astype(jnp.int32)
    count_ref[0] = jnp.broadcast_to(padded * (1.0 / MOE_BLK), (N_EXPERTS, LANES)).astype(jnp.int32)
    in_last = counts - (padded - MOE_BLK)
    short = jnp.where((counts > 0.0) & (in_last <= MOE_BLK // 2), 1.0, 0.0)
    short_ref[0] = jnp.broadcast_to(short, (N_EXPERTS, LANES)).astype(jnp.int32)


def _route(chosen, ik):
    per_region = lambda rows, cols: pl.BlockSpec((1, rows, cols), lambda r: (r, 0, 0))
    table = jax.ShapeDtypeStruct((MOE_REGIONS, N_EXPERTS, LANES), jnp.int32)
    return pl.pallas_call(
        _route_body,
        grid=(MOE_REGIONS,),
        in_specs=[per_region(N_EXPERTS, T_ALL), per_region(K_PER_REGION, T_ALL)],
        out_specs=[per_region(K_PER_REGION, T_ALL)] + [per_region(N_EXPERTS, LANES)] * 3,
        out_shape=[jax.ShapeDtypeStruct((MOE_REGIONS, K_PER_REGION, T_ALL), jnp.int32), table, table, table],
        scratch_shapes=[pltpu.VMEM((N_EXPERTS, T_ALL), F32)],
        compiler_params=_cparams("arbitrary"),
        name="moe_route",
    )(chosen, ik.reshape(MOE_REGIONS, K_PER_REGION, T_ALL))


def _sc_worker_id():
    return lax.axis_index("s") * SC_CORES + lax.axis_index("c")


def _sc_dispatch(h2p, dest):
    n_chunks = T_ALL // DISP_CHUNK
    k_per = dest.shape[0] // DISP_SPLIT
    items_per_worker = n_chunks * DISP_SPLIT // SC_WORKERS
    chunk_stride = SC_WORKERS // DISP_SPLIT
    width = h2p.shape[1]
    mesh = plsc.VectorSubcoreMesh(core_axis_name="c", subcore_axis_name="s")

    @functools.partial(
        pl.kernel, mesh=mesh,
        out_type=jax.ShapeDtypeStruct((MOE_ROWS, width), jnp.int32),
        scratch_types=[pltpu.VMEM((k_per, DISP_CHUNK), jnp.int32), pltpu.VMEM((DISP_CHUNK, width), jnp.int32),
                       pltpu.SemaphoreType.DMA],
    )
    def run(x_hbm, dest_hbm, xs_hbm, idx_v, rows_v, sem):
        wid = _sc_worker_id()
        group = wid % DISP_SPLIT
        for i in range(items_per_worker):
            chunk = i * chunk_stride + wid // DISP_SPLIT
            tokens = pl.ds(pl.multiple_of(chunk * DISP_CHUNK, DISP_CHUNK), DISP_CHUNK)
            pltpu.sync_copy(dest_hbm.at[group, :, tokens], idx_v)
            pltpu.sync_copy(x_hbm.at[tokens], rows_v)
            scatters = [pltpu.make_async_copy(rows_v, xs_hbm.at[idx_v.at[k]], sem) for k in range(k_per)]
            for cp in scatters:
                cp.start()
            for cp in scatters:
                cp.wait()

    return run(h2p, dest.reshape(DISP_SPLIT, k_per, T_ALL))


def _sc_collect(y, dest_flat):
    n_k = dest_flat.shape[0] // T_ALL
    per_worker = T_ALL // SC_WORKERS
    n_chunks = per_worker // COLLECT_CHUNK
    n_steps = n_k * n_chunks
    width = y.shape[1]
    mesh = plsc.VectorSubcoreMesh(core_axis_name="c", subcore_axis_name="s")

    @functools.partial(
        pl.kernel, mesh=mesh,
        out_type=jax.ShapeDtypeStruct((n_k * T_ALL, width), y.dtype),
        scratch_types=[pltpu.VMEM((n_k * per_worker,), jnp.int32),
                       pltpu.VMEM((COLLECT_CHUNK, width), y.dtype), pltpu.VMEM((COLLECT_CHUNK, width), y.dtype),
                       pltpu.SemaphoreType.DMA, pltpu.SemaphoreType.DMA],
    )
    def run(y_hbm, dest_hbm, yg_hbm, idx_v, rows0, rows1, sem0, sem1):
        wid = _sc_worker_id()
        bufs = ((rows0, sem0), (rows1, sem1))
        for k in range(n_k):
            pltpu.sync_copy(dest_hbm.at[pl.ds(k * T_ALL + wid * per_worker, per_worker)],
                            idx_v.at[pl.ds(k * per_worker, per_worker)])

        def gather(step, buf):
            rows, sem = buf
            idx = idx_v.at[pl.ds(pl.multiple_of(step * COLLECT_CHUNK, 8), COLLECT_CHUNK)]
            return pltpu.make_async_copy(y_hbm.at[idx], rows, sem)

        def out_rows(step):
            off = (step // n_chunks) * T_ALL + wid * per_worker + (step % n_chunks) * COLLECT_CHUNK
            return yg_hbm.at[pl.ds(pl.multiple_of(off, 8), COLLECT_CHUNK)]

        gather(0, bufs[0]).start()

        @pl.loop(0, n_steps, step=2)
        def _(base):
            for j in range(2):
                step = base + j

                @pl.when(step + 1 < n_steps)
                def _():
                    gather(step + 1, bufs[1 - j]).start()

                gather(step, bufs[j]).wait()
                pltpu.sync_copy(bufs[j][0], out_rows(step))

    return run(y, dest_flat)


def _expert_body(first_ref, count_ref, short_ref, xs_hbm, wg_hbm, wu_hbm, wd_hbm, y_hbm,
                 wg_f32, wu_f32, wd_f32, wg_bf, wu_bf, wd_bf, x_buf, y_buf, w_sem, in_sem, out_sem, *, layer):
    e = pl.program_id(0)
    first = first_ref[e]
    count = count_ref[e]
    n_used = first_ref[N_EXPERTS - 1] + count_ref[N_EXPERTS - 1]
    half = D_MODEL // 2

    def weight_copies(ex):
        slot = lax.rem(ex, EXPERT_W_SLOTS)
        return [pltpu.make_async_copy(src.at[layer, ex], dst.at[slot], w_sem.at[slot])
                for src, dst in ((wg_hbm, wg_f32), (wu_hbm, wu_f32), (wd_hbm, wd_f32))]

    @pl.when(e == 0)
    def _():
        for ahead in range(EXPERT_W_SLOTS - 1):
            for cp in weight_copies(ahead):
                cp.start()

    for cp in weight_copies(e):
        cp.wait()

    @pl.when(e + EXPERT_W_SLOTS - 1 < N_EXPERTS)
    def _():
        for cp in weight_copies(e + EXPERT_W_SLOTS - 1):
            cp.start()

    w_slot = lax.rem(e, EXPERT_W_SLOTS)
    wg_bf[...] = wg_f32[w_slot].astype(BF16)
    wu_bf[...] = wu_f32[w_slot].astype(BF16)
    wd_bf[...] = wd_f32[w_slot].astype(BF16)

    def part_rows(g, part, n_parts):
        size = MOE_BLK // n_parts
        return pl.ds(pl.multiple_of(g * MOE_BLK + part * size, size), size), pl.ds(part * size, size)

    def in_copies(g):
        slot = g & (EXPERT_SLOTS - 1)
        out = []
        for part in range(EXPERT_IN_PARTS):
            src, dst = part_rows(g, part, EXPERT_IN_PARTS)
            out.append(pltpu.make_async_copy(xs_hbm.at[src], x_buf.at[slot, dst], in_sem.at[slot]))
        return out

    def out_copies(g):
        slot = g & (EXPERT_SLOTS - 1)
        out = []
        for part in range(EXPERT_OUT_PARTS):
            dst, src = part_rows(g, part, EXPERT_OUT_PARTS)
            out.append(pltpu.make_async_copy(y_buf.at[slot, src], y_hbm.at[dst], out_sem.at[slot]))
        return out

    @pl.when((first == 0) & (count > 0))
    def _():
        for ahead in range(EXPERT_SLOTS - 1):
            @pl.when(ahead < n_used)
            def _():
                for cp in in_copies(ahead):
                    cp.start()

    def block(b, carry):
        g = first + b
        slot = g & (EXPERT_SLOTS - 1)
        for cp in in_copies(g):
            cp.wait()

        @pl.when(g + EXPERT_SLOTS - 1 < n_used)
        def _():
            for cp in in_copies(g + EXPERT_SLOTS - 1):
                cp.start()

        @pl.when(g >= EXPERT_SLOTS)
        def _():
            for cp in out_copies(g - EXPERT_SLOTS):
                cp.wait()

        def ffn(n_rows):
            hi, lo = _unpack_bf16_pairs(x_buf[slot, 0:n_rows])

            def proj(w_bf):
                return (jnp.dot(hi, w_bf[0:half, :], preferred_element_type=F32)
                        + jnp.dot(lo, w_bf[half:, :], preferred_element_type=F32))

            hid = _silu(proj(wg_bf)) * proj(wu_bf)
            y_buf[slot, 0:n_rows] = _pack_bf16_pairs(
                jnp.dot(hid.astype(BF16), wd_bf[...], preferred_element_type=F32))

        short = (b == count - 1) & (short_ref[e] == 1)

        @pl.when(short)
        def _():
            ffn(MOE_BLK // 2)
            y_buf[slot, MOE_BLK // 2:MOE_BLK] = jnp.zeros((MOE_BLK // 2, D_MODEL // 2), jnp.int32)

        @pl.when(jnp.logical_not(short))
        def _():
            ffn(MOE_BLK)

        for cp in out_copies(g):
            cp.start()
        return carry

    lax.fori_loop(0, count, block, 0)

    @pl.when(e == N_EXPERTS - 1)
    def _():
        for back in range(EXPERT_SLOTS, 0, -1):
            @pl.when(n_used >= back)
            def _():
                for cp in out_copies(n_used - back):
                    cp.wait()


EXPERT_SLOTS = 4
EXPERT_W_SLOTS = 3
EXPERT_IN_PARTS = 2
EXPERT_OUT_PARTS = 4


def _experts(first_blk, n_blk, short_last, xs, layer, w_gate, w_up, w_down):
    anywhere = pl.BlockSpec(memory_space=pl.ANY)
    grid_spec = pltpu.PrefetchScalarGridSpec(
        num_scalar_prefetch=3,
        grid=(N_EXPERTS,),
        in_specs=[anywhere] * 4,
        out_specs=anywhere,
        scratch_shapes=[pltpu.VMEM((EXPERT_W_SLOTS, D_MODEL, D_EXPERT), F32),
                        pltpu.VMEM((EXPERT_W_SLOTS, D_MODEL, D_EXPERT), F32),
                        pltpu.VMEM((EXPERT_W_SLOTS, D_EXPERT, D_MODEL), F32),
                        pltpu.VMEM((D_MODEL, D_EXPERT), BF16), pltpu.VMEM((D_MODEL, D_EXPERT), BF16),
                        pltpu.VMEM((D_EXPERT, D_MODEL), BF16),
                        pltpu.VMEM((EXPERT_SLOTS, MOE_BLK, D_MODEL // 2), jnp.int32),
                        pltpu.VMEM((EXPERT_SLOTS, MOE_BLK, D_MODEL // 2), jnp.int32),
                        pltpu.SemaphoreType.DMA((EXPERT_W_SLOTS,)),
                        pltpu.SemaphoreType.DMA((EXPERT_SLOTS,)), pltpu.SemaphoreType.DMA((EXPERT_SLOTS,))],
    )
    return pl.pallas_call(
        functools.partial(_expert_body, layer=layer),
        grid_spec=grid_spec,
        out_shape=jax.ShapeDtypeStruct((MOE_ROWS, D_MODEL // 2), jnp.int32),
        compiler_params=_cparams("arbitrary"),
        name="moe_experts",
    )(first_blk, n_blk, short_last, xs, w_gate, w_up, w_down)


def _combine_body(x1_ref, h2_ref, *refs, final):
    yg_refs = refs[:MOE_REGIONS]
    gk_ref, mod_ref, sg_ref, su_ref, sd_ref, fn_ref, *o_refs = refs[MOE_REGIONS:]
    hi, lo = _unpack_bf16_pairs(h2_ref[...])
    half = D_MODEL // 2

    def proj(w_ref):
        return _dot(hi, w_ref[0:half, :]) + _dot(lo, w_ref[half:, :])

    shared = _dot(_silu(proj(sg_ref)) * proj(su_ref), sd_ref[...])
    acc_hi, acc_lo = shared[:, :half], shared[:, half:]
    gk = gk_ref[...]
    for k in range(TOP_K):
        y_hi, y_lo = _unpack_bf16_pairs(yg_refs[k // K_PER_REGION][k % K_PER_REGION])
        acc_hi = acc_hi + gk[:, k:k + 1] * y_hi.astype(F32)
        acc_lo = acc_lo + gk[:, k:k + 1] * y_lo.astype(F32)
    acc = jnp.concatenate([acc_hi, acc_lo], axis=1)
    m = mod_ref[0]
    y = x1_ref[...] + m[:, 5 * D_MODEL:6 * D_MODEL] * acc
    if not final:
        o_refs[0][...] = y
        return
    y = _rms(y, fn_ref[...])
    is_ctx = pl.program_id(0) < N_CTX_TILES

    @pl.when(is_ctx)
    def _():
        o_refs[0][...] = y

    @pl.when(jnp.logical_not(is_ctx))
    def _():
        o_refs[1][...] = y


def _combine(x1, h2p, yg, gk, mod_l, ws_gate, ws_up, ws_down, final_norm, final):
    tok = lambda shape: pl.BlockSpec(shape, lambda i: (i, 0))
    full = lambda shape: pl.BlockSpec(shape, lambda i: (0, 0))
    if final:
        out_specs, _ = _token_specs((None, None), D_MODEL)
        out_shape = [jax.ShapeDtypeStruct((T_CTX, D_MODEL), F32), jax.ShapeDtypeStruct((T_LAT, D_MODEL), F32)]
    else:
        out_specs = tok((TM, D_MODEL))
        out_shape = jax.ShapeDtypeStruct((T_ALL, D_MODEL), F32)
    return pl.pallas_call(
        functools.partial(_combine_body, final=final),
        grid=(T_ALL // TM,),
        in_specs=[tok((TM, D_MODEL)), tok((TM, D_MODEL // 2))]
                 + [pl.BlockSpec((K_PER_REGION, TM, D_MODEL // 2), lambda i: (0, i, 0))] * MOE_REGIONS
                 + [tok((TM, TOP_K)),
                  pl.BlockSpec((1, 1, N_MOD * D_MODEL), lambda i: (_mod_row(i), 0, 0)),
                  full((D_MODEL, D_EXPERT)), full((D_MODEL, D_EXPERT)), full((D_EXPERT, D_MODEL)),
                  full((1, D_MODEL))],
        out_specs=out_specs,
        out_shape=out_shape,
        compiler_params=_cparams("arbitrary"),
        name="moe_combine",
    )(x1, h2p, *yg, gk, mod_l, ws_gate, ws_up, ws_down, final_norm.reshape(1, D_MODEL))


def _moe(x1, h2p, chosen, gk, ik, mod_l, layer, w_gate, w_up, w_down, ws_gate, ws_up, ws_down, final_norm, final):
    dest, first_blk, n_blk, short_last = _route(chosen, ik)
    yg = []
    for r in range(MOE_REGIONS):
        xs = _sc_dispatch(h2p, dest[r])
        y = _experts(first_blk[r, :, 0], n_blk[r, :, 0], short_last[r, :, 0], xs, layer, w_gate, w_up, w_down)
        yg.append(_sc_collect(y, dest[r].reshape(-1)).reshape(K_PER_REGION, T_ALL, D_MODEL // 2))
    return _combine(x1, h2p, yg, gk, mod_l, ws_gate, ws_up, ws_down, final_norm, final)


def kernel(x_prompt, x_sample, cache_a_k, cache_a_v, cache_b_k, cache_b_v, state_d_fwd, state_d_bwd, c, c_ctx, w_ada, b_ada, norm_mix, norm_ffn, w_in_attn, w_out_attn, sink_a, rpb_b, w_in_rec, w_out_rec, conv_w, conv_b, filt_w1, filt_b1, filt_w2, filt_b2, filt_w3, filt_b3, filt_freq, filt_w4, d_skip, lb_fwd, lb_bwd, norm_d, w_router, router_bias, w_gate, w_up, w_down, ws_gate, ws_up, ws_down, final_norm):
    x = (x_prompt.reshape(T_CTX, D_MODEL), x_sample.reshape(T_LAT, D_MODEL))
    cvec = jnp.concatenate([c_ctx[None, :], c], axis=0)
    c_lanes = jnp.broadcast_to(cvec[:, :, None], (N_CVEC, D_MODEL, LANES))
    mod = [_ada(c_lanes, l, w_ada, b_ada).reshape(CVEC_PAD, 1, N_MOD * D_MODEL) for l in range(DEPTH)]

    new_kv = None
    new_state = None
    for l in range(DEPTH):
        j = l // 2
        final = l == DEPTH - 1
        if l % 2 == 0:
            qkv = _inproj(x, mod[l], norm_mix[l], w_in_attn[j])
            oa_ctx, ob_ctx, *new_kv = _ctx_attn(qkv, sink_a[j])
            new_kv = tuple(new_kv)
            q_rot, k_rot = _rope(qkv)
            cache = lambda t: t[:, j].reshape(DEC_BATCH, PAST_LEN, -1)
            oa_lat = _win_attn(qkv, q_rot, k_rot, cache(cache_a_k), cache(cache_a_v), sink_a[j])
            ob_lat = _na_attn(qkv, cache(cache_b_k), cache(cache_b_v), _na_rel_rows(rpb_b[j]))
            mix_a = (oa_ctx, oa_lat)
            mix_b = (ob_ctx, ob_lat)
            w_out = w_out_attn[j]
        else:
            u = _inproj(x, mod[l], norm_mix[l], w_in_rec[j])
            filt = (filt_w1[j], filt_b1[j], filt_w2[j], filt_b2[j], filt_w3[j], filt_b3[j], filt_freq[j],
                    filt_w4[j])
            y_ctx = _hyena(u, 0, BATCH, SEQ, conv_w[j], conv_b[j], d_skip[j], _hyena_filter(SEQ, filt))
            y_lat = _hyena(u, T_CTX // DEC_SEQ, DEC_BATCH, DEC_SEQ, conv_w[j], conv_b[j], d_skip[j],
                           _hyena_filter(DEC_SEQ, filt))
            zeros = jnp.zeros((BATCH, D_HEADS, D_KDIM, D_VDIM), F32)
            o_ctx, s_f, s_b = _hgrn(u, 0, BATCH, SEQ, lb_fwd, lb_bwd, norm_d[j], zeros, zeros, l)
            o_lat, _, _ = _hgrn(u, T_CTX // DEC_SEQ, DEC_BATCH, DEC_SEQ, lb_fwd, lb_bwd, norm_d[j],
                                state_d_fwd[:, j], state_d_bwd[:, j], l)
            new_state = (s_f[:, None], s_b[:, None])
            mix_a = (y_ctx, y_lat)
            mix_b = (o_ctx, o_lat)
            w_out = w_out_rec[j]
        x1, h2p, chosen, gk, ik = _outproj(mix_a, mix_b, x, mod[l], norm_ffn[l], w_out, w_router[l],
                                           router_bias[l])
        x = _moe(x1, h2p, chosen, gk, ik, mod[l], l, w_gate, w_up, w_down, ws_gate[l], ws_up[l],
                 ws_down[l], final_norm, final)

    y_prompt = x[0].reshape(BATCH, SEQ, D_MODEL)
    y_sample = x[1].reshape(DEC_BATCH, DEC_SEQ, D_MODEL)
    return (y_prompt, y_sample) + new_kv + new_state
```

```python
import functools
import math

import numpy as np
import jax
import jax.numpy as jnp
from jax import lax
from jax.experimental import pallas as pl
from jax.experimental.pallas import tpu as pltpu
from jax.experimental.pallas import tpu_sc as plsc

F32 = jnp.float32
BF16 = jnp.bfloat16
HI = lax.Precision.HIGHEST

D_MODEL = 1024
BATCH = 16
SEQ = 256
DEPTH = 2
DEC_BATCH = 2
DEC_SEQ = 1024
PAST_LEN = 512
GRID_W = 64
HEAD_DIM = 64
N_MOD = 6
RMS_EPS = 1e-6
A_HEADS = 8
A_KV_HEADS = 2
A_GROUP = A_HEADS // A_KV_HEADS
WINDOW = 128
ROPE_BASE = 10000.0
B_HEADS = 8
NA_ROWS = 8
NA_COLS = 16
C_DIM = 512
C_EMB = 33
C_FFN = 64
HYENA_MIN_DECAY = math.log(1e-2) / 1.5
HYENA_MAX_DECAY = math.log(1e-2) / 0.3
D_KDIM = 128
D_VDIM = 128
D_HEADS = 4
N_EXPERTS = 64
TOP_K = 8
D_EXPERT = 256
ROUTE_SCALE = 2.5
A_Q = A_HEADS * HEAD_DIM
A_KV = A_KV_HEADS * HEAD_DIM
B_W = B_HEADS * HEAD_DIM
ATTN_IN = A_Q + 2 * A_KV + 3 * B_W

T_CTX = BATCH * SEQ
T_LAT = DEC_BATCH * DEC_SEQ
T_ALL = T_CTX + T_LAT
N_CVEC = 1 + DEC_BATCH
CVEC_PAD = 8
TM = 512
MASK_NEG = -1e30
GLA_CHUNK = 64
GLA_SPAN = 256
HGRN_HEADS_PER_STEP = 4
DFT_CHUNK = 256
MOE_BLK = 512
MOE_REGIONS = 1
K_PER_REGION = TOP_K // MOE_REGIONS
MOE_NBLK = -(-(T_ALL * K_PER_REGION + N_EXPERTS * (MOE_BLK - 1)) // MOE_BLK)
MOE_ROWS = MOE_NBLK * MOE_BLK
SC_CORES = 2
SC_SUBCORES = 16
SC_WORKERS = SC_CORES * SC_SUBCORES
DISP_CHUNK = 128
DISP_SPLIT = 2
COLLECT_CHUNK = 64
VMEM_LIMIT = 56 * 1024 * 1024


def _cparams(*sem):
    return pltpu.CompilerParams(dimension_semantics=sem, vmem_limit_bytes=VMEM_LIMIT)


def _mod_row(i):
    return jnp.where(i < T_CTX // TM, 0, 1 + (i - T_CTX // TM) // (DEC_SEQ // TM))


def _dot(a, b):
    return jnp.dot(a.astype(BF16), b.astype(BF16), preferred_element_type=F32)


def _dot_nt(a, b):
    return lax.dot_general(a.astype(BF16), b.astype(BF16), (((1,), (1,)), ((), ())),
                           preferred_element_type=F32)


def _dot_tn(a, b):
    return lax.dot_general(a.astype(BF16), b.astype(BF16), (((0,), (0,)), ((), ())),
                           preferred_element_type=F32)


def _dot_hi(a, b):
    return jnp.dot(a, b, precision=HI, preferred_element_type=F32)


def _split_bf16(x):
    hi = x.astype(BF16)
    return hi, (x - hi.astype(F32)).astype(BF16)


def _dot_split(a, b):
    a_hi, a_lo = _split_bf16(a)
    b_hi, b_lo = _split_bf16(b)
    dot = lambda x, y: jnp.dot(x, y, preferred_element_type=F32)
    return dot(a_hi, b_hi) + dot(a_hi, b_lo) + dot(a_lo, b_hi)


def _silu(x):
    return x * jax.nn.sigmoid(x)


def _rms(x, g):
    return x * lax.rsqrt(jnp.mean(x * x, axis=-1, keepdims=True) + RMS_EPS) * g


ADA_TN = 1536
ADA_UNROLL = 4


def _ada_body(cb_ref, w_ref, b_ref, o_ref):
    tn = o_ref.shape[-1]
    n_slab = tn // LANES

    def step(k8, accs):
        r0 = pl.multiple_of(k8 * 8, 8)
        sk = [_silu(cb_ref[j, pl.ds(r0, 8), :]) for j in range(N_CVEC)]
        out = []
        for s in range(n_slab):
            wk = w_ref[0, pl.ds(r0, 8), s * LANES:(s + 1) * LANES]
            out.extend(accs[s * N_CVEC + j] + wk * sk[j] for j in range(N_CVEC))
        return tuple(out)

    accs = lax.fori_loop(0, D_MODEL // 8, step,
                         tuple(jnp.zeros((8, LANES), F32) for _ in range(n_slab * N_CVEC)), unroll=ADA_UNROLL)
    o_ref[0] = jnp.zeros((CVEC_PAD, tn), F32)
    for s in range(n_slab):
        for j in range(N_CVEC):
            o_ref[0, j:j + 1, s * LANES:(s + 1) * LANES] = (
                jnp.sum(accs[s * N_CVEC + j], axis=0, keepdims=True) + b_ref[0, :, s * LANES:(s + 1) * LANES])


def _ada(c_lanes, layer, w_ada, b_ada):
    n_out = N_MOD * D_MODEL
    return pl.pallas_call(
        _ada_body,
        grid=(n_out // ADA_TN,),
        in_specs=[pl.BlockSpec((N_CVEC, D_MODEL, LANES), lambda n: (0, 0, 0)),
                  pl.BlockSpec((1, D_MODEL, ADA_TN), lambda n: (layer, 0, n)),
                  pl.BlockSpec((1, 1, ADA_TN), lambda n: (layer, 0, n))],
        out_specs=pl.BlockSpec((1, CVEC_PAD, ADA_TN), lambda n: (0, 0, n)),
        out_shape=jax.ShapeDtypeStruct((1, CVEC_PAD, n_out), F32),
        compiler_params=_cparams("parallel"),
        name="ada",
    )(c_lanes, w_ada, b_ada.reshape(DEPTH, 1, n_out))


N_CTX_TILES = T_CTX // TM


def _token_specs(x, width):
    if not isinstance(x, tuple):
        return [pl.BlockSpec((TM, width), lambda i: (i, 0))], (x,)
    return ([pl.BlockSpec((TM, width), lambda i: (jnp.minimum(i, N_CTX_TILES - 1), 0)),
             pl.BlockSpec((TM, width), lambda i: (jnp.maximum(i - N_CTX_TILES, 0), 0))], x)


def _token_tile(refs):
    if len(refs) == 1:
        return refs[0][...]
    return jnp.where(pl.program_id(0) < N_CTX_TILES, refs[0][...], refs[1][...])


def _inproj_body(*refs, n_x):
    x_refs, (mod_ref, g_ref, w_ref, o_ref, w_bf) = refs[:n_x], refs[n_x:]

    @pl.when(pl.program_id(0) == 0)
    def _():
        w_bf[...] = w_ref[...].astype(BF16)

    m = mod_ref[0]
    h = _rms(_token_tile(x_refs), g_ref[...]) * (1.0 + m[:, D_MODEL:2 * D_MODEL]) + m[:, 0:D_MODEL]
    o_ref[...] = _dot(h, w_bf[...])


def _inproj(x, mod_l, gain, w):
    n = w.shape[1]
    x_specs, x_args = _token_specs(x, D_MODEL)
    return pl.pallas_call(
        functools.partial(_inproj_body, n_x=len(x_args)),
        grid=(T_ALL // TM,),
        in_specs=x_specs + [pl.BlockSpec((1, 1, N_MOD * D_MODEL), lambda i: (_mod_row(i), 0, 0)),
                            pl.BlockSpec((1, D_MODEL), lambda i: (0, 0)),
                            pl.BlockSpec((D_MODEL, n), lambda i: (0, 0), pipeline_mode=pl.Buffered(1))],
        out_specs=pl.BlockSpec((TM, n), lambda i: (i, 0)),
        out_shape=jax.ShapeDtypeStruct((T_ALL, n), F32),
        scratch_shapes=[pltpu.VMEM((D_MODEL, n), BF16)],
        compiler_params=_cparams("arbitrary"),
        name="inproj",
    )(*x_args, mod_l, gain.reshape(1, D_MODEL), w)


def _ctx_attn_body(qkv_ref, sink_ref, oa_ref, ob_ref, ak_ref, av_ref, bk_ref, bv_ref):
    scale = HEAD_DIM ** -0.5
    lane = lax.broadcasted_iota(jnp.int32, (SEQ, LANES), 1)
    in_half = [lane < HEAD_DIM, lane >= HEAD_DIM]

    def attend(q, k, v, sink):
        s = _dot_nt(q, k) * scale
        m = jnp.max(s, axis=-1, keepdims=True)
        if sink is not None:
            m = jnp.maximum(m, sink)
        p = jnp.exp(s - m)
        den = jnp.sum(p, axis=-1, keepdims=True)
        if sink is not None:
            den = den + jnp.exp(sink - m)
        return _dot(p, v) / den

    def tile(first_col, t):
        return qkv_ref[:, first_col + t * LANES:first_col + (t + 1) * LANES]

    base = A_Q + 2 * A_KV
    for hk in range(A_KV_HEADS):
        dst = pl.ds(hk, SEQ, stride=A_KV_HEADS)
        ak_ref[0, dst, :] = qkv_ref[:, A_Q + hk * HEAD_DIM:A_Q + (hk + 1) * HEAD_DIM]
        av_ref[0, dst, :] = qkv_ref[:, A_Q + A_KV + hk * HEAD_DIM:A_Q + A_KV + (hk + 1) * HEAD_DIM]
    for h in range(B_HEADS):
        dst = pl.ds(h, SEQ, stride=B_HEADS)
        bk_ref[0, dst, :] = qkv_ref[:, base + B_W + h * HEAD_DIM:base + B_W + (h + 1) * HEAD_DIM]
        bv_ref[0, dst, :] = qkv_ref[:, base + 2 * B_W + h * HEAD_DIM:base + 2 * B_W + (h + 1) * HEAD_DIM]

    k_t, v_t = tile(A_Q, 0), tile(A_Q + A_KV, 0)
    k_sw, v_sw = pltpu.roll(k_t, HEAD_DIM, axis=1), pltpu.roll(v_t, HEAD_DIM, axis=1)
    tiles_per_kv = A_GROUP // HEADS_PER_TILE
    for hk in range(A_KV_HEADS):
        q_tiles = [tile(0, hk * tiles_per_kv + j) for j in range(tiles_per_kv)]
        halves = []
        for p in range(HEADS_PER_TILE):
            q = jnp.concatenate([jnp.where(in_half[p], qt, 0.0) for qt in q_tiles], axis=0)
            heads = [(hk * tiles_per_kv + j) * HEADS_PER_TILE + p for j in range(tiles_per_kv)]
            sink = jnp.concatenate([jnp.broadcast_to(sink_ref[:, h:h + 1], (SEQ, 1)) for h in heads], axis=0)
            halves.append(attend(q, k_t if p == hk else k_sw, v_t if p == hk else v_sw, sink))
        first_half = lax.broadcasted_iota(jnp.int32, halves[0].shape, 1) < HEAD_DIM
        o = jnp.where(first_half, halves[0], halves[1])
        for j in range(tiles_per_kv):
            t = hk * tiles_per_kv + j
            oa_ref[:, t * LANES:(t + 1) * LANES] = o[j * SEQ:(j + 1) * SEQ]

    for t in range(B_HEADS // HEADS_PER_TILE):
        q_t, k_b, v_b = tile(base, t), tile(base + B_W, t), tile(base + 2 * B_W, t)
        halves = [attend(jnp.where(in_half[p], q_t, 0.0), k_b, v_b, None) for p in range(HEADS_PER_TILE)]
        ob_ref[:, t * LANES:(t + 1) * LANES] = jnp.where(in_half[0], halves[0], halves[1])


def _ctx_attn(qkv, sink):
    kv_spec = lambda heads: pl.BlockSpec((1, SEQ * heads, HEAD_DIM), lambda b: (b, 0, 0))
    kv_sd = lambda heads: jax.ShapeDtypeStruct((BATCH, SEQ * heads, HEAD_DIM), F32)
    outs = pl.pallas_call(
        _ctx_attn_body,
        grid=(BATCH,),
        in_specs=[pl.BlockSpec((SEQ, ATTN_IN), lambda b: (b, 0)),
                  pl.BlockSpec((1, A_HEADS), lambda b: (0, 0))],
        out_specs=[pl.BlockSpec((SEQ, A_Q), lambda b: (b, 0)), pl.BlockSpec((SEQ, B_W), lambda b: (b, 0)),
                   kv_spec(A_KV_HEADS), kv_spec(A_KV_HEADS), kv_spec(B_HEADS), kv_spec(B_HEADS)],
        out_shape=[jax.ShapeDtypeStruct((T_CTX, A_Q), F32), jax.ShapeDtypeStruct((T_CTX, B_W), F32),
                   kv_sd(A_KV_HEADS), kv_sd(A_KV_HEADS), kv_sd(B_HEADS), kv_sd(B_HEADS)],
        compiler_params=_cparams("parallel"),
        name="ctx_attn",
    )(qkv, sink.reshape(1, A_HEADS))
    caches = [t.reshape(BATCH, 1, SEQ, -1, HEAD_DIM) for t in outs[2:]]
    return outs[0], outs[1], *caches


@functools.lru_cache(maxsize=None)
def _rope_tables(width):
    half = HEAD_DIM // 2
    t = np.arange(DEC_SEQ)
    inv = ROPE_BASE ** (-np.arange(0, half, 2, dtype=np.float64) / half)
    ang_r = (t // GRID_W)[:, None] * inv[None, :]
    ang_c = (t % GRID_W)[:, None] * inv[None, :]
    cos = np.concatenate([np.cos(ang_r)] * 2 + [np.cos(ang_c)] * 2, axis=-1)
    sin = np.concatenate([-np.sin(ang_r), np.sin(ang_r), -np.sin(ang_c), np.sin(ang_c)], axis=-1)
    reps = width // HEAD_DIM
    return (np.tile(cos, (1, reps)).astype(np.float32), np.tile(sin, (1, reps)).astype(np.float32))


def _rope_body(q_ref, k_ref, cq_ref, sq_ref, ck_ref, sk_ref, qo_ref, ko_ref):
    quarter = HEAD_DIM // 4

    def rot(x, cos, sin):
        w = x.shape[-1]
        lane = lax.broadcasted_iota(jnp.int32, x.shape, 1)
        fwd = pltpu.roll(x, w - quarter, axis=1)
        bwd = pltpu.roll(x, quarter, axis=1)
        partner = jnp.where((lane & (2 * quarter - 1)) < quarter, fwd, bwd)
        return x * cos + partner * sin

    qo_ref[...] = rot(q_ref[...], cq_ref[...], sq_ref[...])
    ko_ref[...] = rot(k_ref[...], ck_ref[...], sk_ref[...])


def _rope(qkv):
    cq, sq = _rope_tables(A_Q)
    ck, sk = _rope_tables(A_KV)
    tab = lambda w: pl.BlockSpec((DEC_SEQ, w), lambda b: (0, 0))
    row0 = T_CTX // DEC_SEQ
    return pl.pallas_call(
        _rope_body,
        grid=(DEC_BATCH,),
        in_specs=[pl.BlockSpec((DEC_SEQ, A_Q), lambda b: (row0 + b, 0)),
                  pl.BlockSpec((DEC_SEQ, A_KV), lambda b: (row0 + b, A_Q // A_KV)),
                  tab(A_Q), tab(A_Q), tab(A_KV), tab(A_KV)],
        out_specs=[pl.BlockSpec((DEC_SEQ, A_Q), lambda b: (b, 0)),
                   pl.BlockSpec((DEC_SEQ, A_KV), lambda b: (b, 0))],
        out_shape=[jax.ShapeDtypeStruct((T_LAT, A_Q), F32), jax.ShapeDtypeStruct((T_LAT, A_KV), F32)],
        compiler_params=_cparams("parallel"),
        name="rope",
    )(qkv, qkv, jnp.asarray(cq), jnp.asarray(sq), jnp.asarray(ck), jnp.asarray(sk))


WIN_QB = 256


def _win_attn_body(qraw_ref, qrot_ref, krot_ref, v_ref, kc_ref, vc_ref, sink_ref, o_ref):
    scale = HEAD_DIM ** -0.5
    hk = pl.program_id(1)
    tiles = A_GROUP // HEADS_PER_TILE

    def kv_in_half(x):
        swapped = pltpu.roll(x, HEAD_DIM, axis=1)
        return [jnp.where(hk == p, x, swapped) for p in range(HEADS_PER_TILE)]

    k, v, kc, vc = kv_in_half(krot_ref[...]), kv_in_half(v_ref[...]), kv_in_half(kc_ref[0]), kv_in_half(vc_ref[0])
    head_lane = lax.broadcasted_iota(jnp.int32, (1, A_HEADS), 1)

    def sink_rows(p):
        heads = [hk * A_GROUP + j * HEADS_PER_TILE + p for j in range(tiles)]
        vals = [jnp.sum(jnp.where(head_lane == h, sink_ref[...], 0.0), axis=-1, keepdims=True) for h in heads]
        return jnp.concatenate([jnp.broadcast_to(s, (WIN_QB, 1)) for s in vals], axis=0)

    sinks = [sink_rows(p) for p in range(HEADS_PER_TILE)]
    lane = lax.broadcasted_iota(jnp.int32, (tiles * WIN_QB, LANES), 1)
    in_half = [lane < HEAD_DIM, lane >= HEAD_DIM]
    for qb in range(DEC_SEQ // WIN_QB):
        q0 = qb * WIN_QB
        rows = slice(q0, q0 + WIN_QB)
        lo = max(0, q0 - WINDOW)
        hi = min(DEC_SEQ, q0 + WIN_QB + WINDOW)
        q_rot = jnp.concatenate([qrot_ref[rows, j * LANES:(j + 1) * LANES] for j in range(tiles)], axis=0)
        q_raw = jnp.concatenate([qraw_ref[rows, j * LANES:(j + 1) * LANES] for j in range(tiles)], axis=0)
        halves = []
        for p in range(HEADS_PER_TILE):
            s_loc = _dot_nt(jnp.where(in_half[p], q_rot, 0.0), k[p][lo:hi]) * scale
            qpos = q0 + (lax.broadcasted_iota(jnp.int32, s_loc.shape, 0) & (WIN_QB - 1))
            kpos = lo + lax.broadcasted_iota(jnp.int32, s_loc.shape, 1)
            s_loc = jnp.where(jnp.abs(kpos - qpos) <= WINDOW, s_loc, MASK_NEG)
            s_ctx = _dot_nt(jnp.where(in_half[p], q_raw, 0.0), kc[p]) * scale
            m = jnp.maximum(jnp.maximum(jnp.max(s_loc, axis=-1, keepdims=True),
                                        jnp.max(s_ctx, axis=-1, keepdims=True)), sinks[p])
            p_loc = jnp.exp(s_loc - m)
            p_ctx = jnp.exp(s_ctx - m)
            den = (jnp.sum(p_loc, axis=-1, keepdims=True) + jnp.sum(p_ctx, axis=-1, keepdims=True)
                   + jnp.exp(sinks[p] - m))
            halves.append((_dot(p_ctx, vc[p]) + _dot(p_loc, v[p][lo:hi])) / den)
        o = jnp.where(in_half[0], halves[0], halves[1])
        for j in range(tiles):
            o_ref[rows, j * LANES:(j + 1) * LANES] = o[j * WIN_QB:(j + 1) * WIN_QB]


def _win_attn(qkv, q_rot, k_rot, kc, vc, sink):
    row0 = T_CTX // DEC_SEQ
    gw = A_GROUP * HEAD_DIM
    return pl.pallas_call(
        _win_attn_body,
        grid=(DEC_BATCH, A_KV_HEADS),
        in_specs=[pl.BlockSpec((DEC_SEQ, gw), lambda b, h: (row0 + b, h)),
                  pl.BlockSpec((DEC_SEQ, gw), lambda b, h: (b, h)),
                  pl.BlockSpec((DEC_SEQ, A_KV), lambda b, h: (b, 0)),
                  pl.BlockSpec((DEC_SEQ, A_KV), lambda b, h: (row0 + b, (A_Q + A_KV) // A_KV)),
                  pl.BlockSpec((1, PAST_LEN, A_KV), lambda b, h: (b, 0, 0)),
                  pl.BlockSpec((1, PAST_LEN, A_KV), lambda b, h: (b, 0, 0)),
                  pl.BlockSpec((1, A_HEADS), lambda b, h: (0, 0))],
        out_specs=pl.BlockSpec((DEC_SEQ, gw), lambda b, h: (b, h)),
        out_shape=jax.ShapeDtypeStruct((T_LAT, A_Q), F32),
        compiler_params=_cparams("parallel", "parallel"),
        name="win_attn",
    )(qkv, q_rot, k_rot, qkv, kc, vc, sink.reshape(1, A_HEADS))


GRID_ROWS = DEC_SEQ // GRID_W
NA_BAND = min(NA_ROWS, GRID_ROWS)


NA_REL_ROWS = 2 * NA_ROWS - 1
NA_REL_COLS = 2 * NA_COLS - 1
LANES = 128
HEADS_PER_TILE = LANES // HEAD_DIM


def _na_rel_rows(rpb):
    pad = jnp.zeros((B_HEADS, NA_REL_ROWS, GRID_W - NA_REL_COLS), F32)
    one = jnp.concatenate([rpb, pad], axis=-1)
    nxt = jnp.concatenate([one[:, 1:], jnp.zeros((B_HEADS, 1, GRID_W), F32)], axis=1)
    both = jnp.concatenate([one, nxt], axis=-1)
    return jnp.concatenate([both, jnp.zeros((B_HEADS, 16 - NA_REL_ROWS, LANES), F32)], axis=1)


NA_HEADS_PER_STEP = LANES // HEAD_DIM


def _na_row_groups():
    groups = []
    for r in range(GRID_ROWS):
        rs = min(max(r - NA_ROWS // 2, 0), GRID_ROWS - NA_BAND)
        if groups and groups[-1][2] == rs:
            groups[-1][1] += 1
        else:
            groups.append([r, 1, rs])
    return groups


def _na_attn_body(q_ref, k_ref, v_ref, kc_ref, vc_ref, rel_ref, o_ref):
    scale = HEAD_DIM ** -0.5
    cq = lax.broadcasted_iota(jnp.int32, (GRID_W, LANES), 0)
    kcol = lax.broadcasted_iota(jnp.int32, (GRID_W, LANES), 1) & (GRID_W - 1)
    cs = jnp.clip(cq - NA_COLS // 2, 0, GRID_W - NA_COLS)
    col_ok = (kcol >= cs) & (kcol < cs + NA_COLS)
    kc = kc_ref[0]
    vc = vc_ref[0]
    tiles = {}

    def pair_tile(hh, a):
        if (hh, a) not in tiles:
            x = jnp.broadcast_to(rel_ref[hh, a:a + 1, :], (GRID_W, LANES))
            t = pltpu.roll(x, LANES - (NA_COLS - 1), axis=1, stride=1, stride_axis=0)
            tiles[hh, a] = jnp.where(col_ok, t, MASK_NEG)
        return tiles[hh, a]

    for r0, n_r, rs in _na_row_groups():
        rows = slice(r0 * GRID_W, (r0 + n_r) * GRID_W)
        band = slice(rs * GRID_W, (rs + NA_BAND) * GRID_W)
        q_t, k_t, v_t = q_ref[rows, :], k_ref[band, :], v_ref[band, :]
        head_of_lane = lax.broadcasted_iota(jnp.int32, q_t.shape, 1) >> (HEAD_DIM.bit_length() - 1)
        o = jnp.zeros(q_t.shape, F32)
        for hh in range(NA_HEADS_PER_STEP):
            bias = jnp.concatenate(
                [jnp.concatenate([pair_tile(hh, rs - r + NA_ROWS - 1 + 2 * i) for i in range(NA_BAND // 2)], axis=1)
                 for r in range(r0, r0 + n_r)], axis=0)
            q = jnp.where(head_of_lane == hh, q_t, 0.0)
            s_loc = _dot_nt(q, k_t) * scale + bias
            s_ctx = _dot_nt(q, kc) * scale
            m = jnp.maximum(jnp.max(s_loc, axis=-1, keepdims=True), jnp.max(s_ctx, axis=-1, keepdims=True))
            p_loc = jnp.exp(s_loc - m)
            p_ctx = jnp.exp(s_ctx - m)
            den = jnp.sum(p_loc, axis=-1, keepdims=True) + jnp.sum(p_ctx, axis=-1, keepdims=True)
            o = jnp.where(head_of_lane == hh, (_dot(p_ctx, vc) + _dot(p_loc, v_t)) / den, o)
        o_ref[rows, :] = o


def _na_attn(qkv, kc, vc, rel):
    row0 = T_CTX // DEC_SEQ
    col0 = (A_Q + 2 * A_KV) // LANES
    n_blk = B_W // LANES
    col = lambda j: pl.BlockSpec((DEC_SEQ, LANES), lambda b, p: (row0 + b, col0 + j * n_blk + p))
    cache = pl.BlockSpec((1, PAST_LEN, LANES), lambda b, p: (b, 0, p))
    return pl.pallas_call(
        _na_attn_body,
        grid=(DEC_BATCH, n_blk),
        in_specs=[col(0), col(1), col(2), cache, cache,
                  pl.BlockSpec((NA_HEADS_PER_STEP, 16, LANES), lambda b, p: (p, 0, 0))],
        out_specs=pl.BlockSpec((DEC_SEQ, LANES), lambda b, p: (b, p)),
        out_shape=jax.ShapeDtypeStruct((T_LAT, B_W), F32),
        compiler_params=_cparams("parallel", "parallel"),
        name="na_attn",
    )(qkv, qkv, qkv, kc, vc, rel)


@functools.lru_cache(maxsize=None)
def _dft_mats(L):
    n = 2 * L
    fc = min(L, DFT_CHUNK)
    f = np.arange(L)[:, None]
    t = np.arange(L)[None, :]
    ang = 2.0 * np.pi * ((f * t) % n) / n
    m1 = np.cos(ang)
    m2 = np.sin(ang)
    m2[0, :] = np.where(np.arange(L) % 2 == 0, 1.0, -1.0)
    wgt = np.full((L, 1), 2.0)
    wgt[0, 0] = 1.0
    nch = L // fc
    fwd = np.concatenate([m1.reshape(nch, fc, L), m2.reshape(nch, fc, L)], axis=1)
    inv = np.concatenate([(m1 * wgt / n).reshape(nch, fc, L), (m2 * wgt / n).reshape(nch, fc, L)], axis=1)
    inv = np.transpose(inv, (0, 2, 1))
    return fwd.astype(np.float32), inv.astype(np.float32)


@functools.lru_cache(maxsize=None)
def _filter_consts(L):
    t = np.linspace(0.0, 1.0, L)[:, None]
    bands = (C_EMB - 1) // 2
    ang = (2.0 * math.pi / L) * np.arange(L)[:, None] * np.linspace(1e-4, bands - 1, bands)[None, :]
    z = np.concatenate([t, np.cos(ang), -np.sin(ang)], axis=-1)
    zpad = np.zeros((L, 128))
    zpad[:, :C_EMB] = z
    deltas = np.abs(np.linspace(HYENA_MIN_DECAY, HYENA_MAX_DECAY, C_DIM))
    window = np.exp(-t * deltas[None, :])
    return zpad.astype(np.float32), window.astype(np.float32)


def _filter_body(z_ref, w1_ref, b1_ref, w2_ref, b2_ref, w3_ref, b3_ref, fr_ref, w4_ref, win_ref, fm_ref,
                 hr_ref, g_ref, hq_ref, hs_scr, hd_scr):
    c = pl.program_id(0)
    fc = hr_ref.shape[0]

    @pl.when(c == 0)
    def _():
        fr = fr_ref[...]
        hh = jnp.sin(fr * (_dot_hi(z_ref[...], w1_ref[...]) + b1_ref[...]))
        hh = jnp.sin(fr * (_dot_hi(hh, w2_ref[...]) + b2_ref[...]))
        hh = jnp.sin(fr * (_dot_hi(hh, w3_ref[...]) + b3_ref[...]))
        hh = _dot_hi(hh, w4_ref[...])
        hf = hh[:, :C_DIM] * win_ref[...]
        hb = hh[:, C_DIM:] * win_ref[...]
        hs_scr[...] = hf + hb
        hd_scr[...] = hf - hb

    fm = fm_ref[0]
    hr = _dot_split(fm[:fc], hs_scr[...])
    first = (lax.broadcasted_iota(jnp.int32, (fc, C_DIM), 0) == 0) & (c == 0)
    hr_ref[...] = hr
    g_ref[...] = jnp.where(first, 0.0, _dot_split(fm[fc:], hd_scr[...]))
    hs = hs_scr[...]
    sign = jnp.where((lax.broadcasted_iota(jnp.int32, hs.shape, 0) & 1) == 0, 1.0, -1.0)
    hq_ref[...] = jnp.where(first, jnp.sum(hs * sign, axis=0, keepdims=True), hr)


def _hyena_filter(L, filt):
    w1, b1, w2, b2, w3, b3, freq, w4 = filt
    zpad, window = _filter_consts(L)
    fwd, _ = _dft_mats(L)
    nch, fc2, _ = fwd.shape
    fc = fc2 // 2
    w1p = jnp.pad(w1, ((0, 128 - C_EMB), (0, 0)))
    full = lambda shape: pl.BlockSpec(shape, lambda c: tuple(0 for _ in shape))
    out_spec = pl.BlockSpec((fc, C_DIM), lambda c: (c, 0))
    out_sd = jax.ShapeDtypeStruct((L, C_DIM), F32)
    return pl.pallas_call(
        _filter_body,
        grid=(nch,),
        in_specs=[full((L, 128)), full((128, C_FFN)), full((1, C_FFN)), full((C_FFN, C_FFN)), full((1, C_FFN)),
                  full((C_FFN, C_FFN)), full((1, C_FFN)), full((1, C_FFN)), full((C_FFN, 2 * C_DIM)),
                  full((L, C_DIM)), pl.BlockSpec((1, fc2, L), lambda c: (c, 0, 0))],
        out_specs=[out_spec, out_spec, out_spec],
        out_shape=[out_sd, out_sd, out_sd],
        scratch_shapes=[pltpu.VMEM((L, C_DIM), F32), pltpu.VMEM((L, C_DIM), F32)],
        compiler_params=_cparams("arbitrary"),
        name="hyena_filter",
    )(jnp.asarray(zpad), w1p, b1.reshape(1, C_FFN), w2, b2.reshape(1, C_FFN), w3, b3.reshape(1, C_FFN),
      freq.reshape(1, C_FFN), w4, jnp.asarray(window), jnp.asarray(fwd))


def _hyena_body(u_ref, cw_ref, cb_ref, d_ref, fm_ref, fi_ref, hr_ref, g_ref, hq_ref, y_ref,
                x0_scr, z_scr, acc_scr):
    c = pl.program_id(1)
    L = y_ref.shape[0]
    fc = hr_ref.shape[0]

    @pl.when(c == 0)
    def _():
        row = lax.broadcasted_iota(jnp.int32, (L, C_DIM), 0)

        def short_conv(sec):
            cols = slice(sec * C_DIM, (sec + 1) * C_DIM)
            u = u_ref[:, cols]
            prev = jnp.where(row == 0, 0.0, pltpu.roll(u, 1, axis=0))
            nxt = jnp.where(row == L - 1, 0.0, pltpu.roll(u, L - 1, axis=0))
            return (prev * cw_ref[0:1, cols] + u * cw_ref[1:2, cols] + nxt * cw_ref[2:3, cols]
                    + cb_ref[:, cols])

        x0_scr[...] = short_conv(0)
        z_scr[...] = short_conv(1) * short_conv(2)
        acc_scr[...] = jnp.zeros((L, C_DIM), F32)

    ab = _dot_split(fm_ref[0], z_scr[...])
    a, b = ab[:fc], ab[fc:]
    hr, g, hq = hr_ref[...], g_ref[...], hq_ref[...]
    pq = jnp.concatenate([a * hr - b * g, a * g + b * hq], axis=0)
    acc_scr[...] += _dot_split(fi_ref[0], pq)

    @pl.when(c == pl.num_programs(1) - 1)
    def _():
        y_ref[...] = x0_scr[...] * (acc_scr[...] + z_scr[...] * d_ref[...])


def _hyena(u, row_blk0, n_seq, L, conv_w, conv_b, d_skip, spec):
    hr, g, hq = spec
    fwd, inv = _dft_mats(L)
    nch, fc2, _ = fwd.shape
    fc = fc2 // 2
    u_w = 3 * C_DIM
    return pl.pallas_call(
        _hyena_body,
        grid=(n_seq, nch),
        in_specs=[pl.BlockSpec((L, u_w), lambda b, c: (row_blk0 + b, 0)),
                  pl.BlockSpec((3, u_w), lambda b, c: (0, 0)),
                  pl.BlockSpec((1, u_w), lambda b, c: (0, 0)),
                  pl.BlockSpec((1, C_DIM), lambda b, c: (0, 0)),
                  pl.BlockSpec((1, fc2, L), lambda b, c: (c, 0, 0)),
                  pl.BlockSpec((1, L, fc2), lambda b, c: (c, 0, 0)),
                  pl.BlockSpec((fc, C_DIM), lambda b, c: (c, 0)),
                  pl.BlockSpec((fc, C_DIM), lambda b, c: (c, 0)),
                  pl.BlockSpec((fc, C_DIM), lambda b, c: (c, 0))],
        out_specs=pl.BlockSpec((L, C_DIM), lambda b, c: (b, 0)),
        out_shape=jax.ShapeDtypeStruct((n_seq * L, C_DIM), F32),
        scratch_shapes=[pltpu.VMEM((L, C_DIM), F32)] * 3,
        compiler_params=_cparams("parallel", "arbitrary"),
        name="hyena",
    )(u, conv_w, conv_b.reshape(1, u_w), d_skip.reshape(1, C_DIM), jnp.asarray(fwd), jnp.asarray(inv), hr, g, hq)


def _hgrn_body(q_ref, ff_ref, fb_ref, i_ref, g_ref, lbf_ref, lbb_ref, nd_ref, s0f_ref, s0b_ref,
               o_ref, sf_ref, sb_ref, *, layer):
    L = o_ref.shape[0]
    C = GLA_CHUNK
    S = min(L, GLA_SPAN)
    nc = S // C
    n_span = L // S
    mid = C // 2
    def lower_bound(gm):
        e = jnp.exp(gm - jnp.max(gm, axis=0, keepdims=True))
        p = e / jnp.sum(e, axis=0, keepdims=True)
        return jnp.sum(p[0:layer + 1], axis=0, keepdims=True) - p[0:1]

    def gates(fx, lb):
        f = lb + (1.0 - lb) * jax.nn.sigmoid(fx)
        return 1.0 - f, jnp.log(f)


    chunk_shift = C.bit_length() - 1
    block_shift = D_KDIM.bit_length() - 1
    ti = lax.broadcasted_iota(jnp.int32, (S, S), 0)
    si = lax.broadcasted_iota(jnp.int32, (S, S), 1)
    same_chunk = (ti >> chunk_shift) == (si >> chunk_shift)
    causal = same_chunk & (si <= ti)
    anti = same_chunk & (si >= ti)
    row_chunk = lax.broadcasted_iota(jnp.int32, (S, nc * D_KDIM), 0) >> chunk_shift
    col_chunk = lax.broadcasted_iota(jnp.int32, (S, nc * D_KDIM), 1) >> block_shift
    own_block = row_chunk == col_chunk

    def spread(x):
        return jnp.where(own_block, jnp.concatenate([x] * nc, axis=1), 0.0)

    def chunk_cumsum(mask, lg):
        tri = mask.astype(BF16)
        hi = lg.astype(BF16)
        r1 = lg - hi.astype(F32)
        mid_t = r1.astype(BF16)
        lo = (r1 - mid_t.astype(F32)).astype(BF16)
        dot = lambda t: jnp.dot(tri, t, preferred_element_type=F32)
        return dot(hi) + dot(mid_t) + dot(lo)

    def per_chunk_rows(b, pos):
        return jnp.concatenate([jnp.broadcast_to(b[n * C + pos:n * C + pos + 1], (C, D_KDIM)) for n in range(nc)],
                               axis=0)

    def one_head(q, v, kf, lgf, kb, lgb, st_f, st_b):
        local = []
        for u in range(n_span):
            rows = slice(u * S, (u + 1) * S)
            qs, vs, kfs, kbs = q[rows], v[rows], kf[rows], kb[rows]
            lgs = jnp.concatenate([lgf[rows], lgb[rows]], axis=1)
            pre = chunk_cumsum(causal, lgs)
            b_f = pre[:, :D_KDIM]
            pre_b = pre[:, D_KDIM:]
            b_b = per_chunk_rows(pre_b, C - 1) - pre_b + lgb[rows]
            ref_f, ref_b = per_chunk_rows(b_f, mid), per_chunk_rows(b_b, mid)
            sc = (jnp.where(causal, _dot_nt(qs * jnp.exp(b_f - ref_f), kfs * jnp.exp(ref_f - b_f)), 0.0)
                  + jnp.where(anti, _dot_nt(qs * jnp.exp(b_b - ref_b), kbs * jnp.exp(ref_b - b_b)), 0.0))
            k_out = jnp.concatenate([kfs * jnp.exp(per_chunk_rows(b_f, C - 1) - b_f),
                                     kbs * jnp.exp(per_chunk_rows(b_b, 0) - b_b)], axis=1)
            kv_t = _dot_tn(spread(vs), k_out)
            local.append((_dot(sc, vs), kv_t, b_f, b_b, qs))

        states_f = [[None] * nc for _ in range(n_span)]
        for u in range(n_span):
            _, kv_t, b_f, _, _ = local[u]
            for n in range(nc):
                states_f[u][n] = st_f
                st_f = st_f * jnp.exp(b_f[n * C + C - 1:n * C + C]) + kv_t[n * D_VDIM:(n + 1) * D_VDIM, :D_KDIM]
        states_b = [[None] * nc for _ in range(n_span)]
        for u in reversed(range(n_span)):
            _, kv_t, _, b_b, _ = local[u]
            for n in reversed(range(nc)):
                states_b[u][n] = st_b
                st_b = st_b * jnp.exp(b_b[n * C:n * C + 1]) + kv_t[n * D_VDIM:(n + 1) * D_VDIM, D_KDIM:]

        outs = []
        for u in range(n_span):
            intra, _, b_f, b_b, qs = local[u]
            q_in = jnp.concatenate([spread(qs * jnp.exp(b_f)), spread(qs * jnp.exp(b_b))], axis=1)
            outs.append(intra + _dot_nt(q_in, jnp.concatenate(states_f[u] + states_b[u], axis=1)))
        return (jnp.concatenate(outs, axis=0) if n_span > 1 else outs[0]), st_f, st_b

    for hh in range(o_ref.shape[1] // D_VDIM):
        cols = slice(hh * D_KDIM, (hh + 1) * D_KDIM)
        kf, lgf = gates(ff_ref[:, cols], lower_bound(lbf_ref[:, cols]))
        kb, lgb = gates(fb_ref[:, cols], lower_bound(lbb_ref[:, cols]))
        o, st_f, st_b = one_head(_silu(q_ref[:, cols]), i_ref[:, cols], kf, lgf, kb, lgb,
                                 jnp.transpose(s0f_ref[0, hh]), jnp.transpose(s0b_ref[0, hh]))
        sf_ref[0, hh] = jnp.transpose(st_f)
        sb_ref[0, hh] = jnp.transpose(st_b)
        o_ref[:, cols] = _rms(o, nd_ref[...]) * _silu(g_ref[:, cols])


def _hgrn(u, row_blk0, n_seq, L, lb_fwd, lb_bwd, norm_d, s0f, s0b, layer):
    hps = HGRN_HEADS_PER_STEP
    width = hps * D_KDIM
    col0 = 3 * C_DIM // width
    groups = D_HEADS // hps
    col = lambda j: pl.BlockSpec((L, width), lambda b, h: (row_blk0 + b, col0 + j * groups + h))
    lbs = pl.BlockSpec((DEPTH, width), lambda b, h: (0, h))
    st = pl.BlockSpec((1, hps, D_KDIM, D_VDIM), lambda b, h: (b, h, 0, 0))
    st_sd = jax.ShapeDtypeStruct((n_seq, D_HEADS, D_KDIM, D_VDIM), F32)
    return pl.pallas_call(
        functools.partial(_hgrn_body, layer=layer),
        grid=(n_seq, groups),
        in_specs=[col(0), col(1), col(2), col(3), col(4), lbs, lbs,
                  pl.BlockSpec((1, D_VDIM), lambda b, h: (0, 0)), st, st],
        out_specs=[pl.BlockSpec((L, width), lambda b, h: (b, h)), st, st],
        out_shape=[jax.ShapeDtypeStruct((n_seq * L, D_HEADS * D_VDIM), F32), st_sd, st_sd],
        compiler_params=_cparams("parallel", "parallel"),
        name="hgrn",
    )(u, u, u, u, u, lb_fwd, lb_bwd, norm_d.reshape(1, D_VDIM), s0f, s0b)


def _pack_bf16_pairs(h):
    n = h.shape[1] // 2
    hi = lax.bitcast_convert_type(h[:, :n].astype(BF16).astype(F32), jnp.int32)
    lo = lax.bitcast_convert_type(h[:, n:].astype(BF16).astype(F32), jnp.int32)
    return hi | lax.shift_right_logical(lo, 16)


def _unpack_bf16_pairs(p):
    hi = lax.bitcast_convert_type(p & jnp.int32(-65536), F32).astype(BF16)
    lo = lax.bitcast_convert_type(lax.shift_left(p, 16), F32).astype(BF16)
    return hi, lo


def _outproj_body(*refs, n_x):
    a_refs, b_refs, x_refs = refs[0:2], refs[2:4], refs[4:4 + n_x]
    mod_ref, gf_ref, w_ref, wrh_ref, wrl_ref, rb_ref, x1_ref, h2_ref, chosen_ref, gk_ref, ik_ref = refs[4 + n_x:]
    m = mod_ref[0]
    half = a_refs[0].shape[1]
    out = _dot(_token_tile(a_refs), w_ref[0:half, :]) + _dot(_token_tile(b_refs), w_ref[half:, :])
    x1 = _token_tile(x_refs) + m[:, 2 * D_MODEL:3 * D_MODEL] * out
    x1_ref[...] = x1
    h2 = _rms(x1, gf_ref[...]) * (1.0 + m[:, 4 * D_MODEL:5 * D_MODEL]) + m[:, 3 * D_MODEL:4 * D_MODEL]
    h2_ref[...] = _pack_bf16_pairs(h2)
    h_hi = h2.astype(BF16)
    h_lo = (h2 - h_hi.astype(F32)).astype(BF16)
    logits = _dot_nt(wrh_ref[...], h_hi) + _dot_nt(wrh_ref[...], h_lo) + _dot_nt(wrl_ref[...], h_hi)
    scores = jax.nn.sigmoid(logits)
    work = scores + rb_ref[...]
    expert = lax.broadcasted_iota(jnp.int32, work.shape, 0).astype(F32)
    slot = lax.broadcasted_iota(jnp.int32, (TOP_K, work.shape[1]), 0)
    chosen = [jnp.zeros(work.shape, F32) for _ in range(MOE_REGIONS)]
    gk = jnp.zeros((TOP_K, work.shape[1]), F32)
    ik = jnp.zeros((TOP_K, work.shape[1]), F32)
    for k in range(TOP_K):
        best = jnp.max(work, axis=0, keepdims=True)
        first = jnp.min(jnp.where(work == best, expert, float(N_EXPERTS)), axis=0, keepdims=True)
        hit = expert == first
        chosen[k // K_PER_REGION] = jnp.where(hit, 1.0, chosen[k // K_PER_REGION])
        gk = jnp.where(slot == k, jnp.sum(jnp.where(hit, scores, 0.0), axis=0, keepdims=True), gk)
        ik = jnp.where(slot == k, first, ik)
        work = jnp.where(hit, -jnp.inf, work)
    for r in range(MOE_REGIONS):
        chosen_ref[r] = chosen[r]
    gk_ref[...] = jnp.transpose(gk / jnp.sum(gk, axis=0, keepdims=True) * ROUTE_SCALE)
    ik_ref[...] = ik


def _outproj(a, b, x, mod_l, gain_ffn, w_out, w_router, router_bias):
    half = a[0].shape[1]
    a_specs, a_args = _token_specs(a, half)
    b_specs, b_args = _token_specs(b, half)
    x_specs, x_args = _token_specs(x, D_MODEL)
    wr_t = w_router.T
    wr_hi = wr_t.astype(BF16)
    wr_lo = (wr_t - wr_hi.astype(F32)).astype(BF16)
    return pl.pallas_call(
        functools.partial(_outproj_body, n_x=len(x_args)),
        grid=(T_ALL // TM,),
        in_specs=a_specs + b_specs + x_specs + [
                  pl.BlockSpec((1, 1, N_MOD * D_MODEL), lambda i: (_mod_row(i), 0, 0)),
                  pl.BlockSpec((1, D_MODEL), lambda i: (0, 0)),
                  pl.BlockSpec((2 * half, D_MODEL), lambda i: (0, 0)),
                  pl.BlockSpec((N_EXPERTS, D_MODEL), lambda i: (0, 0)),
                  pl.BlockSpec((N_EXPERTS, D_MODEL), lambda i: (0, 0)),
                  pl.BlockSpec((N_EXPERTS, 1), lambda i: (0, 0))],
        out_specs=[pl.BlockSpec((TM, D_MODEL), lambda i: (i, 0)),
                   pl.BlockSpec((TM, D_MODEL // 2), lambda i: (i, 0)),
                   pl.BlockSpec((MOE_REGIONS, N_EXPERTS, TM), lambda i: (0, 0, i)),
                   pl.BlockSpec((TM, TOP_K), lambda i: (i, 0)),
                   pl.BlockSpec((TOP_K, TM), lambda i: (0, i))],
        out_shape=[jax.ShapeDtypeStruct((T_ALL, D_MODEL), F32),
                   jax.ShapeDtypeStruct((T_ALL, D_MODEL // 2), jnp.int32),
                   jax.ShapeDtypeStruct((MOE_REGIONS, N_EXPERTS, T_ALL), F32),
                   jax.ShapeDtypeStruct((T_ALL, TOP_K), F32),
                   jax.ShapeDtypeStruct((TOP_K, T_ALL), F32)],
        compiler_params=_cparams("parallel"),
        name="outproj_router",
    )(*a_args, *b_args, *x_args, mod_l, gain_ffn.reshape(1, D_MODEL), w_out, wr_hi, wr_lo,
      router_bias.reshape(N_EXPERTS, 1))


def _route_body(chosen_ref, ik_ref, dest_ref, first_ref, count_ref, short_ref, pos_scr):
    n_tiles = T_ALL // TM
    r = lax.broadcasted_iota(jnp.int32, (TM, TM), 0)
    c = lax.broadcasted_iota(jnp.int32, (TM, TM), 1)
    before = (r < c).astype(BF16)

    counts = jnp.zeros((N_EXPERTS, 1), F32)
    for i in range(n_tiles):
        cols = slice(i * TM, (i + 1) * TM)
        m = chosen_ref[0, :, cols]
        pos_scr[:, cols] = jnp.dot(m.astype(BF16), before, preferred_element_type=F32) + counts
        counts = counts + jnp.sum(m, axis=1, keepdims=True)
    padded = jnp.ceil(counts * (1.0 / MOE_BLK)) * MOE_BLK
    ei = lax.broadcasted_iota(jnp.int32, (N_EXPERTS, N_EXPERTS), 0)
    ej = lax.broadcasted_iota(jnp.int32, (N_EXPERTS, N_EXPERTS), 1)
    end = _dot_hi((ej <= ei).astype(F32), jnp.broadcast_to(padded, (N_EXPERTS, LANES)))[:, 0:1]
    start = end - padded

    expert = lax.broadcasted_iota(jnp.int32, (N_EXPERTS, TM), 0).astype(F32)
    slot = lax.broadcasted_iota(jnp.int32, (K_PER_REGION, TM), 0)
    for i in range(n_tiles):
        cols = slice(i * TM, (i + 1) * TM)
        row_of = pos_scr[:, cols] + start
        ik = ik_ref[0, :, cols]
        acc = jnp.zeros((K_PER_REGION, TM), F32)
        for k in range(K_PER_REGION):
            pick = jnp.sum(jnp.where(expert == ik[k:k + 1, :], row_of, 0.0), axis=0, keepdims=True)
            acc = jnp.where(slot == k, pick, acc)
        dest_ref[0, :, cols] = acc.astype(jnp.int32)
    first_ref[0] = jnp.broadcast_to(start * (1.0 / MOE_BLK), (N_EXPERTS, LANES)).astype(jnp.int32)
    count_ref[0] = jnp.broadcast_to(padded * (1.0 / MOE_BLK), (N_EXPERTS, LANES)).astype(jnp.int32)
    in_last = counts - (padded - MOE_BLK)
    short = jnp.where((counts > 0.0) & (in_last <= MOE_BLK // 2), 1.0, 0.0)
    short_ref[0] = jnp.broadcast_to(short, (N_EXPERTS, LANES)).astype(jnp.int32)


def _route(chosen, ik):
    per_region = lambda rows, cols: pl.BlockSpec((1, rows, cols), lambda r: (r, 0, 0))
    table = jax.ShapeDtypeStruct((MOE_REGIONS, N_EXPERTS, LANES), jnp.int32)
    return pl.pallas_call(
        _route_body,
        grid=(MOE_REGIONS,),
        in_specs=[per_region(N_EXPERTS, T_ALL), per_region(K_PER_REGION, T_ALL)],
        out_specs=[per_region(K_PER_REGION, T_ALL)] + [per_region(N_EXPERTS, LANES)] * 3,
        out_shape=[jax.ShapeDtypeStruct((MOE_REGIONS, K_PER_REGION, T_ALL), jnp.int32), table, table, table],
        scratch_shapes=[pltpu.VMEM((N_EXPERTS, T_ALL), F32)],
        compiler_params=_cparams("arbitrary"),
        name="moe_route",
    )(chosen, ik.reshape(MOE_REGIONS, K_PER_REGION, T_ALL))


def _sc_worker_id():
    return lax.axis_index("s") * SC_CORES + lax.axis_index("c")


def _sc_dispatch(h2p, dest):
    n_chunks = T_ALL // DISP_CHUNK
    k_per = dest.shape[0] // DISP_SPLIT
    items_per_worker = n_chunks * DISP_SPLIT // SC_WORKERS
    chunk_stride = SC_WORKERS // DISP_SPLIT
    width = h2p.shape[1]
    mesh = plsc.VectorSubcoreMesh(core_axis_name="c", subcore_axis_name="s")

    @functools.partial(
        pl.kernel, mesh=mesh,
        out_type=jax.ShapeDtypeStruct((MOE_ROWS, width), jnp.int32),
        scratch_types=[pltpu.VMEM((k_per, DISP_CHUNK), jnp.int32), pltpu.VMEM((DISP_CHUNK, width), jnp.int32),
                       pltpu.SemaphoreType.DMA],
    )
    def run(x_hbm, dest_hbm, xs_hbm, idx_v, rows_v, sem):
        wid = _sc_worker_id()
        group = wid % DISP_SPLIT
        for i in range(items_per_worker):
            chunk = i * chunk_stride + wid // DISP_SPLIT
            tokens = pl.ds(pl.multiple_of(chunk * DISP_CHUNK, DISP_CHUNK), DISP_CHUNK)
            pltpu.sync_copy(dest_hbm.at[group, :, tokens], idx_v)
            pltpu.sync_copy(x_hbm.at[tokens], rows_v)
            scatters = [pltpu.make_async_copy(rows_v, xs_hbm.at[idx_v.at[k]], sem) for k in range(k_per)]
            for cp in scatters:
                cp.start()
            for cp in scatters:
                cp.wait()

    return run(h2p, dest.reshape(DISP_SPLIT, k_per, T_ALL))


def _sc_collect(y, dest_flat):
    n_k = dest_flat.shape[0] // T_ALL
    per_worker = T_ALL // SC_WORKERS
    n_chunks = per_worker // COLLECT_CHUNK
    n_steps = n_k * n_chunks
    width = y.shape[1]
    mesh = plsc.VectorSubcoreMesh(core_axis_name="c", subcore_axis_name="s")

    @functools.partial(
        pl.kernel, mesh=mesh,
        out_type=jax.ShapeDtypeStruct((n_k * T_ALL, width), y.dtype),
        scratch_types=[pltpu.VMEM((n_k * per_worker,), jnp.int32),
                       pltpu.VMEM((COLLECT_CHUNK, width), y.dtype), pltpu.VMEM((COLLECT_CHUNK, width), y.dtype),
                       pltpu.SemaphoreType.DMA, pltpu.SemaphoreType.DMA],
    )
    def run(y_hbm, dest_hbm, yg_hbm, idx_v, rows0, rows1, sem0, sem1):
        wid = _sc_worker_id()
        bufs = ((rows0, sem0), (rows1, sem1))
        for k in range(n_k):
            pltpu.sync_copy(dest_hbm.at[pl.ds(k * T_ALL + wid * per_worker, per_worker)],
                            idx_v.at[pl.ds(k * per_worker, per_worker)])

        def gather(step, buf):
            rows, sem = buf
            idx = idx_v.at[pl.ds(pl.multiple_of(step * COLLECT_CHUNK, 8), COLLECT_CHUNK)]
            return pltpu.make_async_copy(y_hbm.at[idx], rows, sem)

        def out_rows(step):
            off = (step // n_chunks) * T_ALL + wid * per_worker + (step % n_chunks) * COLLECT_CHUNK
            return yg_hbm.at[pl.ds(pl.multiple_of(off, 8), COLLECT_CHUNK)]

        gather(0, bufs[0]).start()

        @pl.loop(0, n_steps, step=2)
        def _(base):
            for j in range(2):
                step = base + j

                @pl.when(step + 1 < n_steps)
                def _():
                    gather(step + 1, bufs[1 - j]).start()

                gather(step, bufs[j]).wait()
                pltpu.sync_copy(bufs[j][0], out_rows(step))

    return run(y, dest_flat)


def _expert_body(first_ref, count_ref, short_ref, xs_hbm, wg_hbm, wu_hbm, wd_hbm, y_hbm,
                 wg_f32, wu_f32, wd_f32, wg_bf, wu_bf, wd_bf, x_buf, y_buf, w_sem, in_sem, out_sem, *, layer):
    e = pl.program_id(0)
    first = first_ref[e]
    count = count_ref[e]
    n_used = first_ref[N_EXPERTS - 1] + count_ref[N_EXPERTS - 1]
    half = D_MODEL // 2

    def weight_copies(ex):
        slot = lax.rem(ex, EXPERT_W_SLOTS)
        out = []
        for src, dst in ((wg_hbm, wg_f32), (wu_hbm, wu_f32), (wd_hbm, wd_f32)):
            size = dst.shape[1] // EXPERT_W_PARTS
            for part in range(EXPERT_W_PARTS):
                rows = pl.ds(part * size, size)
                out.append(pltpu.make_async_copy(src.at[layer, ex, rows], dst.at[slot, rows], w_sem.at[slot]))
        return out

    @pl.when(e == 0)
    def _():
        for ahead in range(EXPERT_W_SLOTS - 1):
            for cp in weight_copies(ahead):
                cp.start()

    for cp in weight_copies(e):
        cp.wait()

    @pl.when(e + EXPERT_W_SLOTS - 1 < N_EXPERTS)
    def _():
        for cp in weight_copies(e + EXPERT_W_SLOTS - 1):
            cp.start()

    w_slot = lax.rem(e, EXPERT_W_SLOTS)
    wg_bf[...] = wg_f32[w_slot].astype(BF16)
    wu_bf[...] = wu_f32[w_slot].astype(BF16)
    wd_bf[...] = wd_f32[w_slot].astype(BF16)

    def part_rows(g, part, n_parts):
        size = MOE_BLK // n_parts
        return pl.ds(pl.multiple_of(g * MOE_BLK + part * size, size), size), pl.ds(part * size, size)

    def in_copies(g):
        slot = g & (EXPERT_SLOTS - 1)
        out = []
        for part in range(EXPERT_IN_PARTS):
            src, dst = part_rows(g, part, EXPERT_IN_PARTS)
            out.append(pltpu.make_async_copy(xs_hbm.at[src], x_buf.at[slot, dst], in_sem.at[slot]))
        return out

    def out_copies(g):
        slot = g & (EXPERT_SLOTS - 1)
        out = []
        for part in range(EXPERT_OUT_PARTS):
            dst, src = part_rows(g, part, EXPERT_OUT_PARTS)
            out.append(pltpu.make_async_copy(y_buf.at[slot, src], y_hbm.at[dst], out_sem.at[slot]))
        return out

    @pl.when((first == 0) & (count > 0))
    def _():
        for ahead in range(EXPERT_SLOTS - 1):
            @pl.when(ahead < n_used)
            def _():
                for cp in in_copies(ahead):
                    cp.start()

    def block(b, carry):
        g = first + b
        slot = g & (EXPERT_SLOTS - 1)
        for cp in in_copies(g):
            cp.wait()

        @pl.when(g + EXPERT_SLOTS - 1 < n_used)
        def _():
            for cp in in_copies(g + EXPERT_SLOTS - 1):
                cp.start()

        @pl.when(g >= EXPERT_SLOTS)
        def _():
            for cp in out_copies(g - EXPERT_SLOTS):
                cp.wait()

        def ffn(n_rows):
            hi, lo = _unpack_bf16_pairs(x_buf[slot, 0:n_rows])

            def proj(w_bf):
                return (jnp.dot(hi, w_bf[0:half, :], preferred_element_type=F32)
                        + jnp.dot(lo, w_bf[half:, :], preferred_element_type=F32))

            hid = _silu(proj(wg_bf)) * proj(wu_bf)
            y_buf[slot, 0:n_rows] = _pack_bf16_pairs(
                jnp.dot(hid.astype(BF16), wd_bf[...], preferred_element_type=F32))

        short = (b == count - 1) & (short_ref[e] == 1)

        @pl.when(short)
        def _():
            ffn(MOE_BLK // 2)
            y_buf[slot, MOE_BLK // 2:MOE_BLK] = jnp.zeros((MOE_BLK // 2, D_MODEL // 2), jnp.int32)

        @pl.when(jnp.logical_not(short))
        def _():
            ffn(MOE_BLK)

        for cp in out_copies(g):
            cp.start()
        return carry

    lax.fori_loop(0, count, block, 0)

    @pl.when(e == N_EXPERTS - 1)
    def _():
        for back in range(EXPERT_SLOTS, 0, -1):
            @pl.when(n_used >= back)
            def _():
                for cp in out_copies(n_used - back):
                    cp.wait()


EXPERT_SLOTS = 4
EXPERT_W_SLOTS = 4
EXPERT_W_PARTS = 2
EXPERT_IN_PARTS = 2
EXPERT_OUT_PARTS = 4


def _experts(first_blk, n_blk, short_last, xs, layer, w_gate, w_up, w_down):
    anywhere = pl.BlockSpec(memory_space=pl.ANY)
    grid_spec = pltpu.PrefetchScalarGridSpec(
        num_scalar_prefetch=3,
        grid=(N_EXPERTS,),
        in_specs=[anywhere] * 4,
        out_specs=anywhere,
        scratch_shapes=[pltpu.VMEM((EXPERT_W_SLOTS, D_MODEL, D_EXPERT), F32),
                        pltpu.VMEM((EXPERT_W_SLOTS, D_MODEL, D_EXPERT), F32),
                        pltpu.VMEM((EXPERT_W_SLOTS, D_EXPERT, D_MODEL), F32),
                        pltpu.VMEM((D_MODEL, D_EXPERT), BF16), pltpu.VMEM((D_MODEL, D_EXPERT), BF16),
                        pltpu.VMEM((D_EXPERT, D_MODEL), BF16),
                        pltpu.VMEM((EXPERT_SLOTS, MOE_BLK, D_MODEL // 2), jnp.int32),
                        pltpu.VMEM((EXPERT_SLOTS, MOE_BLK, D_MODEL // 2), jnp.int32),
                        pltpu.SemaphoreType.DMA((EXPERT_W_SLOTS,)),
                        pltpu.SemaphoreType.DMA((EXPERT_SLOTS,)), pltpu.SemaphoreType.DMA((EXPERT_SLOTS,))],
    )
    return pl.pallas_call(
        functools.partial(_expert_body, layer=layer),
        grid_spec=grid_spec,
        out_shape=jax.ShapeDtypeStruct((MOE_ROWS, D_MODEL // 2), jnp.int32),
        compiler_params=_cparams("arbitrary"),
        name="moe_experts",
    )(first_blk, n_blk, short_last, xs, w_gate, w_up, w_down)


def _combine_body(x1_ref, h2_ref, *refs, final):
    yg_refs = refs[:MOE_REGIONS]
    gk_ref, mod_ref, sg_ref, su_ref, sd_ref, fn_ref, *o_refs = refs[MOE_REGIONS:]
    hi, lo = _unpack_bf16_pairs(h2_ref[...])
    half = D_MODEL // 2

    def proj(w_ref):
        return _dot(hi, w_ref[0:half, :]) + _dot(lo, w_ref[half:, :])

    shared = _dot(_silu(proj(sg_ref)) * proj(su_ref), sd_ref[...])
    acc_hi, acc_lo = shared[:, :half], shared[:, half:]
    gk = gk_ref[...]
    for k in range(TOP_K):
        y_hi, y_lo = _unpack_bf16_pairs(yg_refs[k // K_PER_REGION][k % K_PER_REGION])
        acc_hi = acc_hi + gk[:, k:k + 1] * y_hi.astype(F32)
        acc_lo = acc_lo + gk[:, k:k + 1] * y_lo.astype(F32)
    acc = jnp.concatenate([acc_hi, acc_lo], axis=1)
    m = mod_ref[0]
    y = x1_ref[...] + m[:, 5 * D_MODEL:6 * D_MODEL] * acc
    if not final:
        o_refs[0][...] = y
        return
    y = _rms(y, fn_ref[...])
    is_ctx = pl.program_id(0) < N_CTX_TILES

    @pl.when(is_ctx)
    def _():
        o_refs[0][...] = y

    @pl.when(jnp.logical_not(is_ctx))
    def _():
        o_refs[1][...] = y


def _combine(x1, h2p, yg, gk, mod_l, ws_gate, ws_up, ws_down, final_norm, final):
    tok = lambda shape: pl.BlockSpec(shape, lambda i: (i, 0))
    full = lambda shape: pl.BlockSpec(shape, lambda i: (0, 0))
    if final:
        out_specs, _ = _token_specs((None, None), D_MODEL)
        out_shape = [jax.ShapeDtypeStruct((T_CTX, D_MODEL), F32), jax.ShapeDtypeStruct((T_LAT, D_MODEL), F32)]
    else:
        out_specs = tok((TM, D_MODEL))
        out_shape = jax.ShapeDtypeStruct((T_ALL, D_MODEL), F32)
    return pl.pallas_call(
        functools.partial(_combine_body, final=final),
        grid=(T_ALL // TM,),
        in_specs=[tok((TM, D_MODEL)), tok((TM, D_MODEL // 2))]
                 + [pl.BlockSpec((K_PER_REGION, TM, D_MODEL // 2), lambda i: (0, i, 0))] * MOE_REGIONS
                 + [tok((TM, TOP_K)),
                  pl.BlockSpec((1, 1, N_MOD * D_MODEL), lambda i: (_mod_row(i), 0, 0)),
                  full((D_MODEL, D_EXPERT)), full((D_MODEL, D_EXPERT)), full((D_EXPERT, D_MODEL)),
                  full((1, D_MODEL))],
        out_specs=out_specs,
        out_shape=out_shape,
        compiler_params=_cparams("arbitrary"),
        name="moe_combine",
    )(x1, h2p, *yg, gk, mod_l, ws_gate, ws_up, ws_down, final_norm.reshape(1, D_MODEL))


def _moe(x1, h2p, chosen, gk, ik, mod_l, layer, w_gate, w_up, w_down, ws_gate, ws_up, ws_down, final_norm, final):
    dest, first_blk, n_blk, short_last = _route(chosen, ik)
    yg = []
    for r in range(MOE_REGIONS):
        xs = _sc_dispatch(h2p, dest[r])
        y = _experts(first_blk[r, :, 0], n_blk[r, :, 0], short_last[r, :, 0], xs, layer, w_gate, w_up, w_down)
        yg.append(_sc_collect(y, dest[r].reshape(-1)).reshape(K_PER_REGION, T_ALL, D_MODEL // 2))
    return _combine(x1, h2p, yg, gk, mod_l, ws_gate, ws_up, ws_down, final_norm, final)


def kernel(x_prompt, x_sample, cache_a_k, cache_a_v, cache_b_k, cache_b_v, state_d_fwd, state_d_bwd, c, c_ctx, w_ada, b_ada, norm_mix, norm_ffn, w_in_attn, w_out_attn, sink_a, rpb_b, w_in_rec, w_out_rec, conv_w, conv_b, filt_w1, filt_b1, filt_w2, filt_b2, filt_w3, filt_b3, filt_freq, filt_w4, d_skip, lb_fwd, lb_bwd, norm_d, w_router, router_bias, w_gate, w_up, w_down, ws_gate, ws_up, ws_down, final_norm):
    x = (x_prompt.reshape(T_CTX, D_MODEL), x_sample.reshape(T_LAT, D_MODEL))
    cvec = jnp.concatenate([c_ctx[None, :], c], axis=0)
    c_lanes = jnp.broadcast_to(cvec[:, :, None], (N_CVEC, D_MODEL, LANES))
    mod = [_ada(c_lanes, l, w_ada, b_ada).reshape(CVEC_PAD, 1, N_MOD * D_MODEL) for l in range(DEPTH)]

    new_kv = None
    new_state = None
    for l in range(DEPTH):
        j = l // 2
        final = l == DEPTH - 1
        if l % 2 == 0:
            qkv = _inproj(x, mod[l], norm_mix[l], w_in_attn[j])
            oa_ctx, ob_ctx, *new_kv = _ctx_attn(qkv, sink_a[j])
            new_kv = tuple(new_kv)
            q_rot, k_rot = _rope(qkv)
            cache = lambda t: t[:, j].reshape(DEC_BATCH, PAST_LEN, -1)
            oa_lat = _win_attn(qkv, q_rot, k_rot, cache(cache_a_k), cache(cache_a_v), sink_a[j])
            ob_lat = _na_attn(qkv, cache(cache_b_k), cache(cache_b_v), _na_rel_rows(rpb_b[j]))
            mix_a = (oa_ctx, oa_lat)
            mix_b = (ob_ctx, ob_lat)
            w_out = w_out_attn[j]
        else:
            u = _inproj(x, mod[l], norm_mix[l], w_in_rec[j])
            filt = (filt_w1[j], filt_b1[j], filt_w2[j], filt_b2[j], filt_w3[j], filt_b3[j], filt_freq[j],
                    filt_w4[j])
            y_ctx = _hyena(u, 0, BATCH, SEQ, conv_w[j], conv_b[j], d_skip[j], _hyena_filter(SEQ, filt))
            y_lat = _hyena(u, T_CTX // DEC_SEQ, DEC_BATCH, DEC_SEQ, conv_w[j], conv_b[j], d_skip[j],
                           _hyena_filter(DEC_SEQ, filt))
            zeros = jnp.zeros((BATCH, D_HEADS, D_KDIM, D_VDIM), F32)
            o_ctx, s_f, s_b = _hgrn(u, 0, BATCH, SEQ, lb_fwd, lb_bwd, norm_d[j], zeros, zeros, l)
            o_lat, _, _ = _hgrn(u, T_CTX // DEC_SEQ, DEC_BATCH, DEC_SEQ, lb_fwd, lb_bwd, norm_d[j],
                                state_d_fwd[:, j], state_d_bwd[:, j], l)
            new_state = (s_f[:, None], s_b[:, None])
            mix_a = (y_ctx, y_lat)
            mix_b = (o_ctx, o_lat)
            w_out = w_out_rec[j]
        x1, h2p, chosen, gk, ik = _outproj(mix_a, mix_b, x, mod[l], norm_ffn[l], w_out, w_router[l],
                                           router_bias[l])
        x = _moe(x1, h2p, chosen, gk, ik, mod[l], l, w_gate, w_up, w_down, ws_gate[l], ws_up[l],
                 ws_down[l], final_norm, final)

    y_prompt = x[0].reshape(BATCH, SEQ, D_MODEL)
    y_sample = x[1].reshape(DEC_BATCH, DEC_SEQ, D_MODEL)
    return (y_prompt, y_sample) + new_kv + new_state
```

```python
import functools
import math

import numpy as np
import jax
import jax.numpy as jnp
from jax import lax
from jax.experimental import pallas as pl
from jax.experimental.pallas import tpu as pltpu
from jax.experimental.pallas import tpu_sc as plsc

F32 = jnp.float32
BF16 = jnp.bfloat16
HI = lax.Precision.HIGHEST

D_MODEL = 1024
BATCH = 16
SEQ = 256
DEPTH = 2
DEC_BATCH = 2
DEC_SEQ = 1024
PAST_LEN = 512
GRID_W = 64
HEAD_DIM = 64
N_MOD = 6
RMS_EPS = 1e-6
A_HEADS = 8
A_KV_HEADS = 2
A_GROUP = A_HEADS // A_KV_HEADS
WINDOW = 128
ROPE_BASE = 10000.0
B_HEADS = 8
NA_ROWS = 8
NA_COLS = 16
C_DIM = 512
C_EMB = 33
C_FFN = 64
HYENA_MIN_DECAY = math.log(1e-2) / 1.5
HYENA_MAX_DECAY = math.log(1e-2) / 0.3
D_KDIM = 128
D_VDIM = 128
D_HEADS = 4
N_EXPERTS = 64
TOP_K = 8
D_EXPERT = 256
ROUTE_SCALE = 2.5
A_Q = A_HEADS * HEAD_DIM
A_KV = A_KV_HEADS * HEAD_DIM
B_W = B_HEADS * HEAD_DIM
ATTN_IN = A_Q + 2 * A_KV + 3 * B_W

T_CTX = BATCH * SEQ
T_LAT = DEC_BATCH * DEC_SEQ
T_ALL = T_CTX + T_LAT
N_CVEC = 1 + DEC_BATCH
CVEC_PAD = 8
TM = 512
MASK_NEG = -1e30
GLA_CHUNK = 64
GLA_SPAN = 256
GLA_MIN_SPANS = 2
HGRN_HEADS_PER_STEP = 4
DFT_CHUNK = 256
MOE_BLK = 512
MOE_REGIONS = 1
K_PER_REGION = TOP_K // MOE_REGIONS
MOE_NBLK = -(-(T_ALL * K_PER_REGION + N_EXPERTS * (MOE_BLK - 1)) // MOE_BLK)
MOE_ROWS = MOE_NBLK * MOE_BLK
SC_CORES = 2
SC_SUBCORES = 16
SC_WORKERS = SC_CORES * SC_SUBCORES
DISP_CHUNK = 128
DISP_SPLIT = 2
COLLECT_CHUNK = 64
VMEM_LIMIT = 56 * 1024 * 1024


def _cparams(*sem):
    return pltpu.CompilerParams(dimension_semantics=sem, vmem_limit_bytes=VMEM_LIMIT)


def _mod_row(i):
    return jnp.where(i < T_CTX // TM, 0, 1 + (i - T_CTX // TM) // (DEC_SEQ // TM))


def _dot(a, b):
    return jnp.dot(a.astype(BF16), b.astype(BF16), preferred_element_type=F32)


def _dot_nt(a, b):
    return lax.dot_general(a.astype(BF16), b.astype(BF16), (((1,), (1,)), ((), ())),
                           preferred_element_type=F32)


def _dot_tn(a, b):
    return lax.dot_general(a.astype(BF16), b.astype(BF16), (((0,), (0,)), ((), ())),
                           preferred_element_type=F32)


def _dot_hi(a, b):
    return jnp.dot(a, b, precision=HI, preferred_element_type=F32)


def _split_bf16(x):
    hi = x.astype(BF16)
    return hi, (x - hi.astype(F32)).astype(BF16)


def _dot_split(a, b):
    a_hi, a_lo = _split_bf16(a)
    b_hi, b_lo = _split_bf16(b)
    dot = lambda x, y: jnp.dot(x, y, preferred_element_type=F32)
    return dot(a_hi, b_hi) + dot(a_hi, b_lo) + dot(a_lo, b_hi)


def _silu(x):
    return x * jax.nn.sigmoid(x)


def _rms(x, g):
    return x * lax.rsqrt(jnp.mean(x * x, axis=-1, keepdims=True) + RMS_EPS) * g


ADA_TN = 1536
ADA_UNROLL = 4


def _ada_body(cb_ref, w_ref, b_ref, o_ref):
    tn = o_ref.shape[-1]
    n_slab = tn // LANES

    def step(k8, accs):
        r0 = pl.multiple_of(k8 * 8, 8)
        sk = [_silu(cb_ref[j, pl.ds(r0, 8), :]) for j in range(N_CVEC)]
        out = []
        for s in range(n_slab):
            wk = w_ref[0, pl.ds(r0, 8), s * LANES:(s + 1) * LANES]
            out.extend(accs[s * N_CVEC + j] + wk * sk[j] for j in range(N_CVEC))
        return tuple(out)

    accs = lax.fori_loop(0, D_MODEL // 8, step,
                         tuple(jnp.zeros((8, LANES), F32) for _ in range(n_slab * N_CVEC)), unroll=ADA_UNROLL)
    o_ref[0] = jnp.zeros((CVEC_PAD, tn), F32)
    for s in range(n_slab):
        for j in range(N_CVEC):
            o_ref[0, j:j + 1, s * LANES:(s + 1) * LANES] = (
                jnp.sum(accs[s * N_CVEC + j], axis=0, keepdims=True) + b_ref[0, :, s * LANES:(s + 1) * LANES])


def _ada(c_lanes, layer, w_ada, b_ada):
    n_out = N_MOD * D_MODEL
    return pl.pallas_call(
        _ada_body,
        grid=(n_out // ADA_TN,),
        in_specs=[pl.BlockSpec((N_CVEC, D_MODEL, LANES), lambda n: (0, 0, 0)),
                  pl.BlockSpec((1, D_MODEL, ADA_TN), lambda n: (layer, 0, n)),
                  pl.BlockSpec((1, 1, ADA_TN), lambda n: (layer, 0, n))],
        out_specs=pl.BlockSpec((1, CVEC_PAD, ADA_TN), lambda n: (0, 0, n)),
        out_shape=jax.ShapeDtypeStruct((1, CVEC_PAD, n_out), F32),
        compiler_params=_cparams("parallel"),
        name="ada",
    )(c_lanes, w_ada, b_ada.reshape(DEPTH, 1, n_out))


N_CTX_TILES = T_CTX // TM


def _token_specs(x, width):
    if not isinstance(x, tuple):
        return [pl.BlockSpec((TM, width), lambda i: (i, 0))], (x,)
    return ([pl.BlockSpec((TM, width), lambda i: (jnp.minimum(i, N_CTX_TILES - 1), 0)),
             pl.BlockSpec((TM, width), lambda i: (jnp.maximum(i - N_CTX_TILES, 0), 0))], x)


def _token_tile(refs):
    if len(refs) == 1:
        return refs[0][...]
    return jnp.where(pl.program_id(0) < N_CTX_TILES, refs[0][...], refs[1][...])


def _inproj_body(*refs, n_x):
    x_refs, (mod_ref, g_ref, w_ref, o_ref, w_bf) = refs[:n_x], refs[n_x:]

    @pl.when(pl.program_id(0) == 0)
    def _():
        w_bf[...] = w_ref[...].astype(BF16)

    m = mod_ref[0]
    h = _rms(_token_tile(x_refs), g_ref[...]) * (1.0 + m[:, D_MODEL:2 * D_MODEL]) + m[:, 0:D_MODEL]
    o_ref[...] = _dot(h, w_bf[...])


def _inproj(x, mod_l, gain, w):
    n = w.shape[1]
    x_specs, x_args = _token_specs(x, D_MODEL)
    return pl.pallas_call(
        functools.partial(_inproj_body, n_x=len(x_args)),
        grid=(T_ALL // TM,),
        in_specs=x_specs + [pl.BlockSpec((1, 1, N_MOD * D_MODEL), lambda i: (_mod_row(i), 0, 0)),
                            pl.BlockSpec((1, D_MODEL), lambda i: (0, 0)),
                            pl.BlockSpec((D_MODEL, n), lambda i: (0, 0), pipeline_mode=pl.Buffered(1))],
        out_specs=pl.BlockSpec((TM, n), lambda i: (i, 0)),
        out_shape=jax.ShapeDtypeStruct((T_ALL, n), F32),
        scratch_shapes=[pltpu.VMEM((D_MODEL, n), BF16)],
        compiler_params=_cparams("arbitrary"),
        name="inproj",
    )(*x_args, mod_l, gain.reshape(1, D_MODEL), w)


def _ctx_attn_body(qkv_ref, sink_ref, oa_ref, ob_ref, ak_ref, av_ref, bk_ref, bv_ref):
    scale = HEAD_DIM ** -0.5
    lane = lax.broadcasted_iota(jnp.int32, (SEQ, LANES), 1)
    in_half = [lane < HEAD_DIM, lane >= HEAD_DIM]

    def attend(q, k, v, sink):
        s = _dot_nt(q, k) * scale
        m = jnp.max(s, axis=-1, keepdims=True)
        if sink is not None:
            m = jnp.maximum(m, sink)
        p = jnp.exp(s - m)
        den = jnp.sum(p, axis=-1, keepdims=True)
        if sink is not None:
            den = den + jnp.exp(sink - m)
        return _dot(p, v) / den

    def tile(first_col, t):
        return qkv_ref[:, first_col + t * LANES:first_col + (t + 1) * LANES]

    base = A_Q + 2 * A_KV
    for hk in range(A_KV_HEADS):
        dst = pl.ds(hk, SEQ, stride=A_KV_HEADS)
        ak_ref[0, dst, :] = qkv_ref[:, A_Q + hk * HEAD_DIM:A_Q + (hk + 1) * HEAD_DIM]
        av_ref[0, dst, :] = qkv_ref[:, A_Q + A_KV + hk * HEAD_DIM:A_Q + A_KV + (hk + 1) * HEAD_DIM]
    for h in range(B_HEADS):
        dst = pl.ds(h, SEQ, stride=B_HEADS)
        bk_ref[0, dst, :] = qkv_ref[:, base + B_W + h * HEAD_DIM:base + B_W + (h + 1) * HEAD_DIM]
        bv_ref[0, dst, :] = qkv_ref[:, base + 2 * B_W + h * HEAD_DIM:base + 2 * B_W + (h + 1) * HEAD_DIM]

    k_t, v_t = tile(A_Q, 0), tile(A_Q + A_KV, 0)
    k_sw, v_sw = pltpu.roll(k_t, HEAD_DIM, axis=1), pltpu.roll(v_t, HEAD_DIM, axis=1)
    tiles_per_kv = A_GROUP // HEADS_PER_TILE
    for hk in range(A_KV_HEADS):
        q_tiles = [tile(0, hk * tiles_per_kv + j) for j in range(tiles_per_kv)]
        halves = []
        for p in range(HEADS_PER_TILE):
            q = jnp.concatenate([jnp.where(in_half[p], qt, 0.0) for qt in q_tiles], axis=0)
            heads = [(hk * tiles_per_kv + j) * HEADS_PER_TILE + p for j in range(tiles_per_kv)]
            sink = jnp.concatenate([jnp.broadcast_to(sink_ref[:, h:h + 1], (SEQ, 1)) for h in heads], axis=0)
            halves.append(attend(q, k_t if p == hk else k_sw, v_t if p == hk else v_sw, sink))
        first_half = lax.broadcasted_iota(jnp.int32, halves[0].shape, 1) < HEAD_DIM
        o = jnp.where(first_half, halves[0], halves[1])
        for j in range(tiles_per_kv):
            t = hk * tiles_per_kv + j
            oa_ref[:, t * LANES:(t + 1) * LANES] = o[j * SEQ:(j + 1) * SEQ]

    for t in range(B_HEADS // HEADS_PER_TILE):
        q_t, k_b, v_b = tile(base, t), tile(base + B_W, t), tile(base + 2 * B_W, t)
        halves = [attend(jnp.where(in_half[p], q_t, 0.0), k_b, v_b, None) for p in range(HEADS_PER_TILE)]
        ob_ref[:, t * LANES:(t + 1) * LANES] = jnp.where(in_half[0], halves[0], halves[1])


def _ctx_attn(qkv, sink):
    kv_spec = lambda heads: pl.BlockSpec((1, SEQ * heads, HEAD_DIM), lambda b: (b, 0, 0))
    kv_sd = lambda heads: jax.ShapeDtypeStruct((BATCH, SEQ * heads, HEAD_DIM), F32)
    outs = pl.pallas_call(
        _ctx_attn_body,
        grid=(BATCH,),
        in_specs=[pl.BlockSpec((SEQ, ATTN_IN), lambda b: (b, 0)),
                  pl.BlockSpec((1, A_HEADS), lambda b: (0, 0))],
        out_specs=[pl.BlockSpec((SEQ, A_Q), lambda b: (b, 0)), pl.BlockSpec((SEQ, B_W), lambda b: (b, 0)),
                   kv_spec(A_KV_HEADS), kv_spec(A_KV_HEADS), kv_spec(B_HEADS), kv_spec(B_HEADS)],
        out_shape=[jax.ShapeDtypeStruct((T_CTX, A_Q), F32), jax.ShapeDtypeStruct((T_CTX, B_W), F32),
                   kv_sd(A_KV_HEADS), kv_sd(A_KV_HEADS), kv_sd(B_HEADS), kv_sd(B_HEADS)],
        compiler_params=_cparams("parallel"),
        name="ctx_attn",
    )(qkv, sink.reshape(1, A_HEADS))
    caches = [t.reshape(BATCH, 1, SEQ, -1, HEAD_DIM) for t in outs[2:]]
    return outs[0], outs[1], *caches


@functools.lru_cache(maxsize=None)
def _rope_tables(width):
    half = HEAD_DIM // 2
    t = np.arange(DEC_SEQ)
    inv = ROPE_BASE ** (-np.arange(0, half, 2, dtype=np.float64) / half)
    ang_r = (t // GRID_W)[:, None] * inv[None, :]
    ang_c = (t % GRID_W)[:, None] * inv[None, :]
    cos = np.concatenate([np.cos(ang_r)] * 2 + [np.cos(ang_c)] * 2, axis=-1)
    sin = np.concatenate([-np.sin(ang_r), np.sin(ang_r), -np.sin(ang_c), np.sin(ang_c)], axis=-1)
    reps = width // HEAD_DIM
    return (np.tile(cos, (1, reps)).astype(np.float32), np.tile(sin, (1, reps)).astype(np.float32))


def _rope_body(q_ref, k_ref, cq_ref, sq_ref, ck_ref, sk_ref, qo_ref, ko_ref):
    quarter = HEAD_DIM // 4

    def rot(x, cos, sin):
        w = x.shape[-1]
        lane = lax.broadcasted_iota(jnp.int32, x.shape, 1)
        fwd = pltpu.roll(x, w - quarter, axis=1)
        bwd = pltpu.roll(x, quarter, axis=1)
        partner = jnp.where((lane & (2 * quarter - 1)) < quarter, fwd, bwd)
        return x * cos + partner * sin

    qo_ref[...] = rot(q_ref[...], cq_ref[...], sq_ref[...])
    ko_ref[...] = rot(k_ref[...], ck_ref[...], sk_ref[...])


def _rope(qkv):
    cq, sq = _rope_tables(A_Q)
    ck, sk = _rope_tables(A_KV)
    tab = lambda w: pl.BlockSpec((DEC_SEQ, w), lambda b: (0, 0))
    row0 = T_CTX // DEC_SEQ
    return pl.pallas_call(
        _rope_body,
        grid=(DEC_BATCH,),
        in_specs=[pl.BlockSpec((DEC_SEQ, A_Q), lambda b: (row0 + b, 0)),
                  pl.BlockSpec((DEC_SEQ, A_KV), lambda b: (row0 + b, A_Q // A_KV)),
                  tab(A_Q), tab(A_Q), tab(A_KV), tab(A_KV)],
        out_specs=[pl.BlockSpec((DEC_SEQ, A_Q), lambda b: (b, 0)),
                   pl.BlockSpec((DEC_SEQ, A_KV), lambda b: (b, 0))],
        out_shape=[jax.ShapeDtypeStruct((T_LAT, A_Q), F32), jax.ShapeDtypeStruct((T_LAT, A_KV), F32)],
        compiler_params=_cparams("parallel"),
        name="rope",
    )(qkv, qkv, jnp.asarray(cq), jnp.asarray(sq), jnp.asarray(ck), jnp.asarray(sk))


WIN_QB = 256


def _win_attn_body(qraw_ref, qrot_ref, krot_ref, v_ref, kc_ref, vc_ref, sink_ref, o_ref):
    scale = HEAD_DIM ** -0.5
    hk = pl.program_id(1)
    tiles = A_GROUP // HEADS_PER_TILE

    def kv_in_half(x):
        swapped = pltpu.roll(x, HEAD_DIM, axis=1)
        return [jnp.where(hk == p, x, swapped) for p in range(HEADS_PER_TILE)]

    k, v, kc, vc = kv_in_half(krot_ref[...]), kv_in_half(v_ref[...]), kv_in_half(kc_ref[0]), kv_in_half(vc_ref[0])
    head_lane = lax.broadcasted_iota(jnp.int32, (1, A_HEADS), 1)

    def sink_rows(p):
        heads = [hk * A_GROUP + j * HEADS_PER_TILE + p for j in range(tiles)]
        vals = [jnp.sum(jnp.where(head_lane == h, sink_ref[...], 0.0), axis=-1, keepdims=True) for h in heads]
        return jnp.concatenate([jnp.broadcast_to(s, (WIN_QB, 1)) for s in vals], axis=0)

    sinks = [sink_rows(p) for p in range(HEADS_PER_TILE)]
    lane = lax.broadcasted_iota(jnp.int32, (tiles * WIN_QB, LANES), 1)
    in_half = [lane < HEAD_DIM, lane >= HEAD_DIM]
    for qb in range(DEC_SEQ // WIN_QB):
        q0 = qb * WIN_QB
        rows = slice(q0, q0 + WIN_QB)
        lo = max(0, q0 - WINDOW)
        hi = min(DEC_SEQ, q0 + WIN_QB + WINDOW)
        q_rot = jnp.concatenate([qrot_ref[rows, j * LANES:(j + 1) * LANES] for j in range(tiles)], axis=0)
        q_raw = jnp.concatenate([qraw_ref[rows, j * LANES:(j + 1) * LANES] for j in range(tiles)], axis=0)
        halves = []
        for p in range(HEADS_PER_TILE):
            s_loc = _dot_nt(jnp.where(in_half[p], q_rot, 0.0), k[p][lo:hi]) * scale
            qpos = q0 + (lax.broadcasted_iota(jnp.int32, s_loc.shape, 0) & (WIN_QB - 1))
            kpos = lo + lax.broadcasted_iota(jnp.int32, s_loc.shape, 1)
            s_loc = jnp.where(jnp.abs(kpos - qpos) <= WINDOW, s_loc, MASK_NEG)
            s_ctx = _dot_nt(jnp.where(in_half[p], q_raw, 0.0), kc[p]) * scale
            m = jnp.maximum(jnp.maximum(jnp.max(s_loc, axis=-1, keepdims=True),
                                        jnp.max(s_ctx, axis=-1, keepdims=True)), sinks[p])
            p_loc = jnp.exp(s_loc - m)
            p_ctx = jnp.exp(s_ctx - m)
            den = (jnp.sum(p_loc, axis=-1, keepdims=True) + jnp.sum(p_ctx, axis=-1, keepdims=True)
                   + jnp.exp(sinks[p] - m))
            halves.append((_dot(p_ctx, vc[p]) + _dot(p_loc, v[p][lo:hi])) / den)
        o = jnp.where(in_half[0], halves[0], halves[1])
        for j in range(tiles):
            o_ref[rows, j * LANES:(j + 1) * LANES] = o[j * WIN_QB:(j + 1) * WIN_QB]


def _win_attn(qkv, q_rot, k_rot, kc, vc, sink):
    row0 = T_CTX // DEC_SEQ
    gw = A_GROUP * HEAD_DIM
    return pl.pallas_call(
        _win_attn_body,
        grid=(DEC_BATCH, A_KV_HEADS),
        in_specs=[pl.BlockSpec((DEC_SEQ, gw), lambda b, h: (row0 + b, h)),
                  pl.BlockSpec((DEC_SEQ, gw), lambda b, h: (b, h)),
                  pl.BlockSpec((DEC_SEQ, A_KV), lambda b, h: (b, 0)),
                  pl.BlockSpec((DEC_SEQ, A_KV), lambda b, h: (row0 + b, (A_Q + A_KV) // A_KV)),
                  pl.BlockSpec((1, PAST_LEN, A_KV), lambda b, h: (b, 0, 0)),
                  pl.BlockSpec((1, PAST_LEN, A_KV), lambda b, h: (b, 0, 0)),
                  pl.BlockSpec((1, A_HEADS), lambda b, h: (0, 0))],
        out_specs=pl.BlockSpec((DEC_SEQ, gw), lambda b, h: (b, h)),
        out_shape=jax.ShapeDtypeStruct((T_LAT, A_Q), F32),
        compiler_params=_cparams("parallel", "parallel"),
        name="win_attn",
    )(qkv, q_rot, k_rot, qkv, kc, vc, sink.reshape(1, A_HEADS))


GRID_ROWS = DEC_SEQ // GRID_W
NA_BAND = min(NA_ROWS, GRID_ROWS)


NA_REL_ROWS = 2 * NA_ROWS - 1
NA_REL_COLS = 2 * NA_COLS - 1
LANES = 128
HEADS_PER_TILE = LANES // HEAD_DIM


def _na_rel_rows(rpb):
    pad = jnp.zeros((B_HEADS, NA_REL_ROWS, GRID_W - NA_REL_COLS), F32)
    one = jnp.concatenate([rpb, pad], axis=-1)
    nxt = jnp.concatenate([one[:, 1:], jnp.zeros((B_HEADS, 1, GRID_W), F32)], axis=1)
    both = jnp.concatenate([one, nxt], axis=-1)
    return jnp.concatenate([both, jnp.zeros((B_HEADS, 16 - NA_REL_ROWS, LANES), F32)], axis=1)


NA_HEADS_PER_STEP = LANES // HEAD_DIM


def _na_row_groups():
    groups = []
    for r in range(GRID_ROWS):
        rs = min(max(r - NA_ROWS // 2, 0), GRID_ROWS - NA_BAND)
        if groups and groups[-1][2] == rs:
            groups[-1][1] += 1
        else:
            groups.append([r, 1, rs])
    return groups


def _na_attn_body(q_ref, k_ref, v_ref, kc_ref, vc_ref, rel_ref, o_ref):
    scale = HEAD_DIM ** -0.5
    cq = lax.broadcasted_iota(jnp.int32, (GRID_W, LANES), 0)
    kcol = lax.broadcasted_iota(jnp.int32, (GRID_W, LANES), 1) & (GRID_W - 1)
    cs = jnp.clip(cq - NA_COLS // 2, 0, GRID_W - NA_COLS)
    col_ok = (kcol >= cs) & (kcol < cs + NA_COLS)
    kc = kc_ref[0]
    vc = vc_ref[0]
    tiles = {}

    def pair_tile(hh, a):
        if (hh, a) not in tiles:
            x = jnp.broadcast_to(rel_ref[hh, a:a + 1, :], (GRID_W, LANES))
            t = pltpu.roll(x, LANES - (NA_COLS - 1), axis=1, stride=1, stride_axis=0)
            tiles[hh, a] = jnp.where(col_ok, t, MASK_NEG)
        return tiles[hh, a]

    for r0, n_r, rs in _na_row_groups():
        rows = slice(r0 * GRID_W, (r0 + n_r) * GRID_W)
        band = slice(rs * GRID_W, (rs + NA_BAND) * GRID_W)
        q_t, k_t, v_t = q_ref[rows, :], k_ref[band, :], v_ref[band, :]
        head_of_lane = lax.broadcasted_iota(jnp.int32, q_t.shape, 1) >> (HEAD_DIM.bit_length() - 1)
        o = jnp.zeros(q_t.shape, F32)
        for hh in range(NA_HEADS_PER_STEP):
            bias = jnp.concatenate(
                [jnp.concatenate([pair_tile(hh, rs - r + NA_ROWS - 1 + 2 * i) for i in range(NA_BAND // 2)], axis=1)
                 for r in range(r0, r0 + n_r)], axis=0)
            q = jnp.where(head_of_lane == hh, q_t, 0.0)
            s_loc = _dot_nt(q, k_t) * scale + bias
            s_ctx = _dot_nt(q, kc) * scale
            m = jnp.maximum(jnp.max(s_loc, axis=-1, keepdims=True), jnp.max(s_ctx, axis=-1, keepdims=True))
            p_loc = jnp.exp(s_loc - m)
            p_ctx = jnp.exp(s_ctx - m)
            den = jnp.sum(p_loc, axis=-1, keepdims=True) + jnp.sum(p_ctx, axis=-1, keepdims=True)
            o = jnp.where(head_of_lane == hh, (_dot(p_ctx, vc) + _dot(p_loc, v_t)) / den, o)
        o_ref[rows, :] = o


def _na_attn(qkv, kc, vc, rel):
    row0 = T_CTX // DEC_SEQ
    col0 = (A_Q + 2 * A_KV) // LANES
    n_blk = B_W // LANES
    col = lambda j: pl.BlockSpec((DEC_SEQ, LANES), lambda b, p: (row0 + b, col0 + j * n_blk + p))
    cache = pl.BlockSpec((1, PAST_LEN, LANES), lambda b, p: (b, 0, p))
    return pl.pallas_call(
        _na_attn_body,
        grid=(DEC_BATCH, n_blk),
        in_specs=[col(0), col(1), col(2), cache, cache,
                  pl.BlockSpec((NA_HEADS_PER_STEP, 16, LANES), lambda b, p: (p, 0, 0))],
        out_specs=pl.BlockSpec((DEC_SEQ, LANES), lambda b, p: (b, p)),
        out_shape=jax.ShapeDtypeStruct((T_LAT, B_W), F32),
        compiler_params=_cparams("parallel", "parallel"),
        name="na_attn",
    )(qkv, qkv, qkv, kc, vc, rel)


@functools.lru_cache(maxsize=None)
def _dft_mats(L):
    n = 2 * L
    fc = min(L, DFT_CHUNK)
    f = np.arange(L)[:, None]
    t = np.arange(L)[None, :]
    ang = 2.0 * np.pi * ((f * t) % n) / n
    m1 = np.cos(ang)
    m2 = np.sin(ang)
    m2[0, :] = np.where(np.arange(L) % 2 == 0, 1.0, -1.0)
    wgt = np.full((L, 1), 2.0)
    wgt[0, 0] = 1.0
    nch = L // fc
    fwd = np.concatenate([m1.reshape(nch, fc, L), m2.reshape(nch, fc, L)], axis=1)
    inv = np.concatenate([(m1 * wgt / n).reshape(nch, fc, L), (m2 * wgt / n).reshape(nch, fc, L)], axis=1)
    inv = np.transpose(inv, (0, 2, 1))
    return fwd.astype(np.float32), inv.astype(np.float32)


@functools.lru_cache(maxsize=None)
def _filter_consts(L):
    t = np.linspace(0.0, 1.0, L)[:, None]
    bands = (C_EMB - 1) // 2
    ang = (2.0 * math.pi / L) * np.arange(L)[:, None] * np.linspace(1e-4, bands - 1, bands)[None, :]
    z = np.concatenate([t, np.cos(ang), -np.sin(ang)], axis=-1)
    zpad = np.zeros((L, 128))
    zpad[:, :C_EMB] = z
    deltas = np.abs(np.linspace(HYENA_MIN_DECAY, HYENA_MAX_DECAY, C_DIM))
    window = np.exp(-t * deltas[None, :])
    return zpad.astype(np.float32), window.astype(np.float32)


def _filter_body(z_ref, w1_ref, b1_ref, w2_ref, b2_ref, w3_ref, b3_ref, fr_ref, w4_ref, win_ref, fm_ref,
                 hr_ref, g_ref, hq_ref, hs_scr, hd_scr):
    c = pl.program_id(0)
    fc = hr_ref.shape[0]

    @pl.when(c == 0)
    def _():
        fr = fr_ref[...]
        hh = jnp.sin(fr * (_dot_hi(z_ref[...], w1_ref[...]) + b1_ref[...]))
        hh = jnp.sin(fr * (_dot_hi(hh, w2_ref[...]) + b2_ref[...]))
        hh = jnp.sin(fr * (_dot_hi(hh, w3_ref[...]) + b3_ref[...]))
        hh = _dot_hi(hh, w4_ref[...])
        hf = hh[:, :C_DIM] * win_ref[...]
        hb = hh[:, C_DIM:] * win_ref[...]
        hs_scr[...] = hf + hb
        hd_scr[...] = hf - hb

    fm = fm_ref[0]
    hr = _dot_split(fm[:fc], hs_scr[...])
    first = (lax.broadcasted_iota(jnp.int32, (fc, C_DIM), 0) == 0) & (c == 0)
    hr_ref[...] = hr
    g_ref[...] = jnp.where(first, 0.0, _dot_split(fm[fc:], hd_scr[...]))
    hs = hs_scr[...]
    sign = jnp.where((lax.broadcasted_iota(jnp.int32, hs.shape, 0) & 1) == 0, 1.0, -1.0)
    hq_ref[...] = jnp.where(first, jnp.sum(hs * sign, axis=0, keepdims=True), hr)


def _hyena_filter(L, filt):
    w1, b1, w2, b2, w3, b3, freq, w4 = filt
    zpad, window = _filter_consts(L)
    fwd, _ = _dft_mats(L)
    nch, fc2, _ = fwd.shape
    fc = fc2 // 2
    w1p = jnp.pad(w1, ((0, 128 - C_EMB), (0, 0)))
    full = lambda shape: pl.BlockSpec(shape, lambda c: tuple(0 for _ in shape))
    out_spec = pl.BlockSpec((fc, C_DIM), lambda c: (c, 0))
    out_sd = jax.ShapeDtypeStruct((L, C_DIM), F32)
    return pl.pallas_call(
        _filter_body,
        grid=(nch,),
        in_specs=[full((L, 128)), full((128, C_FFN)), full((1, C_FFN)), full((C_FFN, C_FFN)), full((1, C_FFN)),
                  full((C_FFN, C_FFN)), full((1, C_FFN)), full((1, C_FFN)), full((C_FFN, 2 * C_DIM)),
                  full((L, C_DIM)), pl.BlockSpec((1, fc2, L), lambda c: (c, 0, 0))],
        out_specs=[out_spec, out_spec, out_spec],
        out_shape=[out_sd, out_sd, out_sd],
        scratch_shapes=[pltpu.VMEM((L, C_DIM), F32), pltpu.VMEM((L, C_DIM), F32)],
        compiler_params=_cparams("arbitrary"),
        name="hyena_filter",
    )(jnp.asarray(zpad), w1p, b1.reshape(1, C_FFN), w2, b2.reshape(1, C_FFN), w3, b3.reshape(1, C_FFN),
      freq.reshape(1, C_FFN), w4, jnp.asarray(window), jnp.asarray(fwd))


def _hyena_body(u_ref, cw_ref, cb_ref, d_ref, fm_ref, fi_ref, hr_ref, g_ref, hq_ref, y_ref,
                x0_scr, z_scr, acc_scr):
    c = pl.program_id(1)
    L = y_ref.shape[0]
    fc = hr_ref.shape[0]

    @pl.when(c == 0)
    def _():
        row = lax.broadcasted_iota(jnp.int32, (L, C_DIM), 0)

        def short_conv(sec):
            cols = slice(sec * C_DIM, (sec + 1) * C_DIM)
            u = u_ref[:, cols]
            prev = jnp.where(row == 0, 0.0, pltpu.roll(u, 1, axis=0))
            nxt = jnp.where(row == L - 1, 0.0, pltpu.roll(u, L - 1, axis=0))
            return (prev * cw_ref[0:1, cols] + u * cw_ref[1:2, cols] + nxt * cw_ref[2:3, cols]
                    + cb_ref[:, cols])

        x0_scr[...] = short_conv(0)
        z_scr[...] = short_conv(1) * short_conv(2)
        acc_scr[...] = jnp.zeros((L, C_DIM), F32)

    ab = _dot_split(fm_ref[0], z_scr[...])
    a, b = ab[:fc], ab[fc:]
    hr, g, hq = hr_ref[...], g_ref[...], hq_ref[...]
    pq = jnp.concatenate([a * hr - b * g, a * g + b * hq], axis=0)
    acc_scr[...] += _dot_split(fi_ref[0], pq)

    @pl.when(c == pl.num_programs(1) - 1)
    def _():
        y_ref[...] = x0_scr[...] * (acc_scr[...] + z_scr[...] * d_ref[...])


def _hyena(u, row_blk0, n_seq, L, conv_w, conv_b, d_skip, spec):
    hr, g, hq = spec
    fwd, inv = _dft_mats(L)
    nch, fc2, _ = fwd.shape
    fc = fc2 // 2
    u_w = 3 * C_DIM
    return pl.pallas_call(
        _hyena_body,
        grid=(n_seq, nch),
        in_specs=[pl.BlockSpec((L, u_w), lambda b, c: (row_blk0 + b, 0)),
                  pl.BlockSpec((3, u_w), lambda b, c: (0, 0)),
                  pl.BlockSpec((1, u_w), lambda b, c: (0, 0)),
                  pl.BlockSpec((1, C_DIM), lambda b, c: (0, 0)),
                  pl.BlockSpec((1, fc2, L), lambda b, c: (c, 0, 0)),
                  pl.BlockSpec((1, L, fc2), lambda b, c: (c, 0, 0)),
                  pl.BlockSpec((fc, C_DIM), lambda b, c: (c, 0)),
                  pl.BlockSpec((fc, C_DIM), lambda b, c: (c, 0)),
                  pl.BlockSpec((fc, C_DIM), lambda b, c: (c, 0))],
        out_specs=pl.BlockSpec((L, C_DIM), lambda b, c: (b, 0)),
        out_shape=jax.ShapeDtypeStruct((n_seq * L, C_DIM), F32),
        scratch_shapes=[pltpu.VMEM((L, C_DIM), F32)] * 3,
        compiler_params=_cparams("parallel", "arbitrary"),
        name="hyena",
    )(u, conv_w, conv_b.reshape(1, u_w), d_skip.reshape(1, C_DIM), jnp.asarray(fwd), jnp.asarray(inv), hr, g, hq)


def _hgrn_body(q_ref, ff_ref, fb_ref, i_ref, g_ref, lbf_ref, lbb_ref, nd_ref, s0f_ref, s0b_ref,
               o_ref, sf_ref, sb_ref, *, layer):
    L = o_ref.shape[0]
    C = GLA_CHUNK
    S = min(L // GLA_MIN_SPANS, GLA_SPAN)
    nc = S // C
    n_span = L // S
    mid = C // 2
    def lower_bound(gm):
        e = jnp.exp(gm - jnp.max(gm, axis=0, keepdims=True))
        p = e / jnp.sum(e, axis=0, keepdims=True)
        return jnp.sum(p[0:layer + 1], axis=0, keepdims=True) - p[0:1]

    def gates(fx, lb):
        f = lb + (1.0 - lb) * jax.nn.sigmoid(fx)
        return 1.0 - f, jnp.log(f)


    chunk_shift = C.bit_length() - 1
    block_shift = D_KDIM.bit_length() - 1
    ti = lax.broadcasted_iota(jnp.int32, (S, S), 0)
    si = lax.broadcasted_iota(jnp.int32, (S, S), 1)
    same_chunk = (ti >> chunk_shift) == (si >> chunk_shift)
    causal = same_chunk & (si <= ti)
    anti = same_chunk & (si >= ti)
    row_chunk = lax.broadcasted_iota(jnp.int32, (S, nc * D_KDIM), 0) >> chunk_shift
    col_chunk = lax.broadcasted_iota(jnp.int32, (S, nc * D_KDIM), 1) >> block_shift
    own_block = row_chunk == col_chunk

    def spread(x):
        return jnp.where(own_block, jnp.concatenate([x] * nc, axis=1), 0.0)

    def chunk_cumsum(mask, lg):
        tri = mask.astype(BF16)
        hi = lg.astype(BF16)
        r1 = lg - hi.astype(F32)
        mid_t = r1.astype(BF16)
        lo = (r1 - mid_t.astype(F32)).astype(BF16)
        dot = lambda t: jnp.dot(tri, t, preferred_element_type=F32)
        return dot(hi) + dot(mid_t) + dot(lo)

    def per_chunk_rows(b, pos):
        return jnp.concatenate([jnp.broadcast_to(b[n * C + pos:n * C + pos + 1], (C, D_KDIM)) for n in range(nc)],
                               axis=0)

    def one_head(q, v, kf, lgf, kb, lgb, st_f, st_b):
        local = []
        for u in range(n_span):
            rows = slice(u * S, (u + 1) * S)
            qs, vs, kfs, kbs = q[rows], v[rows], kf[rows], kb[rows]
            lgs = jnp.concatenate([lgf[rows], lgb[rows]], axis=1)
            pre = chunk_cumsum(causal, lgs)
            b_f = pre[:, :D_KDIM]
            pre_b = pre[:, D_KDIM:]
            b_b = per_chunk_rows(pre_b, C - 1) - pre_b + lgb[rows]
            ref_f, ref_b = per_chunk_rows(b_f, mid), per_chunk_rows(b_b, mid)
            sc = (jnp.where(causal, _dot_nt(qs * jnp.exp(b_f - ref_f), kfs * jnp.exp(ref_f - b_f)), 0.0)
                  + jnp.where(anti, _dot_nt(qs * jnp.exp(b_b - ref_b), kbs * jnp.exp(ref_b - b_b)), 0.0))
            k_out = jnp.concatenate([kfs * jnp.exp(per_chunk_rows(b_f, C - 1) - b_f),
                                     kbs * jnp.exp(per_chunk_rows(b_b, 0) - b_b)], axis=1)
            kv_t = _dot_tn(spread(vs), k_out)
            local.append((_dot(sc, vs), kv_t, b_f, b_b, qs))

        states_f = [[None] * nc for _ in range(n_span)]
        for u in range(n_span):
            _, kv_t, b_f, _, _ = local[u]
            for n in range(nc):
                states_f[u][n] = st_f
                st_f = st_f * jnp.exp(b_f[n * C + C - 1:n * C + C]) + kv_t[n * D_VDIM:(n + 1) * D_VDIM, :D_KDIM]
        states_b = [[None] * nc for _ in range(n_span)]
        for u in reversed(range(n_span)):
            _, kv_t, _, b_b, _ = local[u]
            for n in reversed(range(nc)):
                states_b[u][n] = st_b
                st_b = st_b * jnp.exp(b_b[n * C:n * C + 1]) + kv_t[n * D_VDIM:(n + 1) * D_VDIM, D_KDIM:]

        outs = []
        for u in range(n_span):
            intra, _, b_f, b_b, qs = local[u]
            q_in = jnp.concatenate([spread(qs * jnp.exp(b_f)), spread(qs * jnp.exp(b_b))], axis=1)
            outs.append(intra + _dot_nt(q_in, jnp.concatenate(states_f[u] + states_b[u], axis=1)))
        return (jnp.concatenate(outs, axis=0) if n_span > 1 else outs[0]), st_f, st_b

    for hh in range(o_ref.shape[1] // D_VDIM):
        cols = slice(hh * D_KDIM, (hh + 1) * D_KDIM)
        kf, lgf = gates(ff_ref[:, cols], lower_bound(lbf_ref[:, cols]))
        kb, lgb = gates(fb_ref[:, cols], lower_bound(lbb_ref[:, cols]))
        o, st_f, st_b = one_head(_silu(q_ref[:, cols]), i_ref[:, cols], kf, lgf, kb, lgb,
                                 jnp.transpose(s0f_ref[0, hh]), jnp.transpose(s0b_ref[0, hh]))
        sf_ref[0, hh] = jnp.transpose(st_f)
        sb_ref[0, hh] = jnp.transpose(st_b)
        o_ref[:, cols] = _rms(o, nd_ref[...]) * _silu(g_ref[:, cols])


def _hgrn(u, row_blk0, n_seq, L, lb_fwd, lb_bwd, norm_d, s0f, s0b, layer):
    hps = HGRN_HEADS_PER_STEP
    width = hps * D_KDIM
    col0 = 3 * C_DIM // width
    groups = D_HEADS // hps
    col = lambda j: pl.BlockSpec((L, width), lambda b, h: (row_blk0 + b, col0 + j * groups + h))
    lbs = pl.BlockSpec((DEPTH, width), lambda b, h: (0, h))
    st = pl.BlockSpec((1, hps, D_KDIM, D_VDIM), lambda b, h: (b, h, 0, 0))
    st_sd = jax.ShapeDtypeStruct((n_seq, D_HEADS, D_KDIM, D_VDIM), F32)
    return pl.pallas_call(
        functools.partial(_hgrn_body, layer=layer),
        grid=(n_seq, groups),
        in_specs=[col(0), col(1), col(2), col(3), col(4), lbs, lbs,
                  pl.BlockSpec((1, D_VDIM), lambda b, h: (0, 0)), st, st],
        out_specs=[pl.BlockSpec((L, width), lambda b, h: (b, h)), st, st],
        out_shape=[jax.ShapeDtypeStruct((n_seq * L, D_HEADS * D_VDIM), F32), st_sd, st_sd],
        compiler_params=_cparams("parallel", "parallel"),
        name="hgrn",
    )(u, u, u, u, u, lb_fwd, lb_bwd, norm_d.reshape(1, D_VDIM), s0f, s0b)


def _pack_bf16_pairs(h):
    n = h.shape[1] // 2
    hi = lax.bitcast_convert_type(h[:, :n].astype(BF16).astype(F32), jnp.int32)
    lo = lax.bitcast_convert_type(h[:, n:].astype(BF16).astype(F32), jnp.int32)
    return hi | lax.shift_right_logical(lo, 16)


def _unpack_bf16_pairs(p):
    hi = lax.bitcast_convert_type(p & jnp.int32(-65536), F32).astype(BF16)
    lo = lax.bitcast_convert_type(lax.shift_left(p, 16), F32).astype(BF16)
    return hi, lo


def _outproj_body(*refs, n_x):
    a_refs, b_refs, x_refs = refs[0:2], refs[2:4], refs[4:4 + n_x]
    mod_ref, gf_ref, w_ref, wrh_ref, wrl_ref, rb_ref, x1_ref, h2_ref, chosen_ref, gk_ref, ik_ref = refs[4 + n_x:]
    m = mod_ref[0]
    half = a_refs[0].shape[1]
    out = _dot(_token_tile(a_refs), w_ref[0:half, :]) + _dot(_token_tile(b_refs), w_ref[half:, :])
    x1 = _token_tile(x_refs) + m[:, 2 * D_MODEL:3 * D_MODEL] * out
    x1_ref[...] = x1
    h2 = _rms(x1, gf_ref[...]) * (1.0 + m[:, 4 * D_MODEL:5 * D_MODEL]) + m[:, 3 * D_MODEL:4 * D_MODEL]
    h2_ref[...] = _pack_bf16_pairs(h2)
    h_hi = h2.astype(BF16)
    h_lo = (h2 - h_hi.astype(F32)).astype(BF16)
    logits = _dot_nt(wrh_ref[...], h_hi) + _dot_nt(wrh_ref[...], h_lo) + _dot_nt(wrl_ref[...], h_hi)
    scores = jax.nn.sigmoid(logits)
    work = scores + rb_ref[...]
    expert = lax.broadcasted_iota(jnp.int32, work.shape, 0).astype(F32)
    slot = lax.broadcasted_iota(jnp.int32, (TOP_K, work.shape[1]), 0)
    chosen = [jnp.zeros(work.shape, F32) for _ in range(MOE_REGIONS)]
    gk = jnp.zeros((TOP_K, work.shape[1]), F32)
    ik = jnp.zeros((TOP_K, work.shape[1]), F32)
    for k in range(TOP_K):
        best = jnp.max(work, axis=0, keepdims=True)
        first = jnp.min(jnp.where(work == best, expert, float(N_EXPERTS)), axis=0, keepdims=True)
        hit = expert == first
        chosen[k // K_PER_REGION] = jnp.where(hit, 1.0, chosen[k // K_PER_REGION])
        gk = jnp.where(slot == k, jnp.sum(jnp.where(hit, scores, 0.0), axis=0, keepdims=True), gk)
        ik = jnp.where(slot == k, first, ik)
        work = jnp.where(hit, -jnp.inf, work)
    for r in range(MOE_REGIONS):
        chosen_ref[r] = chosen[r]
    gk_ref[...] = jnp.transpose(gk / jnp.sum(gk, axis=0, keepdims=True) * ROUTE_SCALE)
    ik_ref[...] = ik


def _outproj(a, b, x, mod_l, gain_ffn, w_out, w_router, router_bias):
    half = a[0].shape[1]
    a_specs, a_args = _token_specs(a, half)
    b_specs, b_args = _token_specs(b, half)
    x_specs, x_args = _token_specs(x, D_MODEL)
    wr_t = w_router.T
    wr_hi = wr_t.astype(BF16)
    wr_lo = (wr_t - wr_hi.astype(F32)).astype(BF16)
    return pl.pallas_call(
        functools.partial(_outproj_body, n_x=len(x_args)),
        grid=(T_ALL // TM,),
        in_specs=a_specs + b_specs + x_specs + [
                  pl.BlockSpec((1, 1, N_MOD * D_MODEL), lambda i: (_mod_row(i), 0, 0)),
                  pl.BlockSpec((1, D_MODEL), lambda i: (0, 0)),
                  pl.BlockSpec((2 * half, D_MODEL), lambda i: (0, 0)),
                  pl.BlockSpec((N_EXPERTS, D_MODEL), lambda i: (0, 0)),
                  pl.BlockSpec((N_EXPERTS, D_MODEL), lambda i: (0, 0)),
                  pl.BlockSpec((N_EXPERTS, 1), lambda i: (0, 0))],
        out_specs=[pl.BlockSpec((TM, D_MODEL), lambda i: (i, 0)),
                   pl.BlockSpec((TM, D_MODEL // 2), lambda i: (i, 0)),
                   pl.BlockSpec((MOE_REGIONS, N_EXPERTS, TM), lambda i: (0, 0, i)),
                   pl.BlockSpec((TM, TOP_K), lambda i: (i, 0)),
                   pl.BlockSpec((TOP_K, TM), lambda i: (0, i))],
        out_shape=[jax.ShapeDtypeStruct((T_ALL, D_MODEL), F32),
                   jax.ShapeDtypeStruct((T_ALL, D_MODEL // 2), jnp.int32),
                   jax.ShapeDtypeStruct((MOE_REGIONS, N_EXPERTS, T_ALL), F32),
                   jax.ShapeDtypeStruct((T_ALL, TOP_K), F32),
                   jax.ShapeDtypeStruct((TOP_K, T_ALL), F32)],
        compiler_params=_cparams("parallel"),
        name="outproj_router",
    )(*a_args, *b_args, *x_args, mod_l, gain_ffn.reshape(1, D_MODEL), w_out, wr_hi, wr_lo,
      router_bias.reshape(N_EXPERTS, 1))


def _route_body(chosen_ref, ik_ref, dest_ref, first_ref, count_ref, short_ref, pos_scr):
    n_tiles = T_ALL // TM
    r = lax.broadcasted_iota(jnp.int32, (TM, TM), 0)
    c = lax.broadcasted_iota(jnp.int32, (TM, TM), 1)
    before = (r < c).astype(BF16)

    counts = jnp.zeros((N_EXPERTS, 1), F32)
    for i in range(n_tiles):
        cols = slice(i * TM, (i + 1) * TM)
        m = chosen_ref[0, :, cols]
        pos_scr[:, cols] = jnp.dot(m.astype(BF16), before, preferred_element_type=F32) + counts
        counts = counts + jnp.sum(m, axis=1, keepdims=True)
    padded = jnp.ceil(counts * (1.0 / MOE_BLK)) * MOE_BLK
    ei = lax.broadcasted_iota(jnp.int32, (N_EXPERTS, N_EXPERTS), 0)
    ej = lax.broadcasted_iota(jnp.int32, (N_EXPERTS, N_EXPERTS), 1)
    end = _dot_hi((ej <= ei).astype(F32), jnp.broadcast_to(padded, (N_EXPERTS, LANES)))[:, 0:1]
    start = end - padded

    expert = lax.broadcasted_iota(jnp.int32, (N_EXPERTS, TM), 0).astype(F32)
    slot = lax.broadcasted_iota(jnp.int32, (K_PER_REGION, TM), 0)
    for i in range(n_tiles):
        cols = slice(i * TM, (i + 1) * TM)
        row_of = pos_scr[:, cols] + start
        ik = ik_ref[0, :, cols]
        acc = jnp.zeros((K_PER_REGION, TM), F32)
        for k in range(K_PER_REGION):
            pick = jnp.sum(jnp.where(expert == ik[k:k + 1, :], row_of, 0.0), axis=0, keepdims=True)
            acc = jnp.where(slot == k, pick, acc)
        dest_ref[0, :, cols] = acc.astype(jnp.int32)
    first_ref[0] = jnp.broadcast_to(start * (1.0 / MOE_BLK), (N_EXPERTS, LANES)).astype(jnp.int32)
    count_ref[0] = jnp.broadcast_to(padded * (1.0 / MOE_BLK), (N_EXPERTS, LANES)).astype(jnp.int32)
    in_last = counts - (padded - MOE_BLK)
    short = jnp.where((counts > 0.0) & (in_last <= MOE_BLK // 2), 1.0, 0.0)
    short_ref[0] = jnp.broadcast_to(short, (N_EXPERTS, LANES)).astype(jnp.int32)


def _route(chosen, ik):
    per_region = lambda rows, cols: pl.BlockSpec((1, rows, cols), lambda r: (r, 0, 0))
    table = jax.ShapeDtypeStruct((MOE_REGIONS, N_EXPERTS, LANES), jnp.int32)
    return pl.pallas_call(
        _route_body,
        grid=(MOE_REGIONS,),
        in_specs=[per_region(N_EXPERTS, T_ALL), per_region(K_PER_REGION, T_ALL)],
        out_specs=[per_region(K_PER_REGION, T_ALL)] + [per_region(N_EXPERTS, LANES)] * 3,
        out_shape=[jax.ShapeDtypeStruct((MOE_REGIONS, K_PER_REGION, T_ALL), jnp.int32), table, table, table],
        scratch_shapes=[pltpu.VMEM((N_EXPERTS, T_ALL), F32)],
        compiler_params=_cparams("arbitrary"),
        name="moe_route",
    )(chosen, ik.reshape(MOE_REGIONS, K_PER_REGION, T_ALL))


def _sc_worker_id():
    return lax.axis_index("s") * SC_CORES + lax.axis_index("c")


def _sc_dispatch(h2p, dest):
    n_chunks = T_ALL // DISP_CHUNK
    k_per = dest.shape[0] // DISP_SPLIT
    items_per_worker = n_chunks * DISP_SPLIT // SC_WORKERS
    chunk_stride = SC_WORKERS // DISP_SPLIT
    width = h2p.shape[1]
    mesh = plsc.VectorSubcoreMesh(core_axis_name="c", subcore_axis_name="s")

    @functools.partial(
        pl.kernel, mesh=mesh,
        out_type=jax.ShapeDtypeStruct((MOE_ROWS, width), jnp.int32),
        scratch_types=[pltpu.VMEM((k_per, DISP_CHUNK), jnp.int32), pltpu.VMEM((DISP_CHUNK, width), jnp.int32),
                       pltpu.SemaphoreType.DMA],
    )
    def run(x_hbm, dest_hbm, xs_hbm, idx_v, rows_v, sem):
        wid = _sc_worker_id()
        group = wid % DISP_SPLIT
        for i in range(items_per_worker):
            chunk = i * chunk_stride + wid // DISP_SPLIT
            tokens = pl.ds(pl.multiple_of(chunk * DISP_CHUNK, DISP_CHUNK), DISP_CHUNK)
            pltpu.sync_copy(dest_hbm.at[group, :, tokens], idx_v)
            pltpu.sync_copy(x_hbm.at[tokens], rows_v)
            scatters = [pltpu.make_async_copy(rows_v, xs_hbm.at[idx_v.at[k]], sem) for k in range(k_per)]
            for cp in scatters:
                cp.start()
            for cp in scatters:
                cp.wait()

    return run(h2p, dest.reshape(DISP_SPLIT, k_per, T_ALL))


def _sc_collect(y, dest_flat):
    n_k = dest_flat.shape[0] // T_ALL
    per_worker = T_ALL // SC_WORKERS
    n_chunks = per_worker // COLLECT_CHUNK
    n_steps = n_k * n_chunks
    width = y.shape[1]
    mesh = plsc.VectorSubcoreMesh(core_axis_name="c", subcore_axis_name="s")

    @functools.partial(
        pl.kernel, mesh=mesh,
        out_type=jax.ShapeDtypeStruct((n_k * T_ALL, width), y.dtype),
        scratch_types=[pltpu.VMEM((n_k * per_worker,), jnp.int32),
                       pltpu.VMEM((COLLECT_CHUNK, width), y.dtype), pltpu.VMEM((COLLECT_CHUNK, width), y.dtype),
                       pltpu.SemaphoreType.DMA, pltpu.SemaphoreType.DMA],
    )
    def run(y_hbm, dest_hbm, yg_hbm, idx_v, rows0, rows1, sem0, sem1):
        wid = _sc_worker_id()
        bufs = ((rows0, sem0), (rows1, sem1))
        for k in range(n_k):
            pltpu.sync_copy(dest_hbm.at[pl.ds(k * T_ALL + wid * per_worker, per_worker)],
                            idx_v.at[pl.ds(k * per_worker, per_worker)])

        def gather(step, buf):
            rows, sem = buf
            idx = idx_v.at[pl.ds(pl.multiple_of(step * COLLECT_CHUNK, 8), COLLECT_CHUNK)]
            return pltpu.make_async_copy(y_hbm.at[idx], rows, sem)

        def out_rows(step):
            off = (step // n_chunks) * T_ALL + wid * per_worker + (step % n_chunks) * COLLECT_CHUNK
            return yg_hbm.at[pl.ds(pl.multiple_of(off, 8), COLLECT_CHUNK)]

        gather(0, bufs[0]).start()

        @pl.loop(0, n_steps, step=2)
        def _(base):
            for j in range(2):
                step = base + j

                @pl.when(step + 1 < n_steps)
                def _():
                    gather(step + 1, bufs[1 - j]).start()

                gather(step, bufs[j]).wait()
                pltpu.sync_copy(bufs[j][0], out_rows(step))

    return run(y, dest_flat)


def _expert_body(first_ref, count_ref, short_ref, xs_hbm, wg_hbm, wu_hbm, wd_hbm, y_hbm,
                 wg_f32, wu_f32, wd_f32, wg_bf, wu_bf, wd_bf, x_buf, y_buf, w_sem, in_sem, out_sem, *, layer):
    e = pl.program_id(0)
    first = first_ref[e]
    count = count_ref[e]
    n_used = first_ref[N_EXPERTS - 1] + count_ref[N_EXPERTS - 1]
    half = D_MODEL // 2

    def weight_copies(ex):
        slot = lax.rem(ex, EXPERT_W_SLOTS)
        out = []
        for src, dst in ((wg_hbm, wg_f32), (wu_hbm, wu_f32), (wd_hbm, wd_f32)):
            size = dst.shape[1] // EXPERT_W_PARTS
            for part in range(EXPERT_W_PARTS):
                rows = pl.ds(part * size, size)
                out.append(pltpu.make_async_copy(src.at[layer, ex, rows], dst.at[slot, rows], w_sem.at[slot]))
        return out

    @pl.when(e == 0)
    def _():
        for ahead in range(EXPERT_W_SLOTS - 1):
            for cp in weight_copies(ahead):
                cp.start()

    for cp in weight_copies(e):
        cp.wait()

    @pl.when(e + EXPERT_W_SLOTS - 1 < N_EXPERTS)
    def _():
        for cp in weight_copies(e + EXPERT_W_SLOTS - 1):
            cp.start()

    w_slot = lax.rem(e, EXPERT_W_SLOTS)
    wg_bf[...] = wg_f32[w_slot].astype(BF16)
    wu_bf[...] = wu_f32[w_slot].astype(BF16)
    wd_bf[...] = wd_f32[w_slot].astype(BF16)

    def part_rows(g, part, n_parts):
        size = MOE_BLK // n_parts
        return pl.ds(pl.multiple_of(g * MOE_BLK + part * size, size), size), pl.ds(part * size, size)

    def in_copies(g):
        slot = g & (EXPERT_SLOTS - 1)
        out = []
        for part in range(EXPERT_IN_PARTS):
            src, dst = part_rows(g, part, EXPERT_IN_PARTS)
            out.append(pltpu.make_async_copy(xs_hbm.at[src], x_buf.at[slot, dst], in_sem.at[slot]))
        return out

    def out_copies(g):
        slot = g & (EXPERT_SLOTS - 1)
        out = []
        for part in range(EXPERT_OUT_PARTS):
            dst, src = part_rows(g, part, EXPERT_OUT_PARTS)
            out.append(pltpu.make_async_copy(y_buf.at[slot, src], y_hbm.at[dst], out_sem.at[slot]))
        return out

    @pl.when((first == 0) & (count > 0))
    def _():
        for ahead in range(EXPERT_SLOTS - 1):
            @pl.when(ahead < n_used)
            def _():
                for cp in in_copies(ahead):
                    cp.start()

    def block(b, carry):
        g = first + b
        slot = g & (EXPERT_SLOTS - 1)
        for cp in in_copies(g):
            cp.wait()

        @pl.when(g + EXPERT_SLOTS - 1 < n_used)
        def _():
            for cp in in_copies(g + EXPERT_SLOTS - 1):
                cp.start()

        @pl.when(g >= EXPERT_SLOTS)
        def _():
            for cp in out_copies(g - EXPERT_SLOTS):
                cp.wait()

        def ffn(n_rows):
            hi, lo = _unpack_bf16_pairs(x_buf[slot, 0:n_rows])

            def proj(w_bf):
                return (jnp.dot(hi, w_bf[0:half, :], preferred_element_type=F32)
                        + jnp.dot(lo, w_bf[half:, :], preferred_element_type=F32))

            hid = _silu(proj(wg_bf)) * proj(wu_bf)
            y_buf[slot, 0:n_rows] = _pack_bf16_pairs(
                jnp.dot(hid.astype(BF16), wd_bf[...], preferred_element_type=F32))

        short = (b == count - 1) & (short_ref[e] == 1)

        @pl.when(short)
        def _():
            ffn(MOE_BLK // 2)
            y_buf[slot, MOE_BLK // 2:MOE_BLK] = jnp.zeros((MOE_BLK // 2, D_MODEL // 2), jnp.int32)

        @pl.when(jnp.logical_not(short))
        def _():
            ffn(MOE_BLK)

        for cp in out_copies(g):
            cp.start()
        return carry

    lax.fori_loop(0, count, block, 0)

    @pl.when(e == N_EXPERTS - 1)
    def _():
        for back in range(EXPERT_SLOTS, 0, -1):
            @pl.when(n_used >= back)
            def _():
                for cp in out_copies(n_used - back):
                    cp.wait()


EXPERT_SLOTS = 4
EXPERT_W_SLOTS = 4
EXPERT_W_PARTS = 2
EXPERT_IN_PARTS = 2
EXPERT_OUT_PARTS = 4


def _experts(first_blk, n_blk, short_last, xs, layer, w_gate, w_up, w_down):
    anywhere = pl.BlockSpec(memory_space=pl.ANY)
    grid_spec = pltpu.PrefetchScalarGridSpec(
        num_scalar_prefetch=3,
        grid=(N_EXPERTS,),
        in_specs=[anywhere] * 4,
        out_specs=anywhere,
        scratch_shapes=[pltpu.VMEM((EXPERT_W_SLOTS, D_MODEL, D_EXPERT), F32),
                        pltpu.VMEM((EXPERT_W_SLOTS, D_MODEL, D_EXPERT), F32),
                        pltpu.VMEM((EXPERT_W_SLOTS, D_EXPERT, D_MODEL), F32),
                        pltpu.VMEM((D_MODEL, D_EXPERT), BF16), pltpu.VMEM((D_MODEL, D_EXPERT), BF16),
                        pltpu.VMEM((D_EXPERT, D_MODEL), BF16),
                        pltpu.VMEM((EXPERT_SLOTS, MOE_BLK, D_MODEL // 2), jnp.int32),
                        pltpu.VMEM((EXPERT_SLOTS, MOE_BLK, D_MODEL // 2), jnp.int32),
                        pltpu.SemaphoreType.DMA((EXPERT_W_SLOTS,)),
                        pltpu.SemaphoreType.DMA((EXPERT_SLOTS,)), pltpu.SemaphoreType.DMA((EXPERT_SLOTS,))],
    )
    return pl.pallas_call(
        functools.partial(_expert_body, layer=layer),
        grid_spec=grid_spec,
        out_shape=jax.ShapeDtypeStruct((MOE_ROWS, D_MODEL // 2), jnp.int32),
        compiler_params=_cparams("arbitrary"),
        name="moe_experts",
    )(first_blk, n_blk, short_last, xs, w_gate, w_up, w_down)


def _combine_body(x1_ref, h2_ref, *refs, final):
    yg_refs = refs[:MOE_REGIONS]
    gk_ref, mod_ref, sg_ref, su_ref, sd_ref, fn_ref, *o_refs = refs[MOE_REGIONS:]
    hi, lo = _unpack_bf16_pairs(h2_ref[...])
    half = D_MODEL // 2

    def proj(w_ref):
        return _dot(hi, w_ref[0:half, :]) + _dot(lo, w_ref[half:, :])

    shared = _dot(_silu(proj(sg_ref)) * proj(su_ref), sd_ref[...])
    acc_hi, acc_lo = shared[:, :half], shared[:, half:]
    gk = gk_ref[...]
    for k in range(TOP_K):
        y_hi, y_lo = _unpack_bf16_pairs(yg_refs[k // K_PER_REGION][k % K_PER_REGION])
        acc_hi = acc_hi + gk[:, k:k + 1] * y_hi.astype(F32)
        acc_lo = acc_lo + gk[:, k:k + 1] * y_lo.astype(F32)
    acc = jnp.concatenate([acc_hi, acc_lo], axis=1)
    m = mod_ref[0]
    y = x1_ref[...] + m[:, 5 * D_MODEL:6 * D_MODEL] * acc
    if not final:
        o_refs[0][...] = y
        return
    y = _rms(y, fn_ref[...])
    is_ctx = pl.program_id(0) < N_CTX_TILES

    @pl.when(is_ctx)
    def _():
        o_refs[0][...] = y

    @pl.when(jnp.logical_not(is_ctx))
    def _():
        o_refs[1][...] = y


def _combine(x1, h2p, yg, gk, mod_l, ws_gate, ws_up, ws_down, final_norm, final):
    tok = lambda shape: pl.BlockSpec(shape, lambda i: (i, 0))
    full = lambda shape: pl.BlockSpec(shape, lambda i: (0, 0))
    if final:
        out_specs, _ = _token_specs((None, None), D_MODEL)
        out_shape = [jax.ShapeDtypeStruct((T_CTX, D_MODEL), F32), jax.ShapeDtypeStruct((T_LAT, D_MODEL), F32)]
    else:
        out_specs = tok((TM, D_MODEL))
        out_shape = jax.ShapeDtypeStruct((T_ALL, D_MODEL), F32)
    return pl.pallas_call(
        functools.partial(_combine_body, final=final),
        grid=(T_ALL // TM,),
        in_specs=[tok((TM, D_MODEL)), tok((TM, D_MODEL // 2))]
                 + [pl.BlockSpec((K_PER_REGION, TM, D_MODEL // 2), lambda i: (0, i, 0))] * MOE_REGIONS
                 + [tok((TM, TOP_K)),
                  pl.BlockSpec((1, 1, N_MOD * D_MODEL), lambda i: (_mod_row(i), 0, 0)),
                  full((D_MODEL, D_EXPERT)), full((D_MODEL, D_EXPERT)), full((D_EXPERT, D_MODEL)),
                  full((1, D_MODEL))],
        out_specs=out_specs,
        out_shape=out_shape,
        compiler_params=_cparams("arbitrary"),
        name="moe_combine",
    )(x1, h2p, *yg, gk, mod_l, ws_gate, ws_up, ws_down, final_norm.reshape(1, D_MODEL))


def _moe(x1, h2p, chosen, gk, ik, mod_l, layer, w_gate, w_up, w_down, ws_gate, ws_up, ws_down, final_norm, final):
    dest, first_blk, n_blk, short_last = _route(chosen, ik)
    yg = []
    for r in range(MOE_REGIONS):
        xs = _sc_dispatch(h2p, dest[r])
        y = _experts(first_blk[r, :, 0], n_blk[r, :, 0], short_last[r, :, 0], xs, layer, w_gate, w_up, w_down)
        yg.append(_sc_collect(y, dest[r].reshape(-1)).reshape(K_PER_REGION, T_ALL, D_MODEL // 2))
    return _combine(x1, h2p, yg, gk, mod_l, ws_gate, ws_up, ws_down, final_norm, final)


def kernel(x_prompt, x_sample, cache_a_k, cache_a_v, cache_b_k, cache_b_v, state_d_fwd, state_d_bwd, c, c_ctx, w_ada, b_ada, norm_mix, norm_ffn, w_in_attn, w_out_attn, sink_a, rpb_b, w_in_rec, w_out_rec, conv_w, conv_b, filt_w1, filt_b1, filt_w2, filt_b2, filt_w3, filt_b3, filt_freq, filt_w4, d_skip, lb_fwd, lb_bwd, norm_d, w_router, router_bias, w_gate, w_up, w_down, ws_gate, ws_up, ws_down, final_norm):
    x = (x_prompt.reshape(T_CTX, D_MODEL), x_sample.reshape(T_LAT, D_MODEL))
    cvec = jnp.concatenate([c_ctx[None, :], c], axis=0)
    c_lanes = jnp.broadcast_to(cvec[:, :, None], (N_CVEC, D_MODEL, LANES))
    mod = [_ada(c_lanes, l, w_ada, b_ada).reshape(CVEC_PAD, 1, N_MOD * D_MODEL) for l in range(DEPTH)]

    new_kv = None
    new_state = None
    for l in range(DEPTH):
        j = l // 2
        final = l == DEPTH - 1
        if l % 2 == 0:
            qkv = _inproj(x, mod[l], norm_mix[l], w_in_attn[j])
            oa_ctx, ob_ctx, *new_kv = _ctx_attn(qkv, sink_a[j])
            new_kv = tuple(new_kv)
            q_rot, k_rot = _rope(qkv)
            cache = lambda t: t[:, j].reshape(DEC_BATCH, PAST_LEN, -1)
            oa_lat = _win_attn(qkv, q_rot, k_rot, cache(cache_a_k), cache(cache_a_v), sink_a[j])
            ob_lat = _na_attn(qkv, cache(cache_b_k), cache(cache_b_v), _na_rel_rows(rpb_b[j]))
            mix_a = (oa_ctx, oa_lat)
            mix_b = (ob_ctx, ob_lat)
            w_out = w_out_attn[j]
        else:
            u = _inproj(x, mod[l], norm_mix[l], w_in_rec[j])
            filt = (filt_w1[j], filt_b1[j], filt_w2[j], filt_b2[j], filt_w3[j], filt_b3[j], filt_freq[j],
                    filt_w4[j])
            y_ctx = _hyena(u, 0, BATCH, SEQ, conv_w[j], conv_b[j], d_skip[j], _hyena_filter(SEQ, filt))
            y_lat = _hyena(u, T_CTX // DEC_SEQ, DEC_BATCH, DEC_SEQ, conv_w[j], conv_b[j], d_skip[j],
                           _hyena_filter(DEC_SEQ, filt))
            zeros = jnp.zeros((BATCH, D_HEADS, D_KDIM, D_VDIM), F32)
            o_ctx, s_f, s_b = _hgrn(u, 0, BATCH, SEQ, lb_fwd, lb_bwd, norm_d[j], zeros, zeros, l)
            o_lat, _, _ = _hgrn(u, T_CTX // DEC_SEQ, DEC_BATCH, DEC_SEQ, lb_fwd, lb_bwd, norm_d[j],
                                state_d_fwd[:, j], state_d_bwd[:, j], l)
            new_state = (s_f[:, None], s_b[:, None])
            mix_a = (y_ctx, y_lat)
            mix_b = (o_ctx, o_lat)
            w_out = w_out_rec[j]
        x1, h2p, chosen, gk, ik = _outproj(mix_a, mix_b, x, mod[l], norm_ffn[l], w_out, w_router[l],
                                           router_bias[l])
        x = _moe(x1, h2p, chosen, gk, ik, mod[l], l, w_gate, w_up, w_down, ws_gate[l], ws_up[l],
                 ws_down[l], final_norm, final)

    y_prompt = x[0].reshape(BATCH, SEQ, D_MODEL)
    y_sample = x[1].reshape(DEC_BATCH, DEC_SEQ, D_MODEL)
    return (y_prompt, y_sample) + new_kv + new_state
```

```python
import functools
import math

import numpy as np
import jax
import jax.numpy as jnp
from jax import lax
from jax.experimental import pallas as pl
from jax.experimental.pallas import tpu as pltpu
from jax.experimental.pallas import tpu_sc as plsc

F32 = jnp.float32
BF16 = jnp.bfloat16
HI = lax.Precision.HIGHEST

D_MODEL = 1024
BATCH = 16
SEQ = 256
DEPTH = 2
DEC_BATCH = 2
DEC_SEQ = 1024
PAST_LEN = 512
GRID_W = 64
HEAD_DIM = 64
N_MOD = 6
RMS_EPS = 1e-6
A_HEADS = 8
A_KV_HEADS = 2
A_GROUP = A_HEADS // A_KV_HEADS
WINDOW = 128
ROPE_BASE = 10000.0
B_HEADS = 8
NA_ROWS = 8
NA_COLS = 16
C_DIM = 512
C_EMB = 33
C_FFN = 64
HYENA_MIN_DECAY = math.log(1e-2) / 1.5
HYENA_MAX_DECAY = math.log(1e-2) / 0.3
D_KDIM = 128
D_VDIM = 128
D_HEADS = 4
N_EXPERTS = 64
TOP_K = 8
D_EXPERT = 256
ROUTE_SCALE = 2.5
A_Q = A_HEADS * HEAD_DIM
A_KV = A_KV_HEADS * HEAD_DIM
B_W = B_HEADS * HEAD_DIM
ATTN_IN = A_Q + 2 * A_KV + 3 * B_W

T_CTX = BATCH * SEQ
T_LAT = DEC_BATCH * DEC_SEQ
T_ALL = T_CTX + T_LAT
N_CVEC = 1 + DEC_BATCH
CVEC_PAD = 8
TM = 512
MASK_NEG = -1e30
GLA_CHUNK = 64
GLA_SPAN = 256
GLA_MIN_SPANS = 2
HGRN_HEADS_PER_STEP = 4
DFT_CHUNK = 512
MOE_BLK = 512
MOE_REGIONS = 1
K_PER_REGION = TOP_K // MOE_REGIONS
MOE_NBLK = -(-(T_ALL * K_PER_REGION + N_EXPERTS * (MOE_BLK - 1)) // MOE_BLK)
MOE_ROWS = MOE_NBLK * MOE_BLK
SC_CORES = 2
SC_SUBCORES = 16
SC_WORKERS = SC_CORES * SC_SUBCORES
DISP_CHUNK = 128
DISP_SPLIT = 2
COLLECT_CHUNK = 64
VMEM_LIMIT = 56 * 1024 * 1024


def _cparams(*sem):
    return pltpu.CompilerParams(dimension_semantics=sem, vmem_limit_bytes=VMEM_LIMIT)


def _mod_row(i):
    return jnp.where(i < T_CTX // TM, 0, 1 + (i - T_CTX // TM) // (DEC_SEQ // TM))


def _dot(a, b):
    return jnp.dot(a.astype(BF16), b.astype(BF16), preferred_element_type=F32)


def _dot_nt(a, b):
    return lax.dot_general(a.astype(BF16), b.astype(BF16), (((1,), (1,)), ((), ())),
                           preferred_element_type=F32)


def _dot_tn(a, b):
    return lax.dot_general(a.astype(BF16), b.astype(BF16), (((0,), (0,)), ((), ())),
                           preferred_element_type=F32)


def _dot_hi(a, b):
    return jnp.dot(a, b, precision=HI, preferred_element_type=F32)


def _split_bf16(x):
    hi = x.astype(BF16)
    return hi, (x - hi.astype(F32)).astype(BF16)


def _dot_split(a, b):
    a_hi, a_lo = _split_bf16(a)
    b_hi, b_lo = _split_bf16(b)
    dot = lambda x, y: jnp.dot(x, y, preferred_element_type=F32)
    return dot(a_hi, b_hi) + dot(a_hi, b_lo) + dot(a_lo, b_hi)


def _silu(x):
    return x * jax.nn.sigmoid(x)


def _rms(x, g):
    return x * lax.rsqrt(jnp.mean(x * x, axis=-1, keepdims=True) + RMS_EPS) * g


ADA_TN = 1536
ADA_UNROLL = 4


def _ada_body(cb_ref, w_ref, b_ref, o_ref):
    tn = o_ref.shape[-1]
    n_slab = tn // LANES

    def step(k8, accs):
        r0 = pl.multiple_of(k8 * 8, 8)
        sk = [_silu(cb_ref[j, pl.ds(r0, 8), :]) for j in range(N_CVEC)]
        out = []
        for s in range(n_slab):
            wk = w_ref[0, pl.ds(r0, 8), s * LANES:(s + 1) * LANES]
            out.extend(accs[s * N_CVEC + j] + wk * sk[j] for j in range(N_CVEC))
        return tuple(out)

    accs = lax.fori_loop(0, D_MODEL // 8, step,
                         tuple(jnp.zeros((8, LANES), F32) for _ in range(n_slab * N_CVEC)), unroll=ADA_UNROLL)
    o_ref[0] = jnp.zeros((CVEC_PAD, tn), F32)
    for s in range(n_slab):
        for j in range(N_CVEC):
            o_ref[0, j:j + 1, s * LANES:(s + 1) * LANES] = (
                jnp.sum(accs[s * N_CVEC + j], axis=0, keepdims=True) + b_ref[0, :, s * LANES:(s + 1) * LANES])


def _ada(c_lanes, layer, w_ada, b_ada):
    n_out = N_MOD * D_MODEL
    return pl.pallas_call(
        _ada_body,
        grid=(n_out // ADA_TN,),
        in_specs=[pl.BlockSpec((N_CVEC, D_MODEL, LANES), lambda n: (0, 0, 0)),
                  pl.BlockSpec((1, D_MODEL, ADA_TN), lambda n: (layer, 0, n)),
                  pl.BlockSpec((1, 1, ADA_TN), lambda n: (layer, 0, n))],
        out_specs=pl.BlockSpec((1, CVEC_PAD, ADA_TN), lambda n: (0, 0, n)),
        out_shape=jax.ShapeDtypeStruct((1, CVEC_PAD, n_out), F32),
        compiler_params=_cparams("parallel"),
        name="ada",
    )(c_lanes, w_ada, b_ada.reshape(DEPTH, 1, n_out))


N_CTX_TILES = T_CTX // TM


def _token_specs(x, width):
    if not isinstance(x, tuple):
        return [pl.BlockSpec((TM, width), lambda i: (i, 0))], (x,)
    return ([pl.BlockSpec((TM, width), lambda i: (jnp.minimum(i, N_CTX_TILES - 1), 0)),
             pl.BlockSpec((TM, width), lambda i: (jnp.maximum(i - N_CTX_TILES, 0), 0))], x)


def _token_tile(refs):
    if len(refs) == 1:
        return refs[0][...]
    return jnp.where(pl.program_id(0) < N_CTX_TILES, refs[0][...], refs[1][...])


def _inproj_body(*refs, n_x):
    x_refs, (mod_ref, g_ref, w_ref, o_ref, w_bf) = refs[:n_x], refs[n_x:]

    @pl.when(pl.program_id(0) == 0)
    def _():
        w_bf[...] = w_ref[...].astype(BF16)

    m = mod_ref[0]
    h = _rms(_token_tile(x_refs), g_ref[...]) * (1.0 + m[:, D_MODEL:2 * D_MODEL]) + m[:, 0:D_MODEL]
    o_ref[...] = _dot(h, w_bf[...])


def _inproj(x, mod_l, gain, w):
    n = w.shape[1]
    x_specs, x_args = _token_specs(x, D_MODEL)
    return pl.pallas_call(
        functools.partial(_inproj_body, n_x=len(x_args)),
        grid=(T_ALL // TM,),
        in_specs=x_specs + [pl.BlockSpec((1, 1, N_MOD * D_MODEL), lambda i: (_mod_row(i), 0, 0)),
                            pl.BlockSpec((1, D_MODEL), lambda i: (0, 0)),
                            pl.BlockSpec((D_MODEL, n), lambda i: (0, 0), pipeline_mode=pl.Buffered(1))],
        out_specs=pl.BlockSpec((TM, n), lambda i: (i, 0)),
        out_shape=jax.ShapeDtypeStruct((T_ALL, n), F32),
        scratch_shapes=[pltpu.VMEM((D_MODEL, n), BF16)],
        compiler_params=_cparams("arbitrary"),
        name="inproj",
    )(*x_args, mod_l, gain.reshape(1, D_MODEL), w)


def _ctx_attn_body(qkv_ref, sink_ref, oa_ref, ob_ref, ak_ref, av_ref, bk_ref, bv_ref):
    scale = HEAD_DIM ** -0.5
    lane = lax.broadcasted_iota(jnp.int32, (SEQ, LANES), 1)
    in_half = [lane < HEAD_DIM, lane >= HEAD_DIM]

    def attend(q, k, v, sink):
        s = _dot_nt(q, k) * scale
        m = jnp.max(s, axis=-1, keepdims=True)
        if sink is not None:
            m = jnp.maximum(m, sink)
        p = jnp.exp(s - m)
        den = jnp.sum(p, axis=-1, keepdims=True)
        if sink is not None:
            den = den + jnp.exp(sink - m)
        return _dot(p, v) / den

    def tile(first_col, t):
        return qkv_ref[:, first_col + t * LANES:first_col + (t + 1) * LANES]

    base = A_Q + 2 * A_KV
    for hk in range(A_KV_HEADS):
        dst = pl.ds(hk, SEQ, stride=A_KV_HEADS)
        ak_ref[0, dst, :] = qkv_ref[:, A_Q + hk * HEAD_DIM:A_Q + (hk + 1) * HEAD_DIM]
        av_ref[0, dst, :] = qkv_ref[:, A_Q + A_KV + hk * HEAD_DIM:A_Q + A_KV + (hk + 1) * HEAD_DIM]
    for h in range(B_HEADS):
        dst = pl.ds(h, SEQ, stride=B_HEADS)
        bk_ref[0, dst, :] = qkv_ref[:, base + B_W + h * HEAD_DIM:base + B_W + (h + 1) * HEAD_DIM]
        bv_ref[0, dst, :] = qkv_ref[:, base + 2 * B_W + h * HEAD_DIM:base + 2 * B_W + (h + 1) * HEAD_DIM]

    k_t, v_t = tile(A_Q, 0), tile(A_Q + A_KV, 0)
    k_sw, v_sw = pltpu.roll(k_t, HEAD_DIM, axis=1), pltpu.roll(v_t, HEAD_DIM, axis=1)
    tiles_per_kv = A_GROUP // HEADS_PER_TILE
    for hk in range(A_KV_HEADS):
        q_tiles = [tile(0, hk * tiles_per_kv + j) for j in range(tiles_per_kv)]
        halves = []
        for p in range(HEADS_PER_TILE):
            q = jnp.concatenate([jnp.where(in_half[p], qt, 0.0) for qt in q_tiles], axis=0)
            heads = [(hk * tiles_per_kv + j) * HEADS_PER_TILE + p for j in range(tiles_per_kv)]
            sink = jnp.concatenate([jnp.broadcast_to(sink_ref[:, h:h + 1], (SEQ, 1)) for h in heads], axis=0)
            halves.append(attend(q, k_t if p == hk else k_sw, v_t if p == hk else v_sw, sink))
        first_half = lax.broadcasted_iota(jnp.int32, halves[0].shape, 1) < HEAD_DIM
        o = jnp.where(first_half, halves[0], halves[1])
        for j in range(tiles_per_kv):
            t = hk * tiles_per_kv + j
            oa_ref[:, t * LANES:(t + 1) * LANES] = o[j * SEQ:(j + 1) * SEQ]

    for t in range(B_HEADS // HEADS_PER_TILE):
        q_t, k_b, v_b = tile(base, t), tile(base + B_W, t), tile(base + 2 * B_W, t)
        halves = [attend(jnp.where(in_half[p], q_t, 0.0), k_b, v_b, None) for p in range(HEADS_PER_TILE)]
        ob_ref[:, t * LANES:(t + 1) * LANES] = jnp.where(in_half[0], halves[0], halves[1])


def _ctx_attn(qkv, sink):
    kv_spec = lambda heads: pl.BlockSpec((1, SEQ * heads, HEAD_DIM), lambda b: (b, 0, 0))
    kv_sd = lambda heads: jax.ShapeDtypeStruct((BATCH, SEQ * heads, HEAD_DIM), F32)
    outs = pl.pallas_call(
        _ctx_attn_body,
        grid=(BATCH,),
        in_specs=[pl.BlockSpec((SEQ, ATTN_IN), lambda b: (b, 0)),
                  pl.BlockSpec((1, A_HEADS), lambda b: (0, 0))],
        out_specs=[pl.BlockSpec((SEQ, A_Q), lambda b: (b, 0)), pl.BlockSpec((SEQ, B_W), lambda b: (b, 0)),
                   kv_spec(A_KV_HEADS), kv_spec(A_KV_HEADS), kv_spec(B_HEADS), kv_spec(B_HEADS)],
        out_shape=[jax.ShapeDtypeStruct((T_CTX, A_Q), F32), jax.ShapeDtypeStruct((T_CTX, B_W), F32),
                   kv_sd(A_KV_HEADS), kv_sd(A_KV_HEADS), kv_sd(B_HEADS), kv_sd(B_HEADS)],
        compiler_params=_cparams("parallel"),
        name="ctx_attn",
    )(qkv, sink.reshape(1, A_HEADS))
    caches = [t.reshape(BATCH, 1, SEQ, -1, HEAD_DIM) for t in outs[2:]]
    return outs[0], outs[1], *caches


@functools.lru_cache(maxsize=None)
def _rope_tables(width):
    half = HEAD_DIM // 2
    t = np.arange(DEC_SEQ)
    inv = ROPE_BASE ** (-np.arange(0, half, 2, dtype=np.float64) / half)
    ang_r = (t // GRID_W)[:, None] * inv[None, :]
    ang_c = (t % GRID_W)[:, None] * inv[None, :]
    cos = np.concatenate([np.cos(ang_r)] * 2 + [np.cos(ang_c)] * 2, axis=-1)
    sin = np.concatenate([-np.sin(ang_r), np.sin(ang_r), -np.sin(ang_c), np.sin(ang_c)], axis=-1)
    reps = width // HEAD_DIM
    return (np.tile(cos, (1, reps)).astype(np.float32), np.tile(sin, (1, reps)).astype(np.float32))


def _rope_body(q_ref, k_ref, cq_ref, sq_ref, ck_ref, sk_ref, qo_ref, ko_ref):
    quarter = HEAD_DIM // 4

    def rot(x, cos, sin):
        w = x.shape[-1]
        lane = lax.broadcasted_iota(jnp.int32, x.shape, 1)
        fwd = pltpu.roll(x, w - quarter, axis=1)
        bwd = pltpu.roll(x, quarter, axis=1)
        partner = jnp.where((lane & (2 * quarter - 1)) < quarter, fwd, bwd)
        return x * cos + partner * sin

    qo_ref[...] = rot(q_ref[...], cq_ref[...], sq_ref[...])
    ko_ref[...] = rot(k_ref[...], ck_ref[...], sk_ref[...])


def _rope(qkv):
    cq, sq = _rope_tables(A_Q)
    ck, sk = _rope_tables(A_KV)
    tab = lambda w: pl.BlockSpec((DEC_SEQ, w), lambda b: (0, 0))
    row0 = T_CTX // DEC_SEQ
    return pl.pallas_call(
        _rope_body,
        grid=(DEC_BATCH,),
        in_specs=[pl.BlockSpec((DEC_SEQ, A_Q), lambda b: (row0 + b, 0)),
                  pl.BlockSpec((DEC_SEQ, A_KV), lambda b: (row0 + b, A_Q // A_KV)),
                  tab(A_Q), tab(A_Q), tab(A_KV), tab(A_KV)],
        out_specs=[pl.BlockSpec((DEC_SEQ, A_Q), lambda b: (b, 0)),
                   pl.BlockSpec((DEC_SEQ, A_KV), lambda b: (b, 0))],
        out_shape=[jax.ShapeDtypeStruct((T_LAT, A_Q), F32), jax.ShapeDtypeStruct((T_LAT, A_KV), F32)],
        compiler_params=_cparams("parallel"),
        name="rope",
    )(qkv, qkv, jnp.asarray(cq), jnp.asarray(sq), jnp.asarray(ck), jnp.asarray(sk))


WIN_QB = 128


def _win_attn_body(qraw_ref, qrot_ref, krot_ref, v_ref, kc_ref, vc_ref, sink_ref, o_ref):
    scale = HEAD_DIM ** -0.5
    hk = pl.program_id(1)
    tiles = A_GROUP // HEADS_PER_TILE

    def kv_in_half(x):
        swapped = pltpu.roll(x, HEAD_DIM, axis=1)
        return [jnp.where(hk == p, x, swapped) for p in range(HEADS_PER_TILE)]

    k, v, kc, vc = kv_in_half(krot_ref[...]), kv_in_half(v_ref[...]), kv_in_half(kc_ref[0]), kv_in_half(vc_ref[0])
    head_lane = lax.broadcasted_iota(jnp.int32, (1, A_HEADS), 1)

    def sink_rows(p):
        heads = [hk * A_GROUP + j * HEADS_PER_TILE + p for j in range(tiles)]
        vals = [jnp.sum(jnp.where(head_lane == h, sink_ref[...], 0.0), axis=-1, keepdims=True) for h in heads]
        return jnp.concatenate([jnp.broadcast_to(s, (WIN_QB, 1)) for s in vals], axis=0)

    sinks = [sink_rows(p) for p in range(HEADS_PER_TILE)]
    lane = lax.broadcasted_iota(jnp.int32, (tiles * WIN_QB, LANES), 1)
    in_half = [lane < HEAD_DIM, lane >= HEAD_DIM]
    for qb in range(DEC_SEQ // WIN_QB):
        q0 = qb * WIN_QB
        rows = slice(q0, q0 + WIN_QB)
        lo = max(0, q0 - WINDOW)
        hi = min(DEC_SEQ, q0 + WIN_QB + WINDOW)
        q_rot = jnp.concatenate([qrot_ref[rows, j * LANES:(j + 1) * LANES] for j in range(tiles)], axis=0)
        q_raw = jnp.concatenate([qraw_ref[rows, j * LANES:(j + 1) * LANES] for j in range(tiles)], axis=0)
        halves = []
        for p in range(HEADS_PER_TILE):
            s_loc = _dot_nt(jnp.where(in_half[p], q_rot, 0.0), k[p][lo:hi]) * scale
            qpos = q0 + (lax.broadcasted_iota(jnp.int32, s_loc.shape, 0) & (WIN_QB - 1))
            kpos = lo + lax.broadcasted_iota(jnp.int32, s_loc.shape, 1)
            s_loc = jnp.where(jnp.abs(kpos - qpos) <= WINDOW, s_loc, MASK_NEG)
            s_ctx = _dot_nt(jnp.where(in_half[p], q_raw, 0.0), kc[p]) * scale
            m = jnp.maximum(jnp.maximum(jnp.max(s_loc, axis=-1, keepdims=True),
                                        jnp.max(s_ctx, axis=-1, keepdims=True)), sinks[p])
            p_loc = jnp.exp(s_loc - m)
            p_ctx = jnp.exp(s_ctx - m)
            den = (jnp.sum(p_loc, axis=-1, keepdims=True) + jnp.sum(p_ctx, axis=-1, keepdims=True)
                   + jnp.exp(sinks[p] - m))
            halves.append((_dot(p_ctx, vc[p]) + _dot(p_loc, v[p][lo:hi])) / den)
        o = jnp.where(in_half[0], halves[0], halves[1])
        for j in range(tiles):
            o_ref[rows, j * LANES:(j + 1) * LANES] = o[j * WIN_QB:(j + 1) * WIN_QB]


def _win_attn(qkv, q_rot, k_rot, kc, vc, sink):
    row0 = T_CTX // DEC_SEQ
    gw = A_GROUP * HEAD_DIM
    return pl.pallas_call(
        _win_attn_body,
        grid=(DEC_BATCH, A_KV_HEADS),
        in_specs=[pl.BlockSpec((DEC_SEQ, gw), lambda b, h: (row0 + b, h)),
                  pl.BlockSpec((DEC_SEQ, gw), lambda b, h: (b, h)),
                  pl.BlockSpec((DEC_SEQ, A_KV), lambda b, h: (b, 0)),
                  pl.BlockSpec((DEC_SEQ, A_KV), lambda b, h: (row0 + b, (A_Q + A_KV) // A_KV)),
                  pl.BlockSpec((1, PAST_LEN, A_KV), lambda b, h: (b, 0, 0)),
                  pl.BlockSpec((1, PAST_LEN, A_KV), lambda b, h: (b, 0, 0)),
                  pl.BlockSpec((1, A_HEADS), lambda b, h: (0, 0))],
        out_specs=pl.BlockSpec((DEC_SEQ, gw), lambda b, h: (b, h)),
        out_shape=jax.ShapeDtypeStruct((T_LAT, A_Q), F32),
        compiler_params=_cparams("parallel", "parallel"),
        name="win_attn",
    )(qkv, q_rot, k_rot, qkv, kc, vc, sink.reshape(1, A_HEADS))


GRID_ROWS = DEC_SEQ // GRID_W
NA_BAND = min(NA_ROWS, GRID_ROWS)


NA_REL_ROWS = 2 * NA_ROWS - 1
NA_REL_COLS = 2 * NA_COLS - 1
LANES = 128
HEADS_PER_TILE = LANES // HEAD_DIM


def _na_rel_rows(rpb):
    pad = jnp.zeros((B_HEADS, NA_REL_ROWS, GRID_W - NA_REL_COLS), F32)
    one = jnp.concatenate([rpb, pad], axis=-1)
    nxt = jnp.concatenate([one[:, 1:], jnp.zeros((B_HEADS, 1, GRID_W), F32)], axis=1)
    both = jnp.concatenate([one, nxt], axis=-1)
    return jnp.concatenate([both, jnp.zeros((B_HEADS, 16 - NA_REL_ROWS, LANES), F32)], axis=1)


NA_HEADS_PER_STEP = LANES // HEAD_DIM


def _na_row_groups():
    groups = []
    for r in range(GRID_ROWS):
        rs = min(max(r - NA_ROWS // 2, 0), GRID_ROWS - NA_BAND)
        if groups and groups[-1][2] == rs:
            groups[-1][1] += 1
        else:
            groups.append([r, 1, rs])
    return groups


def _na_attn_body(q_ref, k_ref, v_ref, kc_ref, vc_ref, rel_ref, o_ref):
    scale = HEAD_DIM ** -0.5
    cq = lax.broadcasted_iota(jnp.int32, (GRID_W, LANES), 0)
    kcol = lax.broadcasted_iota(jnp.int32, (GRID_W, LANES), 1) & (GRID_W - 1)
    cs = jnp.clip(cq - NA_COLS // 2, 0, GRID_W - NA_COLS)
    col_ok = (kcol >= cs) & (kcol < cs + NA_COLS)
    kc = kc_ref[0]
    vc = vc_ref[0]
    tiles = {}

    def pair_tile(hh, a):
        if (hh, a) not in tiles:
            x = jnp.broadcast_to(rel_ref[hh, a:a + 1, :], (GRID_W, LANES))
            t = pltpu.roll(x, LANES - (NA_COLS - 1), axis=1, stride=1, stride_axis=0)
            tiles[hh, a] = jnp.where(col_ok, t, MASK_NEG)
        return tiles[hh, a]

    for r0, n_r, rs in _na_row_groups():
        rows = slice(r0 * GRID_W, (r0 + n_r) * GRID_W)
        band = slice(rs * GRID_W, (rs + NA_BAND) * GRID_W)
        q_t, k_t, v_t = q_ref[rows, :], k_ref[band, :], v_ref[band, :]
        head_of_lane = lax.broadcasted_iota(jnp.int32, q_t.shape, 1) >> (HEAD_DIM.bit_length() - 1)
        o = jnp.zeros(q_t.shape, F32)
        for hh in range(NA_HEADS_PER_STEP):
            bias = jnp.concatenate(
                [jnp.concatenate([pair_tile(hh, rs - r + NA_ROWS - 1 + 2 * i) for i in range(NA_BAND // 2)], axis=1)
                 for r in range(r0, r0 + n_r)], axis=0)
            q = jnp.where(head_of_lane == hh, q_t, 0.0)
            s_loc = _dot_nt(q, k_t) * scale + bias
            s_ctx = _dot_nt(q, kc) * scale
            m = jnp.maximum(jnp.max(s_loc, axis=-1, keepdims=True), jnp.max(s_ctx, axis=-1, keepdims=True))
            p_loc = jnp.exp(s_loc - m)
            p_ctx = jnp.exp(s_ctx - m)
            den = jnp.sum(p_loc, axis=-1, keepdims=True) + jnp.sum(p_ctx, axis=-1, keepdims=True)
            o = jnp.where(head_of_lane == hh, (_dot(p_ctx, vc) + _dot(p_loc, v_t)) / den, o)
        o_ref[rows, :] = o


def _na_attn(qkv, kc, vc, rel):
    row0 = T_CTX // DEC_SEQ
    col0 = (A_Q + 2 * A_KV) // LANES
    n_blk = B_W // LANES
    col = lambda j: pl.BlockSpec((DEC_SEQ, LANES), lambda b, p: (row0 + b, col0 + j * n_blk + p))
    cache = pl.BlockSpec((1, PAST_LEN, LANES), lambda b, p: (b, 0, p))
    return pl.pallas_call(
        _na_attn_body,
        grid=(DEC_BATCH, n_blk),
        in_specs=[col(0), col(1), col(2), cache, cache,
                  pl.BlockSpec((NA_HEADS_PER_STEP, 16, LANES), lambda b, p: (p, 0, 0))],
        out_specs=pl.BlockSpec((DEC_SEQ, LANES), lambda b, p: (b, p)),
        out_shape=jax.ShapeDtypeStruct((T_LAT, B_W), F32),
        compiler_params=_cparams("parallel", "parallel"),
        name="na_attn",
    )(qkv, qkv, qkv, kc, vc, rel)


@functools.lru_cache(maxsize=None)
def _dft_mats(L):
    n = 2 * L
    fc = min(L, DFT_CHUNK)
    f = np.arange(L)[:, None]
    t = np.arange(L)[None, :]
    ang = 2.0 * np.pi * ((f * t) % n) / n
    m1 = np.cos(ang)
    m2 = np.sin(ang)
    m2[0, :] = np.where(np.arange(L) % 2 == 0, 1.0, -1.0)
    wgt = np.full((L, 1), 2.0)
    wgt[0, 0] = 1.0
    nch = L // fc
    fwd = np.concatenate([m1.reshape(nch, fc, L), m2.reshape(nch, fc, L)], axis=1)
    inv = np.concatenate([(m1 * wgt / n).reshape(nch, fc, L), (m2 * wgt / n).reshape(nch, fc, L)], axis=1)
    inv = np.transpose(inv, (0, 2, 1))
    return fwd.astype(np.float32), inv.astype(np.float32)


@functools.lru_cache(maxsize=None)
def _filter_consts(L):
    t = np.linspace(0.0, 1.0, L)[:, None]
    bands = (C_EMB - 1) // 2
    ang = (2.0 * math.pi / L) * np.arange(L)[:, None] * np.linspace(1e-4, bands - 1, bands)[None, :]
    z = np.concatenate([t, np.cos(ang), -np.sin(ang)], axis=-1)
    zpad = np.zeros((L, 128))
    zpad[:, :C_EMB] = z
    deltas = np.abs(np.linspace(HYENA_MIN_DECAY, HYENA_MAX_DECAY, C_DIM))
    window = np.exp(-t * deltas[None, :])
    return zpad.astype(np.float32), window.astype(np.float32)


def _filter_body(z_ref, w1_ref, b1_ref, w2_ref, b2_ref, w3_ref, b3_ref, fr_ref, w4_ref, win_ref, fm_ref,
                 hr_ref, g_ref, hq_ref, hs_scr, hd_scr):
    c = pl.program_id(0)
    fc = hr_ref.shape[0]

    @pl.when(c == 0)
    def _():
        fr = fr_ref[...]
        hh = jnp.sin(fr * (_dot_hi(z_ref[...], w1_ref[...]) + b1_ref[...]))
        hh = jnp.sin(fr * (_dot_hi(hh, w2_ref[...]) + b2_ref[...]))
        hh = jnp.sin(fr * (_dot_hi(hh, w3_ref[...]) + b3_ref[...]))
        hh = _dot_hi(hh, w4_ref[...])
        hf = hh[:, :C_DIM] * win_ref[...]
        hb = hh[:, C_DIM:] * win_ref[...]
        hs_scr[...] = hf + hb
        hd_scr[...] = hf - hb

    fm = fm_ref[0]
    hr = _dot_split(fm[:fc], hs_scr[...])
    first = (lax.broadcasted_iota(jnp.int32, (fc, C_DIM), 0) == 0) & (c == 0)
    hr_ref[...] = hr
    g_ref[...] = jnp.where(first, 0.0, _dot_split(fm[fc:], hd_scr[...]))
    hs = hs_scr[...]
    sign = jnp.where((lax.broadcasted_iota(jnp.int32, hs.shape, 0) & 1) == 0, 1.0, -1.0)
    hq_ref[...] = jnp.where(first, jnp.sum(hs * sign, axis=0, keepdims=True), hr)


def _hyena_filter(L, filt):
    w1, b1, w2, b2, w3, b3, freq, w4 = filt
    zpad, window = _filter_consts(L)
    fwd, _ = _dft_mats(L)
    nch, fc2, _ = fwd.shape
    fc = fc2 // 2
    w1p = jnp.pad(w1, ((0, 128 - C_EMB), (0, 0)))
    full = lambda shape: pl.BlockSpec(shape, lambda c: tuple(0 for _ in shape))
    out_spec = pl.BlockSpec((fc, C_DIM), lambda c: (c, 0))
    out_sd = jax.ShapeDtypeStruct((L, C_DIM), F32)
    return pl.pallas_call(
        _filter_body,
        grid=(nch,),
        in_specs=[full((L, 128)), full((128, C_FFN)), full((1, C_FFN)), full((C_FFN, C_FFN)), full((1, C_FFN)),
                  full((C_FFN, C_FFN)), full((1, C_FFN)), full((1, C_FFN)), full((C_FFN, 2 * C_DIM)),
                  full((L, C_DIM)), pl.BlockSpec((1, fc2, L), lambda c: (c, 0, 0))],
        out_specs=[out_spec, out_spec, out_spec],
        out_shape=[out_sd, out_sd, out_sd],
        scratch_shapes=[pltpu.VMEM((L, C_DIM), F32), pltpu.VMEM((L, C_DIM), F32)],
        compiler_params=_cparams("arbitrary"),
        name="hyena_filter",
    )(jnp.asarray(zpad), w1p, b1.reshape(1, C_FFN), w2, b2.reshape(1, C_FFN), w3, b3.reshape(1, C_FFN),
      freq.reshape(1, C_FFN), w4, jnp.asarray(window), jnp.asarray(fwd))


def _hyena_body(u_ref, cw_ref, cb_ref, d_ref, fm_ref, fi_ref, hr_ref, g_ref, hq_ref, y_ref,
                x0_scr, z_scr, acc_scr):
    c = pl.program_id(1)
    L = y_ref.shape[0]
    fc = hr_ref.shape[0]

    @pl.when(c == 0)
    def _():
        row = lax.broadcasted_iota(jnp.int32, (L, C_DIM), 0)

        def short_conv(sec):
            cols = slice(sec * C_DIM, (sec + 1) * C_DIM)
            u = u_ref[:, cols]
            prev = jnp.where(row == 0, 0.0, pltpu.roll(u, 1, axis=0))
            nxt = jnp.where(row == L - 1, 0.0, pltpu.roll(u, L - 1, axis=0))
            return (prev * cw_ref[0:1, cols] + u * cw_ref[1:2, cols] + nxt * cw_ref[2:3, cols]
                    + cb_ref[:, cols])

        x0_scr[...] = short_conv(0)
        z_scr[...] = short_conv(1) * short_conv(2)
        acc_scr[...] = jnp.zeros((L, C_DIM), F32)

    ab = _dot_split(fm_ref[0], z_scr[...])
    a, b = ab[:fc], ab[fc:]
    hr, g, hq = hr_ref[...], g_ref[...], hq_ref[...]
    pq = jnp.concatenate([a * hr - b * g, a * g + b * hq], axis=0)
    acc_scr[...] += _dot_split(fi_ref[0], pq)

    @pl.when(c == pl.num_programs(1) - 1)
    def _():
        y_ref[...] = x0_scr[...] * (acc_scr[...] + z_scr[...] * d_ref[...])


def _hyena(u, row_blk0, n_seq, L, conv_w, conv_b, d_skip, spec):
    hr, g, hq = spec
    fwd, inv = _dft_mats(L)
    nch, fc2, _ = fwd.shape
    fc = fc2 // 2
    u_w = 3 * C_DIM
    return pl.pallas_call(
        _hyena_body,
        grid=(n_seq, nch),
        in_specs=[pl.BlockSpec((L, u_w), lambda b, c: (row_blk0 + b, 0)),
                  pl.BlockSpec((3, u_w), lambda b, c: (0, 0)),
                  pl.BlockSpec((1, u_w), lambda b, c: (0, 0)),
                  pl.BlockSpec((1, C_DIM), lambda b, c: (0, 0)),
                  pl.BlockSpec((1, fc2, L), lambda b, c: (c, 0, 0)),
                  pl.BlockSpec((1, L, fc2), lambda b, c: (c, 0, 0)),
                  pl.BlockSpec((fc, C_DIM), lambda b, c: (c, 0)),
                  pl.BlockSpec((fc, C_DIM), lambda b, c: (c, 0)),
                  pl.BlockSpec((fc, C_DIM), lambda b, c: (c, 0))],
        out_specs=pl.BlockSpec((L, C_DIM), lambda b, c: (b, 0)),
        out_shape=jax.ShapeDtypeStruct((n_seq * L, C_DIM), F32),
        scratch_shapes=[pltpu.VMEM((L, C_DIM), F32)] * 3,
        compiler_params=_cparams("parallel", "arbitrary"),
        name="hyena",
    )(u, conv_w, conv_b.reshape(1, u_w), d_skip.reshape(1, C_DIM), jnp.asarray(fwd), jnp.asarray(inv), hr, g, hq)


def _hgrn_body(q_ref, ff_ref, fb_ref, i_ref, g_ref, lbf_ref, lbb_ref, nd_ref, s0f_ref, s0b_ref,
               o_ref, sf_ref, sb_ref, *, layer):
    L = o_ref.shape[0]
    C = GLA_CHUNK
    S = min(L // GLA_MIN_SPANS, GLA_SPAN)
    nc = S // C
    n_span = L // S
    mid = C // 2
    def lower_bound(gm):
        e = jnp.exp(gm - jnp.max(gm, axis=0, keepdims=True))
        p = e / jnp.sum(e, axis=0, keepdims=True)
        return jnp.sum(p[0:layer + 1], axis=0, keepdims=True) - p[0:1]

    def gates(fx, lb):
        f = lb + (1.0 - lb) * jax.nn.sigmoid(fx)
        return 1.0 - f, jnp.log(f)


    chunk_shift = C.bit_length() - 1
    block_shift = D_KDIM.bit_length() - 1
    ti = lax.broadcasted_iota(jnp.int32, (S, S), 0)
    si = lax.broadcasted_iota(jnp.int32, (S, S), 1)
    same_chunk = (ti >> chunk_shift) == (si >> chunk_shift)
    causal = same_chunk & (si <= ti)
    anti = same_chunk & (si >= ti)
    row_chunk = lax.broadcasted_iota(jnp.int32, (S, nc * D_KDIM), 0) >> chunk_shift
    col_chunk = lax.broadcasted_iota(jnp.int32, (S, nc * D_KDIM), 1) >> block_shift
    own_block = row_chunk == col_chunk

    def spread(x):
        return jnp.where(own_block, jnp.concatenate([x] * nc, axis=1), 0.0)

    def chunk_cumsum(mask, lg):
        tri = mask.astype(BF16)
        hi = lg.astype(BF16)
        r1 = lg - hi.astype(F32)
        mid_t = r1.astype(BF16)
        lo = (r1 - mid_t.astype(F32)).astype(BF16)
        dot = lambda t: jnp.dot(tri, t, preferred_element_type=F32)
        return dot(hi) + dot(mid_t) + dot(lo)

    def per_chunk_rows(b, pos):
        return jnp.concatenate([jnp.broadcast_to(b[n * C + pos:n * C + pos + 1], (C, D_KDIM)) for n in range(nc)],
                               axis=0)

    def one_head(q, v, kf, lgf, kb, lgb, st_f, st_b):
        local = []
        for u in range(n_span):
            rows = slice(u * S, (u + 1) * S)
            qs, vs, kfs, kbs = q[rows], v[rows], kf[rows], kb[rows]
            lgs = jnp.concatenate([lgf[rows], lgb[rows]], axis=1)
            pre = chunk_cumsum(causal, lgs)
            b_f = pre[:, :D_KDIM]
            pre_b = pre[:, D_KDIM:]
            b_b = per_chunk_rows(pre_b, C - 1) - pre_b + lgb[rows]
            ref_f, ref_b = per_chunk_rows(b_f, mid), per_chunk_rows(b_b, mid)
            sc = (jnp.where(causal, _dot_nt(qs * jnp.exp(b_f - ref_f), kfs * jnp.exp(ref_f - b_f)), 0.0)
                  + jnp.where(anti, _dot_nt(qs * jnp.exp(b_b - ref_b), kbs * jnp.exp(ref_b - b_b)), 0.0))
            k_out = jnp.concatenate([kfs * jnp.exp(per_chunk_rows(b_f, C - 1) - b_f),
                                     kbs * jnp.exp(per_chunk_rows(b_b, 0) - b_b)], axis=1)
            kv_t = _dot_tn(spread(vs), k_out)
            local.append((_dot(sc, vs), kv_t, b_f, b_b, qs))

        states_f = [[None] * nc for _ in range(n_span)]
        for u in range(n_span):
            _, kv_t, b_f, _, _ = local[u]
            for n in range(nc):
                states_f[u][n] = st_f
                st_f = st_f * jnp.exp(b_f[n * C + C - 1:n * C + C]) + kv_t[n * D_VDIM:(n + 1) * D_VDIM, :D_KDIM]
        states_b = [[None] * nc for _ in range(n_span)]
        for u in reversed(range(n_span)):
            _, kv_t, _, b_b, _ = local[u]
            for n in reversed(range(nc)):
                states_b[u][n] = st_b
                st_b = st_b * jnp.exp(b_b[n * C:n * C + 1]) + kv_t[n * D_VDIM:(n + 1) * D_VDIM, D_KDIM:]

        outs = []
        for u in range(n_span):
            intra, _, b_f, b_b, qs = local[u]
            q_in = jnp.concatenate([spread(qs * jnp.exp(b_f)), spread(qs * jnp.exp(b_b))], axis=1)
            outs.append(intra + _dot_nt(q_in, jnp.concatenate(states_f[u] + states_b[u], axis=1)))
        return (jnp.concatenate(outs, axis=0) if n_span > 1 else outs[0]), st_f, st_b

    for hh in range(o_ref.shape[1] // D_VDIM):
        cols = slice(hh * D_KDIM, (hh + 1) * D_KDIM)
        kf, lgf = gates(ff_ref[:, cols], lower_bound(lbf_ref[:, cols]))
        kb, lgb = gates(fb_ref[:, cols], lower_bound(lbb_ref[:, cols]))
        o, st_f, st_b = one_head(_silu(q_ref[:, cols]), i_ref[:, cols], kf, lgf, kb, lgb,
                                 jnp.transpose(s0f_ref[0, hh]), jnp.transpose(s0b_ref[0, hh]))
        sf_ref[0, hh] = jnp.transpose(st_f)
        sb_ref[0, hh] = jnp.transpose(st_b)
        o_ref[:, cols] = _rms(o, nd_ref[...]) * _silu(g_ref[:, cols])


def _hgrn(u, row_blk0, n_seq, L, lb_fwd, lb_bwd, norm_d, s0f, s0b, layer):
    hps = HGRN_HEADS_PER_STEP
    width = hps * D_KDIM
    col0 = 3 * C_DIM // width
    groups = D_HEADS // hps
    col = lambda j: pl.BlockSpec((L, width), lambda b, h: (row_blk0 + b, col0 + j * groups + h))
    lbs = pl.BlockSpec((DEPTH, width), lambda b, h: (0, h))
    st = pl.BlockSpec((1, hps, D_KDIM, D_VDIM), lambda b, h: (b, h, 0, 0))
    st_sd = jax.ShapeDtypeStruct((n_seq, D_HEADS, D_KDIM, D_VDIM), F32)
    return pl.pallas_call(
        functools.partial(_hgrn_body, layer=layer),
        grid=(n_seq, groups),
        in_specs=[col(0), col(1), col(2), col(3), col(4), lbs, lbs,
                  pl.BlockSpec((1, D_VDIM), lambda b, h: (0, 0)), st, st],
        out_specs=[pl.BlockSpec((L, width), lambda b, h: (b, h)), st, st],
        out_shape=[jax.ShapeDtypeStruct((n_seq * L, D_HEADS * D_VDIM), F32), st_sd, st_sd],
        compiler_params=_cparams("parallel", "parallel"),
        name="hgrn",
    )(u, u, u, u, u, lb_fwd, lb_bwd, norm_d.reshape(1, D_VDIM), s0f, s0b)


def _pack_bf16_pairs(h):
    n = h.shape[1] // 2
    hi = lax.bitcast_convert_type(h[:, :n].astype(BF16).astype(F32), jnp.int32)
    lo = lax.bitcast_convert_type(h[:, n:].astype(BF16).astype(F32), jnp.int32)
    return hi | lax.shift_right_logical(lo, 16)


def _unpack_bf16_pairs(p):
    hi = lax.bitcast_convert_type(p & jnp.int32(-65536), F32).astype(BF16)
    lo = lax.bitcast_convert_type(lax.shift_left(p, 16), F32).astype(BF16)
    return hi, lo


def _outproj_body(*refs, n_x):
    a_refs, b_refs, x_refs = refs[0:2], refs[2:4], refs[4:4 + n_x]
    mod_ref, gf_ref, w_ref, wrh_ref, wrl_ref, rb_ref, x1_ref, h2_ref, chosen_ref, gk_ref, ik_ref = refs[4 + n_x:]
    m = mod_ref[0]
    half = a_refs[0].shape[1]
    out = _dot(_token_tile(a_refs), w_ref[0:half, :]) + _dot(_token_tile(b_refs), w_ref[half:, :])
    x1 = _token_tile(x_refs) + m[:, 2 * D_MODEL:3 * D_MODEL] * out
    x1_ref[...] = x1
    h2 = _rms(x1, gf_ref[...]) * (1.0 + m[:, 4 * D_MODEL:5 * D_MODEL]) + m[:, 3 * D_MODEL:4 * D_MODEL]
    h2_ref[...] = _pack_bf16_pairs(h2)
    h_hi = h2.astype(BF16)
    h_lo = (h2 - h_hi.astype(F32)).astype(BF16)
    logits = _dot_nt(wrh_ref[...], h_hi) + _dot_nt(wrh_ref[...], h_lo) + _dot_nt(wrl_ref[...], h_hi)
    scores = jax.nn.sigmoid(logits)
    work = scores + rb_ref[...]
    expert = lax.broadcasted_iota(jnp.int32, work.shape, 0).astype(F32)
    slot = lax.broadcasted_iota(jnp.int32, (TOP_K, work.shape[1]), 0)
    chosen = [jnp.zeros(work.shape, F32) for _ in range(MOE_REGIONS)]
    gk = jnp.zeros((TOP_K, work.shape[1]), F32)
    ik = jnp.zeros((TOP_K, work.shape[1]), F32)
    for k in range(TOP_K):
        best = jnp.max(work, axis=0, keepdims=True)
        first = jnp.min(jnp.where(work == best, expert, float(N_EXPERTS)), axis=0, keepdims=True)
        hit = expert == first
        chosen[k // K_PER_REGION] = jnp.where(hit, 1.0, chosen[k // K_PER_REGION])
        gk = jnp.where(slot == k, jnp.sum(jnp.where(hit, scores, 0.0), axis=0, keepdims=True), gk)
        ik = jnp.where(slot == k, first, ik)
        work = jnp.where(hit, -jnp.inf, work)
    for r in range(MOE_REGIONS):
        chosen_ref[r] = chosen[r]
    gk_ref[...] = jnp.transpose(gk / jnp.sum(gk, axis=0, keepdims=True) * ROUTE_SCALE)
    ik_ref[...] = ik


def _outproj(a, b, x, mod_l, gain_ffn, w_out, w_router, router_bias):
    half = a[0].shape[1]
    a_specs, a_args = _token_specs(a, half)
    b_specs, b_args = _token_specs(b, half)
    x_specs, x_args = _token_specs(x, D_MODEL)
    wr_t = w_router.T
    wr_hi = wr_t.astype(BF16)
    wr_lo = (wr_t - wr_hi.astype(F32)).astype(BF16)
    return pl.pallas_call(
        functools.partial(_outproj_body, n_x=len(x_args)),
        grid=(T_ALL // TM,),
        in_specs=a_specs + b_specs + x_specs + [
                  pl.BlockSpec((1, 1, N_MOD * D_MODEL), lambda i: (_mod_row(i), 0, 0)),
                  pl.BlockSpec((1, D_MODEL), lambda i: (0, 0)),
                  pl.BlockSpec((2 * half, D_MODEL), lambda i: (0, 0)),
                  pl.BlockSpec((N_EXPERTS, D_MODEL), lambda i: (0, 0)),
                  pl.BlockSpec((N_EXPERTS, D_MODEL), lambda i: (0, 0)),
                  pl.BlockSpec((N_EXPERTS, 1), lambda i: (0, 0))],
        out_specs=[pl.BlockSpec((TM, D_MODEL), lambda i: (i, 0)),
                   pl.BlockSpec((TM, D_MODEL // 2), lambda i: (i, 0)),
                   pl.BlockSpec((MOE_REGIONS, N_EXPERTS, TM), lambda i: (0, 0, i)),
                   pl.BlockSpec((TM, TOP_K), lambda i: (i, 0)),
                   pl.BlockSpec((TOP_K, TM), lambda i: (0, i))],
        out_shape=[jax.ShapeDtypeStruct((T_ALL, D_MODEL), F32),
                   jax.ShapeDtypeStruct((T_ALL, D_MODEL // 2), jnp.int32),
                   jax.ShapeDtypeStruct((MOE_REGIONS, N_EXPERTS, T_ALL), F32),
                   jax.ShapeDtypeStruct((T_ALL, TOP_K), F32),
                   jax.ShapeDtypeStruct((TOP_K, T_ALL), F32)],
        compiler_params=_cparams("parallel"),
        name="outproj_router",
    )(*a_args, *b_args, *x_args, mod_l, gain_ffn.reshape(1, D_MODEL), w_out, wr_hi, wr_lo,
      router_bias.reshape(N_EXPERTS, 1))


def _route_body(chosen_ref, ik_ref, dest_ref, first_ref, count_ref, short_ref, pos_scr):
    n_tiles = T_ALL // TM
    r = lax.broadcasted_iota(jnp.int32, (TM, TM), 0)
    c = lax.broadcasted_iota(jnp.int32, (TM, TM), 1)
    before = (r < c).astype(BF16)

    counts = jnp.zeros((N_EXPERTS, 1), F32)
    for i in range(n_tiles):
        cols = slice(i * TM, (i + 1) * TM)
        m = chosen_ref[0, :, cols]
        pos_scr[:, cols] = jnp.dot(m.astype(BF16), before, preferred_element_type=F32) + counts
        counts = counts + jnp.sum(m, axis=1, keepdims=True)
    padded = jnp.ceil(counts * (1.0 / MOE_BLK)) * MOE_BLK
    ei = lax.broadcasted_iota(jnp.int32, (N_EXPERTS, N_EXPERTS), 0)
    ej = lax.broadcasted_iota(jnp.int32, (N_EXPERTS, N_EXPERTS), 1)
    end = _dot_hi((ej <= ei).astype(F32), jnp.broadcast_to(padded, (N_EXPERTS, LANES)))[:, 0:1]
    start = end - padded

    expert = lax.broadcasted_iota(jnp.int32, (N_EXPERTS, TM), 0).astype(F32)
    slot = lax.broadcasted_iota(jnp.int32, (K_PER_REGION, TM), 0)
    for i in range(n_tiles):
        cols = slice(i * TM, (i + 1) * TM)
        row_of = pos_scr[:, cols] + start
        ik = ik_ref[0, :, cols]
        acc = jnp.zeros((K_PER_REGION, TM), F32)
        for k in range(K_PER_REGION):
            pick = jnp.sum(jnp.where(expert == ik[k:k + 1, :], row_of, 0.0), axis=0, keepdims=True)
            acc = jnp.where(slot == k, pick, acc)
        dest_ref[0, :, cols] = acc.astype(jnp.int32)
    first_ref[0] = jnp.broadcast_to(start * (1.0 / MOE_BLK), (N_EXPERTS, LANES)).astype(jnp.int32)
    count_ref[0] = jnp.broadcast_to(padded * (1.0 / MOE_BLK), (N_EXPERTS, LANES)).astype(jnp.int32)
    in_last = counts - (padded - MOE_BLK)
    short = jnp.where((counts > 0.0) & (in_last <= MOE_BLK // 2), 1.0, 0.0)
    short_ref[0] = jnp.broadcast_to(short, (N_EXPERTS, LANES)).astype(jnp.int32)


def _route(chosen, ik):
    per_region = lambda rows, cols: pl.BlockSpec((1, rows, cols), lambda r: (r, 0, 0))
    table = jax.ShapeDtypeStruct((MOE_REGIONS, N_EXPERTS, LANES), jnp.int32)
    return pl.pallas_call(
        _route_body,
        grid=(MOE_REGIONS,),
        in_specs=[per_region(N_EXPERTS, T_ALL), per_region(K_PER_REGION, T_ALL)],
        out_specs=[per_region(K_PER_REGION, T_ALL)] + [per_region(N_EXPERTS, LANES)] * 3,
        out_shape=[jax.ShapeDtypeStruct((MOE_REGIONS, K_PER_REGION, T_ALL), jnp.int32), table, table, table],
        scratch_shapes=[pltpu.VMEM((N_EXPERTS, T_ALL), F32)],
        compiler_params=_cparams("arbitrary"),
        name="moe_route",
    )(chosen, ik.reshape(MOE_REGIONS, K_PER_REGION, T_ALL))


def _sc_worker_id():
    return lax.axis_index("s") * SC_CORES + lax.axis_index("c")


def _sc_dispatch(h2p, dest):
    n_chunks = T_ALL // DISP_CHUNK
    k_per = dest.shape[0] // DISP_SPLIT
    items_per_worker = n_chunks * DISP_SPLIT // SC_WORKERS
    chunk_stride = SC_WORKERS // DISP_SPLIT
    width = h2p.shape[1]
    mesh = plsc.VectorSubcoreMesh(core_axis_name="c", subcore_axis_name="s")

    @functools.partial(
        pl.kernel, mesh=mesh,
        out_type=jax.ShapeDtypeStruct((MOE_ROWS, width), jnp.int32),
        scratch_types=[pltpu.VMEM((k_per, DISP_CHUNK), jnp.int32), pltpu.VMEM((DISP_CHUNK, width), jnp.int32),
                       pltpu.SemaphoreType.DMA],
    )
    def run(x_hbm, dest_hbm, xs_hbm, idx_v, rows_v, sem):
        wid = _sc_worker_id()
        group = wid % DISP_SPLIT
        for i in range(items_per_worker):
            chunk = i * chunk_stride + wid // DISP_SPLIT
            tokens = pl.ds(pl.multiple_of(chunk * DISP_CHUNK, DISP_CHUNK), DISP_CHUNK)
            pltpu.sync_copy(dest_hbm.at[group, :, tokens], idx_v)
            pltpu.sync_copy(x_hbm.at[tokens], rows_v)
            scatters = [pltpu.make_async_copy(rows_v, xs_hbm.at[idx_v.at[k]], sem) for k in range(k_per)]
            for cp in scatters:
                cp.start()
            for cp in scatters:
                cp.wait()

    return run(h2p, dest.reshape(DISP_SPLIT, k_per, T_ALL))


def _sc_collect(y, dest_flat):
    n_k = dest_flat.shape[0] // T_ALL
    per_worker = T_ALL // SC_WORKERS
    n_chunks = per_worker // COLLECT_CHUNK
    n_steps = n_k * n_chunks
    width = y.shape[1]
    mesh = plsc.VectorSubcoreMesh(core_axis_name="c", subcore_axis_name="s")

    @functools.partial(
        pl.kernel, mesh=mesh,
        out_type=jax.ShapeDtypeStruct((n_k * T_ALL, width), y.dtype),
        scratch_types=[pltpu.VMEM((n_k * per_worker,), jnp.int32),
                       pltpu.VMEM((COLLECT_CHUNK, width), y.dtype), pltpu.VMEM((COLLECT_CHUNK, width), y.dtype),
                       pltpu.SemaphoreType.DMA, pltpu.SemaphoreType.DMA],
    )
    def run(y_hbm, dest_hbm, yg_hbm, idx_v, rows0, rows1, sem0, sem1):
        wid = _sc_worker_id()
        bufs = ((rows0, sem0), (rows1, sem1))
        for k in range(n_k):
            pltpu.sync_copy(dest_hbm.at[pl.ds(k * T_ALL + wid * per_worker, per_worker)],
                            idx_v.at[pl.ds(k * per_worker, per_worker)])

        def gather(step, buf):
            rows, sem = buf
            idx = idx_v.at[pl.ds(pl.multiple_of(step * COLLECT_CHUNK, 8), COLLECT_CHUNK)]
            return pltpu.make_async_copy(y_hbm.at[idx], rows, sem)

        def out_rows(step):
            off = (step // n_chunks) * T_ALL + wid * per_worker + (step % n_chunks) * COLLECT_CHUNK
            return yg_hbm.at[pl.ds(pl.multiple_of(off, 8), COLLECT_CHUNK)]

        gather(0, bufs[0]).start()

        @pl.loop(0, n_steps, step=2)
        def _(base):
            for j in range(2):
                step = base + j

                @pl.when(step + 1 < n_steps)
                def _():
                    gather(step + 1, bufs[1 - j]).start()

                gather(step, bufs[j]).wait()
                pltpu.sync_copy(bufs[j][0], out_rows(step))

    return run(y, dest_flat)


def _expert_body(first_ref, count_ref, short_ref, xs_hbm, wg_hbm, wu_hbm, wd_hbm, y_hbm,
                 wg_f32, wu_f32, wd_f32, wg_bf, wu_bf, wd_bf, x_buf, y_buf, w_sem, in_sem, out_sem, *, layer):
    e = pl.program_id(0)
    first = first_ref[e]
    count = count_ref[e]
    n_used = first_ref[N_EXPERTS - 1] + count_ref[N_EXPERTS - 1]
    half = D_MODEL // 2

    def weight_copies(ex):
        slot = lax.rem(ex, EXPERT_W_SLOTS)
        out = []
        for src, dst in ((wg_hbm, wg_f32), (wu_hbm, wu_f32), (wd_hbm, wd_f32)):
            size = dst.shape[1] // EXPERT_W_PARTS
            for part in range(EXPERT_W_PARTS):
                rows = pl.ds(part * size, size)
                out.append(pltpu.make_async_copy(src.at[layer, ex, rows], dst.at[slot, rows], w_sem.at[slot]))
        return out

    @pl.when(e == 0)
    def _():
        for ahead in range(EXPERT_W_SLOTS - 1):
            for cp in weight_copies(ahead):
                cp.start()

    for cp in weight_copies(e):
        cp.wait()

    @pl.when(e + EXPERT_W_SLOTS - 1 < N_EXPERTS)
    def _():
        for cp in weight_copies(e + EXPERT_W_SLOTS - 1):
            cp.start()

    w_slot = lax.rem(e, EXPERT_W_SLOTS)
    wg_bf[...] = wg_f32[w_slot].astype(BF16)
    wu_bf[...] = wu_f32[w_slot].astype(BF16)
    wd_bf[...] = wd_f32[w_slot].astype(BF16)

    def part_rows(g, part, n_parts):
        size = MOE_BLK // n_parts
        return pl.ds(pl.multiple_of(g * MOE_BLK + part * size, size), size), pl.ds(part * size, size)

    def in_copies(g):
        slot = g & (EXPERT_SLOTS - 1)
        out = []
        for part in range(EXPERT_IN_PARTS):
            src, dst = part_rows(g, part, EXPERT_IN_PARTS)
            out.append(pltpu.make_async_copy(xs_hbm.at[src], x_buf.at[slot, dst], in_sem.at[slot]))
        return out

    def out_copies(g):
        slot = g & (EXPERT_SLOTS - 1)
        out = []
        for part in range(EXPERT_OUT_PARTS):
            dst, src = part_rows(g, part, EXPERT_OUT_PARTS)
            out.append(pltpu.make_async_copy(y_buf.at[slot, src], y_hbm.at[dst], out_sem.at[slot]))
        return out

    @pl.when((first == 0) & (count > 0))
    def _():
        for ahead in range(EXPERT_SLOTS - 1):
            @pl.when(ahead < n_used)
            def _():
                for cp in in_copies(ahead):
                    cp.start()

    def block(b, carry):
        g = first + b
        slot = g & (EXPERT_SLOTS - 1)
        for cp in in_copies(g):
            cp.wait()

        @pl.when(g + EXPERT_SLOTS - 1 < n_used)
        def _():
            for cp in in_copies(g + EXPERT_SLOTS - 1):
                cp.start()

        @pl.when(g >= EXPERT_SLOTS)
        def _():
            for cp in out_copies(g - EXPERT_SLOTS):
                cp.wait()

        def ffn(n_rows):
            hi, lo = _unpack_bf16_pairs(x_buf[slot, 0:n_rows])

            def proj(w_bf):
                return (jnp.dot(hi, w_bf[0:half, :], preferred_element_type=F32)
                        + jnp.dot(lo, w_bf[half:, :], preferred_element_type=F32))

            hid = _silu(proj(wg_bf)) * proj(wu_bf)
            y_buf[slot, 0:n_rows] = _pack_bf16_pairs(
                jnp.dot(hid.astype(BF16), wd_bf[...], preferred_element_type=F32))

        short = (b == count - 1) & (short_ref[e] == 1)

        @pl.when(short)
        def _():
            ffn(MOE_BLK // 2)
            y_buf[slot, MOE_BLK // 2:MOE_BLK] = jnp.zeros((MOE_BLK // 2, D_MODEL // 2), jnp.int32)

        @pl.when(jnp.logical_not(short))
        def _():
            ffn(MOE_BLK)

        for cp in out_copies(g):
            cp.start()
        return carry

    lax.fori_loop(0, count, block, 0)

    @pl.when(e == N_EXPERTS - 1)
    def _():
        for back in range(EXPERT_SLOTS, 0, -1):
            @pl.when(n_used >= back)
            def _():
                for cp in out_copies(n_used - back):
                    cp.wait()


EXPERT_SLOTS = 4
EXPERT_W_SLOTS = 4
EXPERT_W_PARTS = 2
EXPERT_IN_PARTS = 2
EXPERT_OUT_PARTS = 4


def _experts(first_blk, n_blk, short_last, xs, layer, w_gate, w_up, w_down):
    anywhere = pl.BlockSpec(memory_space=pl.ANY)
    grid_spec = pltpu.PrefetchScalarGridSpec(
        num_scalar_prefetch=3,
        grid=(N_EXPERTS,),
        in_specs=[anywhere] * 4,
        out_specs=anywhere,
        scratch_shapes=[pltpu.VMEM((EXPERT_W_SLOTS, D_MODEL, D_EXPERT), F32),
                        pltpu.VMEM((EXPERT_W_SLOTS, D_MODEL, D_EXPERT), F32),
                        pltpu.VMEM((EXPERT_W_SLOTS, D_EXPERT, D_MODEL), F32),
                        pltpu.VMEM((D_MODEL, D_EXPERT), BF16), pltpu.VMEM((D_MODEL, D_EXPERT), BF16),
                        pltpu.VMEM((D_EXPERT, D_MODEL), BF16),
                        pltpu.VMEM((EXPERT_SLOTS, MOE_BLK, D_MODEL // 2), jnp.int32),
                        pltpu.VMEM((EXPERT_SLOTS, MOE_BLK, D_MODEL // 2), jnp.int32),
                        pltpu.SemaphoreType.DMA((EXPERT_W_SLOTS,)),
                        pltpu.SemaphoreType.DMA((EXPERT_SLOTS,)), pltpu.SemaphoreType.DMA((EXPERT_SLOTS,))],
    )
    return pl.pallas_call(
        functools.partial(_expert_body, layer=layer),
        grid_spec=grid_spec,
        out_shape=jax.ShapeDtypeStruct((MOE_ROWS, D_MODEL // 2), jnp.int32),
        compiler_params=_cparams("arbitrary"),
        name="moe_experts",
    )(first_blk, n_blk, short_last, xs, w_gate, w_up, w_down)


def _combine_body(x1_ref, h2_ref, *refs, final):
    yg_refs = refs[:MOE_REGIONS]
    gk_ref, mod_ref, sg_ref, su_ref, sd_ref, fn_ref, *o_refs = refs[MOE_REGIONS:]
    hi, lo = _unpack_bf16_pairs(h2_ref[...])
    half = D_MODEL // 2

    def proj(w_ref):
        return _dot(hi, w_ref[0:half, :]) + _dot(lo, w_ref[half:, :])

    shared = _dot(_silu(proj(sg_ref)) * proj(su_ref), sd_ref[...])
    acc_hi, acc_lo = shared[:, :half], shared[:, half:]
    gk = gk_ref[...]
    for k in range(TOP_K):
        y_hi, y_lo = _unpack_bf16_pairs(yg_refs[k // K_PER_REGION][k % K_PER_REGION])
        acc_hi = acc_hi + gk[:, k:k + 1] * y_hi.astype(F32)
        acc_lo = acc_lo + gk[:, k:k + 1] * y_lo.astype(F32)
    acc = jnp.concatenate([acc_hi, acc_lo], axis=1)
    m = mod_ref[0]
    y = x1_ref[...] + m[:, 5 * D_MODEL:6 * D_MODEL] * acc
    if not final:
        o_refs[0][...] = y
        return
    y = _rms(y, fn_ref[...])
    is_ctx = pl.program_id(0) < N_CTX_TILES

    @pl.when(is_ctx)
    def _():
        o_refs[0][...] = y

    @pl.when(jnp.logical_not(is_ctx))
    def _():
        o_refs[1][...] = y


def _combine(x1, h2p, yg, gk, mod_l, ws_gate, ws_up, ws_down, final_norm, final):
    tok = lambda shape: pl.BlockSpec(shape, lambda i: (i, 0))
    full = lambda shape: pl.BlockSpec(shape, lambda i: (0, 0))
    if final:
        out_specs, _ = _token_specs((None, None), D_MODEL)
        out_shape = [jax.ShapeDtypeStruct((T_CTX, D_MODEL), F32), jax.ShapeDtypeStruct((T_LAT, D_MODEL), F32)]
    else:
        out_specs = tok((TM, D_MODEL))
        out_shape = jax.ShapeDtypeStruct((T_ALL, D_MODEL), F32)
    return pl.pallas_call(
        functools.partial(_combine_body, final=final),
        grid=(T_ALL // TM,),
        in_specs=[tok((TM, D_MODEL)), tok((TM, D_MODEL // 2))]
                 + [pl.BlockSpec((K_PER_REGION, TM, D_MODEL // 2), lambda i: (0, i, 0))] * MOE_REGIONS
                 + [tok((TM, TOP_K)),
                  pl.BlockSpec((1, 1, N_MOD * D_MODEL), lambda i: (_mod_row(i), 0, 0)),
                  full((D_MODEL, D_EXPERT)), full((D_MODEL, D_EXPERT)), full((D_EXPERT, D_MODEL)),
                  full((1, D_MODEL))],
        out_specs=out_specs,
        out_shape=out_shape,
        compiler_params=_cparams("arbitrary"),
        name="moe_combine",
    )(x1, h2p, *yg, gk, mod_l, ws_gate, ws_up, ws_down, final_norm.reshape(1, D_MODEL))


def _moe(x1, h2p, chosen, gk, ik, mod_l, layer, w_gate, w_up, w_down, ws_gate, ws_up, ws_down, final_norm, final):
    dest, first_blk, n_blk, short_last = _route(chosen, ik)
    yg = []
    for r in range(MOE_REGIONS):
        xs = _sc_dispatch(h2p, dest[r])
        y = _experts(first_blk[r, :, 0], n_blk[r, :, 0], short_last[r, :, 0], xs, layer, w_gate, w_up, w_down)
        yg.append(_sc_collect(y, dest[r].reshape(-1)).reshape(K_PER_REGION, T_ALL, D_MODEL // 2))
    return _combine(x1, h2p, yg, gk, mod_l, ws_gate, ws_up, ws_down, final_norm, final)


def kernel(x_prompt, x_sample, cache_a_k, cache_a_v, cache_b_k, cache_b_v, state_d_fwd, state_d_bwd, c, c_ctx, w_ada, b_ada, norm_mix, norm_ffn, w_in_attn, w_out_attn, sink_a, rpb_b, w_in_rec, w_out_rec, conv_w, conv_b, filt_w1, filt_b1, filt_w2, filt_b2, filt_w3, filt_b3, filt_freq, filt_w4, d_skip, lb_fwd, lb_bwd, norm_d, w_router, router_bias, w_gate, w_up, w_down, ws_gate, ws_up, ws_down, final_norm):
    x = (x_prompt.reshape(T_CTX, D_MODEL), x_sample.reshape(T_LAT, D_MODEL))
    cvec = jnp.concatenate([c_ctx[None, :], c], axis=0)
    c_lanes = jnp.broadcast_to(cvec[:, :, None], (N_CVEC, D_MODEL, LANES))
    mod = [_ada(c_lanes, l, w_ada, b_ada).reshape(CVEC_PAD, 1, N_MOD * D_MODEL) for l in range(DEPTH)]

    new_kv = None
    new_state = None
    for l in range(DEPTH):
        j = l // 2
        final = l == DEPTH - 1
        if l % 2 == 0:
            qkv = _inproj(x, mod[l], norm_mix[l], w_in_attn[j])
            oa_ctx, ob_ctx, *new_kv = _ctx_attn(qkv, sink_a[j])
            new_kv = tuple(new_kv)
            q_rot, k_rot = _rope(qkv)
            cache = lambda t: t[:, j].reshape(DEC_BATCH, PAST_LEN, -1)
            oa_lat = _win_attn(qkv, q_rot, k_rot, cache(cache_a_k), cache(cache_a_v), sink_a[j])
            ob_lat = _na_attn(qkv, cache(cache_b_k), cache(cache_b_v), _na_rel_rows(rpb_b[j]))
            mix_a = (oa_ctx, oa_lat)
            mix_b = (ob_ctx, ob_lat)
            w_out = w_out_attn[j]
        else:
            u = _inproj(x, mod[l], norm_mix[l], w_in_rec[j])
            filt = (filt_w1[j], filt_b1[j], filt_w2[j], filt_b2[j], filt_w3[j], filt_b3[j], filt_freq[j],
                    filt_w4[j])
            y_ctx = _hyena(u, 0, BATCH, SEQ, conv_w[j], conv_b[j], d_skip[j], _hyena_filter(SEQ, filt))
            y_lat = _hyena(u, T_CTX // DEC_SEQ, DEC_BATCH, DEC_SEQ, conv_w[j], conv_b[j], d_skip[j],
                           _hyena_filter(DEC_SEQ, filt))
            zeros = jnp.zeros((BATCH, D_HEADS, D_KDIM, D_VDIM), F32)
            o_ctx, s_f, s_b = _hgrn(u, 0, BATCH, SEQ, lb_fwd, lb_bwd, norm_d[j], zeros, zeros, l)
            o_lat, _, _ = _hgrn(u, T_CTX // DEC_SEQ, DEC_BATCH, DEC_SEQ, lb_fwd, lb_bwd, norm_d[j],
                                state_d_fwd[:, j], state_d_bwd[:, j], l)
            new_state = (s_f[:, None], s_b[:, None])
            mix_a = (y_ctx, y_lat)
            mix_b = (o_ctx, o_lat)
            w_out = w_out_rec[j]
        x1, h2p, chosen, gk, ik = _outproj(mix_a, mix_b, x, mod[l], norm_ffn[l], w_out, w_router[l],
                                           router_bias[l])
        x = _moe(x1, h2p, chosen, gk, ik, mod[l], l, w_gate, w_up, w_down, ws_gate[l], ws_up[l],
                 ws_down[l], final_norm, final)

    y_prompt = x[0].reshape(BATCH, SEQ, D_MODEL)
    y_sample = x[1].reshape(DEC_BATCH, DEC_SEQ, D_MODEL)
    return (y_prompt, y_sample) + new_kv + new_state
```

```python
import functools
import math

import numpy as np
import jax
import jax.numpy as jnp
from jax import lax
from jax.experimental import pallas as pl
from jax.experimental.pallas import tpu as pltpu
from jax.experimental.pallas import tpu_sc as plsc

F32 = jnp.float32
BF16 = jnp.bfloat16
HI = lax.Precision.HIGHEST

D_MODEL = 1024
BATCH = 16
SEQ = 256
DEPTH = 2
DEC_BATCH = 2
DEC_SEQ = 1024
PAST_LEN = 512
GRID_W = 64
HEAD_DIM = 64
N_MOD = 6
RMS_EPS = 1e-6
A_HEADS = 8
A_KV_HEADS = 2
A_GROUP = A_HEADS // A_KV_HEADS
WINDOW = 128
ROPE_BASE = 10000.0
B_HEADS = 8
NA_ROWS = 8
NA_COLS = 16
C_DIM = 512
C_EMB = 33
C_FFN = 64
HYENA_MIN_DECAY = math.log(1e-2) / 1.5
HYENA_MAX_DECAY = math.log(1e-2) / 0.3
D_KDIM = 128
D_VDIM = 128
D_HEADS = 4
N_EXPERTS = 64
TOP_K = 8
D_EXPERT = 256
ROUTE_SCALE = 2.5
A_Q = A_HEADS * HEAD_DIM
A_KV = A_KV_HEADS * HEAD_DIM
B_W = B_HEADS * HEAD_DIM
ATTN_IN = A_Q + 2 * A_KV + 3 * B_W

T_CTX = BATCH * SEQ
T_LAT = DEC_BATCH * DEC_SEQ
T_ALL = T_CTX + T_LAT
N_CVEC = 1 + DEC_BATCH
CVEC_PAD = 8
TM = 512
MASK_NEG = -1e30
GLA_CHUNK = 64
GLA_SPAN = 256
GLA_MIN_SPANS = 2
HGRN_HEADS_PER_STEP = 4
DFT_CHUNK = 512
MOE_BLK = 512
MOE_REGIONS = 1
K_PER_REGION = TOP_K // MOE_REGIONS
MOE_NBLK = -(-(T_ALL * K_PER_REGION + N_EXPERTS * (MOE_BLK - 1)) // MOE_BLK)
MOE_ROWS = MOE_NBLK * MOE_BLK
SC_CORES = 2
SC_SUBCORES = 16
SC_WORKERS = SC_CORES * SC_SUBCORES
DISP_CHUNK = 128
DISP_SPLIT = 2
COLLECT_CHUNK = 64
VMEM_LIMIT = 56 * 1024 * 1024


def _cparams(*sem):
    return pltpu.CompilerParams(dimension_semantics=sem, vmem_limit_bytes=VMEM_LIMIT)


def _mod_row(i):
    return jnp.where(i < T_CTX // TM, 0, 1 + (i - T_CTX // TM) // (DEC_SEQ // TM))


def _dot(a, b):
    return jnp.dot(a.astype(BF16), b.astype(BF16), preferred_element_type=F32)


def _dot_nt(a, b):
    return lax.dot_general(a.astype(BF16), b.astype(BF16), (((1,), (1,)), ((), ())),
                           preferred_element_type=F32)


def _dot_tn(a, b):
    return lax.dot_general(a.astype(BF16), b.astype(BF16), (((0,), (0,)), ((), ())),
                           preferred_element_type=F32)


def _dot_hi(a, b):
    return jnp.dot(a, b, precision=HI, preferred_element_type=F32)


def _split_bf16(x):
    hi = x.astype(BF16)
    return hi, (x - hi.astype(F32)).astype(BF16)


def _dot_split(a, b):
    a_hi, a_lo = _split_bf16(a)
    b_hi, b_lo = _split_bf16(b)
    dot = lambda x, y: jnp.dot(x, y, preferred_element_type=F32)
    return dot(a_hi, b_hi) + dot(a_hi, b_lo) + dot(a_lo, b_hi)


def _silu(x):
    return x * jax.nn.sigmoid(x)


def _rms(x, g):
    return x * lax.rsqrt(jnp.mean(x * x, axis=-1, keepdims=True) + RMS_EPS) * g


ADA_TN = 1536
ADA_UNROLL = 4


def _ada_body(cb_ref, w_ref, b_ref, o_ref):
    tn = o_ref.shape[-1]
    n_slab = tn // LANES

    def step(k8, accs):
        r0 = pl.multiple_of(k8 * 8, 8)
        sk = [_silu(cb_ref[j, pl.ds(r0, 8), :]) for j in range(N_CVEC)]
        out = []
        for s in range(n_slab):
            wk = w_ref[0, pl.ds(r0, 8), s * LANES:(s + 1) * LANES]
            out.extend(accs[s * N_CVEC + j] + wk * sk[j] for j in range(N_CVEC))
        return tuple(out)

    accs = lax.fori_loop(0, D_MODEL // 8, step,
                         tuple(jnp.zeros((8, LANES), F32) for _ in range(n_slab * N_CVEC)), unroll=ADA_UNROLL)
    o_ref[0] = jnp.zeros((CVEC_PAD, tn), F32)
    for s in range(n_slab):
        for j in range(N_CVEC):
            o_ref[0, j:j + 1, s * LANES:(s + 1) * LANES] = (
                jnp.sum(accs[s * N_CVEC + j], axis=0, keepdims=True) + b_ref[0, :, s * LANES:(s + 1) * LANES])


def _ada(c_lanes, layer, w_ada, b_ada):
    n_out = N_MOD * D_MODEL
    return pl.pallas_call(
        _ada_body,
        grid=(n_out // ADA_TN,),
        in_specs=[pl.BlockSpec((N_CVEC, D_MODEL, LANES), lambda n: (0, 0, 0)),
                  pl.BlockSpec((1, D_MODEL, ADA_TN), lambda n: (layer, 0, n)),
                  pl.BlockSpec((1, 1, ADA_TN), lambda n: (layer, 0, n))],
        out_specs=pl.BlockSpec((1, CVEC_PAD, ADA_TN), lambda n: (0, 0, n)),
        out_shape=jax.ShapeDtypeStruct((1, CVEC_PAD, n_out), F32),
        compiler_params=_cparams("parallel"),
        name="ada",
    )(c_lanes, w_ada, b_ada.reshape(DEPTH, 1, n_out))


N_CTX_TILES = T_CTX // TM


def _token_specs(x, width):
    if not isinstance(x, tuple):
        return [pl.BlockSpec((TM, width), lambda i: (i, 0))], (x,)
    return ([pl.BlockSpec((TM, width), lambda i: (jnp.minimum(i, N_CTX_TILES - 1), 0)),
             pl.BlockSpec((TM, width), lambda i: (jnp.maximum(i - N_CTX_TILES, 0), 0))], x)


def _token_tile(refs):
    if len(refs) == 1:
        return refs[0][...]
    return jnp.where(pl.program_id(0) < N_CTX_TILES, refs[0][...], refs[1][...])


def _inproj_body(*refs, n_x):
    x_refs, (mod_ref, g_ref, w_ref, o_ref, w_bf) = refs[:n_x], refs[n_x:]

    @pl.when(pl.program_id(0) == 0)
    def _():
        w_bf[...] = w_ref[...].astype(BF16)

    m = mod_ref[0]
    h = _rms(_token_tile(x_refs), g_ref[...]) * (1.0 + m[:, D_MODEL:2 * D_MODEL]) + m[:, 0:D_MODEL]
    o_ref[...] = _dot(h, w_bf[...])


def _inproj(x, mod_l, gain, w):
    n = w.shape[1]
    x_specs, x_args = _token_specs(x, D_MODEL)
    return pl.pallas_call(
        functools.partial(_inproj_body, n_x=len(x_args)),
        grid=(T_ALL // TM,),
        in_specs=x_specs + [pl.BlockSpec((1, 1, N_MOD * D_MODEL), lambda i: (_mod_row(i), 0, 0)),
                            pl.BlockSpec((1, D_MODEL), lambda i: (0, 0)),
                            pl.BlockSpec((D_MODEL, n), lambda i: (0, 0), pipeline_mode=pl.Buffered(1))],
        out_specs=pl.BlockSpec((TM, n), lambda i: (i, 0)),
        out_shape=jax.ShapeDtypeStruct((T_ALL, n), F32),
        scratch_shapes=[pltpu.VMEM((D_MODEL, n), BF16)],
        compiler_params=_cparams("arbitrary"),
        name="inproj",
    )(*x_args, mod_l, gain.reshape(1, D_MODEL), w)


def _ctx_attn_body(qkv_ref, sink_ref, oa_ref, ob_ref, ak_ref, av_ref, bk_ref, bv_ref):
    scale = HEAD_DIM ** -0.5
    lane = lax.broadcasted_iota(jnp.int32, (SEQ, LANES), 1)
    in_half = [lane < HEAD_DIM, lane >= HEAD_DIM]

    def attend(q, k, v, sink):
        s = _dot_nt(q, k) * scale
        m = jnp.max(s, axis=-1, keepdims=True)
        if sink is not None:
            m = jnp.maximum(m, sink)
        p = jnp.exp(s - m)
        den = jnp.sum(p, axis=-1, keepdims=True)
        if sink is not None:
            den = den + jnp.exp(sink - m)
        return _dot(p, v) / den

    def tile(first_col, t):
        return qkv_ref[:, first_col + t * LANES:first_col + (t + 1) * LANES]

    base = A_Q + 2 * A_KV
    for hk in range(A_KV_HEADS):
        dst = pl.ds(hk, SEQ, stride=A_KV_HEADS)
        ak_ref[0, dst, :] = qkv_ref[:, A_Q + hk * HEAD_DIM:A_Q + (hk + 1) * HEAD_DIM]
        av_ref[0, dst, :] = qkv_ref[:, A_Q + A_KV + hk * HEAD_DIM:A_Q + A_KV + (hk + 1) * HEAD_DIM]
    for h in range(B_HEADS):
        dst = pl.ds(h, SEQ, stride=B_HEADS)
        bk_ref[0, dst, :] = qkv_ref[:, base + B_W + h * HEAD_DIM:base + B_W + (h + 1) * HEAD_DIM]
        bv_ref[0, dst, :] = qkv_ref[:, base + 2 * B_W + h * HEAD_DIM:base + 2 * B_W + (h + 1) * HEAD_DIM]

    k_t, v_t = tile(A_Q, 0), tile(A_Q + A_KV, 0)
    k_sw, v_sw = pltpu.roll(k_t, HEAD_DIM, axis=1), pltpu.roll(v_t, HEAD_DIM, axis=1)
    tiles_per_kv = A_GROUP // HEADS_PER_TILE
    for hk in range(A_KV_HEADS):
        q_tiles = [tile(0, hk * tiles_per_kv + j) for j in range(tiles_per_kv)]
        halves = []
        for p in range(HEADS_PER_TILE):
            q = jnp.concatenate([jnp.where(in_half[p], qt, 0.0) for qt in q_tiles], axis=0)
            heads = [(hk * tiles_per_kv + j) * HEADS_PER_TILE + p for j in range(tiles_per_kv)]
            sink = jnp.concatenate([jnp.broadcast_to(sink_ref[:, h:h + 1], (SEQ, 1)) for h in heads], axis=0)
            halves.append(attend(q, k_t if p == hk else k_sw, v_t if p == hk else v_sw, sink))
        first_half = lax.broadcasted_iota(jnp.int32, halves[0].shape, 1) < HEAD_DIM
        o = jnp.where(first_half, halves[0], halves[1])
        for j in range(tiles_per_kv):
            t = hk * tiles_per_kv + j
            oa_ref[:, t * LANES:(t + 1) * LANES] = o[j * SEQ:(j + 1) * SEQ]

    for t in range(B_HEADS // HEADS_PER_TILE):
        q_t, k_b, v_b = tile(base, t), tile(base + B_W, t), tile(base + 2 * B_W, t)
        halves = [attend(jnp.where(in_half[p], q_t, 0.0), k_b, v_b, None) for p in range(HEADS_PER_TILE)]
        ob_ref[:, t * LANES:(t + 1) * LANES] = jnp.where(in_half[0], halves[0], halves[1])


def _ctx_attn(qkv, sink):
    kv_spec = lambda heads: pl.BlockSpec((1, SEQ * heads, HEAD_DIM), lambda b: (b, 0, 0))
    kv_sd = lambda heads: jax.ShapeDtypeStruct((BATCH, SEQ * heads, HEAD_DIM), F32)
    outs = pl.pallas_call(
        _ctx_attn_body,
        grid=(BATCH,),
        in_specs=[pl.BlockSpec((SEQ, ATTN_IN), lambda b: (b, 0)),
                  pl.BlockSpec((1, A_HEADS), lambda b: (0, 0))],
        out_specs=[pl.BlockSpec((SEQ, A_Q), lambda b: (b, 0)), pl.BlockSpec((SEQ, B_W), lambda b: (b, 0)),
                   kv_spec(A_KV_HEADS), kv_spec(A_KV_HEADS), kv_spec(B_HEADS), kv_spec(B_HEADS)],
        out_shape=[jax.ShapeDtypeStruct((T_CTX, A_Q), F32), jax.ShapeDtypeStruct((T_CTX, B_W), F32),
                   kv_sd(A_KV_HEADS), kv_sd(A_KV_HEADS), kv_sd(B_HEADS), kv_sd(B_HEADS)],
        compiler_params=_cparams("parallel"),
        name="ctx_attn",
    )(qkv, sink.reshape(1, A_HEADS))
    caches = [t.reshape(BATCH, 1, SEQ, -1, HEAD_DIM) for t in outs[2:]]
    return outs[0], outs[1], *caches


@functools.lru_cache(maxsize=None)
def _rope_tables(width):
    half = HEAD_DIM // 2
    t = np.arange(DEC_SEQ)
    inv = ROPE_BASE ** (-np.arange(0, half, 2, dtype=np.float64) / half)
    ang_r = (t // GRID_W)[:, None] * inv[None, :]
    ang_c = (t % GRID_W)[:, None] * inv[None, :]
    cos = np.concatenate([np.cos(ang_r)] * 2 + [np.cos(ang_c)] * 2, axis=-1)
    sin = np.concatenate([-np.sin(ang_r), np.sin(ang_r), -np.sin(ang_c), np.sin(ang_c)], axis=-1)
    reps = width // HEAD_DIM
    return (np.tile(cos, (1, reps)).astype(np.float32), np.tile(sin, (1, reps)).astype(np.float32))


def _rope_body(q_ref, k_ref, cq_ref, sq_ref, ck_ref, sk_ref, qo_ref, ko_ref):
    quarter = HEAD_DIM // 4

    def rot(x, cos, sin):
        w = x.shape[-1]
        lane = lax.broadcasted_iota(jnp.int32, x.shape, 1)
        fwd = pltpu.roll(x, w - quarter, axis=1)
        bwd = pltpu.roll(x, quarter, axis=1)
        partner = jnp.where((lane & (2 * quarter - 1)) < quarter, fwd, bwd)
        return x * cos + partner * sin

    qo_ref[...] = rot(q_ref[...], cq_ref[...], sq_ref[...])
    ko_ref[...] = rot(k_ref[...], ck_ref[...], sk_ref[...])


def _rope(qkv):
    cq, sq = _rope_tables(A_Q)
    ck, sk = _rope_tables(A_KV)
    tab = lambda w: pl.BlockSpec((DEC_SEQ, w), lambda b: (0, 0))
    row0 = T_CTX // DEC_SEQ
    return pl.pallas_call(
        _rope_body,
        grid=(DEC_BATCH,),
        in_specs=[pl.BlockSpec((DEC_SEQ, A_Q), lambda b: (row0 + b, 0)),
                  pl.BlockSpec((DEC_SEQ, A_KV), lambda b: (row0 + b, A_Q // A_KV)),
                  tab(A_Q), tab(A_Q), tab(A_KV), tab(A_KV)],
        out_specs=[pl.BlockSpec((DEC_SEQ, A_Q), lambda b: (b, 0)),
                   pl.BlockSpec((DEC_SEQ, A_KV), lambda b: (b, 0))],
        out_shape=[jax.ShapeDtypeStruct((T_LAT, A_Q), F32), jax.ShapeDtypeStruct((T_LAT, A_KV), F32)],
        compiler_params=_cparams("parallel"),
        name="rope",
    )(qkv, qkv, jnp.asarray(cq), jnp.asarray(sq), jnp.asarray(ck), jnp.asarray(sk))


WIN_QB = 128


def _win_attn_body(qraw_ref, qrot_ref, krot_ref, v_ref, kc_ref, vc_ref, sink_ref, o_ref):
    scale = HEAD_DIM ** -0.5
    hk = pl.program_id(1)
    tiles = A_GROUP // HEADS_PER_TILE

    def kv_in_half(x):
        swapped = pltpu.roll(x, HEAD_DIM, axis=1)
        return [jnp.where(hk == p, x, swapped) for p in range(HEADS_PER_TILE)]

    k, v, kc, vc = kv_in_half(krot_ref[...]), kv_in_half(v_ref[...]), kv_in_half(kc_ref[0]), kv_in_half(vc_ref[0])
    head_lane = lax.broadcasted_iota(jnp.int32, (1, A_HEADS), 1)

    def sink_rows(p):
        heads = [hk * A_GROUP + j * HEADS_PER_TILE + p for j in range(tiles)]
        vals = [jnp.sum(jnp.where(head_lane == h, sink_ref[...], 0.0), axis=-1, keepdims=True) for h in heads]
        return jnp.concatenate([jnp.broadcast_to(s, (WIN_QB, 1)) for s in vals], axis=0)

    sinks = [sink_rows(p) for p in range(HEADS_PER_TILE)]
    lane = lax.broadcasted_iota(jnp.int32, (tiles * WIN_QB, LANES), 1)
    in_half = [lane < HEAD_DIM, lane >= HEAD_DIM]
    for qb in range(DEC_SEQ // WIN_QB):
        q0 = qb * WIN_QB
        rows = slice(q0, q0 + WIN_QB)
        lo = max(0, q0 - WINDOW)
        hi = min(DEC_SEQ, q0 + WIN_QB + WINDOW)
        q_rot = jnp.concatenate([qrot_ref[rows, j * LANES:(j + 1) * LANES] for j in range(tiles)], axis=0)
        q_raw = jnp.concatenate([qraw_ref[rows, j * LANES:(j + 1) * LANES] for j in range(tiles)], axis=0)
        halves = []
        for p in range(HEADS_PER_TILE):
            s_loc = _dot_nt(jnp.where(in_half[p], q_rot, 0.0), k[p][lo:hi]) * scale
            qpos = q0 + (lax.broadcasted_iota(jnp.int32, s_loc.shape, 0) & (WIN_QB - 1))
            kpos = lo + lax.broadcasted_iota(jnp.int32, s_loc.shape, 1)
            s_loc = jnp.where(jnp.abs(kpos - qpos) <= WINDOW, s_loc, MASK_NEG)
            s_ctx = _dot_nt(jnp.where(in_half[p], q_raw, 0.0), kc[p]) * scale
            m = jnp.maximum(jnp.maximum(jnp.max(s_loc, axis=-1, keepdims=True),
                                        jnp.max(s_ctx, axis=-1, keepdims=True)), sinks[p])
            p_loc = jnp.exp(s_loc - m)
            p_ctx = jnp.exp(s_ctx - m)
            den = (jnp.sum(p_loc, axis=-1, keepdims=True) + jnp.sum(p_ctx, axis=-1, keepdims=True)
                   + jnp.exp(sinks[p] - m))
            halves.append((_dot(p_ctx, vc[p]) + _dot(p_loc, v[p][lo:hi])) / den)
        o = jnp.where(in_half[0], halves[0], halves[1])
        for j in range(tiles):
            o_ref[rows, j * LANES:(j + 1) * LANES] = o[j * WIN_QB:(j + 1) * WIN_QB]


def _win_attn(qkv, q_rot, k_rot, kc, vc, sink):
    row0 = T_CTX // DEC_SEQ
    gw = A_GROUP * HEAD_DIM
    return pl.pallas_call(
        _win_attn_body,
        grid=(DEC_BATCH, A_KV_HEADS),
        in_specs=[pl.BlockSpec((DEC_SEQ, gw), lambda b, h: (row0 + b, h)),
                  pl.BlockSpec((DEC_SEQ, gw), lambda b, h: (b, h)),
                  pl.BlockSpec((DEC_SEQ, A_KV), lambda b, h: (b, 0)),
                  pl.BlockSpec((DEC_SEQ, A_KV), lambda b, h: (row0 + b, (A_Q + A_KV) // A_KV)),
                  pl.BlockSpec((1, PAST_LEN, A_KV), lambda b, h: (b, 0, 0)),
                  pl.BlockSpec((1, PAST_LEN, A_KV), lambda b, h: (b, 0, 0)),
                  pl.BlockSpec((1, A_HEADS), lambda b, h: (0, 0))],
        out_specs=pl.BlockSpec((DEC_SEQ, gw), lambda b, h: (b, h)),
        out_shape=jax.ShapeDtypeStruct((T_LAT, A_Q), F32),
        compiler_params=_cparams("parallel", "parallel"),
        name="win_attn",
    )(qkv, q_rot, k_rot, qkv, kc, vc, sink.reshape(1, A_HEADS))


GRID_ROWS = DEC_SEQ // GRID_W
NA_BAND = min(NA_ROWS, GRID_ROWS)


NA_REL_ROWS = 2 * NA_ROWS - 1
NA_REL_COLS = 2 * NA_COLS - 1
LANES = 128
HEADS_PER_TILE = LANES // HEAD_DIM


def _na_rel_rows(rpb):
    pad = jnp.zeros((B_HEADS, NA_REL_ROWS, GRID_W - NA_REL_COLS), F32)
    one = jnp.concatenate([rpb, pad], axis=-1)
    nxt = jnp.concatenate([one[:, 1:], jnp.zeros((B_HEADS, 1, GRID_W), F32)], axis=1)
    both = jnp.concatenate([one, nxt], axis=-1)
    return jnp.concatenate([both, jnp.zeros((B_HEADS, 16 - NA_REL_ROWS, LANES), F32)], axis=1)


NA_HEADS_PER_STEP = LANES // HEAD_DIM


def _na_row_groups():
    groups = []
    for r in range(GRID_ROWS):
        rs = min(max(r - NA_ROWS // 2, 0), GRID_ROWS - NA_BAND)
        if groups and groups[-1][2] == rs:
            groups[-1][1] += 1
        else:
            groups.append([r, 1, rs])
    return groups


def _na_attn_body(q_ref, k_ref, v_ref, kc_ref, vc_ref, rel_ref, o_ref):
    scale = HEAD_DIM ** -0.5
    cq = lax.broadcasted_iota(jnp.int32, (GRID_W, LANES), 0)
    kcol = lax.broadcasted_iota(jnp.int32, (GRID_W, LANES), 1) & (GRID_W - 1)
    cs = jnp.clip(cq - NA_COLS // 2, 0, GRID_W - NA_COLS)
    col_ok = (kcol >= cs) & (kcol < cs + NA_COLS)
    kc = kc_ref[0]
    vc = vc_ref[0]
    tiles = {}

    def pair_tile(hh, a):
        if (hh, a) not in tiles:
            x = jnp.broadcast_to(rel_ref[hh, a:a + 1, :], (GRID_W, LANES))
            t = pltpu.roll(x, LANES - (NA_COLS - 1), axis=1, stride=1, stride_axis=0)
            tiles[hh, a] = jnp.where(col_ok, t, MASK_NEG)
        return tiles[hh, a]

    for r0, n_r, rs in _na_row_groups():
        rows = slice(r0 * GRID_W, (r0 + n_r) * GRID_W)
        band = slice(rs * GRID_W, (rs + NA_BAND) * GRID_W)
        q_t, k_t, v_t = q_ref[rows, :], k_ref[band, :], v_ref[band, :]
        head_of_lane = lax.broadcasted_iota(jnp.int32, q_t.shape, 1) >> (HEAD_DIM.bit_length() - 1)
        o = jnp.zeros(q_t.shape, F32)
        for hh in range(NA_HEADS_PER_STEP):
            bias = jnp.concatenate(
                [jnp.concatenate([pair_tile(hh, rs - r + NA_ROWS - 1 + 2 * i) for i in range(NA_BAND // 2)], axis=1)
                 for r in range(r0, r0 + n_r)], axis=0)
            q = jnp.where(head_of_lane == hh, q_t, 0.0)
            s_loc = _dot_nt(q, k_t) * scale + bias
            s_ctx = _dot_nt(q, kc) * scale
            m = jnp.maximum(jnp.max(s_loc, axis=-1, keepdims=True), jnp.max(s_ctx, axis=-1, keepdims=True))
            p_loc = jnp.exp(s_loc - m)
            p_ctx = jnp.exp(s_ctx - m)
            den = jnp.sum(p_loc, axis=-1, keepdims=True) + jnp.sum(p_ctx, axis=-1, keepdims=True)
            o = jnp.where(head_of_lane == hh, (_dot(p_ctx, vc) + _dot(p_loc, v_t)) / den, o)
        o_ref[rows, :] = o


def _na_attn(qkv, kc, vc, rel):
    row0 = T_CTX // DEC_SEQ
    col0 = (A_Q + 2 * A_KV) // LANES
    n_blk = B_W // LANES
    col = lambda j: pl.BlockSpec((DEC_SEQ, LANES), lambda b, p: (row0 + b, col0 + j * n_blk + p))
    cache = pl.BlockSpec((1, PAST_LEN, LANES), lambda b, p: (b, 0, p))
    return pl.pallas_call(
        _na_attn_body,
        grid=(DEC_BATCH, n_blk),
        in_specs=[col(0), col(1), col(2), cache, cache,
                  pl.BlockSpec((NA_HEADS_PER_STEP, 16, LANES), lambda b, p: (p, 0, 0))],
        out_specs=pl.BlockSpec((DEC_SEQ, LANES), lambda b, p: (b, p)),
        out_shape=jax.ShapeDtypeStruct((T_LAT, B_W), F32),
        compiler_params=_cparams("parallel", "parallel"),
        name="na_attn",
    )(qkv, qkv, qkv, kc, vc, rel)


@functools.lru_cache(maxsize=None)
def _dft_mats(L):
    n = 2 * L
    fc = min(L, DFT_CHUNK)
    f = np.arange(L)[:, None]
    t = np.arange(L)[None, :]
    ang = 2.0 * np.pi * ((f * t) % n) / n
    m1 = np.cos(ang)
    m2 = np.sin(ang)
    m2[0, :] = np.where(np.arange(L) % 2 == 0, 1.0, -1.0)
    wgt = np.full((L, 1), 2.0)
    wgt[0, 0] = 1.0
    nch = L // fc
    fwd = np.concatenate([m1.reshape(nch, fc, L), m2.reshape(nch, fc, L)], axis=1)
    inv = np.concatenate([(m1 * wgt / n).reshape(nch, fc, L), (m2 * wgt / n).reshape(nch, fc, L)], axis=1)
    inv = np.transpose(inv, (0, 2, 1))
    return fwd.astype(np.float32), inv.astype(np.float32)


@functools.lru_cache(maxsize=None)
def _filter_consts(L):
    t = np.linspace(0.0, 1.0, L)[:, None]
    bands = (C_EMB - 1) // 2
    ang = (2.0 * math.pi / L) * np.arange(L)[:, None] * np.linspace(1e-4, bands - 1, bands)[None, :]
    z = np.concatenate([t, np.cos(ang), -np.sin(ang)], axis=-1)
    zpad = np.zeros((L, 128))
    zpad[:, :C_EMB] = z
    deltas = np.abs(np.linspace(HYENA_MIN_DECAY, HYENA_MAX_DECAY, C_DIM))
    window = np.exp(-t * deltas[None, :])
    return zpad.astype(np.float32), window.astype(np.float32)


def _filter_body(z_ref, w1_ref, b1_ref, w2_ref, b2_ref, w3_ref, b3_ref, fr_ref, w4_ref, win_ref, fm_ref,
                 hr_ref, g_ref, hq_ref, hs_scr, hd_scr):
    c = pl.program_id(0)
    fc = hr_ref.shape[0]

    @pl.when(c == 0)
    def _():
        fr = fr_ref[...]
        hh = jnp.sin(fr * (_dot_hi(z_ref[...], w1_ref[...]) + b1_ref[...]))
        hh = jnp.sin(fr * (_dot_hi(hh, w2_ref[...]) + b2_ref[...]))
        hh = jnp.sin(fr * (_dot_hi(hh, w3_ref[...]) + b3_ref[...]))
        hh = _dot_hi(hh, w4_ref[...])
        hf = hh[:, :C_DIM] * win_ref[...]
        hb = hh[:, C_DIM:] * win_ref[...]
        hs_scr[...] = hf + hb
        hd_scr[...] = hf - hb

    fm = fm_ref[0]
    hr = _dot_split(fm[:fc], hs_scr[...])
    first = (lax.broadcasted_iota(jnp.int32, (fc, C_DIM), 0) == 0) & (c == 0)
    hr_ref[...] = hr
    g_ref[...] = jnp.where(first, 0.0, _dot_split(fm[fc:], hd_scr[...]))
    hs = hs_scr[...]
    sign = jnp.where((lax.broadcasted_iota(jnp.int32, hs.shape, 0) & 1) == 0, 1.0, -1.0)
    hq_ref[...] = jnp.where(first, jnp.sum(hs * sign, axis=0, keepdims=True), hr)


def _hyena_filter(L, filt):
    w1, b1, w2, b2, w3, b3, freq, w4 = filt
    zpad, window = _filter_consts(L)
    fwd, _ = _dft_mats(L)
    nch, fc2, _ = fwd.shape
    fc = fc2 // 2
    w1p = jnp.pad(w1, ((0, 128 - C_EMB), (0, 0)))
    full = lambda shape: pl.BlockSpec(shape, lambda c: tuple(0 for _ in shape))
    out_spec = pl.BlockSpec((fc, C_DIM), lambda c: (c, 0))
    out_sd = jax.ShapeDtypeStruct((L, C_DIM), F32)
    return pl.pallas_call(
        _filter_body,
        grid=(nch,),
        in_specs=[full((L, 128)), full((128, C_FFN)), full((1, C_FFN)), full((C_FFN, C_FFN)), full((1, C_FFN)),
                  full((C_FFN, C_FFN)), full((1, C_FFN)), full((1, C_FFN)), full((C_FFN, 2 * C_DIM)),
                  full((L, C_DIM)), pl.BlockSpec((1, fc2, L), lambda c: (c, 0, 0))],
        out_specs=[out_spec, out_spec, out_spec],
        out_shape=[out_sd, out_sd, out_sd],
        scratch_shapes=[pltpu.VMEM((L, C_DIM), F32), pltpu.VMEM((L, C_DIM), F32)],
        compiler_params=_cparams("arbitrary"),
        name="hyena_filter",
    )(jnp.asarray(zpad), w1p, b1.reshape(1, C_FFN), w2, b2.reshape(1, C_FFN), w3, b3.reshape(1, C_FFN),
      freq.reshape(1, C_FFN), w4, jnp.asarray(window), jnp.asarray(fwd))


def _hyena_body(u_ref, cw_ref, cb_ref, d_ref, fm_ref, fi_ref, hr_ref, g_ref, hq_ref, y_ref,
                x0_scr, z_scr, acc_scr):
    c = pl.program_id(1)
    L = y_ref.shape[0]
    fc = hr_ref.shape[0]

    @pl.when(c == 0)
    def _():
        row = lax.broadcasted_iota(jnp.int32, (L, C_DIM), 0)

        def short_conv(sec):
            cols = slice(sec * C_DIM, (sec + 1) * C_DIM)
            u = u_ref[:, cols]
            prev = jnp.where(row == 0, 0.0, pltpu.roll(u, 1, axis=0))
            nxt = jnp.where(row == L - 1, 0.0, pltpu.roll(u, L - 1, axis=0))
            return (prev * cw_ref[0:1, cols] + u * cw_ref[1:2, cols] + nxt * cw_ref[2:3, cols]
                    + cb_ref[:, cols])

        x0_scr[...] = short_conv(0)
        z_scr[...] = short_conv(1) * short_conv(2)
        acc_scr[...] = jnp.zeros((L, C_DIM), F32)

    ab = _dot(fm_ref[0], z_scr[...])
    a, b = ab[:fc], ab[fc:]
    hr, g, hq = hr_ref[...], g_ref[...], hq_ref[...]
    pq = jnp.concatenate([a * hr - b * g, a * g + b * hq], axis=0)
    acc_scr[...] += _dot(fi_ref[0], pq)

    @pl.when(c == pl.num_programs(1) - 1)
    def _():
        y_ref[...] = x0_scr[...] * (acc_scr[...] + z_scr[...] * d_ref[...])


def _hyena(u, row_blk0, n_seq, L, conv_w, conv_b, d_skip, spec):
    hr, g, hq = spec
    fwd, inv = _dft_mats(L)
    nch, fc2, _ = fwd.shape
    fc = fc2 // 2
    u_w = 3 * C_DIM
    return pl.pallas_call(
        _hyena_body,
        grid=(n_seq, nch),
        in_specs=[pl.BlockSpec((L, u_w), lambda b, c: (row_blk0 + b, 0)),
                  pl.BlockSpec((3, u_w), lambda b, c: (0, 0)),
                  pl.BlockSpec((1, u_w), lambda b, c: (0, 0)),
                  pl.BlockSpec((1, C_DIM), lambda b, c: (0, 0)),
                  pl.BlockSpec((1, fc2, L), lambda b, c: (c, 0, 0)),
                  pl.BlockSpec((1, L, fc2), lambda b, c: (c, 0, 0)),
                  pl.BlockSpec((fc, C_DIM), lambda b, c: (c, 0)),
                  pl.BlockSpec((fc, C_DIM), lambda b, c: (c, 0)),
                  pl.BlockSpec((fc, C_DIM), lambda b, c: (c, 0))],
        out_specs=pl.BlockSpec((L, C_DIM), lambda b, c: (b, 0)),
        out_shape=jax.ShapeDtypeStruct((n_seq * L, C_DIM), F32),
        scratch_shapes=[pltpu.VMEM((L, C_DIM), F32)] * 3,
        compiler_params=_cparams("parallel", "arbitrary"),
        name="hyena",
    )(u, conv_w, conv_b.reshape(1, u_w), d_skip.reshape(1, C_DIM), jnp.asarray(fwd), jnp.asarray(inv), hr, g, hq)


def _hgrn_body(q_ref, ff_ref, fb_ref, i_ref, g_ref, lbf_ref, lbb_ref, nd_ref, s0f_ref, s0b_ref,
               o_ref, sf_ref, sb_ref, *, layer):
    L = o_ref.shape[0]
    C = GLA_CHUNK
    S = min(L // GLA_MIN_SPANS, GLA_SPAN)
    nc = S // C
    n_span = L // S
    mid = C // 2
    def lower_bound(gm):
        e = jnp.exp(gm - jnp.max(gm, axis=0, keepdims=True))
        p = e / jnp.sum(e, axis=0, keepdims=True)
        return jnp.sum(p[0:layer + 1], axis=0, keepdims=True) - p[0:1]

    def gates(fx, lb):
        f = lb + (1.0 - lb) * jax.nn.sigmoid(fx)
        return 1.0 - f, jnp.log(f)


    chunk_shift = C.bit_length() - 1
    block_shift = D_KDIM.bit_length() - 1
    ti = lax.broadcasted_iota(jnp.int32, (S, S), 0)
    si = lax.broadcasted_iota(jnp.int32, (S, S), 1)
    same_chunk = (ti >> chunk_shift) == (si >> chunk_shift)
    causal = same_chunk & (si <= ti)
    anti = same_chunk & (si >= ti)
    row_chunk = lax.broadcasted_iota(jnp.int32, (S, nc * D_KDIM), 0) >> chunk_shift
    col_chunk = lax.broadcasted_iota(jnp.int32, (S, nc * D_KDIM), 1) >> block_shift
    own_block = row_chunk == col_chunk

    def spread(x):
        return jnp.where(own_block, jnp.concatenate([x] * nc, axis=1), 0.0)

    def chunk_cumsum(mask, lg):
        tri = mask.astype(BF16)
        hi = lg.astype(BF16)
        r1 = lg - hi.astype(F32)
        mid_t = r1.astype(BF16)
        lo = (r1 - mid_t.astype(F32)).astype(BF16)
        dot = lambda t: jnp.dot(tri, t, preferred_element_type=F32)
        return dot(hi) + dot(mid_t) + dot(lo)

    def per_chunk_rows(b, pos):
        return jnp.concatenate([jnp.broadcast_to(b[n * C + pos:n * C + pos + 1], (C, D_KDIM)) for n in range(nc)],
                               axis=0)

    def one_head(q, v, kf, lgf, kb, lgb, st_f, st_b):
        local = []
        for u in range(n_span):
            rows = slice(u * S, (u + 1) * S)
            qs, vs, kfs, kbs = q[rows], v[rows], kf[rows], kb[rows]
            lgs = jnp.concatenate([lgf[rows], lgb[rows]], axis=1)
            pre = chunk_cumsum(causal, lgs)
            b_f = pre[:, :D_KDIM]
            pre_b = pre[:, D_KDIM:]
            b_b = per_chunk_rows(pre_b, C - 1) - pre_b + lgb[rows]
            ref_f, ref_b = per_chunk_rows(b_f, mid), per_chunk_rows(b_b, mid)
            sc = (jnp.where(causal, _dot_nt(qs * jnp.exp(b_f - ref_f), kfs * jnp.exp(ref_f - b_f)), 0.0)
                  + jnp.where(anti, _dot_nt(qs * jnp.exp(b_b - ref_b), kbs * jnp.exp(ref_b - b_b)), 0.0))
            k_out = jnp.concatenate([kfs * jnp.exp(per_chunk_rows(b_f, C - 1) - b_f),
                                     kbs * jnp.exp(per_chunk_rows(b_b, 0) - b_b)], axis=1)
            kv_t = _dot_tn(spread(vs), k_out)
            local.append((_dot(sc, vs), kv_t, b_f, b_b, qs))

        states_f = [[None] * nc for _ in range(n_span)]
        for u in range(n_span):
            _, kv_t, b_f, _, _ = local[u]
            for n in range(nc):
                states_f[u][n] = st_f
                st_f = st_f * jnp.exp(b_f[n * C + C - 1:n * C + C]) + kv_t[n * D_VDIM:(n + 1) * D_VDIM, :D_KDIM]
        states_b = [[None] * nc for _ in range(n_span)]
        for u in reversed(range(n_span)):
            _, kv_t, _, b_b, _ = local[u]
            for n in reversed(range(nc)):
                states_b[u][n] = st_b
                st_b = st_b * jnp.exp(b_b[n * C:n * C + 1]) + kv_t[n * D_VDIM:(n + 1) * D_VDIM, D_KDIM:]

        outs = []
        for u in range(n_span):
            intra, _, b_f, b_b, qs = local[u]
            q_in = jnp.concatenate([spread(qs * jnp.exp(b_f)), spread(qs * jnp.exp(b_b))], axis=1)
            outs.append(intra + _dot_nt(q_in, jnp.concatenate(states_f[u] + states_b[u], axis=1)))
        return (jnp.concatenate(outs, axis=0) if n_span > 1 else outs[0]), st_f, st_b

    for hh in range(o_ref.shape[1] // D_VDIM):
        cols = slice(hh * D_KDIM, (hh + 1) * D_KDIM)
        kf, lgf = gates(ff_ref[:, cols], lower_bound(lbf_ref[:, cols]))
        kb, lgb = gates(fb_ref[:, cols], lower_bound(lbb_ref[:, cols]))
        o, st_f, st_b = one_head(_silu(q_ref[:, cols]), i_ref[:, cols], kf, lgf, kb, lgb,
                                 jnp.transpose(s0f_ref[0, hh]), jnp.transpose(s0b_ref[0, hh]))
        sf_ref[0, hh] = jnp.transpose(st_f)
        sb_ref[0, hh] = jnp.transpose(st_b)
        o_ref[:, cols] = _rms(o, nd_ref[...]) * _silu(g_ref[:, cols])


def _hgrn(u, row_blk0, n_seq, L, lb_fwd, lb_bwd, norm_d, s0f, s0b, layer):
    hps = HGRN_HEADS_PER_STEP
    width = hps * D_KDIM
    col0 = 3 * C_DIM // width
    groups = D_HEADS // hps
    col = lambda j: pl.BlockSpec((L, width), lambda b, h: (row_blk0 + b, col0 + j * groups + h))
    lbs = pl.BlockSpec((DEPTH, width), lambda b, h: (0, h))
    st = pl.BlockSpec((1, hps, D_KDIM, D_VDIM), lambda b, h: (b, h, 0, 0))
    st_sd = jax.ShapeDtypeStruct((n_seq, D_HEADS, D_KDIM, D_VDIM), F32)
    return pl.pallas_call(
        functools.partial(_hgrn_body, layer=layer),
        grid=(n_seq, groups),
        in_specs=[col(0), col(1), col(2), col(3), col(4), lbs, lbs,
                  pl.BlockSpec((1, D_VDIM), lambda b, h: (0, 0)), st, st],
        out_specs=[pl.BlockSpec((L, width), lambda b, h: (b, h)), st, st],
        out_shape=[jax.ShapeDtypeStruct((n_seq * L, D_HEADS * D_VDIM), F32), st_sd, st_sd],
        compiler_params=_cparams("parallel", "parallel"),
        name="hgrn",
    )(u, u, u, u, u, lb_fwd, lb_bwd, norm_d.reshape(1, D_VDIM), s0f, s0b)


def _pack_bf16_pairs(h):
    n = h.shape[1] // 2
    hi = lax.bitcast_convert_type(h[:, :n].astype(BF16).astype(F32), jnp.int32)
    lo = lax.bitcast_convert_type(h[:, n:].astype(BF16).astype(F32), jnp.int32)
    return hi | lax.shift_right_logical(lo, 16)


def _unpack_bf16_pairs(p):
    hi = lax.bitcast_convert_type(p & jnp.int32(-65536), F32).astype(BF16)
    lo = lax.bitcast_convert_type(lax.shift_left(p, 16), F32).astype(BF16)
    return hi, lo


def _outproj_body(*refs, n_x):
    a_refs, b_refs, x_refs = refs[0:2], refs[2:4], refs[4:4 + n_x]
    mod_ref, gf_ref, w_ref, wrh_ref, wrl_ref, rb_ref, x1_ref, h2_ref, chosen_ref, gk_ref, ik_ref = refs[4 + n_x:]
    m = mod_ref[0]
    half = a_refs[0].shape[1]
    out = _dot(_token_tile(a_refs), w_ref[0:half, :]) + _dot(_token_tile(b_refs), w_ref[half:, :])
    x1 = _token_tile(x_refs) + m[:, 2 * D_MODEL:3 * D_MODEL] * out
    x1_ref[...] = x1
    h2 = _rms(x1, gf_ref[...]) * (1.0 + m[:, 4 * D_MODEL:5 * D_MODEL]) + m[:, 3 * D_MODEL:4 * D_MODEL]
    h2_ref[...] = _pack_bf16_pairs(h2)
    h_hi = h2.astype(BF16)
    h_lo = (h2 - h_hi.astype(F32)).astype(BF16)
    logits = _dot_nt(wrh_ref[...], h_hi) + _dot_nt(wrh_ref[...], h_lo) + _dot_nt(wrl_ref[...], h_hi)
    scores = jax.nn.sigmoid(logits)
    work = scores + rb_ref[...]
    expert = lax.broadcasted_iota(jnp.int32, work.shape, 0).astype(F32)
    slot = lax.broadcasted_iota(jnp.int32, (TOP_K, work.shape[1]), 0)
    chosen = [jnp.zeros(work.shape, F32) for _ in range(MOE_REGIONS)]
    gk = jnp.zeros((TOP_K, work.shape[1]), F32)
    ik = jnp.zeros((TOP_K, work.shape[1]), F32)
    for k in range(TOP_K):
        best = jnp.max(work, axis=0, keepdims=True)
        first = jnp.min(jnp.where(work == best, expert, float(N_EXPERTS)), axis=0, keepdims=True)
        hit = expert == first
        chosen[k // K_PER_REGION] = jnp.where(hit, 1.0, chosen[k // K_PER_REGION])
        gk = jnp.where(slot == k, jnp.sum(jnp.where(hit, scores, 0.0), axis=0, keepdims=True), gk)
        ik = jnp.where(slot == k, first, ik)
        work = jnp.where(hit, -jnp.inf, work)
    for r in range(MOE_REGIONS):
        chosen_ref[r] = chosen[r]
    gk_ref[...] = jnp.transpose(gk / jnp.sum(gk, axis=0, keepdims=True) * ROUTE_SCALE)
    ik_ref[...] = ik


def _outproj(a, b, x, mod_l, gain_ffn, w_out, w_router, router_bias):
    half = a[0].shape[1]
    a_specs, a_args = _token_specs(a, half)
    b_specs, b_args = _token_specs(b, half)
    x_specs, x_args = _token_specs(x, D_MODEL)
    wr_t = w_router.T
    wr_hi = wr_t.astype(BF16)
    wr_lo = (wr_t - wr_hi.astype(F32)).astype(BF16)
    return pl.pallas_call(
        functools.partial(_outproj_body, n_x=len(x_args)),
        grid=(T_ALL // TM,),
        in_specs=a_specs + b_specs + x_specs + [
                  pl.BlockSpec((1, 1, N_MOD * D_MODEL), lambda i: (_mod_row(i), 0, 0)),
                  pl.BlockSpec((1, D_MODEL), lambda i: (0, 0)),
                  pl.BlockSpec((2 * half, D_MODEL), lambda i: (0, 0)),
                  pl.BlockSpec((N_EXPERTS, D_MODEL), lambda i: (0, 0)),
                  pl.BlockSpec((N_EXPERTS, D_MODEL), lambda i: (0, 0)),
                  pl.BlockSpec((N_EXPERTS, 1), lambda i: (0, 0))],
        out_specs=[pl.BlockSpec((TM, D_MODEL), lambda i: (i, 0)),
                   pl.BlockSpec((TM, D_MODEL // 2), lambda i: (i, 0)),
                   pl.BlockSpec((MOE_REGIONS, N_EXPERTS, TM), lambda i: (0, 0, i)),
                   pl.BlockSpec((TM, TOP_K), lambda i: (i, 0)),
                   pl.BlockSpec((TOP_K, TM), lambda i: (0, i))],
        out_shape=[jax.ShapeDtypeStruct((T_ALL, D_MODEL), F32),
                   jax.ShapeDtypeStruct((T_ALL, D_MODEL // 2), jnp.int32),
                   jax.ShapeDtypeStruct((MOE_REGIONS, N_EXPERTS, T_ALL), F32),
                   jax.ShapeDtypeStruct((T_ALL, TOP_K), F32),
                   jax.ShapeDtypeStruct((TOP_K, T_ALL), F32)],
        compiler_params=_cparams("parallel"),
        name="outproj_router",
    )(*a_args, *b_args, *x_args, mod_l, gain_ffn.reshape(1, D_MODEL), w_out, wr_hi, wr_lo,
      router_bias.reshape(N_EXPERTS, 1))


def _route_body(chosen_ref, ik_ref, dest_ref, first_ref, count_ref, short_ref, pos_scr):
    n_tiles = T_ALL // TM
    r = lax.broadcasted_iota(jnp.int32, (TM, TM), 0)
    c = lax.broadcasted_iota(jnp.int32, (TM, TM), 1)
    before = (r < c).astype(BF16)

    counts = jnp.zeros((N_EXPERTS, 1), F32)
    for i in range(n_tiles):
        cols = slice(i * TM, (i + 1) * TM)
        m = chosen_ref[0, :, cols]
        pos_scr[:, cols] = jnp.dot(m.astype(BF16), before, preferred_element_type=F32) + counts
        counts = counts + jnp.sum(m, axis=1, keepdims=True)
    padded = jnp.ceil(counts * (1.0 / MOE_BLK)) * MOE_BLK
    ei = lax.broadcasted_iota(jnp.int32, (N_EXPERTS, N_EXPERTS), 0)
    ej = lax.broadcasted_iota(jnp.int32, (N_EXPERTS, N_EXPERTS), 1)
    end = _dot_hi((ej <= ei).astype(F32), jnp.broadcast_to(padded, (N_EXPERTS, LANES)))[:, 0:1]
    start = end - padded

    expert = lax.broadcasted_iota(jnp.int32, (N_EXPERTS, TM), 0).astype(F32)
    slot = lax.broadcasted_iota(jnp.int32, (K_PER_REGION, TM), 0)
    for i in range(n_tiles):
        cols = slice(i * TM, (i + 1) * TM)
        row_of = pos_scr[:, cols] + start
        ik = ik_ref[0, :, cols]
        acc = jnp.zeros((K_PER_REGION, TM), F32)
        for k in range(K_PER_REGION):
            pick = jnp.sum(jnp.where(expert == ik[k:k + 1, :], row_of, 0.0), axis=0, keepdims=True)
            acc = jnp.where(slot == k, pick, acc)
        dest_ref[0, :, cols] = acc.astype(jnp.int32)
    first_ref[0] = jnp.broadcast_to(start * (1.0 / MOE_BLK), (N_EXPERTS, LANES)).astype(jnp.int32)
    count_ref[0] = jnp.broadcast_to(padded * (1.0 / MOE_BLK), (N_EXPERTS, LANES)).astype(jnp.int32)
    in_last = counts - (padded - MOE_BLK)
    short = jnp.where((counts > 0.0) & (in_last <= MOE_BLK // 2), 1.0, 0.0)
    short_ref[0] = jnp.broadcast_to(short, (N_EXPERTS, LANES)).astype(jnp.int32)


def _route(chosen, ik):
    per_region = lambda rows, cols: pl.BlockSpec((1, rows, cols), lambda r: (r, 0, 0))
    table = jax.ShapeDtypeStruct((MOE_REGIONS, N_EXPERTS, LANES), jnp.int32)
    return pl.pallas_call(
        _route_body,
        grid=(MOE_REGIONS,),
        in_specs=[per_region(N_EXPERTS, T_ALL), per_region(K_PER_REGION, T_ALL)],
        out_specs=[per_region(K_PER_REGION, T_ALL)] + [per_region(N_EXPERTS, LANES)] * 3,
        out_shape=[jax.ShapeDtypeStruct((MOE_REGIONS, K_PER_REGION, T_ALL), jnp.int32), table, table, table],
        scratch_shapes=[pltpu.VMEM((N_EXPERTS, T_ALL), F32)],
        compiler_params=_cparams("arbitrary"),
        name="moe_route",
    )(chosen, ik.reshape(MOE_REGIONS, K_PER_REGION, T_ALL))


def _sc_worker_id():
    return lax.axis_index("s") * SC_CORES + lax.axis_index("c")


def _sc_dispatch(h2p, dest):
    n_chunks = T_ALL // DISP_CHUNK
    k_per = dest.shape[0] // DISP_SPLIT
    items_per_worker = n_chunks * DISP_SPLIT // SC_WORKERS
    chunk_stride = SC_WORKERS // DISP_SPLIT
    width = h2p.shape[1]
    mesh = plsc.VectorSubcoreMesh(core_axis_name="c", subcore_axis_name="s")

    @functools.partial(
        pl.kernel, mesh=mesh,
        out_type=jax.ShapeDtypeStruct((MOE_ROWS, width), jnp.int32),
        scratch_types=[pltpu.VMEM((k_per, DISP_CHUNK), jnp.int32), pltpu.VMEM((DISP_CHUNK, width), jnp.int32),
                       pltpu.SemaphoreType.DMA],
    )
    def run(x_hbm, dest_hbm, xs_hbm, idx_v, rows_v, sem):
        wid = _sc_worker_id()
        group = wid % DISP_SPLIT
        for i in range(items_per_worker):
            chunk = i * chunk_stride + wid // DISP_SPLIT
            tokens = pl.ds(pl.multiple_of(chunk * DISP_CHUNK, DISP_CHUNK), DISP_CHUNK)
            pltpu.sync_copy(dest_hbm.at[group, :, tokens], idx_v)
            pltpu.sync_copy(x_hbm.at[tokens], rows_v)
            scatters = [pltpu.make_async_copy(rows_v, xs_hbm.at[idx_v.at[k]], sem) for k in range(k_per)]
            for cp in scatters:
                cp.start()
            for cp in scatters:
                cp.wait()

    return run(h2p, dest.reshape(DISP_SPLIT, k_per, T_ALL))


def _sc_collect(y, dest_flat):
    n_k = dest_flat.shape[0] // T_ALL
    per_worker = T_ALL // SC_WORKERS
    n_chunks = per_worker // COLLECT_CHUNK
    n_steps = n_k * n_chunks
    width = y.shape[1]
    mesh = plsc.VectorSubcoreMesh(core_axis_name="c", subcore_axis_name="s")

    @functools.partial(
        pl.kernel, mesh=mesh,
        out_type=jax.ShapeDtypeStruct((n_k * T_ALL, width), y.dtype),
        scratch_types=[pltpu.VMEM((n_k * per_worker,), jnp.int32),
                       pltpu.VMEM((COLLECT_CHUNK, width), y.dtype), pltpu.VMEM((COLLECT_CHUNK, width), y.dtype),
                       pltpu.SemaphoreType.DMA, pltpu.SemaphoreType.DMA],
    )
    def run(y_hbm, dest_hbm, yg_hbm, idx_v, rows0, rows1, sem0, sem1):
        wid = _sc_worker_id()
        bufs = ((rows0, sem0), (rows1, sem1))
        for k in range(n_k):
            pltpu.sync_copy(dest_hbm.at[pl.ds(k * T_ALL + wid * per_worker, per_worker)],
                            idx_v.at[pl.ds(k * per_worker, per_worker)])

        def gather(step, buf):
            rows, sem = buf
            idx = idx_v.at[pl.ds(pl.multiple_of(step * COLLECT_CHUNK, 8), COLLECT_CHUNK)]
            return pltpu.make_async_copy(y_hbm.at[idx], rows, sem)

        def out_rows(step):
            off = (step // n_chunks) * T_ALL + wid * per_worker + (step % n_chunks) * COLLECT_CHUNK
            return yg_hbm.at[pl.ds(pl.multiple_of(off, 8), COLLECT_CHUNK)]

        gather(0, bufs[0]).start()

        @pl.loop(0, n_steps, step=2)
        def _(base):
            for j in range(2):
                step = base + j

                @pl.when(step + 1 < n_steps)
                def _():
                    gather(step + 1, bufs[1 - j]).start()

                gather(step, bufs[j]).wait()
                pltpu.sync_copy(bufs[j][0], out_rows(step))

    return run(y, dest_flat)


def _expert_body(first_ref, count_ref, short_ref, xs_hbm, wg_hbm, wu_hbm, wd_hbm, y_hbm,
                 wg_f32, wu_f32, wd_f32, wg_bf, wu_bf, wd_bf, x_buf, y_buf, w_sem, in_sem, out_sem, *, layer):
    e = pl.program_id(0)
    first = first_ref[e]
    count = count_ref[e]
    n_used = first_ref[N_EXPERTS - 1] + count_ref[N_EXPERTS - 1]
    half = D_MODEL // 2

    def weight_copies(ex):
        slot = lax.rem(ex, EXPERT_W_SLOTS)
        out = []
        for src, dst in ((wg_hbm, wg_f32), (wu_hbm, wu_f32), (wd_hbm, wd_f32)):
            size = dst.shape[1] // EXPERT_W_PARTS
            for part in range(EXPERT_W_PARTS):
                rows = pl.ds(part * size, size)
                out.append(pltpu.make_async_copy(src.at[layer, ex, rows], dst.at[slot, rows], w_sem.at[slot]))
        return out

    @pl.when(e == 0)
    def _():
        for ahead in range(EXPERT_W_SLOTS - 1):
            for cp in weight_copies(ahead):
                cp.start()

    for cp in weight_copies(e):
        cp.wait()

    @pl.when(e + EXPERT_W_SLOTS - 1 < N_EXPERTS)
    def _():
        for cp in weight_copies(e + EXPERT_W_SLOTS - 1):
            cp.start()

    w_slot = lax.rem(e, EXPERT_W_SLOTS)
    wg_bf[...] = wg_f32[w_slot].astype(BF16)
    wu_bf[...] = wu_f32[w_slot].astype(BF16)
    wd_bf[...] = wd_f32[w_slot].astype(BF16)

    def part_rows(g, part, n_parts):
        size = MOE_BLK // n_parts
        return pl.ds(pl.multiple_of(g * MOE_BLK + part * size, size), size), pl.ds(part * size, size)

    def in_copies(g):
        slot = g & (EXPERT_SLOTS - 1)
        out = []
        for part in range(EXPERT_IN_PARTS):
            src, dst = part_rows(g, part, EXPERT_IN_PARTS)
            out.append(pltpu.make_async_copy(xs_hbm.at[src], x_buf.at[slot, dst], in_sem.at[slot]))
        return out

    def out_copies(g):
        slot = g & (EXPERT_SLOTS - 1)
        out = []
        for part in range(EXPERT_OUT_PARTS):
            dst, src = part_rows(g, part, EXPERT_OUT_PARTS)
            out.append(pltpu.make_async_copy(y_buf.at[slot, src], y_hbm.at[dst], out_sem.at[slot]))
        return out

    @pl.when((first == 0) & (count > 0))
    def _():
        for ahead in range(EXPERT_SLOTS - 1):
            @pl.when(ahead < n_used)
            def _():
                for cp in in_copies(ahead):
                    cp.start()

    def block(b, carry):
        g = first + b
        slot = g & (EXPERT_SLOTS - 1)
        for cp in in_copies(g):
            cp.wait()

        @pl.when(g + EXPERT_SLOTS - 1 < n_used)
        def _():
            for cp in in_copies(g + EXPERT_SLOTS - 1):
                cp.start()

        @pl.when(g >= EXPERT_SLOTS)
        def _():
            for cp in out_copies(g - EXPERT_SLOTS):
                cp.wait()

        def ffn(n_rows):
            hi, lo = _unpack_bf16_pairs(x_buf[slot, 0:n_rows])

            def proj(w_bf):
                return (jnp.dot(hi, w_bf[0:half, :], preferred_element_type=F32)
                        + jnp.dot(lo, w_bf[half:, :], preferred_element_type=F32))

            hid = _silu(proj(wg_bf)) * proj(wu_bf)
            y_buf[slot, 0:n_rows] = _pack_bf16_pairs(
                jnp.dot(hid.astype(BF16), wd_bf[...], preferred_element_type=F32))

        short = (b == count - 1) & (short_ref[e] == 1)

        @pl.when(short)
        def _():
            ffn(MOE_BLK // 2)
            y_buf[slot, MOE_BLK // 2:MOE_BLK] = jnp.zeros((MOE_BLK // 2, D_MODEL // 2), jnp.int32)

        @pl.when(jnp.logical_not(short))
        def _():
            ffn(MOE_BLK)

        for cp in out_copies(g):
            cp.start()
        return carry

    lax.fori_loop(0, count, block, 0)

    @pl.when(e == N_EXPERTS - 1)
    def _():
        for back in range(EXPERT_SLOTS, 0, -1):
            @pl.when(n_used >= back)
            def _():
                for cp in out_copies(n_used - back):
                    cp.wait()


EXPERT_SLOTS = 4
EXPERT_W_SLOTS = 4
EXPERT_W_PARTS = 2
EXPERT_IN_PARTS = 2
EXPERT_OUT_PARTS = 4


def _experts(first_blk, n_blk, short_last, xs, layer, w_gate, w_up, w_down):
    anywhere = pl.BlockSpec(memory_space=pl.ANY)
    grid_spec = pltpu.PrefetchScalarGridSpec(
        num_scalar_prefetch=3,
        grid=(N_EXPERTS,),
        in_specs=[anywhere] * 4,
        out_specs=anywhere,
        scratch_shapes=[pltpu.VMEM((EXPERT_W_SLOTS, D_MODEL, D_EXPERT), F32),
                        pltpu.VMEM((EXPERT_W_SLOTS, D_MODEL, D_EXPERT), F32),
                        pltpu.VMEM((EXPERT_W_SLOTS, D_EXPERT, D_MODEL), F32),
                        pltpu.VMEM((D_MODEL, D_EXPERT), BF16), pltpu.VMEM((D_MODEL, D_EXPERT), BF16),
                        pltpu.VMEM((D_EXPERT, D_MODEL), BF16),
                        pltpu.VMEM((EXPERT_SLOTS, MOE_BLK, D_MODEL // 2), jnp.int32),
                        pltpu.VMEM((EXPERT_SLOTS, MOE_BLK, D_MODEL // 2), jnp.int32),
                        pltpu.SemaphoreType.DMA((EXPERT_W_SLOTS,)),
                        pltpu.SemaphoreType.DMA((EXPERT_SLOTS,)), pltpu.SemaphoreType.DMA((EXPERT_SLOTS,))],
    )
    return pl.pallas_call(
        functools.partial(_expert_body, layer=layer),
        grid_spec=grid_spec,
        out_shape=jax.ShapeDtypeStruct((MOE_ROWS, D_MODEL // 2), jnp.int32),
        compiler_params=_cparams("arbitrary"),
        name="moe_experts",
    )(first_blk, n_blk, short_last, xs, w_gate, w_up, w_down)


def _combine_body(x1_ref, h2_ref, *refs, final):
    yg_refs = refs[:MOE_REGIONS]
    gk_ref, mod_ref, sg_ref, su_ref, sd_ref, fn_ref, *o_refs = refs[MOE_REGIONS:]
    hi, lo = _unpack_bf16_pairs(h2_ref[...])
    half = D_MODEL // 2

    def proj(w_ref):
        return _dot(hi, w_ref[0:half, :]) + _dot(lo, w_ref[half:, :])

    shared = _dot(_silu(proj(sg_ref)) * proj(su_ref), sd_ref[...])
    acc_hi, acc_lo = shared[:, :half], shared[:, half:]
    gk = gk_ref[...]
    for k in range(TOP_K):
        y_hi, y_lo = _unpack_bf16_pairs(yg_refs[k // K_PER_REGION][k % K_PER_REGION])
        acc_hi = acc_hi + gk[:, k:k + 1] * y_hi.astype(F32)
        acc_lo = acc_lo + gk[:, k:k + 1] * y_lo.astype(F32)
    acc = jnp.concatenate([acc_hi, acc_lo], axis=1)
    m = mod_ref[0]
    y = x1_ref[...] + m[:, 5 * D_MODEL:6 * D_MODEL] * acc
    if not final:
        o_refs[0][...] = y
        return
    y = _rms(y, fn_ref[...])
    is_ctx = pl.program_id(0) < N_CTX_TILES

    @pl.when(is_ctx)
    def _():
        o_refs[0][...] = y

    @pl.when(jnp.logical_not(is_ctx))
    def _():
        o_refs[1][...] = y


def _combine(x1, h2p, yg, gk, mod_l, ws_gate, ws_up, ws_down, final_norm, final):
    tok = lambda shape: pl.BlockSpec(shape, lambda i: (i, 0))
    full = lambda shape: pl.BlockSpec(shape, lambda i: (0, 0))
    if final:
        out_specs, _ = _token_specs((None, None), D_MODEL)
        out_shape = [jax.ShapeDtypeStruct((T_CTX, D_MODEL), F32), jax.ShapeDtypeStruct((T_LAT, D_MODEL), F32)]
    else:
        out_specs = tok((TM, D_MODEL))
        out_shape = jax.ShapeDtypeStruct((T_ALL, D_MODEL), F32)
    return pl.pallas_call(
        functools.partial(_combine_body, final=final),
        grid=(T_ALL // TM,),
        in_specs=[tok((TM, D_MODEL)), tok((TM, D_MODEL // 2))]
                 + [pl.BlockSpec((K_PER_REGION, TM, D_MODEL // 2), lambda i: (0, i, 0))] * MOE_REGIONS
                 + [tok((TM, TOP_K)),
                  pl.BlockSpec((1, 1, N_MOD * D_MODEL), lambda i: (_mod_row(i), 0, 0)),
                  full((D_MODEL, D_EXPERT)), full((D_MODEL, D_EXPERT)), full((D_EXPERT, D_MODEL)),
                  full((1, D_MODEL))],
        out_specs=out_specs,
        out_shape=out_shape,
        compiler_params=_cparams("arbitrary"),
        name="moe_combine",
    )(x1, h2p, *yg, gk, mod_l, ws_gate, ws_up, ws_down, final_norm.reshape(1, D_MODEL))


def _moe(x1, h2p, chosen, gk, ik, mod_l, layer, w_gate, w_up, w_down, ws_gate, ws_up, ws_down, final_norm, final):
    dest, first_blk, n_blk, short_last = _route(chosen, ik)
    yg = []
    for r in range(MOE_REGIONS):
        xs = _sc_dispatch(h2p, dest[r])
        y = _experts(first_blk[r, :, 0], n_blk[r, :, 0], short_last[r, :, 0], xs, layer, w_gate, w_up, w_down)
        yg.append(_sc_collect(y, dest[r].reshape(-1)).reshape(K_PER_REGION, T_ALL, D_MODEL // 2))
    return _combine(x1, h2p, yg, gk, mod_l, ws_gate, ws_up, ws_down, final_norm, final)


def kernel(x_prompt, x_sample, cache_a_k, cache_a_v, cache_b_k, cache_b_v, state_d_fwd, state_d_bwd, c, c_ctx, w_ada, b_ada, norm_mix, norm_ffn, w_in_attn, w_out_attn, sink_a, rpb_b, w_in_rec, w_out_rec, conv_w, conv_b, filt_w1, filt_b1, filt_w2, filt_b2, filt_w3, filt_b3, filt_freq, filt_w4, d_skip, lb_fwd, lb_bwd, norm_d, w_router, router_bias, w_gate, w_up, w_down, ws_gate, ws_up, ws_down, final_norm):
    x = (x_prompt.reshape(T_CTX, D_MODEL), x_sample.reshape(T_LAT, D_MODEL))
    cvec = jnp.concatenate([c_ctx[None, :], c], axis=0)
    c_lanes = jnp.broadcast_to(cvec[:, :, None], (N_CVEC, D_MODEL, LANES))
    mod = [_ada(c_lanes, l, w_ada, b_ada).reshape(CVEC_PAD, 1, N_MOD * D_MODEL) for l in range(DEPTH)]

    new_kv = None
    new_state = None
    for l in range(DEPTH):
        j = l // 2
        final = l == DEPTH - 1
        if l % 2 == 0:
            qkv = _inproj(x, mod[l], norm_mix[l], w_in_attn[j])
            oa_ctx, ob_ctx, *new_kv = _ctx_attn(qkv, sink_a[j])
            new_kv = tuple(new_kv)
            q_rot, k_rot = _rope(qkv)
            cache = lambda t: t[:, j].reshape(DEC_BATCH, PAST_LEN, -1)
            oa_lat = _win_attn(qkv, q_rot, k_rot, cache(cache_a_k), cache(cache_a_v), sink_a[j])
            ob_lat = _na_attn(qkv, cache(cache_b_k), cache(cache_b_v), _na_rel_rows(rpb_b[j]))
            mix_a = (oa_ctx, oa_lat)
            mix_b = (ob_ctx, ob_lat)
            w_out = w_out_attn[j]
        else:
            u = _inproj(x, mod[l], norm_mix[l], w_in_rec[j])
            filt = (filt_w1[j], filt_b1[j], filt_w2[j], filt_b2[j], filt_w3[j], filt_b3[j], filt_freq[j],
                    filt_w4[j])
            y_ctx = _hyena(u, 0, BATCH, SEQ, conv_w[j], conv_b[j], d_skip[j], _hyena_filter(SEQ, filt))
            y_lat = _hyena(u, T_CTX // DEC_SEQ, DEC_BATCH, DEC_SEQ, conv_w[j], conv_b[j], d_skip[j],
                           _hyena_filter(DEC_SEQ, filt))
            zeros = jnp.zeros((BATCH, D_HEADS, D_KDIM, D_VDIM), F32)
            o_ctx, s_f, s_b = _hgrn(u, 0, BATCH, SEQ, lb_fwd, lb_bwd, norm_d[j], zeros, zeros, l)
            o_lat, _, _ = _hgrn(u, T_CTX // DEC_SEQ, DEC_BATCH, DEC_SEQ, lb_fwd, lb_bwd, norm_d[j],
                                state_d_fwd[:, j], state_d_bwd[:, j], l)
            new_state = (s_f[:, None], s_b[:, None])
            mix_a = (y_ctx, y_lat)
            mix_b = (o_ctx, o_lat)
            w_out = w_out_rec[j]
        x1, h2p, chosen, gk, ik = _outproj(mix_a, mix_b, x, mod[l], norm_ffn[l], w_out, w_router[l],
                                           router_bias[l])
        x = _moe(x1, h2p, chosen, gk, ik, mod[l], l, w_gate, w_up, w_down, ws_gate[l], ws_up[l],
                 ws_down[l], final_norm, final)

    y_prompt = x[0].reshape(BATCH, SEQ, D_MODEL)
    y_sample = x[1].reshape(DEC_BATCH, DEC_SEQ, D_MODEL)
    return (y_prompt, y_sample) + new_kv + new_state
```

```python
import functools
import math

import numpy as np
import jax
import jax.numpy as jnp
from jax import lax
from jax.experimental import pallas as pl
from jax.experimental.pallas import tpu as pltpu
from jax.experimental.pallas import tpu_sc as plsc

F32 = jnp.float32
BF16 = jnp.bfloat16
HI = lax.Precision.HIGHEST

D_MODEL = 1024
BATCH = 16
SEQ = 256
DEPTH = 2
DEC_BATCH = 2
DEC_SEQ = 1024
PAST_LEN = 512
GRID_W = 64
HEAD_DIM = 64
N_MOD = 6
RMS_EPS = 1e-6
A_HEADS = 8
A_KV_HEADS = 2
A_GROUP = A_HEADS // A_KV_HEADS
WINDOW = 128
ROPE_BASE = 10000.0
B_HEADS = 8
NA_ROWS = 8
NA_COLS = 16
C_DIM = 512
C_EMB = 33
C_FFN = 64
HYENA_MIN_DECAY = math.log(1e-2) / 1.5
HYENA_MAX_DECAY = math.log(1e-2) / 0.3
D_KDIM = 128
D_VDIM = 128
D_HEADS = 4
N_EXPERTS = 64
TOP_K = 8
D_EXPERT = 256
ROUTE_SCALE = 2.5
A_Q = A_HEADS * HEAD_DIM
A_KV = A_KV_HEADS * HEAD_DIM
B_W = B_HEADS * HEAD_DIM
ATTN_IN = A_Q + 2 * A_KV + 3 * B_W

T_CTX = BATCH * SEQ
T_LAT = DEC_BATCH * DEC_SEQ
T_ALL = T_CTX + T_LAT
N_CVEC = 1 + DEC_BATCH
CVEC_PAD = 8
TM = 512
MASK_NEG = -1e30
GLA_CHUNK = 64
GLA_SPAN = 256
GLA_MIN_SPANS = 2
HGRN_HEADS_PER_STEP = 4
DFT_CHUNK = 512
MOE_BLK = 512
MOE_REGIONS = 1
K_PER_REGION = TOP_K // MOE_REGIONS
MOE_NBLK = -(-(T_ALL * K_PER_REGION + N_EXPERTS * (MOE_BLK - 1)) // MOE_BLK)
MOE_ROWS = MOE_NBLK * MOE_BLK
SC_CORES = 2
SC_SUBCORES = 16
SC_WORKERS = SC_CORES * SC_SUBCORES
DISP_CHUNK = 128
DISP_SPLIT = 2
COLLECT_CHUNK = 64
VMEM_LIMIT = 56 * 1024 * 1024


def _cparams(*sem):
    return pltpu.CompilerParams(dimension_semantics=sem, vmem_limit_bytes=VMEM_LIMIT)


def _mod_row(i):
    return jnp.where(i < T_CTX // TM, 0, 1 + (i - T_CTX // TM) // (DEC_SEQ // TM))


def _dot(a, b):
    return jnp.dot(a.astype(BF16), b.astype(BF16), preferred_element_type=F32)


def _dot_nt(a, b):
    return lax.dot_general(a.astype(BF16), b.astype(BF16), (((1,), (1,)), ((), ())),
                           preferred_element_type=F32)


def _dot_tn(a, b):
    return lax.dot_general(a.astype(BF16), b.astype(BF16), (((0,), (0,)), ((), ())),
                           preferred_element_type=F32)


def _dot_hi(a, b):
    return jnp.dot(a, b, precision=HI, preferred_element_type=F32)


def _split_bf16(x):
    hi = x.astype(BF16)
    return hi, (x - hi.astype(F32)).astype(BF16)


def _dot_split(a, b):
    a_hi, a_lo = _split_bf16(a)
    b_hi, b_lo = _split_bf16(b)
    dot = lambda x, y: jnp.dot(x, y, preferred_element_type=F32)
    return dot(a_hi, b_hi) + dot(a_hi, b_lo) + dot(a_lo, b_hi)


def _silu(x):
    return x * jax.nn.sigmoid(x)


def _rms(x, g):
    return x * lax.rsqrt(jnp.mean(x * x, axis=-1, keepdims=True) + RMS_EPS) * g


ADA_TN = 1536
ADA_UNROLL = 4


def _ada_body(cb_ref, w_ref, b_ref, o_ref):
    tn = o_ref.shape[-1]
    n_slab = tn // LANES

    def step(k8, accs):
        r0 = pl.multiple_of(k8 * 8, 8)
        sk = [_silu(cb_ref[j, pl.ds(r0, 8), :]) for j in range(N_CVEC)]
        out = []
        for s in range(n_slab):
            wk = w_ref[0, pl.ds(r0, 8), s * LANES:(s + 1) * LANES]
            out.extend(accs[s * N_CVEC + j] + wk * sk[j] for j in range(N_CVEC))
        return tuple(out)

    accs = lax.fori_loop(0, D_MODEL // 8, step,
                         tuple(jnp.zeros((8, LANES), F32) for _ in range(n_slab * N_CVEC)), unroll=ADA_UNROLL)
    o_ref[0] = jnp.zeros((CVEC_PAD, tn), F32)
    for s in range(n_slab):
        for j in range(N_CVEC):
            o_ref[0, j:j + 1, s * LANES:(s + 1) * LANES] = (
                jnp.sum(accs[s * N_CVEC + j], axis=0, keepdims=True) + b_ref[0, :, s * LANES:(s + 1) * LANES])


def _ada(c_lanes, layer, w_ada, b_ada):
    n_out = N_MOD * D_MODEL
    return pl.pallas_call(
        _ada_body,
        grid=(n_out // ADA_TN,),
        in_specs=[pl.BlockSpec((N_CVEC, D_MODEL, LANES), lambda n: (0, 0, 0)),
                  pl.BlockSpec((1, D_MODEL, ADA_TN), lambda n: (layer, 0, n)),
                  pl.BlockSpec((1, 1, ADA_TN), lambda n: (layer, 0, n))],
        out_specs=pl.BlockSpec((1, CVEC_PAD, ADA_TN), lambda n: (0, 0, n)),
        out_shape=jax.ShapeDtypeStruct((1, CVEC_PAD, n_out), F32),
        compiler_params=_cparams("parallel"),
        name="ada",
    )(c_lanes, w_ada, b_ada.reshape(DEPTH, 1, n_out))


N_CTX_TILES = T_CTX // TM


def _token_specs(x, width):
    if not isinstance(x, tuple):
        return [pl.BlockSpec((TM, width), lambda i: (i, 0))], (x,)
    return ([pl.BlockSpec((TM, width), lambda i: (jnp.minimum(i, N_CTX_TILES - 1), 0)),
             pl.BlockSpec((TM, width), lambda i: (jnp.maximum(i - N_CTX_TILES, 0), 0))], x)


def _token_tile(refs):
    if len(refs) == 1:
        return refs[0][...]
    return jnp.where(pl.program_id(0) < N_CTX_TILES, refs[0][...], refs[1][...])


def _inproj_body(*refs, n_x):
    x_refs, (mod_ref, g_ref, w_ref, o_ref, w_bf) = refs[:n_x], refs[n_x:]

    @pl.when(pl.program_id(0) == 0)
    def _():
        w_bf[...] = w_ref[...].astype(BF16)

    m = mod_ref[0]
    h = _rms(_token_tile(x_refs), g_ref[...]) * (1.0 + m[:, D_MODEL:2 * D_MODEL]) + m[:, 0:D_MODEL]
    o_ref[...] = _dot(h, w_bf[...])


def _inproj(x, mod_l, gain, w):
    n = w.shape[1]
    x_specs, x_args = _token_specs(x, D_MODEL)
    return pl.pallas_call(
        functools.partial(_inproj_body, n_x=len(x_args)),
        grid=(T_ALL // TM,),
        in_specs=x_specs + [pl.BlockSpec((1, 1, N_MOD * D_MODEL), lambda i: (_mod_row(i), 0, 0)),
                            pl.BlockSpec((1, D_MODEL), lambda i: (0, 0)),
                            pl.BlockSpec((D_MODEL, n), lambda i: (0, 0), pipeline_mode=pl.Buffered(1))],
        out_specs=pl.BlockSpec((TM, n), lambda i: (i, 0)),
        out_shape=jax.ShapeDtypeStruct((T_ALL, n), F32),
        scratch_shapes=[pltpu.VMEM((D_MODEL, n), BF16)],
        compiler_params=_cparams("arbitrary"),
        name="inproj",
    )(*x_args, mod_l, gain.reshape(1, D_MODEL), w)


def _ctx_attn_body(qkv_ref, sink_ref, oa_ref, ob_ref, ak_ref, av_ref, bk_ref, bv_ref):
    scale = HEAD_DIM ** -0.5
    lane = lax.broadcasted_iota(jnp.int32, (SEQ, LANES), 1)
    in_half = [lane < HEAD_DIM, lane >= HEAD_DIM]

    def attend(q, k, v, sink):
        s = _dot_nt(q, k) * scale
        m = jnp.max(s, axis=-1, keepdims=True)
        if sink is not None:
            m = jnp.maximum(m, sink)
        p = jnp.exp(s - m)
        den = jnp.sum(p, axis=-1, keepdims=True)
        if sink is not None:
            den = den + jnp.exp(sink - m)
        return _dot(p, v) / den

    def tile(first_col, t):
        return qkv_ref[:, first_col + t * LANES:first_col + (t + 1) * LANES]

    base = A_Q + 2 * A_KV
    for hk in range(A_KV_HEADS):
        dst = pl.ds(hk, SEQ, stride=A_KV_HEADS)
        ak_ref[0, dst, :] = qkv_ref[:, A_Q + hk * HEAD_DIM:A_Q + (hk + 1) * HEAD_DIM]
        av_ref[0, dst, :] = qkv_ref[:, A_Q + A_KV + hk * HEAD_DIM:A_Q + A_KV + (hk + 1) * HEAD_DIM]
    for h in range(B_HEADS):
        dst = pl.ds(h, SEQ, stride=B_HEADS)
        bk_ref[0, dst, :] = qkv_ref[:, base + B_W + h * HEAD_DIM:base + B_W + (h + 1) * HEAD_DIM]
        bv_ref[0, dst, :] = qkv_ref[:, base + 2 * B_W + h * HEAD_DIM:base + 2 * B_W + (h + 1) * HEAD_DIM]

    k_t, v_t = tile(A_Q, 0), tile(A_Q + A_KV, 0)
    k_sw, v_sw = pltpu.roll(k_t, HEAD_DIM, axis=1), pltpu.roll(v_t, HEAD_DIM, axis=1)
    tiles_per_kv = A_GROUP // HEADS_PER_TILE
    for hk in range(A_KV_HEADS):
        q_tiles = [tile(0, hk * tiles_per_kv + j) for j in range(tiles_per_kv)]
        halves = []
        for p in range(HEADS_PER_TILE):
            q = jnp.concatenate([jnp.where(in_half[p], qt, 0.0) for qt in q_tiles], axis=0)
            heads = [(hk * tiles_per_kv + j) * HEADS_PER_TILE + p for j in range(tiles_per_kv)]
            sink = jnp.concatenate([jnp.broadcast_to(sink_ref[:, h:h + 1], (SEQ, 1)) for h in heads], axis=0)
            halves.append(attend(q, k_t if p == hk else k_sw, v_t if p == hk else v_sw, sink))
        first_half = lax.broadcasted_iota(jnp.int32, halves[0].shape, 1) < HEAD_DIM
        o = jnp.where(first_half, halves[0], halves[1])
        for j in range(tiles_per_kv):
            t = hk * tiles_per_kv + j
            oa_ref[:, t * LANES:(t + 1) * LANES] = o[j * SEQ:(j + 1) * SEQ]

    for t in range(B_HEADS // HEADS_PER_TILE):
        q_t, k_b, v_b = tile(base, t), tile(base + B_W, t), tile(base + 2 * B_W, t)
        halves = [attend(jnp.where(in_half[p], q_t, 0.0), k_b, v_b, None) for p in range(HEADS_PER_TILE)]
        ob_ref[:, t * LANES:(t + 1) * LANES] = jnp.where(in_half[0], halves[0], halves[1])


def _ctx_attn(qkv, sink):
    kv_spec = lambda heads: pl.BlockSpec((1, SEQ * heads, HEAD_DIM), lambda b: (b, 0, 0))
    kv_sd = lambda heads: jax.ShapeDtypeStruct((BATCH, SEQ * heads, HEAD_DIM), F32)
    outs = pl.pallas_call(
        _ctx_attn_body,
        grid=(BATCH,),
        in_specs=[pl.BlockSpec((SEQ, ATTN_IN), lambda b: (b, 0)),
                  pl.BlockSpec((1, A_HEADS), lambda b: (0, 0))],
        out_specs=[pl.BlockSpec((SEQ, A_Q), lambda b: (b, 0)), pl.BlockSpec((SEQ, B_W), lambda b: (b, 0)),
                   kv_spec(A_KV_HEADS), kv_spec(A_KV_HEADS), kv_spec(B_HEADS), kv_spec(B_HEADS)],
        out_shape=[jax.ShapeDtypeStruct((T_CTX, A_Q), F32), jax.ShapeDtypeStruct((T_CTX, B_W), F32),
                   kv_sd(A_KV_HEADS), kv_sd(A_KV_HEADS), kv_sd(B_HEADS), kv_sd(B_HEADS)],
        compiler_params=_cparams("parallel"),
        name="ctx_attn",
    )(qkv, sink.reshape(1, A_HEADS))
    caches = [t.reshape(BATCH, 1, SEQ, -1, HEAD_DIM) for t in outs[2:]]
    return outs[0], outs[1], *caches


@functools.lru_cache(maxsize=None)
def _rope_tables(width):
    half = HEAD_DIM // 2
    t = np.arange(DEC_SEQ)
    inv = ROPE_BASE ** (-np.arange(0, half, 2, dtype=np.float64) / half)
    ang_r = (t // GRID_W)[:, None] * inv[None, :]
    ang_c = (t % GRID_W)[:, None] * inv[None, :]
    cos = np.concatenate([np.cos(ang_r)] * 2 + [np.cos(ang_c)] * 2, axis=-1)
    sin = np.concatenate([-np.sin(ang_r), np.sin(ang_r), -np.sin(ang_c), np.sin(ang_c)], axis=-1)
    reps = width // HEAD_DIM
    return (np.tile(cos, (1, reps)).astype(np.float32), np.tile(sin, (1, reps)).astype(np.float32))


def _rope_body(q_ref, k_ref, cq_ref, sq_ref, ck_ref, sk_ref, qo_ref, ko_ref):
    quarter = HEAD_DIM // 4

    def rot(x, cos, sin):
        w = x.shape[-1]
        lane = lax.broadcasted_iota(jnp.int32, x.shape, 1)
        fwd = pltpu.roll(x, w - quarter, axis=1)
        bwd = pltpu.roll(x, quarter, axis=1)
        partner = jnp.where((lane & (2 * quarter - 1)) < quarter, fwd, bwd)
        return x * cos + partner * sin

    qo_ref[...] = rot(q_ref[...], cq_ref[...], sq_ref[...])
    ko_ref[...] = rot(k_ref[...], ck_ref[...], sk_ref[...])


def _rope(qkv):
    cq, sq = _rope_tables(A_Q)
    ck, sk = _rope_tables(A_KV)
    tab = lambda w: pl.BlockSpec((DEC_SEQ, w), lambda b: (0, 0))
    row0 = T_CTX // DEC_SEQ
    return pl.pallas_call(
        _rope_body,
        grid=(DEC_BATCH,),
        in_specs=[pl.BlockSpec((DEC_SEQ, A_Q), lambda b: (row0 + b, 0)),
                  pl.BlockSpec((DEC_SEQ, A_KV), lambda b: (row0 + b, A_Q // A_KV)),
                  tab(A_Q), tab(A_Q), tab(A_KV), tab(A_KV)],
        out_specs=[pl.BlockSpec((DEC_SEQ, A_Q), lambda b: (b, 0)),
                   pl.BlockSpec((DEC_SEQ, A_KV), lambda b: (b, 0))],
        out_shape=[jax.ShapeDtypeStruct((T_LAT, A_Q), F32), jax.ShapeDtypeStruct((T_LAT, A_KV), F32)],
        compiler_params=_cparams("parallel"),
        name="rope",
    )(qkv, qkv, jnp.asarray(cq), jnp.asarray(sq), jnp.asarray(ck), jnp.asarray(sk))


WIN_QB = 128


def _win_attn_body(qraw_ref, qrot_ref, krot_ref, v_ref, kc_ref, vc_ref, sink_ref, o_ref):
    scale = HEAD_DIM ** -0.5
    hk = pl.program_id(1)
    tiles = A_GROUP // HEADS_PER_TILE

    def kv_in_half(x):
        swapped = pltpu.roll(x, HEAD_DIM, axis=1)
        return [jnp.where(hk == p, x, swapped) for p in range(HEADS_PER_TILE)]

    k, v, kc, vc = kv_in_half(krot_ref[...]), kv_in_half(v_ref[...]), kv_in_half(kc_ref[0]), kv_in_half(vc_ref[0])
    head_lane = lax.broadcasted_iota(jnp.int32, (1, A_HEADS), 1)

    def sink_rows(p):
        heads = [hk * A_GROUP + j * HEADS_PER_TILE + p for j in range(tiles)]
        vals = [jnp.sum(jnp.where(head_lane == h, sink_ref[...], 0.0), axis=-1, keepdims=True) for h in heads]
        return jnp.concatenate([jnp.broadcast_to(s, (WIN_QB, 1)) for s in vals], axis=0)

    sinks = [sink_rows(p) for p in range(HEADS_PER_TILE)]
    lane = lax.broadcasted_iota(jnp.int32, (tiles * WIN_QB, LANES), 1)
    in_half = [lane < HEAD_DIM, lane >= HEAD_DIM]
    for qb in range(DEC_SEQ // WIN_QB):
        q0 = qb * WIN_QB
        rows = slice(q0, q0 + WIN_QB)
        lo = max(0, q0 - WINDOW)
        hi = min(DEC_SEQ, q0 + WIN_QB + WINDOW)
        q_rot = jnp.concatenate([qrot_ref[rows, j * LANES:(j + 1) * LANES] for j in range(tiles)], axis=0)
        q_raw = jnp.concatenate([qraw_ref[rows, j * LANES:(j + 1) * LANES] for j in range(tiles)], axis=0)
        halves = []
        for p in range(HEADS_PER_TILE):
            s_loc = _dot_nt(jnp.where(in_half[p], q_rot, 0.0), k[p][lo:hi]) * scale
            qpos = q0 + (lax.broadcasted_iota(jnp.int32, s_loc.shape, 0) & (WIN_QB - 1))
            kpos = lo + lax.broadcasted_iota(jnp.int32, s_loc.shape, 1)
            s_loc = jnp.where(jnp.abs(kpos - qpos) <= WINDOW, s_loc, MASK_NEG)
            s_ctx = _dot_nt(jnp.where(in_half[p], q_raw, 0.0), kc[p]) * scale
            m = jnp.maximum(jnp.maximum(jnp.max(s_loc, axis=-1, keepdims=True),
                                        jnp.max(s_ctx, axis=-1, keepdims=True)), sinks[p])
            p_loc = jnp.exp(s_loc - m)
            p_ctx = jnp.exp(s_ctx - m)
            den = (jnp.sum(p_loc, axis=-1, keepdims=True) + jnp.sum(p_ctx, axis=-1, keepdims=True)
                   + jnp.exp(sinks[p] - m))
            halves.append((_dot(p_ctx, vc[p]) + _dot(p_loc, v[p][lo:hi])) / den)
        o = jnp.where(in_half[0], halves[0], halves[1])
        for j in range(tiles):
            o_ref[rows, j * LANES:(j + 1) * LANES] = o[j * WIN_QB:(j + 1) * WIN_QB]


def _win_attn(qkv, q_rot, k_rot, kc, vc, sink):
    row0 = T_CTX // DEC_SEQ
    gw = A_GROUP * HEAD_DIM
    return pl.pallas_call(
        _win_attn_body,
        grid=(DEC_BATCH, A_KV_HEADS),
        in_specs=[pl.BlockSpec((DEC_SEQ, gw), lambda b, h: (row0 + b, h)),
                  pl.BlockSpec((DEC_SEQ, gw), lambda b, h: (b, h)),
                  pl.BlockSpec((DEC_SEQ, A_KV), lambda b, h: (b, 0)),
                  pl.BlockSpec((DEC_SEQ, A_KV), lambda b, h: (row0 + b, (A_Q + A_KV) // A_KV)),
                  pl.BlockSpec((1, PAST_LEN, A_KV), lambda b, h: (b, 0, 0)),
                  pl.BlockSpec((1, PAST_LEN, A_KV), lambda b, h: (b, 0, 0)),
                  pl.BlockSpec((1, A_HEADS), lambda b, h: (0, 0))],
        out_specs=pl.BlockSpec((DEC_SEQ, gw), lambda b, h: (b, h)),
        out_shape=jax.ShapeDtypeStruct((T_LAT, A_Q), F32),
        compiler_params=_cparams("parallel", "parallel"),
        name="win_attn",
    )(qkv, q_rot, k_rot, qkv, kc, vc, sink.reshape(1, A_HEADS))


GRID_ROWS = DEC_SEQ // GRID_W
NA_BAND = min(NA_ROWS, GRID_ROWS)


NA_REL_ROWS = 2 * NA_ROWS - 1
NA_REL_COLS = 2 * NA_COLS - 1
LANES = 128
HEADS_PER_TILE = LANES // HEAD_DIM


def _na_rel_rows(rpb):
    pad = jnp.zeros((B_HEADS, NA_REL_ROWS, GRID_W - NA_REL_COLS), F32)
    one = jnp.concatenate([rpb, pad], axis=-1)
    nxt = jnp.concatenate([one[:, 1:], jnp.zeros((B_HEADS, 1, GRID_W), F32)], axis=1)
    both = jnp.concatenate([one, nxt], axis=-1)
    return jnp.concatenate([both, jnp.zeros((B_HEADS, 16 - NA_REL_ROWS, LANES), F32)], axis=1)


NA_HEADS_PER_STEP = LANES // HEAD_DIM


def _na_row_groups():
    groups = []
    for r in range(GRID_ROWS):
        rs = min(max(r - NA_ROWS // 2, 0), GRID_ROWS - NA_BAND)
        if groups and groups[-1][2] == rs:
            groups[-1][1] += 1
        else:
            groups.append([r, 1, rs])
    return groups


def _na_attn_body(q_ref, k_ref, v_ref, kc_ref, vc_ref, rel_ref, o_ref):
    scale = HEAD_DIM ** -0.5
    cq = lax.broadcasted_iota(jnp.int32, (GRID_W, LANES), 0)
    kcol = lax.broadcasted_iota(jnp.int32, (GRID_W, LANES), 1) & (GRID_W - 1)
    cs = jnp.clip(cq - NA_COLS // 2, 0, GRID_W - NA_COLS)
    col_ok = (kcol >= cs) & (kcol < cs + NA_COLS)
    kc = kc_ref[0]
    vc = vc_ref[0]
    tiles = {}

    def pair_tile(hh, a):
        if (hh, a) not in tiles:
            x = jnp.broadcast_to(rel_ref[hh, a:a + 1, :], (GRID_W, LANES))
            t = pltpu.roll(x, LANES - (NA_COLS - 1), axis=1, stride=1, stride_axis=0)
            tiles[hh, a] = jnp.where(col_ok, t, MASK_NEG)
        return tiles[hh, a]

    for r0, n_r, rs in _na_row_groups():
        rows = slice(r0 * GRID_W, (r0 + n_r) * GRID_W)
        band = slice(rs * GRID_W, (rs + NA_BAND) * GRID_W)
        q_t, k_t, v_t = q_ref[rows, :], k_ref[band, :], v_ref[band, :]
        head_of_lane = lax.broadcasted_iota(jnp.int32, q_t.shape, 1) >> (HEAD_DIM.bit_length() - 1)
        o = jnp.zeros(q_t.shape, F32)
        for hh in range(NA_HEADS_PER_STEP):
            bias = jnp.concatenate(
                [jnp.concatenate([pair_tile(hh, rs - r + NA_ROWS - 1 + 2 * i) for i in range(NA_BAND // 2)], axis=1)
                 for r in range(r0, r0 + n_r)], axis=0)
            q = jnp.where(head_of_lane == hh, q_t, 0.0)
            s_loc = _dot_nt(q, k_t) * scale + bias
            s_ctx = _dot_nt(q, kc) * scale
            m = jnp.maximum(jnp.max(s_loc, axis=-1, keepdims=True), jnp.max(s_ctx, axis=-1, keepdims=True))
            p_loc = jnp.exp(s_loc - m)
            p_ctx = jnp.exp(s_ctx - m)
            den = jnp.sum(p_loc, axis=-1, keepdims=True) + jnp.sum(p_ctx, axis=-1, keepdims=True)
            o = jnp.where(head_of_lane == hh, (_dot(p_ctx, vc) + _dot(p_loc, v_t)) / den, o)
        o_ref[rows, :] = o


def _na_attn(qkv, kc, vc, rel):
    row0 = T_CTX // DEC_SEQ
    col0 = (A_Q + 2 * A_KV) // LANES
    n_blk = B_W // LANES
    col = lambda j: pl.BlockSpec((DEC_SEQ, LANES), lambda b, p: (row0 + b, col0 + j * n_blk + p))
    cache = pl.BlockSpec((1, PAST_LEN, LANES), lambda b, p: (b, 0, p))
    return pl.pallas_call(
        _na_attn_body,
        grid=(DEC_BATCH, n_blk),
        in_specs=[col(0), col(1), col(2), cache, cache,
                  pl.BlockSpec((NA_HEADS_PER_STEP, 16, LANES), lambda b, p: (p, 0, 0))],
        out_specs=pl.BlockSpec((DEC_SEQ, LANES), lambda b, p: (b, p)),
        out_shape=jax.ShapeDtypeStruct((T_LAT, B_W), F32),
        compiler_params=_cparams("parallel", "parallel"),
        name="na_attn",
    )(qkv, qkv, qkv, kc, vc, rel)


@functools.lru_cache(maxsize=None)
def _dft_mats(L):
    n = 2 * L
    fc = min(L, DFT_CHUNK)
    f = np.arange(L)[:, None]
    t = np.arange(L)[None, :]
    ang = 2.0 * np.pi * ((f * t) % n) / n
    m1 = np.cos(ang)
    m2 = np.sin(ang)
    m2[0, :] = np.where(np.arange(L) % 2 == 0, 1.0, -1.0)
    wgt = np.full((L, 1), 2.0)
    wgt[0, 0] = 1.0
    nch = L // fc
    fwd = np.concatenate([m1.reshape(nch, fc, L), m2.reshape(nch, fc, L)], axis=1)
    inv = np.concatenate([(m1 * wgt / n).reshape(nch, fc, L), (m2 * wgt / n).reshape(nch, fc, L)], axis=1)
    inv = np.transpose(inv, (0, 2, 1))
    return fwd.astype(np.float32), inv.astype(np.float32)


@functools.lru_cache(maxsize=None)
def _filter_consts(L):
    t = np.linspace(0.0, 1.0, L)[:, None]
    bands = (C_EMB - 1) // 2
    ang = (2.0 * math.pi / L) * np.arange(L)[:, None] * np.linspace(1e-4, bands - 1, bands)[None, :]
    z = np.concatenate([t, np.cos(ang), -np.sin(ang)], axis=-1)
    zpad = np.zeros((L, 128))
    zpad[:, :C_EMB] = z
    deltas = np.abs(np.linspace(HYENA_MIN_DECAY, HYENA_MAX_DECAY, C_DIM))
    window = np.exp(-t * deltas[None, :])
    return zpad.astype(np.float32), window.astype(np.float32)


def _filter_body(z_ref, w1_ref, b1_ref, w2_ref, b2_ref, w3_ref, b3_ref, fr_ref, w4_ref, win_ref, fm_ref,
                 hr_ref, g_ref, hq_ref, hs_scr, hd_scr):
    c = pl.program_id(0)
    fc = hr_ref.shape[0]

    @pl.when(c == 0)
    def _():
        fr = fr_ref[...]
        hh = jnp.sin(fr * (_dot_hi(z_ref[...], w1_ref[...]) + b1_ref[...]))
        hh = jnp.sin(fr * (_dot_hi(hh, w2_ref[...]) + b2_ref[...]))
        hh = jnp.sin(fr * (_dot_hi(hh, w3_ref[...]) + b3_ref[...]))
        hh = _dot_hi(hh, w4_ref[...])
        hf = hh[:, :C_DIM] * win_ref[...]
        hb = hh[:, C_DIM:] * win_ref[...]
        hs_scr[...] = hf + hb
        hd_scr[...] = hf - hb

    fm = fm_ref[0]
    hr = _dot_split(fm[:fc], hs_scr[...])
    first = (lax.broadcasted_iota(jnp.int32, (fc, C_DIM), 0) == 0) & (c == 0)
    hr_ref[...] = hr
    g_ref[...] = jnp.where(first, 0.0, _dot_split(fm[fc:], hd_scr[...]))
    hs = hs_scr[...]
    sign = jnp.where((lax.broadcasted_iota(jnp.int32, hs.shape, 0) & 1) == 0, 1.0, -1.0)
    hq_ref[...] = jnp.where(first, jnp.sum(hs * sign, axis=0, keepdims=True), hr)


def _hyena_filter(L, filt):
    w1, b1, w2, b2, w3, b3, freq, w4 = filt
    zpad, window = _filter_consts(L)
    fwd, _ = _dft_mats(L)
    nch, fc2, _ = fwd.shape
    fc = fc2 // 2
    w1p = jnp.pad(w1, ((0, 128 - C_EMB), (0, 0)))
    full = lambda shape: pl.BlockSpec(shape, lambda c: tuple(0 for _ in shape))
    out_spec = pl.BlockSpec((fc, C_DIM), lambda c: (c, 0))
    out_sd = jax.ShapeDtypeStruct((L, C_DIM), F32)
    return pl.pallas_call(
        _filter_body,
        grid=(nch,),
        in_specs=[full((L, 128)), full((128, C_FFN)), full((1, C_FFN)), full((C_FFN, C_FFN)), full((1, C_FFN)),
                  full((C_FFN, C_FFN)), full((1, C_FFN)), full((1, C_FFN)), full((C_FFN, 2 * C_DIM)),
                  full((L, C_DIM)), pl.BlockSpec((1, fc2, L), lambda c: (c, 0, 0))],
        out_specs=[out_spec, out_spec, out_spec],
        out_shape=[out_sd, out_sd, out_sd],
        scratch_shapes=[pltpu.VMEM((L, C_DIM), F32), pltpu.VMEM((L, C_DIM), F32)],
        compiler_params=_cparams("arbitrary"),
        name="hyena_filter",
    )(jnp.asarray(zpad), w1p, b1.reshape(1, C_FFN), w2, b2.reshape(1, C_FFN), w3, b3.reshape(1, C_FFN),
      freq.reshape(1, C_FFN), w4, jnp.asarray(window), jnp.asarray(fwd))


def _hyena_body(u_ref, cw_ref, cb_ref, d_ref, fm_ref, fi_ref, hr_ref, g_ref, hq_ref, y_ref,
                x0_scr, z_scr, acc_scr):
    c = pl.program_id(1)
    L = y_ref.shape[0]
    fc = hr_ref.shape[0]

    @pl.when(c == 0)
    def _():
        row = lax.broadcasted_iota(jnp.int32, (L, C_DIM), 0)

        def short_conv(sec):
            cols = slice(sec * C_DIM, (sec + 1) * C_DIM)
            u = u_ref[:, cols]
            prev = jnp.where(row == 0, 0.0, pltpu.roll(u, 1, axis=0))
            nxt = jnp.where(row == L - 1, 0.0, pltpu.roll(u, L - 1, axis=0))
            return (prev * cw_ref[0:1, cols] + u * cw_ref[1:2, cols] + nxt * cw_ref[2:3, cols]
                    + cb_ref[:, cols])

        x0_scr[...] = short_conv(0)
        z_scr[...] = short_conv(1) * short_conv(2)
        acc_scr[...] = jnp.zeros((L, C_DIM), F32)

    ab = _dot(fm_ref[0], z_scr[...])
    a, b = ab[:fc], ab[fc:]
    hr, g, hq = hr_ref[...], g_ref[...], hq_ref[...]
    pq = jnp.concatenate([a * hr - b * g, a * g + b * hq], axis=0)
    acc_scr[...] += _dot(fi_ref[0], pq)

    @pl.when(c == pl.num_programs(1) - 1)
    def _():
        y_ref[...] = x0_scr[...] * (acc_scr[...] + z_scr[...] * d_ref[...])


def _hyena(u, row_blk0, n_seq, L, conv_w, conv_b, d_skip, spec):
    hr, g, hq = spec
    fwd, inv = _dft_mats(L)
    nch, fc2, _ = fwd.shape
    fc = fc2 // 2
    u_w = 3 * C_DIM
    return pl.pallas_call(
        _hyena_body,
        grid=(n_seq, nch),
        in_specs=[pl.BlockSpec((L, u_w), lambda b, c: (row_blk0 + b, 0)),
                  pl.BlockSpec((3, u_w), lambda b, c: (0, 0)),
                  pl.BlockSpec((1, u_w), lambda b, c: (0, 0)),
                  pl.BlockSpec((1, C_DIM), lambda b, c: (0, 0)),
                  pl.BlockSpec((1, fc2, L), lambda b, c: (c, 0, 0)),
                  pl.BlockSpec((1, L, fc2), lambda b, c: (c, 0, 0)),
                  pl.BlockSpec((fc, C_DIM), lambda b, c: (c, 0)),
                  pl.BlockSpec((fc, C_DIM), lambda b, c: (c, 0)),
                  pl.BlockSpec((fc, C_DIM), lambda b, c: (c, 0))],
        out_specs=pl.BlockSpec((L, C_DIM), lambda b, c: (b, 0)),
        out_shape=jax.ShapeDtypeStruct((n_seq * L, C_DIM), F32),
        scratch_shapes=[pltpu.VMEM((L, C_DIM), F32)] * 3,
        compiler_params=_cparams("parallel", "arbitrary"),
        name="hyena",
    )(u, conv_w, conv_b.reshape(1, u_w), d_skip.reshape(1, C_DIM), jnp.asarray(fwd), jnp.asarray(inv), hr, g, hq)


def _hgrn_body(q_ref, ff_ref, fb_ref, i_ref, g_ref, lbf_ref, lbb_ref, nd_ref, s0f_ref, s0b_ref,
               o_ref, sf_ref, sb_ref, *, layer):
    L = o_ref.shape[0]
    C = GLA_CHUNK
    S = min(L // GLA_MIN_SPANS, GLA_SPAN)
    nc = S // C
    n_span = L // S
    mid = C // 2
    def lower_bound(gm):
        e = jnp.exp(gm - jnp.max(gm, axis=0, keepdims=True))
        p = e / jnp.sum(e, axis=0, keepdims=True)
        return jnp.sum(p[0:layer + 1], axis=0, keepdims=True) - p[0:1]

    def gates(fx, lb):
        f = lb + (1.0 - lb) * jax.nn.sigmoid(fx)
        return 1.0 - f, jnp.log(f)


    chunk_shift = C.bit_length() - 1
    block_shift = D_KDIM.bit_length() - 1
    ti = lax.broadcasted_iota(jnp.int32, (S, S), 0)
    si = lax.broadcasted_iota(jnp.int32, (S, S), 1)
    same_chunk = (ti >> chunk_shift) == (si >> chunk_shift)
    causal = same_chunk & (si <= ti)
    anti = same_chunk & (si >= ti)
    row_chunk = lax.broadcasted_iota(jnp.int32, (S, nc * D_KDIM), 0) >> chunk_shift
    col_chunk = lax.broadcasted_iota(jnp.int32, (S, nc * D_KDIM), 1) >> block_shift
    own_block = row_chunk == col_chunk

    def spread(x):
        return jnp.where(own_block, jnp.concatenate([x] * nc, axis=1), 0.0)

    def chunk_cumsum(mask, lg):
        tri = mask.astype(BF16)
        hi = lg.astype(BF16)
        r1 = lg - hi.astype(F32)
        mid_t = r1.astype(BF16)
        lo = (r1 - mid_t.astype(F32)).astype(BF16)
        dot = lambda t: jnp.dot(tri, t, preferred_element_type=F32)
        return dot(hi) + dot(mid_t) + dot(lo)

    def per_chunk_rows(b, pos):
        return jnp.concatenate([jnp.broadcast_to(b[n * C + pos:n * C + pos + 1], (C, D_KDIM)) for n in range(nc)],
                               axis=0)

    def one_head(q, v, kf, lgf, kb, lgb, st_f, st_b):
        local = []
        for u in range(n_span):
            rows = slice(u * S, (u + 1) * S)
            qs, vs, kfs, kbs = q[rows], v[rows], kf[rows], kb[rows]
            lgs = jnp.concatenate([lgf[rows], lgb[rows]], axis=1)
            pre = chunk_cumsum(causal, lgs)
            b_f = pre[:, :D_KDIM]
            pre_b = pre[:, D_KDIM:]
            b_b = per_chunk_rows(pre_b, C - 1) - pre_b + lgb[rows]
            ref_f, ref_b = per_chunk_rows(b_f, mid), per_chunk_rows(b_b, mid)
            sc = (jnp.where(causal, _dot_nt(qs * jnp.exp(b_f - ref_f), kfs * jnp.exp(ref_f - b_f)), 0.0)
                  + jnp.where(anti, _dot_nt(qs * jnp.exp(b_b - ref_b), kbs * jnp.exp(ref_b - b_b)), 0.0))
            k_out = jnp.concatenate([kfs * jnp.exp(per_chunk_rows(b_f, C - 1) - b_f),
                                     kbs * jnp.exp(per_chunk_rows(b_b, 0) - b_b)], axis=1)
            kv_t = _dot_tn(spread(vs), k_out)
            local.append((_dot(sc, vs), kv_t, b_f, b_b, qs))

        states_f = [[None] * nc for _ in range(n_span)]
        for u in range(n_span):
            _, kv_t, b_f, _, _ = local[u]
            for n in range(nc):
                states_f[u][n] = st_f
                st_f = st_f * jnp.exp(b_f[n * C + C - 1:n * C + C]) + kv_t[n * D_VDIM:(n + 1) * D_VDIM, :D_KDIM]
        states_b = [[None] * nc for _ in range(n_span)]
        for u in reversed(range(n_span)):
            _, kv_t, _, b_b, _ = local[u]
            for n in reversed(range(nc)):
                states_b[u][n] = st_b
                st_b = st_b * jnp.exp(b_b[n * C:n * C + 1]) + kv_t[n * D_VDIM:(n + 1) * D_VDIM, D_KDIM:]

        outs = []
        for u in range(n_span):
            intra, _, b_f, b_b, qs = local[u]
            q_in = jnp.concatenate([spread(qs * jnp.exp(b_f)), spread(qs * jnp.exp(b_b))], axis=1)
            outs.append(intra + _dot_nt(q_in, jnp.concatenate(states_f[u] + states_b[u], axis=1)))
        return (jnp.concatenate(outs, axis=0) if n_span > 1 else outs[0]), st_f, st_b

    for hh in range(o_ref.shape[1] // D_VDIM):
        cols = slice(hh * D_KDIM, (hh + 1) * D_KDIM)
        kf, lgf = gates(ff_ref[:, cols], lower_bound(lbf_ref[:, cols]))
        kb, lgb = gates(fb_ref[:, cols], lower_bound(lbb_ref[:, cols]))
        o, st_f, st_b = one_head(_silu(q_ref[:, cols]), i_ref[:, cols], kf, lgf, kb, lgb,
                                 jnp.transpose(s0f_ref[0, hh]), jnp.transpose(s0b_ref[0, hh]))
        sf_ref[0, hh] = jnp.transpose(st_f)
        sb_ref[0, hh] = jnp.transpose(st_b)
        o_ref[:, cols] = _rms(o, nd_ref[...]) * _silu(g_ref[:, cols])


def _hgrn(u, row_blk0, n_seq, L, lb_fwd, lb_bwd, norm_d, s0f, s0b, layer):
    hps = HGRN_HEADS_PER_STEP
    width = hps * D_KDIM
    col0 = 3 * C_DIM // width
    groups = D_HEADS // hps
    col = lambda j: pl.BlockSpec((L, width), lambda b, h: (row_blk0 + b, col0 + j * groups + h))
    lbs = pl.BlockSpec((DEPTH, width), lambda b, h: (0, h))
    st = pl.BlockSpec((1, hps, D_KDIM, D_VDIM), lambda b, h: (b, h, 0, 0))
    st_sd = jax.ShapeDtypeStruct((n_seq, D_HEADS, D_KDIM, D_VDIM), F32)
    return pl.pallas_call(
        functools.partial(_hgrn_body, layer=layer),
        grid=(n_seq, groups),
        in_specs=[col(0), col(1), col(2), col(3), col(4), lbs, lbs,
                  pl.BlockSpec((1, D_VDIM), lambda b, h: (0, 0)), st, st],
        out_specs=[pl.BlockSpec((L, width), lambda b, h: (b, h)), st, st],
        out_shape=[jax.ShapeDtypeStruct((n_seq * L, D_HEADS * D_VDIM), F32), st_sd, st_sd],
        compiler_params=_cparams("parallel", "parallel"),
        name="hgrn",
    )(u, u, u, u, u, lb_fwd, lb_bwd, norm_d.reshape(1, D_VDIM), s0f, s0b)


def _pack_bf16_pairs(h):
    n = h.shape[1] // 2
    hi = lax.bitcast_convert_type(h[:, :n].astype(BF16).astype(F32), jnp.int32)
    lo = lax.bitcast_convert_type(h[:, n:].astype(BF16).astype(F32), jnp.int32)
    return hi | lax.shift_right_logical(lo, 16)


def _unpack_bf16_pairs(p):
    hi = lax.bitcast_convert_type(p & jnp.int32(-65536), F32).astype(BF16)
    lo = lax.bitcast_convert_type(lax.shift_left(p, 16), F32).astype(BF16)
    return hi, lo


def _outproj_body(*refs, n_x):
    a_refs, b_refs, x_refs = refs[0:2], refs[2:4], refs[4:4 + n_x]
    mod_ref, gf_ref, w_ref, wrh_ref, wrl_ref, rb_ref, x1_ref, h2_ref, chosen_ref, gk_ref, ik_ref = refs[4 + n_x:]
    m = mod_ref[0]
    half = a_refs[0].shape[1]
    out = _dot(_token_tile(a_refs), w_ref[0:half, :]) + _dot(_token_tile(b_refs), w_ref[half:, :])
    x1 = _token_tile(x_refs) + m[:, 2 * D_MODEL:3 * D_MODEL] * out
    x1_ref[...] = x1
    h2 = _rms(x1, gf_ref[...]) * (1.0 + m[:, 4 * D_MODEL:5 * D_MODEL]) + m[:, 3 * D_MODEL:4 * D_MODEL]
    h2_ref[...] = _pack_bf16_pairs(h2)
    h_hi = h2.astype(BF16)
    h_lo = (h2 - h_hi.astype(F32)).astype(BF16)
    logits = _dot_nt(wrh_ref[...], h_hi) + _dot_nt(wrh_ref[...], h_lo) + _dot_nt(wrl_ref[...], h_hi)
    scores = jax.nn.sigmoid(logits)
    work = scores + rb_ref[...]
    expert = lax.broadcasted_iota(jnp.int32, work.shape, 0).astype(F32)
    slot = lax.broadcasted_iota(jnp.int32, (TOP_K, work.shape[1]), 0)
    chosen = [jnp.zeros(work.shape, F32) for _ in range(MOE_REGIONS)]
    gk = jnp.zeros((TOP_K, work.shape[1]), F32)
    ik = jnp.zeros((TOP_K, work.shape[1]), F32)
    for k in range(TOP_K):
        best = jnp.max(work, axis=0, keepdims=True)
        first = jnp.min(jnp.where(work == best, expert, float(N_EXPERTS)), axis=0, keepdims=True)
        hit = expert == first
        chosen[k // K_PER_REGION] = jnp.where(hit, 1.0, chosen[k // K_PER_REGION])
        gk = jnp.where(slot == k, jnp.sum(jnp.where(hit, scores, 0.0), axis=0, keepdims=True), gk)
        ik = jnp.where(slot == k, first, ik)
        work = jnp.where(hit, -jnp.inf, work)
    for r in range(MOE_REGIONS):
        chosen_ref[r] = chosen[r]
    gk_ref[...] = jnp.transpose(gk / jnp.sum(gk, axis=0, keepdims=True) * ROUTE_SCALE)
    ik_ref[...] = ik


def _outproj(a, b, x, mod_l, gain_ffn, w_out, w_router, router_bias):
    half = a[0].shape[1]
    a_specs, a_args = _token_specs(a, half)
    b_specs, b_args = _token_specs(b, half)
    x_specs, x_args = _token_specs(x, D_MODEL)
    wr_t = w_router.T
    wr_hi = wr_t.astype(BF16)
    wr_lo = (wr_t - wr_hi.astype(F32)).astype(BF16)
    return pl.pallas_call(
        functools.partial(_outproj_body, n_x=len(x_args)),
        grid=(T_ALL // TM,),
        in_specs=a_specs + b_specs + x_specs + [
                  pl.BlockSpec((1, 1, N_MOD * D_MODEL), lambda i: (_mod_row(i), 0, 0)),
                  pl.BlockSpec((1, D_MODEL), lambda i: (0, 0)),
                  pl.BlockSpec((2 * half, D_MODEL), lambda i: (0, 0)),
                  pl.BlockSpec((N_EXPERTS, D_MODEL), lambda i: (0, 0)),
                  pl.BlockSpec((N_EXPERTS, D_MODEL), lambda i: (0, 0)),
                  pl.BlockSpec((N_EXPERTS, 1), lambda i: (0, 0))],
        out_specs=[pl.BlockSpec((TM, D_MODEL), lambda i: (i, 0)),
                   pl.BlockSpec((TM, D_MODEL // 2), lambda i: (i, 0)),
                   pl.BlockSpec((MOE_REGIONS, N_EXPERTS, TM), lambda i: (0, 0, i)),
                   pl.BlockSpec((TM, TOP_K), lambda i: (i, 0)),
                   pl.BlockSpec((TOP_K, TM), lambda i: (0, i))],
        out_shape=[jax.ShapeDtypeStruct((T_ALL, D_MODEL), F32),
                   jax.ShapeDtypeStruct((T_ALL, D_MODEL // 2), jnp.int32),
                   jax.ShapeDtypeStruct((MOE_REGIONS, N_EXPERTS, T_ALL), F32),
                   jax.ShapeDtypeStruct((T_ALL, TOP_K), F32),
                   jax.ShapeDtypeStruct((TOP_K, T_ALL), F32)],
        compiler_params=_cparams("parallel"),
        name="outproj_router",
    )(*a_args, *b_args, *x_args, mod_l, gain_ffn.reshape(1, D_MODEL), w_out, wr_hi, wr_lo,
      router_bias.reshape(N_EXPERTS, 1))


def _route_body(chosen_ref, ik_ref, dest_ref, first_ref, count_ref, short_ref, pos_scr):
    n_tiles = T_ALL // TM
    r = lax.broadcasted_iota(jnp.int32, (TM, TM), 0)
    c = lax.broadcasted_iota(jnp.int32, (TM, TM), 1)
    before = (r < c).astype(BF16)

    counts = jnp.zeros((N_EXPERTS, 1), F32)
    for i in range(n_tiles):
        cols = slice(i * TM, (i + 1) * TM)
        m = chosen_ref[0, :, cols]
        pos_scr[:, cols] = jnp.dot(m.astype(BF16), before, preferred_element_type=F32) + counts
        counts = counts + jnp.sum(m, axis=1, keepdims=True)
    padded = jnp.ceil(counts * (1.0 / MOE_BLK)) * MOE_BLK
    ei = lax.broadcasted_iota(jnp.int32, (N_EXPERTS, N_EXPERTS), 0)
    ej = lax.broadcasted_iota(jnp.int32, (N_EXPERTS, N_EXPERTS), 1)
    end = _dot_hi((ej <= ei).astype(F32), jnp.broadcast_to(padded, (N_EXPERTS, LANES)))[:, 0:1]
    start = end - padded

    expert = lax.broadcasted_iota(jnp.int32, (N_EXPERTS, TM), 0).astype(F32)
    slot = lax.broadcasted_iota(jnp.int32, (K_PER_REGION, TM), 0)
    for i in range(n_tiles):
        cols = slice(i * TM, (i + 1) * TM)
        row_of = pos_scr[:, cols] + start
        ik = ik_ref[0, :, cols]
        acc = jnp.zeros((K_PER_REGION, TM), F32)
        for k in range(K_PER_REGION):
            pick = jnp.sum(jnp.where(expert == ik[k:k + 1, :], row_of, 0.0), axis=0, keepdims=True)
            acc = jnp.where(slot == k, pick, acc)
        dest_ref[0, :, cols] = acc.astype(jnp.int32)
    first_ref[0] = jnp.broadcast_to(start * (1.0 / MOE_BLK), (N_EXPERTS, LANES)).astype(jnp.int32)
    count_ref[0] = jnp.broadcast_to(padded * (1.0 / MOE_BLK), (N_EXPERTS, LANES)).astype(jnp.int32)
    in_last = counts - (padded - MOE_BLK)
    short = jnp.where((counts > 0.0) & (in_last <= MOE_BLK // 2), 1.0, 0.0)
    short_ref[0] = jnp.broadcast_to(short, (N_EXPERTS, LANES)).astype(jnp.int32)


def _route(chosen, ik):
    per_region = lambda rows, cols: pl.BlockSpec((1, rows, cols), lambda r: (r, 0, 0))
    table = jax.ShapeDtypeStruct((MOE_REGIONS, N_EXPERTS, LANES), jnp.int32)
    return pl.pallas_call(
        _route_body,
        grid=(MOE_REGIONS,),
        in_specs=[per_region(N_EXPERTS, T_ALL), per_region(K_PER_REGION, T_ALL)],
        out_specs=[per_region(K_PER_REGION, T_ALL)] + [per_region(N_EXPERTS, LANES)] * 3,
        out_shape=[jax.ShapeDtypeStruct((MOE_REGIONS, K_PER_REGION, T_ALL), jnp.int32), table, table, table],
        scratch_shapes=[pltpu.VMEM((N_EXPERTS, T_ALL), F32)],
        compiler_params=_cparams("arbitrary"),
        name="moe_route",
    )(chosen, ik.reshape(MOE_REGIONS, K_PER_REGION, T_ALL))


def _sc_worker_id():
    return lax.axis_index("s") * SC_CORES + lax.axis_index("c")


def _sc_dispatch(h2p, dest):
    n_chunks = T_ALL // DISP_CHUNK
    k_per = dest.shape[0] // DISP_SPLIT
    items_per_worker = n_chunks * DISP_SPLIT // SC_WORKERS
    chunk_stride = SC_WORKERS // DISP_SPLIT
    width = h2p.shape[1]
    mesh = plsc.VectorSubcoreMesh(core_axis_name="c", subcore_axis_name="s")

    @functools.partial(
        pl.kernel, mesh=mesh,
        out_type=jax.ShapeDtypeStruct((MOE_ROWS, width), jnp.int32),
        scratch_types=[pltpu.VMEM((k_per, DISP_CHUNK), jnp.int32), pltpu.VMEM((DISP_CHUNK, width), jnp.int32),
                       pltpu.SemaphoreType.DMA],
    )
    def run(x_hbm, dest_hbm, xs_hbm, idx_v, rows_v, sem):
        wid = _sc_worker_id()
        group = wid % DISP_SPLIT
        for i in range(items_per_worker):
            chunk = i * chunk_stride + wid // DISP_SPLIT
            tokens = pl.ds(pl.multiple_of(chunk * DISP_CHUNK, DISP_CHUNK), DISP_CHUNK)
            pltpu.sync_copy(dest_hbm.at[group, :, tokens], idx_v)
            pltpu.sync_copy(x_hbm.at[tokens], rows_v)
            scatters = [pltpu.make_async_copy(rows_v, xs_hbm.at[idx_v.at[k]], sem) for k in range(k_per)]
            for cp in scatters:
                cp.start()
            for cp in scatters:
                cp.wait()

    return run(h2p, dest.reshape(DISP_SPLIT, k_per, T_ALL))


def _sc_collect(y, dest_flat):
    n_k = dest_flat.shape[0] // T_ALL
    per_worker = T_ALL // SC_WORKERS
    n_chunks = per_worker // COLLECT_CHUNK
    n_steps = n_k * n_chunks
    width = y.shape[1]
    mesh = plsc.VectorSubcoreMesh(core_axis_name="c", subcore_axis_name="s")

    @functools.partial(
        pl.kernel, mesh=mesh,
        out_type=jax.ShapeDtypeStruct((n_k * T_ALL, width), y.dtype),
        scratch_types=[pltpu.VMEM((n_k * per_worker,), jnp.int32),
                       pltpu.VMEM((COLLECT_CHUNK, width), y.dtype), pltpu.VMEM((COLLECT_CHUNK, width), y.dtype),
                       pltpu.SemaphoreType.DMA, pltpu.SemaphoreType.DMA],
    )
    def run(y_hbm, dest_hbm, yg_hbm, idx_v, rows0, rows1, sem0, sem1):
        wid = _sc_worker_id()
        bufs = ((rows0, sem0), (rows1, sem1))
        for k in range(n_k):
            pltpu.sync_copy(dest_hbm.at[pl.ds(k * T_ALL + wid * per_worker, per_worker)],
                            idx_v.at[pl.ds(k * per_worker, per_worker)])

        def gather(step, buf):
            rows, sem = buf
            idx = idx_v.at[pl.ds(pl.multiple_of(step * COLLECT_CHUNK, 8), COLLECT_CHUNK)]
            return pltpu.make_async_copy(y_hbm.at[idx], rows, sem)

        def out_rows(step):
            off = (step // n_chunks) * T_ALL + wid * per_worker + (step % n_chunks) * COLLECT_CHUNK
            return yg_hbm.at[pl.ds(pl.multiple_of(off, 8), COLLECT_CHUNK)]

        gather(0, bufs[0]).start()

        @pl.loop(0, n_steps, step=2)
        def _(base):
            for j in range(2):
                step = base + j

                @pl.when(step + 1 < n_steps)
                def _():
                    gather(step + 1, bufs[1 - j]).start()

                gather(step, bufs[j]).wait()
                pltpu.sync_copy(bufs[j][0], out_rows(step))

    return run(y, dest_flat)


def _expert_body(first_ref, count_ref, short_ref, xs_hbm, wg_hbm, wu_hbm, wd_hbm, y_hbm,
                 wg_f32, wu_f32, wd_f32, wg_bf, wu_bf, wd_bf, x_buf, y_buf, w_sem, in_sem, out_sem, *, layer):
    e = pl.program_id(0)
    first = first_ref[e]
    count = count_ref[e]
    n_used = first_ref[N_EXPERTS - 1] + count_ref[N_EXPERTS - 1]
    half = D_MODEL // 2

    def weight_copies(ex):
        slot = lax.rem(ex, EXPERT_W_SLOTS)
        out = []
        for src, dst in ((wg_hbm, wg_f32), (wu_hbm, wu_f32), (wd_hbm, wd_f32)):
            size = dst.shape[1] // EXPERT_W_PARTS
            for part in range(EXPERT_W_PARTS):
                rows = pl.ds(part * size, size)
                out.append(pltpu.make_async_copy(src.at[layer, ex, rows], dst.at[slot, rows], w_sem.at[slot]))
        return out

    @pl.when(e == 0)
    def _():
        for ahead in range(EXPERT_W_SLOTS - 1):
            for cp in weight_copies(ahead):
                cp.start()

    for cp in weight_copies(e):
        cp.wait()

    @pl.when(e + EXPERT_W_SLOTS - 1 < N_EXPERTS)
    def _():
        for cp in weight_copies(e + EXPERT_W_SLOTS - 1):
            cp.start()

    w_slot = lax.rem(e, EXPERT_W_SLOTS)
    wg_bf[...] = wg_f32[w_slot].astype(BF16)
    wu_bf[...] = wu_f32[w_slot].astype(BF16)
    wd_bf[...] = wd_f32[w_slot].astype(BF16)

    def part_rows(g, part, n_parts):
        size = MOE_BLK // n_parts
        return pl.ds(pl.multiple_of(g * MOE_BLK + part * size, size), size), pl.ds(part * size, size)

    def in_copies(g):
        slot = g & (EXPERT_SLOTS - 1)
        out = []
        for part in range(EXPERT_IN_PARTS):
            src, dst = part_rows(g, part, EXPERT_IN_PARTS)
            out.append(pltpu.make_async_copy(xs_hbm.at[src], x_buf.at[slot, dst], in_sem.at[slot]))
        return out

    def out_copies(g):
        slot = g & (EXPERT_SLOTS - 1)
        out = []
        for part in range(EXPERT_OUT_PARTS):
            dst, src = part_rows(g, part, EXPERT_OUT_PARTS)
            out.append(pltpu.make_async_copy(y_buf.at[slot, src], y_hbm.at[dst], out_sem.at[slot]))
        return out

    @pl.when((first == 0) & (count > 0))
    def _():
        for ahead in range(EXPERT_SLOTS - 1):
            @pl.when(ahead < n_used)
            def _():
                for cp in in_copies(ahead):
                    cp.start()

    def block(b, carry):
        g = first + b
        slot = g & (EXPERT_SLOTS - 1)
        for cp in in_copies(g):
            cp.wait()

        @pl.when(g + EXPERT_SLOTS - 1 < n_used)
        def _():
            for cp in in_copies(g + EXPERT_SLOTS - 1):
                cp.start()

        @pl.when(g >= EXPERT_SLOTS)
        def _():
            for cp in out_copies(g - EXPERT_SLOTS):
                cp.wait()

        def ffn(n_rows):
            hi, lo = _unpack_bf16_pairs(x_buf[slot, 0:n_rows])

            def proj(w_bf):
                return (jnp.dot(hi, w_bf[0:half, :], preferred_element_type=F32)
                        + jnp.dot(lo, w_bf[half:, :], preferred_element_type=F32))

            hid = _silu(proj(wg_bf)) * proj(wu_bf)
            y_buf[slot, 0:n_rows] = _pack_bf16_pairs(
                jnp.dot(hid.astype(BF16), wd_bf[...], preferred_element_type=F32))

        short = (b == count - 1) & (short_ref[e] == 1)

        @pl.when(short)
        def _():
            ffn(MOE_BLK // 2)
            y_buf[slot, MOE_BLK // 2:MOE_BLK] = jnp.zeros((MOE_BLK // 2, D_MODEL // 2), jnp.int32)

        @pl.when(jnp.logical_not(short))
        def _():
            ffn(MOE_BLK)

        for cp in out_copies(g):
            cp.start()
        return carry

    lax.fori_loop(0, count, block, 0)

    @pl.when(e == N_EXPERTS - 1)
    def _():
        for back in range(EXPERT_SLOTS, 0, -1):
            @pl.when(n_used >= back)
            def _():
                for cp in out_copies(n_used - back):
                    cp.wait()


EXPERT_SLOTS = 4
EXPERT_W_SLOTS = 4
EXPERT_W_PARTS = 2
EXPERT_IN_PARTS = 2
EXPERT_OUT_PARTS = 4


def _experts(first_blk, n_blk, short_last, xs, layer, w_gate, w_up, w_down):
    anywhere = pl.BlockSpec(memory_space=pl.ANY)
    grid_spec = pltpu.PrefetchScalarGridSpec(
        num_scalar_prefetch=3,
        grid=(N_EXPERTS,),
        in_specs=[anywhere] * 4,
        out_specs=anywhere,
        scratch_shapes=[pltpu.VMEM((EXPERT_W_SLOTS, D_MODEL, D_EXPERT), F32),
                        pltpu.VMEM((EXPERT_W_SLOTS, D_MODEL, D_EXPERT), F32),
                        pltpu.VMEM((EXPERT_W_SLOTS, D_EXPERT, D_MODEL), F32),
                        pltpu.VMEM((D_MODEL, D_EXPERT), BF16), pltpu.VMEM((D_MODEL, D_EXPERT), BF16),
                        pltpu.VMEM((D_EXPERT, D_MODEL), BF16),
                        pltpu.VMEM((EXPERT_SLOTS, MOE_BLK, D_MODEL // 2), jnp.int32),
                        pltpu.VMEM((EXPERT_SLOTS, MOE_BLK, D_MODEL // 2), jnp.int32),
                        pltpu.SemaphoreType.DMA((EXPERT_W_SLOTS,)),
                        pltpu.SemaphoreType.DMA((EXPERT_SLOTS,)), pltpu.SemaphoreType.DMA((EXPERT_SLOTS,))],
    )
    return pl.pallas_call(
        functools.partial(_expert_body, layer=layer),
        grid_spec=grid_spec,
        out_shape=jax.ShapeDtypeStruct((MOE_ROWS, D_MODEL // 2), jnp.int32),
        compiler_params=_cparams("arbitrary"),
        name="moe_experts",
    )(first_blk, n_blk, short_last, xs, w_gate, w_up, w_down)


def _shared_body(h2_ref, sg_ref, su_ref, sd_ref, o_ref):
    hi, lo = _unpack_bf16_pairs(h2_ref[...])
    half = D_MODEL // 2

    def proj(w_ref):
        return _dot(hi, w_ref[0:half, :]) + _dot(lo, w_ref[half:, :])

    o_ref[...] = _dot(_silu(proj(sg_ref)) * proj(su_ref), sd_ref[...])


def _shared_ffn(h2p, ws_gate, ws_up, ws_down):
    tok = lambda shape: pl.BlockSpec(shape, lambda i: (i, 0))
    full = lambda shape: pl.BlockSpec(shape, lambda i: (0, 0))
    return pl.pallas_call(
        _shared_body,
        grid=(T_ALL // TM,),
        in_specs=[tok((TM, D_MODEL // 2)), full((D_MODEL, D_EXPERT)), full((D_MODEL, D_EXPERT)),
                  full((D_EXPERT, D_MODEL))],
        out_specs=tok((TM, D_MODEL)),
        out_shape=jax.ShapeDtypeStruct((T_ALL, D_MODEL), F32),
        compiler_params=_cparams("parallel"),
        name="moe_shared",
    )(h2p, ws_gate, ws_up, ws_down)


def _combine_body(x1_ref, ys_ref, *refs, final):
    yg_refs = refs[:MOE_REGIONS]
    gk_ref, mod_ref, fn_ref, *o_refs = refs[MOE_REGIONS:]
    half = D_MODEL // 2
    acc_hi, acc_lo = ys_ref[:, :half], ys_ref[:, half:]
    gk = gk_ref[...]
    for k in range(TOP_K):
        y_hi, y_lo = _unpack_bf16_pairs(yg_refs[k // K_PER_REGION][k % K_PER_REGION])
        acc_hi = acc_hi + gk[:, k:k + 1] * y_hi.astype(F32)
        acc_lo = acc_lo + gk[:, k:k + 1] * y_lo.astype(F32)
    acc = jnp.concatenate([acc_hi, acc_lo], axis=1)
    m = mod_ref[0]
    y = x1_ref[...] + m[:, 5 * D_MODEL:6 * D_MODEL] * acc
    if not final:
        o_refs[0][...] = y
        return
    y = _rms(y, fn_ref[...])
    is_ctx = pl.program_id(0) < N_CTX_TILES

    @pl.when(is_ctx)
    def _():
        o_refs[0][...] = y

    @pl.when(jnp.logical_not(is_ctx))
    def _():
        o_refs[1][...] = y


def _combine(x1, ys, yg, gk, mod_l, final_norm, final):
    tok = lambda shape: pl.BlockSpec(shape, lambda i: (i, 0))
    full = lambda shape: pl.BlockSpec(shape, lambda i: (0, 0))
    if final:
        out_specs, _ = _token_specs((None, None), D_MODEL)
        out_shape = [jax.ShapeDtypeStruct((T_CTX, D_MODEL), F32), jax.ShapeDtypeStruct((T_LAT, D_MODEL), F32)]
    else:
        out_specs = tok((TM, D_MODEL))
        out_shape = jax.ShapeDtypeStruct((T_ALL, D_MODEL), F32)
    return pl.pallas_call(
        functools.partial(_combine_body, final=final),
        grid=(T_ALL // TM,),
        in_specs=[tok((TM, D_MODEL)), tok((TM, D_MODEL))]
                 + [pl.BlockSpec((K_PER_REGION, TM, D_MODEL // 2), lambda i: (0, i, 0))] * MOE_REGIONS
                 + [tok((TM, TOP_K)),
                  pl.BlockSpec((1, 1, N_MOD * D_MODEL), lambda i: (_mod_row(i), 0, 0)),
                  full((1, D_MODEL))],
        out_specs=out_specs,
        out_shape=out_shape,
        compiler_params=_cparams("arbitrary"),
        name="moe_combine",
    )(x1, ys, *yg, gk, mod_l, final_norm.reshape(1, D_MODEL))


def _moe(x1, h2p, chosen, gk, ik, mod_l, layer, w_gate, w_up, w_down, ws_gate, ws_up, ws_down, final_norm, final):
    dest, first_blk, n_blk, short_last = _route(chosen, ik)
    ys = _shared_ffn(h2p, ws_gate, ws_up, ws_down)
    yg = []
    for r in range(MOE_REGIONS):
        xs = _sc_dispatch(h2p, dest[r])
        y = _experts(first_blk[r, :, 0], n_blk[r, :, 0], short_last[r, :, 0], xs, layer, w_gate, w_up, w_down)
        yg.append(_sc_collect(y, dest[r].reshape(-1)).reshape(K_PER_REGION, T_ALL, D_MODEL // 2))
    return _combine(x1, ys, yg, gk, mod_l, final_norm, final)


def kernel(x_prompt, x_sample, cache_a_k, cache_a_v, cache_b_k, cache_b_v, state_d_fwd, state_d_bwd, c, c_ctx, w_ada, b_ada, norm_mix, norm_ffn, w_in_attn, w_out_attn, sink_a, rpb_b, w_in_rec, w_out_rec, conv_w, conv_b, filt_w1, filt_b1, filt_w2, filt_b2, filt_w3, filt_b3, filt_freq, filt_w4, d_skip, lb_fwd, lb_bwd, norm_d, w_router, router_bias, w_gate, w_up, w_down, ws_gate, ws_up, ws_down, final_norm):
    x = (x_prompt.reshape(T_CTX, D_MODEL), x_sample.reshape(T_LAT, D_MODEL))
    cvec = jnp.concatenate([c_ctx[None, :], c], axis=0)
    c_lanes = jnp.broadcast_to(cvec[:, :, None], (N_CVEC, D_MODEL, LANES))
    mod = [_ada(c_lanes, l, w_ada, b_ada).reshape(CVEC_PAD, 1, N_MOD * D_MODEL) for l in range(DEPTH)]

    new_kv = None
    new_state = None
    for l in range(DEPTH):
        j = l // 2
        final = l == DEPTH - 1
        if l % 2 == 0:
            qkv = _inproj(x, mod[l], norm_mix[l], w_in_attn[j])
            oa_ctx, ob_ctx, *new_kv = _ctx_attn(qkv, sink_a[j])
            new_kv = tuple(new_kv)
            q_rot, k_rot = _rope(qkv)
            cache = lambda t: t[:, j].reshape(DEC_BATCH, PAST_LEN, -1)
            oa_lat = _win_attn(qkv, q_rot, k_rot, cache(cache_a_k), cache(cache_a_v), sink_a[j])
            ob_lat = _na_attn(qkv, cache(cache_b_k), cache(cache_b_v), _na_rel_rows(rpb_b[j]))
            mix_a = (oa_ctx, oa_lat)
            mix_b = (ob_ctx, ob_lat)
            w_out = w_out_attn[j]
        else:
            u = _inproj(x, mod[l], norm_mix[l], w_in_rec[j])
            filt = (filt_w1[j], filt_b1[j], filt_w2[j], filt_b2[j], filt_w3[j], filt_b3[j], filt_freq[j],
                    filt_w4[j])
            y_ctx = _hyena(u, 0, BATCH, SEQ, conv_w[j], conv_b[j], d_skip[j], _hyena_filter(SEQ, filt))
            y_lat = _hyena(u, T_CTX // DEC_SEQ, DEC_BATCH, DEC_SEQ, conv_w[j], conv_b[j], d_skip[j],
                           _hyena_filter(DEC_SEQ, filt))
            zeros = jnp.zeros((BATCH, D_HEADS, D_KDIM, D_VDIM), F32)
            o_ctx, s_f, s_b = _hgrn(u, 0, BATCH, SEQ, lb_fwd, lb_bwd, norm_d[j], zeros, zeros, l)
            o_lat, _, _ = _hgrn(u, T_CTX // DEC_SEQ, DEC_BATCH, DEC_SEQ, lb_fwd, lb_bwd, norm_d[j],
                                state_d_fwd[:, j], state_d_bwd[:, j], l)
            new_state = (s_f[:, None], s_b[:, None])
            mix_a = (y_ctx, y_lat)
            mix_b = (o_ctx, o_lat)
            w_out = w_out_rec[j]
        x1, h2p, chosen, gk, ik = _outproj(mix_a, mix_b, x, mod[l], norm_ffn[l], w_out, w_router[l],
                                           router_bias[l])
        x = _moe(x1, h2p, chosen, gk, ik, mod[l], l, w_gate, w_up, w_down, ws_gate[l], ws_up[l],
                 ws_down[l], final_norm, final)

    y_prompt = x[0].reshape(BATCH, SEQ, D_MODEL)
    y_sample = x[1].reshape(DEC_BATCH, DEC_SEQ, D_MODEL)
    return (y_prompt, y_sample) + new_kv + new_state
```

```python
import functools
import math

import numpy as np
import jax
import jax.numpy as jnp
from jax import lax
from jax.experimental import pallas as pl
from jax.experimental.pallas import tpu as pltpu
from jax.experimental.pallas import tpu_sc as plsc

F32 = jnp.float32
BF16 = jnp.bfloat16
HI = lax.Precision.HIGHEST

D_MODEL = 1024
BATCH = 16
SEQ = 256
DEPTH = 2
DEC_BATCH = 2
DEC_SEQ = 1024
PAST_LEN = 512
GRID_W = 64
HEAD_DIM = 64
N_MOD = 6
RMS_EPS = 1e-6
A_HEADS = 8
A_KV_HEADS = 2
A_GROUP = A_HEADS // A_KV_HEADS
WINDOW = 128
ROPE_BASE = 10000.0
B_HEADS = 8
NA_ROWS = 8
NA_COLS = 16
C_DIM = 512
C_EMB = 33
C_FFN = 64
HYENA_MIN_DECAY = math.log(1e-2) / 1.5
HYENA_MAX_DECAY = math.log(1e-2) / 0.3
D_KDIM = 128
D_VDIM = 128
D_HEADS = 4
N_EXPERTS = 64
TOP_K = 8
D_EXPERT = 256
ROUTE_SCALE = 2.5
A_Q = A_HEADS * HEAD_DIM
A_KV = A_KV_HEADS * HEAD_DIM
B_W = B_HEADS * HEAD_DIM
ATTN_IN = A_Q + 2 * A_KV + 3 * B_W

T_CTX = BATCH * SEQ
T_LAT = DEC_BATCH * DEC_SEQ
T_ALL = T_CTX + T_LAT
N_CVEC = 1 + DEC_BATCH
CVEC_PAD = 8
TM = 512
MASK_NEG = -1e30
GLA_CHUNK = 64
GLA_SPAN = 256
GLA_MIN_SPANS = 2
HGRN_HEADS_PER_STEP = 4
DFT_CHUNK = 512
MOE_BLK = 512
MOE_REGIONS = 1
K_PER_REGION = TOP_K // MOE_REGIONS
MOE_NBLK = -(-(T_ALL * K_PER_REGION + N_EXPERTS * (MOE_BLK - 1)) // MOE_BLK)
MOE_ROWS = MOE_NBLK * MOE_BLK
SC_CORES = 2
SC_SUBCORES = 16
SC_WORKERS = SC_CORES * SC_SUBCORES
DISP_CHUNK = 128
DISP_SPLIT = 2
COLLECT_CHUNK = 64
VMEM_LIMIT = 56 * 1024 * 1024


def _cparams(*sem):
    return pltpu.CompilerParams(dimension_semantics=sem, vmem_limit_bytes=VMEM_LIMIT)


def _mod_row(i):
    return jnp.where(i < T_CTX // TM, 0, 1 + (i - T_CTX // TM) // (DEC_SEQ // TM))


def _dot(a, b):
    return jnp.dot(a.astype(BF16), b.astype(BF16), preferred_element_type=F32)


def _dot_nt(a, b):
    return lax.dot_general(a.astype(BF16), b.astype(BF16), (((1,), (1,)), ((), ())),
                           preferred_element_type=F32)


def _dot_tn(a, b):
    return lax.dot_general(a.astype(BF16), b.astype(BF16), (((0,), (0,)), ((), ())),
                           preferred_element_type=F32)


def _dot_hi(a, b):
    return jnp.dot(a, b, precision=HI, preferred_element_type=F32)


def _split_bf16(x):
    hi = x.astype(BF16)
    return hi, (x - hi.astype(F32)).astype(BF16)


def _dot_split(a, b):
    a_hi, a_lo = _split_bf16(a)
    b_hi, b_lo = _split_bf16(b)
    dot = lambda x, y: jnp.dot(x, y, preferred_element_type=F32)
    return dot(a_hi, b_hi) + dot(a_hi, b_lo) + dot(a_lo, b_hi)


def _silu(x):
    return x * jax.nn.sigmoid(x)


def _rms(x, g):
    return x * lax.rsqrt(jnp.mean(x * x, axis=-1, keepdims=True) + RMS_EPS) * g


ADA_TN = 1536
ADA_UNROLL = 4


def _ada_body(cb_ref, w_ref, b_ref, o_ref):
    tn = o_ref.shape[-1]
    n_slab = tn // LANES

    def step(k8, accs):
        r0 = pl.multiple_of(k8 * 8, 8)
        sk = [_silu(cb_ref[j, pl.ds(r0, 8), :]) for j in range(N_CVEC)]
        out = []
        for s in range(n_slab):
            wk = w_ref[0, pl.ds(r0, 8), s * LANES:(s + 1) * LANES]
            out.extend(accs[s * N_CVEC + j] + wk * sk[j] for j in range(N_CVEC))
        return tuple(out)

    accs = lax.fori_loop(0, D_MODEL // 8, step,
                         tuple(jnp.zeros((8, LANES), F32) for _ in range(n_slab * N_CVEC)), unroll=ADA_UNROLL)
    o_ref[0] = jnp.zeros((CVEC_PAD, tn), F32)
    for s in range(n_slab):
        for j in range(N_CVEC):
            o_ref[0, j:j + 1, s * LANES:(s + 1) * LANES] = (
                jnp.sum(accs[s * N_CVEC + j], axis=0, keepdims=True) + b_ref[0, :, s * LANES:(s + 1) * LANES])


def _ada(c_lanes, layer, w_ada, b_ada):
    n_out = N_MOD * D_MODEL
    return pl.pallas_call(
        _ada_body,
        grid=(n_out // ADA_TN,),
        in_specs=[pl.BlockSpec((N_CVEC, D_MODEL, LANES), lambda n: (0, 0, 0)),
                  pl.BlockSpec((1, D_MODEL, ADA_TN), lambda n: (layer, 0, n)),
                  pl.BlockSpec((1, 1, ADA_TN), lambda n: (layer, 0, n))],
        out_specs=pl.BlockSpec((1, CVEC_PAD, ADA_TN), lambda n: (0, 0, n)),
        out_shape=jax.ShapeDtypeStruct((1, CVEC_PAD, n_out), F32),
        compiler_params=_cparams("parallel"),
        name="ada",
    )(c_lanes, w_ada, b_ada.reshape(DEPTH, 1, n_out))


N_CTX_TILES = T_CTX // TM


def _token_specs(x, width):
    if not isinstance(x, tuple):
        return [pl.BlockSpec((TM, width), lambda i: (i, 0))], (x,)
    return ([pl.BlockSpec((TM, width), lambda i: (jnp.minimum(i, N_CTX_TILES - 1), 0)),
             pl.BlockSpec((TM, width), lambda i: (jnp.maximum(i - N_CTX_TILES, 0), 0))], x)


def _token_tile(refs):
    if len(refs) == 1:
        return refs[0][...]
    return jnp.where(pl.program_id(0) < N_CTX_TILES, refs[0][...], refs[1][...])


def _inproj_body(*refs, n_x):
    x_refs, (mod_ref, g_ref, w_ref, o_ref, w_bf) = refs[:n_x], refs[n_x:]

    @pl.when(pl.program_id(0) == 0)
    def _():
        w_bf[...] = w_ref[...].astype(BF16)

    m = mod_ref[0]
    h = _rms(_token_tile(x_refs), g_ref[...]) * (1.0 + m[:, D_MODEL:2 * D_MODEL]) + m[:, 0:D_MODEL]
    o_ref[...] = _dot(h, w_bf[...])


def _inproj(x, mod_l, gain, w):
    n = w.shape[1]
    x_specs, x_args = _token_specs(x, D_MODEL)
    return pl.pallas_call(
        functools.partial(_inproj_body, n_x=len(x_args)),
        grid=(T_ALL // TM,),
        in_specs=x_specs + [pl.BlockSpec((1, 1, N_MOD * D_MODEL), lambda i: (_mod_row(i), 0, 0)),
                            pl.BlockSpec((1, D_MODEL), lambda i: (0, 0)),
                            pl.BlockSpec((D_MODEL, n), lambda i: (0, 0), pipeline_mode=pl.Buffered(1))],
        out_specs=pl.BlockSpec((TM, n), lambda i: (i, 0)),
        out_shape=jax.ShapeDtypeStruct((T_ALL, n), F32),
        scratch_shapes=[pltpu.VMEM((D_MODEL, n), BF16)],
        compiler_params=_cparams("arbitrary"),
        name="inproj",
    )(*x_args, mod_l, gain.reshape(1, D_MODEL), w)


def _ctx_attn_body(qkv_ref, sink_ref, oa_ref, ob_ref, ak_ref, av_ref, bk_ref, bv_ref):
    scale = HEAD_DIM ** -0.5
    lane = lax.broadcasted_iota(jnp.int32, (SEQ, LANES), 1)
    in_half = [lane < HEAD_DIM, lane >= HEAD_DIM]

    def attend(q, k, v, sink):
        s = _dot_nt(q, k) * scale
        m = jnp.max(s, axis=-1, keepdims=True)
        if sink is not None:
            m = jnp.maximum(m, sink)
        p = jnp.exp(s - m)
        den = jnp.sum(p, axis=-1, keepdims=True)
        if sink is not None:
            den = den + jnp.exp(sink - m)
        return _dot(p, v) / den

    def tile(first_col, t):
        return qkv_ref[:, first_col + t * LANES:first_col + (t + 1) * LANES]

    base = A_Q + 2 * A_KV
    for hk in range(A_KV_HEADS):
        dst = pl.ds(hk, SEQ, stride=A_KV_HEADS)
        ak_ref[0, dst, :] = qkv_ref[:, A_Q + hk * HEAD_DIM:A_Q + (hk + 1) * HEAD_DIM]
        av_ref[0, dst, :] = qkv_ref[:, A_Q + A_KV + hk * HEAD_DIM:A_Q + A_KV + (hk + 1) * HEAD_DIM]
    for h in range(B_HEADS):
        dst = pl.ds(h, SEQ, stride=B_HEADS)
        bk_ref[0, dst, :] = qkv_ref[:, base + B_W + h * HEAD_DIM:base + B_W + (h + 1) * HEAD_DIM]
        bv_ref[0, dst, :] = qkv_ref[:, base + 2 * B_W + h * HEAD_DIM:base + 2 * B_W + (h + 1) * HEAD_DIM]

    k_t, v_t = tile(A_Q, 0), tile(A_Q + A_KV, 0)
    k_sw, v_sw = pltpu.roll(k_t, HEAD_DIM, axis=1), pltpu.roll(v_t, HEAD_DIM, axis=1)
    tiles_per_kv = A_GROUP // HEADS_PER_TILE
    for hk in range(A_KV_HEADS):
        q_tiles = [tile(0, hk * tiles_per_kv + j) for j in range(tiles_per_kv)]
        halves = []
        for p in range(HEADS_PER_TILE):
            q = jnp.concatenate([jnp.where(in_half[p], qt, 0.0) for qt in q_tiles], axis=0)
            heads = [(hk * tiles_per_kv + j) * HEADS_PER_TILE + p for j in range(tiles_per_kv)]
            sink = jnp.concatenate([jnp.broadcast_to(sink_ref[:, h:h + 1], (SEQ, 1)) for h in heads], axis=0)
            halves.append(attend(q, k_t if p == hk else k_sw, v_t if p == hk else v_sw, sink))
        first_half = lax.broadcasted_iota(jnp.int32, halves[0].shape, 1) < HEAD_DIM
        o = jnp.where(first_half, halves[0], halves[1])
        for j in range(tiles_per_kv):
            t = hk * tiles_per_kv + j
            oa_ref[:, t * LANES:(t + 1) * LANES] = o[j * SEQ:(j + 1) * SEQ]

    for t in range(B_HEADS // HEADS_PER_TILE):
        q_t, k_b, v_b = tile(base, t), tile(base + B_W, t), tile(base + 2 * B_W, t)
        halves = [attend(jnp.where(in_half[p], q_t, 0.0), k_b, v_b, None) for p in range(HEADS_PER_TILE)]
        ob_ref[:, t * LANES:(t + 1) * LANES] = jnp.where(in_half[0], halves[0], halves[1])


def _ctx_attn(qkv, sink):
    kv_spec = lambda heads: pl.BlockSpec((1, SEQ * heads, HEAD_DIM), lambda b: (b, 0, 0))
    kv_sd = lambda heads: jax.ShapeDtypeStruct((BATCH, SEQ * heads, HEAD_DIM), F32)
    outs = pl.pallas_call(
        _ctx_attn_body,
        grid=(BATCH,),
        in_specs=[pl.BlockSpec((SEQ, ATTN_IN), lambda b: (b, 0)),
                  pl.BlockSpec((1, A_HEADS), lambda b: (0, 0))],
        out_specs=[pl.BlockSpec((SEQ, A_Q), lambda b: (b, 0)), pl.BlockSpec((SEQ, B_W), lambda b: (b, 0)),
                   kv_spec(A_KV_HEADS), kv_spec(A_KV_HEADS), kv_spec(B_HEADS), kv_spec(B_HEADS)],
        out_shape=[jax.ShapeDtypeStruct((T_CTX, A_Q), F32), jax.ShapeDtypeStruct((T_CTX, B_W), F32),
                   kv_sd(A_KV_HEADS), kv_sd(A_KV_HEADS), kv_sd(B_HEADS), kv_sd(B_HEADS)],
        compiler_params=_cparams("parallel"),
        name="ctx_attn",
    )(qkv, sink.reshape(1, A_HEADS))
    caches = [t.reshape(BATCH, 1, SEQ, -1, HEAD_DIM) for t in outs[2:]]
    return outs[0], outs[1], *caches


@functools.lru_cache(maxsize=None)
def _rope_tables(width):
    half = HEAD_DIM // 2
    t = np.arange(DEC_SEQ)
    inv = ROPE_BASE ** (-np.arange(0, half, 2, dtype=np.float64) / half)
    ang_r = (t // GRID_W)[:, None] * inv[None, :]
    ang_c = (t % GRID_W)[:, None] * inv[None, :]
    cos = np.concatenate([np.cos(ang_r)] * 2 + [np.cos(ang_c)] * 2, axis=-1)
    sin = np.concatenate([-np.sin(ang_r), np.sin(ang_r), -np.sin(ang_c), np.sin(ang_c)], axis=-1)
    reps = width // HEAD_DIM
    return (np.tile(cos, (1, reps)).astype(np.float32), np.tile(sin, (1, reps)).astype(np.float32))


def _rope_body(q_ref, k_ref, cq_ref, sq_ref, ck_ref, sk_ref, qo_ref, ko_ref):
    quarter = HEAD_DIM // 4

    def rot(x, cos, sin):
        w = x.shape[-1]
        lane = lax.broadcasted_iota(jnp.int32, x.shape, 1)
        fwd = pltpu.roll(x, w - quarter, axis=1)
        bwd = pltpu.roll(x, quarter, axis=1)
        partner = jnp.where((lane & (2 * quarter - 1)) < quarter, fwd, bwd)
        return x * cos + partner * sin

    qo_ref[...] = rot(q_ref[...], cq_ref[...], sq_ref[...])
    ko_ref[...] = rot(k_ref[...], ck_ref[...], sk_ref[...])


def _rope(qkv):
    cq, sq = _rope_tables(A_Q)
    ck, sk = _rope_tables(A_KV)
    tab = lambda w: pl.BlockSpec((DEC_SEQ, w), lambda b: (0, 0))
    row0 = T_CTX // DEC_SEQ
    return pl.pallas_call(
        _rope_body,
        grid=(DEC_BATCH,),
        in_specs=[pl.BlockSpec((DEC_SEQ, A_Q), lambda b: (row0 + b, 0)),
                  pl.BlockSpec((DEC_SEQ, A_KV), lambda b: (row0 + b, A_Q // A_KV)),
                  tab(A_Q), tab(A_Q), tab(A_KV), tab(A_KV)],
        out_specs=[pl.BlockSpec((DEC_SEQ, A_Q), lambda b: (b, 0)),
                   pl.BlockSpec((DEC_SEQ, A_KV), lambda b: (b, 0))],
        out_shape=[jax.ShapeDtypeStruct((T_LAT, A_Q), F32), jax.ShapeDtypeStruct((T_LAT, A_KV), F32)],
        compiler_params=_cparams("parallel"),
        name="rope",
    )(qkv, qkv, jnp.asarray(cq), jnp.asarray(sq), jnp.asarray(ck), jnp.asarray(sk))


WIN_QB = 128


def _win_attn_body(qraw_ref, qrot_ref, krot_ref, v_ref, kc_ref, vc_ref, sink_ref, o_ref):
    scale = HEAD_DIM ** -0.5
    hk = pl.program_id(1)
    tiles = A_GROUP // HEADS_PER_TILE

    def kv_in_half(x):
        swapped = pltpu.roll(x, HEAD_DIM, axis=1)
        return [jnp.where(hk == p, x, swapped) for p in range(HEADS_PER_TILE)]

    k, v, kc, vc = kv_in_half(krot_ref[...]), kv_in_half(v_ref[...]), kv_in_half(kc_ref[0]), kv_in_half(vc_ref[0])
    head_lane = lax.broadcasted_iota(jnp.int32, (1, A_HEADS), 1)

    def sink_rows(p):
        heads = [hk * A_GROUP + j * HEADS_PER_TILE + p for j in range(tiles)]
        vals = [jnp.sum(jnp.where(head_lane == h, sink_ref[...], 0.0), axis=-1, keepdims=True) for h in heads]
        return jnp.concatenate([jnp.broadcast_to(s, (WIN_QB, 1)) for s in vals], axis=0)

    sinks = [sink_rows(p) for p in range(HEADS_PER_TILE)]
    lane = lax.broadcasted_iota(jnp.int32, (tiles * WIN_QB, LANES), 1)
    in_half = [lane < HEAD_DIM, lane >= HEAD_DIM]
    for qb in range(DEC_SEQ // WIN_QB):
        q0 = qb * WIN_QB
        rows = slice(q0, q0 + WIN_QB)
        lo = max(0, q0 - WINDOW)
        hi = min(DEC_SEQ, q0 + WIN_QB + WINDOW)
        q_rot = jnp.concatenate([qrot_ref[rows, j * LANES:(j + 1) * LANES] for j in range(tiles)], axis=0)
        q_raw = jnp.concatenate([qraw_ref[rows, j * LANES:(j + 1) * LANES] for j in range(tiles)], axis=0)
        halves = []
        for p in range(HEADS_PER_TILE):
            s_loc = _dot_nt(jnp.where(in_half[p], q_rot, 0.0), k[p][lo:hi]) * scale
            qpos = q0 + (lax.broadcasted_iota(jnp.int32, s_loc.shape, 0) & (WIN_QB - 1))
            kpos = lo + lax.broadcasted_iota(jnp.int32, s_loc.shape, 1)
            s_loc = jnp.where(jnp.abs(kpos - qpos) <= WINDOW, s_loc, MASK_NEG)
            s_ctx = _dot_nt(jnp.where(in_half[p], q_raw, 0.0), kc[p]) * scale
            m = jnp.maximum(jnp.maximum(jnp.max(s_loc, axis=-1, keepdims=True),
                                        jnp.max(s_ctx, axis=-1, keepdims=True)), sinks[p])
            p_loc = jnp.exp(s_loc - m)
            p_ctx = jnp.exp(s_ctx - m)
            den = (jnp.sum(p_loc, axis=-1, keepdims=True) + jnp.sum(p_ctx, axis=-1, keepdims=True)
                   + jnp.exp(sinks[p] - m))
            halves.append((_dot(p_ctx, vc[p]) + _dot(p_loc, v[p][lo:hi])) / den)
        o = jnp.where(in_half[0], halves[0], halves[1])
        for j in range(tiles):
            o_ref[rows, j * LANES:(j + 1) * LANES] = o[j * WIN_QB:(j + 1) * WIN_QB]


def _win_attn(qkv, q_rot, k_rot, kc, vc, sink):
    row0 = T_CTX // DEC_SEQ
    gw = A_GROUP * HEAD_DIM
    return pl.pallas_call(
        _win_attn_body,
        grid=(DEC_BATCH, A_KV_HEADS),
        in_specs=[pl.BlockSpec((DEC_SEQ, gw), lambda b, h: (row0 + b, h)),
                  pl.BlockSpec((DEC_SEQ, gw), lambda b, h: (b, h)),
                  pl.BlockSpec((DEC_SEQ, A_KV), lambda b, h: (b, 0)),
                  pl.BlockSpec((DEC_SEQ, A_KV), lambda b, h: (row0 + b, (A_Q + A_KV) // A_KV)),
                  pl.BlockSpec((1, PAST_LEN, A_KV), lambda b, h: (b, 0, 0)),
                  pl.BlockSpec((1, PAST_LEN, A_KV), lambda b, h: (b, 0, 0)),
                  pl.BlockSpec((1, A_HEADS), lambda b, h: (0, 0))],
        out_specs=pl.BlockSpec((DEC_SEQ, gw), lambda b, h: (b, h)),
        out_shape=jax.ShapeDtypeStruct((T_LAT, A_Q), F32),
        compiler_params=_cparams("parallel", "parallel"),
        name="win_attn",
    )(qkv, q_rot, k_rot, qkv, kc, vc, sink.reshape(1, A_HEADS))


GRID_ROWS = DEC_SEQ // GRID_W
NA_BAND = min(NA_ROWS, GRID_ROWS)


NA_REL_ROWS = 2 * NA_ROWS - 1
NA_REL_COLS = 2 * NA_COLS - 1
LANES = 128
HEADS_PER_TILE = LANES // HEAD_DIM


def _na_rel_rows(rpb):
    pad = jnp.zeros((B_HEADS, NA_REL_ROWS, GRID_W - NA_REL_COLS), F32)
    one = jnp.concatenate([rpb, pad], axis=-1)
    nxt = jnp.concatenate([one[:, 1:], jnp.zeros((B_HEADS, 1, GRID_W), F32)], axis=1)
    both = jnp.concatenate([one, nxt], axis=-1)
    return jnp.concatenate([both, jnp.zeros((B_HEADS, 16 - NA_REL_ROWS, LANES), F32)], axis=1)


NA_HEADS_PER_STEP = LANES // HEAD_DIM


def _na_row_groups():
    groups = []
    for r in range(GRID_ROWS):
        rs = min(max(r - NA_ROWS // 2, 0), GRID_ROWS - NA_BAND)
        if groups and groups[-1][2] == rs:
            groups[-1][1] += 1
        else:
            groups.append([r, 1, rs])
    return groups


def _na_attn_body(q_ref, k_ref, v_ref, kc_ref, vc_ref, rel_ref, o_ref):
    scale = HEAD_DIM ** -0.5
    cq = lax.broadcasted_iota(jnp.int32, (GRID_W, LANES), 0)
    kcol = lax.broadcasted_iota(jnp.int32, (GRID_W, LANES), 1) & (GRID_W - 1)
    cs = jnp.clip(cq - NA_COLS // 2, 0, GRID_W - NA_COLS)
    col_ok = (kcol >= cs) & (kcol < cs + NA_COLS)
    kc = kc_ref[0]
    vc = vc_ref[0]
    tiles = {}

    def pair_tile(hh, a):
        if (hh, a) not in tiles:
            x = jnp.broadcast_to(rel_ref[hh, a:a + 1, :], (GRID_W, LANES))
            t = pltpu.roll(x, LANES - (NA_COLS - 1), axis=1, stride=1, stride_axis=0)
            tiles[hh, a] = jnp.where(col_ok, t, MASK_NEG)
        return tiles[hh, a]

    for r0, n_r, rs in _na_row_groups():
        rows = slice(r0 * GRID_W, (r0 + n_r) * GRID_W)
        band = slice(rs * GRID_W, (rs + NA_BAND) * GRID_W)
        q_t, k_t, v_t = q_ref[rows, :], k_ref[band, :], v_ref[band, :]
        head_of_lane = lax.broadcasted_iota(jnp.int32, q_t.shape, 1) >> (HEAD_DIM.bit_length() - 1)
        o = jnp.zeros(q_t.shape, F32)
        for hh in range(NA_HEADS_PER_STEP):
            bias = jnp.concatenate(
                [jnp.concatenate([pair_tile(hh, rs - r + NA_ROWS - 1 + 2 * i) for i in range(NA_BAND // 2)], axis=1)
                 for r in range(r0, r0 + n_r)], axis=0)
            q = jnp.where(head_of_lane == hh, q_t, 0.0)
            s_loc = _dot_nt(q, k_t) * scale + bias
            s_ctx = _dot_nt(q, kc) * scale
            m = jnp.maximum(jnp.max(s_loc, axis=-1, keepdims=True), jnp.max(s_ctx, axis=-1, keepdims=True))
            p_loc = jnp.exp(s_loc - m)
            p_ctx = jnp.exp(s_ctx - m)
            den = jnp.sum(p_loc, axis=-1, keepdims=True) + jnp.sum(p_ctx, axis=-1, keepdims=True)
            o = jnp.where(head_of_lane == hh, (_dot(p_ctx, vc) + _dot(p_loc, v_t)) / den, o)
        o_ref[rows, :] = o


def _na_attn(qkv, kc, vc, rel):
    row0 = T_CTX // DEC_SEQ
    col0 = (A_Q + 2 * A_KV) // LANES
    n_blk = B_W // LANES
    col = lambda j: pl.BlockSpec((DEC_SEQ, LANES), lambda b, p: (row0 + b, col0 + j * n_blk + p))
    cache = pl.BlockSpec((1, PAST_LEN, LANES), lambda b, p: (b, 0, p))
    return pl.pallas_call(
        _na_attn_body,
        grid=(DEC_BATCH, n_blk),
        in_specs=[col(0), col(1), col(2), cache, cache,
                  pl.BlockSpec((NA_HEADS_PER_STEP, 16, LANES), lambda b, p: (p, 0, 0))],
        out_specs=pl.BlockSpec((DEC_SEQ, LANES), lambda b, p: (b, p)),
        out_shape=jax.ShapeDtypeStruct((T_LAT, B_W), F32),
        compiler_params=_cparams("parallel", "parallel"),
        name="na_attn",
    )(qkv, qkv, qkv, kc, vc, rel)


@functools.lru_cache(maxsize=None)
def _dft_mats(L):
    n = 2 * L
    fc = min(L, DFT_CHUNK)
    f = np.arange(L)[:, None]
    t = np.arange(L)[None, :]
    ang = 2.0 * np.pi * ((f * t) % n) / n
    m1 = np.cos(ang)
    m2 = np.sin(ang)
    m2[0, :] = np.where(np.arange(L) % 2 == 0, 1.0, -1.0)
    wgt = np.full((L, 1), 2.0)
    wgt[0, 0] = 1.0
    nch = L // fc
    fwd = np.concatenate([m1.reshape(nch, fc, L), m2.reshape(nch, fc, L)], axis=1)
    inv = np.concatenate([(m1 * wgt / n).reshape(nch, fc, L), (m2 * wgt / n).reshape(nch, fc, L)], axis=1)
    inv = np.transpose(inv, (0, 2, 1))
    return fwd.astype(np.float32), inv.astype(np.float32)


@functools.lru_cache(maxsize=None)
def _filter_consts(L):
    t = np.linspace(0.0, 1.0, L)[:, None]
    bands = (C_EMB - 1) // 2
    ang = (2.0 * math.pi / L) * np.arange(L)[:, None] * np.linspace(1e-4, bands - 1, bands)[None, :]
    z = np.concatenate([t, np.cos(ang), -np.sin(ang)], axis=-1)
    zpad = np.zeros((L, 128))
    zpad[:, :C_EMB] = z
    deltas = np.abs(np.linspace(HYENA_MIN_DECAY, HYENA_MAX_DECAY, C_DIM))
    window = np.exp(-t * deltas[None, :])
    return zpad.astype(np.float32), window.astype(np.float32)


def _filter_body(z_ref, w1_ref, b1_ref, w2_ref, b2_ref, w3_ref, b3_ref, fr_ref, w4_ref, win_ref, fm_ref,
                 hr_ref, g_ref, hq_ref, hs_scr, hd_scr):
    c = pl.program_id(0)
    fc = hr_ref.shape[0]

    @pl.when(c == 0)
    def _():
        fr = fr_ref[...]
        hh = jnp.sin(fr * (_dot_hi(z_ref[...], w1_ref[...]) + b1_ref[...]))
        hh = jnp.sin(fr * (_dot_hi(hh, w2_ref[...]) + b2_ref[...]))
        hh = jnp.sin(fr * (_dot_hi(hh, w3_ref[...]) + b3_ref[...]))
        hh = _dot_hi(hh, w4_ref[...])
        hf = hh[:, :C_DIM] * win_ref[...]
        hb = hh[:, C_DIM:] * win_ref[...]
        hs_scr[...] = hf + hb
        hd_scr[...] = hf - hb

    fm = fm_ref[0]
    hr = _dot_split(fm[:fc], hs_scr[...])
    first = (lax.broadcasted_iota(jnp.int32, (fc, C_DIM), 0) == 0) & (c == 0)
    hr_ref[...] = hr
    g_ref[...] = jnp.where(first, 0.0, _dot_split(fm[fc:], hd_scr[...]))
    hs = hs_scr[...]
    sign = jnp.where((lax.broadcasted_iota(jnp.int32, hs.shape, 0) & 1) == 0, 1.0, -1.0)
    hq_ref[...] = jnp.where(first, jnp.sum(hs * sign, axis=0, keepdims=True), hr)


def _hyena_filter(L, filt):
    w1, b1, w2, b2, w3, b3, freq, w4 = filt
    zpad, window = _filter_consts(L)
    fwd, _ = _dft_mats(L)
    nch, fc2, _ = fwd.shape
    fc = fc2 // 2
    w1p = jnp.pad(w1, ((0, 128 - C_EMB), (0, 0)))
    full = lambda shape: pl.BlockSpec(shape, lambda c: tuple(0 for _ in shape))
    out_spec = pl.BlockSpec((fc, C_DIM), lambda c: (c, 0))
    out_sd = jax.ShapeDtypeStruct((L, C_DIM), F32)
    return pl.pallas_call(
        _filter_body,
        grid=(nch,),
        in_specs=[full((L, 128)), full((128, C_FFN)), full((1, C_FFN)), full((C_FFN, C_FFN)), full((1, C_FFN)),
                  full((C_FFN, C_FFN)), full((1, C_FFN)), full((1, C_FFN)), full((C_FFN, 2 * C_DIM)),
                  full((L, C_DIM)), pl.BlockSpec((1, fc2, L), lambda c: (c, 0, 0))],
        out_specs=[out_spec, out_spec, out_spec],
        out_shape=[out_sd, out_sd, out_sd],
        scratch_shapes=[pltpu.VMEM((L, C_DIM), F32), pltpu.VMEM((L, C_DIM), F32)],
        compiler_params=_cparams("arbitrary"),
        name="hyena_filter",
    )(jnp.asarray(zpad), w1p, b1.reshape(1, C_FFN), w2, b2.reshape(1, C_FFN), w3, b3.reshape(1, C_FFN),
      freq.reshape(1, C_FFN), w4, jnp.asarray(window), jnp.asarray(fwd))


def _hyena_body(u_ref, cw_ref, cb_ref, d_ref, fm_ref, fi_ref, hr_ref, g_ref, hq_ref, y_ref,
                x0_scr, z_scr, acc_scr):
    c = pl.program_id(1)
    L = y_ref.shape[0]
    fc = hr_ref.shape[0]

    @pl.when(c == 0)
    def _():
        row = lax.broadcasted_iota(jnp.int32, (L, C_DIM), 0)

        def short_conv(sec):
            cols = slice(sec * C_DIM, (sec + 1) * C_DIM)
            u = u_ref[:, cols]
            prev = jnp.where(row == 0, 0.0, pltpu.roll(u, 1, axis=0))
            nxt = jnp.where(row == L - 1, 0.0, pltpu.roll(u, L - 1, axis=0))
            return (prev * cw_ref[0:1, cols] + u * cw_ref[1:2, cols] + nxt * cw_ref[2:3, cols]
                    + cb_ref[:, cols])

        x0_scr[...] = short_conv(0)
        z_scr[...] = short_conv(1) * short_conv(2)
        acc_scr[...] = jnp.zeros((L, C_DIM), F32)

    ab = _dot(fm_ref[0], z_scr[...])
    a, b = ab[:fc], ab[fc:]
    hr, g, hq = hr_ref[...], g_ref[...], hq_ref[...]
    pq = jnp.concatenate([a * hr - b * g, a * g + b * hq], axis=0)
    acc_scr[...] += _dot(fi_ref[0], pq)

    @pl.when(c == pl.num_programs(1) - 1)
    def _():
        y_ref[...] = x0_scr[...] * (acc_scr[...] + z_scr[...] * d_ref[...])


def _hyena(u, row_blk0, n_seq, L, conv_w, conv_b, d_skip, spec):
    hr, g, hq = spec
    fwd, inv = _dft_mats(L)
    nch, fc2, _ = fwd.shape
    fc = fc2 // 2
    u_w = 3 * C_DIM
    return pl.pallas_call(
        _hyena_body,
        grid=(n_seq, nch),
        in_specs=[pl.BlockSpec((L, u_w), lambda b, c: (row_blk0 + b, 0)),
                  pl.BlockSpec((3, u_w), lambda b, c: (0, 0)),
                  pl.BlockSpec((1, u_w), lambda b, c: (0, 0)),
                  pl.BlockSpec((1, C_DIM), lambda b, c: (0, 0)),
                  pl.BlockSpec((1, fc2, L), lambda b, c: (c, 0, 0)),
                  pl.BlockSpec((1, L, fc2), lambda b, c: (c, 0, 0)),
                  pl.BlockSpec((fc, C_DIM), lambda b, c: (c, 0)),
                  pl.BlockSpec((fc, C_DIM), lambda b, c: (c, 0)),
                  pl.BlockSpec((fc, C_DIM), lambda b, c: (c, 0))],
        out_specs=pl.BlockSpec((L, C_DIM), lambda b, c: (b, 0)),
        out_shape=jax.ShapeDtypeStruct((n_seq * L, C_DIM), F32),
        scratch_shapes=[pltpu.VMEM((L, C_DIM), F32)] * 3,
        compiler_params=_cparams("parallel", "arbitrary"),
        name="hyena",
    )(u, conv_w, conv_b.reshape(1, u_w), d_skip.reshape(1, C_DIM), jnp.asarray(fwd), jnp.asarray(inv), hr, g, hq)


def _hgrn_body(q_ref, ff_ref, fb_ref, i_ref, g_ref, lbf_ref, lbb_ref, nd_ref, s0f_ref, s0b_ref,
               o_ref, sf_ref, sb_ref, *, layer):
    L = o_ref.shape[0]
    C = GLA_CHUNK
    S = min(L // GLA_MIN_SPANS, GLA_SPAN)
    nc = S // C
    n_span = L // S
    mid = C // 2
    def lower_bound(gm):
        e = jnp.exp(gm - jnp.max(gm, axis=0, keepdims=True))
        p = e / jnp.sum(e, axis=0, keepdims=True)
        return jnp.sum(p[0:layer + 1], axis=0, keepdims=True) - p[0:1]

    def gates(fx, lb):
        f = lb + (1.0 - lb) * jax.nn.sigmoid(fx)
        return 1.0 - f, jnp.log(f)


    chunk_shift = C.bit_length() - 1
    block_shift = D_KDIM.bit_length() - 1
    ti = lax.broadcasted_iota(jnp.int32, (S, S), 0)
    si = lax.broadcasted_iota(jnp.int32, (S, S), 1)
    same_chunk = (ti >> chunk_shift) == (si >> chunk_shift)
    causal = same_chunk & (si <= ti)
    anti = same_chunk & (si >= ti)
    row_chunk = lax.broadcasted_iota(jnp.int32, (S, nc * D_KDIM), 0) >> chunk_shift
    col_chunk = lax.broadcasted_iota(jnp.int32, (S, nc * D_KDIM), 1) >> block_shift
    own_block = row_chunk == col_chunk

    def spread(x):
        return jnp.where(own_block, jnp.concatenate([x] * nc, axis=1), 0.0)

    def chunk_cumsum(mask, lg):
        tri = mask.astype(BF16)
        hi = lg.astype(BF16)
        r1 = lg - hi.astype(F32)
        mid_t = r1.astype(BF16)
        lo = (r1 - mid_t.astype(F32)).astype(BF16)
        dot = lambda t: jnp.dot(tri, t, preferred_element_type=F32)
        return dot(hi) + dot(mid_t) + dot(lo)

    def per_chunk_rows(b, pos):
        return jnp.concatenate([jnp.broadcast_to(b[n * C + pos:n * C + pos + 1], (C, D_KDIM)) for n in range(nc)],
                               axis=0)

    def one_head(q, v, kf, lgf, kb, lgb, st_f, st_b):
        local = []
        for u in range(n_span):
            rows = slice(u * S, (u + 1) * S)
            qs, vs, kfs, kbs = q[rows], v[rows], kf[rows], kb[rows]
            lgs = jnp.concatenate([lgf[rows], lgb[rows]], axis=1)
            pre = chunk_cumsum(causal, lgs)
            b_f = pre[:, :D_KDIM]
            pre_b = pre[:, D_KDIM:]
            b_b = per_chunk_rows(pre_b, C - 1) - pre_b + lgb[rows]
            ref_f, ref_b = per_chunk_rows(b_f, mid), per_chunk_rows(b_b, mid)
            sc = (jnp.where(causal, _dot_nt(qs * jnp.exp(b_f - ref_f), kfs * jnp.exp(ref_f - b_f)), 0.0)
                  + jnp.where(anti, _dot_nt(qs * jnp.exp(b_b - ref_b), kbs * jnp.exp(ref_b - b_b)), 0.0))
            k_out = jnp.concatenate([kfs * jnp.exp(per_chunk_rows(b_f, C - 1) - b_f),
                                     kbs * jnp.exp(per_chunk_rows(b_b, 0) - b_b)], axis=1)
            kv_t = _dot_tn(spread(vs), k_out)
            local.append((_dot(sc, vs), kv_t, b_f, b_b, qs))

        states_f = [[None] * nc for _ in range(n_span)]
        for u in range(n_span):
            _, kv_t, b_f, _, _ = local[u]
            for n in range(nc):
                states_f[u][n] = st_f
                st_f = st_f * jnp.exp(b_f[n * C + C - 1:n * C + C]) + kv_t[n * D_VDIM:(n + 1) * D_VDIM, :D_KDIM]
        states_b = [[None] * nc for _ in range(n_span)]
        for u in reversed(range(n_span)):
            _, kv_t, _, b_b, _ = local[u]
            for n in reversed(range(nc)):
                states_b[u][n] = st_b
                st_b = st_b * jnp.exp(b_b[n * C:n * C + 1]) + kv_t[n * D_VDIM:(n + 1) * D_VDIM, D_KDIM:]

        outs = []
        for u in range(n_span):
            intra, _, b_f, b_b, qs = local[u]
            q_in = jnp.concatenate([spread(qs * jnp.exp(b_f)), spread(qs * jnp.exp(b_b))], axis=1)
            outs.append(intra + _dot_nt(q_in, jnp.concatenate(states_f[u] + states_b[u], axis=1)))
        return (jnp.concatenate(outs, axis=0) if n_span > 1 else outs[0]), st_f, st_b

    for hh in range(o_ref.shape[1] // D_VDIM):
        cols = slice(hh * D_KDIM, (hh + 1) * D_KDIM)
        kf, lgf = gates(ff_ref[:, cols], lower_bound(lbf_ref[:, cols]))
        kb, lgb = gates(fb_ref[:, cols], lower_bound(lbb_ref[:, cols]))
        o, st_f, st_b = one_head(_silu(q_ref[:, cols]), i_ref[:, cols], kf, lgf, kb, lgb,
                                 jnp.transpose(s0f_ref[0, hh]), jnp.transpose(s0b_ref[0, hh]))
        sf_ref[0, hh] = jnp.transpose(st_f)
        sb_ref[0, hh] = jnp.transpose(st_b)
        o_ref[:, cols] = _rms(o, nd_ref[...]) * _silu(g_ref[:, cols])


def _hgrn(u, row_blk0, n_seq, L, lb_fwd, lb_bwd, norm_d, s0f, s0b, layer):
    hps = HGRN_HEADS_PER_STEP
    width = hps * D_KDIM
    col0 = 3 * C_DIM // width
    groups = D_HEADS // hps
    col = lambda j: pl.BlockSpec((L, width), lambda b, h: (row_blk0 + b, col0 + j * groups + h))
    lbs = pl.BlockSpec((DEPTH, width), lambda b, h: (0, h))
    st = pl.BlockSpec((1, hps, D_KDIM, D_VDIM), lambda b, h: (b, h, 0, 0))
    st_sd = jax.ShapeDtypeStruct((n_seq, D_HEADS, D_KDIM, D_VDIM), F32)
    return pl.pallas_call(
        functools.partial(_hgrn_body, layer=layer),
        grid=(n_seq, groups),
        in_specs=[col(0), col(1), col(2), col(3), col(4), lbs, lbs,
                  pl.BlockSpec((1, D_VDIM), lambda b, h: (0, 0)), st, st],
        out_specs=[pl.BlockSpec((L, width), lambda b, h: (b, h)), st, st],
        out_shape=[jax.ShapeDtypeStruct((n_seq * L, D_HEADS * D_VDIM), F32), st_sd, st_sd],
        compiler_params=_cparams("parallel", "parallel"),
        name="hgrn",
    )(u, u, u, u, u, lb_fwd, lb_bwd, norm_d.reshape(1, D_VDIM), s0f, s0b)


def _pack_bf16_pairs(h):
    n = h.shape[1] // 2
    hi = lax.bitcast_convert_type(h[:, :n].astype(BF16).astype(F32), jnp.int32)
    lo = lax.bitcast_convert_type(h[:, n:].astype(BF16).astype(F32), jnp.int32)
    return hi | lax.shift_right_logical(lo, 16)


def _unpack_bf16_pairs(p):
    hi = lax.bitcast_convert_type(p & jnp.int32(-65536), F32).astype(BF16)
    lo = lax.bitcast_convert_type(lax.shift_left(p, 16), F32).astype(BF16)
    return hi, lo


def _outproj_body(*refs, n_x):
    a_refs, b_refs, x_refs = refs[0:2], refs[2:4], refs[4:4 + n_x]
    mod_ref, gf_ref, w_ref, wrh_ref, wrl_ref, rb_ref, x1_ref, h2_ref, chosen_ref, gk_ref, ik_ref = refs[4 + n_x:]
    m = mod_ref[0]
    half = a_refs[0].shape[1]
    out = _dot(_token_tile(a_refs), w_ref[0:half, :]) + _dot(_token_tile(b_refs), w_ref[half:, :])
    x1 = _token_tile(x_refs) + m[:, 2 * D_MODEL:3 * D_MODEL] * out
    x1_ref[...] = x1
    h2 = _rms(x1, gf_ref[...]) * (1.0 + m[:, 4 * D_MODEL:5 * D_MODEL]) + m[:, 3 * D_MODEL:4 * D_MODEL]
    h2_ref[...] = _pack_bf16_pairs(h2)
    h_hi = h2.astype(BF16)
    h_lo = (h2 - h_hi.astype(F32)).astype(BF16)
    logits = _dot_nt(wrh_ref[...], h_hi) + _dot_nt(wrh_ref[...], h_lo) + _dot_nt(wrl_ref[...], h_hi)
    scores = jax.nn.sigmoid(logits)
    work = scores + rb_ref[...]
    expert = lax.broadcasted_iota(jnp.int32, work.shape, 0).astype(F32)
    slot = lax.broadcasted_iota(jnp.int32, (TOP_K, work.shape[1]), 0)
    chosen = [jnp.zeros(work.shape, F32) for _ in range(MOE_REGIONS)]
    gk = jnp.zeros((TOP_K, work.shape[1]), F32)
    ik = jnp.zeros((TOP_K, work.shape[1]), F32)
    for k in range(TOP_K):
        best = jnp.max(work, axis=0, keepdims=True)
        first = jnp.min(jnp.where(work == best, expert, float(N_EXPERTS)), axis=0, keepdims=True)
        hit = expert == first
        chosen[k // K_PER_REGION] = jnp.where(hit, 1.0, chosen[k // K_PER_REGION])
        gk = jnp.where(slot == k, jnp.sum(jnp.where(hit, scores, 0.0), axis=0, keepdims=True), gk)
        ik = jnp.where(slot == k, first, ik)
        work = jnp.where(hit, -jnp.inf, work)
    for r in range(MOE_REGIONS):
        chosen_ref[r] = chosen[r]
    gk_ref[...] = jnp.transpose(gk / jnp.sum(gk, axis=0, keepdims=True) * ROUTE_SCALE)
    ik_ref[...] = ik


def _outproj(a, b, x, mod_l, gain_ffn, w_out, w_router, router_bias):
    half = a[0].shape[1]
    a_specs, a_args = _token_specs(a, half)
    b_specs, b_args = _token_specs(b, half)
    x_specs, x_args = _token_specs(x, D_MODEL)
    wr_t = w_router.T
    wr_hi = wr_t.astype(BF16)
    wr_lo = (wr_t - wr_hi.astype(F32)).astype(BF16)
    return pl.pallas_call(
        functools.partial(_outproj_body, n_x=len(x_args)),
        grid=(T_ALL // TM,),
        in_specs=a_specs + b_specs + x_specs + [
                  pl.BlockSpec((1, 1, N_MOD * D_MODEL), lambda i: (_mod_row(i), 0, 0)),
                  pl.BlockSpec((1, D_MODEL), lambda i: (0, 0)),
                  pl.BlockSpec((2 * half, D_MODEL), lambda i: (0, 0)),
                  pl.BlockSpec((N_EXPERTS, D_MODEL), lambda i: (0, 0)),
                  pl.BlockSpec((N_EXPERTS, D_MODEL), lambda i: (0, 0)),
                  pl.BlockSpec((N_EXPERTS, 1), lambda i: (0, 0))],
        out_specs=[pl.BlockSpec((TM, D_MODEL), lambda i: (i, 0)),
                   pl.BlockSpec((TM, D_MODEL // 2), lambda i: (i, 0)),
                   pl.BlockSpec((MOE_REGIONS, N_EXPERTS, TM), lambda i: (0, 0, i)),
                   pl.BlockSpec((TM, TOP_K), lambda i: (i, 0)),
                   pl.BlockSpec((TOP_K, TM), lambda i: (0, i))],
        out_shape=[jax.ShapeDtypeStruct((T_ALL, D_MODEL), F32),
                   jax.ShapeDtypeStruct((T_ALL, D_MODEL // 2), jnp.int32),
                   jax.ShapeDtypeStruct((MOE_REGIONS, N_EXPERTS, T_ALL), F32),
                   jax.ShapeDtypeStruct((T_ALL, TOP_K), F32),
                   jax.ShapeDtypeStruct((TOP_K, T_ALL), F32)],
        compiler_params=_cparams("parallel"),
        name="outproj_router",
    )(*a_args, *b_args, *x_args, mod_l, gain_ffn.reshape(1, D_MODEL), w_out, wr_hi, wr_lo,
      router_bias.reshape(N_EXPERTS, 1))


def _route_body(chosen_ref, ik_ref, dest_ref, first_ref, count_ref, short_ref, pos_scr):
    n_tiles = T_ALL // TM
    r = lax.broadcasted_iota(jnp.int32, (TM, TM), 0)
    c = lax.broadcasted_iota(jnp.int32, (TM, TM), 1)
    before = (r < c).astype(BF16)

    counts = jnp.zeros((N_EXPERTS, 1), F32)
    for i in range(n_tiles):
        cols = slice(i * TM, (i + 1) * TM)
        m = chosen_ref[0, :, cols]
        pos_scr[:, cols] = jnp.dot(m.astype(BF16), before, preferred_element_type=F32) + counts
        counts = counts + jnp.sum(m, axis=1, keepdims=True)
    padded = jnp.ceil(counts * (1.0 / MOE_BLK)) * MOE_BLK
    ei = lax.broadcasted_iota(jnp.int32, (N_EXPERTS, N_EXPERTS), 0)
    ej = lax.broadcasted_iota(jnp.int32, (N_EXPERTS, N_EXPERTS), 1)
    end = _dot_hi((ej <= ei).astype(F32), jnp.broadcast_to(padded, (N_EXPERTS, LANES)))[:, 0:1]
    start = end - padded

    expert = lax.broadcasted_iota(jnp.int32, (N_EXPERTS, TM), 0).astype(F32)
    slot = lax.broadcasted_iota(jnp.int32, (K_PER_REGION, TM), 0)
    for i in range(n_tiles):
        cols = slice(i * TM, (i + 1) * TM)
        row_of = pos_scr[:, cols] + start
        ik = ik_ref[0, :, cols]
        acc = jnp.zeros((K_PER_REGION, TM), F32)
        for k in range(K_PER_REGION):
            pick = jnp.sum(jnp.where(expert == ik[k:k + 1, :], row_of, 0.0), axis=0, keepdims=True)
            acc = jnp.where(slot == k, pick, acc)
        dest_ref[0, :, cols] = acc.astype(jnp.int32)
    first_ref[0] = jnp.broadcast_to(start * (1.0 / MOE_BLK), (N_EXPERTS, LANES)).astype(jnp.int32)
    count_ref[0] = jnp.broadcast_to(padded * (1.0 / MOE_BLK), (N_EXPERTS, LANES)).astype(jnp.int32)
    in_last = counts - (padded - MOE_BLK)
    short = jnp.where((counts > 0.0) & (in_last <= MOE_BLK // 2), 1.0, 0.0)
    short_ref[0] = jnp.broadcast_to(short, (N_EXPERTS, LANES)).astype(jnp.int32)


def _route(chosen, ik):
    per_region = lambda rows, cols: pl.BlockSpec((1, rows, cols), lambda r: (r, 0, 0))
    table = jax.ShapeDtypeStruct((MOE_REGIONS, N_EXPERTS, LANES), jnp.int32)
    return pl.pallas_call(
        _route_body,
        grid=(MOE_REGIONS,),
        in_specs=[per_region(N_EXPERTS, T_ALL), per_region(K_PER_REGION, T_ALL)],
        out_specs=[per_region(K_PER_REGION, T_ALL)] + [per_region(N_EXPERTS, LANES)] * 3,
        out_shape=[jax.ShapeDtypeStruct((MOE_REGIONS, K_PER_REGION, T_ALL), jnp.int32), table, table, table],
        scratch_shapes=[pltpu.VMEM((N_EXPERTS, T_ALL), F32)],
        compiler_params=_cparams("arbitrary"),
        name="moe_route",
    )(chosen, ik.reshape(MOE_REGIONS, K_PER_REGION, T_ALL))


def _sc_worker_id():
    return lax.axis_index("s") * SC_CORES + lax.axis_index("c")


def _sc_dispatch(h2p, dest):
    n_chunks = T_ALL // DISP_CHUNK
    k_per = dest.shape[0] // DISP_SPLIT
    items_per_worker = n_chunks * DISP_SPLIT // SC_WORKERS
    chunk_stride = SC_WORKERS // DISP_SPLIT
    width = h2p.shape[1]
    mesh = plsc.VectorSubcoreMesh(core_axis_name="c", subcore_axis_name="s")

    @functools.partial(
        pl.kernel, mesh=mesh,
        out_type=jax.ShapeDtypeStruct((MOE_ROWS, width), jnp.int32),
        scratch_types=[pltpu.VMEM((k_per, DISP_CHUNK), jnp.int32), pltpu.VMEM((DISP_CHUNK, width), jnp.int32),
                       pltpu.SemaphoreType.DMA],
    )
    def run(x_hbm, dest_hbm, xs_hbm, idx_v, rows_v, sem):
        wid = _sc_worker_id()
        group = wid % DISP_SPLIT
        for i in range(items_per_worker):
            chunk = i * chunk_stride + wid // DISP_SPLIT
            tokens = pl.ds(pl.multiple_of(chunk * DISP_CHUNK, DISP_CHUNK), DISP_CHUNK)
            pltpu.sync_copy(dest_hbm.at[group, :, tokens], idx_v)
            pltpu.sync_copy(x_hbm.at[tokens], rows_v)
            scatters = [pltpu.make_async_copy(rows_v, xs_hbm.at[idx_v.at[k]], sem) for k in range(k_per)]
            for cp in scatters:
                cp.start()
            for cp in scatters:
                cp.wait()

    return run(h2p, dest.reshape(DISP_SPLIT, k_per, T_ALL))


def _sc_collect(y, dest_flat):
    n_k = dest_flat.shape[0] // T_ALL
    per_worker = T_ALL // SC_WORKERS
    n_chunks = per_worker // COLLECT_CHUNK
    n_steps = n_k * n_chunks
    width = y.shape[1]
    mesh = plsc.VectorSubcoreMesh(core_axis_name="c", subcore_axis_name="s")

    @functools.partial(
        pl.kernel, mesh=mesh,
        out_type=jax.ShapeDtypeStruct((n_k * T_ALL, width), y.dtype),
        scratch_types=[pltpu.VMEM((n_k * per_worker,), jnp.int32),
                       pltpu.VMEM((COLLECT_CHUNK, width), y.dtype), pltpu.VMEM((COLLECT_CHUNK, width), y.dtype),
                       pltpu.SemaphoreType.DMA, pltpu.SemaphoreType.DMA],
    )
    def run(y_hbm, dest_hbm, yg_hbm, idx_v, rows0, rows1, sem0, sem1):
        wid = _sc_worker_id()
        bufs = ((rows0, sem0), (rows1, sem1))
        for k in range(n_k):
            pltpu.sync_copy(dest_hbm.at[pl.ds(k * T_ALL + wid * per_worker, per_worker)],
                            idx_v.at[pl.ds(k * per_worker, per_worker)])

        def gather(step, buf):
            rows, sem = buf
            idx = idx_v.at[pl.ds(pl.multiple_of(step * COLLECT_CHUNK, 8), COLLECT_CHUNK)]
            return pltpu.make_async_copy(y_hbm.at[idx], rows, sem)

        def out_rows(step):
            off = (step // n_chunks) * T_ALL + wid * per_worker + (step % n_chunks) * COLLECT_CHUNK
            return yg_hbm.at[pl.ds(pl.multiple_of(off, 8), COLLECT_CHUNK)]

        gather(0, bufs[0]).start()

        @pl.loop(0, n_steps, step=2)
        def _(base):
            for j in range(2):
                step = base + j

                @pl.when(step + 1 < n_steps)
                def _():
                    gather(step + 1, bufs[1 - j]).start()

                gather(step, bufs[j]).wait()
                pltpu.sync_copy(bufs[j][0], out_rows(step))

    return run(y, dest_flat)


def _expert_body(first_ref, count_ref, short_ref, xs_hbm, wg_hbm, wu_hbm, wd_hbm, y_hbm,
                 wg_f32, wu_f32, wd_f32, wg_bf, wu_bf, wd_bf, x_buf, y_buf, w_sem, in_sem, out_sem, *, layer):
    e = pl.program_id(0)
    first = first_ref[e]
    count = count_ref[e]
    n_used = first_ref[N_EXPERTS - 1] + count_ref[N_EXPERTS - 1]
    half = D_MODEL // 2

    def weight_copies(ex):
        slot = lax.rem(ex, EXPERT_W_SLOTS)
        out = []
        for src, dst in ((wg_hbm, wg_f32), (wu_hbm, wu_f32), (wd_hbm, wd_f32)):
            size = dst.shape[1] // EXPERT_W_PARTS
            for part in range(EXPERT_W_PARTS):
                rows = pl.ds(part * size, size)
                out.append(pltpu.make_async_copy(src.at[layer, ex, rows], dst.at[slot, rows], w_sem.at[slot]))
        return out

    @pl.when(e == 0)
    def _():
        for ahead in range(EXPERT_W_SLOTS - 1):
            for cp in weight_copies(ahead):
                cp.start(priority=1)

    for cp in weight_copies(e):
        cp.wait()

    @pl.when(e + EXPERT_W_SLOTS - 1 < N_EXPERTS)
    def _():
        for cp in weight_copies(e + EXPERT_W_SLOTS - 1):
            cp.start(priority=1)

    w_slot = lax.rem(e, EXPERT_W_SLOTS)
    wg_bf[...] = wg_f32[w_slot].astype(BF16)
    wu_bf[...] = wu_f32[w_slot].astype(BF16)
    wd_bf[...] = wd_f32[w_slot].astype(BF16)

    def part_rows(g, part, n_parts):
        size = MOE_BLK // n_parts
        return pl.ds(pl.multiple_of(g * MOE_BLK + part * size, size), size), pl.ds(part * size, size)

    def in_copies(g):
        slot = g & (EXPERT_SLOTS - 1)
        out = []
        for part in range(EXPERT_IN_PARTS):
            src, dst = part_rows(g, part, EXPERT_IN_PARTS)
            out.append(pltpu.make_async_copy(xs_hbm.at[src], x_buf.at[slot, dst], in_sem.at[slot]))
        return out

    def out_copies(g):
        slot = g & (EXPERT_SLOTS - 1)
        out = []
        for part in range(EXPERT_OUT_PARTS):
            dst, src = part_rows(g, part, EXPERT_OUT_PARTS)
            out.append(pltpu.make_async_copy(y_buf.at[slot, src], y_hbm.at[dst], out_sem.at[slot]))
        return out

    @pl.when((first == 0) & (count > 0))
    def _():
        for ahead in range(EXPERT_SLOTS - 1):
            @pl.when(ahead < n_used)
            def _():
                for cp in in_copies(ahead):
                    cp.start()

    def block(b, carry):
        g = first + b
        slot = g & (EXPERT_SLOTS - 1)
        for cp in in_copies(g):
            cp.wait()

        @pl.when(g + EXPERT_SLOTS - 1 < n_used)
        def _():
            for cp in in_copies(g + EXPERT_SLOTS - 1):
                cp.start()

        @pl.when(g >= EXPERT_SLOTS)
        def _():
            for cp in out_copies(g - EXPERT_SLOTS):
                cp.wait()

        def ffn(n_rows):
            hi, lo = _unpack_bf16_pairs(x_buf[slot, 0:n_rows])

            def proj(w_bf):
                return (jnp.dot(hi, w_bf[0:half, :], preferred_element_type=F32)
                        + jnp.dot(lo, w_bf[half:, :], preferred_element_type=F32))

            hid = _silu(proj(wg_bf)) * proj(wu_bf)
            y_buf[slot, 0:n_rows] = _pack_bf16_pairs(
                jnp.dot(hid.astype(BF16), wd_bf[...], preferred_element_type=F32))

        short = (b == count - 1) & (short_ref[e] == 1)

        @pl.when(short)
        def _():
            ffn(MOE_BLK // 2)
            y_buf[slot, MOE_BLK // 2:MOE_BLK] = jnp.zeros((MOE_BLK // 2, D_MODEL // 2), jnp.int32)

        @pl.when(jnp.logical_not(short))
        def _():
            ffn(MOE_BLK)

        for cp in out_copies(g):
            cp.start()
        return carry

    lax.fori_loop(0, count, block, 0)

    @pl.when(e == N_EXPERTS - 1)
    def _():
        for back in range(EXPERT_SLOTS, 0, -1):
            @pl.when(n_used >= back)
            def _():
                for cp in out_copies(n_used - back):
                    cp.wait()


EXPERT_SLOTS = 4
EXPERT_W_SLOTS = 4
EXPERT_W_PARTS = 2
EXPERT_IN_PARTS = 2
EXPERT_OUT_PARTS = 4


def _experts(first_blk, n_blk, short_last, xs, layer, w_gate, w_up, w_down):
    anywhere = pl.BlockSpec(memory_space=pl.ANY)
    grid_spec = pltpu.PrefetchScalarGridSpec(
        num_scalar_prefetch=3,
        grid=(N_EXPERTS,),
        in_specs=[anywhere] * 4,
        out_specs=anywhere,
        scratch_shapes=[pltpu.VMEM((EXPERT_W_SLOTS, D_MODEL, D_EXPERT), F32),
                        pltpu.VMEM((EXPERT_W_SLOTS, D_MODEL, D_EXPERT), F32),
                        pltpu.VMEM((EXPERT_W_SLOTS, D_EXPERT, D_MODEL), F32),
                        pltpu.VMEM((D_MODEL, D_EXPERT), BF16), pltpu.VMEM((D_MODEL, D_EXPERT), BF16),
                        pltpu.VMEM((D_EXPERT, D_MODEL), BF16),
                        pltpu.VMEM((EXPERT_SLOTS, MOE_BLK, D_MODEL // 2), jnp.int32),
                        pltpu.VMEM((EXPERT_SLOTS, MOE_BLK, D_MODEL // 2), jnp.int32),
                        pltpu.SemaphoreType.DMA((EXPERT_W_SLOTS,)),
                        pltpu.SemaphoreType.DMA((EXPERT_SLOTS,)), pltpu.SemaphoreType.DMA((EXPERT_SLOTS,))],
    )
    return pl.pallas_call(
        functools.partial(_expert_body, layer=layer),
        grid_spec=grid_spec,
        out_shape=jax.ShapeDtypeStruct((MOE_ROWS, D_MODEL // 2), jnp.int32),
        compiler_params=_cparams("arbitrary"),
        name="moe_experts",
    )(first_blk, n_blk, short_last, xs, w_gate, w_up, w_down)


def _combine_body(x1_ref, h2_ref, *refs, final):
    yg_refs = refs[:MOE_REGIONS]
    gk_ref, mod_ref, sg_ref, su_ref, sd_ref, fn_ref, *o_refs = refs[MOE_REGIONS:]
    hi, lo = _unpack_bf16_pairs(h2_ref[...])
    half = D_MODEL // 2

    def proj(w_ref):
        return _dot(hi, w_ref[0:half, :]) + _dot(lo, w_ref[half:, :])

    shared = _dot(_silu(proj(sg_ref)) * proj(su_ref), sd_ref[...])
    acc_hi, acc_lo = shared[:, :half], shared[:, half:]
    gk = gk_ref[...]
    for k in range(TOP_K):
        y_hi, y_lo = _unpack_bf16_pairs(yg_refs[k // K_PER_REGION][k % K_PER_REGION])
        acc_hi = acc_hi + gk[:, k:k + 1] * y_hi.astype(F32)
        acc_lo = acc_lo + gk[:, k:k + 1] * y_lo.astype(F32)
    acc = jnp.concatenate([acc_hi, acc_lo], axis=1)
    m = mod_ref[0]
    y = x1_ref[...] + m[:, 5 * D_MODEL:6 * D_MODEL] * acc
    if not final:
        o_refs[0][...] = y
        return
    y = _rms(y, fn_ref[...])
    is_ctx = pl.program_id(0) < N_CTX_TILES

    @pl.when(is_ctx)
    def _():
        o_refs[0][...] = y

    @pl.when(jnp.logical_not(is_ctx))
    def _():
        o_refs[1][...] = y


def _combine(x1, h2p, yg, gk, mod_l, ws_gate, ws_up, ws_down, final_norm, final):
    tok = lambda shape: pl.BlockSpec(shape, lambda i: (i, 0))
    full = lambda shape: pl.BlockSpec(shape, lambda i: (0, 0))
    if final:
        out_specs, _ = _token_specs((None, None), D_MODEL)
        out_shape = [jax.ShapeDtypeStruct((T_CTX, D_MODEL), F32), jax.ShapeDtypeStruct((T_LAT, D_MODEL), F32)]
    else:
        out_specs = tok((TM, D_MODEL))
        out_shape = jax.ShapeDtypeStruct((T_ALL, D_MODEL), F32)
    return pl.pallas_call(
        functools.partial(_combine_body, final=final),
        grid=(T_ALL // TM,),
        in_specs=[tok((TM, D_MODEL)), tok((TM, D_MODEL // 2))]
                 + [pl.BlockSpec((K_PER_REGION, TM, D_MODEL // 2), lambda i: (0, i, 0))] * MOE_REGIONS
                 + [tok((TM, TOP_K)),
                  pl.BlockSpec((1, 1, N_MOD * D_MODEL), lambda i: (_mod_row(i), 0, 0)),
                  full((D_MODEL, D_EXPERT)), full((D_MODEL, D_EXPERT)), full((D_EXPERT, D_MODEL)),
                  full((1, D_MODEL))],
        out_specs=out_specs,
        out_shape=out_shape,
        compiler_params=_cparams("arbitrary"),
        name="moe_combine",
    )(x1, h2p, *yg, gk, mod_l, ws_gate, ws_up, ws_down, final_norm.reshape(1, D_MODEL))


def _moe(x1, h2p, chosen, gk, ik, mod_l, layer, w_gate, w_up, w_down, ws_gate, ws_up, ws_down, final_norm, final):
    dest, first_blk, n_blk, short_last = _route(chosen, ik)
    yg = []
    for r in range(MOE_REGIONS):
        xs = _sc_dispatch(h2p, dest[r])
        y = _experts(first_blk[r, :, 0], n_blk[r, :, 0], short_last[r, :, 0], xs, layer, w_gate, w_up, w_down)
        yg.append(_sc_collect(y, dest[r].reshape(-1)).reshape(K_PER_REGION, T_ALL, D_MODEL // 2))
    return _combine(x1, h2p, yg, gk, mod_l, ws_gate, ws_up, ws_down, final_norm, final)


def kernel(x_prompt, x_sample, cache_a_k, cache_a_v, cache_b_k, cache_b_v, state_d_fwd, state_d_bwd, c, c_ctx, w_ada, b_ada, norm_mix, norm_ffn, w_in_attn, w_out_attn, sink_a, rpb_b, w_in_rec, w_out_rec, conv_w, conv_b, filt_w1, filt_b1, filt_w2, filt_b2, filt_w3, filt_b3, filt_freq, filt_w4, d_skip, lb_fwd, lb_bwd, norm_d, w_router, router_bias, w_gate, w_up, w_down, ws_gate, ws_up, ws_down, final_norm):
    x = (x_prompt.reshape(T_CTX, D_MODEL), x_sample.reshape(T_LAT, D_MODEL))
    cvec = jnp.concatenate([c_ctx[None, :], c], axis=0)
    c_lanes = jnp.broadcast_to(cvec[:, :, None], (N_CVEC, D_MODEL, LANES))
    mod = [_ada(c_lanes, l, w_ada, b_ada).reshape(CVEC_PAD, 1, N_MOD * D_MODEL) for l in range(DEPTH)]

    new_kv = None
    new_state = None
    for l in range(DEPTH):
        j = l // 2
        final = l == DEPTH - 1
        if l % 2 == 0:
            qkv = _inproj(x, mod[l], norm_mix[l], w_in_attn[j])
            oa_ctx, ob_ctx, *new_kv = _ctx_attn(qkv, sink_a[j])
            new_kv = tuple(new_kv)
            q_rot, k_rot = _rope(qkv)
            cache = lambda t: t[:, j].reshape(DEC_BATCH, PAST_LEN, -1)
            oa_lat = _win_attn(qkv, q_rot, k_rot, cache(cache_a_k), cache(cache_a_v), sink_a[j])
            ob_lat = _na_attn(qkv, cache(cache_b_k), cache(cache_b_v), _na_rel_rows(rpb_b[j]))
            mix_a = (oa_ctx, oa_lat)
            mix_b = (ob_ctx, ob_lat)
            w_out = w_out_attn[j]
        else:
            u = _inproj(x, mod[l], norm_mix[l], w_in_rec[j])
            filt = (filt_w1[j], filt_b1[j], filt_w2[j], filt_b2[j], filt_w3[j], filt_b3[j], filt_freq[j],
                    filt_w4[j])
            y_ctx = _hyena(u, 0, BATCH, SEQ, conv_w[j], conv_b[j], d_skip[j], _hyena_filter(SEQ, filt))
            y_lat = _hyena(u, T_CTX // DEC_SEQ, DEC_BATCH, DEC_SEQ, conv_w[j], conv_b[j], d_skip[j],
                           _hyena_filter(DEC_SEQ, filt))
            zeros = jnp.zeros((BATCH, D_HEADS, D_KDIM, D_VDIM), F32)
            o_ctx, s_f, s_b = _hgrn(u, 0, BATCH, SEQ, lb_fwd, lb_bwd, norm_d[j], zeros, zeros, l)
            o_lat, _, _ = _hgrn(u, T_CTX // DEC_SEQ, DEC_BATCH, DEC_SEQ, lb_fwd, lb_bwd, norm_d[j],
                                state_d_fwd[:, j], state_d_bwd[:, j], l)
            new_state = (s_f[:, None], s_b[:, None])
            mix_a = (y_ctx, y_lat)
            mix_b = (o_ctx, o_lat)
            w_out = w_out_rec[j]
        x1, h2p, chosen, gk, ik = _outproj(mix_a, mix_b, x, mod[l], norm_ffn[l], w_out, w_router[l],
                                           router_bias[l])
        x = _moe(x1, h2p, chosen, gk, ik, mod[l], l, w_gate, w_up, w_down, ws_gate[l], ws_up[l],
                 ws_down[l], final_norm, final)

    y_prompt = x[0].reshape(BATCH, SEQ, D_MODEL)
    y_sample = x[1].reshape(DEC_BATCH, DEC_SEQ, D_MODEL)
    return (y_prompt, y_sample) + new_kv + new_state
```
